```python
import jax, jax.numpy as jnp
from jax import lax

D_MODEL = 1024
BATCH = 8
SEQ = 4096
DEPTH = 2

CHUNK = 64
N_BRANCH = 3
BRANCH_WIDTH = D_MODEL
ATTN_HEADS = 8
ATTN_HEAD_DIM = BRANCH_WIDTH // ATTN_HEADS
ATTN_LEFT_CHUNKS = 8
REL_CLIP = 2 * CHUNK
RET_HEADS = 4
RET_KEY_DIM = BRANCH_WIDTH // (2 * RET_HEADS)
RET_VAL_DIM = BRANCH_WIDTH // RET_HEADS
ROPE_BASE = 10000.0
GLA_HEADS = 4
GLA_KEY_DIM = BRANCH_WIDTH // (2 * GLA_HEADS)
GLA_VAL_DIM = BRANCH_WIDTH // GLA_HEADS
GLA_GATE_RANK = 16
GLA_GATE_NORMALIZER = 16.0
D_FF = 4 * D_MODEL
LN_EPS = 1e-5
NEG_INF = -1e30
DEEPNORM_ALPHA = (2 * DEPTH) ** 0.25
DEEPNORM_BETA = (8 * DEPTH) ** -0.25

SPLIT_SIZES = (
    ATTN_HEADS * ATTN_HEAD_DIM, ATTN_HEADS * ATTN_HEAD_DIM, ATTN_HEADS * ATTN_HEAD_DIM,
    RET_HEADS * RET_KEY_DIM, RET_HEADS * RET_KEY_DIM, RET_HEADS * RET_VAL_DIM, RET_HEADS * RET_VAL_DIM,
    GLA_HEADS * GLA_KEY_DIM, GLA_HEADS * GLA_KEY_DIM, GLA_HEADS * GLA_VAL_DIM, GLA_HEADS * GLA_VAL_DIM,
    GLA_GATE_RANK, N_BRANCH * D_MODEL,
)
PROJ_WIDTH = sum(SPLIT_SIZES)

kernel_name = 'hybrid_chunk_attn_retention_gla_deepnorm'


def split_columns(t, sizes):
    parts, start = [], 0
    for sz in sizes:
        parts.append(t[..., start:start + sz])
        start += sz
    return parts


def layer_norm(x, g, b):
    xf = x.astype(jnp.float32)
    mu = jnp.mean(xf, axis=-1, keepdims=True)
    var = jnp.mean(jnp.square(xf - mu), axis=-1, keepdims=True)
    return ((xf - mu) * lax.rsqrt(var + LN_EPS) * g + b).astype(x.dtype)


def head_layer_norm(t):
    mu = jnp.mean(t, axis=-1, keepdims=True)
    var = jnp.mean(jnp.square(t - mu), axis=-1, keepdims=True)
    return (t - mu) * lax.rsqrt(var + LN_EPS)


def head_rms_norm(t, g):
    return t * lax.rsqrt(jnp.mean(jnp.square(t), axis=-1, keepdims=True) + LN_EPS) * g.astype(jnp.float32)


def rope(t):
    s, d = t.shape[1], t.shape[-1]
    inv = ROPE_BASE ** (-jnp.arange(0, d, 2, dtype=jnp.float32) / d)
    ang = jnp.arange(s, dtype=jnp.float32)[:, None] * inv[None, :]
    cos, sin = jnp.cos(ang)[None, :, None, :], jnp.sin(ang)[None, :, None, :]
    t1, t2 = jnp.split(t.astype(jnp.float32), 2, axis=-1)
    return jnp.concatenate([t1 * cos - t2 * sin, t1 * sin + t2 * cos], axis=-1)


def to_chunks(t):
    b, s, n, d = t.shape
    return t.astype(jnp.float32).reshape(b, s // CHUNK, CHUNK, n, d).transpose(0, 3, 1, 2, 4)


def from_chunks(t):
    b, n, nc, c, d = t.shape
    return t.transpose(0, 2, 3, 1, 4).reshape(b, nc * c, n * d)


def chunk_band_attention(q, k, v, rel_bias):
    b, s, h, d = q.shape
    nc = s // CHUNK
    band = ATTN_LEFT_CHUNKS + 1
    pad = ((0, 0), (ATTN_LEFT_CHUNKS * CHUNK, 0), (0, 0), (0, 0))
    qc = (q * d ** -0.5).reshape(b, nc, CHUNK, h, d)
    kc = jnp.pad(k, pad).reshape(b, nc + ATTN_LEFT_CHUNKS, CHUNK, h, d)
    vc = jnp.pad(v, pad).reshape(b, nc + ATTN_LEFT_CHUNKS, CHUNK, h, d)
    scores = jnp.concatenate(
        [jnp.einsum('bnihd,bnjhd->bhnij', qc, kc[:, j:j + nc]) for j in range(band)], axis=-1
    ).astype(jnp.float32)
    i = jnp.arange(CHUNK)[:, None]
    m = jnp.arange(band * CHUNK)[None, :]
    rel = jnp.clip(i + ATTN_LEFT_CHUNKS * CHUNK - m, -REL_CLIP, REL_CLIP) + REL_CLIP
    bias = rel_bias.astype(jnp.float32)[:, rel]
    valid = (jnp.arange(nc)[:, None] + m // CHUNK - ATTN_LEFT_CHUNKS) >= 0
    scores = jnp.where(valid[None, None, :, None, :], scores + bias[None, :, None, :, :], NEG_INF)
    p = jax.nn.softmax(scores, axis=-1).astype(v.dtype)
    out = jnp.einsum('bhnij,bnjhd->bnihd', p[..., :CHUNK], vc[:, 0:nc])
    for j in range(1, band):
        out = out + jnp.einsum('bhnij,bnjhd->bnihd', p[..., j * CHUNK:(j + 1) * CHUNK], vc[:, j:j + nc])
    return out.reshape(b, s, h * d)


def chunk_decay_linear_attn(q, k, v, log_a):
    c = q.shape[-2]
    cum = jnp.cumsum(log_a, axis=-2)
    last = cum[..., -1:, :]
    q_fwd = q * jnp.exp(cum)
    s_fwd = jnp.einsum('bhnid,bhnjd->bhnij', q_fwd, k * jnp.exp(-cum))
    s_bwd = jnp.einsum('bhnid,bhnjd->bhnij', q * jnp.exp(-cum), k * jnp.exp(cum))
    lower = jnp.tril(jnp.ones((c, c), dtype=bool))
    intra = jnp.einsum('bhnij,bhnjv->bhniv', jnp.where(lower, s_fwd, s_bwd), v)
    ds = jnp.einsum('bhnjd,bhnjv->bhndv', k * jnp.exp(last - cum), v)
    chunk_decay = jnp.broadcast_to(jnp.exp(last[..., 0, :]), ds.shape[:-1])

    def step(state, inp):
        dec, inc = inp
        return dec[..., None] * state + inc, state

    init = jnp.zeros(ds.shape[:2] + ds.shape[3:], ds.dtype)
    _, s_prev = lax.scan(step, init, (jnp.moveaxis(chunk_decay, 2, 0), jnp.moveaxis(ds, 2, 0)))
    inter = jnp.einsum('bhnid,nbhdv->bhniv', q_fwd, s_prev)
    return intra + inter


def hybrid_token_mixer(h, w_in, rel_bias, gla_w_lr, gla_b_lr, gla_norm_g, w_branch, w_out):
    b, s, _ = h.shape
    (aq, ak, av, rq, rk, rv, rg, gq, gk, gv, gg, glr, gate_logits) = split_columns(h @ w_in, SPLIT_SIZES)

    def heads(t, n):
        return t.reshape(b, s, n, -1)

    attn = chunk_band_attention(heads(aq, ATTN_HEADS), heads(ak, ATTN_HEADS), heads(av, ATTN_HEADS), rel_bias)

    log_gamma = jnp.log1p(-jnp.exp2(-5.0 - jnp.arange(RET_HEADS, dtype=jnp.float32)))
    ret_log_a = jnp.broadcast_to(log_gamma[None, :, None, None, None], (1, RET_HEADS, 1, CHUNK, 1))
    ret = chunk_decay_linear_attn(
        to_chunks(rope(heads(rq, RET_HEADS))),
        to_chunks(rope(heads(rk, RET_HEADS)) * RET_KEY_DIM ** -0.5),
        to_chunks(heads(rv, RET_HEADS)),
        ret_log_a,
    )
    ret = (jax.nn.silu(rg.astype(jnp.float32)) * from_chunks(head_layer_norm(ret))).astype(h.dtype)

    gate_pre = (glr @ gla_w_lr + gla_b_lr).astype(jnp.float32)
    gla_log_a = to_chunks(heads(jax.nn.log_sigmoid(gate_pre) / GLA_GATE_NORMALIZER, GLA_HEADS))
    gla = chunk_decay_linear_attn(
        to_chunks(heads(gq, GLA_HEADS) * GLA_KEY_DIM ** -0.5),
        to_chunks(heads(gk, GLA_HEADS)),
        to_chunks(heads(gv, GLA_HEADS)),
        gla_log_a,
    )
    gla = (jax.nn.silu(gg.astype(jnp.float32)) * from_chunks(head_rms_norm(gla, gla_norm_g))).astype(h.dtype)

    branches = jnp.stack([attn, ret, gla], axis=2)
    proj = jnp.einsum('bsnc,ncd->bsnd', branches, w_branch)
    gates = jax.nn.sigmoid(gate_logits.reshape(b, s, N_BRANCH, D_MODEL))
    merged = jnp.sum(gates * proj, axis=2)
    return merged @ w_out


def squared_relu_mlp(h, w_up, w_down):
    return jnp.square(jax.nn.relu(h @ w_up)) @ w_down


def _fwd_setup_inputs(seed: int = 0) -> dict:
    key = jax.random.key(seed)
    ks = jax.random.split(key, 16)

    def nrm(k, shape, scale):
        return jax.random.normal(k, shape, jnp.float32) * scale

    return {
        'x': nrm(ks[0], (BATCH, SEQ, D_MODEL), 1.0),
        'ln_in_g': 1.0 + nrm(ks[1], (D_MODEL,), 0.02),
        'ln_in_b': nrm(ks[2], (D_MODEL,), 0.02),
        'w_in': nrm(ks[3], (DEPTH, D_MODEL, PROJ_WIDTH), D_MODEL ** -0.5),
        'rel_bias': nrm(ks[4], (DEPTH, ATTN_HEADS, 2 * REL_CLIP + 1), 0.5),
        'gla_w_lr': nrm(ks[5], (DEPTH, GLA_GATE_RANK, GLA_HEADS * GLA_KEY_DIM), GLA_GATE_RANK ** -0.5),
        'gla_b_lr': nrm(ks[6], (DEPTH, GLA_HEADS * GLA_KEY_DIM), 0.5),
        'gla_norm_g': 1.0 + nrm(ks[7], (DEPTH, GLA_VAL_DIM), 0.02),
        'w_branch': nrm(ks[8], (DEPTH, N_BRANCH, BRANCH_WIDTH, D_MODEL), BRANCH_WIDTH ** -0.5 * DEEPNORM_BETA),
        'w_out': nrm(ks[9], (DEPTH, D_MODEL, D_MODEL), D_MODEL ** -0.5 * DEEPNORM_BETA),
        'ln1_g': 1.0 + nrm(ks[10], (DEPTH, D_MODEL), 0.02),
        'ln1_b': nrm(ks[11], (DEPTH, D_MODEL), 0.02),
        'w_up': nrm(ks[12], (DEPTH, D_MODEL, D_FF), D_MODEL ** -0.5),
        'w_down': nrm(ks[13], (DEPTH, D_FF, D_MODEL), D_FF ** -0.5 * DEEPNORM_BETA),
        'ln2_g': 1.0 + nrm(ks[14], (DEPTH, D_MODEL), 0.02),
        'ln2_b': nrm(ks[15], (DEPTH, D_MODEL), 0.02),
    }


def _fwd_reference(x, ln_in_g, ln_in_b, w_in, rel_bias, gla_w_lr, gla_b_lr, gla_norm_g, w_branch, w_out,
              ln1_g, ln1_b, w_up, w_down, ln2_g, ln2_b):
    x = layer_norm(x, ln_in_g, ln_in_b)
    for l in range(DEPTH):
        mix = hybrid_token_mixer(x, w_in[l], rel_bias[l], gla_w_lr[l], gla_b_lr[l], gla_norm_g[l],
                                 w_branch[l], w_out[l])
        x = layer_norm(DEEPNORM_ALPHA * x + mix, ln1_g[l], ln1_b[l])
        x = layer_norm(DEEPNORM_ALPHA * x + squared_relu_mlp(x, w_up[l], w_down[l]), ln2_g[l], ln2_b[l])
    return x


import jax as _jax
import jax.numpy as _jnp

TWIN_FORMAT = 'train_step'
FWD_PARAMS = ['x', 'ln_in_g', 'ln_in_b', 'w_in', 'rel_bias', 'gla_w_lr', 'gla_b_lr', 'gla_norm_g', 'w_branch', 'w_out', 'ln1_g', 'ln1_b', 'w_up', 'w_down', 'ln2_g', 'ln2_b']
TWIN_WEIGHTS = ['ln_in_g', 'ln_in_b', 'w_in', 'rel_bias', 'gla_w_lr', 'gla_b_lr', 'gla_norm_g', 'w_branch', 'w_out', 'ln1_g', 'ln1_b', 'w_up', 'w_down', 'ln2_g', 'ln2_b']
TWIN_DIFF_INPUT = 'x'
TWIN_INPUTS = ['x', 'ln_in_g', 'ln_in_b', 'w_in', 'rel_bias', 'gla_w_lr', 'gla_b_lr', 'gla_norm_g', 'w_branch', 'w_out', 'ln1_g', 'ln1_b', 'w_up', 'w_down', 'ln2_g', 'ln2_b', 'loss_target', 'm_ln_in_g', 'm_ln_in_b', 'm_w_in', 'm_rel_bias', 'm_gla_w_lr', 'm_gla_b_lr', 'm_gla_norm_g', 'm_w_branch', 'm_w_out', 'm_ln1_g', 'm_ln1_b', 'm_w_up', 'm_w_down', 'm_ln2_g', 'm_ln2_b', 'v_ln_in_g', 'v_ln_in_b', 'v_w_in', 'v_rel_bias', 'v_gla_w_lr', 'v_gla_b_lr', 'v_gla_norm_g', 'v_w_branch', 'v_w_out', 'v_ln1_g', 'v_ln1_b', 'v_w_up', 'v_w_down', 'v_ln2_g', 'v_ln2_b']
TWIN_OUTPUTS = ['loss', 'grad_x', 'grad_ln_in_g', 'grad_ln_in_b', 'grad_w_in', 'grad_rel_bias', 'grad_gla_w_lr', 'grad_gla_b_lr', 'grad_gla_norm_g', 'grad_w_branch', 'grad_w_out', 'grad_ln1_g', 'grad_ln1_b', 'grad_w_up', 'grad_w_down', 'grad_ln2_g', 'grad_ln2_b', 'delta_ln_in_g', 'delta_ln_in_b', 'delta_w_in', 'delta_rel_bias', 'delta_gla_w_lr', 'delta_gla_b_lr', 'delta_gla_norm_g', 'delta_w_branch', 'delta_w_out', 'delta_ln1_g', 'delta_ln1_b', 'delta_w_up', 'delta_w_down', 'delta_ln2_g', 'delta_ln2_b', 'new_m_ln_in_g', 'new_m_ln_in_b', 'new_m_w_in', 'new_m_rel_bias', 'new_m_gla_w_lr', 'new_m_gla_b_lr', 'new_m_gla_norm_g', 'new_m_w_branch', 'new_m_w_out', 'new_m_ln1_g', 'new_m_ln1_b', 'new_m_w_up', 'new_m_w_down', 'new_m_ln2_g', 'new_m_ln2_b', 'new_v_ln_in_g', 'new_v_ln_in_b', 'new_v_w_in', 'new_v_rel_bias', 'new_v_gla_w_lr', 'new_v_gla_b_lr', 'new_v_gla_norm_g', 'new_v_w_branch', 'new_v_w_out', 'new_v_ln1_g', 'new_v_ln1_b', 'new_v_w_up', 'new_v_w_down', 'new_v_ln2_g', 'new_v_ln2_b']
TWIN_LEAF_KINDS = {'loss': 'loss', 'grad_x': 'grad_x', 'grad_ln_in_g': 'grad_w', 'grad_ln_in_b': 'grad_w', 'grad_w_in': 'grad_w', 'grad_rel_bias': 'grad_w', 'grad_gla_w_lr': 'grad_w', 'grad_gla_b_lr': 'grad_w', 'grad_gla_norm_g': 'grad_w', 'grad_w_branch': 'grad_w', 'grad_w_out': 'grad_w', 'grad_ln1_g': 'grad_w', 'grad_ln1_b': 'grad_w', 'grad_w_up': 'grad_w', 'grad_w_down': 'grad_w', 'grad_ln2_g': 'grad_w', 'grad_ln2_b': 'grad_w', 'delta_ln_in_g': 'delta_w', 'delta_ln_in_b': 'delta_w', 'delta_w_in': 'delta_w', 'delta_rel_bias': 'delta_w', 'delta_gla_w_lr': 'delta_w', 'delta_gla_b_lr': 'delta_w', 'delta_gla_norm_g': 'delta_w', 'delta_w_branch': 'delta_w', 'delta_w_out': 'delta_w', 'delta_ln1_g': 'delta_w', 'delta_ln1_b': 'delta_w', 'delta_w_up': 'delta_w', 'delta_w_down': 'delta_w', 'delta_ln2_g': 'delta_w', 'delta_ln2_b': 'delta_w', 'new_m_ln_in_g': 'new_m', 'new_m_ln_in_b': 'new_m', 'new_m_w_in': 'new_m', 'new_m_rel_bias': 'new_m', 'new_m_gla_w_lr': 'new_m', 'new_m_gla_b_lr': 'new_m', 'new_m_gla_norm_g': 'new_m', 'new_m_w_branch': 'new_m', 'new_m_w_out': 'new_m', 'new_m_ln1_g': 'new_m', 'new_m_ln1_b': 'new_m', 'new_m_w_up': 'new_m', 'new_m_w_down': 'new_m', 'new_m_ln2_g': 'new_m', 'new_m_ln2_b': 'new_m', 'new_v_ln_in_g': 'new_v', 'new_v_ln_in_b': 'new_v', 'new_v_w_in': 'new_v', 'new_v_rel_bias': 'new_v', 'new_v_gla_w_lr': 'new_v', 'new_v_gla_b_lr': 'new_v', 'new_v_gla_norm_g': 'new_v', 'new_v_w_branch': 'new_v', 'new_v_w_out': 'new_v', 'new_v_ln1_g': 'new_v', 'new_v_ln1_b': 'new_v', 'new_v_w_up': 'new_v', 'new_v_w_down': 'new_v', 'new_v_ln2_g': 'new_v', 'new_v_ln2_b': 'new_v'}


def _forward(args):
    return _fwd_reference(*[args[k] for k in FWD_PARAMS])


def _output_shape():
    out = _jax.eval_shape(lambda: _forward(_fwd_setup_inputs(0)))
    return out.shape, out.dtype

N_MICROBATCH = 1
ADAM_LR = 0.001
ADAM_B1 = 0.9
ADAM_B2 = 0.999
ADAM_EPS = 1e-08
ADAM_WD = 0.01
ADAM_STEP = 10
PER_EXAMPLE_BATCH_AXIS = {'x': 0, 'loss_target': 0}
SHARED_INPUTS = []
_WEIGHT_DTYPES = {'ln_in_g': _jnp.float32, 'ln_in_b': _jnp.float32, 'w_in': _jnp.float32, 'rel_bias': _jnp.float32, 'gla_w_lr': _jnp.float32, 'gla_b_lr': _jnp.float32, 'gla_norm_g': _jnp.float32, 'w_branch': _jnp.float32, 'w_out': _jnp.float32, 'ln1_g': _jnp.float32, 'ln1_b': _jnp.float32, 'w_up': _jnp.float32, 'w_down': _jnp.float32, 'ln2_g': _jnp.float32, 'ln2_b': _jnp.float32}
MOMENT_SCALE = {'ln_in_g': 6.716510e-01, 'ln_in_b': 4.426382e-01, 'w_in': 9.027091e-03, 'rel_bias': 1.867605e-03, 'gla_w_lr': 1.683453e-03, 'gla_b_lr': 6.761459e-03, 'gla_norm_g': 2.151738e-02, 'w_branch': 1.776693e-02, 'w_out': 3.072571e-02, 'ln1_g': 7.725343e-01, 'ln1_b': 4.504881e-01, 'w_up': 4.182265e-02, 'w_down': 1.900571e-01, 'ln2_g': 2.270215e+01, 'ln2_b': 4.988439e+00}


def _to_microbatches(a, axis):
    t = _jnp.moveaxis(a, axis, 0)
    t = t.reshape((N_MICROBATCH, t.shape[0] // N_MICROBATCH) + t.shape[1:])
    return _jnp.moveaxis(t, 1, axis + 1)


def setup_inputs(seed: int = 0) -> dict:
    inp = _fwd_setup_inputs(seed)
    key = _jax.random.fold_in(_jax.random.key(seed), 7919)
    shape, _ = _output_shape()
    out = dict(inp)
    out["loss_target"] = _jax.random.normal(_jax.random.fold_in(key, 0), shape, _jnp.float32)
    for i, name in enumerate(TWIN_WEIGHTS):
        w = inp[name].astype(_jnp.float32)
        if MOMENT_SCALE is None:
            s = _jnp.sqrt(_jnp.mean(_jnp.square(w)) + 1e-30)
        else:
            s = MOMENT_SCALE[name]
        km, kv = _jax.random.split(_jax.random.fold_in(key, i + 1))
        out[name] = w
        out["m_" + name] = s * _jax.random.normal(km, w.shape, _jnp.float32)
        out["v_" + name] = (s * s) * _jax.random.uniform(kv, w.shape, _jnp.float32, 0.5, 1.5)
    if N_MICROBATCH > 1:
        for name, axis in PER_EXAMPLE_BATCH_AXIS.items():
            out[name] = _to_microbatches(out[name], axis)
    return {'x': out['x'], 'ln_in_g': out['ln_in_g'], 'ln_in_b': out['ln_in_b'], 'w_in': out['w_in'], 'rel_bias': out['rel_bias'], 'gla_w_lr': out['gla_w_lr'], 'gla_b_lr': out['gla_b_lr'], 'gla_norm_g': out['gla_norm_g'], 'w_branch': out['w_branch'], 'w_out': out['w_out'], 'ln1_g': out['ln1_g'], 'ln1_b': out['ln1_b'], 'w_up': out['w_up'], 'w_down': out['w_down'], 'ln2_g': out['ln2_g'], 'ln2_b': out['ln2_b'], 'loss_target': out['loss_target'], 'm_ln_in_g': out['m_ln_in_g'], 'm_ln_in_b': out['m_ln_in_b'], 'm_w_in': out['m_w_in'], 'm_rel_bias': out['m_rel_bias'], 'm_gla_w_lr': out['m_gla_w_lr'], 'm_gla_b_lr': out['m_gla_b_lr'], 'm_gla_norm_g': out['m_gla_norm_g'], 'm_w_branch': out['m_w_branch'], 'm_w_out': out['m_w_out'], 'm_ln1_g': out['m_ln1_g'], 'm_ln1_b': out['m_ln1_b'], 'm_w_up': out['m_w_up'], 'm_w_down': out['m_w_down'], 'm_ln2_g': out['m_ln2_g'], 'm_ln2_b': out['m_ln2_b'], 'v_ln_in_g': out['v_ln_in_g'], 'v_ln_in_b': out['v_ln_in_b'], 'v_w_in': out['v_w_in'], 'v_rel_bias': out['v_rel_bias'], 'v_gla_w_lr': out['v_gla_w_lr'], 'v_gla_b_lr': out['v_gla_b_lr'], 'v_gla_norm_g': out['v_gla_norm_g'], 'v_w_branch': out['v_w_branch'], 'v_w_out': out['v_w_out'], 'v_ln1_g': out['v_ln1_g'], 'v_ln1_b': out['v_ln1_b'], 'v_w_up': out['v_w_up'], 'v_w_down': out['v_w_down'], 'v_ln2_g': out['v_ln2_g'], 'v_ln2_b': out['v_ln2_b']}


def _loss(weights, diff, rest, loss_target):
    with _jax.named_scope("forward"):
        args = {**rest, TWIN_DIFF_INPUT: diff, **{k: w.astype(_WEIGHT_DTYPES[k]) for k, w in weights.items()}}
        y = _forward(args)
    with _jax.named_scope("loss_head"):
        err = _jnp.square(y.astype(_jnp.float32) - loss_target)
        return 0.5 * _jnp.sum(_jnp.mean(err, axis=-1)) if err.ndim else 0.5 * err


def _adamw(w, g, m, v):
    m = ADAM_B1 * m + (1.0 - ADAM_B1) * g
    v = ADAM_B2 * v + (1.0 - ADAM_B2) * _jnp.square(g)
    m_hat = m / (1.0 - ADAM_B1 ** ADAM_STEP)
    v_hat = v / (1.0 - ADAM_B2 ** ADAM_STEP)
    delta = -ADAM_LR * (m_hat / (_jnp.sqrt(v_hat) + ADAM_EPS) + ADAM_WD * w)
    return delta, m, v


def reference(x, ln_in_g, ln_in_b, w_in, rel_bias, gla_w_lr, gla_b_lr, gla_norm_g, w_branch, w_out, ln1_g, ln1_b, w_up, w_down, ln2_g, ln2_b, loss_target, m_ln_in_g, m_ln_in_b, m_w_in, m_rel_bias, m_gla_w_lr, m_gla_b_lr, m_gla_norm_g, m_w_branch, m_w_out, m_ln1_g, m_ln1_b, m_w_up, m_w_down, m_ln2_g, m_ln2_b, v_ln_in_g, v_ln_in_b, v_w_in, v_rel_bias, v_gla_w_lr, v_gla_b_lr, v_gla_norm_g, v_w_branch, v_w_out, v_ln1_g, v_ln1_b, v_w_up, v_w_down, v_ln2_g, v_ln2_b):
    given = dict(x=x, ln_in_g=ln_in_g, ln_in_b=ln_in_b, w_in=w_in, rel_bias=rel_bias, gla_w_lr=gla_w_lr, gla_b_lr=gla_b_lr, gla_norm_g=gla_norm_g, w_branch=w_branch, w_out=w_out, ln1_g=ln1_g, ln1_b=ln1_b, w_up=w_up, w_down=w_down, ln2_g=ln2_g, ln2_b=ln2_b, loss_target=loss_target, m_ln_in_g=m_ln_in_g, m_ln_in_b=m_ln_in_b, m_w_in=m_w_in, m_rel_bias=m_rel_bias, m_gla_w_lr=m_gla_w_lr, m_gla_b_lr=m_gla_b_lr, m_gla_norm_g=m_gla_norm_g, m_w_branch=m_w_branch, m_w_out=m_w_out, m_ln1_g=m_ln1_g, m_ln1_b=m_ln1_b, m_w_up=m_w_up, m_w_down=m_w_down, m_ln2_g=m_ln2_g, m_ln2_b=m_ln2_b, v_ln_in_g=v_ln_in_g, v_ln_in_b=v_ln_in_b, v_w_in=v_w_in, v_rel_bias=v_rel_bias, v_gla_w_lr=v_gla_w_lr, v_gla_b_lr=v_gla_b_lr, v_gla_norm_g=v_gla_norm_g, v_w_branch=v_w_branch, v_w_out=v_w_out, v_ln1_g=v_ln1_g, v_ln1_b=v_ln1_b, v_w_up=v_w_up, v_w_down=v_w_down, v_ln2_g=v_ln2_g, v_ln2_b=v_ln2_b)
    weights = {n: given[n] for n in TWIN_WEIGHTS}
    shared = {n: given[n] for n in SHARED_INPUTS}
    per_example = {n: given[n] for n in ['x']}
    grad_fn = _jax.value_and_grad(_loss, argnums=(0, 1))

    def one_microbatch(ex, loss_target):
        ex = dict(ex)
        diff = ex.pop(TWIN_DIFF_INPUT)
        return grad_fn(weights, diff, {**shared, **ex}, loss_target)

    if N_MICROBATCH == 1:
        loss, (grad_w, grad_x) = one_microbatch(per_example, given["loss_target"])
    else:
        def body(carry, xs):
            loss_sum, grad_sum = carry
            l_k, (gw_k, gx_k) = one_microbatch(xs[0], xs[1])
            with _jax.named_scope("update"):
                return (loss_sum + l_k, _jax.tree.map(_jnp.add, grad_sum, gw_k)), gx_k

        init = (_jnp.zeros((), _jnp.float32), _jax.tree.map(_jnp.zeros_like, weights))
        (loss, grad_w), grad_x = _jax.lax.scan(body, init, (per_example, given["loss_target"]))
    with _jax.named_scope("update"):
        delta_w, new_m, new_v = {}, {}, {}
        for n in TWIN_WEIGHTS:
            delta_w[n], new_m[n], new_v[n] = _adamw(weights[n], grad_w[n], given["m_" + n], given["v_" + n])
    return (loss, grad_x, *[grad_w[n] for n in TWIN_WEIGHTS], *[delta_w[n] for n in TWIN_WEIGHTS],
            *[new_m[n] for n in TWIN_WEIGHTS], *[new_v[n] for n in TWIN_WEIGHTS])
```

```python
import functools

import numpy as np
import jax
import jax.numpy as jnp
from jax import lax
from jax.experimental import pallas as pl
from jax.experimental.pallas import tpu as pltpu

F32 = jnp.float32
BF16 = jnp.bfloat16
MXU_DTYPE = BF16
HI = lax.Precision.HIGHEST

DEPTH = 2
CHUNK = 64
N_BRANCH = 3
ATTN_HEADS = 8
ATTN_LEFT = 8
REL_CLIP = 2 * CHUNK
LIN_HEADS = 4
GATE_RANK = 16
GATE_NORM = 16.0
LN_EPS = 1e-5
NEG_INF = -1e30
ALPHA = (2 * DEPTH) ** 0.25
ADAM_LR, ADAM_B1, ADAM_B2, ADAM_EPS, ADAM_WD, ADAM_STEP = 0.001, 0.9, 0.999, 1e-08, 0.01, 10

LANE = 128
VMEM_LIMIT = 56 << 20
QB = 256
KW = 3 * QB
LB = 256
MESH_AXES = ("x", "y", "c")
DEV = pl.DeviceIdType.MESH


def _cp(sem):
    return pltpu.CompilerParams(dimension_semantics=sem, vmem_limit_bytes=VMEM_LIMIT)


def _mx(v):
    return v.astype(MXU_DTYPE)


def _dot(a, b):
    return jnp.dot(_mx(a), _mx(b), preferred_element_type=F32)


def _dot_nt(a, b):
    return lax.dot_general(_mx(a), _mx(b), (((1,), (1,)), ((), ())), preferred_element_type=F32)


def _dot_hi(a, b):
    return jnp.dot(a, b, precision=HI, preferred_element_type=F32)


def _sigmoid(v):
    return 1.0 / (1.0 + jnp.exp(-v))


def _sds(shape, dtype):
    return jax.ShapeDtypeStruct(shape, dtype)


def _mm(name, a, b, tm, tn, nt=False, out_dtype=F32):
    batched = a.ndim == 3
    m, k = a.shape[-2:]
    n = b.shape[-2] if nt else b.shape[-1]
    tm, tn = min(tm, m), min(tn, n)

    def body(a_ref, b_ref, o_ref):
        f = _dot_nt if nt else _dot
        o_ref[...] = f(a_ref[...], b_ref[...]).astype(o_ref.dtype)

    if batched:
        nb = a.shape[0]
        grid = (nb, m // tm, n // tn)
        a_spec = pl.BlockSpec((None, tm, k), lambda g, i, j: (g, i, 0))
        b_spec = (pl.BlockSpec((None, tn, k), lambda g, i, j: (g, j, 0)) if nt
                  else pl.BlockSpec((None, k, tn), lambda g, i, j: (g, 0, j)))
        o_spec = pl.BlockSpec((None, tm, tn), lambda g, i, j: (g, i, j))
        out_shape = _sds((nb, m, n), out_dtype)
        sem = ("parallel", "parallel", "parallel")
    else:
        grid = (m // tm, n // tn)
        a_spec = pl.BlockSpec((tm, k), lambda i, j: (i, 0))
        b_spec = (pl.BlockSpec((tn, k), lambda i, j: (j, 0)) if nt
                  else pl.BlockSpec((k, tn), lambda i, j: (0, j)))
        o_spec = pl.BlockSpec((tm, tn), lambda i, j: (i, j))
        out_shape = _sds((m, n), out_dtype)
        sem = ("parallel", "parallel")
    return pl.pallas_call(body, name=name, grid=grid, in_specs=[a_spec, b_spec], out_specs=o_spec,
                          out_shape=out_shape, compiler_params=_cp(sem))(a, b)


def _ln_rows(y, g, b):
    mu = jnp.mean(y, axis=-1, keepdims=True)
    yc = y - mu
    var = jnp.mean(yc * yc, axis=-1, keepdims=True)
    rs = lax.rsqrt(var + LN_EPS)
    xh = yc * rs
    return xh * g + b, xh, rs


def _ln_in(x, g, b, tm=256):
    t, d = x.shape

    def body(x_ref, g_ref, b_ref, o_ref, ob_ref, xh_ref, rs_ref):
        o, xh, rs = _ln_rows(x_ref[...], g_ref[...], b_ref[...])
        o_ref[...] = o
        ob_ref[...] = o.astype(BF16)
        xh_ref[...] = xh
        rs_ref[...] = rs

    row = pl.BlockSpec((tm, d), lambda i: (i, 0))
    vec = pl.BlockSpec((1, d), lambda i: (0, 0))
    return pl.pallas_call(
        body, name="ln_in", grid=(t // tm,), in_specs=[row, vec, vec],
        out_specs=[row, row, row, pl.BlockSpec((tm, 1), lambda i: (i, 0))],
        out_shape=[_sds((t, d), F32), _sds((t, d), BF16), _sds((t, d), F32), _sds((t, 1), F32)],
        compiler_params=_cp(("parallel",)))(x, g, b)


def _mm_res_ln(name, a, w, res, g, b, tm, relu2):
    t, k = a.shape
    d = w.shape[1]

    def body(a_ref, w_ref, r_ref, g_ref, b_ref, o_ref, ob_ref, xh_ref, rs_ref, *act_ref):
        av = a_ref[...]
        if relu2:
            av = jnp.square(jnp.maximum(av, 0.0))
            act_ref[0][...] = av.astype(BF16)
        y = ALPHA * r_ref[...] + _dot(av, w_ref[...])
        o, xh, rs = _ln_rows(y, g_ref[...], b_ref[...])
        o_ref[...] = o
        ob_ref[...] = o.astype(BF16)
        xh_ref[...] = xh
        rs_ref[...] = rs

    row = pl.BlockSpec((tm, d), lambda i: (i, 0))
    vec = pl.BlockSpec((1, d), lambda i: (0, 0))
    arow = pl.BlockSpec((tm, k), lambda i: (i, 0))
    out_specs = [row, row, row, pl.BlockSpec((tm, 1), lambda i: (i, 0))]
    out_shape = [_sds((t, d), F32), _sds((t, d), BF16), _sds((t, d), F32), _sds((t, 1), F32)]
    if relu2:
        out_specs.append(arow)
        out_shape.append(_sds((t, k), BF16))
    return pl.pallas_call(
        body, name=name, grid=(t // tm,),
        in_specs=[arow, pl.BlockSpec((k, d), lambda i: (0, 0)), row, vec, vec],
        out_specs=out_specs, out_shape=out_shape, compiler_params=_cp(("parallel",)))(a, w, res, g, b)


def _merge_fwd(bo, wb, p, gate_off, tm=512, tn=512):
    _, t, d = bo.shape
    gb = gate_off // tn

    def body(bo_ref, wb_ref, g0, g1, g2, proj_ref, m_ref):
        acc = None
        for n, g_ref in enumerate((g0, g1, g2)):
            pr = _dot(bo_ref[n], wb_ref[n])
            proj_ref[n] = pr
            term = _sigmoid(g_ref[...]) * pr
            acc = term if acc is None else acc + term
        m_ref[...] = acc.astype(BF16)

    gspecs = [pl.BlockSpec((tm, tn), functools.partial(lambda i, j, n: (i, gb + n * (d // tn) + j), n=n))
              for n in range(3)]
    return pl.pallas_call(
        body, name="merge_fwd", grid=(t // tm, d // tn),
        in_specs=[pl.BlockSpec((3, tm, d), lambda i, j: (0, i, 0)),
                  pl.BlockSpec((3, d, tn), lambda i, j: (0, 0, j))] + gspecs,
        out_specs=[pl.BlockSpec((3, tm, tn), lambda i, j: (0, i, j)), pl.BlockSpec((tm, tn), lambda i, j: (i, j))],
        out_shape=[_sds((3, t, d), F32), _sds((t, d), BF16)],
        compiler_params=_cp(("parallel", "parallel")))(bo, wb, p, p, p)


def _merge_bwd(dz, wout, proj, p, gate_off, tm=512, tn=512):
    t, d = dz.shape
    gb = gate_off // tn

    def body(dz_ref, w_ref, proj_ref, g0, g1, g2, dproj_ref, dgl_ref):
        dm = _dot_nt(dz_ref[...], w_ref[...])
        for n, g_ref in enumerate((g0, g1, g2)):
            s = _sigmoid(g_ref[...])
            dproj_ref[n] = (dm * s).astype(BF16)
            dgl_ref[n] = (dm * proj_ref[n] * (s * (1.0 - s))).astype(BF16)

    gspecs = [pl.BlockSpec((tm, tn), functools.partial(lambda i, j, n: (i, gb + n * (d // tn) + j), n=n))
              for n in range(3)]
    dproj, dgl = pl.pallas_call(
        body, name="merge_bwd", grid=(t // tm, d // tn),
        in_specs=[pl.BlockSpec((tm, d), lambda i, j: (i, 0)), pl.BlockSpec((tn, d), lambda i, j: (j, 0)),
                  pl.BlockSpec((3, tm, tn), lambda i, j: (0, i, j))] + gspecs,
        out_specs=[pl.BlockSpec((3, tm, tn), lambda i, j: (0, i, j)),
                   pl.BlockSpec((3, tm, tn), lambda i, j: (0, i, j))],
        out_shape=[_sds((3, t, d), BF16), _sds((3, t, d), BF16)],
        compiler_params=_cp(("parallel", "parallel")))(dz, wout, proj, p, p, p)
    return dproj, dgl


def _mm_nt_relu2_bwd(dz, wdown, u, tm=512, tn=1024):
    t, d = dz.shape
    f = wdown.shape[0]

    def body(dz_ref, w_ref, u_ref, du_ref):
        da = _dot_nt(dz_ref[...], w_ref[...])
        du_ref[...] = (da * (2.0 * jnp.maximum(u_ref[...], 0.0))).astype(BF16)

    return pl.pallas_call(
        body, name="mlp_down_bwd", grid=(t // tm, f // tn),
        in_specs=[pl.BlockSpec((tm, d), lambda i, j: (i, 0)), pl.BlockSpec((tn, d), lambda i, j: (j, 0)),
                  pl.BlockSpec((tm, tn), lambda i, j: (i, j))],
        out_specs=pl.BlockSpec((tm, tn), lambda i, j: (i, j)), out_shape=_sds((t, f), BF16),
        compiler_params=_cp(("parallel", "parallel")))(dz, wdown, u)


def _ln_bwd_rows(dx, xh, rs, g):
    dxh = dx * g
    m1 = jnp.mean(dxh, axis=-1, keepdims=True)
    m2 = jnp.mean(dxh * xh, axis=-1, keepdims=True)
    return rs * (dxh - m1 - xh * m2)


def _mm_nt_res_lnbwd(name, a, w, dres, xh, rs, g, tm, tk):
    t, k = a.shape
    d = w.shape[0]
    nk = k // tk

    def body(a_ref, w_ref, dr_ref, xh_ref, rs_ref, g_ref, dz_ref, dzb_ref, dg_ref, db_ref, acc_ref):
        i, kk = pl.program_id(0), pl.program_id(1)

        @pl.when(kk == 0)
        def _():
            acc_ref[...] = ALPHA * dr_ref[...]

        acc_ref[...] += _dot_nt(a_ref[...], w_ref[...])

        @pl.when(jnp.logical_and(i == 0, kk == 0))
        def _():
            dg_ref[...] = jnp.zeros_like(dg_ref)
            db_ref[...] = jnp.zeros_like(db_ref)

        @pl.when(kk == nk - 1)
        def _():
            dx = acc_ref[...]
            xhv = xh_ref[...]
            dz = _ln_bwd_rows(dx, xhv, rs_ref[...], g_ref[...])
            dz_ref[...] = dz
            dzb_ref[...] = dz.astype(BF16)
            dg_ref[...] += jnp.sum(dx * xhv, axis=0, keepdims=True)
            db_ref[...] += jnp.sum(dx, axis=0, keepdims=True)

    row = pl.BlockSpec((tm, d), lambda i, kk: (i, 0))
    vec = pl.BlockSpec((1, d), lambda i, kk: (0, 0))
    return pl.pallas_call(
        body, name=name, grid=(t // tm, nk),
        in_specs=[pl.BlockSpec((tm, tk), lambda i, kk: (i, kk)), pl.BlockSpec((d, tk), lambda i, kk: (0, kk)),
                  row, row, pl.BlockSpec((tm, 1), lambda i, kk: (i, 0)), vec],
        out_specs=[row, row, vec, vec],
        out_shape=[_sds((t, d), F32), _sds((t, d), BF16), _sds((1, d), F32), _sds((1, d), F32)],
        scratch_shapes=[pltpu.VMEM((tm, d), F32)],
        compiler_params=_cp(("arbitrary", "arbitrary")))(a, w, dres, xh, rs, g)


def _loss_ln_bwd(x2, target, xh, rs, g, tm=256):
    t, d = x2.shape

    def body(x_ref, t_ref, xh_ref, rs_ref, g_ref, loss_ref, dz_ref, dzb_ref, dg_ref, db_ref):
        @pl.when(pl.program_id(0) == 0)
        def _():
            loss_ref[...] = jnp.zeros_like(loss_ref)
            dg_ref[...] = jnp.zeros_like(dg_ref)
            db_ref[...] = jnp.zeros_like(db_ref)

        err = x_ref[...] - t_ref[...]
        per_row = jnp.mean(err * err, axis=-1, keepdims=True)
        loss_ref[...] += 0.5 * jnp.sum(per_row, axis=0, keepdims=True)
        dx = err * (1.0 / d)
        xhv = xh_ref[...]
        dz = _ln_bwd_rows(dx, xhv, rs_ref[...], g_ref[...])
        dz_ref[...] = dz
        dzb_ref[...] = dz.astype(BF16)
        dg_ref[...] += jnp.sum(dx * xhv, axis=0, keepdims=True)
        db_ref[...] += jnp.sum(dx, axis=0, keepdims=True)

    row = pl.BlockSpec((tm, d), lambda i: (i, 0))
    vec = pl.BlockSpec((1, d), lambda i: (0, 0))
    return pl.pallas_call(
        body, name="loss_ln_bwd", grid=(t // tm,),
        in_specs=[row, row, row, pl.BlockSpec((tm, 1), lambda i: (i, 0)), vec],
        out_specs=[pl.BlockSpec((1, LANE), lambda i: (0, 0)), row, row, vec, vec],
        out_shape=[_sds((1, LANE), F32), _sds((t, d), F32), _sds((t, d), BF16), _sds((1, d), F32),
                   _sds((1, d), F32)],
        compiler_params=_cp(("arbitrary",)))(x2, target, xh, rs, g)


def _attn_scores(q_ref, k_refs, bias_ref, i, dh):
    q = q_ref[...] * (dh ** -0.5)
    k = jnp.concatenate([r[...] for r in k_refs], axis=0)
    s = _dot_nt(q, k) + bias_ref[...]
    col = lax.broadcasted_iota(jnp.int32, s.shape, 1)
    s = jnp.where(col >= (2 - i) * QB, s, NEG_INF)
    m = jnp.max(s, axis=-1, keepdims=True)
    e = jnp.exp(s - m)
    return q, k, e / jnp.sum(e, axis=-1, keepdims=True)


def _attn_specs(dh, qcol, kcol, vcol):
    q_spec = pl.BlockSpec((QB, dh), lambda h, i: (i, qcol + h))
    k_specs = [pl.BlockSpec((QB, dh), functools.partial(lambda h, i, j: (jnp.maximum(i - 2 + j, 0), kcol + h), j=j))
               for j in range(3)]
    v_specs = [pl.BlockSpec((QB, dh), functools.partial(lambda h, i, j: (jnp.maximum(i - 2 + j, 0), vcol + h), j=j))
               for j in range(3)]
    bias_spec = pl.BlockSpec((None, QB, KW), lambda h, i: (h, 0, 0))
    return q_spec, k_specs, v_specs, bias_spec


def _attn_fwd(p, bias, d, off):
    t = p.shape[0]
    dh = d // ATTN_HEADS

    def body(q_ref, k0, k1, k2, v0, v1, v2, bias_ref, o_ref):
        _, _, pr = _attn_scores(q_ref, (k0, k1, k2), bias_ref, pl.program_id(1), dh)
        v = jnp.concatenate([v0[...], v1[...], v2[...]], axis=0)
        o_ref[...] = _dot(pr, v).astype(o_ref.dtype)

    q_spec, k_specs, v_specs, bias_spec = _attn_specs(dh, off["aq"] // dh, off["ak"] // dh, off["av"] // dh)
    return pl.pallas_call(
        body, name="attn_fwd", grid=(ATTN_HEADS, t // QB),
        in_specs=[q_spec] + k_specs + v_specs + [bias_spec],
        out_specs=pl.BlockSpec((QB, dh), lambda h, i: (i, h)), out_shape=_sds((t, d), BF16),
        compiler_params=_cp(("parallel", "parallel")))(p, p, p, p, p, p, p, bias)


def _attn_bwd(p, bias, do, d, off):
    t = p.shape[0]
    dh = d // ATTN_HEADS
    tp = t + 2 * QB

    def body(q_ref, k0, k1, k2, v0, v1, v2, bias_ref, do_ref, dq_ref, dk_ref, dv_ref, dbias_ref):
        i = pl.program_id(1)

        @pl.when(i == 0)
        def _():
            dk_ref[...] = jnp.zeros_like(dk_ref)
            dv_ref[...] = jnp.zeros_like(dv_ref)
            dbias_ref[...] = jnp.zeros_like(dbias_ref)

        q, k, pr = _attn_scores(q_ref, (k0, k1, k2), bias_ref, i, dh)
        v = jnp.concatenate([v0[...], v1[...], v2[...]], axis=0)
        dov = do_ref[...]
        dp = _dot_nt(dov, v)
        delta = jnp.sum(pr * dp, axis=-1, keepdims=True)
        ds = pr * (dp - delta)
        dbias_ref[...] += ds
        dq_ref[...] = (_dot(ds, k) * (dh ** -0.5)).astype(dq_ref.dtype)
        rows = pl.ds(pl.multiple_of(i * QB, QB), KW)
        dk_ref[rows, :] += _dot(ds.T, q)
        dv_ref[rows, :] += _dot(pr.T, dov)

    q_spec, k_specs, v_specs, bias_spec = _attn_specs(dh, off["aq"] // dh, off["ak"] // dh, off["av"] // dh)
    acc_spec = pl.BlockSpec((None, tp, dh), lambda h, i: (h, 0, 0))
    return pl.pallas_call(
        body, name="attn_bwd", grid=(ATTN_HEADS, t // QB),
        in_specs=[q_spec] + k_specs + v_specs + [bias_spec, pl.BlockSpec((QB, dh), lambda h, i: (i, h))],
        out_specs=[pl.BlockSpec((QB, dh), lambda h, i: (i, h)), acc_spec, acc_spec, bias_spec],
        out_shape=[_sds((t, d), BF16), _sds((ATTN_HEADS, tp, dh), F32), _sds((ATTN_HEADS, tp, dh), F32),
                   _sds((ATTN_HEADS, QB, KW), F32)],
        compiler_params=_cp(("parallel", "arbitrary")))(p, p, p, p, p, p, p, bias, do)


def _rel_index():
    i = np.arange(QB)[:, None]
    j = np.arange(KW)[None, :]
    rel = np.clip(i + 2 * QB - j, -REL_CLIP, REL_CLIP) + REL_CLIP
    qc, kc = i // CHUNK + (2 * QB) // CHUNK, j // CHUNK
    band = (kc <= qc) & (kc >= qc - ATTN_LEFT)
    return rel, band


def _bias_expand(rel_bias):
    rel, band = _rel_index()
    return jnp.where(band[None], jnp.take(rel_bias, jnp.asarray(rel), axis=1), NEG_INF)


def _bias_reduce(dbias):
    h = dbias.shape[0]
    nq, nk = QB // CHUNK, KW // CHUNK
    nbin = 3 * LANE
    blocks = dbias.reshape(h, nq, CHUNK, nk, CHUNK).transpose(0, 1, 3, 2, 4).reshape(h * nq * nk, CHUNK * CHUNK)
    ii, jj = np.arange(CHUNK)[:, None], np.arange(CHUNK)[None, :]
    diag = (jnp.asarray((ii - jj + CHUNK - 1).reshape(-1, 1)) == jnp.arange(LANE)[None, :]).astype(F32)
    ic = np.arange(nq)[:, None, None]
    jc = np.arange(nk)[None, :, None]
    dl = np.arange(LANE)[None, None, :] - (CHUNK - 1)
    rel = np.clip(CHUNK * (ic - jc + (2 * QB) // CHUNK) + dl, -REL_CLIP, REL_CLIP) + REL_CLIP
    bins = (jnp.asarray(rel.reshape(-1, 1)) == jnp.arange(nbin)[None, :]).astype(F32)

    def onehot_mm(name, a, b):
        def body(a_ref, b_ref, o_ref):
            o_ref[...] = _dot_hi(a_ref[...], b_ref[...])
        return pl.pallas_call(body, name=name, out_shape=_sds((a.shape[0], b.shape[1]), F32),
                              compiler_params=pltpu.CompilerParams(vmem_limit_bytes=VMEM_LIMIT))(a, b)

    diags = onehot_mm("bias_diag_sums", blocks, diag)
    out = onehot_mm("bias_bin_sums", diags.reshape(h, nq * nk * LANE), bins)
    return out[:, :2 * REL_CLIP + 1]


def _tri(lower):
    r = lax.broadcasted_iota(jnp.int32, (CHUNK, CHUNK), 0)
    c = lax.broadcasted_iota(jnp.int32, (CHUNK, CHUNK), 1)
    return (r >= c) if lower else (r <= c)


def _lin_prep(gla, q, k, aux):
    dk = q.shape[-1]
    if gla:
        glr, wlr, blr = aux
        q = q * (dk ** -0.5)
        pre = _dot(glr, wlr) + blr
        log_a = (jnp.minimum(pre, 0.0) - jnp.log(1.0 + jnp.exp(-jnp.abs(pre)))) / GATE_NORM
        b = _dot_hi(_tri(True).astype(F32), log_a)
        return q, k, b, pre
    cs, sn, lg = aux
    half = dk // 2
    q = q * cs + pltpu.roll(q, half, 1) * sn
    k = (k * cs + pltpu.roll(k, half, 1) * sn) * (dk ** -0.5)
    pos = lax.broadcasted_iota(jnp.int32, (CHUNK, dk), 0).astype(F32) + 1.0
    return q, k, pos * lg, None


def _lin_chunk(q, k, v, b, st):
    eb, enb = jnp.exp(b), jnp.exp(-b)
    last = b[CHUNK - 1:CHUNK, :]
    qf, kf, qb, kb = q * eb, k * enb, q * enb, k * eb
    kl = k * jnp.exp(last - b)
    s = jnp.where(_tri(True), _dot_nt(qf, kf), _dot_nt(qb, kb))
    o = _dot(s, v) + _dot_nt(qf, st)
    st_new = st * jnp.exp(last) + _dot(v.T, kl)
    return o, st_new, (eb, enb, last, qf, kf, qb, kb, kl, s)


def _lin_norm_gate(gla, o, gate, gn):
    sg = _sigmoid(gate)
    silu = gate * sg
    if gla:
        r = lax.rsqrt(jnp.mean(o * o, axis=-1, keepdims=True) + LN_EPS)
        hn = o * r
        return silu * (hn * gn), (sg, silu, r, hn)
    mu = jnp.mean(o, axis=-1, keepdims=True)
    oc = o - mu
    r = lax.rsqrt(jnp.mean(oc * oc, axis=-1, keepdims=True) + LN_EPS)
    hn = oc * r
    return silu * hn, (sg, silu, r, hn)


def _lin_specs(gla, dk, dv, off, rev, nb):
    pre = "g" if gla else "r"
    qc, kc, vc, gc = (off[pre + "q"] // dk, off[pre + "k"] // dk, off[pre + "v"] // dv, off[pre + "g"] // dv)

    def blk(i):
        return nb - 1 - i if rev else i

    specs = [pl.BlockSpec((LB, dk), lambda h, i: (blk(i), qc + h)),
             pl.BlockSpec((LB, dk), lambda h, i: (blk(i), kc + h)),
             pl.BlockSpec((LB, dv), lambda h, i: (blk(i), vc + h)),
             pl.BlockSpec((LB, dv), lambda h, i: (blk(i), gc + h))]
    if gla:
        specs += [pl.BlockSpec((LB, LANE), lambda h, i: (blk(i), off["glr"] // LANE)),
                  pl.BlockSpec((LANE, dk), lambda h, i: (0, h)),
                  pl.BlockSpec((1, dk), lambda h, i: (0, h)),
                  pl.BlockSpec((1, dv), lambda h, i: (0, 0))]
    else:
        specs += [pl.BlockSpec((LB, dk), lambda h, i: (blk(i), 0)),
                  pl.BlockSpec((LB, dk), lambda h, i: (blk(i), 0)),
                  pl.BlockSpec((None, 1, dk), lambda h, i: (h, 0, 0))]
    return specs, blk


def _lin_aux(gla, refs, rows):
    if gla:
        glr_ref, wlr_ref, blr_ref, gn_ref = refs
        return (glr_ref[rows, :], wlr_ref[...], blr_ref[...]), gn_ref[...]
    cs_ref, sn_ref, lg_ref = refs
    return (cs_ref[rows, :], sn_ref[rows, :], lg_ref[...]), None


def _lin_fwd(gla, p, aux_arrays, d, off):
    t = p.shape[0]
    dk, dv = d // (2 * LIN_HEADS), d // LIN_HEADS
    nb, cb = t // LB, LB // CHUNK
    naux = len(aux_arrays)

    def body(*refs):
        q_ref, k_ref, v_ref, g_ref = refs[:4]
        aux_refs = refs[4:4 + naux]
        o_ref, bo_ref, st_out_ref, st_ref = refs[4 + naux:]

        @pl.when(pl.program_id(1) == 0)
        def _():
            st_ref[...] = jnp.zeros_like(st_ref)

        for c in range(cb):
            rows = pl.ds(c * CHUNK, CHUNK)
            aux, gn = _lin_aux(gla, aux_refs, rows)
            q, k, b, _ = _lin_prep(gla, q_ref[rows, :], k_ref[rows, :], aux)
            st = st_ref[...]
            st_out_ref[c] = st
            o, st_new, _ = _lin_chunk(q, k, v_ref[rows, :], b, st)
            st_ref[...] = st_new
            o_ref[rows, :] = o
            out, _ = _lin_norm_gate(gla, o, g_ref[rows, :], gn)
            bo_ref[rows, :] = out.astype(BF16)

    specs, _ = _lin_specs(gla, dk, dv, off, False, nb)
    orow = pl.BlockSpec((LB, dv), lambda h, i: (i, h))
    return pl.pallas_call(
        body, name="gla_fwd" if gla else "ret_fwd", grid=(LIN_HEADS, nb), in_specs=specs,
        out_specs=[orow, orow, pl.BlockSpec((None, cb, dv, dk), lambda h, i: (h, i, 0, 0))],
        out_shape=[_sds((t, d), F32), _sds((t, d), BF16), _sds((LIN_HEADS, t // CHUNK, dv, dk), F32)],
        scratch_shapes=[pltpu.VMEM((dv, dk), F32)],
        compiler_params=_cp(("parallel", "arbitrary")))(p, p, p, p, *aux_arrays)


def _lin_bwd(gla, p, aux_arrays, o, states, dbo, d, off):
    t = p.shape[0]
    dk, dv = d // (2 * LIN_HEADS), d // LIN_HEADS
    nb, cb = t // LB, LB // CHUNK
    naux = len(aux_arrays)

    def body(*refs):
        q_ref, k_ref, v_ref, g_ref = refs[:4]
        aux_refs = refs[4:4 + naux]
        o_ref, st_in_ref, dbo_ref = refs[4 + naux:7 + naux]
        outs = refs[7 + naux:]
        dq_ref, dk_ref, dv_ref, dg_ref = outs[:4]
        dst_ref = outs[-1]
        first = pl.program_id(1) == 0

        @pl.when(first)
        def _():
            dst_ref[...] = jnp.zeros_like(dst_ref)

        if gla:
            dpre_ref, dblr_ref, dgn_ref = outs[4:7]

            @pl.when(first)
            def _():
                dblr_ref[...] = jnp.zeros_like(dblr_ref)
                dgn_ref[...] = jnp.zeros_like(dgn_ref)

        for c in reversed(range(cb)):
            rows = pl.ds(c * CHUNK, CHUNK)
            aux, gn = _lin_aux(gla, aux_refs, rows)
            q, k, b, pre = _lin_prep(gla, q_ref[rows, :], k_ref[rows, :], aux)
            v = v_ref[rows, :]
            st = st_in_ref[c]
            _, _, (eb, enb, last, qf, kf, qb, kb, kl, s) = _lin_chunk(q, k, v, b, st)
            gate = g_ref[rows, :]
            dout = dbo_ref[rows, :]
            _, (sg, silu, r, hn) = _lin_norm_gate(gla, o_ref[rows, :], gate, gn)
            dsilu = sg * (1.0 + gate * (1.0 - sg))
            if gla:
                y = hn * gn
                dy = dout * silu
                dg_ref[rows, :] = (dout * y * dsilu).astype(BF16)
                dgn_ref[...] += jnp.sum(dy * hn, axis=0, keepdims=True)
                dhn = dy * gn
                do = r * (dhn - hn * jnp.mean(dhn * hn, axis=-1, keepdims=True))
            else:
                dhn = dout * silu
                dg_ref[rows, :] = (dout * hn * dsilu).astype(BF16)
                do = r * (dhn - jnp.mean(dhn, axis=-1, keepdims=True)
                          - hn * jnp.mean(dhn * hn, axis=-1, keepdims=True))
            dstn = dst_ref[...]
            dec = jnp.exp(last)
            ds = _dot_nt(do, v)
            low = _tri(True)
            dsf = jnp.where(low, ds, 0.0)
            dsb = jnp.where(low, 0.0, ds)
            dvv = _dot(s.T, do) + _dot_nt(kl, dstn)
            dqf = _dot(dsf, kf) + _dot(do, st)
            dkf = _dot(dsf.T, qf)
            dqb = _dot(dsb, kb)
            dkb = _dot(dsb.T, qb)
            dkl = _dot(v, dstn)
            dst_ref[...] = dstn * dec + _dot(do.T, qf)
            dq = dqf * eb + dqb * enb
            dkk = dkf * enb + dkb * eb + dkl * jnp.exp(last - b)
            dv_ref[rows, :] = dvv.astype(BF16)
            if gla:
                ddec = jnp.sum(dstn * st, axis=0, keepdims=True)
                db = dqf * qf - dkf * kf - dqb * qb + dkb * kb - dkl * kl
                dlast = jnp.sum(dkl * kl, axis=0, keepdims=True) + ddec * dec
                rowi = lax.broadcasted_iota(jnp.int32, db.shape, 0)
                db = db + jnp.where(rowi == CHUNK - 1, dlast, 0.0)
                dlog_a = _dot_hi(_tri(False).astype(F32), db)
                dpre = dlog_a * (1.0 / GATE_NORM) * (1.0 - _sigmoid(pre))
                dpre_ref[rows, :] = dpre
                dblr_ref[...] += jnp.sum(dpre, axis=0, keepdims=True)
                dq_ref[rows, :] = (dq * (dk ** -0.5)).astype(BF16)
                dk_ref[rows, :] = dkk.astype(BF16)
            else:
                cs, sn, _ = aux
                half = dk // 2
                dkk = dkk * (dk ** -0.5)
                dq_ref[rows, :] = (dq * cs + pltpu.roll(dq * sn, half, 1)).astype(BF16)
                dk_ref[rows, :] = (dkk * cs + pltpu.roll(dkk * sn, half, 1)).astype(BF16)

    specs, blk = _lin_specs(gla, dk, dv, off, True, nb)
    vrow = pl.BlockSpec((LB, dv), lambda h, i: (blk(i), h))
    krow = pl.BlockSpec((LB, dk), lambda h, i: (blk(i), h))
    specs += [vrow, pl.BlockSpec((None, cb, dv, dk), lambda h, i: (h, blk(i), 0, 0)), vrow]
    out_specs = [krow, krow, vrow, vrow]
    out_shape = [_sds((t, d // 2), BF16), _sds((t, d // 2), BF16), _sds((t, d), BF16), _sds((t, d), BF16)]
    if gla:
        out_specs += [krow, pl.BlockSpec((None, 1, dk), lambda h, i: (h, 0, 0)),
                      pl.BlockSpec((None, 1, dv), lambda h, i: (h, 0, 0))]
        out_shape += [_sds((t, d // 2), F32), _sds((LIN_HEADS, 1, dk), F32), _sds((LIN_HEADS, 1, dv), F32)]
    out_specs.append(pl.BlockSpec((None, dv, dk), lambda h, i: (h, 0, 0)))
    out_shape.append(_sds((LIN_HEADS, dv, dk), F32))
    res = pl.pallas_call(
        body, name="gla_bwd" if gla else "ret_bwd", grid=(LIN_HEADS, nb), in_specs=specs,
        out_specs=out_specs, out_shape=out_shape,
        compiler_params=_cp(("parallel", "arbitrary")))(p, p, p, p, *aux_arrays, o, states, dbo)
    return res[:-1]


def _row_tile(rows, cols):
    cap = max(8, (2 << 20) // (4 * cols))
    t = rows
    while t > cap and t % 2 == 0:
        t //= 2
    return t


def _add_selected(name, g, r, sel):
    _, rows, cols = g.shape
    tr = _row_tile(rows, cols)

    def body(sel_ref, g_ref, r_ref, o_ref):
        o_ref[...] = g_ref[...] + r_ref[...]

    gs = pltpu.PrefetchScalarGridSpec(
        num_scalar_prefetch=1, grid=(rows // tr,),
        in_specs=[pl.BlockSpec((None, tr, cols), lambda i, s: (s[0], i, 0)),
                  pl.BlockSpec((tr, cols), lambda i, s: (i, 0))],
        out_specs=pl.BlockSpec((tr, cols), lambda i, s: (i, 0)))
    return pl.pallas_call(body, name=name, grid_spec=gs, out_shape=_sds((rows, cols), F32),
                          compiler_params=_cp(("parallel",)))(sel, g, r)


def _sum_shards(name, h, rcv, sel):
    _, rows, cols = h.shape
    tr = _row_tile(rows, cols)

    def body(sel_ref, h_ref, r0, r1, r2, o_ref):
        o_ref[...] = ((h_ref[...] + r0[...]) + r1[...]) + r2[...]

    rspecs = [pl.BlockSpec((None, tr, cols), functools.partial(lambda i, s, j: (j, i, 0), j=j)) for j in range(3)]
    gs = pltpu.PrefetchScalarGridSpec(
        num_scalar_prefetch=1, grid=(rows // tr,),
        in_specs=[pl.BlockSpec((None, tr, cols), lambda i, s: (s[0], i, 0))] + rspecs,
        out_specs=pl.BlockSpec((tr, cols), lambda i, s: (i, 0)))
    return pl.pallas_call(body, name=name, grid_spec=gs, out_shape=_sds((rows, cols), F32),
                          compiler_params=_cp(("parallel",)))(sel, h, rcv, rcv, rcv)


def _adamw(name, w, g, m, v):
    rows, cols = w.shape
    tr = _row_tile(rows, cols)
    c1 = 1.0 - ADAM_B1 ** ADAM_STEP
    c2 = 1.0 - ADAM_B2 ** ADAM_STEP

    def body(w_ref, g_ref, m_ref, v_ref, d_ref, nm_ref, nv_ref):
        gv = g_ref[...]
        nm = ADAM_B1 * m_ref[...] + (1.0 - ADAM_B1) * gv
        nv = ADAM_B2 * v_ref[...] + (1.0 - ADAM_B2) * jnp.square(gv)
        d_ref[...] = -ADAM_LR * ((nm / c1) / (jnp.sqrt(nv / c2) + ADAM_EPS) + ADAM_WD * w_ref[...])
        nm_ref[...] = nm
        nv_ref[...] = nv

    spec = pl.BlockSpec((tr, cols), lambda i: (i, 0))
    return pl.pallas_call(body, name=name, grid=(rows // tr,), in_specs=[spec] * 4, out_specs=[spec] * 3,
                          out_shape=[_sds((rows, cols), F32)] * 3, compiler_params=_cp(("parallel",)))(w, g, m, v)


def _place():
    x, y, c = (lax.axis_index(a) for a in MESH_AXES)
    chips = [(1 - x, y), (x, 1 - y), (1 - x, 1 - y)]
    return x, y, c, chips


def _chip_index(xy):
    return 2 * xy[0] + xy[1]


ANY = pl.BlockSpec(memory_space=pl.ANY)


def _allgather_weights(ws):
    n = len(ws)

    def body(*refs):
        ins, outs = refs[:n], refs[n:2 * n]
        send, recv, lsem = refs[2 * n:]
        x, y, c, chips = _place()
        me = _chip_index((x, y))

        def rcopy(a, k, src, dst, to):
            return pltpu.make_async_remote_copy(src_ref=src, dst_ref=dst, send_sem=send.at[a * 6 + k],
                                                recv_sem=recv.at[a * 6 + k], device_id=to, device_id_type=DEV)

        local = [pltpu.make_async_copy(ins[a], outs[a].at[me], lsem.at[a]) for a in range(n)]
        for cp in local:
            cp.start()
        first = [rcopy(a, j, ins[a].at[c], outs[a].at[me, c], (*ch, c)) for a in range(n) for j, ch in enumerate(chips)]
        for cp in first:
            cp.start()
        passed = []
        for a in range(n):
            for j, ch in enumerate(chips):
                slot = outs[a].at[_chip_index(ch), c]
                rcopy(a, j, slot, slot, (x, y, c)).wait_recv()
                fw = rcopy(a, 3 + j, slot, slot, (x, y, 1 - c))
                fw.start()
                passed.append(fw)
        for a in range(n):
            for j, ch in enumerate(chips):
                slot = outs[a].at[_chip_index(ch), 1 - c]
                rcopy(a, 3 + j, slot, slot, (x, y, c)).wait_recv()
        for cp in first + passed:
            cp.wait_send()
        for cp in local:
            cp.wait()

    return pl.pallas_call(
        body, name="allgather_weights", in_specs=[ANY] * n, out_specs=[ANY] * n,
        out_shape=[_sds((4,) + w.shape, w.dtype) for w in ws],
        scratch_shapes=[pltpu.SemaphoreType.DMA((6 * n,)), pltpu.SemaphoreType.DMA((6 * n,)),
                        pltpu.SemaphoreType.DMA((n,))])(*ws)


def _sibling_send(name, srcs, pick, shapes):
    n = len(srcs)

    def body(*refs):
        ins, outs = refs[:n], refs[n:2 * n]
        send, recv = refs[2 * n:]
        x, y, c, _ = _place()
        cps = [pltpu.make_async_remote_copy(src_ref=pick(ins[a], c), dst_ref=outs[a], send_sem=send.at[a],
                                            recv_sem=recv.at[a], device_id=(x, y, 1 - c), device_id_type=DEV)
               for a in range(n)]
        for cp in cps:
            cp.start()
        for cp in cps:
            cp.wait()

    return pl.pallas_call(
        body, name=name, in_specs=[ANY] * n, out_specs=[ANY] * n,
        out_shape=[_sds(s, F32) for s in shapes],
        scratch_shapes=[pltpu.SemaphoreType.DMA((n,)), pltpu.SemaphoreType.DMA((n,))])(*srcs)


def _chip_scatter(hs):
    n = len(hs)

    def body(*refs):
        ins, outs = refs[:n], refs[n:2 * n]
        send, recv = refs[2 * n:]
        x, y, c, chips = _place()
        cps = [pltpu.make_async_remote_copy(src_ref=ins[a].at[_chip_index(ch)], dst_ref=outs[a].at[j],
                                            send_sem=send.at[a * 3 + j], recv_sem=recv.at[a * 3 + j],
                                            device_id=(*ch, c), device_id_type=DEV)
               for a in range(n) for j, ch in enumerate(chips)]
        for cp in cps:
            cp.start()
        for cp in cps:
            cp.wait()

    return pl.pallas_call(
        body, name="grad_chip_scatter", in_specs=[ANY] * n, out_specs=[ANY] * n,
        out_shape=[_sds((3,) + h.shape[1:], h.dtype) for h in hs],
        scratch_shapes=[pltpu.SemaphoreType.DMA((3 * n,)), pltpu.SemaphoreType.DMA((3 * n,))])(*hs)


def _small_allreduce(v):
    rows = v.shape[0]
    ndev = 8

    def body(v_ref, o_ref, gat_ref, send, recv):
        x, y, c, _ = _place()
        me = 4 * x + 2 * y + c
        cps = []
        for k in range(1, ndev):
            to = (me + k) % ndev
            cp = pltpu.make_async_remote_copy(src_ref=v_ref, dst_ref=gat_ref.at[me], send_sem=send.at[k - 1],
                                              recv_sem=recv.at[me], device_id=(to // 4, (to // 2) % 2, to % 2),
                                              device_id_type=DEV)
            cp.start()
            cps.append(cp)
        gat_ref[me] = v_ref[...]
        for k in range(1, ndev):
            frm = (me + k) % ndev
            pltpu.make_async_remote_copy(src_ref=v_ref, dst_ref=gat_ref.at[frm], send_sem=send.at[k - 1],
                                         recv_sem=recv.at[frm], device_id=(x, y, c), device_id_type=DEV).wait_recv()
        for cp in cps:
            cp.wait_send()
        acc = gat_ref[0]
        for k in range(1, ndev):
            acc = acc + gat_ref[k]
        o_ref[...] = acc

    vm = pl.BlockSpec(memory_space=pltpu.VMEM)
    return pl.pallas_call(
        body, name="small_allreduce", in_specs=[vm], out_specs=vm, out_shape=_sds((rows, LANE), F32),
        scratch_shapes=[pltpu.VMEM((ndev, rows, LANE), F32), pltpu.SemaphoreType.DMA((ndev - 1,)),
                        pltpu.SemaphoreType.DMA((ndev,))])(v)


def _layout(d):
    half = d // 2
    names = [("aq", d), ("ak", d), ("av", d), ("rq", half), ("rk", half), ("rv", d), ("rg", d),
             ("gq", half), ("gk", half), ("gv", d), ("gg", d), ("gates", 3 * d), ("glr", 2 * LANE)]
    off, pos = {}, 0
    for nm, sz in names:
        off[nm] = pos
        pos += sz
    return off, pos


def _pad_cols(w, d):
    a = 8 * d + d
    lr = w[..., a:a + GATE_RANK]
    z = jnp.zeros(w.shape[:-1] + (2 * LANE - GATE_RANK,), w.dtype)
    return jnp.concatenate([w[..., :a], w[..., a + GATE_RANK:], lr, z], axis=-1)


def _unpad_cols(g, d):
    a = 8 * d + d
    return jnp.concatenate([g[..., :a], g[..., a + 3 * d:a + 3 * d + GATE_RANK], g[..., a:a + 3 * d]], axis=-1)


def kernel(x, ln_in_g, ln_in_b, w_in, rel_bias, gla_w_lr, gla_b_lr, gla_norm_g, w_branch, w_out, ln1_g, ln1_b, w_up, w_down, ln2_g, ln2_b, loss_target, m_ln_in_g, m_ln_in_b, m_w_in, m_rel_bias, m_gla_w_lr, m_gla_b_lr, m_gla_norm_g, m_w_branch, m_w_out, m_ln1_g, m_ln1_b, m_w_up, m_w_down, m_ln2_g, m_ln2_b, v_ln_in_g, v_ln_in_b, v_w_in, v_rel_bias, v_gla_w_lr, v_gla_b_lr, v_gla_norm_g, v_w_branch, v_w_out, v_ln1_g, v_ln1_b, v_w_up, v_w_down, v_ln2_g, v_ln2_b):
    t, d = x.shape[1], x.shape[2]
    dff = 4 * d
    half = d // 2
    off, npad = _layout(d)
    xi, yi, ci = (lax.axis_index(a) for a in MESH_AXES)
    chip = 2 * xi + yi
    csel = jnp.reshape(ci, (1,)).astype(jnp.int32)
    psel = jnp.reshape(chip, (1,)).astype(jnp.int32)

    big = [w_in, w_branch, w_out, w_up, w_down]
    g_in, g_br, g_out, g_up, g_down = _allgather_weights([w.astype(BF16) for w in big])
    win = [_pad_cols(jnp.transpose(g_in[:, l], (1, 0, 2)).reshape(d, -1), d) for l in range(DEPTH)]
    wbr = [jnp.transpose(g_br[:, l], (1, 0, 2, 3)).reshape(N_BRANCH, d, d) for l in range(DEPTH)]
    wout = [g_out[:, l].reshape(d, d) for l in range(DEPTH)]
    wup = [jnp.transpose(g_up[:, l], (1, 0, 2)).reshape(d, dff) for l in range(DEPTH)]
    wdown = [g_down[:, l].reshape(dff, d) for l in range(DEPTH)]

    dkh = half // LIN_HEADS
    lr_rows = DEPTH * GATE_RANK
    lr_slab = jnp.zeros((lr_rows, 4, half // 4), F32)
    lr_slab = lax.dynamic_update_slice(lr_slab, (gla_w_lr.reshape(lr_rows, 1, half // 4) * jnp.where(ci == 0, 1.0, 0.0)),
                                       (0, chip, 0))
    wlr_full = _small_allreduce(lr_slab.reshape(-1, LANE)).reshape(DEPTH, GATE_RANK, half)
    wlr_pad = jnp.concatenate([wlr_full, jnp.zeros((DEPTH, LANE - GATE_RANK, half), F32)], axis=1)

    inv = 10000.0 ** (-jnp.arange(0, dkh, 2, dtype=F32) / dkh)
    ang = jnp.arange(t, dtype=F32)[:, None] * inv[None, :]
    cos, sin = jnp.cos(ang), jnp.sin(ang)
    rope_c = jnp.concatenate([cos, cos], axis=1)
    rope_s = jnp.concatenate([-sin, sin], axis=1)
    log_gamma = jnp.log1p(-jnp.exp2(-5.0 - jnp.arange(LIN_HEADS, dtype=F32)))
    lg_tab = jnp.broadcast_to(log_gamma[:, None, None], (LIN_HEADS, 1, dkh))

    def vec(a):
        return a.reshape(1, -1)

    x0, x0b, xh_in, rs_in = _ln_in(x[0], vec(ln_in_g), vec(ln_in_b))
    saved = []
    xl, xlb = x0, x0b
    for l in range(DEPTH):
        p = _mm("proj_in", xlb, win[l], 512, 1792)
        bias = _bias_expand(rel_bias[l])
        attn = _attn_fwd(p, bias, d, off)
        ret_aux = (rope_c, rope_s, lg_tab)
        gla_aux = (p, wlr_pad[l], vec(gla_b_lr[l]), vec(gla_norm_g[l]))
        o_ret, b_ret, st_ret = _lin_fwd(False, p, ret_aux, d, off)
        o_gla, b_gla, st_gla = _lin_fwd(True, p, gla_aux, d, off)
        bo = jnp.stack([attn, b_ret, b_gla])
        proj, merged = _merge_fwd(bo, wbr[l], p, off["gates"])
        x1, x1b, xh1, rs1 = _mm_res_ln("out_proj_ln", merged, wout[l], xl, vec(ln1_g[l]), vec(ln1_b[l]), 256, False)
        u = _mm("mlp_up", x1b, wup[l], 512, 1024)
        x2, x2b, xh2, rs2, act = _mm_res_ln("mlp_down_ln", u, wdown[l], x1, vec(ln2_g[l]), vec(ln2_b[l]), 256, True)
        saved.append(dict(xlb=xlb, p=p, bias=bias, ret_aux=ret_aux, gla_aux=gla_aux, o_ret=o_ret, o_gla=o_gla,
                          st_ret=st_ret, st_gla=st_gla, bo=bo, proj=proj, merged=merged, x1b=x1b, xh1=xh1,
                          rs1=rs1, u=u, xh2=xh2, rs2=rs2, act=act))
        xl, xlb = x2, x2b

    small = {}
    big_grads = [None] * DEPTH
    last = saved[-1]
    loss_p, dz2, dz2b, dg, db = _loss_ln_bwd(xl, loss_target[0], last["xh2"], last["rs2"], vec(ln2_g[DEPTH - 1]))
    small["loss"] = loss_p[:, :1]
    grad_x = None
    for l in reversed(range(DEPTH)):
        s = saved[l]
        small[("ln2_g", l)], small[("ln2_b", l)] = dg, db
        du = _mm_nt_relu2_bwd(dz2b, wdown[l], s["u"])
        g_wdown = _mm("grad_w_down", s["act"].T, dz2b, 1024, 512)
        g_wup = _mm("grad_w_up", s["x1b"].T, du, 1024, 512)
        dz1, dz1b, dg1, db1 = _mm_nt_res_lnbwd("mlp_up_bwd_ln", du, wup[l], dz2, s["xh1"], s["rs1"],
                                               vec(ln1_g[l]), 256, dff)
        small[("ln1_g", l)], small[("ln1_b", l)] = dg1, db1
        dproj, dgl = _merge_bwd(dz1b, wout[l], s["proj"], s["p"], off["gates"])
        g_wout = _mm("grad_w_out", s["merged"].T, dz1b, 1024, 512)
        dbo = _mm("branch_proj_bwd", dproj, wbr[l], 512, 512, nt=True)
        g_wbr = _mm("grad_w_branch", jnp.transpose(s["bo"], (0, 2, 1)), dproj, 1024, 512)
        dq_a, dk_acc, dv_acc, dbias = _attn_bwd(s["p"], s["bias"], dbo[0], d, off)
        small[("rel_bias", l)] = _bias_reduce(dbias)
        dk_a = jnp.transpose(dk_acc[:, 2 * QB:], (1, 0, 2)).reshape(t, d).astype(BF16)
        dv_a = jnp.transpose(dv_acc[:, 2 * QB:], (1, 0, 2)).reshape(t, d).astype(BF16)
        dq_r, dk_r, dv_r, dg_r = _lin_bwd(False, s["p"], s["ret_aux"], s["o_ret"], s["st_ret"], dbo[1], d, off)
        dq_g, dk_g, dv_g, dg_g, dpre, dblr, dgn = _lin_bwd(True, s["p"], s["gla_aux"], s["o_gla"], s["st_gla"],
                                                           dbo[2], d, off)
        small[("gla_b_lr", l)] = dblr.reshape(1, half)
        small[("gla_norm_g", l)] = jnp.sum(dgn, axis=0)
        dpre_b = dpre.astype(BF16)
        glr_b = s["p"][:, off["glr"]:off["glr"] + LANE].astype(BF16)
        dglr = _mm("gate_lr_bwd", dpre_b, wlr_pad[l], 512, LANE, nt=True, out_dtype=BF16)
        small[("gla_w_lr", l)] = _mm("grad_gla_w_lr", glr_b.T, dpre_b, LANE, half)[:GATE_RANK]
        dp = jnp.concatenate([dq_a, dk_a, dv_a, dq_r, dk_r, dv_r, dg_r, dq_g, dk_g, dv_g, dg_g,
                              dgl[0], dgl[1], dgl[2], dglr, jnp.zeros((t, LANE), BF16)], axis=1)
        g_win = _mm("grad_w_in", s["xlb"].T, dp, 1024, 896)
        if l > 0:
            prev = saved[l - 1]
            xh_p, rs_p, g_p = prev["xh2"], prev["rs2"], vec(ln2_g[l - 1])
        else:
            xh_p, rs_p, g_p = xh_in, rs_in, vec(ln_in_g)
        dzp, dzpb, dg, db = _mm_nt_res_lnbwd("proj_in_bwd_ln", dp, win[l], dz1, xh_p, rs_p, g_p, 512, 1792)
        big_grads[l] = (g_win, g_wbr, g_wout, g_wup, g_wdown)
        dz2, dz2b = dzp, dzpb
        grad_x = dzp
    small["ln_in_g"], small["ln_in_b"] = dg, db

    def to_shards(gs):
        g_win, g_wbr, g_wout, g_wup, g_wdown = gs
        a = _unpad_cols(g_win, d).reshape(d, 4, -1).transpose(1, 0, 2)
        b = g_wbr.reshape(N_BRANCH, 4, d // 4, d).transpose(1, 0, 2, 3).reshape(4, -1, d)
        cc = g_wout.reshape(4, d // 4, d)
        e = g_wup.reshape(d, 4, d).transpose(1, 0, 2)
        f = g_wdown.reshape(4, d, d)
        return [a, b, cc, e, f]

    per_layer = [to_shards(big_grads[l]) for l in range(DEPTH)]
    gfull = [jnp.stack([per_layer[l][a] for l in range(DEPTH)]) for a in range(5)]
    flat = [g.reshape(DEPTH, -1, g.shape[-1]) for g in gfull]
    theirs = _sibling_send("grad_sibling_send", flat, lambda r, c: r.at[1 - c], [g.shape[1:] for g in flat])
    hsum = [_add_selected("grad_sibling_add", flat[a], theirs[a], csel) for a in range(5)]
    hs = [h.reshape((4, -1, h.shape[-1])) for h in hsum]
    rcv = _chip_scatter(hs)
    mine = [_sum_shards("grad_chip_sum", hs[a], rcv[a], psel) for a in range(5)]
    other = _sibling_send("grad_sibling_share", mine, lambda r, c: r, [g.shape for g in mine])
    grads_big = [jnp.where(ci == 0, jnp.stack([mine[a], other[a]]), jnp.stack([other[a], mine[a]]))
                 for a in range(5)]

    rb_pad = 3 * LANE
    pieces = [small["loss"].reshape(-1), jnp.zeros((LANE - 1,), F32), small["ln_in_g"].reshape(-1),
              small["ln_in_b"].reshape(-1)]
    for l in range(DEPTH):
        rb = jnp.pad(small[("rel_bias", l)], ((0, 0), (0, rb_pad - (2 * REL_CLIP + 1))))
        pieces += [rb.reshape(-1), small[("gla_w_lr", l)].reshape(-1), small[("gla_b_lr", l)].reshape(-1),
                   small[("gla_norm_g", l)].reshape(-1), small[("ln1_g", l)].reshape(-1),
                   small[("ln1_b", l)].reshape(-1), small[("ln2_g", l)].reshape(-1), small[("ln2_b", l)].reshape(-1)]
    sizes = [pc.shape[0] for pc in pieces]
    packed = jnp.concatenate(pieces)
    padn = (-packed.shape[0]) % (8 * LANE)
    packed = jnp.concatenate([packed, jnp.zeros((padn,), F32)]).reshape(-1, LANE)
    red = _small_allreduce(packed).reshape(-1)
    parts, pos = [], 0
    for sz in sizes:
        parts.append(red[pos:pos + sz])
        pos += sz
    loss = parts[0][0]
    g_ln_in_g, g_ln_in_b = parts[2], parts[3]
    per = 8
    g_rel = jnp.stack([parts[4 + per * l].reshape(ATTN_HEADS, rb_pad)[:, :2 * REL_CLIP + 1] for l in range(DEPTH)])
    g_wlr_full = jnp.stack([parts[5 + per * l].reshape(GATE_RANK, half) for l in range(DEPTH)])
    g_wlr = lax.dynamic_slice_in_dim(g_wlr_full, chip * (half // 4), half // 4, axis=2)
    g_blr = jnp.stack([parts[6 + per * l] for l in range(DEPTH)])
    g_gn = jnp.stack([parts[7 + per * l] for l in range(DEPTH)])
    g_ln1g = jnp.stack([parts[8 + per * l] for l in range(DEPTH)])
    g_ln1b = jnp.stack([parts[9 + per * l] for l in range(DEPTH)])
    g_ln2g = jnp.stack([parts[10 + per * l] for l in range(DEPTH)])
    g_ln2b = jnp.stack([parts[11 + per * l] for l in range(DEPTH)])

    g_w_in = grads_big[0].reshape(w_in.shape)
    g_w_branch = grads_big[1].reshape(w_branch.shape)
    g_w_out = grads_big[2].reshape(w_out.shape)
    g_w_up = grads_big[3].reshape(w_up.shape)
    g_w_down = grads_big[4].reshape(w_down.shape)

    grads = [g_ln_in_g, g_ln_in_b, g_w_in, g_rel, g_wlr, g_blr, g_gn, g_w_branch, g_w_out, g_ln1g, g_ln1b,
             g_w_up, g_w_down, g_ln2g, g_ln2b]
    ws = [ln_in_g, ln_in_b, w_in, rel_bias, gla_w_lr, gla_b_lr, gla_norm_g, w_branch, w_out, ln1_g, ln1_b,
          w_up, w_down, ln2_g, ln2_b]
    ms = [m_ln_in_g, m_ln_in_b, m_w_in, m_rel_bias, m_gla_w_lr, m_gla_b_lr, m_gla_norm_g, m_w_branch, m_w_out,
          m_ln1_g, m_ln1_b, m_w_up, m_w_down, m_ln2_g, m_ln2_b]
    vs = [v_ln_in_g, v_ln_in_b, v_w_in, v_rel_bias, v_gla_w_lr, v_gla_b_lr, v_gla_norm_g, v_w_branch, v_w_out,
          v_ln1_g, v_ln1_b, v_w_up, v_w_down, v_ln2_g, v_ln2_b]

    deltas, new_ms, new_vs = [None] * 15, [None] * 15, [None] * 15
    big_idx = [2, 7, 8, 11, 12]
    for i in big_idx:
        shp = ws[i].shape
        two = lambda a: a.reshape(-1, shp[-1])
        dl, nm, nv = _adamw("adamw_large", two(ws[i]), two(grads[i]), two(ms[i]), two(vs[i]))
        deltas[i], new_ms[i], new_vs[i] = dl.reshape(shp), nm.reshape(shp), nv.reshape(shp)
    small_idx = [i for i in range(15) if i not in big_idx]

    def pack(arrs):
        flat_ = jnp.concatenate([arrs[i].reshape(-1) for i in small_idx])
        pad_ = (-flat_.shape[0]) % (8 * LANE)
        return jnp.concatenate([flat_, jnp.ones((pad_,), F32)]).reshape(-1, LANE)

    dl, nm, nv = _adamw("adamw_small", pack(ws), pack(grads), pack(ms), pack(vs))
    pos = 0
    for i in small_idx:
        sz = int(np.prod(ws[i].shape))
        deltas[i] = dl.reshape(-1)[pos:pos + sz].reshape(ws[i].shape)
        new_ms[i] = nm.reshape(-1)[pos:pos + sz].reshape(ws[i].shape)
        new_vs[i] = nv.reshape(-1)[pos:pos + sz].reshape(ws[i].shape)
        pos += sz

    return (loss, grad_x[None], *grads, *deltas, *new_ms, *new_vs)
```

```python
import functools

import numpy as np
import jax
import jax.numpy as jnp
from jax import lax
from jax.experimental import pallas as pl
from jax.experimental.pallas import tpu as pltpu

F32 = jnp.float32
BF16 = jnp.bfloat16
MXU_DTYPE = BF16
HI = lax.Precision.HIGHEST

DEPTH = 2
CHUNK = 64
N_BRANCH = 3
ATTN_HEADS = 8
ATTN_LEFT = 8
REL_CLIP = 2 * CHUNK
LIN_HEADS = 4
GATE_RANK = 16
GATE_NORM = 16.0
LN_EPS = 1e-5
NEG_INF = -1e30
ALPHA = (2 * DEPTH) ** 0.25
ADAM_LR, ADAM_B1, ADAM_B2, ADAM_EPS, ADAM_WD, ADAM_STEP = 0.001, 0.9, 0.999, 1e-08, 0.01, 10

LANE = 128
VMEM_LIMIT = 56 << 20
QB = 256
KW = 3 * QB
LB = 256
MESH_AXES = ("x", "y", "c")
DEV = pl.DeviceIdType.MESH


def _cp(sem):
    return pltpu.CompilerParams(dimension_semantics=sem, vmem_limit_bytes=VMEM_LIMIT)


def _mx(v):
    return v.astype(MXU_DTYPE)


def _dot(a, b):
    return jnp.dot(_mx(a), _mx(b), preferred_element_type=F32)


def _dot_nt(a, b):
    return lax.dot_general(_mx(a), _mx(b), (((1,), (1,)), ((), ())), preferred_element_type=F32)


def _dot_hi(a, b):
    return jnp.dot(a, b, precision=HI, preferred_element_type=F32)


def _sigmoid(v):
    return 1.0 / (1.0 + jnp.exp(-v))


def _sds(shape, dtype):
    return jax.ShapeDtypeStruct(shape, dtype)


def _mm(name, a, b, tm, tn, nt=False, out_dtype=F32):
    batched = a.ndim == 3
    m, k = a.shape[-2:]
    n = b.shape[-2] if nt else b.shape[-1]
    tm, tn = min(tm, m), min(tn, n)

    def body(a_ref, b_ref, o_ref):
        f = _dot_nt if nt else _dot
        o_ref[...] = f(a_ref[...], b_ref[...]).astype(o_ref.dtype)

    if batched:
        nb = a.shape[0]
        grid = (nb, m // tm, n // tn)
        a_spec = pl.BlockSpec((None, tm, k), lambda g, i, j: (g, i, 0))
        b_spec = (pl.BlockSpec((None, tn, k), lambda g, i, j: (g, j, 0)) if nt
                  else pl.BlockSpec((None, k, tn), lambda g, i, j: (g, 0, j)))
        o_spec = pl.BlockSpec((None, tm, tn), lambda g, i, j: (g, i, j))
        out_shape = _sds((nb, m, n), out_dtype)
        sem = ("parallel", "parallel", "parallel")
    else:
        grid = (m // tm, n // tn)
        a_spec = pl.BlockSpec((tm, k), lambda i, j: (i, 0))
        b_spec = (pl.BlockSpec((tn, k), lambda i, j: (j, 0)) if nt
                  else pl.BlockSpec((k, tn), lambda i, j: (0, j)))
        o_spec = pl.BlockSpec((tm, tn), lambda i, j: (i, j))
        out_shape = _sds((m, n), out_dtype)
        sem = ("parallel", "parallel")
    return pl.pallas_call(body, name=name, grid=grid, in_specs=[a_spec, b_spec], out_specs=o_spec,
                          out_shape=out_shape, compiler_params=_cp(sem))(a, b)


def _ln_rows(y, g, b):
    mu = jnp.mean(y, axis=-1, keepdims=True)
    yc = y - mu
    var = jnp.mean(yc * yc, axis=-1, keepdims=True)
    rs = lax.rsqrt(var + LN_EPS)
    xh = yc * rs
    return xh * g + b, xh, rs


def _ln_in(x, g, b, tm=256):
    t, d = x.shape

    def body(x_ref, g_ref, b_ref, o_ref, ob_ref, xh_ref, rs_ref):
        o, xh, rs = _ln_rows(x_ref[...], g_ref[...], b_ref[...])
        o_ref[...] = o
        ob_ref[...] = o.astype(BF16)
        xh_ref[...] = xh
        rs_ref[...] = rs

    row = pl.BlockSpec((tm, d), lambda i: (i, 0))
    vec = pl.BlockSpec((1, d), lambda i: (0, 0))
    return pl.pallas_call(
        body, name="ln_in", grid=(t // tm,), in_specs=[row, vec, vec],
        out_specs=[row, row, row, pl.BlockSpec((tm, 1), lambda i: (i, 0))],
        out_shape=[_sds((t, d), F32), _sds((t, d), BF16), _sds((t, d), F32), _sds((t, 1), F32)],
        compiler_params=_cp(("parallel",)))(x, g, b)


def _mm_res_ln(name, a, w, res, g, b, tm, relu2):
    t, k = a.shape
    d = w.shape[1]

    def body(a_ref, w_ref, r_ref, g_ref, b_ref, o_ref, ob_ref, xh_ref, rs_ref, *act_ref):
        av = a_ref[...]
        if relu2:
            av = jnp.square(jnp.maximum(av, 0.0))
            act_ref[0][...] = av.astype(BF16)
        y = ALPHA * r_ref[...] + _dot(av, w_ref[...])
        o, xh, rs = _ln_rows(y, g_ref[...], b_ref[...])
        o_ref[...] = o
        ob_ref[...] = o.astype(BF16)
        xh_ref[...] = xh
        rs_ref[...] = rs

    row = pl.BlockSpec((tm, d), lambda i: (i, 0))
    vec = pl.BlockSpec((1, d), lambda i: (0, 0))
    arow = pl.BlockSpec((tm, k), lambda i: (i, 0))
    out_specs = [row, row, row, pl.BlockSpec((tm, 1), lambda i: (i, 0))]
    out_shape = [_sds((t, d), F32), _sds((t, d), BF16), _sds((t, d), F32), _sds((t, 1), F32)]
    if relu2:
        out_specs.append(arow)
        out_shape.append(_sds((t, k), BF16))
    return pl.pallas_call(
        body, name=name, grid=(t // tm,),
        in_specs=[arow, pl.BlockSpec((k, d), lambda i: (0, 0)), row, vec, vec],
        out_specs=out_specs, out_shape=out_shape, compiler_params=_cp(("parallel",)))(a, w, res, g, b)


def _merge_fwd(bo, wb, p, gate_off, tm=512, tn=512):
    _, t, d = bo.shape
    gb = gate_off // tn

    def body(bo_ref, wb_ref, g0, g1, g2, proj_ref, m_ref):
        acc = None
        for n, g_ref in enumerate((g0, g1, g2)):
            pr = _dot(bo_ref[n], wb_ref[n])
            proj_ref[n] = pr
            term = _sigmoid(g_ref[...]) * pr
            acc = term if acc is None else acc + term
        m_ref[...] = acc.astype(BF16)

    gspecs = [pl.BlockSpec((tm, tn), functools.partial(lambda i, j, n: (i, gb + n * (d // tn) + j), n=n))
              for n in range(3)]
    return pl.pallas_call(
        body, name="merge_fwd", grid=(t // tm, d // tn),
        in_specs=[pl.BlockSpec((3, tm, d), lambda i, j: (0, i, 0)),
                  pl.BlockSpec((3, d, tn), lambda i, j: (0, 0, j))] + gspecs,
        out_specs=[pl.BlockSpec((3, tm, tn), lambda i, j: (0, i, j)), pl.BlockSpec((tm, tn), lambda i, j: (i, j))],
        out_shape=[_sds((3, t, d), F32), _sds((t, d), BF16)],
        compiler_params=_cp(("parallel", "parallel")))(bo, wb, p, p, p)


def _merge_bwd(dz, wout, proj, p, gate_off, tm=512, tn=512):
    t, d = dz.shape
    gb = gate_off // tn

    def body(dz_ref, w_ref, proj_ref, g0, g1, g2, dproj_ref, dgl_ref):
        dm = _dot_nt(dz_ref[...], w_ref[...])
        for n, g_ref in enumerate((g0, g1, g2)):
            s = _sigmoid(g_ref[...])
            dproj_ref[n] = (dm * s).astype(BF16)
            dgl_ref[n] = (dm * proj_ref[n] * (s * (1.0 - s))).astype(BF16)

    gspecs = [pl.BlockSpec((tm, tn), functools.partial(lambda i, j, n: (i, gb + n * (d // tn) + j), n=n))
              for n in range(3)]
    dproj, dgl = pl.pallas_call(
        body, name="merge_bwd", grid=(t // tm, d // tn),
        in_specs=[pl.BlockSpec((tm, d), lambda i, j: (i, 0)), pl.BlockSpec((tn, d), lambda i, j: (j, 0)),
                  pl.BlockSpec((3, tm, tn), lambda i, j: (0, i, j))] + gspecs,
        out_specs=[pl.BlockSpec((3, tm, tn), lambda i, j: (0, i, j)),
                   pl.BlockSpec((3, tm, tn), lambda i, j: (0, i, j))],
        out_shape=[_sds((3, t, d), BF16), _sds((3, t, d), BF16)],
        compiler_params=_cp(("parallel", "parallel")))(dz, wout, proj, p, p, p)
    return dproj, dgl


def _mm_nt_relu2_bwd(dz, wdown, u, tm=512, tn=1024):
    t, d = dz.shape
    f = wdown.shape[0]

    def body(dz_ref, w_ref, u_ref, du_ref):
        da = _dot_nt(dz_ref[...], w_ref[...])
        du_ref[...] = (da * (2.0 * jnp.maximum(u_ref[...], 0.0))).astype(BF16)

    return pl.pallas_call(
        body, name="mlp_down_bwd", grid=(t // tm, f // tn),
        in_specs=[pl.BlockSpec((tm, d), lambda i, j: (i, 0)), pl.BlockSpec((tn, d), lambda i, j: (j, 0)),
                  pl.BlockSpec((tm, tn), lambda i, j: (i, j))],
        out_specs=pl.BlockSpec((tm, tn), lambda i, j: (i, j)), out_shape=_sds((t, f), BF16),
        compiler_params=_cp(("parallel", "parallel")))(dz, wdown, u)


def _ln_bwd_rows(dx, xh, rs, g):
    dxh = dx * g
    m1 = jnp.mean(dxh, axis=-1, keepdims=True)
    m2 = jnp.mean(dxh * xh, axis=-1, keepdims=True)
    return rs * (dxh - m1 - xh * m2)


def _mm_nt_res_lnbwd(name, a, w, dres, xh, rs, g, tm, tk):
    t, k = a.shape
    d = w.shape[0]
    nk = k // tk

    def body(a_ref, w_ref, dr_ref, xh_ref, rs_ref, g_ref, dz_ref, dzb_ref, dg_ref, db_ref, acc_ref):
        i, kk = pl.program_id(0), pl.program_id(1)

        @pl.when(kk == 0)
        def _():
            acc_ref[...] = ALPHA * dr_ref[...]

        acc_ref[...] += _dot_nt(a_ref[...], w_ref[...])

        @pl.when(jnp.logical_and(i == 0, kk == 0))
        def _():
            dg_ref[...] = jnp.zeros_like(dg_ref)
            db_ref[...] = jnp.zeros_like(db_ref)

        @pl.when(kk == nk - 1)
        def _():
            dx = acc_ref[...]
            xhv = xh_ref[...]
            dz = _ln_bwd_rows(dx, xhv, rs_ref[...], g_ref[...])
            dz_ref[...] = dz
            dzb_ref[...] = dz.astype(BF16)
            dg_ref[...] += jnp.sum(dx * xhv, axis=0, keepdims=True)
            db_ref[...] += jnp.sum(dx, axis=0, keepdims=True)

    row = pl.BlockSpec((tm, d), lambda i, kk: (i, 0))
    vec = pl.BlockSpec((1, d), lambda i, kk: (0, 0))
    return pl.pallas_call(
        body, name=name, grid=(t // tm, nk),
        in_specs=[pl.BlockSpec((tm, tk), lambda i, kk: (i, kk)), pl.BlockSpec((d, tk), lambda i, kk: (0, kk)),
                  row, row, pl.BlockSpec((tm, 1), lambda i, kk: (i, 0)), vec],
        out_specs=[row, row, vec, vec],
        out_shape=[_sds((t, d), F32), _sds((t, d), BF16), _sds((1, d), F32), _sds((1, d), F32)],
        scratch_shapes=[pltpu.VMEM((tm, d), F32)],
        compiler_params=_cp(("arbitrary", "arbitrary")))(a, w, dres, xh, rs, g)


def _loss_ln_bwd(x2, target, xh, rs, g, tm=256):
    t, d = x2.shape

    def body(x_ref, t_ref, xh_ref, rs_ref, g_ref, loss_ref, dz_ref, dzb_ref, dg_ref, db_ref):
        @pl.when(pl.program_id(0) == 0)
        def _():
            loss_ref[...] = jnp.zeros_like(loss_ref)
            dg_ref[...] = jnp.zeros_like(dg_ref)
            db_ref[...] = jnp.zeros_like(db_ref)

        err = x_ref[...] - t_ref[...]
        per_row = jnp.mean(err * err, axis=-1, keepdims=True)
        loss_ref[...] += 0.5 * jnp.sum(per_row, axis=0, keepdims=True)
        dx = err * (1.0 / d)
        xhv = xh_ref[...]
        dz = _ln_bwd_rows(dx, xhv, rs_ref[...], g_ref[...])
        dz_ref[...] = dz
        dzb_ref[...] = dz.astype(BF16)
        dg_ref[...] += jnp.sum(dx * xhv, axis=0, keepdims=True)
        db_ref[...] += jnp.sum(dx, axis=0, keepdims=True)

    row = pl.BlockSpec((tm, d), lambda i: (i, 0))
    vec = pl.BlockSpec((1, d), lambda i: (0, 0))
    return pl.pallas_call(
        body, name="loss_ln_bwd", grid=(t // tm,),
        in_specs=[row, row, row, pl.BlockSpec((tm, 1), lambda i: (i, 0)), vec],
        out_specs=[pl.BlockSpec((1, LANE), lambda i: (0, 0)), row, row, vec, vec],
        out_shape=[_sds((1, LANE), F32), _sds((t, d), F32), _sds((t, d), BF16), _sds((1, d), F32),
                   _sds((1, d), F32)],
        compiler_params=_cp(("arbitrary",)))(x2, target, xh, rs, g)


def _attn_scores(q_ref, k_refs, bias_ref, i, dh):
    q = q_ref[...] * (dh ** -0.5)
    k = jnp.concatenate([r[...] for r in k_refs], axis=0)
    s = _dot_nt(q, k) + bias_ref[...]
    col = lax.broadcasted_iota(jnp.int32, s.shape, 1)
    s = jnp.where(col >= (2 - i) * QB, s, NEG_INF)
    m = jnp.max(s, axis=-1, keepdims=True)
    e = jnp.exp(s - m)
    return q, k, e / jnp.sum(e, axis=-1, keepdims=True)


def _attn_specs(dh, qcol, kcol, vcol):
    q_spec = pl.BlockSpec((QB, dh), lambda h, i: (i, qcol + h))
    k_specs = [pl.BlockSpec((QB, dh), functools.partial(lambda h, i, j: (jnp.maximum(i - 2 + j, 0), kcol + h), j=j))
               for j in range(3)]
    v_specs = [pl.BlockSpec((QB, dh), functools.partial(lambda h, i, j: (jnp.maximum(i - 2 + j, 0), vcol + h), j=j))
               for j in range(3)]
    bias_spec = pl.BlockSpec((None, QB, KW), lambda h, i: (h, 0, 0))
    return q_spec, k_specs, v_specs, bias_spec


def _attn_fwd(p, bias, d, off):
    t = p.shape[0]
    dh = d // ATTN_HEADS

    def body(q_ref, k0, k1, k2, v0, v1, v2, bias_ref, o_ref):
        _, _, pr = _attn_scores(q_ref, (k0, k1, k2), bias_ref, pl.program_id(1), dh)
        v = jnp.concatenate([v0[...], v1[...], v2[...]], axis=0)
        o_ref[...] = _dot(pr, v).astype(o_ref.dtype)

    q_spec, k_specs, v_specs, bias_spec = _attn_specs(dh, off["aq"] // dh, off["ak"] // dh, off["av"] // dh)
    return pl.pallas_call(
        body, name="attn_fwd", grid=(ATTN_HEADS, t // QB),
        in_specs=[q_spec] + k_specs + v_specs + [bias_spec],
        out_specs=pl.BlockSpec((QB, dh), lambda h, i: (i, h)), out_shape=_sds((t, d), BF16),
        compiler_params=_cp(("parallel", "parallel")))(p, p, p, p, p, p, p, bias)


def _attn_bwd(p, bias, do, d, off):
    t = p.shape[0]
    dh = d // ATTN_HEADS
    tp = t + 2 * QB

    def body(q_ref, k0, k1, k2, v0, v1, v2, bias_ref, do_ref, dq_ref, dk_ref, dv_ref, dbias_ref):
        i = pl.program_id(1)

        @pl.when(i == 0)
        def _():
            dk_ref[...] = jnp.zeros_like(dk_ref)
            dv_ref[...] = jnp.zeros_like(dv_ref)
            dbias_ref[...] = jnp.zeros_like(dbias_ref)

        q, k, pr = _attn_scores(q_ref, (k0, k1, k2), bias_ref, i, dh)
        v = jnp.concatenate([v0[...], v1[...], v2[...]], axis=0)
        dov = do_ref[...]
        dp = _dot_nt(dov, v)
        delta = jnp.sum(pr * dp, axis=-1, keepdims=True)
        ds = pr * (dp - delta)
        dbias_ref[...] += ds
        dq_ref[...] = (_dot(ds, k) * (dh ** -0.5)).astype(dq_ref.dtype)
        rows = pl.ds(pl.multiple_of(i * QB, QB), KW)
        dk_ref[rows, :] += _dot(ds.T, q)
        dv_ref[rows, :] += _dot(pr.T, dov)

    q_spec, k_specs, v_specs, bias_spec = _attn_specs(dh, off["aq"] // dh, off["ak"] // dh, off["av"] // dh)
    acc_spec = pl.BlockSpec((tp, dh), lambda h, i: (0, h))
    return pl.pallas_call(
        body, name="attn_bwd", grid=(ATTN_HEADS, t // QB),
        in_specs=[q_spec] + k_specs + v_specs + [bias_spec, pl.BlockSpec((QB, dh), lambda h, i: (i, h))],
        out_specs=[pl.BlockSpec((QB, dh), lambda h, i: (i, h)), acc_spec, acc_spec, bias_spec],
        out_shape=[_sds((t, d), BF16), _sds((tp, d), F32), _sds((tp, d), F32),
                   _sds((ATTN_HEADS, QB, KW), F32)],
        compiler_params=_cp(("parallel", "arbitrary")))(p, p, p, p, p, p, p, bias, do)


def _onehot_mm(name, a, b):
    def body(a_ref, b_ref, o_ref):
        o_ref[...] = _dot_hi(a_ref[...], b_ref[...])

    return pl.pallas_call(body, name=name, out_shape=_sds((a.shape[0], b.shape[1]), F32),
                          compiler_params=pltpu.CompilerParams(vmem_limit_bytes=VMEM_LIMIT))(a, b)


def _diag_index():
    ii, jj = np.arange(CHUNK)[:, None], np.arange(CHUNK)[None, :]
    return (ii - jj + CHUNK - 1).reshape(-1)


def _bias_expand(rel_bias):
    h = rel_bias.shape[0]
    nq, nk, shift = QB // CHUNK, KW // CHUNK, (2 * QB) // CHUNK
    nbin, ndc = 3 * LANE, 4
    rb = jnp.pad(rel_bias, ((0, 0), (0, nbin - rel_bias.shape[1])))
    win = np.clip(CHUNK * np.arange(ndc)[:, None] + np.arange(LANE)[None, :] - (CHUNK - 1), -REL_CLIP, REL_CLIP)
    sel = (jnp.arange(nbin)[:, None] == jnp.asarray((win + REL_CLIP).reshape(1, -1))).astype(F32)
    windows = _onehot_mm("bias_windows", rb, sel)
    diag_t = (jnp.arange(LANE)[:, None] == jnp.asarray(_diag_index().reshape(1, -1))).astype(F32)
    blocks = _onehot_mm("bias_blocks", windows.reshape(h * ndc, LANE), diag_t).reshape(h, ndc, CHUNK, CHUNK)
    off_band = jnp.full((h, CHUNK, CHUNK), NEG_INF, F32)
    rows = []
    for ic in range(nq):
        dcs = [ic - jc + shift for jc in range(nk)]
        rows.append(jnp.concatenate([blocks[:, min(dc, ndc - 1)] if 0 <= dc <= ATTN_LEFT else off_band
                                     for dc in dcs], axis=2))
    return jnp.concatenate(rows, axis=1)


def _bias_reduce(dbias):
    h = dbias.shape[0]
    nq, nk = QB // CHUNK, KW // CHUNK
    nbin = 3 * LANE
    blocks = dbias.reshape(h, nq, CHUNK, nk, CHUNK).transpose(0, 1, 3, 2, 4).reshape(h * nq * nk, CHUNK * CHUNK)
    diag = (jnp.asarray(_diag_index().reshape(-1, 1)) == jnp.arange(LANE)[None, :]).astype(F32)
    ic = np.arange(nq)[:, None, None]
    jc = np.arange(nk)[None, :, None]
    dl = np.arange(LANE)[None, None, :] - (CHUNK - 1)
    rel = np.clip(CHUNK * (ic - jc + (2 * QB) // CHUNK) + dl, -REL_CLIP, REL_CLIP) + REL_CLIP
    bins = (jnp.asarray(rel.reshape(-1, 1)) == jnp.arange(nbin)[None, :]).astype(F32)

    diags = _onehot_mm("bias_diag_sums", blocks, diag)
    out = _onehot_mm("bias_bin_sums", diags.reshape(h, nq * nk * LANE), bins)
    return out[:, :2 * REL_CLIP + 1]


def _tri(lower):
    r = lax.broadcasted_iota(jnp.int32, (CHUNK, CHUNK), 0)
    c = lax.broadcasted_iota(jnp.int32, (CHUNK, CHUNK), 1)
    return (r >= c) if lower else (r <= c)


def _lin_prep(gla, q, k, aux):
    dk = q.shape[-1]
    if gla:
        glr, wlr, blr = aux
        q = q * (dk ** -0.5)
        pre = _dot(glr, wlr) + blr
        log_a = (jnp.minimum(pre, 0.0) - jnp.log(1.0 + jnp.exp(-jnp.abs(pre)))) / GATE_NORM
        b = _dot_hi(_tri(True).astype(F32), log_a)
        return q, k, b, pre
    cs, sn, lg = aux
    half = dk // 2
    q = q * cs + pltpu.roll(q, half, 1) * sn
    k = (k * cs + pltpu.roll(k, half, 1) * sn) * (dk ** -0.5)
    pos = lax.broadcasted_iota(jnp.int32, (CHUNK, dk), 0).astype(F32) + 1.0
    return q, k, pos * lg, None


def _lin_chunk(q, k, v, b, st):
    eb, enb = jnp.exp(b), jnp.exp(-b)
    last = b[CHUNK - 1:CHUNK, :]
    qf, kf, qb, kb = q * eb, k * enb, q * enb, k * eb
    kl = k * jnp.exp(last - b)
    s = jnp.where(_tri(True), _dot_nt(qf, kf), _dot_nt(qb, kb))
    o = _dot(s, v) + _dot_nt(qf, st)
    st_new = st * jnp.exp(last) + _dot(v.T, kl)
    return o, st_new, (eb, enb, last, qf, kf, qb, kb, kl, s)


def _lin_norm_gate(gla, o, gate, gn):
    sg = _sigmoid(gate)
    silu = gate * sg
    if gla:
        r = lax.rsqrt(jnp.mean(o * o, axis=-1, keepdims=True) + LN_EPS)
        hn = o * r
        return silu * (hn * gn), (sg, silu, r, hn)
    mu = jnp.mean(o, axis=-1, keepdims=True)
    oc = o - mu
    r = lax.rsqrt(jnp.mean(oc * oc, axis=-1, keepdims=True) + LN_EPS)
    hn = oc * r
    return silu * hn, (sg, silu, r, hn)


def _lin_specs(gla, dk, dv, off, rev, nb):
    pre = "g" if gla else "r"
    qc, kc, vc, gc = (off[pre + "q"] // dk, off[pre + "k"] // dk, off[pre + "v"] // dv, off[pre + "g"] // dv)

    def blk(i):
        return nb - 1 - i if rev else i

    specs = [pl.BlockSpec((LB, dk), lambda h, i: (blk(i), qc + h)),
             pl.BlockSpec((LB, dk), lambda h, i: (blk(i), kc + h)),
             pl.BlockSpec((LB, dv), lambda h, i: (blk(i), vc + h)),
             pl.BlockSpec((LB, dv), lambda h, i: (blk(i), gc + h))]
    if gla:
        specs += [pl.BlockSpec((LB, LANE), lambda h, i: (blk(i), off["glr"] // LANE)),
                  pl.BlockSpec((LANE, dk), lambda h, i: (0, h)),
                  pl.BlockSpec((1, dk), lambda h, i: (0, h)),
                  pl.BlockSpec((1, dv), lambda h, i: (0, 0))]
    else:
        specs += [pl.BlockSpec((LB, dk), lambda h, i: (blk(i), 0)),
                  pl.BlockSpec((LB, dk), lambda h, i: (blk(i), 0)),
                  pl.BlockSpec((None, 1, dk), lambda h, i: (h, 0, 0))]
    return specs, blk


def _lin_aux(gla, refs, rows):
    if gla:
        glr_ref, wlr_ref, blr_ref, gn_ref = refs
        return (glr_ref[rows, :], wlr_ref[...], blr_ref[...]), gn_ref[...]
    cs_ref, sn_ref, lg_ref = refs
    return (cs_ref[rows, :], sn_ref[rows, :], lg_ref[...]), None


def _lin_fwd(gla, p, aux_arrays, d, off):
    t = p.shape[0]
    dk, dv = d // (2 * LIN_HEADS), d // LIN_HEADS
    nb, cb = t // LB, LB // CHUNK
    naux = len(aux_arrays)

    def body(*refs):
        q_ref, k_ref, v_ref, g_ref = refs[:4]
        aux_refs = refs[4:4 + naux]
        o_ref, bo_ref, st_out_ref, st_ref = refs[4 + naux:]

        @pl.when(pl.program_id(1) == 0)
        def _():
            st_ref[...] = jnp.zeros_like(st_ref)

        for c in range(cb):
            rows = pl.ds(c * CHUNK, CHUNK)
            aux, gn = _lin_aux(gla, aux_refs, rows)
            q, k, b, _ = _lin_prep(gla, q_ref[rows, :], k_ref[rows, :], aux)
            st = st_ref[...]
            st_out_ref[c] = st
            o, st_new, _ = _lin_chunk(q, k, v_ref[rows, :], b, st)
            st_ref[...] = st_new
            o_ref[rows, :] = o
            out, _ = _lin_norm_gate(gla, o, g_ref[rows, :], gn)
            bo_ref[rows, :] = out.astype(BF16)

    specs, _ = _lin_specs(gla, dk, dv, off, False, nb)
    orow = pl.BlockSpec((LB, dv), lambda h, i: (i, h))
    return pl.pallas_call(
        body, name="gla_fwd" if gla else "ret_fwd", grid=(LIN_HEADS, nb), in_specs=specs,
        out_specs=[orow, orow, pl.BlockSpec((None, cb, dv, dk), lambda h, i: (h, i, 0, 0))],
        out_shape=[_sds((t, d), F32), _sds((t, d), BF16), _sds((LIN_HEADS, t // CHUNK, dv, dk), F32)],
        scratch_shapes=[pltpu.VMEM((dv, dk), F32)],
        compiler_params=_cp(("parallel", "arbitrary")))(p, p, p, p, *aux_arrays)


def _lin_bwd(gla, p, aux_arrays, o, states, dbo, d, off):
    t = p.shape[0]
    dk, dv = d // (2 * LIN_HEADS), d // LIN_HEADS
    nb, cb = t // LB, LB // CHUNK
    naux = len(aux_arrays)

    def body(*refs):
        q_ref, k_ref, v_ref, g_ref = refs[:4]
        aux_refs = refs[4:4 + naux]
        o_ref, st_in_ref, dbo_ref = refs[4 + naux:7 + naux]
        outs = refs[7 + naux:]
        dq_ref, dk_ref, dv_ref, dg_ref = outs[:4]
        dst_ref = outs[-1]
        first = pl.program_id(1) == 0

        @pl.when(first)
        def _():
            dst_ref[...] = jnp.zeros_like(dst_ref)

        if gla:
            dpre_ref, dblr_ref, dgn_ref = outs[4:7]

            @pl.when(first)
            def _():
                dblr_ref[...] = jnp.zeros_like(dblr_ref)
                dgn_ref[...] = jnp.zeros_like(dgn_ref)

        for c in reversed(range(cb)):
            rows = pl.ds(c * CHUNK, CHUNK)
            aux, gn = _lin_aux(gla, aux_refs, rows)
            q, k, b, pre = _lin_prep(gla, q_ref[rows, :], k_ref[rows, :], aux)
            v = v_ref[rows, :]
            st = st_in_ref[c]
            _, _, (eb, enb, last, qf, kf, qb, kb, kl, s) = _lin_chunk(q, k, v, b, st)
            gate = g_ref[rows, :]
            dout = dbo_ref[rows, :]
            _, (sg, silu, r, hn) = _lin_norm_gate(gla, o_ref[rows, :], gate, gn)
            dsilu = sg * (1.0 + gate * (1.0 - sg))
            if gla:
                y = hn * gn
                dy = dout * silu
                dg_ref[rows, :] = (dout * y * dsilu).astype(BF16)
                dgn_ref[...] += jnp.sum(dy * hn, axis=0, keepdims=True)
                dhn = dy * gn
                do = r * (dhn - hn * jnp.mean(dhn * hn, axis=-1, keepdims=True))
            else:
                dhn = dout * silu
                dg_ref[rows, :] = (dout * hn * dsilu).astype(BF16)
                do = r * (dhn - jnp.mean(dhn, axis=-1, keepdims=True)
                          - hn * jnp.mean(dhn * hn, axis=-1, keepdims=True))
            dstn = dst_ref[...]
            dec = jnp.exp(last)
            ds = _dot_nt(do, v)
            low = _tri(True)
            dsf = jnp.where(low, ds, 0.0)
            dsb = jnp.where(low, 0.0, ds)
            dvv = _dot(s.T, do) + _dot_nt(kl, dstn)
            dqf = _dot(dsf, kf) + _dot(do, st)
            dkf = _dot(dsf.T, qf)
            dqb = _dot(dsb, kb)
            dkb = _dot(dsb.T, qb)
            dkl = _dot(v, dstn)
            dst_ref[...] = dstn * dec + _dot(do.T, qf)
            dq = dqf * eb + dqb * enb
            dkk = dkf * enb + dkb * eb + dkl * jnp.exp(last - b)
            dv_ref[rows, :] = dvv.astype(BF16)
            if gla:
                ddec = jnp.sum(dstn * st, axis=0, keepdims=True)
                db = dqf * qf - dkf * kf - dqb * qb + dkb * kb - dkl * kl
                dlast = jnp.sum(dkl * kl, axis=0, keepdims=True) + ddec * dec
                rowi = lax.broadcasted_iota(jnp.int32, db.shape, 0)
                db = db + jnp.where(rowi == CHUNK - 1, dlast, 0.0)
                dlog_a = _dot_hi(_tri(False).astype(F32), db)
                dpre = dlog_a * (1.0 / GATE_NORM) * (1.0 - _sigmoid(pre))
                dpre_ref[rows, :] = dpre
                dblr_ref[...] += jnp.sum(dpre, axis=0, keepdims=True)
                dq_ref[rows, :] = (dq * (dk ** -0.5)).astype(BF16)
                dk_ref[rows, :] = dkk.astype(BF16)
            else:
                cs, sn, _ = aux
                half = dk // 2
                dkk = dkk * (dk ** -0.5)
                dq_ref[rows, :] = (dq * cs + pltpu.roll(dq * sn, half, 1)).astype(BF16)
                dk_ref[rows, :] = (dkk * cs + pltpu.roll(dkk * sn, half, 1)).astype(BF16)

    specs, blk = _lin_specs(gla, dk, dv, off, True, nb)
    vrow = pl.BlockSpec((LB, dv), lambda h, i: (blk(i), h))
    krow = pl.BlockSpec((LB, dk), lambda h, i: (blk(i), h))
    specs += [vrow, pl.BlockSpec((None, cb, dv, dk), lambda h, i: (h, blk(i), 0, 0)), vrow]
    out_specs = [krow, krow, vrow, vrow]
    out_shape = [_sds((t, d // 2), BF16), _sds((t, d // 2), BF16), _sds((t, d), BF16), _sds((t, d), BF16)]
    if gla:
        out_specs += [krow, pl.BlockSpec((None, 1, dk), lambda h, i: (h, 0, 0)),
                      pl.BlockSpec((None, 1, dv), lambda h, i: (h, 0, 0))]
        out_shape += [_sds((t, d // 2), F32), _sds((LIN_HEADS, 1, dk), F32), _sds((LIN_HEADS, 1, dv), F32)]
    out_specs.append(pl.BlockSpec((None, dv, dk), lambda h, i: (h, 0, 0)))
    out_shape.append(_sds((LIN_HEADS, dv, dk), F32))
    res = pl.pallas_call(
        body, name="gla_bwd" if gla else "ret_bwd", grid=(LIN_HEADS, nb), in_specs=specs,
        out_specs=out_specs, out_shape=out_shape,
        compiler_params=_cp(("parallel", "arbitrary")))(p, p, p, p, *aux_arrays, o, states, dbo)
    return res[:-1]


def _row_tile(rows, cols):
    cap = max(8, (2 << 20) // (4 * cols))
    t = rows
    while t > cap and t % 2 == 0:
        t //= 2
    return t


def _add_selected(name, g, r, sel):
    _, rows, cols = g.shape
    tr = _row_tile(rows, cols)

    def body(sel_ref, g_ref, r_ref, o_ref):
        o_ref[...] = g_ref[...] + r_ref[...]

    gs = pltpu.PrefetchScalarGridSpec(
        num_scalar_prefetch=1, grid=(rows // tr,),
        in_specs=[pl.BlockSpec((None, tr, cols), lambda i, s: (s[0], i, 0)),
                  pl.BlockSpec((tr, cols), lambda i, s: (i, 0))],
        out_specs=pl.BlockSpec((tr, cols), lambda i, s: (i, 0)))
    return pl.pallas_call(body, name=name, grid_spec=gs, out_shape=_sds((rows, cols), F32),
                          compiler_params=_cp(("parallel",)))(sel, g, r)


def _sum_shards(name, h, rcv, sel):
    _, rows, cols = h.shape
    tr = _row_tile(rows, cols)

    def body(sel_ref, h_ref, r0, r1, r2, o_ref):
        o_ref[...] = ((h_ref[...] + r0[...]) + r1[...]) + r2[...]

    rspecs = [pl.BlockSpec((None, tr, cols), functools.partial(lambda i, s, j: (j, i, 0), j=j)) for j in range(3)]
    gs = pltpu.PrefetchScalarGridSpec(
        num_scalar_prefetch=1, grid=(rows // tr,),
        in_specs=[pl.BlockSpec((None, tr, cols), lambda i, s: (s[0], i, 0))] + rspecs,
        out_specs=pl.BlockSpec((tr, cols), lambda i, s: (i, 0)))
    return pl.pallas_call(body, name=name, grid_spec=gs, out_shape=_sds((rows, cols), F32),
                          compiler_params=_cp(("parallel",)))(sel, h, rcv, rcv, rcv)


def _adamw(name, w, g, m, v):
    rows, cols = w.shape
    tr = _row_tile(rows, cols)
    c1 = 1.0 - ADAM_B1 ** ADAM_STEP
    c2 = 1.0 - ADAM_B2 ** ADAM_STEP

    def body(w_ref, g_ref, m_ref, v_ref, d_ref, nm_ref, nv_ref):
        gv = g_ref[...]
        nm = ADAM_B1 * m_ref[...] + (1.0 - ADAM_B1) * gv
        nv = ADAM_B2 * v_ref[...] + (1.0 - ADAM_B2) * jnp.square(gv)
        d_ref[...] = -ADAM_LR * ((nm / c1) / (jnp.sqrt(nv / c2) + ADAM_EPS) + ADAM_WD * w_ref[...])
        nm_ref[...] = nm
        nv_ref[...] = nv

    spec = pl.BlockSpec((tr, cols), lambda i: (i, 0))
    return pl.pallas_call(body, name=name, grid=(rows // tr,), in_specs=[spec] * 4, out_specs=[spec] * 3,
                          out_shape=[_sds((rows, cols), F32)] * 3, compiler_params=_cp(("parallel",)))(w, g, m, v)


def _place():
    x, y, c = (lax.axis_index(a) for a in MESH_AXES)
    chips = [(1 - x, y), (x, 1 - y), (1 - x, 1 - y)]
    return x, y, c, chips


def _chip_index(xy):
    return 2 * xy[0] + xy[1]


ANY = pl.BlockSpec(memory_space=pl.ANY)


def _allgather_weights(ws):
    n = len(ws)

    def body(*refs):
        ins, outs = refs[:n], refs[n:2 * n]
        send, recv, lsem = refs[2 * n:]
        x, y, c, chips = _place()
        me = _chip_index((x, y))

        def rcopy(a, k, src, dst, to):
            return pltpu.make_async_remote_copy(src_ref=src, dst_ref=dst, send_sem=send.at[a * 6 + k],
                                                recv_sem=recv.at[a * 6 + k], device_id=to, device_id_type=DEV)

        local = [pltpu.make_async_copy(ins[a], outs[a].at[me], lsem.at[a]) for a in range(n)]
        for cp in local:
            cp.start()
        first = [rcopy(a, j, ins[a].at[c], outs[a].at[me, c], (*ch, c)) for a in range(n) for j, ch in enumerate(chips)]
        for cp in first:
            cp.start()
        passed = []
        for a in range(n):
            for j, ch in enumerate(chips):
                slot = outs[a].at[_chip_index(ch), c]
                rcopy(a, j, slot, slot, (x, y, c)).wait_recv()
                fw = rcopy(a, 3 + j, slot, slot, (x, y, 1 - c))
                fw.start()
                passed.append(fw)
        for a in range(n):
            for j, ch in enumerate(chips):
                slot = outs[a].at[_chip_index(ch), 1 - c]
                rcopy(a, 3 + j, slot, slot, (x, y, c)).wait_recv()
        for cp in first + passed:
            cp.wait_send()
        for cp in local:
            cp.wait()

    return pl.pallas_call(
        body, name="allgather_weights", in_specs=[ANY] * n, out_specs=[ANY] * n,
        out_shape=[_sds((4,) + w.shape, w.dtype) for w in ws],
        scratch_shapes=[pltpu.SemaphoreType.DMA((6 * n,)), pltpu.SemaphoreType.DMA((6 * n,)),
                        pltpu.SemaphoreType.DMA((n,))])(*ws)


def _sibling_send(name, srcs, pick, shapes):
    n = len(srcs)

    def body(*refs):
        ins, outs = refs[:n], refs[n:2 * n]
        send, recv = refs[2 * n:]
        x, y, c, _ = _place()
        cps = [pltpu.make_async_remote_copy(src_ref=pick(ins[a], c), dst_ref=outs[a], send_sem=send.at[a],
                                            recv_sem=recv.at[a], device_id=(x, y, 1 - c), device_id_type=DEV)
               for a in range(n)]
        for cp in cps:
            cp.start()
        for cp in cps:
            cp.wait()

    return pl.pallas_call(
        body, name=name, in_specs=[ANY] * n, out_specs=[ANY] * n,
        out_shape=[_sds(s, F32) for s in shapes],
        scratch_shapes=[pltpu.SemaphoreType.DMA((n,)), pltpu.SemaphoreType.DMA((n,))])(*srcs)


def _chip_scatter(hs):
    n = len(hs)

    def body(*refs):
        ins, outs = refs[:n], refs[n:2 * n]
        send, recv = refs[2 * n:]
        x, y, c, chips = _place()
        cps = [pltpu.make_async_remote_copy(src_ref=ins[a].at[_chip_index(ch)], dst_ref=outs[a].at[j],
                                            send_sem=send.at[a * 3 + j], recv_sem=recv.at[a * 3 + j],
                                            device_id=(*ch, c), device_id_type=DEV)
               for a in range(n) for j, ch in enumerate(chips)]
        for cp in cps:
            cp.start()
        for cp in cps:
            cp.wait()

    return pl.pallas_call(
        body, name="grad_chip_scatter", in_specs=[ANY] * n, out_specs=[ANY] * n,
        out_shape=[_sds((3,) + h.shape[1:], h.dtype) for h in hs],
        scratch_shapes=[pltpu.SemaphoreType.DMA((3 * n,)), pltpu.SemaphoreType.DMA((3 * n,))])(*hs)


def _small_allreduce(v):
    rows = v.shape[0]
    ndev = 8

    def body(v_ref, o_ref, gat_ref, send, recv):
        x, y, c, _ = _place()
        me = 4 * x + 2 * y + c
        cps = []
        for k in range(1, ndev):
            to = (me + k) % ndev
            cp = pltpu.make_async_remote_copy(src_ref=v_ref, dst_ref=gat_ref.at[me], send_sem=send.at[k - 1],
                                              recv_sem=recv.at[me], device_id=(to // 4, (to // 2) % 2, to % 2),
                                              device_id_type=DEV)
            cp.start()
            cps.append(cp)
        gat_ref[me] = v_ref[...]
        for k in range(1, ndev):
            frm = (me + k) % ndev
            pltpu.make_async_remote_copy(src_ref=v_ref, dst_ref=gat_ref.at[frm], send_sem=send.at[k - 1],
                                         recv_sem=recv.at[frm], device_id=(x, y, c), device_id_type=DEV).wait_recv()
        for cp in cps:
            cp.wait_send()
        acc = gat_ref[0]
        for k in range(1, ndev):
            acc = acc + gat_ref[k]
        o_ref[...] = acc

    vm = pl.BlockSpec(memory_space=pltpu.VMEM)
    return pl.pallas_call(
        body, name="small_allreduce", in_specs=[vm], out_specs=vm, out_shape=_sds((rows, LANE), F32),
        scratch_shapes=[pltpu.VMEM((ndev, rows, LANE), F32), pltpu.SemaphoreType.DMA((ndev - 1,)),
                        pltpu.SemaphoreType.DMA((ndev,))])(v)


def _layout(d):
    half = d // 2
    names = [("aq", d), ("ak", d), ("av", d), ("rq", half), ("rk", half), ("rv", d), ("rg", d),
             ("gq", half), ("gk", half), ("gv", d), ("gg", d), ("gates", 3 * d), ("glr", 2 * LANE)]
    off, pos = {}, 0
    for nm, sz in names:
        off[nm] = pos
        pos += sz
    return off, pos


def _pad_cols(w, d):
    a = 8 * d + d
    lr = w[..., a:a + GATE_RANK]
    z = jnp.zeros(w.shape[:-1] + (2 * LANE - GATE_RANK,), w.dtype)
    return jnp.concatenate([w[..., :a], w[..., a + GATE_RANK:], lr, z], axis=-1)


def _unpad_cols(g, d):
    a = 8 * d + d
    return jnp.concatenate([g[..., :a], g[..., a + 3 * d:a + 3 * d + GATE_RANK], g[..., a:a + 3 * d]], axis=-1)


def kernel(x, ln_in_g, ln_in_b, w_in, rel_bias, gla_w_lr, gla_b_lr, gla_norm_g, w_branch, w_out, ln1_g, ln1_b, w_up, w_down, ln2_g, ln2_b, loss_target, m_ln_in_g, m_ln_in_b, m_w_in, m_rel_bias, m_gla_w_lr, m_gla_b_lr, m_gla_norm_g, m_w_branch, m_w_out, m_ln1_g, m_ln1_b, m_w_up, m_w_down, m_ln2_g, m_ln2_b, v_ln_in_g, v_ln_in_b, v_w_in, v_rel_bias, v_gla_w_lr, v_gla_b_lr, v_gla_norm_g, v_w_branch, v_w_out, v_ln1_g, v_ln1_b, v_w_up, v_w_down, v_ln2_g, v_ln2_b):
    t, d = x.shape[1], x.shape[2]
    dff = 4 * d
    half = d // 2
    off, npad = _layout(d)
    xi, yi, ci = (lax.axis_index(a) for a in MESH_AXES)
    chip = 2 * xi + yi
    csel = jnp.reshape(ci, (1,)).astype(jnp.int32)
    psel = jnp.reshape(chip, (1,)).astype(jnp.int32)

    big = [w_in, w_branch, w_out, w_up, w_down]
    g_in, g_br, g_out, g_up, g_down = _allgather_weights([w.astype(BF16) for w in big])
    win = [_pad_cols(jnp.transpose(g_in[:, l], (1, 0, 2)).reshape(d, -1), d) for l in range(DEPTH)]
    wbr = [jnp.transpose(g_br[:, l], (1, 0, 2, 3)).reshape(N_BRANCH, d, d) for l in range(DEPTH)]
    wout = [g_out[:, l].reshape(d, d) for l in range(DEPTH)]
    wup = [jnp.transpose(g_up[:, l], (1, 0, 2)).reshape(d, dff) for l in range(DEPTH)]
    wdown = [g_down[:, l].reshape(dff, d) for l in range(DEPTH)]

    dkh = half // LIN_HEADS
    lr_rows = DEPTH * GATE_RANK
    lr_slab = jnp.zeros((lr_rows, 4, half // 4), F32)
    lr_slab = lax.dynamic_update_slice(lr_slab, (gla_w_lr.reshape(lr_rows, 1, half // 4) * jnp.where(ci == 0, 1.0, 0.0)),
                                       (0, chip, 0))
    wlr_full = _small_allreduce(lr_slab.reshape(-1, LANE)).reshape(DEPTH, GATE_RANK, half)
    wlr_pad = jnp.concatenate([wlr_full, jnp.zeros((DEPTH, LANE - GATE_RANK, half), F32)], axis=1)

    inv = 10000.0 ** (-jnp.arange(0, dkh, 2, dtype=F32) / dkh)
    ang = jnp.arange(t, dtype=F32)[:, None] * inv[None, :]
    cos, sin = jnp.cos(ang), jnp.sin(ang)
    rope_c = jnp.concatenate([cos, cos], axis=1)
    rope_s = jnp.concatenate([-sin, sin], axis=1)
    log_gamma = jnp.log1p(-jnp.exp2(-5.0 - jnp.arange(LIN_HEADS, dtype=F32)))
    lg_tab = jnp.broadcast_to(log_gamma[:, None, None], (LIN_HEADS, 1, dkh))

    def vec(a):
        return a.reshape(1, -1)

    x0, x0b, xh_in, rs_in = _ln_in(x[0], vec(ln_in_g), vec(ln_in_b))
    saved = []
    xl, xlb = x0, x0b
    for l in range(DEPTH):
        p = _mm("proj_in", xlb, win[l], 512, 1792)
        bias = _bias_expand(rel_bias[l])
        attn = _attn_fwd(p, bias, d, off)
        ret_aux = (rope_c, rope_s, lg_tab)
        gla_aux = (p, wlr_pad[l], vec(gla_b_lr[l]), vec(gla_norm_g[l]))
        o_ret, b_ret, st_ret = _lin_fwd(False, p, ret_aux, d, off)
        o_gla, b_gla, st_gla = _lin_fwd(True, p, gla_aux, d, off)
        bo = jnp.stack([attn, b_ret, b_gla])
        proj, merged = _merge_fwd(bo, wbr[l], p, off["gates"])
        x1, x1b, xh1, rs1 = _mm_res_ln("out_proj_ln", merged, wout[l], xl, vec(ln1_g[l]), vec(ln1_b[l]), 256, False)
        u = _mm("mlp_up", x1b, wup[l], 512, 1024)
        x2, x2b, xh2, rs2, act = _mm_res_ln("mlp_down_ln", u, wdown[l], x1, vec(ln2_g[l]), vec(ln2_b[l]), 256, True)
        saved.append(dict(xlb=xlb, p=p, bias=bias, ret_aux=ret_aux, gla_aux=gla_aux, o_ret=o_ret, o_gla=o_gla,
                          st_ret=st_ret, st_gla=st_gla, bo=bo, proj=proj, merged=merged, x1b=x1b, xh1=xh1,
                          rs1=rs1, u=u, xh2=xh2, rs2=rs2, act=act))
        xl, xlb = x2, x2b

    small = {}
    big_grads = [None] * DEPTH
    last = saved[-1]
    loss_p, dz2, dz2b, dg, db = _loss_ln_bwd(xl, loss_target[0], last["xh2"], last["rs2"], vec(ln2_g[DEPTH - 1]))
    small["loss"] = loss_p[:, :1]
    grad_x = None
    for l in reversed(range(DEPTH)):
        s = saved[l]
        small[("ln2_g", l)], small[("ln2_b", l)] = dg, db
        du = _mm_nt_relu2_bwd(dz2b, wdown[l], s["u"])
        g_wdown = _mm("grad_w_down", s["act"].T, dz2b, 1024, 512)
        g_wup = _mm("grad_w_up", s["x1b"].T, du, 1024, 512)
        dz1, dz1b, dg1, db1 = _mm_nt_res_lnbwd("mlp_up_bwd_ln", du, wup[l], dz2, s["xh1"], s["rs1"],
                                               vec(ln1_g[l]), 256, dff)
        small[("ln1_g", l)], small[("ln1_b", l)] = dg1, db1
        dproj, dgl = _merge_bwd(dz1b, wout[l], s["proj"], s["p"], off["gates"])
        g_wout = _mm("grad_w_out", s["merged"].T, dz1b, 1024, 512)
        dbo = _mm("branch_proj_bwd", dproj, wbr[l], 512, 512, nt=True)
        g_wbr = _mm("grad_w_branch", jnp.transpose(s["bo"], (0, 2, 1)), dproj, 1024, 512)
        dq_a, dk_acc, dv_acc, dbias = _attn_bwd(s["p"], s["bias"], dbo[0], d, off)
        small[("rel_bias", l)] = _bias_reduce(dbias)
        dk_a = dk_acc[2 * QB:].astype(BF16)
        dv_a = dv_acc[2 * QB:].astype(BF16)
        dq_r, dk_r, dv_r, dg_r = _lin_bwd(False, s["p"], s["ret_aux"], s["o_ret"], s["st_ret"], dbo[1], d, off)
        dq_g, dk_g, dv_g, dg_g, dpre, dblr, dgn = _lin_bwd(True, s["p"], s["gla_aux"], s["o_gla"], s["st_gla"],
                                                           dbo[2], d, off)
        small[("gla_b_lr", l)] = dblr.reshape(1, half)
        small[("gla_norm_g", l)] = jnp.sum(dgn, axis=0)
        dpre_b = dpre.astype(BF16)
        glr_b = s["p"][:, off["glr"]:off["glr"] + LANE].astype(BF16)
        dglr = _mm("gate_lr_bwd", dpre_b, wlr_pad[l], 512, LANE, nt=True, out_dtype=BF16)
        small[("gla_w_lr", l)] = _mm("grad_gla_w_lr", glr_b.T, dpre_b, LANE, half)[:GATE_RANK]
        dp = jnp.concatenate([dq_a, dk_a, dv_a, dq_r, dk_r, dv_r, dg_r, dq_g, dk_g, dv_g, dg_g,
                              dgl[0], dgl[1], dgl[2], dglr, jnp.zeros((t, LANE), BF16)], axis=1)
        g_win = _mm("grad_w_in", s["xlb"].T, dp, 1024, 896)
        if l > 0:
            prev = saved[l - 1]
            xh_p, rs_p, g_p = prev["xh2"], prev["rs2"], vec(ln2_g[l - 1])
        else:
            xh_p, rs_p, g_p = xh_in, rs_in, vec(ln_in_g)
        dzp, dzpb, dg, db = _mm_nt_res_lnbwd("proj_in_bwd_ln", dp, win[l], dz1, xh_p, rs_p, g_p, 512, 1792)
        big_grads[l] = (g_win, g_wbr, g_wout, g_wup, g_wdown)
        dz2, dz2b = dzp, dzpb
        grad_x = dzp
    small["ln_in_g"], small["ln_in_b"] = dg, db

    def to_shards(gs):
        g_win, g_wbr, g_wout, g_wup, g_wdown = gs
        a = _unpad_cols(g_win, d).reshape(d, 4, -1).transpose(1, 0, 2)
        b = g_wbr.reshape(N_BRANCH, 4, d // 4, d).transpose(1, 0, 2, 3).reshape(4, -1, d)
        cc = g_wout.reshape(4, d // 4, d)
        e = g_wup.reshape(d, 4, d).transpose(1, 0, 2)
        f = g_wdown.reshape(4, d, d)
        return [a, b, cc, e, f]

    per_layer = [to_shards(big_grads[l]) for l in range(DEPTH)]
    gfull = [jnp.stack([per_layer[l][a] for l in range(DEPTH)]) for a in range(5)]
    flat = [g.reshape(DEPTH, -1, g.shape[-1]) for g in gfull]
    theirs = _sibling_send("grad_sibling_send", flat, lambda r, c: r.at[1 - c], [g.shape[1:] for g in flat])
    hsum = [_add_selected("grad_sibling_add", flat[a], theirs[a], csel) for a in range(5)]
    hs = [h.reshape((4, -1, h.shape[-1])) for h in hsum]
    rcv = _chip_scatter(hs)
    mine = [_sum_shards("grad_chip_sum", hs[a], rcv[a], psel) for a in range(5)]
    other = _sibling_send("grad_sibling_share", mine, lambda r, c: r, [g.shape for g in mine])
    grads_big = [jnp.where(ci == 0, jnp.stack([mine[a], other[a]]), jnp.stack([other[a], mine[a]]))
                 for a in range(5)]

    rb_pad = 3 * LANE
    pieces = [small["loss"].reshape(-1), jnp.zeros((LANE - 1,), F32), small["ln_in_g"].reshape(-1),
              small["ln_in_b"].reshape(-1)]
    for l in range(DEPTH):
        rb = jnp.pad(small[("rel_bias", l)], ((0, 0), (0, rb_pad - (2 * REL_CLIP + 1))))
        pieces += [rb.reshape(-1), small[("gla_w_lr", l)].reshape(-1), small[("gla_b_lr", l)].reshape(-1),
                   small[("gla_norm_g", l)].reshape(-1), small[("ln1_g", l)].reshape(-1),
                   small[("ln1_b", l)].reshape(-1), small[("ln2_g", l)].reshape(-1), small[("ln2_b", l)].reshape(-1)]
    sizes = [pc.shape[0] for pc in pieces]
    packed = jnp.concatenate(pieces)
    padn = (-packed.shape[0]) % (8 * LANE)
    packed = jnp.concatenate([packed, jnp.zeros((padn,), F32)]).reshape(-1, LANE)
    red = _small_allreduce(packed).reshape(-1)
    parts, pos = [], 0
    for sz in sizes:
        parts.append(red[pos:pos + sz])
        pos += sz
    loss = parts[0][0]
    g_ln_in_g, g_ln_in_b = parts[2], parts[3]
    per = 8
    g_rel = jnp.stack([parts[4 + per * l].reshape(ATTN_HEADS, rb_pad)[:, :2 * REL_CLIP + 1] for l in range(DEPTH)])
    g_wlr_full = jnp.stack([parts[5 + per * l].reshape(GATE_RANK, half) for l in range(DEPTH)])
    g_wlr = lax.dynamic_slice_in_dim(g_wlr_full, chip * (half // 4), half // 4, axis=2)
    g_blr = jnp.stack([parts[6 + per * l] for l in range(DEPTH)])
    g_gn = jnp.stack([parts[7 + per * l] for l in range(DEPTH)])
    g_ln1g = jnp.stack([parts[8 + per * l] for l in range(DEPTH)])
    g_ln1b = jnp.stack([parts[9 + per * l] for l in range(DEPTH)])
    g_ln2g = jnp.stack([parts[10 + per * l] for l in range(DEPTH)])
    g_ln2b = jnp.stack([parts[11 + per * l] for l in range(DEPTH)])

    g_w_in = grads_big[0].reshape(w_in.shape)
    g_w_branch = grads_big[1].reshape(w_branch.shape)
    g_w_out = grads_big[2].reshape(w_out.shape)
    g_w_up = grads_big[3].reshape(w_up.shape)
    g_w_down = grads_big[4].reshape(w_down.shape)

    grads = [g_ln_in_g, g_ln_in_b, g_w_in, g_rel, g_wlr, g_blr, g_gn, g_w_branch, g_w_out, g_ln1g, g_ln1b,
             g_w_up, g_w_down, g_ln2g, g_ln2b]
    ws = [ln_in_g, ln_in_b, w_in, rel_bias, gla_w_lr, gla_b_lr, gla_norm_g, w_branch, w_out, ln1_g, ln1_b,
          w_up, w_down, ln2_g, ln2_b]
    ms = [m_ln_in_g, m_ln_in_b, m_w_in, m_rel_bias, m_gla_w_lr, m_gla_b_lr, m_gla_norm_g, m_w_branch, m_w_out,
          m_ln1_g, m_ln1_b, m_w_up, m_w_down, m_ln2_g, m_ln2_b]
    vs = [v_ln_in_g, v_ln_in_b, v_w_in, v_rel_bias, v_gla_w_lr, v_gla_b_lr, v_gla_norm_g, v_w_branch, v_w_out,
          v_ln1_g, v_ln1_b, v_w_up, v_w_down, v_ln2_g, v_ln2_b]

    deltas, new_ms, new_vs = [None] * 15, [None] * 15, [None] * 15
    big_idx = [2, 7, 8, 11, 12]
    for i in big_idx:
        shp = ws[i].shape
        two = lambda a: a.reshape(-1, shp[-1])
        dl, nm, nv = _adamw("adamw_large", two(ws[i]), two(grads[i]), two(ms[i]), two(vs[i]))
        deltas[i], new_ms[i], new_vs[i] = dl.reshape(shp), nm.reshape(shp), nv.reshape(shp)
    small_idx = [i for i in range(15) if i not in big_idx]

    def pack(arrs):
        flat_ = jnp.concatenate([arrs[i].reshape(-1) for i in small_idx])
        pad_ = (-flat_.shape[0]) % (8 * LANE)
        return jnp.concatenate([flat_, jnp.ones((pad_,), F32)]).reshape(-1, LANE)

    dl, nm, nv = _adamw("adamw_small", pack(ws), pack(grads), pack(ms), pack(vs))
    pos = 0
    for i in small_idx:
        sz = int(np.prod(ws[i].shape))
        deltas[i] = dl.reshape(-1)[pos:pos + sz].reshape(ws[i].shape)
        new_ms[i] = nm.reshape(-1)[pos:pos + sz].reshape(ws[i].shape)
        new_vs[i] = nv.reshape(-1)[pos:pos + sz].reshape(ws[i].shape)
        pos += sz

    return (loss, grad_x[None], *grads, *deltas, *new_ms, *new_vs)
```

```python
import functools

import numpy as np
import jax
import jax.numpy as jnp
from jax import lax
from jax.experimental import pallas as pl
from jax.experimental.pallas import tpu as pltpu

F32 = jnp.float32
BF16 = jnp.bfloat16
MXU_DTYPE = BF16
HI = lax.Precision.HIGHEST

DEPTH = 2
CHUNK = 64
N_BRANCH = 3
ATTN_HEADS = 8
ATTN_LEFT = 8
REL_CLIP = 2 * CHUNK
LIN_HEADS = 4
GATE_RANK = 16
GATE_NORM = 16.0
LN_EPS = 1e-5
NEG_INF = -1e30
ALPHA = (2 * DEPTH) ** 0.25
ADAM_LR, ADAM_B1, ADAM_B2, ADAM_EPS, ADAM_WD, ADAM_STEP = 0.001, 0.9, 0.999, 1e-08, 0.01, 10

LANE = 128
VMEM_LIMIT = 56 << 20
QB = 256
KW = 3 * QB
LB = 256
MESH_AXES = ("x", "y", "c")
DEV = pl.DeviceIdType.MESH


def _cp(sem):
    return pltpu.CompilerParams(dimension_semantics=sem, vmem_limit_bytes=VMEM_LIMIT)


def _mx(v):
    return v.astype(MXU_DTYPE)


def _dot(a, b):
    return jnp.dot(_mx(a), _mx(b), preferred_element_type=F32)


def _dot_nt(a, b):
    return lax.dot_general(_mx(a), _mx(b), (((1,), (1,)), ((), ())), preferred_element_type=F32)


def _dot_hi(a, b):
    return jnp.dot(a, b, precision=HI, preferred_element_type=F32)


def _sigmoid(v):
    return 1.0 / (1.0 + jnp.exp(-v))


def _sds(shape, dtype):
    return jax.ShapeDtypeStruct(shape, dtype)


def _mm(name, a, b, tm, tn, nt=False, out_dtype=F32):
    batched = a.ndim == 3
    m, k = a.shape[-2:]
    n = b.shape[-2] if nt else b.shape[-1]
    tm, tn = min(tm, m), min(tn, n)

    def body(a_ref, b_ref, o_ref):
        f = _dot_nt if nt else _dot
        o_ref[...] = f(a_ref[...], b_ref[...]).astype(o_ref.dtype)

    if batched:
        nb = a.shape[0]
        grid = (nb, m // tm, n // tn)
        a_spec = pl.BlockSpec((None, tm, k), lambda g, i, j: (g, i, 0))
        b_spec = (pl.BlockSpec((None, tn, k), lambda g, i, j: (g, j, 0)) if nt
                  else pl.BlockSpec((None, k, tn), lambda g, i, j: (g, 0, j)))
        o_spec = pl.BlockSpec((None, tm, tn), lambda g, i, j: (g, i, j))
        out_shape = _sds((nb, m, n), out_dtype)
        sem = ("parallel", "parallel", "parallel")
    else:
        grid = (m // tm, n // tn)
        a_spec = pl.BlockSpec((tm, k), lambda i, j: (i, 0))
        b_spec = (pl.BlockSpec((tn, k), lambda i, j: (j, 0)) if nt
                  else pl.BlockSpec((k, tn), lambda i, j: (0, j)))
        o_spec = pl.BlockSpec((tm, tn), lambda i, j: (i, j))
        out_shape = _sds((m, n), out_dtype)
        sem = ("parallel", "parallel")
    return pl.pallas_call(body, name=name, grid=grid, in_specs=[a_spec, b_spec], out_specs=o_spec,
                          out_shape=out_shape, compiler_params=_cp(sem))(a, b)


def _ln_rows(y, g, b):
    mu = jnp.mean(y, axis=-1, keepdims=True)
    yc = y - mu
    var = jnp.mean(yc * yc, axis=-1, keepdims=True)
    rs = lax.rsqrt(var + LN_EPS)
    xh = yc * rs
    return xh * g + b, xh, rs


def _ln_in(x, g, b, tm=256):
    t, d = x.shape

    def body(x_ref, g_ref, b_ref, o_ref, ob_ref, xh_ref, rs_ref):
        o, xh, rs = _ln_rows(x_ref[...], g_ref[...], b_ref[...])
        o_ref[...] = o
        ob_ref[...] = o.astype(BF16)
        xh_ref[...] = xh
        rs_ref[...] = rs

    row = pl.BlockSpec((tm, d), lambda i: (i, 0))
    vec = pl.BlockSpec((1, d), lambda i: (0, 0))
    return pl.pallas_call(
        body, name="ln_in", grid=(t // tm,), in_specs=[row, vec, vec],
        out_specs=[row, row, row, pl.BlockSpec((tm, 1), lambda i: (i, 0))],
        out_shape=[_sds((t, d), F32), _sds((t, d), BF16), _sds((t, d), F32), _sds((t, 1), F32)],
        compiler_params=_cp(("parallel",)))(x, g, b)


def _mm_res_ln(name, a, w, res, g, b, tm, relu2):
    t, k = a.shape
    d = w.shape[1]

    def body(a_ref, w_ref, r_ref, g_ref, b_ref, o_ref, ob_ref, xh_ref, rs_ref, *act_ref):
        av = a_ref[...]
        if relu2:
            av = jnp.square(jnp.maximum(av, 0.0))
            act_ref[0][...] = av.astype(BF16)
        y = ALPHA * r_ref[...] + _dot(av, w_ref[...])
        o, xh, rs = _ln_rows(y, g_ref[...], b_ref[...])
        o_ref[...] = o
        ob_ref[...] = o.astype(BF16)
        xh_ref[...] = xh
        rs_ref[...] = rs

    row = pl.BlockSpec((tm, d), lambda i: (i, 0))
    vec = pl.BlockSpec((1, d), lambda i: (0, 0))
    arow = pl.BlockSpec((tm, k), lambda i: (i, 0))
    out_specs = [row, row, row, pl.BlockSpec((tm, 1), lambda i: (i, 0))]
    out_shape = [_sds((t, d), F32), _sds((t, d), BF16), _sds((t, d), F32), _sds((t, 1), F32)]
    if relu2:
        out_specs.append(arow)
        out_shape.append(_sds((t, k), BF16))
    return pl.pallas_call(
        body, name=name, grid=(t // tm,),
        in_specs=[arow, pl.BlockSpec((k, d), lambda i: (0, 0)), row, vec, vec],
        out_specs=out_specs, out_shape=out_shape, compiler_params=_cp(("parallel",)))(a, w, res, g, b)


def _merge_fwd(bo, wb, p, gate_off, tm=512, tn=512):
    _, t, d = bo.shape
    gb = gate_off // tn

    def body(bo_ref, wb_ref, g0, g1, g2, proj_ref, m_ref):
        acc = None
        for n, g_ref in enumerate((g0, g1, g2)):
            pr = _dot(bo_ref[n], wb_ref[n])
            proj_ref[n] = pr
            term = _sigmoid(g_ref[...]) * pr
            acc = term if acc is None else acc + term
        m_ref[...] = acc.astype(BF16)

    gspecs = [pl.BlockSpec((tm, tn), functools.partial(lambda i, j, n: (i, gb + n * (d // tn) + j), n=n))
              for n in range(3)]
    return pl.pallas_call(
        body, name="merge_fwd", grid=(t // tm, d // tn),
        in_specs=[pl.BlockSpec((3, tm, d), lambda i, j: (0, i, 0)),
                  pl.BlockSpec((3, d, tn), lambda i, j: (0, 0, j))] + gspecs,
        out_specs=[pl.BlockSpec((3, tm, tn), lambda i, j: (0, i, j)), pl.BlockSpec((tm, tn), lambda i, j: (i, j))],
        out_shape=[_sds((3, t, d), F32), _sds((t, d), BF16)],
        compiler_params=_cp(("parallel", "parallel")))(bo, wb, p, p, p)


def _merge_bwd(dz, wout, proj, p, gate_off, tm=512, tn=512):
    t, d = dz.shape
    gb = gate_off // tn

    def body(dz_ref, w_ref, proj_ref, g0, g1, g2, dproj_ref, dgl_ref):
        dm = _dot_nt(dz_ref[...], w_ref[...])
        for n, g_ref in enumerate((g0, g1, g2)):
            s = _sigmoid(g_ref[...])
            dproj_ref[n] = (dm * s).astype(BF16)
            dgl_ref[n] = (dm * proj_ref[n] * (s * (1.0 - s))).astype(BF16)

    gspecs = [pl.BlockSpec((tm, tn), functools.partial(lambda i, j, n: (i, gb + n * (d // tn) + j), n=n))
              for n in range(3)]
    dproj, dgl = pl.pallas_call(
        body, name="merge_bwd", grid=(t // tm, d // tn),
        in_specs=[pl.BlockSpec((tm, d), lambda i, j: (i, 0)), pl.BlockSpec((tn, d), lambda i, j: (j, 0)),
                  pl.BlockSpec((3, tm, tn), lambda i, j: (0, i, j))] + gspecs,
        out_specs=[pl.BlockSpec((3, tm, tn), lambda i, j: (0, i, j)),
                   pl.BlockSpec((3, tm, tn), lambda i, j: (0, i, j))],
        out_shape=[_sds((3, t, d), BF16), _sds((3, t, d), BF16)],
        compiler_params=_cp(("parallel", "parallel")))(dz, wout, proj, p, p, p)
    return dproj, dgl


def _mm_nt_relu2_bwd(dz, wdown, u, tm=512, tn=1024):
    t, d = dz.shape
    f = wdown.shape[0]

    def body(dz_ref, w_ref, u_ref, du_ref):
        da = _dot_nt(dz_ref[...], w_ref[...])
        du_ref[...] = (da * (2.0 * jnp.maximum(u_ref[...], 0.0))).astype(BF16)

    return pl.pallas_call(
        body, name="mlp_down_bwd", grid=(t // tm, f // tn),
        in_specs=[pl.BlockSpec((tm, d), lambda i, j: (i, 0)), pl.BlockSpec((tn, d), lambda i, j: (j, 0)),
                  pl.BlockSpec((tm, tn), lambda i, j: (i, j))],
        out_specs=pl.BlockSpec((tm, tn), lambda i, j: (i, j)), out_shape=_sds((t, f), BF16),
        compiler_params=_cp(("parallel", "parallel")))(dz, wdown, u)


def _ln_bwd_rows(dx, xh, rs, g):
    dxh = dx * g
    m1 = jnp.mean(dxh, axis=-1, keepdims=True)
    m2 = jnp.mean(dxh * xh, axis=-1, keepdims=True)
    return rs * (dxh - m1 - xh * m2)


def _mm_nt_res_lnbwd(name, a, w, dres, xh, rs, g, tm, tk):
    t, k = a.shape
    d = w.shape[0]
    nk = k // tk

    def body(a_ref, w_ref, dr_ref, xh_ref, rs_ref, g_ref, dz_ref, dzb_ref, dg_ref, db_ref, acc_ref):
        i, kk = pl.program_id(0), pl.program_id(1)

        @pl.when(kk == 0)
        def _():
            acc_ref[...] = ALPHA * dr_ref[...]

        acc_ref[...] += _dot_nt(a_ref[...], w_ref[...])

        @pl.when(jnp.logical_and(i == 0, kk == 0))
        def _():
            dg_ref[...] = jnp.zeros_like(dg_ref)
            db_ref[...] = jnp.zeros_like(db_ref)

        @pl.when(kk == nk - 1)
        def _():
            dx = acc_ref[...]
            xhv = xh_ref[...]
            dz = _ln_bwd_rows(dx, xhv, rs_ref[...], g_ref[...])
            dz_ref[...] = dz
            dzb_ref[...] = dz.astype(BF16)
            dg_ref[...] += jnp.sum(dx * xhv, axis=0, keepdims=True)
            db_ref[...] += jnp.sum(dx, axis=0, keepdims=True)

    row = pl.BlockSpec((tm, d), lambda i, kk: (i, 0))
    vec = pl.BlockSpec((1, d), lambda i, kk: (0, 0))
    return pl.pallas_call(
        body, name=name, grid=(t // tm, nk),
        in_specs=[pl.BlockSpec((tm, tk), lambda i, kk: (i, kk)), pl.BlockSpec((d, tk), lambda i, kk: (0, kk)),
                  row, row, pl.BlockSpec((tm, 1), lambda i, kk: (i, 0)), vec],
        out_specs=[row, row, vec, vec],
        out_shape=[_sds((t, d), F32), _sds((t, d), BF16), _sds((1, d), F32), _sds((1, d), F32)],
        scratch_shapes=[pltpu.VMEM((tm, d), F32)],
        compiler_params=_cp(("arbitrary", "arbitrary")))(a, w, dres, xh, rs, g)


def _loss_ln_bwd(x2, target, xh, rs, g, tm=256):
    t, d = x2.shape

    def body(x_ref, t_ref, xh_ref, rs_ref, g_ref, loss_ref, dz_ref, dzb_ref, dg_ref, db_ref):
        @pl.when(pl.program_id(0) == 0)
        def _():
            loss_ref[...] = jnp.zeros_like(loss_ref)
            dg_ref[...] = jnp.zeros_like(dg_ref)
            db_ref[...] = jnp.zeros_like(db_ref)

        err = x_ref[...] - t_ref[...]
        per_row = jnp.mean(err * err, axis=-1, keepdims=True)
        loss_ref[...] += 0.5 * jnp.sum(per_row, axis=0, keepdims=True)
        dx = err * (1.0 / d)
        xhv = xh_ref[...]
        dz = _ln_bwd_rows(dx, xhv, rs_ref[...], g_ref[...])
        dz_ref[...] = dz
        dzb_ref[...] = dz.astype(BF16)
        dg_ref[...] += jnp.sum(dx * xhv, axis=0, keepdims=True)
        db_ref[...] += jnp.sum(dx, axis=0, keepdims=True)

    row = pl.BlockSpec((tm, d), lambda i: (i, 0))
    vec = pl.BlockSpec((1, d), lambda i: (0, 0))
    return pl.pallas_call(
        body, name="loss_ln_bwd", grid=(t // tm,),
        in_specs=[row, row, row, pl.BlockSpec((tm, 1), lambda i: (i, 0)), vec],
        out_specs=[pl.BlockSpec((1, LANE), lambda i: (0, 0)), row, row, vec, vec],
        out_shape=[_sds((1, LANE), F32), _sds((t, d), F32), _sds((t, d), BF16), _sds((1, d), F32),
                   _sds((1, d), F32)],
        compiler_params=_cp(("arbitrary",)))(x2, target, xh, rs, g)


def _attn_scores(q_ref, k_refs, bias_ref, i, dh):
    q = q_ref[...] * (dh ** -0.5)
    k = jnp.concatenate([r[...] for r in k_refs], axis=0)
    s = _dot_nt(q, k) + bias_ref[...]
    col = lax.broadcasted_iota(jnp.int32, s.shape, 1)
    s = jnp.where(col >= (2 - i) * QB, s, NEG_INF)
    m = jnp.max(s, axis=-1, keepdims=True)
    e = jnp.exp(s - m)
    return q, k, e / jnp.sum(e, axis=-1, keepdims=True)


def _attn_specs(dh, qcol, kcol, vcol):
    q_spec = pl.BlockSpec((QB, dh), lambda h, i: (i, qcol + h))
    k_specs = [pl.BlockSpec((QB, dh), functools.partial(lambda h, i, j: (jnp.maximum(i - 2 + j, 0), kcol + h), j=j))
               for j in range(3)]
    v_specs = [pl.BlockSpec((QB, dh), functools.partial(lambda h, i, j: (jnp.maximum(i - 2 + j, 0), vcol + h), j=j))
               for j in range(3)]
    bias_spec = pl.BlockSpec((None, QB, KW), lambda h, i: (h, 0, 0))
    return q_spec, k_specs, v_specs, bias_spec


def _attn_fwd(p, bias, d, off):
    t = p.shape[0]
    dh = d // ATTN_HEADS

    def body(q_ref, k0, k1, k2, v0, v1, v2, bias_ref, o_ref):
        _, _, pr = _attn_scores(q_ref, (k0, k1, k2), bias_ref, pl.program_id(1), dh)
        v = jnp.concatenate([v0[...], v1[...], v2[...]], axis=0)
        o_ref[...] = _dot(pr, v).astype(o_ref.dtype)

    q_spec, k_specs, v_specs, bias_spec = _attn_specs(dh, off["aq"] // dh, off["ak"] // dh, off["av"] // dh)
    return pl.pallas_call(
        body, name="attn_fwd", grid=(ATTN_HEADS, t // QB),
        in_specs=[q_spec] + k_specs + v_specs + [bias_spec],
        out_specs=pl.BlockSpec((QB, dh), lambda h, i: (i, h)), out_shape=_sds((t, d), BF16),
        compiler_params=_cp(("parallel", "parallel")))(p, p, p, p, p, p, p, bias)


def _attn_bwd(p, bias, do, d, off):
    t = p.shape[0]
    dh = d // ATTN_HEADS
    tp = t + 2 * QB

    def body(q_ref, k0, k1, k2, v0, v1, v2, bias_ref, do_ref, dq_ref, dk_ref, dv_ref, dbias_ref):
        i = pl.program_id(1)

        @pl.when(i == 0)
        def _():
            dk_ref[...] = jnp.zeros_like(dk_ref)
            dv_ref[...] = jnp.zeros_like(dv_ref)
            dbias_ref[...] = jnp.zeros_like(dbias_ref)

        q, k, pr = _attn_scores(q_ref, (k0, k1, k2), bias_ref, i, dh)
        v = jnp.concatenate([v0[...], v1[...], v2[...]], axis=0)
        dov = do_ref[...]
        dp = _dot_nt(dov, v)
        delta = jnp.sum(pr * dp, axis=-1, keepdims=True)
        ds = pr * (dp - delta)
        dbias_ref[...] += ds
        dq_ref[...] = (_dot(ds, k) * (dh ** -0.5)).astype(dq_ref.dtype)
        rows = pl.ds(pl.multiple_of(i * QB, QB), KW)
        dk_ref[rows, :] += _dot(ds.T, q)
        dv_ref[rows, :] += _dot(pr.T, dov)

    q_spec, k_specs, v_specs, bias_spec = _attn_specs(dh, off["aq"] // dh, off["ak"] // dh, off["av"] // dh)
    acc_spec = pl.BlockSpec((tp, dh), lambda h, i: (0, h))
    return pl.pallas_call(
        body, name="attn_bwd", grid=(ATTN_HEADS, t // QB),
        in_specs=[q_spec] + k_specs + v_specs + [bias_spec, pl.BlockSpec((QB, dh), lambda h, i: (i, h))],
        out_specs=[pl.BlockSpec((QB, dh), lambda h, i: (i, h)), acc_spec, acc_spec, bias_spec],
        out_shape=[_sds((t, d), BF16), _sds((tp, d), F32), _sds((tp, d), F32),
                   _sds((ATTN_HEADS, QB, KW), F32)],
        compiler_params=_cp(("parallel", "arbitrary")))(p, p, p, p, p, p, p, bias, do)


def _onehot_mm(name, a, b):
    def body(a_ref, b_ref, o_ref):
        o_ref[...] = _dot_hi(a_ref[...], b_ref[...])

    return pl.pallas_call(body, name=name, out_shape=_sds((a.shape[0], b.shape[1]), F32),
                          compiler_params=pltpu.CompilerParams(vmem_limit_bytes=VMEM_LIMIT))(a, b)


def _diag_index():
    ii, jj = np.arange(CHUNK)[:, None], np.arange(CHUNK)[None, :]
    return (ii - jj + CHUNK - 1).reshape(-1)


def _bias_expand(rel_bias):
    h = rel_bias.shape[0]
    nq, nk, shift = QB // CHUNK, KW // CHUNK, (2 * QB) // CHUNK
    nbin, ndc = 3 * LANE, 4
    rb = jnp.pad(rel_bias, ((0, 0), (0, nbin - rel_bias.shape[1])))
    win = np.clip(CHUNK * np.arange(ndc)[:, None] + np.arange(LANE)[None, :] - (CHUNK - 1), -REL_CLIP, REL_CLIP)
    sel = (jnp.arange(nbin)[:, None] == jnp.asarray((win + REL_CLIP).reshape(1, -1))).astype(F32)
    windows = _onehot_mm("bias_windows", rb, sel)
    diag_t = (jnp.arange(LANE)[:, None] == jnp.asarray(_diag_index().reshape(1, -1))).astype(F32)
    blocks = _onehot_mm("bias_blocks", windows.reshape(h * ndc, LANE), diag_t).reshape(h, ndc, CHUNK, CHUNK)
    off_band = jnp.full((h, CHUNK, CHUNK), NEG_INF, F32)
    rows = []
    for ic in range(nq):
        dcs = [ic - jc + shift for jc in range(nk)]
        rows.append(jnp.concatenate([blocks[:, min(dc, ndc - 1)] if 0 <= dc <= ATTN_LEFT else off_band
                                     for dc in dcs], axis=2))
    return jnp.concatenate(rows, axis=1)


def _bias_reduce(dbias):
    h = dbias.shape[0]
    nq, nk = QB // CHUNK, KW // CHUNK
    nbin = 3 * LANE
    blocks = dbias.reshape(h, nq, CHUNK, nk, CHUNK).transpose(0, 1, 3, 2, 4).reshape(h * nq * nk, CHUNK * CHUNK)
    diag = (jnp.asarray(_diag_index().reshape(-1, 1)) == jnp.arange(LANE)[None, :]).astype(F32)
    ic = np.arange(nq)[:, None, None]
    jc = np.arange(nk)[None, :, None]
    dl = np.arange(LANE)[None, None, :] - (CHUNK - 1)
    rel = np.clip(CHUNK * (ic - jc + (2 * QB) // CHUNK) + dl, -REL_CLIP, REL_CLIP) + REL_CLIP
    bins = (jnp.asarray(rel.reshape(-1, 1)) == jnp.arange(nbin)[None, :]).astype(F32)

    diags = _onehot_mm("bias_diag_sums", blocks, diag)
    out = _onehot_mm("bias_bin_sums", diags.reshape(h, nq * nk * LANE), bins)
    return out[:, :2 * REL_CLIP + 1]


def _tri(lower):
    r = lax.broadcasted_iota(jnp.int32, (CHUNK, CHUNK), 0)
    c = lax.broadcasted_iota(jnp.int32, (CHUNK, CHUNK), 1)
    return (r >= c) if lower else (r <= c)


def _lin_prep(gla, q, k, aux):
    dk = q.shape[-1]
    if gla:
        glr, wlr, blr = aux
        q = q * (dk ** -0.5)
        pre = _dot(glr, wlr) + blr
        log_a = (jnp.minimum(pre, 0.0) - jnp.log(1.0 + jnp.exp(-jnp.abs(pre)))) / GATE_NORM
        b = _dot_hi(_tri(True).astype(F32), log_a)
        return q, k, b, pre
    cs, sn, lg = aux
    half = dk // 2
    q = q * cs + pltpu.roll(q, half, 1) * sn
    k = (k * cs + pltpu.roll(k, half, 1) * sn) * (dk ** -0.5)
    pos = lax.broadcasted_iota(jnp.int32, (CHUNK, dk), 0).astype(F32) + 1.0
    return q, k, pos * lg, None


def _lin_chunk(q, k, v, b, st):
    eb, enb = jnp.exp(b), jnp.exp(-b)
    last = b[CHUNK - 1:CHUNK, :]
    qf, kf, qb, kb = q * eb, k * enb, q * enb, k * eb
    kl = k * jnp.exp(last - b)
    s = jnp.where(_tri(True), _dot_nt(qf, kf), _dot_nt(qb, kb))
    o = _dot(s, v) + _dot_nt(qf, st)
    st_new = st * jnp.exp(last) + _dot(v.T, kl)
    return o, st_new, (eb, enb, last, qf, kf, qb, kb, kl, s)


def _lin_norm_gate(gla, o, gate, gn):
    sg = _sigmoid(gate)
    silu = gate * sg
    if gla:
        r = lax.rsqrt(jnp.mean(o * o, axis=-1, keepdims=True) + LN_EPS)
        hn = o * r
        return silu * (hn * gn), (sg, silu, r, hn)
    mu = jnp.mean(o, axis=-1, keepdims=True)
    oc = o - mu
    r = lax.rsqrt(jnp.mean(oc * oc, axis=-1, keepdims=True) + LN_EPS)
    hn = oc * r
    return silu * hn, (sg, silu, r, hn)


def _lin_specs(gla, dk, dv, off, rev, nb):
    pre = "g" if gla else "r"
    qc, kc, vc, gc = (off[pre + "q"] // dk, off[pre + "k"] // dk, off[pre + "v"] // dv, off[pre + "g"] // dv)

    def blk(i):
        return nb - 1 - i if rev else i

    specs = [pl.BlockSpec((LB, dk), lambda h, i: (blk(i), qc + h)),
             pl.BlockSpec((LB, dk), lambda h, i: (blk(i), kc + h)),
             pl.BlockSpec((LB, dv), lambda h, i: (blk(i), vc + h)),
             pl.BlockSpec((LB, dv), lambda h, i: (blk(i), gc + h))]
    if gla:
        specs += [pl.BlockSpec((LB, LANE), lambda h, i: (blk(i), off["glr"] // LANE)),
                  pl.BlockSpec((LANE, dk), lambda h, i: (0, h)),
                  pl.BlockSpec((1, dk), lambda h, i: (0, h)),
                  pl.BlockSpec((1, dv), lambda h, i: (0, 0))]
    else:
        specs += [pl.BlockSpec((LB, dk), lambda h, i: (blk(i), 0)),
                  pl.BlockSpec((LB, dk), lambda h, i: (blk(i), 0)),
                  pl.BlockSpec((None, 1, dk), lambda h, i: (h, 0, 0))]
    return specs, blk


def _lin_aux(gla, refs, rows):
    if gla:
        glr_ref, wlr_ref, blr_ref, gn_ref = refs
        return (glr_ref[rows, :], wlr_ref[...], blr_ref[...]), gn_ref[...]
    cs_ref, sn_ref, lg_ref = refs
    return (cs_ref[rows, :], sn_ref[rows, :], lg_ref[...]), None


def _lin_fwd(gla, p, aux_arrays, d, off):
    t = p.shape[0]
    dk, dv = d // (2 * LIN_HEADS), d // LIN_HEADS
    nb, cb = t // LB, LB // CHUNK
    naux = len(aux_arrays)

    def body(*refs):
        q_ref, k_ref, v_ref, g_ref = refs[:4]
        aux_refs = refs[4:4 + naux]
        o_ref, bo_ref, st_out_ref, st_ref = refs[4 + naux:]

        @pl.when(pl.program_id(1) == 0)
        def _():
            st_ref[...] = jnp.zeros_like(st_ref)

        for c in range(cb):
            rows = pl.ds(c * CHUNK, CHUNK)
            aux, gn = _lin_aux(gla, aux_refs, rows)
            q, k, b, _ = _lin_prep(gla, q_ref[rows, :], k_ref[rows, :], aux)
            st = st_ref[...]
            st_out_ref[c] = st
            o, st_new, _ = _lin_chunk(q, k, v_ref[rows, :], b, st)
            st_ref[...] = st_new
            o_ref[rows, :] = o
            out, _ = _lin_norm_gate(gla, o, g_ref[rows, :], gn)
            bo_ref[rows, :] = out.astype(BF16)

    specs, _ = _lin_specs(gla, dk, dv, off, False, nb)
    orow = pl.BlockSpec((LB, dv), lambda h, i: (i, h))
    return pl.pallas_call(
        body, name="gla_fwd" if gla else "ret_fwd", grid=(LIN_HEADS, nb), in_specs=specs,
        out_specs=[orow, orow, pl.BlockSpec((None, cb, dv, dk), lambda h, i: (h, i, 0, 0))],
        out_shape=[_sds((t, d), F32), _sds((t, d), BF16), _sds((LIN_HEADS, t // CHUNK, dv, dk), F32)],
        scratch_shapes=[pltpu.VMEM((dv, dk), F32)],
        compiler_params=_cp(("parallel", "arbitrary")))(p, p, p, p, *aux_arrays)


def _lin_bwd(gla, p, aux_arrays, o, states, dbo, d, off):
    t = p.shape[0]
    dk, dv = d // (2 * LIN_HEADS), d // LIN_HEADS
    nb, cb = t // LB, LB // CHUNK
    naux = len(aux_arrays)

    def body(*refs):
        q_ref, k_ref, v_ref, g_ref = refs[:4]
        aux_refs = refs[4:4 + naux]
        o_ref, st_in_ref, dbo_ref = refs[4 + naux:7 + naux]
        outs = refs[7 + naux:]
        dq_ref, dk_ref, dv_ref, dg_ref = outs[:4]
        dst_ref = outs[-1]
        first = pl.program_id(1) == 0

        @pl.when(first)
        def _():
            dst_ref[...] = jnp.zeros_like(dst_ref)

        if gla:
            dpre_ref, dblr_ref, dgn_ref = outs[4:7]

            @pl.when(first)
            def _():
                dblr_ref[...] = jnp.zeros_like(dblr_ref)
                dgn_ref[...] = jnp.zeros_like(dgn_ref)

        for c in reversed(range(cb)):
            rows = pl.ds(c * CHUNK, CHUNK)
            aux, gn = _lin_aux(gla, aux_refs, rows)
            q, k, b, pre = _lin_prep(gla, q_ref[rows, :], k_ref[rows, :], aux)
            v = v_ref[rows, :]
            st = st_in_ref[c]
            _, _, (eb, enb, last, qf, kf, qb, kb, kl, s) = _lin_chunk(q, k, v, b, st)
            gate = g_ref[rows, :]
            dout = dbo_ref[rows, :]
            _, (sg, silu, r, hn) = _lin_norm_gate(gla, o_ref[rows, :], gate, gn)
            dsilu = sg * (1.0 + gate * (1.0 - sg))
            if gla:
                y = hn * gn
                dy = dout * silu
                dg_ref[rows, :] = (dout * y * dsilu).astype(BF16)
                dgn_ref[...] += jnp.sum(dy * hn, axis=0, keepdims=True)
                dhn = dy * gn
                do = r * (dhn - hn * jnp.mean(dhn * hn, axis=-1, keepdims=True))
            else:
                dhn = dout * silu
                dg_ref[rows, :] = (dout * hn * dsilu).astype(BF16)
                do = r * (dhn - jnp.mean(dhn, axis=-1, keepdims=True)
                          - hn * jnp.mean(dhn * hn, axis=-1, keepdims=True))
            dstn = dst_ref[...]
            dec = jnp.exp(last)
            ds = _dot_nt(do, v)
            low = _tri(True)
            dsf = jnp.where(low, ds, 0.0)
            dsb = jnp.where(low, 0.0, ds)
            dvv = _dot(s.T, do) + _dot_nt(kl, dstn)
            dqf = _dot(dsf, kf) + _dot(do, st)
            dkf = _dot(dsf.T, qf)
            dqb = _dot(dsb, kb)
            dkb = _dot(dsb.T, qb)
            dkl = _dot(v, dstn)
            dst_ref[...] = dstn * dec + _dot(do.T, qf)
            dq = dqf * eb + dqb * enb
            dkk = dkf * enb + dkb * eb + dkl * jnp.exp(last - b)
            dv_ref[rows, :] = dvv.astype(BF16)
            if gla:
                ddec = jnp.sum(dstn * st, axis=0, keepdims=True)
                db = dqf * qf - dkf * kf - dqb * qb + dkb * kb - dkl * kl
                dlast = jnp.sum(dkl * kl, axis=0, keepdims=True) + ddec * dec
                rowi = lax.broadcasted_iota(jnp.int32, db.shape, 0)
                db = db + jnp.where(rowi == CHUNK - 1, dlast, 0.0)
                dlog_a = _dot_hi(_tri(False).astype(F32), db)
                dpre = dlog_a * (1.0 / GATE_NORM) * (1.0 - _sigmoid(pre))
                dpre_ref[rows, :] = dpre
                dblr_ref[...] += jnp.sum(dpre, axis=0, keepdims=True)
                dq_ref[rows, :] = (dq * (dk ** -0.5)).astype(BF16)
                dk_ref[rows, :] = dkk.astype(BF16)
            else:
                cs, sn, _ = aux
                half = dk // 2
                dkk = dkk * (dk ** -0.5)
                dq_ref[rows, :] = (dq * cs + pltpu.roll(dq * sn, half, 1)).astype(BF16)
                dk_ref[rows, :] = (dkk * cs + pltpu.roll(dkk * sn, half, 1)).astype(BF16)

    specs, blk = _lin_specs(gla, dk, dv, off, True, nb)
    vrow = pl.BlockSpec((LB, dv), lambda h, i: (blk(i), h))
    krow = pl.BlockSpec((LB, dk), lambda h, i: (blk(i), h))
    specs += [vrow, pl.BlockSpec((None, cb, dv, dk), lambda h, i: (h, blk(i), 0, 0)), vrow]
    out_specs = [krow, krow, vrow, vrow]
    out_shape = [_sds((t, d // 2), BF16), _sds((t, d // 2), BF16), _sds((t, d), BF16), _sds((t, d), BF16)]
    if gla:
        out_specs += [krow, pl.BlockSpec((None, 1, dk), lambda h, i: (h, 0, 0)),
                      pl.BlockSpec((None, 1, dv), lambda h, i: (h, 0, 0))]
        out_shape += [_sds((t, d // 2), F32), _sds((LIN_HEADS, 1, dk), F32), _sds((LIN_HEADS, 1, dv), F32)]
    out_specs.append(pl.BlockSpec((None, dv, dk), lambda h, i: (h, 0, 0)))
    out_shape.append(_sds((LIN_HEADS, dv, dk), F32))
    res = pl.pallas_call(
        body, name="gla_bwd" if gla else "ret_bwd", grid=(LIN_HEADS, nb), in_specs=specs,
        out_specs=out_specs, out_shape=out_shape,
        compiler_params=_cp(("parallel", "arbitrary")))(p, p, p, p, *aux_arrays, o, states, dbo)
    return res[:-1]


def _row_tile(rows, cols):
    cap = max(8, (2 << 20) // (4 * cols))
    t = rows
    while t > cap and t % 2 == 0:
        t //= 2
    return t


def _add_half(name, g, t, sel):
    nchip, hr, cols = t.shape
    tr = _row_tile(hr, cols)
    nb = hr // tr

    def body(sel_ref, g_ref, t_ref, o_ref):
        o_ref[...] = g_ref[...] + t_ref[...]

    half = pl.BlockSpec((None, tr, cols), lambda p, i, s: (p, i, 0))
    gs = pltpu.PrefetchScalarGridSpec(
        num_scalar_prefetch=1, grid=(nchip, nb),
        in_specs=[pl.BlockSpec((None, tr, cols), lambda p, i, s: (p, s[0] * nb + i, 0)), half], out_specs=half)
    return pl.pallas_call(body, name=name, grid_spec=gs, out_shape=_sds(t.shape, F32),
                          compiler_params=_cp(("parallel", "parallel")))(sel, g, t)


def _sum_shards(name, h, rcv, sel):
    _, rows, cols = h.shape
    tr = _row_tile(rows, cols)

    def body(sel_ref, h_ref, r0, r1, r2, o_ref):
        o_ref[...] = ((h_ref[...] + r0[...]) + r1[...]) + r2[...]

    rspecs = [pl.BlockSpec((None, tr, cols), functools.partial(lambda i, s, j: (j, i, 0), j=j)) for j in range(3)]
    gs = pltpu.PrefetchScalarGridSpec(
        num_scalar_prefetch=1, grid=(rows // tr,),
        in_specs=[pl.BlockSpec((None, tr, cols), lambda i, s: (s[0], i, 0))] + rspecs,
        out_specs=pl.BlockSpec((tr, cols), lambda i, s: (i, 0)))
    return pl.pallas_call(body, name=name, grid_spec=gs, out_shape=_sds((rows, cols), F32),
                          compiler_params=_cp(("parallel",)))(sel, h, rcv, rcv, rcv)


def _adamw_math(w, g, m, v):
    c1 = 1.0 - ADAM_B1 ** ADAM_STEP
    c2 = 1.0 - ADAM_B2 ** ADAM_STEP
    nm = ADAM_B1 * m + (1.0 - ADAM_B1) * g
    nv = ADAM_B2 * v + (1.0 - ADAM_B2) * jnp.square(g)
    return -ADAM_LR * ((nm / c1) / (jnp.sqrt(nv / c2) + ADAM_EPS) + ADAM_WD * w), nm, nv


def _adamw(name, w, g, m, v):
    rows, cols = w.shape
    tr = _row_tile(rows, cols)

    def body(w_ref, g_ref, m_ref, v_ref, d_ref, nm_ref, nv_ref):
        d_ref[...], nm_ref[...], nv_ref[...] = _adamw_math(w_ref[...], g_ref[...], m_ref[...], v_ref[...])

    spec = pl.BlockSpec((tr, cols), lambda i: (i, 0))
    return pl.pallas_call(body, name=name, grid=(rows // tr,), in_specs=[spec] * 4, out_specs=[spec] * 3,
                          out_shape=[_sds((rows, cols), F32)] * 3, compiler_params=_cp(("parallel",)))(w, g, m, v)


def _adamw_layer(name, w, g, m, v, layer, prev):
    depth, rows, cols = w.shape
    tr = _row_tile(rows, cols)
    nprev = 0 if prev is None else 4

    def body(w_ref, g_ref, m_ref, v_ref, *rest):
        go_ref, d_ref, nm_ref, nv_ref = rest[nprev:]
        gv = g_ref[...]
        go_ref[...] = gv
        d_ref[...], nm_ref[...], nv_ref[...] = _adamw_math(w_ref[...], gv, m_ref[...], v_ref[...])

    lay = pl.BlockSpec((None, tr, cols), lambda i: (layer, i, 0))
    in_specs = [lay, pl.BlockSpec((tr, cols), lambda i: (i, 0)), lay, lay] + [ANY] * nprev
    args = (w, g, m, v) + (() if prev is None else tuple(prev))
    return pl.pallas_call(
        body, name=name, grid=(rows // tr,), in_specs=in_specs, out_specs=[lay] * 4,
        out_shape=[_sds((depth, rows, cols), F32)] * 4,
        input_output_aliases={4 + k: k for k in range(nprev)},
        compiler_params=_cp(("parallel",)))(*args)


def _place():
    x, y, c = (lax.axis_index(a) for a in MESH_AXES)
    chips = [(1 - x, y), (x, 1 - y), (1 - x, 1 - y)]
    return x, y, c, chips


def _chip_index(xy):
    return 2 * xy[0] + xy[1]


ANY = pl.BlockSpec(memory_space=pl.ANY)


HBM_SPEC = pl.BlockSpec(memory_space=pltpu.HBM)
SEM = pl.BlockSpec(memory_space=pltpu.SEMAPHORE)
EFFECT = pltpu.SideEffectType.DATAFLOW_SIDE_EFFECTING


def _half(ref, c):
    hr = ref.shape[-2] // 2
    return pl.ds(pl.multiple_of(c * hr, 16), hr)


def _gather_copies(srcs, lands, send, recv):
    x, y, c, chips = _place()
    me = _chip_index((x, y))
    return [pltpu.make_async_remote_copy(src_ref=s.at[_half(s, c)], dst_ref=g.at[me, _half(s, c)],
                                         send_sem=send.at[3 * a + j], recv_sem=recv.at[3 * a + j],
                                         device_id=(*ch, c), device_id_type=DEV)
            for a, (s, g) in enumerate(zip(srcs, lands)) for j, ch in enumerate(chips)]


def _scatter_copies(srcs, lands, send, recv):
    x, y, c, chips = _place()
    return [pltpu.make_async_remote_copy(src_ref=h.at[_chip_index(ch)], dst_ref=r.at[j],
                                         send_sem=send.at[3 * a + j], recv_sem=recv.at[3 * a + j],
                                         device_id=(*ch, c), device_id_type=DEV)
            for a, (h, r) in enumerate(zip(srcs, lands)) for j, ch in enumerate(chips)]


def _in_hbm(a):
    return pltpu.with_memory_space_constraint(a, pltpu.HBM)


def _split_start(name, srcs, land_shapes, copies_fn):
    ns, nl = len(srcs), len(land_shapes)
    ncp = 3 * ns
    lands = [lax.empty(s.shape, s.dtype) for s in land_shapes]

    def body(*refs):
        src, land = refs[:ns], refs[ns:ns + nl]
        send, recv = refs[ns + nl], refs[ns + nl + 1]
        for cp in copies_fn(src, land, send, recv):
            cp.start()
        refs[-1][...] = jnp.zeros_like(refs[-1])

    bufs = list(srcs) + lands
    outs = pl.pallas_call(
        body, name=name, in_specs=[HBM_SPEC] * (ns + nl),
        out_specs=[SEM, SEM] + [HBM_SPEC] * (ns + nl) + [pl.BlockSpec(memory_space=pltpu.VMEM)],
        out_shape=[pltpu.SemaphoreType.DMA((ncp,)), pltpu.SemaphoreType.DMA((ncp,))]
        + [pltpu.HBM(b.shape, b.dtype) for b in bufs] + [_sds((8, LANE), F32)],
        input_output_aliases={i: 2 + i for i in range(ns + nl)},
        compiler_params=pltpu.CompilerParams(has_side_effects=EFFECT))(*[_in_hbm(b) for b in bufs])
    return outs[0], outs[1], list(outs[2:2 + ns]), list(outs[2 + ns:2 + ns + nl]), outs[-1]


def _split_wait(name, started, copies_fn, after):
    send, recv, srcs, lands, _ = started
    ns, nl = len(srcs), len(lands)

    def body(*refs):
        src, land = refs[:ns], refs[ns:ns + nl]
        for cp in copies_fn(src, land, refs[ns + nl], refs[ns + nl + 1]):
            cp.wait_send()
            cp.wait_recv()

    bufs = list(srcs) + list(lands)
    outs = pl.pallas_call(
        body, name=name, in_specs=[HBM_SPEC] * (ns + nl) + [SEM, SEM, ANY], out_specs=[HBM_SPEC] * (ns + nl),
        out_shape=[pltpu.HBM(b.shape, b.dtype) for b in bufs],
        input_output_aliases={i: i for i in range(ns + nl)},
        compiler_params=pltpu.CompilerParams(has_side_effects=EFFECT))(*bufs, send, recv, after)
    return list(outs[:ns]), list(outs[ns:])


def _gather_plain(name, srcs):
    n = len(srcs)

    def body(*refs):
        src, land = refs[:n], refs[n:2 * n]
        send, recv, fsend, frecv, lsem = refs[2 * n:]
        first = _gather_copies(src, land, send, recv)
        for cp in first:
            cp.start()
        _forward_body(src, land, first, fsend, frecv, lsem)

    return pl.pallas_call(
        body, name=name, in_specs=[ANY] * n, out_specs=[ANY] * n,
        out_shape=[_sds((4,) + s.shape, s.dtype) for s in srcs],
        scratch_shapes=[pltpu.SemaphoreType.DMA((3 * n,))] * 4 + [pltpu.SemaphoreType.DMA((n,))])(*srcs)


def _forward_body(src, land, arrivals, fsend, frecv, lsem):
    x, y, c, chips = _place()
    me = _chip_index((x, y))
    n = len(src)
    local = [pltpu.make_async_copy(src[a], land[a].at[me], lsem.at[a]) for a in range(n)]
    for cp in local:
        cp.start()
    passed = []
    for a in range(n):
        for j, ch in enumerate(chips):
            if arrivals is not None:
                arrivals[3 * a + j].wait_recv()
            slot = land[a].at[_chip_index(ch), _half(src[a], c)]
            fw = pltpu.make_async_remote_copy(src_ref=slot, dst_ref=slot, send_sem=fsend.at[3 * a + j],
                                              recv_sem=frecv.at[3 * a + j], device_id=(x, y, 1 - c),
                                              device_id_type=DEV)
            fw.start()
            passed.append(fw)
    for a in range(n):
        for j, ch in enumerate(chips):
            slot = land[a].at[_chip_index(ch), _half(src[a], 1 - c)]
            pltpu.make_async_remote_copy(src_ref=slot, dst_ref=slot, send_sem=fsend.at[3 * a + j],
                                         recv_sem=frecv.at[3 * a + j], device_id=(x, y, c),
                                         device_id_type=DEV).wait_recv()
    for cp in passed:
        cp.wait_send()
    if arrivals is not None:
        for cp in arrivals:
            cp.wait_send()
    for cp in local:
        cp.wait()


def _gather_forward(name, srcs, lands):
    n = len(srcs)

    def body(*refs):
        src, land = refs[:n], refs[2 * n:3 * n]
        fsend, frecv, lsem = refs[3 * n:]
        _forward_body(src, land, None, fsend, frecv, lsem)

    return pl.pallas_call(
        body, name=name, in_specs=[ANY] * (2 * n), out_specs=[ANY] * n,
        out_shape=[_sds(g.shape, g.dtype) for g in lands], input_output_aliases={n + a: a for a in range(n)},
        scratch_shapes=[pltpu.SemaphoreType.DMA((3 * n,))] * 2 + [pltpu.SemaphoreType.DMA((n,))])(*srcs, *lands)


def _sibling_halves(name, grs):
    n = len(grs)

    def body(*refs):
        ins, outs = refs[:n], refs[n:2 * n]
        send, recv = refs[2 * n:]
        x, y, c, _ = _place()
        cps = [pltpu.make_async_remote_copy(src_ref=ins[a].at[:, _half(ins[a], 1 - c)], dst_ref=outs[a],
                                            send_sem=send.at[a], recv_sem=recv.at[a], device_id=(x, y, 1 - c),
                                            device_id_type=DEV) for a in range(n)]
        for cp in cps:
            cp.start()
        for cp in cps:
            cp.wait()

    return pl.pallas_call(
        body, name=name, in_specs=[ANY] * n, out_specs=[ANY] * n,
        out_shape=[_sds((g.shape[0], g.shape[1] // 2, g.shape[2]), F32) for g in grs],
        scratch_shapes=[pltpu.SemaphoreType.DMA((n,)), pltpu.SemaphoreType.DMA((n,))])(*grs)


def _sibling_share(name, sms):
    n = len(sms)

    def body(*refs):
        ins, outs = refs[:n], refs[n:2 * n]
        send, recv, lsem = refs[2 * n:]
        x, y, c, _ = _place()
        local = [pltpu.make_async_copy(ins[a], outs[a].at[_half(outs[a], c)], lsem.at[a]) for a in range(n)]
        cps = [pltpu.make_async_remote_copy(src_ref=ins[a], dst_ref=outs[a].at[_half(outs[a], c)],
                                            send_sem=send.at[a], recv_sem=recv.at[a], device_id=(x, y, 1 - c),
                                            device_id_type=DEV) for a in range(n)]
        for cp in local + cps:
            cp.start()
        for a in range(n):
            cps[a].wait_send()
            pltpu.make_async_remote_copy(src_ref=ins[a], dst_ref=outs[a].at[_half(outs[a], 1 - c)],
                                         send_sem=send.at[a], recv_sem=recv.at[a], device_id=(x, y, c),
                                         device_id_type=DEV).wait_recv()
            local[a].wait()

    return pl.pallas_call(
        body, name=name, in_specs=[ANY] * n, out_specs=[ANY] * n,
        out_shape=[_sds((2 * s.shape[0], s.shape[1]), F32) for s in sms],
        scratch_shapes=[pltpu.SemaphoreType.DMA((n,))] * 3)(*sms)


def _small_allreduce(v):
    rows = v.shape[0]
    ndev = 8

    def body(v_ref, o_ref, gat_ref, send, recv):
        x, y, c, _ = _place()
        me = 4 * x + 2 * y + c
        cps = []
        for k in range(1, ndev):
            to = (me + k) % ndev
            cp = pltpu.make_async_remote_copy(src_ref=v_ref, dst_ref=gat_ref.at[me], send_sem=send.at[k - 1],
                                              recv_sem=recv.at[me], device_id=(to // 4, (to // 2) % 2, to % 2),
                                              device_id_type=DEV)
            cp.start()
            cps.append(cp)
        gat_ref[me] = v_ref[...]
        for k in range(1, ndev):
            frm = (me + k) % ndev
            pltpu.make_async_remote_copy(src_ref=v_ref, dst_ref=gat_ref.at[frm], send_sem=send.at[k - 1],
                                         recv_sem=recv.at[frm], device_id=(x, y, c), device_id_type=DEV).wait_recv()
        for cp in cps:
            cp.wait_send()
        acc = gat_ref[0]
        for k in range(1, ndev):
            acc = acc + gat_ref[k]
        o_ref[...] = acc

    vm = pl.BlockSpec(memory_space=pltpu.VMEM)
    return pl.pallas_call(
        body, name="small_allreduce", in_specs=[vm], out_specs=vm, out_shape=_sds((rows, LANE), F32),
        scratch_shapes=[pltpu.VMEM((ndev, rows, LANE), F32), pltpu.SemaphoreType.DMA((ndev - 1,)),
                        pltpu.SemaphoreType.DMA((ndev,))])(v)


def _layout(d):
    half = d // 2
    names = [("aq", d), ("ak", d), ("av", d), ("rq", half), ("rk", half), ("rv", d), ("rg", d),
             ("gq", half), ("gk", half), ("gv", d), ("gg", d), ("gates", 3 * d), ("glr", 2 * LANE)]
    off, pos = {}, 0
    for nm, sz in names:
        off[nm] = pos
        pos += sz
    return off, pos


def _pad_cols(w, d):
    a = 8 * d + d
    lr = w[..., a:a + GATE_RANK]
    z = jnp.zeros(w.shape[:-1] + (2 * LANE - GATE_RANK,), w.dtype)
    return jnp.concatenate([w[..., :a], w[..., a + GATE_RANK:], lr, z], axis=-1)


def _unpad_cols(g, d):
    a = 8 * d + d
    return jnp.concatenate([g[..., :a], g[..., a + 3 * d:a + 3 * d + GATE_RANK], g[..., a:a + 3 * d]], axis=-1)


def kernel(x, ln_in_g, ln_in_b, w_in, rel_bias, gla_w_lr, gla_b_lr, gla_norm_g, w_branch, w_out, ln1_g, ln1_b, w_up, w_down, ln2_g, ln2_b, loss_target, m_ln_in_g, m_ln_in_b, m_w_in, m_rel_bias, m_gla_w_lr, m_gla_b_lr, m_gla_norm_g, m_w_branch, m_w_out, m_ln1_g, m_ln1_b, m_w_up, m_w_down, m_ln2_g, m_ln2_b, v_ln_in_g, v_ln_in_b, v_w_in, v_rel_bias, v_gla_w_lr, v_gla_b_lr, v_gla_norm_g, v_w_branch, v_w_out, v_ln1_g, v_ln1_b, v_w_up, v_w_down, v_ln2_g, v_ln2_b):
    t, d = x.shape[1], x.shape[2]
    dff = 4 * d
    half = d // 2
    off, npad = _layout(d)
    xi, yi, ci = (lax.axis_index(a) for a in MESH_AXES)
    chip = 2 * xi + yi
    csel = jnp.reshape(ci, (1,)).astype(jnp.int32)
    psel = jnp.reshape(chip, (1,)).astype(jnp.int32)

    big_w = [w_in, w_branch.reshape(DEPTH, -1, d), w_out, w_up, w_down]
    big_m = [m_w_in, m_w_branch.reshape(DEPTH, -1, d), m_w_out, m_w_up, m_w_down]
    big_v = [v_w_in, v_w_branch.reshape(DEPTH, -1, d), v_w_out, v_w_up, v_w_down]
    W_IN, REST = [0], [1, 2, 3, 4]

    def shards_of(l, idx):
        return [big_w[i][l].astype(BF16) for i in idx]

    def lands_of(srcs):
        return [_sds((4,) + s.shape, s.dtype) for s in srcs]

    def full_w_in(g):
        return _pad_cols(jnp.transpose(g, (1, 0, 2)).reshape(d, -1), d)

    def full_rest(gs):
        g_br, g_out, g_up, g_down = gs
        return (jnp.transpose(g_br.reshape(4, N_BRANCH, d // 4, d), (1, 0, 2, 3)).reshape(N_BRANCH, d, d),
                g_out.reshape(d, d), jnp.transpose(g_up, (1, 0, 2)).reshape(d, dff), g_down.reshape(dff, d))

    def gather_start(tag, l, idx):
        srcs = shards_of(l, idx)
        return srcs, _split_start(f"gather_{tag}{l}_start", srcs, lands_of(srcs), _gather_copies)

    def gather_finish(tag, l, pending, after):
        srcs, started = pending
        _, lands = _split_wait(f"gather_{tag}{l}_wait", started, _gather_copies, after)
        return _gather_forward(f"gather_{tag}{l}_pass", srcs, lands)

    def token(pending):
        return pending[1][4][0, 0]

    win, wbr, wout, wup, wdown = ([None] * DEPTH for _ in range(5))
    win[0] = full_w_in(_gather_plain("gather_in0", shards_of(0, W_IN))[0])
    pend_rest = gather_start("rest", 0, REST)

    dkh = half // LIN_HEADS
    lr_rows = DEPTH * GATE_RANK
    lr_slab = jnp.zeros((lr_rows, 4, half // 4), F32)
    lr_slab = lax.dynamic_update_slice(lr_slab, (gla_w_lr.reshape(lr_rows, 1, half // 4) * jnp.where(ci == 0, 1.0, 0.0)),
                                       (0, chip, 0))
    wlr_full = _small_allreduce(lr_slab.reshape(-1, LANE)).reshape(DEPTH, GATE_RANK, half)
    wlr_pad = jnp.concatenate([wlr_full, jnp.zeros((DEPTH, LANE - GATE_RANK, half), F32)], axis=1)

    inv = 10000.0 ** (-jnp.arange(0, dkh, 2, dtype=F32) / dkh)
    ang = jnp.arange(t, dtype=F32)[:, None] * inv[None, :]
    cos, sin = jnp.cos(ang), jnp.sin(ang)
    rope_c = jnp.concatenate([cos, cos], axis=1)
    rope_s = jnp.concatenate([-sin, sin], axis=1)
    log_gamma = jnp.log1p(-jnp.exp2(-5.0 - jnp.arange(LIN_HEADS, dtype=F32)))
    lg_tab = jnp.broadcast_to(log_gamma[:, None, None], (LIN_HEADS, 1, dkh))

    def vec(a):
        return a.reshape(1, -1)

    x0, x0b, xh_in, rs_in = _ln_in(x[0], vec(ln_in_g) + token(pend_rest), vec(ln_in_b))
    saved = []
    xl, xlb = x0, x0b
    for l in range(DEPTH):
        p = _mm("proj_in", xlb, win[l], 512, 1792)
        wbr[l], wout[l], wup[l], wdown[l] = full_rest(gather_finish("rest", l, pend_rest, p))
        tok = 0.0
        if l + 1 < DEPTH:
            pend_in = gather_start("in", l + 1, W_IN)
            tok = token(pend_in)
        bias = _bias_expand(rel_bias[l] + tok)
        attn = _attn_fwd(p, bias, d, off)
        ret_aux = (rope_c, rope_s, lg_tab + tok)
        gla_aux = (p, wlr_pad[l], vec(gla_b_lr[l]) + tok, vec(gla_norm_g[l]))
        o_ret, b_ret, st_ret = _lin_fwd(False, p, ret_aux, d, off)
        o_gla, b_gla, st_gla = _lin_fwd(True, p, gla_aux, d, off)
        tok = 0.0
        if l + 1 < DEPTH:
            win[l + 1] = full_w_in(gather_finish("in", l + 1, pend_in, b_gla)[0])
            pend_rest = gather_start("rest", l + 1, REST)
            tok = token(pend_rest)
        bo = jnp.stack([attn, b_ret, b_gla])
        proj, merged = _merge_fwd(bo, wbr[l], p, off["gates"])
        x1, x1b, xh1, rs1 = _mm_res_ln("out_proj_ln", merged, wout[l], xl, vec(ln1_g[l]) + tok, vec(ln1_b[l]),
                                       256, False)
        u = _mm("mlp_up", x1b, wup[l], 512, 1024)
        x2, x2b, xh2, rs2, act = _mm_res_ln("mlp_down_ln", u, wdown[l], x1, vec(ln2_g[l]), vec(ln2_b[l]), 256, True)
        saved.append(dict(xlb=xlb, p=p, bias=bias, ret_aux=ret_aux, gla_aux=gla_aux, o_ret=o_ret, o_gla=o_gla,
                          st_ret=st_ret, st_gla=st_gla, bo=bo, proj=proj, merged=merged, x1b=x1b, xh1=xh1,
                          rs1=rs1, u=u, xh2=xh2, rs2=rs2, act=act))
        xl, xlb = x2, x2b

    small = {}
    last = saved[-1]
    loss_p, dz2, dz2b, dg, db = _loss_ln_bwd(xl, loss_target[0], last["xh2"], last["rs2"], vec(ln2_g[DEPTH - 1]))
    small["loss"] = loss_p[:, :1]
    grad_x = None

    def scatter_start(tag, l, idx, shards):
        theirs = _sibling_halves(f"grad_{tag}{l}_sibling", shards)
        hs = [_add_half("grad_sibling_add", g, th, csel) for g, th in zip(shards, theirs)]
        lands = [_sds((3,) + h.shape[1:], F32) for h in hs]
        return tag, l, idx, _split_start(f"grad_{tag}{l}_scatter_start", hs, lands, _scatter_copies)

    adam_out = [None] * len(big_w)

    def scatter_finish(pending, after):
        tag, l, idx, started = pending
        hs, rcv = _split_wait(f"grad_{tag}{l}_scatter_wait", started, _scatter_copies, after)
        sms = [_sum_shards("grad_chip_sum", h, r, psel) for h, r in zip(hs, rcv)]
        for i, g in zip(idx, _sibling_share(f"grad_{tag}{l}_share", sms)):
            adam_out[i] = _adamw_layer("adamw_large", big_w[i], g, big_m[i], big_v[i], l, adam_out[i])
        return adam_out[idx[0]][0]

    in_flight = []

    def scatter(tag, l, idx, shards):
        pending = scatter_start(tag, l, idx, shards)
        in_flight.append(pending)
        if len(in_flight) > 2:
            scatter_finish(in_flight.pop(0), pending[3][4])
        return pending[3][4][0, 0]

    for l in reversed(range(DEPTH)):
        s = saved[l]
        small[("ln2_g", l)], small[("ln2_b", l)] = dg, db
        du = _mm_nt_relu2_bwd(dz2b, wdown[l], s["u"])
        g_wdown = _mm("grad_w_down", s["act"].T, dz2b, 1024, 512)
        g_wup = _mm("grad_w_up", s["x1b"].T, du, 1024, 512)
        dz1, dz1b, dg1, db1 = _mm_nt_res_lnbwd("mlp_up_bwd_ln", du, wup[l], dz2, s["xh1"], s["rs1"],
                                               vec(ln1_g[l]), 256, dff)
        small[("ln1_g", l)], small[("ln1_b", l)] = dg1, db1
        dproj, dgl = _merge_bwd(dz1b, wout[l], s["proj"], s["p"], off["gates"])
        g_wout = _mm("grad_w_out", s["merged"].T, dz1b, 1024, 512)
        dbo = _mm("branch_proj_bwd", dproj, wbr[l], 512, 512, nt=True)
        g_wbr = _mm("grad_w_branch", jnp.transpose(s["bo"], (0, 2, 1)), dproj, 1024, 512)
        tok = scatter("rest", l, REST, [
            jnp.transpose(g_wbr.reshape(N_BRANCH, 4, d // 4, d), (1, 0, 2, 3)).reshape(4, -1, d),
            g_wout.reshape(4, d // 4, d), jnp.transpose(g_wup.reshape(d, 4, d), (1, 0, 2)), g_wdown.reshape(4, d, d)])
        rc, rs_, lg = s["ret_aux"]
        gp, gw, gb, gn_ = s["gla_aux"]
        dq_a, dk_acc, dv_acc, dbias = _attn_bwd(s["p"], s["bias"] + tok, dbo[0], d, off)
        small[("rel_bias", l)] = _bias_reduce(dbias)
        dk_a = dk_acc[2 * QB:].astype(BF16)
        dv_a = dv_acc[2 * QB:].astype(BF16)
        dq_r, dk_r, dv_r, dg_r = _lin_bwd(False, s["p"], (rc, rs_, lg + tok), s["o_ret"], s["st_ret"], dbo[1], d, off)
        dq_g, dk_g, dv_g, dg_g, dpre, dblr, dgn = _lin_bwd(True, s["p"], (gp, gw, gb + tok, gn_), s["o_gla"],
                                                           s["st_gla"], dbo[2], d, off)
        small[("gla_b_lr", l)] = dblr.reshape(1, half)
        small[("gla_norm_g", l)] = jnp.sum(dgn, axis=0)
        dpre_b = dpre.astype(BF16)
        glr_b = s["p"][:, off["glr"]:off["glr"] + LANE].astype(BF16)
        dglr = _mm("gate_lr_bwd", dpre_b, wlr_pad[l], 512, LANE, nt=True, out_dtype=BF16)
        small[("gla_w_lr", l)] = _mm("grad_gla_w_lr", glr_b.T, dpre_b, LANE, half)[:GATE_RANK]
        dp = jnp.concatenate([dq_a, dk_a, dv_a, dq_r, dk_r, dv_r, dg_r, dq_g, dk_g, dv_g, dg_g,
                              dgl[0], dgl[1], dgl[2], dglr, jnp.zeros((t, LANE), BF16)], axis=1)
        g_win = _mm("grad_w_in", s["xlb"].T, dp, 1024, 896)
        tok = scatter("in", l, W_IN, [jnp.transpose(_unpad_cols(g_win, d).reshape(d, 4, -1), (1, 0, 2))])
        if l > 0:
            prev = saved[l - 1]
            xh_p, rs_p, g_p = prev["xh2"], prev["rs2"], vec(ln2_g[l - 1])
        else:
            xh_p, rs_p, g_p = xh_in, rs_in, vec(ln_in_g)
        dzp, dzpb, dg, db = _mm_nt_res_lnbwd("proj_in_bwd_ln", dp, win[l], dz1, xh_p, rs_p, g_p + tok, 512, 1792)
        dz2, dz2b = dzp, dzpb
        grad_x = dzp
    small["ln_in_g"], small["ln_in_b"] = dg, db
    after = grad_x
    while in_flight:
        after = scatter_finish(in_flight.pop(0), after)

    rb_pad = 3 * LANE
    pieces = [small["loss"].reshape(-1), jnp.zeros((LANE - 1,), F32), small["ln_in_g"].reshape(-1),
              small["ln_in_b"].reshape(-1)]
    for l in range(DEPTH):
        rb = jnp.pad(small[("rel_bias", l)], ((0, 0), (0, rb_pad - (2 * REL_CLIP + 1))))
        pieces += [rb.reshape(-1), small[("gla_w_lr", l)].reshape(-1), small[("gla_b_lr", l)].reshape(-1),
                   small[("gla_norm_g", l)].reshape(-1), small[("ln1_g", l)].reshape(-1),
                   small[("ln1_b", l)].reshape(-1), small[("ln2_g", l)].reshape(-1), small[("ln2_b", l)].reshape(-1)]
    sizes = [pc.shape[0] for pc in pieces]
    packed = jnp.concatenate(pieces)
    padn = (-packed.shape[0]) % (8 * LANE)
    packed = jnp.concatenate([packed, jnp.zeros((padn,), F32)]).reshape(-1, LANE)
    red = _small_allreduce(packed).reshape(-1)
    parts, pos = [], 0
    for sz in sizes:
        parts.append(red[pos:pos + sz])
        pos += sz
    loss = parts[0][0]
    g_ln_in_g, g_ln_in_b = parts[2], parts[3]
    per = 8
    g_rel = jnp.stack([parts[4 + per * l].reshape(ATTN_HEADS, rb_pad)[:, :2 * REL_CLIP + 1] for l in range(DEPTH)])
    g_wlr_full = jnp.stack([parts[5 + per * l].reshape(GATE_RANK, half) for l in range(DEPTH)])
    g_wlr = lax.dynamic_slice_in_dim(g_wlr_full, chip * (half // 4), half // 4, axis=2)
    g_blr = jnp.stack([parts[6 + per * l] for l in range(DEPTH)])
    g_gn = jnp.stack([parts[7 + per * l] for l in range(DEPTH)])
    g_ln1g = jnp.stack([parts[8 + per * l] for l in range(DEPTH)])
    g_ln1b = jnp.stack([parts[9 + per * l] for l in range(DEPTH)])
    g_ln2g = jnp.stack([parts[10 + per * l] for l in range(DEPTH)])
    g_ln2b = jnp.stack([parts[11 + per * l] for l in range(DEPTH)])

    grads = [g_ln_in_g, g_ln_in_b, None, g_rel, g_wlr, g_blr, g_gn, None, None, g_ln1g, g_ln1b, None, None,
             g_ln2g, g_ln2b]
    ws = [ln_in_g, ln_in_b, w_in, rel_bias, gla_w_lr, gla_b_lr, gla_norm_g, w_branch, w_out, ln1_g, ln1_b,
          w_up, w_down, ln2_g, ln2_b]
    ms = [m_ln_in_g, m_ln_in_b, m_w_in, m_rel_bias, m_gla_w_lr, m_gla_b_lr, m_gla_norm_g, m_w_branch, m_w_out,
          m_ln1_g, m_ln1_b, m_w_up, m_w_down, m_ln2_g, m_ln2_b]
    vs = [v_ln_in_g, v_ln_in_b, v_w_in, v_rel_bias, v_gla_w_lr, v_gla_b_lr, v_gla_norm_g, v_w_branch, v_w_out,
          v_ln1_g, v_ln1_b, v_w_up, v_w_down, v_ln2_g, v_ln2_b]

    deltas, new_ms, new_vs = [None] * 15, [None] * 15, [None] * 15
    big_idx = [2, 7, 8, 11, 12]
    for i, res in zip(big_idx, adam_out):
        shp = ws[i].shape
        grads[i], deltas[i], new_ms[i], new_vs[i] = (r.reshape(shp) for r in res)
    small_idx = [i for i in range(15) if i not in big_idx]

    def pack(arrs):
        flat_ = jnp.concatenate([arrs[i].reshape(-1) for i in small_idx])
        pad_ = (-flat_.shape[0]) % (8 * LANE)
        return jnp.concatenate([flat_, jnp.ones((pad_,), F32)]).reshape(-1, LANE)

    dl, nm, nv = _adamw("adamw_small", pack(ws), pack(grads), pack(ms), pack(vs))
    pos = 0
    for i in small_idx:
        sz = int(np.prod(ws[i].shape))
        deltas[i] = dl.reshape(-1)[pos:pos + sz].reshape(ws[i].shape)
        new_ms[i] = nm.reshape(-1)[pos:pos + sz].reshape(ws[i].shape)
        new_vs[i] = nv.reshape(-1)[pos:pos + sz].reshape(ws[i].shape)
        pos += sz

    return (loss, grad_x[None], *grads, *deltas, *new_ms, *new_vs)
```

```python
import functools

import numpy as np
import jax
import jax.numpy as jnp
from jax import lax
from jax.experimental import pallas as pl
from jax.experimental.pallas import tpu as pltpu

F32 = jnp.float32
BF16 = jnp.bfloat16
MXU_DTYPE = BF16
HI = lax.Precision.HIGHEST

DEPTH = 2
CHUNK = 64
N_BRANCH = 3
ATTN_HEADS = 8
ATTN_LEFT = 8
REL_CLIP = 2 * CHUNK
LIN_HEADS = 4
GATE_RANK = 16
GATE_NORM = 16.0
LN_EPS = 1e-5
NEG_INF = -1e30
ALPHA = (2 * DEPTH) ** 0.25
ADAM_LR, ADAM_B1, ADAM_B2, ADAM_EPS, ADAM_WD, ADAM_STEP = 0.001, 0.9, 0.999, 1e-08, 0.01, 10

LANE = 128
VMEM_LIMIT = 56 << 20
QB = 256
KW = 3 * QB
LB = 256
MESH_AXES = ("x", "y", "c")
DEV = pl.DeviceIdType.MESH


def _cp(sem):
    return pltpu.CompilerParams(dimension_semantics=sem, vmem_limit_bytes=VMEM_LIMIT)


def _mx(v):
    return v.astype(MXU_DTYPE)


def _dot(a, b):
    return jnp.dot(_mx(a), _mx(b), preferred_element_type=F32)


def _dot_nt(a, b):
    return lax.dot_general(_mx(a), _mx(b), (((1,), (1,)), ((), ())), preferred_element_type=F32)


def _dot_hi(a, b):
    return jnp.dot(a, b, precision=HI, preferred_element_type=F32)


def _sigmoid(v):
    return 1.0 / (1.0 + jnp.exp(-v))


def _sds(shape, dtype):
    return jax.ShapeDtypeStruct(shape, dtype)


def _mm(name, a, b, tm, tn, nt=False, out_dtype=F32):
    batched = a.ndim == 3
    m, k = a.shape[-2:]
    n = b.shape[-2] if nt else b.shape[-1]
    tm, tn = min(tm, m), min(tn, n)

    def body(a_ref, b_ref, o_ref):
        f = _dot_nt if nt else _dot
        o_ref[...] = f(a_ref[...], b_ref[...]).astype(o_ref.dtype)

    if batched:
        nb = a.shape[0]
        grid = (nb, m // tm, n // tn)
        a_spec = pl.BlockSpec((None, tm, k), lambda g, i, j: (g, i, 0))
        b_spec = (pl.BlockSpec((None, tn, k), lambda g, i, j: (g, j, 0)) if nt
                  else pl.BlockSpec((None, k, tn), lambda g, i, j: (g, 0, j)))
        o_spec = pl.BlockSpec((None, tm, tn), lambda g, i, j: (g, i, j))
        out_shape = _sds((nb, m, n), out_dtype)
        sem = ("parallel", "parallel", "parallel")
    else:
        grid = (m // tm, n // tn)
        a_spec = pl.BlockSpec((tm, k), lambda i, j: (i, 0))
        b_spec = (pl.BlockSpec((tn, k), lambda i, j: (j, 0)) if nt
                  else pl.BlockSpec((k, tn), lambda i, j: (0, j)))
        o_spec = pl.BlockSpec((tm, tn), lambda i, j: (i, j))
        out_shape = _sds((m, n), out_dtype)
        sem = ("parallel", "parallel")
    return pl.pallas_call(body, name=name, grid=grid, in_specs=[a_spec, b_spec], out_specs=o_spec,
                          out_shape=out_shape, compiler_params=_cp(sem))(a, b)


def _ln_rows(y, g, b):
    mu = jnp.mean(y, axis=-1, keepdims=True)
    yc = y - mu
    var = jnp.mean(yc * yc, axis=-1, keepdims=True)
    rs = lax.rsqrt(var + LN_EPS)
    xh = yc * rs
    return xh * g + b, xh, rs


def _ln_in(x, g, b, tm=256):
    t, d = x.shape

    def body(x_ref, g_ref, b_ref, o_ref, ob_ref, xh_ref, rs_ref):
        o, xh, rs = _ln_rows(x_ref[...], g_ref[...], b_ref[...])
        o_ref[...] = o
        ob_ref[...] = o.astype(BF16)
        xh_ref[...] = xh
        rs_ref[...] = rs

    row = pl.BlockSpec((tm, d), lambda i: (i, 0))
    vec = pl.BlockSpec((1, d), lambda i: (0, 0))
    return pl.pallas_call(
        body, name="ln_in", grid=(t // tm,), in_specs=[row, vec, vec],
        out_specs=[row, row, row, pl.BlockSpec((tm, 1), lambda i: (i, 0))],
        out_shape=[_sds((t, d), F32), _sds((t, d), BF16), _sds((t, d), F32), _sds((t, 1), F32)],
        compiler_params=_cp(("parallel",)))(x, g, b)


def _mm_res_ln(name, a, w, res, g, b, tm, relu2):
    t, k = a.shape
    d = w.shape[1]

    def body(a_ref, w_ref, r_ref, g_ref, b_ref, o_ref, ob_ref, xh_ref, rs_ref, *act_ref):
        av = a_ref[...]
        if relu2:
            av = jnp.square(jnp.maximum(av, 0.0))
            act_ref[0][...] = av.astype(BF16)
        y = ALPHA * r_ref[...] + _dot(av, w_ref[...])
        o, xh, rs = _ln_rows(y, g_ref[...], b_ref[...])
        o_ref[...] = o
        ob_ref[...] = o.astype(BF16)
        xh_ref[...] = xh
        rs_ref[...] = rs

    row = pl.BlockSpec((tm, d), lambda i: (i, 0))
    vec = pl.BlockSpec((1, d), lambda i: (0, 0))
    arow = pl.BlockSpec((tm, k), lambda i: (i, 0))
    out_specs = [row, row, row, pl.BlockSpec((tm, 1), lambda i: (i, 0))]
    out_shape = [_sds((t, d), F32), _sds((t, d), BF16), _sds((t, d), F32), _sds((t, 1), F32)]
    if relu2:
        out_specs.append(arow)
        out_shape.append(_sds((t, k), BF16))
    return pl.pallas_call(
        body, name=name, grid=(t // tm,),
        in_specs=[arow, pl.BlockSpec((k, d), lambda i: (0, 0)), row, vec, vec],
        out_specs=out_specs, out_shape=out_shape, compiler_params=_cp(("parallel",)))(a, w, res, g, b)


def _merge_fwd(bo, wb, p, gate_off, tm=512, tn=512):
    _, t, d = bo.shape
    gb = gate_off // tn

    def body(bo_ref, wb_ref, g0, g1, g2, proj_ref, m_ref):
        acc = None
        for n, g_ref in enumerate((g0, g1, g2)):
            pr = _dot(bo_ref[n], wb_ref[n])
            proj_ref[n] = pr
            term = _sigmoid(g_ref[...]) * pr
            acc = term if acc is None else acc + term
        m_ref[...] = acc.astype(BF16)

    gspecs = [pl.BlockSpec((tm, tn), functools.partial(lambda i, j, n: (i, gb + n * (d // tn) + j), n=n))
              for n in range(3)]
    return pl.pallas_call(
        body, name="merge_fwd", grid=(t // tm, d // tn),
        in_specs=[pl.BlockSpec((3, tm, d), lambda i, j: (0, i, 0)),
                  pl.BlockSpec((3, d, tn), lambda i, j: (0, 0, j))] + gspecs,
        out_specs=[pl.BlockSpec((3, tm, tn), lambda i, j: (0, i, j)), pl.BlockSpec((tm, tn), lambda i, j: (i, j))],
        out_shape=[_sds((3, t, d), F32), _sds((t, d), BF16)],
        compiler_params=_cp(("parallel", "parallel")))(bo, wb, p, p, p)


def _merge_bwd(dz, wout, proj, p, gate_off, tm=512, tn=512):
    t, d = dz.shape
    gb = gate_off // tn

    def body(dz_ref, w_ref, proj_ref, g0, g1, g2, dproj_ref, dgl_ref):
        dm = _dot_nt(dz_ref[...], w_ref[...])
        for n, g_ref in enumerate((g0, g1, g2)):
            s = _sigmoid(g_ref[...])
            dproj_ref[n] = (dm * s).astype(BF16)
            dgl_ref[n] = (dm * proj_ref[n] * (s * (1.0 - s))).astype(BF16)

    gspecs = [pl.BlockSpec((tm, tn), functools.partial(lambda i, j, n: (i, gb + n * (d // tn) + j), n=n))
              for n in range(3)]
    dproj, dgl = pl.pallas_call(
        body, name="merge_bwd", grid=(t // tm, d // tn),
        in_specs=[pl.BlockSpec((tm, d), lambda i, j: (i, 0)), pl.BlockSpec((tn, d), lambda i, j: (j, 0)),
                  pl.BlockSpec((3, tm, tn), lambda i, j: (0, i, j))] + gspecs,
        out_specs=[pl.BlockSpec((3, tm, tn), lambda i, j: (0, i, j)),
                   pl.BlockSpec((3, tm, tn), lambda i, j: (0, i, j))],
        out_shape=[_sds((3, t, d), BF16), _sds((3, t, d), BF16)],
        compiler_params=_cp(("parallel", "parallel")))(dz, wout, proj, p, p, p)
    return dproj, dgl


def _mm_nt_relu2_bwd(dz, wdown, u, tm=512, tn=1024):
    t, d = dz.shape
    f = wdown.shape[0]

    def body(dz_ref, w_ref, u_ref, du_ref):
        da = _dot_nt(dz_ref[...], w_ref[...])
        du_ref[...] = (da * (2.0 * jnp.maximum(u_ref[...], 0.0))).astype(BF16)

    return pl.pallas_call(
        body, name="mlp_down_bwd", grid=(t // tm, f // tn),
        in_specs=[pl.BlockSpec((tm, d), lambda i, j: (i, 0)), pl.BlockSpec((tn, d), lambda i, j: (j, 0)),
                  pl.BlockSpec((tm, tn), lambda i, j: (i, j))],
        out_specs=pl.BlockSpec((tm, tn), lambda i, j: (i, j)), out_shape=_sds((t, f), BF16),
        compiler_params=_cp(("parallel", "parallel")))(dz, wdown, u)


def _ln_bwd_rows(dx, xh, rs, g):
    dxh = dx * g
    m1 = jnp.mean(dxh, axis=-1, keepdims=True)
    m2 = jnp.mean(dxh * xh, axis=-1, keepdims=True)
    return rs * (dxh - m1 - xh * m2)


def _mm_nt_res_lnbwd(name, a, w, dres, xh, rs, g, tm, tk):
    t, k = a.shape
    d = w.shape[0]
    nk = k // tk

    def body(a_ref, w_ref, dr_ref, xh_ref, rs_ref, g_ref, dz_ref, dzb_ref, dg_ref, db_ref, acc_ref):
        i, kk = pl.program_id(0), pl.program_id(1)

        @pl.when(kk == 0)
        def _():
            acc_ref[...] = ALPHA * dr_ref[...]

        acc_ref[...] += _dot_nt(a_ref[...], w_ref[...])

        @pl.when(jnp.logical_and(i == 0, kk == 0))
        def _():
            dg_ref[...] = jnp.zeros_like(dg_ref)
            db_ref[...] = jnp.zeros_like(db_ref)

        @pl.when(kk == nk - 1)
        def _():
            dx = acc_ref[...]
            xhv = xh_ref[...]
            dz = _ln_bwd_rows(dx, xhv, rs_ref[...], g_ref[...])
            dz_ref[...] = dz
            dzb_ref[...] = dz.astype(BF16)
            dg_ref[...] += jnp.sum(dx * xhv, axis=0, keepdims=True)
            db_ref[...] += jnp.sum(dx, axis=0, keepdims=True)

    row = pl.BlockSpec((tm, d), lambda i, kk: (i, 0))
    vec = pl.BlockSpec((1, d), lambda i, kk: (0, 0))
    return pl.pallas_call(
        body, name=name, grid=(t // tm, nk),
        in_specs=[pl.BlockSpec((tm, tk), lambda i, kk: (i, kk)), pl.BlockSpec((d, tk), lambda i, kk: (0, kk)),
                  row, row, pl.BlockSpec((tm, 1), lambda i, kk: (i, 0)), vec],
        out_specs=[row, row, vec, vec],
        out_shape=[_sds((t, d), F32), _sds((t, d), BF16), _sds((1, d), F32), _sds((1, d), F32)],
        scratch_shapes=[pltpu.VMEM((tm, d), F32)],
        compiler_params=_cp(("arbitrary", "arbitrary")))(a, w, dres, xh, rs, g)


def _loss_ln_bwd(x2, target, xh, rs, g, tm=256):
    t, d = x2.shape

    def body(x_ref, t_ref, xh_ref, rs_ref, g_ref, loss_ref, dz_ref, dzb_ref, dg_ref, db_ref):
        @pl.when(pl.program_id(0) == 0)
        def _():
            loss_ref[...] = jnp.zeros_like(loss_ref)
            dg_ref[...] = jnp.zeros_like(dg_ref)
            db_ref[...] = jnp.zeros_like(db_ref)

        err = x_ref[...] - t_ref[...]
        per_row = jnp.mean(err * err, axis=-1, keepdims=True)
        loss_ref[...] += 0.5 * jnp.sum(per_row, axis=0, keepdims=True)
        dx = err * (1.0 / d)
        xhv = xh_ref[...]
        dz = _ln_bwd_rows(dx, xhv, rs_ref[...], g_ref[...])
        dz_ref[...] = dz
        dzb_ref[...] = dz.astype(BF16)
        dg_ref[...] += jnp.sum(dx * xhv, axis=0, keepdims=True)
        db_ref[...] += jnp.sum(dx, axis=0, keepdims=True)

    row = pl.BlockSpec((tm, d), lambda i: (i, 0))
    vec = pl.BlockSpec((1, d), lambda i: (0, 0))
    return pl.pallas_call(
        body, name="loss_ln_bwd", grid=(t // tm,),
        in_specs=[row, row, row, pl.BlockSpec((tm, 1), lambda i: (i, 0)), vec],
        out_specs=[pl.BlockSpec((1, LANE), lambda i: (0, 0)), row, row, vec, vec],
        out_shape=[_sds((1, LANE), F32), _sds((t, d), F32), _sds((t, d), BF16), _sds((1, d), F32),
                   _sds((1, d), F32)],
        compiler_params=_cp(("arbitrary",)))(x2, target, xh, rs, g)


def _attn_scores(q_ref, k_refs, bias_ref, i, dh):
    q = q_ref[...] * (dh ** -0.5)
    k = jnp.concatenate([r[...] for r in k_refs], axis=0)
    s = _dot_nt(q, k) + bias_ref[...]
    col = lax.broadcasted_iota(jnp.int32, s.shape, 1)
    s = jnp.where(col >= (2 - i) * QB, s, NEG_INF)
    m = jnp.max(s, axis=-1, keepdims=True)
    e = jnp.exp(s - m)
    return q, k, e / jnp.sum(e, axis=-1, keepdims=True)


def _attn_specs(dh, qcol, kcol, vcol):
    q_spec = pl.BlockSpec((QB, dh), lambda h, i: (i, qcol + h))
    k_specs = [pl.BlockSpec((QB, dh), functools.partial(lambda h, i, j: (jnp.maximum(i - 2 + j, 0), kcol + h), j=j))
               for j in range(3)]
    v_specs = [pl.BlockSpec((QB, dh), functools.partial(lambda h, i, j: (jnp.maximum(i - 2 + j, 0), vcol + h), j=j))
               for j in range(3)]
    bias_spec = pl.BlockSpec((None, QB, KW), lambda h, i: (h, 0, 0))
    return q_spec, k_specs, v_specs, bias_spec


def _attn_fwd(p, bias, d, off):
    t = p.shape[0]
    dh = d // ATTN_HEADS

    def body(q_ref, k0, k1, k2, v0, v1, v2, bias_ref, o_ref):
        _, _, pr = _attn_scores(q_ref, (k0, k1, k2), bias_ref, pl.program_id(1), dh)
        v = jnp.concatenate([v0[...], v1[...], v2[...]], axis=0)
        o_ref[...] = _dot(pr, v).astype(o_ref.dtype)

    q_spec, k_specs, v_specs, bias_spec = _attn_specs(dh, off["aq"] // dh, off["ak"] // dh, off["av"] // dh)
    return pl.pallas_call(
        body, name="attn_fwd", grid=(ATTN_HEADS, t // QB),
        in_specs=[q_spec] + k_specs + v_specs + [bias_spec],
        out_specs=pl.BlockSpec((QB, dh), lambda h, i: (i, h)), out_shape=_sds((t, d), BF16),
        compiler_params=_cp(("parallel", "parallel")))(p, p, p, p, p, p, p, bias)


def _attn_bwd(p, bias, do, d, off):
    t = p.shape[0]
    dh = d // ATTN_HEADS
    tp = t + 2 * QB

    def body(q_ref, k0, k1, k2, v0, v1, v2, bias_ref, do_ref, dq_ref, dk_ref, dv_ref, dbias_ref):
        i = pl.program_id(1)

        @pl.when(i == 0)
        def _():
            dk_ref[...] = jnp.zeros_like(dk_ref)
            dv_ref[...] = jnp.zeros_like(dv_ref)
            dbias_ref[...] = jnp.zeros_like(dbias_ref)

        q, k, pr = _attn_scores(q_ref, (k0, k1, k2), bias_ref, i, dh)
        v = jnp.concatenate([v0[...], v1[...], v2[...]], axis=0)
        dov = do_ref[...]
        dp = _dot_nt(dov, v)
        delta = jnp.sum(pr * dp, axis=-1, keepdims=True)
        ds = pr * (dp - delta)
        dbias_ref[...] += ds
        dq_ref[...] = (_dot(ds, k) * (dh ** -0.5)).astype(dq_ref.dtype)
        rows = pl.ds(pl.multiple_of(i * QB, QB), KW)
        dk_ref[rows, :] += _dot(ds.T, q)
        dv_ref[rows, :] += _dot(pr.T, dov)

    q_spec, k_specs, v_specs, bias_spec = _attn_specs(dh, off["aq"] // dh, off["ak"] // dh, off["av"] // dh)
    acc_spec = pl.BlockSpec((tp, dh), lambda h, i: (0, h))
    return pl.pallas_call(
        body, name="attn_bwd", grid=(ATTN_HEADS, t // QB),
        in_specs=[q_spec] + k_specs + v_specs + [bias_spec, pl.BlockSpec((QB, dh), lambda h, i: (i, h))],
        out_specs=[pl.BlockSpec((QB, dh), lambda h, i: (i, h)), acc_spec, acc_spec, bias_spec],
        out_shape=[_sds((t, d), BF16), _sds((tp, d), F32), _sds((tp, d), F32),
                   _sds((ATTN_HEADS, QB, KW), F32)],
        compiler_params=_cp(("parallel", "arbitrary")))(p, p, p, p, p, p, p, bias, do)


def _onehot_mm(name, a, b):
    def body(a_ref, b_ref, o_ref):
        o_ref[...] = _dot_hi(a_ref[...], b_ref[...])

    return pl.pallas_call(body, name=name, out_shape=_sds((a.shape[0], b.shape[1]), F32),
                          compiler_params=pltpu.CompilerParams(vmem_limit_bytes=VMEM_LIMIT))(a, b)


def _diag_index():
    ii, jj = np.arange(CHUNK)[:, None], np.arange(CHUNK)[None, :]
    return (ii - jj + CHUNK - 1).reshape(-1)


def _bias_expand(rel_bias):
    h = rel_bias.shape[0]
    nq, nk, shift = QB // CHUNK, KW // CHUNK, (2 * QB) // CHUNK
    nbin, ndc = 3 * LANE, 4
    rb = jnp.pad(rel_bias, ((0, 0), (0, nbin - rel_bias.shape[1])))
    win = np.clip(CHUNK * np.arange(ndc)[:, None] + np.arange(LANE)[None, :] - (CHUNK - 1), -REL_CLIP, REL_CLIP)
    sel = (jnp.arange(nbin)[:, None] == jnp.asarray((win + REL_CLIP).reshape(1, -1))).astype(F32)
    windows = _onehot_mm("bias_windows", rb, sel)
    diag_t = (jnp.arange(LANE)[:, None] == jnp.asarray(_diag_index().reshape(1, -1))).astype(F32)
    blocks = _onehot_mm("bias_blocks", windows.reshape(h * ndc, LANE), diag_t).reshape(h, ndc, CHUNK, CHUNK)
    off_band = jnp.full((h, CHUNK, CHUNK), NEG_INF, F32)
    rows = []
    for ic in range(nq):
        dcs = [ic - jc + shift for jc in range(nk)]
        rows.append(jnp.concatenate([blocks[:, min(dc, ndc - 1)] if 0 <= dc <= ATTN_LEFT else off_band
                                     for dc in dcs], axis=2))
    return jnp.concatenate(rows, axis=1)


def _bias_reduce(dbias):
    h = dbias.shape[0]
    nq, nk = QB // CHUNK, KW // CHUNK
    nbin = 3 * LANE
    blocks = dbias.reshape(h, nq, CHUNK, nk, CHUNK).transpose(0, 1, 3, 2, 4).reshape(h * nq * nk, CHUNK * CHUNK)
    diag = (jnp.asarray(_diag_index().reshape(-1, 1)) == jnp.arange(LANE)[None, :]).astype(F32)
    ic = np.arange(nq)[:, None, None]
    jc = np.arange(nk)[None, :, None]
    dl = np.arange(LANE)[None, None, :] - (CHUNK - 1)
    rel = np.clip(CHUNK * (ic - jc + (2 * QB) // CHUNK) + dl, -REL_CLIP, REL_CLIP) + REL_CLIP
    bins = (jnp.asarray(rel.reshape(-1, 1)) == jnp.arange(nbin)[None, :]).astype(F32)

    diags = _onehot_mm("bias_diag_sums", blocks, diag)
    out = _onehot_mm("bias_bin_sums", diags.reshape(h, nq * nk * LANE), bins)
    return out[:, :2 * REL_CLIP + 1]


def _tri(lower):
    r = lax.broadcasted_iota(jnp.int32, (CHUNK, CHUNK), 0)
    c = lax.broadcasted_iota(jnp.int32, (CHUNK, CHUNK), 1)
    return (r >= c) if lower else (r <= c)


def _lin_prep(gla, q, k, aux):
    dk = q.shape[-1]
    if gla:
        glr, wlr, blr = aux
        q = q * (dk ** -0.5)
        pre = _dot(glr, wlr) + blr
        log_a = (jnp.minimum(pre, 0.0) - jnp.log(1.0 + jnp.exp(-jnp.abs(pre)))) / GATE_NORM
        b = _dot_hi(_tri(True).astype(F32), log_a)
        return q, k, b, pre
    cs, sn, lg = aux
    half = dk // 2
    q = q * cs + pltpu.roll(q, half, 1) * sn
    k = (k * cs + pltpu.roll(k, half, 1) * sn) * (dk ** -0.5)
    pos = lax.broadcasted_iota(jnp.int32, (CHUNK, dk), 0).astype(F32) + 1.0
    return q, k, pos * lg, None


def _lin_chunk(q, k, v, b, st):
    eb, enb = jnp.exp(b), jnp.exp(-b)
    last = b[CHUNK - 1:CHUNK, :]
    qf, kf, qb, kb = q * eb, k * enb, q * enb, k * eb
    kl = k * jnp.exp(last - b)
    s = jnp.where(_tri(True), _dot_nt(qf, kf), _dot_nt(qb, kb))
    o = _dot(s, v) + _dot_nt(qf, st)
    st_new = st * jnp.exp(last) + _dot(v.T, kl)
    return o, st_new, (eb, enb, last, qf, kf, qb, kb, kl, s)


def _lin_norm_gate(gla, o, gate, gn):
    sg = _sigmoid(gate)
    silu = gate * sg
    if gla:
        r = lax.rsqrt(jnp.mean(o * o, axis=-1, keepdims=True) + LN_EPS)
        hn = o * r
        return silu * (hn * gn), (sg, silu, r, hn)
    mu = jnp.mean(o, axis=-1, keepdims=True)
    oc = o - mu
    r = lax.rsqrt(jnp.mean(oc * oc, axis=-1, keepdims=True) + LN_EPS)
    hn = oc * r
    return silu * hn, (sg, silu, r, hn)


def _lin_specs(gla, dk, dv, off, rev, nb):
    pre = "g" if gla else "r"
    qc, kc, vc, gc = (off[pre + "q"] // dk, off[pre + "k"] // dk, off[pre + "v"] // dv, off[pre + "g"] // dv)

    def blk(i):
        return nb - 1 - i if rev else i

    specs = [pl.BlockSpec((LB, dk), lambda h, i: (blk(i), qc + h)),
             pl.BlockSpec((LB, dk), lambda h, i: (blk(i), kc + h)),
             pl.BlockSpec((LB, dv), lambda h, i: (blk(i), vc + h)),
             pl.BlockSpec((LB, dv), lambda h, i: (blk(i), gc + h))]
    if gla:
        specs += [pl.BlockSpec((LB, LANE), lambda h, i: (blk(i), off["glr"] // LANE)),
                  pl.BlockSpec((LANE, dk), lambda h, i: (0, h)),
                  pl.BlockSpec((1, dk), lambda h, i: (0, h)),
                  pl.BlockSpec((1, dv), lambda h, i: (0, 0))]
    else:
        specs += [pl.BlockSpec((LB, dk), lambda h, i: (blk(i), 0)),
                  pl.BlockSpec((LB, dk), lambda h, i: (blk(i), 0)),
                  pl.BlockSpec((None, 1, dk), lambda h, i: (h, 0, 0))]
    return specs, blk


def _lin_aux(gla, refs, rows):
    if gla:
        glr_ref, wlr_ref, blr_ref, gn_ref = refs
        return (glr_ref[rows, :], wlr_ref[...], blr_ref[...]), gn_ref[...]
    cs_ref, sn_ref, lg_ref = refs
    return (cs_ref[rows, :], sn_ref[rows, :], lg_ref[...]), None


def _lin_fwd(gla, p, aux_arrays, d, off):
    t = p.shape[0]
    dk, dv = d // (2 * LIN_HEADS), d // LIN_HEADS
    nb, cb = t // LB, LB // CHUNK
    naux = len(aux_arrays)

    def body(*refs):
        q_ref, k_ref, v_ref, g_ref = refs[:4]
        aux_refs = refs[4:4 + naux]
        o_ref, bo_ref, st_out_ref, st_ref = refs[4 + naux:]

        @pl.when(pl.program_id(1) == 0)
        def _():
            st_ref[...] = jnp.zeros_like(st_ref)

        for c in range(cb):
            rows = pl.ds(c * CHUNK, CHUNK)
            aux, gn = _lin_aux(gla, aux_refs, rows)
            q, k, b, _ = _lin_prep(gla, q_ref[rows, :], k_ref[rows, :], aux)
            st = st_ref[...]
            st_out_ref[c] = st
            o, st_new, _ = _lin_chunk(q, k, v_ref[rows, :], b, st)
            st_ref[...] = st_new
            o_ref[rows, :] = o
            out, _ = _lin_norm_gate(gla, o, g_ref[rows, :], gn)
            bo_ref[rows, :] = out.astype(BF16)

    specs, _ = _lin_specs(gla, dk, dv, off, False, nb)
    orow = pl.BlockSpec((LB, dv), lambda h, i: (i, h))
    return pl.pallas_call(
        body, name="gla_fwd" if gla else "ret_fwd", grid=(LIN_HEADS, nb), in_specs=specs,
        out_specs=[orow, orow, pl.BlockSpec((None, cb, dv, dk), lambda h, i: (h, i, 0, 0))],
        out_shape=[_sds((t, d), F32), _sds((t, d), BF16), _sds((LIN_HEADS, t // CHUNK, dv, dk), F32)],
        scratch_shapes=[pltpu.VMEM((dv, dk), F32)],
        compiler_params=_cp(("parallel", "arbitrary")))(p, p, p, p, *aux_arrays)


def _lin_bwd(gla, p, aux_arrays, o, states, dbo, d, off):
    t = p.shape[0]
    dk, dv = d // (2 * LIN_HEADS), d // LIN_HEADS
    nb, cb = t // LB, LB // CHUNK
    naux = len(aux_arrays)

    def body(*refs):
        q_ref, k_ref, v_ref, g_ref = refs[:4]
        aux_refs = refs[4:4 + naux]
        o_ref, st_in_ref, dbo_ref = refs[4 + naux:7 + naux]
        outs = refs[7 + naux:]
        dq_ref, dk_ref, dv_ref, dg_ref = outs[:4]
        dst_ref = outs[-1]
        first = pl.program_id(1) == 0

        @pl.when(first)
        def _():
            dst_ref[...] = jnp.zeros_like(dst_ref)

        if gla:
            dpre_ref, dblr_ref, dgn_ref = outs[4:7]

            @pl.when(first)
            def _():
                dblr_ref[...] = jnp.zeros_like(dblr_ref)
                dgn_ref[...] = jnp.zeros_like(dgn_ref)

        for c in reversed(range(cb)):
            rows = pl.ds(c * CHUNK, CHUNK)
            aux, gn = _lin_aux(gla, aux_refs, rows)
            q, k, b, pre = _lin_prep(gla, q_ref[rows, :], k_ref[rows, :], aux)
            v = v_ref[rows, :]
            st = st_in_ref[c]
            _, _, (eb, enb, last, qf, kf, qb, kb, kl, s) = _lin_chunk(q, k, v, b, st)
            gate = g_ref[rows, :]
            dout = dbo_ref[rows, :]
            _, (sg, silu, r, hn) = _lin_norm_gate(gla, o_ref[rows, :], gate, gn)
            dsilu = sg * (1.0 + gate * (1.0 - sg))
            if gla:
                y = hn * gn
                dy = dout * silu
                dg_ref[rows, :] = (dout * y * dsilu).astype(BF16)
                dgn_ref[...] += jnp.sum(dy * hn, axis=0, keepdims=True)
                dhn = dy * gn
                do = r * (dhn - hn * jnp.mean(dhn * hn, axis=-1, keepdims=True))
            else:
                dhn = dout * silu
                dg_ref[rows, :] = (dout * hn * dsilu).astype(BF16)
                do = r * (dhn - jnp.mean(dhn, axis=-1, keepdims=True)
                          - hn * jnp.mean(dhn * hn, axis=-1, keepdims=True))
            dstn = dst_ref[...]
            dec = jnp.exp(last)
            ds = _dot_nt(do, v)
            low = _tri(True)
            dsf = jnp.where(low, ds, 0.0)
            dsb = jnp.where(low, 0.0, ds)
            dvv = _dot(s.T, do) + _dot_nt(kl, dstn)
            dqf = _dot(dsf, kf) + _dot(do, st)
            dkf = _dot(dsf.T, qf)
            dqb = _dot(dsb, kb)
            dkb = _dot(dsb.T, qb)
            dkl = _dot(v, dstn)
            dst_ref[...] = dstn * dec + _dot(do.T, qf)
            dq = dqf * eb + dqb * enb
            dkk = dkf * enb + dkb * eb + dkl * jnp.exp(last - b)
            dv_ref[rows, :] = dvv.astype(BF16)
            if gla:
                ddec = jnp.sum(dstn * st, axis=0, keepdims=True)
                db = dqf * qf - dkf * kf - dqb * qb + dkb * kb - dkl * kl
                dlast = jnp.sum(dkl * kl, axis=0, keepdims=True) + ddec * dec
                rowi = lax.broadcasted_iota(jnp.int32, db.shape, 0)
                db = db + jnp.where(rowi == CHUNK - 1, dlast, 0.0)
                dlog_a = _dot_hi(_tri(False).astype(F32), db)
                dpre = dlog_a * (1.0 / GATE_NORM) * (1.0 - _sigmoid(pre))
                dpre_ref[rows, :] = dpre
                dblr_ref[...] += jnp.sum(dpre, axis=0, keepdims=True)
                dq_ref[rows, :] = (dq * (dk ** -0.5)).astype(BF16)
                dk_ref[rows, :] = dkk.astype(BF16)
            else:
                cs, sn, _ = aux
                half = dk // 2
                dkk = dkk * (dk ** -0.5)
                dq_ref[rows, :] = (dq * cs + pltpu.roll(dq * sn, half, 1)).astype(BF16)
                dk_ref[rows, :] = (dkk * cs + pltpu.roll(dkk * sn, half, 1)).astype(BF16)

    specs, blk = _lin_specs(gla, dk, dv, off, True, nb)
    vrow = pl.BlockSpec((LB, dv), lambda h, i: (blk(i), h))
    krow = pl.BlockSpec((LB, dk), lambda h, i: (blk(i), h))
    specs += [vrow, pl.BlockSpec((None, cb, dv, dk), lambda h, i: (h, blk(i), 0, 0)), vrow]
    out_specs = [krow, krow, vrow, vrow]
    out_shape = [_sds((t, d // 2), BF16), _sds((t, d // 2), BF16), _sds((t, d), BF16), _sds((t, d), BF16)]
    if gla:
        out_specs += [krow, pl.BlockSpec((None, 1, dk), lambda h, i: (h, 0, 0)),
                      pl.BlockSpec((None, 1, dv), lambda h, i: (h, 0, 0))]
        out_shape += [_sds((t, d // 2), F32), _sds((LIN_HEADS, 1, dk), F32), _sds((LIN_HEADS, 1, dv), F32)]
    out_specs.append(pl.BlockSpec((None, dv, dk), lambda h, i: (h, 0, 0)))
    out_shape.append(_sds((LIN_HEADS, dv, dk), F32))
    res = pl.pallas_call(
        body, name="gla_bwd" if gla else "ret_bwd", grid=(LIN_HEADS, nb), in_specs=specs,
        out_specs=out_specs, out_shape=out_shape,
        compiler_params=_cp(("parallel", "arbitrary")))(p, p, p, p, *aux_arrays, o, states, dbo)
    return res[:-1]


def _row_tile(rows, cols):
    cap = max(8, (2 << 20) // (4 * cols))
    t = rows
    while t > cap and t % 2 == 0:
        t //= 2
    return t


def _add_half(name, g, t, sel):
    nchip, hr, cols = t.shape
    tr = _row_tile(hr, cols)
    nb = hr // tr

    def body(sel_ref, g_ref, t_ref, o_ref):
        o_ref[...] = g_ref[...] + t_ref[...]

    half = pl.BlockSpec((None, tr, cols), lambda p, i, s: (p, i, 0))
    gs = pltpu.PrefetchScalarGridSpec(
        num_scalar_prefetch=1, grid=(nchip, nb),
        in_specs=[pl.BlockSpec((None, tr, cols), lambda p, i, s: (p, s[0] * nb + i, 0)), half], out_specs=half)
    return pl.pallas_call(body, name=name, grid_spec=gs, out_shape=_sds(t.shape, F32),
                          compiler_params=_cp(("parallel", "parallel")))(sel, g, t)


def _sum_shards(name, h, rcv, sel):
    _, rows, cols = h.shape
    tr = _row_tile(rows, cols)

    def body(sel_ref, h_ref, r0, r1, r2, o_ref):
        o_ref[...] = ((h_ref[...] + r0[...]) + r1[...]) + r2[...]

    rspecs = [pl.BlockSpec((None, tr, cols), functools.partial(lambda i, s, j: (j, i, 0), j=j)) for j in range(3)]
    gs = pltpu.PrefetchScalarGridSpec(
        num_scalar_prefetch=1, grid=(rows // tr,),
        in_specs=[pl.BlockSpec((None, tr, cols), lambda i, s: (s[0], i, 0))] + rspecs,
        out_specs=pl.BlockSpec((tr, cols), lambda i, s: (i, 0)))
    return pl.pallas_call(body, name=name, grid_spec=gs, out_shape=_sds((rows, cols), F32),
                          compiler_params=_cp(("parallel",)))(sel, h, rcv, rcv, rcv)


def _adamw_math(w, g, m, v):
    c1 = 1.0 - ADAM_B1 ** ADAM_STEP
    c2 = 1.0 - ADAM_B2 ** ADAM_STEP
    nm = ADAM_B1 * m + (1.0 - ADAM_B1) * g
    nv = ADAM_B2 * v + (1.0 - ADAM_B2) * jnp.square(g)
    return -ADAM_LR * ((nm / c1) / (jnp.sqrt(nv / c2) + ADAM_EPS) + ADAM_WD * w), nm, nv


def _adamw(name, w, g, m, v):
    rows, cols = w.shape
    tr = _row_tile(rows, cols)

    def body(w_ref, g_ref, m_ref, v_ref, d_ref, nm_ref, nv_ref):
        d_ref[...], nm_ref[...], nv_ref[...] = _adamw_math(w_ref[...], g_ref[...], m_ref[...], v_ref[...])

    spec = pl.BlockSpec((tr, cols), lambda i: (i, 0))
    return pl.pallas_call(body, name=name, grid=(rows // tr,), in_specs=[spec] * 4, out_specs=[spec] * 3,
                          out_shape=[_sds((rows, cols), F32)] * 3, compiler_params=_cp(("parallel",)))(w, g, m, v)


def _adamw_layer(name, w, g_own, g_sib, sel, m, v, layer, prev):
    depth, rows, cols = w.shape
    tr = _row_tile(rows // 2, cols)
    nbh = rows // 2 // tr
    nprev = 0 if prev is None else 4

    def body(sel_ref, w_ref, own_ref, sib_ref, m_ref, v_ref, *rest):
        go_ref, d_ref, nm_ref, nv_ref = rest[nprev:]
        gv = jnp.where(pl.program_id(0) // nbh == sel_ref[0], own_ref[...], sib_ref[...])
        go_ref[...] = gv
        d_ref[...], nm_ref[...], nv_ref[...] = _adamw_math(w_ref[...], gv, m_ref[...], v_ref[...])

    lay = pl.BlockSpec((None, tr, cols), lambda i, s: (layer, i, 0))
    hlf = pl.BlockSpec((tr, cols), lambda i, s: (i % nbh, 0))
    gs = pltpu.PrefetchScalarGridSpec(
        num_scalar_prefetch=1, grid=(2 * nbh,), in_specs=[lay, hlf, hlf, lay, lay] + [ANY] * nprev,
        out_specs=[lay] * 4)
    args = (sel, w, g_own, g_sib, m, v) + (() if prev is None else tuple(prev))
    return pl.pallas_call(
        body, name=name, grid_spec=gs, out_shape=[_sds((depth, rows, cols), F32)] * 4,
        input_output_aliases={6 + k: k for k in range(nprev)},
        compiler_params=_cp(("parallel",)))(*args)


def _place():
    x, y, c = (lax.axis_index(a) for a in MESH_AXES)
    chips = [(1 - x, y), (x, 1 - y), (1 - x, 1 - y)]
    return x, y, c, chips


def _chip_index(xy):
    return 2 * xy[0] + xy[1]


ANY = pl.BlockSpec(memory_space=pl.ANY)


HBM_SPEC = pl.BlockSpec(memory_space=pltpu.HBM)
SEM = pl.BlockSpec(memory_space=pltpu.SEMAPHORE)
EFFECT = pltpu.SideEffectType.DATAFLOW_SIDE_EFFECTING


def _half(ref, c):
    hr = ref.shape[-2] // 2
    return pl.ds(pl.multiple_of(c * hr, 16), hr)


def _gather_copies(srcs, lands, send, recv):
    x, y, c, chips = _place()
    me = _chip_index((x, y))
    return [pltpu.make_async_remote_copy(src_ref=s.at[_half(s, c)], dst_ref=g.at[me, _half(s, c)],
                                         send_sem=send.at[3 * a + j], recv_sem=recv.at[3 * a + j],
                                         device_id=(*ch, c), device_id_type=DEV)
            for a, (s, g) in enumerate(zip(srcs, lands)) for j, ch in enumerate(chips)]


def _scatter_copies(srcs, lands, send, recv):
    x, y, c, chips = _place()
    return [pltpu.make_async_remote_copy(src_ref=h.at[_chip_index(ch)], dst_ref=r.at[j],
                                         send_sem=send.at[3 * a + j], recv_sem=recv.at[3 * a + j],
                                         device_id=(*ch, c), device_id_type=DEV)
            for a, (h, r) in enumerate(zip(srcs, lands)) for j, ch in enumerate(chips)]


def _in_hbm(a):
    return pltpu.with_memory_space_constraint(a, pltpu.HBM)


def _split_start(name, srcs, land_shapes, copies_fn, after=None):
    ns, nl = len(srcs), len(land_shapes)
    ncp = 3 * ns
    lands = [lax.empty(s.shape, s.dtype) for s in land_shapes]
    behind = [] if after is None else [after]

    def body(*refs):
        src, land = refs[:ns], refs[ns:ns + nl]
        send, recv = refs[ns + nl + len(behind)], refs[ns + nl + len(behind) + 1]
        for cp in copies_fn(src, land, send, recv):
            cp.start()
        refs[-1][...] = jnp.zeros_like(refs[-1])

    bufs = list(srcs) + lands
    outs = pl.pallas_call(
        body, name=name, in_specs=[HBM_SPEC] * (ns + nl) + [ANY] * len(behind),
        out_specs=[SEM, SEM] + [HBM_SPEC] * (ns + nl) + [pl.BlockSpec(memory_space=pltpu.VMEM)],
        out_shape=[pltpu.SemaphoreType.DMA((ncp,)), pltpu.SemaphoreType.DMA((ncp,))]
        + [pltpu.HBM(b.shape, b.dtype) for b in bufs] + [_sds((8, LANE), F32)],
        input_output_aliases={i: 2 + i for i in range(ns + nl)},
        compiler_params=pltpu.CompilerParams(has_side_effects=EFFECT))(*[_in_hbm(b) for b in bufs], *behind)
    return outs[0], outs[1], list(outs[2:2 + ns]), list(outs[2 + ns:2 + ns + nl]), outs[-1]


def _split_wait(name, started, copies_fn, after):
    send, recv, srcs, lands, _ = started
    ns, nl = len(srcs), len(lands)

    def body(*refs):
        src, land = refs[:ns], refs[ns:ns + nl]
        for cp in copies_fn(src, land, refs[ns + nl], refs[ns + nl + 1]):
            cp.wait_send()
            cp.wait_recv()

    bufs = list(srcs) + list(lands)
    outs = pl.pallas_call(
        body, name=name, in_specs=[HBM_SPEC] * (ns + nl) + [SEM, SEM, ANY], out_specs=[HBM_SPEC] * (ns + nl),
        out_shape=[pltpu.HBM(b.shape, b.dtype) for b in bufs],
        input_output_aliases={i: i for i in range(ns + nl)},
        compiler_params=pltpu.CompilerParams(has_side_effects=EFFECT))(*bufs, send, recv, after)
    return list(outs[:ns]), list(outs[ns:])


def _gather_plain(name, srcs):
    n = len(srcs)

    def body(*refs):
        src, land = refs[:n], refs[n:2 * n]
        send, recv, fsend, frecv = refs[2 * n:]
        first = _gather_copies(src, land, send, recv)
        for cp in first:
            cp.start()
        _forward_body(land, first, fsend, frecv)

    return pl.pallas_call(
        body, name=name, in_specs=[ANY] * n, out_specs=[ANY] * n,
        out_shape=[_sds((4,) + s.shape, s.dtype) for s in srcs],
        scratch_shapes=[pltpu.SemaphoreType.DMA((3 * n,))] * 4)(*srcs)


def _forward_body(land, arrivals, fsend, frecv):
    x, y, c, chips = _place()
    n = len(land)
    passed = []
    for a in range(n):
        for j, ch in enumerate(chips):
            if arrivals is not None:
                arrivals[3 * a + j].wait_recv()
            slot = land[a].at[_chip_index(ch), _half(land[a], c)]
            fw = pltpu.make_async_remote_copy(src_ref=slot, dst_ref=slot, send_sem=fsend.at[3 * a + j],
                                              recv_sem=frecv.at[3 * a + j], device_id=(x, y, 1 - c),
                                              device_id_type=DEV)
            fw.start()
            passed.append(fw)
    for a in range(n):
        for j, ch in enumerate(chips):
            slot = land[a].at[_chip_index(ch), _half(land[a], 1 - c)]
            pltpu.make_async_remote_copy(src_ref=slot, dst_ref=slot, send_sem=fsend.at[3 * a + j],
                                         recv_sem=frecv.at[3 * a + j], device_id=(x, y, c),
                                         device_id_type=DEV).wait_recv()
    for cp in passed:
        cp.wait_send()
    if arrivals is not None:
        for cp in arrivals:
            cp.wait_send()


def _gather_forward(name, lands):
    n = len(lands)

    def body(*refs):
        _forward_body(refs[n:2 * n], None, refs[2 * n], refs[2 * n + 1])

    return pl.pallas_call(
        body, name=name, in_specs=[ANY] * n, out_specs=[ANY] * n,
        out_shape=[_sds(g.shape, g.dtype) for g in lands], input_output_aliases={a: a for a in range(n)},
        scratch_shapes=[pltpu.SemaphoreType.DMA((3 * n,))] * 2)(*lands)


def _sibling_halves(name, grs):
    n = len(grs)

    def body(*refs):
        ins, outs = refs[:n], refs[n:2 * n]
        send, recv = refs[2 * n:]
        x, y, c, _ = _place()
        cps = [pltpu.make_async_remote_copy(src_ref=ins[a].at[:, _half(ins[a], 1 - c)], dst_ref=outs[a],
                                            send_sem=send.at[a], recv_sem=recv.at[a], device_id=(x, y, 1 - c),
                                            device_id_type=DEV) for a in range(n)]
        for cp in cps:
            cp.start()
        for cp in cps:
            cp.wait()

    return pl.pallas_call(
        body, name=name, in_specs=[ANY] * n, out_specs=[ANY] * n,
        out_shape=[_sds((g.shape[0], g.shape[1] // 2, g.shape[2]), F32) for g in grs],
        scratch_shapes=[pltpu.SemaphoreType.DMA((n,)), pltpu.SemaphoreType.DMA((n,))])(*grs)


def _sibling_share(name, sms):
    n = len(sms)

    def body(*refs):
        ins, outs = refs[:n], refs[n:2 * n]
        send, recv = refs[2 * n:]
        x, y, c, _ = _place()
        cps = [pltpu.make_async_remote_copy(src_ref=ins[a], dst_ref=outs[a], send_sem=send.at[a],
                                            recv_sem=recv.at[a], device_id=(x, y, 1 - c), device_id_type=DEV)
               for a in range(n)]
        for cp in cps:
            cp.start()
        for cp in cps:
            cp.wait()

    return pl.pallas_call(
        body, name=name, in_specs=[ANY] * n, out_specs=[ANY] * n, out_shape=[_sds(s.shape, F32) for s in sms],
        scratch_shapes=[pltpu.SemaphoreType.DMA((n,))] * 2)(*sms)


def _small_allreduce(v):
    rows = v.shape[0]
    ndev = 8

    def body(v_ref, o_ref, gat_ref, send, recv):
        x, y, c, _ = _place()
        me = 4 * x + 2 * y + c
        cps = []
        for k in range(1, ndev):
            to = (me + k) % ndev
            cp = pltpu.make_async_remote_copy(src_ref=v_ref, dst_ref=gat_ref.at[me], send_sem=send.at[k - 1],
                                              recv_sem=recv.at[me], device_id=(to // 4, (to // 2) % 2, to % 2),
                                              device_id_type=DEV)
            cp.start()
            cps.append(cp)
        gat_ref[me] = v_ref[...]
        for k in range(1, ndev):
            frm = (me + k) % ndev
            pltpu.make_async_remote_copy(src_ref=v_ref, dst_ref=gat_ref.at[frm], send_sem=send.at[k - 1],
                                         recv_sem=recv.at[frm], device_id=(x, y, c), device_id_type=DEV).wait_recv()
        for cp in cps:
            cp.wait_send()
        acc = gat_ref[0]
        for k in range(1, ndev):
            acc = acc + gat_ref[k]
        o_ref[...] = acc

    vm = pl.BlockSpec(memory_space=pltpu.VMEM)
    return pl.pallas_call(
        body, name="small_allreduce", in_specs=[vm], out_specs=vm, out_shape=_sds((rows, LANE), F32),
        scratch_shapes=[pltpu.VMEM((ndev, rows, LANE), F32), pltpu.SemaphoreType.DMA((ndev - 1,)),
                        pltpu.SemaphoreType.DMA((ndev,))])(v)


def _layout(d):
    half = d // 2
    names = [("aq", d), ("ak", d), ("av", d), ("rq", half), ("rk", half), ("rv", d), ("rg", d),
             ("gq", half), ("gk", half), ("gv", d), ("gg", d), ("gates", 3 * d), ("glr", 2 * LANE)]
    off, pos = {}, 0
    for nm, sz in names:
        off[nm] = pos
        pos += sz
    return off, pos


def _pad_cols(w, d):
    a = 8 * d + d
    lr = w[..., a:a + GATE_RANK]
    z = jnp.zeros(w.shape[:-1] + (2 * LANE - GATE_RANK,), w.dtype)
    return jnp.concatenate([w[..., :a], w[..., a + GATE_RANK:], lr, z], axis=-1)


def _unpad_cols(g, d):
    a = 8 * d + d
    return jnp.concatenate([g[..., :a], g[..., a + 3 * d:a + 3 * d + GATE_RANK], g[..., a:a + 3 * d]], axis=-1)


def kernel(x, ln_in_g, ln_in_b, w_in, rel_bias, gla_w_lr, gla_b_lr, gla_norm_g, w_branch, w_out, ln1_g, ln1_b, w_up, w_down, ln2_g, ln2_b, loss_target, m_ln_in_g, m_ln_in_b, m_w_in, m_rel_bias, m_gla_w_lr, m_gla_b_lr, m_gla_norm_g, m_w_branch, m_w_out, m_ln1_g, m_ln1_b, m_w_up, m_w_down, m_ln2_g, m_ln2_b, v_ln_in_g, v_ln_in_b, v_w_in, v_rel_bias, v_gla_w_lr, v_gla_b_lr, v_gla_norm_g, v_w_branch, v_w_out, v_ln1_g, v_ln1_b, v_w_up, v_w_down, v_ln2_g, v_ln2_b):
    t, d = x.shape[1], x.shape[2]
    dff = 4 * d
    half = d // 2
    off, npad = _layout(d)
    xi, yi, ci = (lax.axis_index(a) for a in MESH_AXES)
    chip = 2 * xi + yi
    csel = jnp.reshape(ci, (1,)).astype(jnp.int32)
    psel = jnp.reshape(chip, (1,)).astype(jnp.int32)

    big_w = [w_in, w_branch.reshape(DEPTH, -1, d), w_out, w_up, w_down]
    big_m = [m_w_in, m_w_branch.reshape(DEPTH, -1, d), m_w_out, m_w_up, m_w_down]
    big_v = [v_w_in, v_w_branch.reshape(DEPTH, -1, d), v_w_out, v_w_up, v_w_down]
    W_IN, REST = [0], [1, 2, 3, 4]

    def shards_of(l, idx):
        return [big_w[i][l].astype(BF16) for i in idx]

    def lands_of(srcs):
        return [_sds((4,) + s.shape, s.dtype) for s in srcs]

    def full_w_in(g):
        return _pad_cols(jnp.transpose(g, (1, 0, 2)).reshape(d, -1), d)

    def full_rest(gs):
        g_br, g_out, g_up, g_down = gs
        return (jnp.transpose(g_br.reshape(4, N_BRANCH, d // 4, d), (1, 0, 2, 3)).reshape(N_BRANCH, d, d),
                g_out.reshape(d, d), jnp.transpose(g_up, (1, 0, 2)).reshape(d, dff), g_down.reshape(dff, d))

    def with_own(srcs, lands):
        return [lax.dynamic_update_slice(g, s[None], (chip, 0, 0)) for s, g in zip(srcs, lands)]

    def gather_start(tag, l, idx, after):
        srcs = shards_of(l, idx)
        return srcs, _split_start(f"gather_{tag}{l}_start", srcs, lands_of(srcs), _gather_copies, after)

    def gather_finish(tag, l, pending, after):
        srcs, started = pending
        _, lands = _split_wait(f"gather_{tag}{l}_wait", started, _gather_copies, after)
        return with_own(srcs, _gather_forward(f"gather_{tag}{l}_pass", lands))

    def token(pending):
        return pending[1][4][0, 0]

    win, wbr, wout, wup, wdown = ([None] * DEPTH for _ in range(5))
    src_first = shards_of(0, W_IN)
    g_first = with_own(src_first, _gather_plain("gather_in0", src_first))
    win[0] = full_w_in(g_first[0])
    pend_rest = gather_start("rest", 0, REST, g_first[0])

    dkh = half // LIN_HEADS
    lr_rows = DEPTH * GATE_RANK
    lr_slab = jnp.zeros((lr_rows, 4, half // 4), F32)
    lr_slab = lax.dynamic_update_slice(lr_slab, (gla_w_lr.reshape(lr_rows, 1, half // 4) * jnp.where(ci == 0, 1.0, 0.0)),
                                       (0, chip, 0))
    wlr_full = _small_allreduce(lr_slab.reshape(-1, LANE)).reshape(DEPTH, GATE_RANK, half)
    wlr_pad = jnp.concatenate([wlr_full, jnp.zeros((DEPTH, LANE - GATE_RANK, half), F32)], axis=1)

    inv = 10000.0 ** (-jnp.arange(0, dkh, 2, dtype=F32) / dkh)
    ang = jnp.arange(t, dtype=F32)[:, None] * inv[None, :]
    cos, sin = jnp.cos(ang), jnp.sin(ang)
    rope_c = jnp.concatenate([cos, cos], axis=1)
    rope_s = jnp.concatenate([-sin, sin], axis=1)
    log_gamma = jnp.log1p(-jnp.exp2(-5.0 - jnp.arange(LIN_HEADS, dtype=F32)))
    lg_tab = jnp.broadcast_to(log_gamma[:, None, None], (LIN_HEADS, 1, dkh))

    def vec(a):
        return a.reshape(1, -1)

    x0, x0b, xh_in, rs_in = _ln_in(x[0], vec(ln_in_g) + token(pend_rest), vec(ln_in_b))
    saved = []
    xl, xlb = x0, x0b
    for l in range(DEPTH):
        p = _mm("proj_in", xlb, win[l], 512, 1792)
        g_rest = gather_finish("rest", l, pend_rest, p)
        wbr[l], wout[l], wup[l], wdown[l] = full_rest(g_rest)
        tok = 0.0
        if l + 1 < DEPTH:
            pend_in = gather_start("in", l + 1, W_IN, g_rest[0])
            tok = token(pend_in)
        bias = _bias_expand(rel_bias[l] + tok)
        attn = _attn_fwd(p, bias, d, off)
        ret_aux = (rope_c, rope_s, lg_tab + tok)
        gla_aux = (p, wlr_pad[l], vec(gla_b_lr[l]) + tok, vec(gla_norm_g[l]))
        o_ret, b_ret, st_ret = _lin_fwd(False, p, ret_aux, d, off)
        o_gla, b_gla, st_gla = _lin_fwd(True, p, gla_aux, d, off)
        tok = 0.0
        if l + 1 < DEPTH:
            g_in = gather_finish("in", l + 1, pend_in, b_gla)
            win[l + 1] = full_w_in(g_in[0])
            pend_rest = gather_start("rest", l + 1, REST, g_in[0])
            tok = token(pend_rest)
        bo = jnp.stack([attn, b_ret, b_gla])
        proj, merged = _merge_fwd(bo, wbr[l], p, off["gates"])
        x1, x1b, xh1, rs1 = _mm_res_ln("out_proj_ln", merged, wout[l], xl, vec(ln1_g[l]) + tok, vec(ln1_b[l]),
                                       256, False)
        u = _mm("mlp_up", x1b, wup[l], 512, 1024)
        x2, x2b, xh2, rs2, act = _mm_res_ln("mlp_down_ln", u, wdown[l], x1, vec(ln2_g[l]), vec(ln2_b[l]), 256, True)
        saved.append(dict(xlb=xlb, p=p, bias=bias, ret_aux=ret_aux, gla_aux=gla_aux, o_ret=o_ret, o_gla=o_gla,
                          st_ret=st_ret, st_gla=st_gla, bo=bo, proj=proj, merged=merged, x1b=x1b, xh1=xh1,
                          rs1=rs1, u=u, xh2=xh2, rs2=rs2, act=act))
        xl, xlb = x2, x2b

    small = {}
    last = saved[-1]
    loss_p, dz2, dz2b, dg, db = _loss_ln_bwd(xl, loss_target[0], last["xh2"], last["rs2"], vec(ln2_g[DEPTH - 1]))
    small["loss"] = loss_p[:, :1]
    grad_x = None

    def scatter_start(tag, l, idx, shards):
        theirs = _sibling_halves(f"grad_{tag}{l}_sibling", shards)
        hs = [_add_half("grad_sibling_add", g, th, csel) for g, th in zip(shards, theirs)]
        lands = [_sds((3,) + h.shape[1:], F32) for h in hs]
        return tag, l, idx, _split_start(f"grad_{tag}{l}_scatter_start", hs, lands, _scatter_copies)

    adam_out = [None] * len(big_w)

    def scatter_finish(pending, after):
        tag, l, idx, started = pending
        hs, rcv = _split_wait(f"grad_{tag}{l}_scatter_wait", started, _scatter_copies, after)
        sms = [_sum_shards("grad_chip_sum", h, r, psel) for h, r in zip(hs, rcv)]
        for i, own, sib in zip(idx, sms, _sibling_share(f"grad_{tag}{l}_share", sms)):
            adam_out[i] = _adamw_layer("adamw_large", big_w[i], own, sib, csel, big_m[i], big_v[i], l, adam_out[i])
        return adam_out[idx[0]][0]

    in_flight = []

    def scatter(tag, l, idx, shards):
        pending = scatter_start(tag, l, idx, shards)
        in_flight.append(pending)
        if len(in_flight) > 2:
            scatter_finish(in_flight.pop(0), pending[3][4])
        return pending[3][4][0, 0]

    for l in reversed(range(DEPTH)):
        s = saved[l]
        small[("ln2_g", l)], small[("ln2_b", l)] = dg, db
        du = _mm_nt_relu2_bwd(dz2b, wdown[l], s["u"])
        g_wdown = _mm("grad_w_down", s["act"].T, dz2b, 1024, 512)
        g_wup = _mm("grad_w_up", s["x1b"].T, du, 1024, 512)
        dz1, dz1b, dg1, db1 = _mm_nt_res_lnbwd("mlp_up_bwd_ln", du, wup[l], dz2, s["xh1"], s["rs1"],
                                               vec(ln1_g[l]), 256, dff)
        small[("ln1_g", l)], small[("ln1_b", l)] = dg1, db1
        dproj, dgl = _merge_bwd(dz1b, wout[l], s["proj"], s["p"], off["gates"])
        g_wout = _mm("grad_w_out", s["merged"].T, dz1b, 1024, 512)
        dbo = _mm("branch_proj_bwd", dproj, wbr[l], 512, 512, nt=True)
        g_wbr = _mm("grad_w_branch", jnp.transpose(s["bo"], (0, 2, 1)), dproj, 1024, 512)
        tok = scatter("rest", l, REST, [
            jnp.transpose(g_wbr.reshape(N_BRANCH, 4, d // 4, d), (1, 0, 2, 3)).reshape(4, -1, d),
            g_wout.reshape(4, d // 4, d), jnp.transpose(g_wup.reshape(d, 4, d), (1, 0, 2)), g_wdown.reshape(4, d, d)])
        rc, rs_, lg = s["ret_aux"]
        gp, gw, gb, gn_ = s["gla_aux"]
        dq_a, dk_acc, dv_acc, dbias = _attn_bwd(s["p"], s["bias"] + tok, dbo[0], d, off)
        small[("rel_bias", l)] = _bias_reduce(dbias)
        dk_a = dk_acc[2 * QB:].astype(BF16)
        dv_a = dv_acc[2 * QB:].astype(BF16)
        dq_r, dk_r, dv_r, dg_r = _lin_bwd(False, s["p"], (rc, rs_, lg + tok), s["o_ret"], s["st_ret"], dbo[1], d, off)
        dq_g, dk_g, dv_g, dg_g, dpre, dblr, dgn = _lin_bwd(True, s["p"], (gp, gw, gb + tok, gn_), s["o_gla"],
                                                           s["st_gla"], dbo[2], d, off)
        small[("gla_b_lr", l)] = dblr.reshape(1, half)
        small[("gla_norm_g", l)] = jnp.sum(dgn, axis=0)
        dpre_b = dpre.astype(BF16)
        glr_b = s["p"][:, off["glr"]:off["glr"] + LANE].astype(BF16)
        dglr = _mm("gate_lr_bwd", dpre_b, wlr_pad[l], 512, LANE, nt=True, out_dtype=BF16)
        small[("gla_w_lr", l)] = _mm("grad_gla_w_lr", glr_b.T, dpre_b, LANE, half)[:GATE_RANK]
        dp = jnp.concatenate([dq_a, dk_a, dv_a, dq_r, dk_r, dv_r, dg_r, dq_g, dk_g, dv_g, dg_g,
                              dgl[0], dgl[1], dgl[2], dglr, jnp.zeros((t, LANE), BF16)], axis=1)
        g_win = _mm("grad_w_in", s["xlb"].T, dp, 1024, 896)
        tok = scatter("in", l, W_IN, [jnp.transpose(_unpad_cols(g_win, d).reshape(d, 4, -1), (1, 0, 2))])
        if l > 0:
            prev = saved[l - 1]
            xh_p, rs_p, g_p = prev["xh2"], prev["rs2"], vec(ln2_g[l - 1])
        else:
            xh_p, rs_p, g_p = xh_in, rs_in, vec(ln_in_g)
        dzp, dzpb, dg, db = _mm_nt_res_lnbwd("proj_in_bwd_ln", dp, win[l], dz1, xh_p, rs_p, g_p + tok, 512, 1792)
        dz2, dz2b = dzp, dzpb
        grad_x = dzp
    small["ln_in_g"], small["ln_in_b"] = dg, db
    after = grad_x
    while in_flight:
        after = scatter_finish(in_flight.pop(0), after)

    rb_pad = 3 * LANE
    pieces = [small["loss"].reshape(-1), jnp.zeros((LANE - 1,), F32), small["ln_in_g"].reshape(-1),
              small["ln_in_b"].reshape(-1)]
    for l in range(DEPTH):
        rb = jnp.pad(small[("rel_bias", l)], ((0, 0), (0, rb_pad - (2 * REL_CLIP + 1))))
        pieces += [rb.reshape(-1), small[("gla_w_lr", l)].reshape(-1), small[("gla_b_lr", l)].reshape(-1),
                   small[("gla_norm_g", l)].reshape(-1), small[("ln1_g", l)].reshape(-1),
                   small[("ln1_b", l)].reshape(-1), small[("ln2_g", l)].reshape(-1), small[("ln2_b", l)].reshape(-1)]
    sizes = [pc.shape[0] for pc in pieces]
    packed = jnp.concatenate(pieces)
    padn = (-packed.shape[0]) % (8 * LANE)
    packed = jnp.concatenate([packed, jnp.zeros((padn,), F32)]).reshape(-1, LANE)
    red = _small_allreduce(packed).reshape(-1)
    parts, pos = [], 0
    for sz in sizes:
        parts.append(red[pos:pos + sz])
        pos += sz
    loss = parts[0][0]
    g_ln_in_g, g_ln_in_b = parts[2], parts[3]
    per = 8
    g_rel = jnp.stack([parts[4 + per * l].reshape(ATTN_HEADS, rb_pad)[:, :2 * REL_CLIP + 1] for l in range(DEPTH)])
    g_wlr_full = jnp.stack([parts[5 + per * l].reshape(GATE_RANK, half) for l in range(DEPTH)])
    g_wlr = lax.dynamic_slice_in_dim(g_wlr_full, chip * (half // 4), half // 4, axis=2)
    g_blr = jnp.stack([parts[6 + per * l] for l in range(DEPTH)])
    g_gn = jnp.stack([parts[7 + per * l] for l in range(DEPTH)])
    g_ln1g = jnp.stack([parts[8 + per * l] for l in range(DEPTH)])
    g_ln1b = jnp.stack([parts[9 + per * l] for l in range(DEPTH)])
    g_ln2g = jnp.stack([parts[10 + per * l] for l in range(DEPTH)])
    g_ln2b = jnp.stack([parts[11 + per * l] for l in range(DEPTH)])

    grads = [g_ln_in_g, g_ln_in_b, None, g_rel, g_wlr, g_blr, g_gn, None, None, g_ln1g, g_ln1b, None, None,
             g_ln2g, g_ln2b]
    ws = [ln_in_g, ln_in_b, w_in, rel_bias, gla_w_lr, gla_b_lr, gla_norm_g, w_branch, w_out, ln1_g, ln1_b,
          w_up, w_down, ln2_g, ln2_b]
    ms = [m_ln_in_g, m_ln_in_b, m_w_in, m_rel_bias, m_gla_w_lr, m_gla_b_lr, m_gla_norm_g, m_w_branch, m_w_out,
          m_ln1_g, m_ln1_b, m_w_up, m_w_down, m_ln2_g, m_ln2_b]
    vs = [v_ln_in_g, v_ln_in_b, v_w_in, v_rel_bias, v_gla_w_lr, v_gla_b_lr, v_gla_norm_g, v_w_branch, v_w_out,
          v_ln1_g, v_ln1_b, v_w_up, v_w_down, v_ln2_g, v_ln2_b]

    deltas, new_ms, new_vs = [None] * 15, [None] * 15, [None] * 15
    big_idx = [2, 7, 8, 11, 12]
    for i, res in zip(big_idx, adam_out):
        shp = ws[i].shape
        grads[i], deltas[i], new_ms[i], new_vs[i] = (r.reshape(shp) for r in res)
    small_idx = [i for i in range(15) if i not in big_idx]

    def pack(arrs):
        flat_ = jnp.concatenate([arrs[i].reshape(-1) for i in small_idx])
        pad_ = (-flat_.shape[0]) % (8 * LANE)
        return jnp.concatenate([flat_, jnp.ones((pad_,), F32)]).reshape(-1, LANE)

    dl, nm, nv = _adamw("adamw_small", pack(ws), pack(grads), pack(ms), pack(vs))
    pos = 0
    for i in small_idx:
        sz = int(np.prod(ws[i].shape))
        deltas[i] = dl.reshape(-1)[pos:pos + sz].reshape(ws[i].shape)
        new_ms[i] = nm.reshape(-1)[pos:pos + sz].reshape(ws[i].shape)
        new_vs[i] = nv.reshape(-1)[pos:pos + sz].reshape(ws[i].shape)
        pos += sz

    return (loss, grad_x[None], *grads, *deltas, *new_ms, *new_vs)
```

```python
import functools

import numpy as np
import jax
import jax.numpy as jnp
from jax import lax
from jax.experimental import pallas as pl
from jax.experimental.pallas import tpu as pltpu

F32 = jnp.float32
BF16 = jnp.bfloat16
MXU_DTYPE = BF16
HI = lax.Precision.HIGHEST

DEPTH = 2
CHUNK = 64
N_BRANCH = 3
ATTN_HEADS = 8
ATTN_LEFT = 8
REL_CLIP = 2 * CHUNK
LIN_HEADS = 4
GATE_RANK = 16
GATE_NORM = 16.0
LN_EPS = 1e-5
NEG_INF = -1e30
ALPHA = (2 * DEPTH) ** 0.25
ADAM_LR, ADAM_B1, ADAM_B2, ADAM_EPS, ADAM_WD, ADAM_STEP = 0.001, 0.9, 0.999, 1e-08, 0.01, 10

LANE = 128
VMEM_LIMIT = 56 << 20
QB = 256
KW = 3 * QB
LB = 256
MESH_AXES = ("x", "y", "c")
DEV = pl.DeviceIdType.MESH


def _cp(sem):
    return pltpu.CompilerParams(dimension_semantics=sem, vmem_limit_bytes=VMEM_LIMIT)


def _mx(v):
    return v.astype(MXU_DTYPE)


def _dot(a, b):
    return jnp.dot(_mx(a), _mx(b), preferred_element_type=F32)


def _dot_nt(a, b):
    return lax.dot_general(_mx(a), _mx(b), (((1,), (1,)), ((), ())), preferred_element_type=F32)


def _dot_hi(a, b):
    return jnp.dot(a, b, precision=HI, preferred_element_type=F32)


def _sigmoid(v):
    return 1.0 / (1.0 + jnp.exp(-v))


def _sds(shape, dtype):
    return jax.ShapeDtypeStruct(shape, dtype)


def _mm(name, a, b, tm, tn, nt=False, out_dtype=F32):
    batched = a.ndim == 3
    m, k = a.shape[-2:]
    n = b.shape[-2] if nt else b.shape[-1]
    tm, tn = min(tm, m), min(tn, n)

    def body(a_ref, b_ref, o_ref):
        f = _dot_nt if nt else _dot
        o_ref[...] = f(a_ref[...], b_ref[...]).astype(o_ref.dtype)

    rows_inner = (n // tn) * m < (m // tm) * n

    def ij(u, v):
        return (v, u) if rows_inner else (u, v)

    if batched:
        nb = a.shape[0]
        grid = (nb,) + ij(m // tm, n // tn)
        a_spec = pl.BlockSpec((None, tm, k), lambda g, u, v: (g, ij(u, v)[0], 0))
        b_spec = (pl.BlockSpec((None, tn, k), lambda g, u, v: (g, ij(u, v)[1], 0)) if nt
                  else pl.BlockSpec((None, k, tn), lambda g, u, v: (g, 0, ij(u, v)[1])))
        o_spec = pl.BlockSpec((None, tm, tn), lambda g, u, v: (g,) + ij(u, v))
        out_shape = _sds((nb, m, n), out_dtype)
        sem = ("parallel", "parallel", "parallel")
    else:
        grid = ij(m // tm, n // tn)
        a_spec = pl.BlockSpec((tm, k), lambda u, v: (ij(u, v)[0], 0))
        b_spec = (pl.BlockSpec((tn, k), lambda u, v: (ij(u, v)[1], 0)) if nt
                  else pl.BlockSpec((k, tn), lambda u, v: (0, ij(u, v)[1])))
        o_spec = pl.BlockSpec((tm, tn), lambda u, v: ij(u, v))
        out_shape = _sds((m, n), out_dtype)
        sem = ("parallel", "parallel")
    return pl.pallas_call(body, name=name, grid=grid, in_specs=[a_spec, b_spec], out_specs=o_spec,
                          out_shape=out_shape, compiler_params=_cp(sem))(a, b)


def _ln_rows(y, g, b):
    mu = jnp.mean(y, axis=-1, keepdims=True)
    yc = y - mu
    var = jnp.mean(yc * yc, axis=-1, keepdims=True)
    rs = lax.rsqrt(var + LN_EPS)
    xh = yc * rs
    return xh * g + b, xh, rs


def _ln_in(x, g, b, tm=256):
    t, d = x.shape

    def body(x_ref, g_ref, b_ref, o_ref, ob_ref, xh_ref, rs_ref):
        o, xh, rs = _ln_rows(x_ref[...], g_ref[...], b_ref[...])
        o_ref[...] = o
        ob_ref[...] = o.astype(BF16)
        xh_ref[...] = xh
        rs_ref[...] = rs

    row = pl.BlockSpec((tm, d), lambda i: (i, 0))
    vec = pl.BlockSpec((1, d), lambda i: (0, 0))
    return pl.pallas_call(
        body, name="ln_in", grid=(t // tm,), in_specs=[row, vec, vec],
        out_specs=[row, row, row, pl.BlockSpec((tm, 1), lambda i: (i, 0))],
        out_shape=[_sds((t, d), F32), _sds((t, d), BF16), _sds((t, d), F32), _sds((t, 1), F32)],
        compiler_params=_cp(("parallel",)))(x, g, b)


def _mm_res_ln(name, a, w, res, g, b, tm, relu2):
    t, k = a.shape
    d = w.shape[1]

    def body(a_ref, w_ref, r_ref, g_ref, b_ref, o_ref, ob_ref, xh_ref, rs_ref, *act_ref):
        av = a_ref[...]
        if relu2:
            av = jnp.square(jnp.maximum(av, 0.0))
            act_ref[0][...] = av.astype(BF16)
        y = ALPHA * r_ref[...] + _dot(av, w_ref[...])
        o, xh, rs = _ln_rows(y, g_ref[...], b_ref[...])
        o_ref[...] = o
        ob_ref[...] = o.astype(BF16)
        xh_ref[...] = xh
        rs_ref[...] = rs

    row = pl.BlockSpec((tm, d), lambda i: (i, 0))
    vec = pl.BlockSpec((1, d), lambda i: (0, 0))
    arow = pl.BlockSpec((tm, k), lambda i: (i, 0))
    out_specs = [row, row, row, pl.BlockSpec((tm, 1), lambda i: (i, 0))]
    out_shape = [_sds((t, d), F32), _sds((t, d), BF16), _sds((t, d), F32), _sds((t, 1), F32)]
    if relu2:
        out_specs.append(arow)
        out_shape.append(_sds((t, k), BF16))
    return pl.pallas_call(
        body, name=name, grid=(t // tm,),
        in_specs=[arow, pl.BlockSpec((k, d), lambda i: (0, 0)), row, vec, vec],
        out_specs=out_specs, out_shape=out_shape, compiler_params=_cp(("parallel",)))(a, w, res, g, b)


def _merge_fwd(bo, wb, p, gate_off, tm=512, tn=512):
    _, t, d = bo.shape
    gb = gate_off // tn

    def body(bo_ref, wb_ref, g0, g1, g2, proj_ref, m_ref):
        acc = None
        for n, g_ref in enumerate((g0, g1, g2)):
            pr = _dot(bo_ref[n], wb_ref[n])
            proj_ref[n] = pr
            term = _sigmoid(g_ref[...]) * pr
            acc = term if acc is None else acc + term
        m_ref[...] = acc.astype(BF16)

    gspecs = [pl.BlockSpec((tm, tn), functools.partial(lambda i, j, n: (i, gb + n * (d // tn) + j), n=n))
              for n in range(3)]
    return pl.pallas_call(
        body, name="merge_fwd", grid=(t // tm, d // tn),
        in_specs=[pl.BlockSpec((3, tm, d), lambda i, j: (0, i, 0)),
                  pl.BlockSpec((3, d, tn), lambda i, j: (0, 0, j))] + gspecs,
        out_specs=[pl.BlockSpec((3, tm, tn), lambda i, j: (0, i, j)), pl.BlockSpec((tm, tn), lambda i, j: (i, j))],
        out_shape=[_sds((3, t, d), F32), _sds((t, d), BF16)],
        compiler_params=_cp(("parallel", "parallel")))(bo, wb, p, p, p)


def _merge_bwd(dz, wout, proj, p, gate_off, tm=512, tn=512):
    t, d = dz.shape
    gb = gate_off // tn

    def body(dz_ref, w_ref, proj_ref, g0, g1, g2, dproj_ref, dgl_ref):
        dm = _dot_nt(dz_ref[...], w_ref[...])
        for n, g_ref in enumerate((g0, g1, g2)):
            s = _sigmoid(g_ref[...])
            dproj_ref[n] = (dm * s).astype(BF16)
            dgl_ref[n] = (dm * proj_ref[n] * (s * (1.0 - s))).astype(BF16)

    gspecs = [pl.BlockSpec((tm, tn), functools.partial(lambda i, j, n: (i, gb + n * (d // tn) + j), n=n))
              for n in range(3)]
    dproj, dgl = pl.pallas_call(
        body, name="merge_bwd", grid=(t // tm, d // tn),
        in_specs=[pl.BlockSpec((tm, d), lambda i, j: (i, 0)), pl.BlockSpec((tn, d), lambda i, j: (j, 0)),
                  pl.BlockSpec((3, tm, tn), lambda i, j: (0, i, j))] + gspecs,
        out_specs=[pl.BlockSpec((3, tm, tn), lambda i, j: (0, i, j)),
                   pl.BlockSpec((3, tm, tn), lambda i, j: (0, i, j))],
        out_shape=[_sds((3, t, d), BF16), _sds((3, t, d), BF16)],
        compiler_params=_cp(("parallel", "parallel")))(dz, wout, proj, p, p, p)
    return dproj, dgl


def _mm_nt_relu2_bwd(dz, wdown, u, tm=512, tn=1024):
    t, d = dz.shape
    f = wdown.shape[0]

    def body(dz_ref, w_ref, u_ref, du_ref):
        da = _dot_nt(dz_ref[...], w_ref[...])
        du_ref[...] = (da * (2.0 * jnp.maximum(u_ref[...], 0.0))).astype(BF16)

    return pl.pallas_call(
        body, name="mlp_down_bwd", grid=(t // tm, f // tn),
        in_specs=[pl.BlockSpec((tm, d), lambda i, j: (i, 0)), pl.BlockSpec((tn, d), lambda i, j: (j, 0)),
                  pl.BlockSpec((tm, tn), lambda i, j: (i, j))],
        out_specs=pl.BlockSpec((tm, tn), lambda i, j: (i, j)), out_shape=_sds((t, f), BF16),
        compiler_params=_cp(("parallel", "parallel")))(dz, wdown, u)


def _ln_bwd_rows(dx, xh, rs, g):
    dxh = dx * g
    m1 = jnp.mean(dxh, axis=-1, keepdims=True)
    m2 = jnp.mean(dxh * xh, axis=-1, keepdims=True)
    return rs * (dxh - m1 - xh * m2)


def _mm_nt_res_lnbwd(name, a, w, dres, xh, rs, g, tm, tk):
    t, k = a.shape
    d = w.shape[0]
    nk = k // tk

    def body(a_ref, w_ref, dr_ref, xh_ref, rs_ref, g_ref, dz_ref, dzb_ref, dg_ref, db_ref, acc_ref):
        i, kk = pl.program_id(0), pl.program_id(1)

        @pl.when(kk == 0)
        def _():
            acc_ref[...] = ALPHA * dr_ref[...]

        acc_ref[...] += _dot_nt(a_ref[...], w_ref[...])

        @pl.when(jnp.logical_and(i == 0, kk == 0))
        def _():
            dg_ref[...] = jnp.zeros_like(dg_ref)
            db_ref[...] = jnp.zeros_like(db_ref)

        @pl.when(kk == nk - 1)
        def _():
            dx = acc_ref[...]
            xhv = xh_ref[...]
            dz = _ln_bwd_rows(dx, xhv, rs_ref[...], g_ref[...])
            dz_ref[...] = dz
            dzb_ref[...] = dz.astype(BF16)
            dg_ref[...] += jnp.sum(dx * xhv, axis=0, keepdims=True)
            db_ref[...] += jnp.sum(dx, axis=0, keepdims=True)

    row = pl.BlockSpec((tm, d), lambda i, kk: (i, 0))
    vec = pl.BlockSpec((1, d), lambda i, kk: (0, 0))
    return pl.pallas_call(
        body, name=name, grid=(t // tm, nk),
        in_specs=[pl.BlockSpec((tm, tk), lambda i, kk: (i, kk)), pl.BlockSpec((d, tk), lambda i, kk: (0, kk)),
                  row, row, pl.BlockSpec((tm, 1), lambda i, kk: (i, 0)), vec],
        out_specs=[row, row, vec, vec],
        out_shape=[_sds((t, d), F32), _sds((t, d), BF16), _sds((1, d), F32), _sds((1, d), F32)],
        scratch_shapes=[pltpu.VMEM((tm, d), F32)],
        compiler_params=_cp(("arbitrary", "arbitrary")))(a, w, dres, xh, rs, g)


def _loss_ln_bwd(x2, target, xh, rs, g, tm=256):
    t, d = x2.shape

    def body(x_ref, t_ref, xh_ref, rs_ref, g_ref, loss_ref, dz_ref, dzb_ref, dg_ref, db_ref):
        @pl.when(pl.program_id(0) == 0)
        def _():
            loss_ref[...] = jnp.zeros_like(loss_ref)
            dg_ref[...] = jnp.zeros_like(dg_ref)
            db_ref[...] = jnp.zeros_like(db_ref)

        err = x_ref[...] - t_ref[...]
        per_row = jnp.mean(err * err, axis=-1, keepdims=True)
        loss_ref[...] += 0.5 * jnp.sum(per_row, axis=0, keepdims=True)
        dx = err * (1.0 / d)
        xhv = xh_ref[...]
        dz = _ln_bwd_rows(dx, xhv, rs_ref[...], g_ref[...])
        dz_ref[...] = dz
        dzb_ref[...] = dz.astype(BF16)
        dg_ref[...] += jnp.sum(dx * xhv, axis=0, keepdims=True)
        db_ref[...] += jnp.sum(dx, axis=0, keepdims=True)

    row = pl.BlockSpec((tm, d), lambda i: (i, 0))
    vec = pl.BlockSpec((1, d), lambda i: (0, 0))
    return pl.pallas_call(
        body, name="loss_ln_bwd", grid=(t // tm,),
        in_specs=[row, row, row, pl.BlockSpec((tm, 1), lambda i: (i, 0)), vec],
        out_specs=[pl.BlockSpec((1, LANE), lambda i: (0, 0)), row, row, vec, vec],
        out_shape=[_sds((1, LANE), F32), _sds((t, d), F32), _sds((t, d), BF16), _sds((1, d), F32),
                   _sds((1, d), F32)],
        compiler_params=_cp(("arbitrary",)))(x2, target, xh, rs, g)


def _attn_scores(q_ref, k_refs, bias_ref, i, dh):
    q = q_ref[...] * (dh ** -0.5)
    k = jnp.concatenate([r[...] for r in k_refs], axis=0)
    s = _dot_nt(q, k) + bias_ref[...]
    col = lax.broadcasted_iota(jnp.int32, s.shape, 1)
    s = jnp.where(col >= (2 - i) * QB, s, NEG_INF)
    m = jnp.max(s, axis=-1, keepdims=True)
    e = jnp.exp(s - m)
    return q, k, e / jnp.sum(e, axis=-1, keepdims=True)


def _attn_specs(dh, qcol, kcol, vcol):
    q_spec = pl.BlockSpec((QB, dh), lambda h, i: (i, qcol + h))
    k_specs = [pl.BlockSpec((QB, dh), functools.partial(lambda h, i, j: (jnp.maximum(i - 2 + j, 0), kcol + h), j=j))
               for j in range(3)]
    v_specs = [pl.BlockSpec((QB, dh), functools.partial(lambda h, i, j: (jnp.maximum(i - 2 + j, 0), vcol + h), j=j))
               for j in range(3)]
    bias_spec = pl.BlockSpec((None, QB, KW), lambda h, i: (h, 0, 0))
    return q_spec, k_specs, v_specs, bias_spec


def _attn_fwd(p, bias, d, off):
    t = p.shape[0]
    dh = d // ATTN_HEADS

    def body(q_ref, k0, k1, k2, v0, v1, v2, bias_ref, o_ref):
        _, _, pr = _attn_scores(q_ref, (k0, k1, k2), bias_ref, pl.program_id(1), dh)
        v = jnp.concatenate([v0[...], v1[...], v2[...]], axis=0)
        o_ref[...] = _dot(pr, v).astype(o_ref.dtype)

    q_spec, k_specs, v_specs, bias_spec = _attn_specs(dh, off["aq"] // dh, off["ak"] // dh, off["av"] // dh)
    return pl.pallas_call(
        body, name="attn_fwd", grid=(ATTN_HEADS, t // QB),
        in_specs=[q_spec] + k_specs + v_specs + [bias_spec],
        out_specs=pl.BlockSpec((QB, dh), lambda h, i: (i, h)), out_shape=_sds((t, d), BF16),
        compiler_params=_cp(("parallel", "parallel")))(p, p, p, p, p, p, p, bias)


def _attn_bwd(p, bias, do, d, off):
    t = p.shape[0]
    dh = d // ATTN_HEADS
    tp = t + 2 * QB

    def body(q_ref, k0, k1, k2, v0, v1, v2, bias_ref, do_ref, dq_ref, dk_ref, dv_ref, dbias_ref):
        i = pl.program_id(1)

        @pl.when(i == 0)
        def _():
            dk_ref[...] = jnp.zeros_like(dk_ref)
            dv_ref[...] = jnp.zeros_like(dv_ref)
            dbias_ref[...] = jnp.zeros_like(dbias_ref)

        q, k, pr = _attn_scores(q_ref, (k0, k1, k2), bias_ref, i, dh)
        v = jnp.concatenate([v0[...], v1[...], v2[...]], axis=0)
        dov = do_ref[...]
        dp = _dot_nt(dov, v)
        delta = jnp.sum(pr * dp, axis=-1, keepdims=True)
        ds = pr * (dp - delta)
        dbias_ref[...] += ds
        dq_ref[...] = (_dot(ds, k) * (dh ** -0.5)).astype(dq_ref.dtype)
        rows = pl.ds(pl.multiple_of(i * QB, QB), KW)
        dk_ref[rows, :] += _dot(ds.T, q)
        dv_ref[rows, :] += _dot(pr.T, dov)

    q_spec, k_specs, v_specs, bias_spec = _attn_specs(dh, off["aq"] // dh, off["ak"] // dh, off["av"] // dh)
    acc_spec = pl.BlockSpec((tp, dh), lambda h, i: (0, h))
    return pl.pallas_call(
        body, name="attn_bwd", grid=(ATTN_HEADS, t // QB),
        in_specs=[q_spec] + k_specs + v_specs + [bias_spec, pl.BlockSpec((QB, dh), lambda h, i: (i, h))],
        out_specs=[pl.BlockSpec((QB, dh), lambda h, i: (i, h)), acc_spec, acc_spec, bias_spec],
        out_shape=[_sds((t, d), BF16), _sds((tp, d), F32), _sds((tp, d), F32),
                   _sds((ATTN_HEADS, QB, KW), F32)],
        compiler_params=_cp(("parallel", "arbitrary")))(p, p, p, p, p, p, p, bias, do)


def _onehot_mm(name, a, b):
    def body(a_ref, b_ref, o_ref):
        o_ref[...] = _dot_hi(a_ref[...], b_ref[...])

    return pl.pallas_call(body, name=name, out_shape=_sds((a.shape[0], b.shape[1]), F32),
                          compiler_params=pltpu.CompilerParams(vmem_limit_bytes=VMEM_LIMIT))(a, b)


def _diag_index():
    ii, jj = np.arange(CHUNK)[:, None], np.arange(CHUNK)[None, :]
    return (ii - jj + CHUNK - 1).reshape(-1)


def _bias_expand(rel_bias):
    h = rel_bias.shape[0]
    nq, nk, shift = QB // CHUNK, KW // CHUNK, (2 * QB) // CHUNK
    nbin, ndc = 3 * LANE, 4
    rb = jnp.pad(rel_bias, ((0, 0), (0, nbin - rel_bias.shape[1])))
    win = np.clip(CHUNK * np.arange(ndc)[:, None] + np.arange(LANE)[None, :] - (CHUNK - 1), -REL_CLIP, REL_CLIP)
    sel = (jnp.arange(nbin)[:, None] == jnp.asarray((win + REL_CLIP).reshape(1, -1))).astype(F32)
    windows = _onehot_mm("bias_windows", rb, sel)
    diag_t = (jnp.arange(LANE)[:, None] == jnp.asarray(_diag_index().reshape(1, -1))).astype(F32)
    blocks = _onehot_mm("bias_blocks", windows.reshape(h * ndc, LANE), diag_t).reshape(h, ndc, CHUNK, CHUNK)
    off_band = jnp.full((h, CHUNK, CHUNK), NEG_INF, F32)
    rows = []
    for ic in range(nq):
        dcs = [ic - jc + shift for jc in range(nk)]
        rows.append(jnp.concatenate([blocks[:, min(dc, ndc - 1)] if 0 <= dc <= ATTN_LEFT else off_band
                                     for dc in dcs], axis=2))
    return jnp.concatenate(rows, axis=1)


def _bias_reduce(dbias):
    h = dbias.shape[0]
    nq, nk = QB // CHUNK, KW // CHUNK
    nbin = 3 * LANE
    blocks = dbias.reshape(h, nq, CHUNK, nk, CHUNK).transpose(0, 1, 3, 2, 4).reshape(h * nq * nk, CHUNK * CHUNK)
    diag = (jnp.asarray(_diag_index().reshape(-1, 1)) == jnp.arange(LANE)[None, :]).astype(F32)
    ic = np.arange(nq)[:, None, None]
    jc = np.arange(nk)[None, :, None]
    dl = np.arange(LANE)[None, None, :] - (CHUNK - 1)
    rel = np.clip(CHUNK * (ic - jc + (2 * QB) // CHUNK) + dl, -REL_CLIP, REL_CLIP) + REL_CLIP
    bins = (jnp.asarray(rel.reshape(-1, 1)) == jnp.arange(nbin)[None, :]).astype(F32)

    diags = _onehot_mm("bias_diag_sums", blocks, diag)
    out = _onehot_mm("bias_bin_sums", diags.reshape(h, nq * nk * LANE), bins)
    return out[:, :2 * REL_CLIP + 1]


def _tri(lower):
    r = lax.broadcasted_iota(jnp.int32, (CHUNK, CHUNK), 0)
    c = lax.broadcasted_iota(jnp.int32, (CHUNK, CHUNK), 1)
    return (r >= c) if lower else (r <= c)


def _lin_prep(gla, q, k, aux):
    dk = q.shape[-1]
    if gla:
        glr, wlr, blr = aux
        q = q * (dk ** -0.5)
        pre = _dot(glr, wlr) + blr
        log_a = (jnp.minimum(pre, 0.0) - jnp.log(1.0 + jnp.exp(-jnp.abs(pre)))) / GATE_NORM
        b = _dot_hi(_tri(True).astype(F32), log_a)
        return q, k, b, pre
    cs, sn, lg = aux
    half = dk // 2
    q = q * cs + pltpu.roll(q, half, 1) * sn
    k = (k * cs + pltpu.roll(k, half, 1) * sn) * (dk ** -0.5)
    pos = lax.broadcasted_iota(jnp.int32, (CHUNK, dk), 0).astype(F32) + 1.0
    return q, k, pos * lg, None


def _lin_chunk(q, k, v, b, st):
    eb, enb = jnp.exp(b), jnp.exp(-b)
    last = b[CHUNK - 1:CHUNK, :]
    qf, kf, qb, kb = q * eb, k * enb, q * enb, k * eb
    kl = k * jnp.exp(last - b)
    s = jnp.where(_tri(True), _dot_nt(qf, kf), _dot_nt(qb, kb))
    o = _dot(s, v) + _dot_nt(qf, st)
    st_new = st * jnp.exp(last) + _dot(v.T, kl)
    return o, st_new, (eb, enb, last, qf, kf, qb, kb, kl, s)


def _lin_norm_gate(gla, o, gate, gn):
    sg = _sigmoid(gate)
    silu = gate * sg
    if gla:
        r = lax.rsqrt(jnp.mean(o * o, axis=-1, keepdims=True) + LN_EPS)
        hn = o * r
        return silu * (hn * gn), (sg, silu, r, hn)
    mu = jnp.mean(o, axis=-1, keepdims=True)
    oc = o - mu
    r = lax.rsqrt(jnp.mean(oc * oc, axis=-1, keepdims=True) + LN_EPS)
    hn = oc * r
    return silu * hn, (sg, silu, r, hn)


HPS = 2


def _lin_specs(gla, dk, dv, off, rev, nb):
    pre = "g" if gla else "r"
    wk, wv = HPS * dk, HPS * dv
    qc, kc, vc, gc = (off[pre + "q"] // wk, off[pre + "k"] // wk, off[pre + "v"] // wv, off[pre + "g"] // wv)

    def blk(i):
        return nb - 1 - i if rev else i

    specs = [pl.BlockSpec((LB, wk), lambda g, i: (blk(i), qc + g)),
             pl.BlockSpec((LB, wk), lambda g, i: (blk(i), kc + g)),
             pl.BlockSpec((LB, wv), lambda g, i: (blk(i), vc + g)),
             pl.BlockSpec((LB, wv), lambda g, i: (blk(i), gc + g))]
    if gla:
        specs += [pl.BlockSpec((LB, LANE), lambda g, i: (blk(i), off["glr"] // LANE)),
                  pl.BlockSpec((LANE, wk), lambda g, i: (0, g)),
                  pl.BlockSpec((1, wk), lambda g, i: (0, g)),
                  pl.BlockSpec((1, dv), lambda g, i: (0, 0))]
    else:
        specs += [pl.BlockSpec((LB, dk), lambda g, i: (blk(i), 0)),
                  pl.BlockSpec((LB, dk), lambda g, i: (blk(i), 0)),
                  pl.BlockSpec((HPS, 1, dk), lambda g, i: (g, 0, 0))]
    return specs, blk


def _lin_aux(gla, refs, rows, hh, dk):
    if gla:
        glr_ref, wlr_ref, blr_ref, gn_ref = refs
        kcols = pl.ds(hh * dk, dk)
        return (glr_ref[rows, :], wlr_ref[:, kcols], blr_ref[:, kcols]), gn_ref[...]
    cs_ref, sn_ref, lg_ref = refs
    return (cs_ref[rows, :], sn_ref[rows, :], lg_ref[hh]), None


def _lin_fwd(gla, p, aux_arrays, d, off):
    t = p.shape[0]
    dk, dv = d // (2 * LIN_HEADS), d // LIN_HEADS
    nb, cb = t // LB, LB // CHUNK
    naux = len(aux_arrays)

    def body(*refs):
        q_ref, k_ref, v_ref, g_ref = refs[:4]
        aux_refs = refs[4:4 + naux]
        o_ref, bo_ref, st_out_ref, st_ref = refs[4 + naux:]

        @pl.when(pl.program_id(1) == 0)
        def _():
            st_ref[...] = jnp.zeros_like(st_ref)

        st = [st_ref[hh] for hh in range(HPS)]
        for c in range(cb):
            rows = pl.ds(c * CHUNK, CHUNK)
            for hh in range(HPS):
                kcols, vcols = pl.ds(hh * dk, dk), pl.ds(hh * dv, dv)
                aux, gn = _lin_aux(gla, aux_refs, rows, hh, dk)
                q, k, b, _ = _lin_prep(gla, q_ref[rows, kcols], k_ref[rows, kcols], aux)
                st_out_ref[hh, c] = st[hh]
                o, st[hh], _ = _lin_chunk(q, k, v_ref[rows, vcols], b, st[hh])
                o_ref[rows, vcols] = o
                out, _ = _lin_norm_gate(gla, o, g_ref[rows, vcols], gn)
                bo_ref[rows, vcols] = out.astype(BF16)
        for hh in range(HPS):
            st_ref[hh] = st[hh]

    specs, _ = _lin_specs(gla, dk, dv, off, False, nb)
    orow = pl.BlockSpec((LB, HPS * dv), lambda g, i: (i, g))
    return pl.pallas_call(
        body, name="gla_fwd" if gla else "ret_fwd", grid=(LIN_HEADS // HPS, nb), in_specs=specs,
        out_specs=[orow, orow, pl.BlockSpec((HPS, cb, dv, dk), lambda g, i: (g, i, 0, 0))],
        out_shape=[_sds((t, d), F32), _sds((t, d), BF16), _sds((LIN_HEADS, t // CHUNK, dv, dk), F32)],
        scratch_shapes=[pltpu.VMEM((HPS, dv, dk), F32)],
        compiler_params=_cp(("parallel", "arbitrary")))(p, p, p, p, *aux_arrays)


def _lin_bwd(gla, p, aux_arrays, o, states, dbo, d, off):
    t = p.shape[0]
    dk, dv = d // (2 * LIN_HEADS), d // LIN_HEADS
    nb, cb = t // LB, LB // CHUNK
    naux = len(aux_arrays)

    def body(*refs):
        q_ref, k_ref, v_ref, g_ref = refs[:4]
        aux_refs = refs[4:4 + naux]
        o_ref, st_in_ref, dbo_ref = refs[4 + naux:7 + naux]
        outs = refs[7 + naux:]
        dq_ref, dk_ref, dv_ref, dg_ref = outs[:4]
        dst_ref = outs[-1]
        first = pl.program_id(1) == 0

        @pl.when(first)
        def _():
            dst_ref[...] = jnp.zeros_like(dst_ref)

        if gla:
            dpre_ref, dblr_ref, dgn_ref = outs[4:7]

            @pl.when(first)
            def _():
                dblr_ref[...] = jnp.zeros_like(dblr_ref)
                dgn_ref[...] = jnp.zeros_like(dgn_ref)

        dst = [dst_ref[hh] for hh in range(HPS)]
        for c in reversed(range(cb)):
            rows = pl.ds(c * CHUNK, CHUNK)
            for hh in range(HPS):
                kcols, vcols = pl.ds(hh * dk, dk), pl.ds(hh * dv, dv)
                aux, gn = _lin_aux(gla, aux_refs, rows, hh, dk)
                q, k, b, pre = _lin_prep(gla, q_ref[rows, kcols], k_ref[rows, kcols], aux)
                v = v_ref[rows, vcols]
                st = st_in_ref[hh, c]
                _, _, (eb, enb, last, qf, kf, qb, kb, kl, s) = _lin_chunk(q, k, v, b, st)
                gate = g_ref[rows, vcols]
                dout = dbo_ref[rows, vcols]
                _, (sg, silu, r, hn) = _lin_norm_gate(gla, o_ref[rows, vcols], gate, gn)
                dsilu = sg * (1.0 + gate * (1.0 - sg))
                if gla:
                    y = hn * gn
                    dy = dout * silu
                    dg_ref[rows, vcols] = (dout * y * dsilu).astype(BF16)
                    dgn_ref[hh] += jnp.sum(dy * hn, axis=0, keepdims=True)
                    dhn = dy * gn
                    do = r * (dhn - hn * jnp.mean(dhn * hn, axis=-1, keepdims=True))
                else:
                    dhn = dout * silu
                    dg_ref[rows, vcols] = (dout * hn * dsilu).astype(BF16)
                    do = r * (dhn - jnp.mean(dhn, axis=-1, keepdims=True)
                              - hn * jnp.mean(dhn * hn, axis=-1, keepdims=True))
                dstn = dst[hh]
                dec = jnp.exp(last)
                ds = _dot_nt(do, v)
                low = _tri(True)
                dsf = jnp.where(low, ds, 0.0)
                dsb = jnp.where(low, 0.0, ds)
                dvv = _dot(s.T, do) + _dot_nt(kl, dstn)
                dqf = _dot(dsf, kf) + _dot(do, st)
                dkf = _dot(dsf.T, qf)
                dqb = _dot(dsb, kb)
                dkb = _dot(dsb.T, qb)
                dkl = _dot(v, dstn)
                dst[hh] = dstn * dec + _dot(do.T, qf)
                dq = dqf * eb + dqb * enb
                dkk = dkf * enb + dkb * eb + dkl * jnp.exp(last - b)
                dv_ref[rows, vcols] = dvv.astype(BF16)
                if gla:
                    ddec = jnp.sum(dstn * st, axis=0, keepdims=True)
                    db = dqf * qf - dkf * kf - dqb * qb + dkb * kb - dkl * kl
                    dlast = jnp.sum(dkl * kl, axis=0, keepdims=True) + ddec * dec
                    rowi = lax.broadcasted_iota(jnp.int32, db.shape, 0)
                    db = db + jnp.where(rowi == CHUNK - 1, dlast, 0.0)
                    dlog_a = _dot_hi(_tri(False).astype(F32), db)
                    dpre = dlog_a * (1.0 / GATE_NORM) * (1.0 - _sigmoid(pre))
                    dpre_ref[rows, kcols] = dpre
                    dblr_ref[hh] += jnp.sum(dpre, axis=0, keepdims=True)
                    dq_ref[rows, kcols] = (dq * (dk ** -0.5)).astype(BF16)
                    dk_ref[rows, kcols] = dkk.astype(BF16)
                else:
                    cs, sn, _ = aux
                    half = dk // 2
                    dkk = dkk * (dk ** -0.5)
                    dq_ref[rows, kcols] = (dq * cs + pltpu.roll(dq * sn, half, 1)).astype(BF16)
                    dk_ref[rows, kcols] = (dkk * cs + pltpu.roll(dkk * sn, half, 1)).astype(BF16)
        for hh in range(HPS):
            dst_ref[hh] = dst[hh]

    specs, blk = _lin_specs(gla, dk, dv, off, True, nb)
    vrow = pl.BlockSpec((LB, HPS * dv), lambda g, i: (blk(i), g))
    krow = pl.BlockSpec((LB, HPS * dk), lambda g, i: (blk(i), g))
    specs += [vrow, pl.BlockSpec((HPS, cb, dv, dk), lambda g, i: (g, blk(i), 0, 0)), vrow]
    out_specs = [krow, krow, vrow, vrow]
    out_shape = [_sds((t, d // 2), BF16), _sds((t, d // 2), BF16), _sds((t, d), BF16), _sds((t, d), BF16)]
    if gla:
        out_specs += [krow, pl.BlockSpec((HPS, 1, dk), lambda g, i: (g, 0, 0)),
                      pl.BlockSpec((HPS, 1, dv), lambda g, i: (g, 0, 0))]
        out_shape += [_sds((t, d // 2), F32), _sds((LIN_HEADS, 1, dk), F32), _sds((LIN_HEADS, 1, dv), F32)]
    out_specs.append(pl.BlockSpec((HPS, dv, dk), lambda g, i: (g, 0, 0)))
    out_shape.append(_sds((LIN_HEADS, dv, dk), F32))
    res = pl.pallas_call(
        body, name="gla_bwd" if gla else "ret_bwd", grid=(LIN_HEADS // HPS, nb), in_specs=specs,
        out_specs=out_specs, out_shape=out_shape,
        compiler_params=_cp(("parallel", "arbitrary")))(p, p, p, p, *aux_arrays, o, states, dbo)
    return res[:-1]


def _row_tile(rows, cols):
    cap = max(8, (2 << 20) // (4 * cols))
    t = rows
    while t > cap and t % 2 == 0:
        t //= 2
    return t


def _add_half(name, g, t, sel):
    nchip, hr, cols = t.shape
    tr = _row_tile(hr, cols)
    nb = hr // tr

    def body(sel_ref, g_ref, t_ref, o_ref):
        o_ref[...] = g_ref[...] + t_ref[...]

    half = pl.BlockSpec((None, tr, cols), lambda p, i, s: (p, i, 0))
    gs = pltpu.PrefetchScalarGridSpec(
        num_scalar_prefetch=1, grid=(nchip, nb),
        in_specs=[pl.BlockSpec((None, tr, cols), lambda p, i, s: (p, s[0] * nb + i, 0)), half], out_specs=half)
    return pl.pallas_call(body, name=name, grid_spec=gs, out_shape=_sds(t.shape, F32),
                          compiler_params=_cp(("parallel", "parallel")))(sel, g, t)


def _sum_shards(name, h, rcv, sel):
    _, rows, cols = h.shape
    tr = _row_tile(rows, cols)

    def body(sel_ref, h_ref, r0, r1, r2, o_ref):
        o_ref[...] = ((h_ref[...] + r0[...]) + r1[...]) + r2[...]

    rspecs = [pl.BlockSpec((None, tr, cols), functools.partial(lambda i, s, j: (j, i, 0), j=j)) for j in range(3)]
    gs = pltpu.PrefetchScalarGridSpec(
        num_scalar_prefetch=1, grid=(rows // tr,),
        in_specs=[pl.BlockSpec((None, tr, cols), lambda i, s: (s[0], i, 0))] + rspecs,
        out_specs=pl.BlockSpec((tr, cols), lambda i, s: (i, 0)))
    return pl.pallas_call(body, name=name, grid_spec=gs, out_shape=_sds((rows, cols), F32),
                          compiler_params=_cp(("parallel",)))(sel, h, rcv, rcv, rcv)


def _adamw_math(w, g, m, v):
    c1 = 1.0 - ADAM_B1 ** ADAM_STEP
    c2 = 1.0 - ADAM_B2 ** ADAM_STEP
    nm = ADAM_B1 * m + (1.0 - ADAM_B1) * g
    nv = ADAM_B2 * v + (1.0 - ADAM_B2) * jnp.square(g)
    return -ADAM_LR * ((nm / c1) / (jnp.sqrt(nv / c2) + ADAM_EPS) + ADAM_WD * w), nm, nv


def _adamw(name, w, g, m, v):
    rows, cols = w.shape
    tr = _row_tile(rows, cols)

    def body(w_ref, g_ref, m_ref, v_ref, d_ref, nm_ref, nv_ref):
        d_ref[...], nm_ref[...], nv_ref[...] = _adamw_math(w_ref[...], g_ref[...], m_ref[...], v_ref[...])

    spec = pl.BlockSpec((tr, cols), lambda i: (i, 0))
    return pl.pallas_call(body, name=name, grid=(rows // tr,), in_specs=[spec] * 4, out_specs=[spec] * 3,
                          out_shape=[_sds((rows, cols), F32)] * 3, compiler_params=_cp(("parallel",)))(w, g, m, v)


def _adamw_layer(name, w, g_own, g_sib, sel, m, v, layer, prev):
    depth, rows, cols = w.shape
    tr = _row_tile(rows // 2, cols)
    nbh = rows // 2 // tr
    nprev = 0 if prev is None else 4

    def body(sel_ref, w_ref, own_ref, sib_ref, m_ref, v_ref, *rest):
        go_ref, d_ref, nm_ref, nv_ref = rest[nprev:]
        gv = jnp.where(pl.program_id(0) // nbh == sel_ref[0], own_ref[...], sib_ref[...])
        go_ref[...] = gv
        d_ref[...], nm_ref[...], nv_ref[...] = _adamw_math(w_ref[...], gv, m_ref[...], v_ref[...])

    lay = pl.BlockSpec((None, tr, cols), lambda i, s: (layer, i, 0))
    hlf = pl.BlockSpec((tr, cols), lambda i, s: (i % nbh, 0))
    gs = pltpu.PrefetchScalarGridSpec(
        num_scalar_prefetch=1, grid=(2 * nbh,), in_specs=[lay, hlf, hlf, lay, lay] + [ANY] * nprev,
        out_specs=[lay] * 4)
    args = (sel, w, g_own, g_sib, m, v) + (() if prev is None else tuple(prev))
    return pl.pallas_call(
        body, name=name, grid_spec=gs, out_shape=[_sds((depth, rows, cols), F32)] * 4,
        input_output_aliases={6 + k: k for k in range(nprev)},
        compiler_params=_cp(("parallel",)))(*args)


def _place():
    x, y, c = (lax.axis_index(a) for a in MESH_AXES)
    chips = [(1 - x, y), (x, 1 - y), (1 - x, 1 - y)]
    return x, y, c, chips


def _chip_index(xy):
    return 2 * xy[0] + xy[1]


ANY = pl.BlockSpec(memory_space=pl.ANY)


HBM_SPEC = pl.BlockSpec(memory_space=pltpu.HBM)
SEM = pl.BlockSpec(memory_space=pltpu.SEMAPHORE)
EFFECT = pltpu.SideEffectType.DATAFLOW_SIDE_EFFECTING


def _half(ref, c):
    hr = ref.shape[-2] // 2
    return pl.ds(pl.multiple_of(c * hr, 16), hr)


def _gather_copies(srcs, lands, send, recv):
    x, y, c, chips = _place()
    me = _chip_index((x, y))
    return [pltpu.make_async_remote_copy(src_ref=s.at[_half(s, c)], dst_ref=g.at[me, _half(s, c)],
                                         send_sem=send.at[3 * a + j], recv_sem=recv.at[3 * a + j],
                                         device_id=(*ch, c), device_id_type=DEV)
            for a, (s, g) in enumerate(zip(srcs, lands)) for j, ch in enumerate(chips)]


def _scatter_copies(srcs, lands, send, recv):
    x, y, c, chips = _place()
    return [pltpu.make_async_remote_copy(src_ref=h.at[_chip_index(ch)], dst_ref=r.at[j],
                                         send_sem=send.at[3 * a + j], recv_sem=recv.at[3 * a + j],
                                         device_id=(*ch, c), device_id_type=DEV)
            for a, (h, r) in enumerate(zip(srcs, lands)) for j, ch in enumerate(chips)]


def _in_hbm(a):
    return pltpu.with_memory_space_constraint(a, pltpu.HBM)


def _split_start(name, srcs, land_shapes, copies_fn, after=None):
    ns, nl = len(srcs), len(land_shapes)
    ncp = 3 * ns
    lands = [lax.empty(s.shape, s.dtype) for s in land_shapes]
    behind = [] if after is None else [after]

    def body(*refs):
        src, land = refs[:ns], refs[ns:ns + nl]
        send, recv = refs[ns + nl + len(behind)], refs[ns + nl + len(behind) + 1]
        for cp in copies_fn(src, land, send, recv):
            cp.start()
        refs[-1][...] = jnp.zeros_like(refs[-1])

    bufs = list(srcs) + lands
    outs = pl.pallas_call(
        body, name=name, in_specs=[HBM_SPEC] * (ns + nl) + [ANY] * len(behind),
        out_specs=[SEM, SEM] + [HBM_SPEC] * (ns + nl) + [pl.BlockSpec(memory_space=pltpu.VMEM)],
        out_shape=[pltpu.SemaphoreType.DMA((ncp,)), pltpu.SemaphoreType.DMA((ncp,))]
        + [pltpu.HBM(b.shape, b.dtype) for b in bufs] + [_sds((8, LANE), F32)],
        input_output_aliases={i: 2 + i for i in range(ns + nl)},
        compiler_params=pltpu.CompilerParams(has_side_effects=EFFECT))(*[_in_hbm(b) for b in bufs], *behind)
    return outs[0], outs[1], list(outs[2:2 + ns]), list(outs[2 + ns:2 + ns + nl]), outs[-1]


def _split_wait(name, started, copies_fn, after):
    send, recv, srcs, lands, _ = started
    ns, nl = len(srcs), len(lands)

    def body(*refs):
        src, land = refs[:ns], refs[ns:ns + nl]
        for cp in copies_fn(src, land, refs[ns + nl], refs[ns + nl + 1]):
            cp.wait_send()
            cp.wait_recv()

    bufs = list(srcs) + list(lands)
    outs = pl.pallas_call(
        body, name=name, in_specs=[HBM_SPEC] * (ns + nl) + [SEM, SEM, ANY], out_specs=[HBM_SPEC] * (ns + nl),
        out_shape=[pltpu.HBM(b.shape, b.dtype) for b in bufs],
        input_output_aliases={i: i for i in range(ns + nl)},
        compiler_params=pltpu.CompilerParams(has_side_effects=EFFECT))(*bufs, send, recv, after)
    return list(outs[:ns]), list(outs[ns:])


def _gather_plain(name, srcs):
    n = len(srcs)

    def body(*refs):
        src, land = refs[:n], refs[n:2 * n]
        send, recv, fsend, frecv = refs[2 * n:]
        first = _gather_copies(src, land, send, recv)
        for cp in first:
            cp.start()
        _forward_body(land, first, fsend, frecv)

    return pl.pallas_call(
        body, name=name, in_specs=[ANY] * n, out_specs=[ANY] * n,
        out_shape=[_sds((4,) + s.shape, s.dtype) for s in srcs],
        scratch_shapes=[pltpu.SemaphoreType.DMA((3 * n,))] * 4)(*srcs)


def _forward_body(land, arrivals, fsend, frecv):
    x, y, c, chips = _place()
    n = len(land)
    passed = []
    for a in range(n):
        for j, ch in enumerate(chips):
            if arrivals is not None:
                arrivals[3 * a + j].wait_recv()
            slot = land[a].at[_chip_index(ch), _half(land[a], c)]
            fw = pltpu.make_async_remote_copy(src_ref=slot, dst_ref=slot, send_sem=fsend.at[3 * a + j],
                                              recv_sem=frecv.at[3 * a + j], device_id=(x, y, 1 - c),
                                              device_id_type=DEV)
            fw.start()
            passed.append(fw)
    for a in range(n):
        for j, ch in enumerate(chips):
            slot = land[a].at[_chip_index(ch), _half(land[a], 1 - c)]
            pltpu.make_async_remote_copy(src_ref=slot, dst_ref=slot, send_sem=fsend.at[3 * a + j],
                                         recv_sem=frecv.at[3 * a + j], device_id=(x, y, c),
                                         device_id_type=DEV).wait_recv()
    for cp in passed:
        cp.wait_send()
    if arrivals is not None:
        for cp in arrivals:
            cp.wait_send()


def _gather_forward(name, lands):
    n = len(lands)

    def body(*refs):
        _forward_body(refs[n:2 * n], None, refs[2 * n], refs[2 * n + 1])

    return pl.pallas_call(
        body, name=name, in_specs=[ANY] * n, out_specs=[ANY] * n,
        out_shape=[_sds(g.shape, g.dtype) for g in lands], input_output_aliases={a: a for a in range(n)},
        scratch_shapes=[pltpu.SemaphoreType.DMA((3 * n,))] * 2)(*lands)


def _sibling_halves(name, grs):
    n = len(grs)

    def body(*refs):
        ins, outs = refs[:n], refs[n:2 * n]
        send, recv = refs[2 * n:]
        x, y, c, _ = _place()
        cps = [pltpu.make_async_remote_copy(src_ref=ins[a].at[:, _half(ins[a], 1 - c)], dst_ref=outs[a],
                                            send_sem=send.at[a], recv_sem=recv.at[a], device_id=(x, y, 1 - c),
                                            device_id_type=DEV) for a in range(n)]
        for cp in cps:
            cp.start()
        for cp in cps:
            cp.wait()

    return pl.pallas_call(
        body, name=name, in_specs=[ANY] * n, out_specs=[ANY] * n,
        out_shape=[_sds((g.shape[0], g.shape[1] // 2, g.shape[2]), F32) for g in grs],
        scratch_shapes=[pltpu.SemaphoreType.DMA((n,)), pltpu.SemaphoreType.DMA((n,))])(*grs)


def _sibling_share(name, sms):
    n = len(sms)

    def body(*refs):
        ins, outs = refs[:n], refs[n:2 * n]
        send, recv = refs[2 * n:]
        x, y, c, _ = _place()
        cps = [pltpu.make_async_remote_copy(src_ref=ins[a], dst_ref=outs[a], send_sem=send.at[a],
                                            recv_sem=recv.at[a], device_id=(x, y, 1 - c), device_id_type=DEV)
               for a in range(n)]
        for cp in cps:
            cp.start()
        for cp in cps:
            cp.wait()

    return pl.pallas_call(
        body, name=name, in_specs=[ANY] * n, out_specs=[ANY] * n, out_shape=[_sds(s.shape, F32) for s in sms],
        scratch_shapes=[pltpu.SemaphoreType.DMA((n,))] * 2)(*sms)


def _small_allreduce(v):
    rows = v.shape[0]
    ndev = 8

    def body(v_ref, o_ref, gat_ref, send, recv):
        x, y, c, _ = _place()
        me = 4 * x + 2 * y + c
        cps = []
        for k in range(1, ndev):
            to = (me + k) % ndev
            cp = pltpu.make_async_remote_copy(src_ref=v_ref, dst_ref=gat_ref.at[me], send_sem=send.at[k - 1],
                                              recv_sem=recv.at[me], device_id=(to // 4, (to // 2) % 2, to % 2),
                                              device_id_type=DEV)
            cp.start()
            cps.append(cp)
        gat_ref[me] = v_ref[...]
        for k in range(1, ndev):
            frm = (me + k) % ndev
            pltpu.make_async_remote_copy(src_ref=v_ref, dst_ref=gat_ref.at[frm], send_sem=send.at[k - 1],
                                         recv_sem=recv.at[frm], device_id=(x, y, c), device_id_type=DEV).wait_recv()
        for cp in cps:
            cp.wait_send()
        acc = gat_ref[0]
        for k in range(1, ndev):
            acc = acc + gat_ref[k]
        o_ref[...] = acc

    vm = pl.BlockSpec(memory_space=pltpu.VMEM)
    return pl.pallas_call(
        body, name="small_allreduce", in_specs=[vm], out_specs=vm, out_shape=_sds((rows, LANE), F32),
        scratch_shapes=[pltpu.VMEM((ndev, rows, LANE), F32), pltpu.SemaphoreType.DMA((ndev - 1,)),
                        pltpu.SemaphoreType.DMA((ndev,))])(v)


def _layout(d):
    half = d // 2
    names = [("aq", d), ("ak", d), ("av", d), ("rq", half), ("rk", half), ("rv", d), ("rg", d),
             ("gq", half), ("gk", half), ("gv", d), ("gg", d), ("gates", 3 * d), ("glr", 2 * LANE)]
    off, pos = {}, 0
    for nm, sz in names:
        off[nm] = pos
        pos += sz
    return off, pos


def _pad_cols(w, d):
    a = 8 * d + d
    lr = w[..., a:a + GATE_RANK]
    z = jnp.zeros(w.shape[:-1] + (2 * LANE - GATE_RANK,), w.dtype)
    return jnp.concatenate([w[..., :a], w[..., a + GATE_RANK:], lr, z], axis=-1)


def _unpad_cols(g, d):
    a = 8 * d + d
    return jnp.concatenate([g[..., :a], g[..., a + 3 * d:a + 3 * d + GATE_RANK], g[..., a:a + 3 * d]], axis=-1)


def kernel(x, ln_in_g, ln_in_b, w_in, rel_bias, gla_w_lr, gla_b_lr, gla_norm_g, w_branch, w_out, ln1_g, ln1_b, w_up, w_down, ln2_g, ln2_b, loss_target, m_ln_in_g, m_ln_in_b, m_w_in, m_rel_bias, m_gla_w_lr, m_gla_b_lr, m_gla_norm_g, m_w_branch, m_w_out, m_ln1_g, m_ln1_b, m_w_up, m_w_down, m_ln2_g, m_ln2_b, v_ln_in_g, v_ln_in_b, v_w_in, v_rel_bias, v_gla_w_lr, v_gla_b_lr, v_gla_norm_g, v_w_branch, v_w_out, v_ln1_g, v_ln1_b, v_w_up, v_w_down, v_ln2_g, v_ln2_b):
    t, d = x.shape[1], x.shape[2]
    dff = 4 * d
    half = d // 2
    off, npad = _layout(d)
    xi, yi, ci = (lax.axis_index(a) for a in MESH_AXES)
    chip = 2 * xi + yi
    csel = jnp.reshape(ci, (1,)).astype(jnp.int32)
    psel = jnp.reshape(chip, (1,)).astype(jnp.int32)

    big_w = [w_in, w_branch.reshape(DEPTH, -1, d), w_out, w_up, w_down]
    big_m = [m_w_in, m_w_branch.reshape(DEPTH, -1, d), m_w_out, m_w_up, m_w_down]
    big_v = [v_w_in, v_w_branch.reshape(DEPTH, -1, d), v_w_out, v_w_up, v_w_down]
    W_IN, REST = [0], [1, 2, 3, 4]

    def shards_of(l, idx):
        return [big_w[i][l].astype(BF16) for i in idx]

    def lands_of(srcs):
        return [_sds((4,) + s.shape, s.dtype) for s in srcs]

    def full_w_in(g):
        return _pad_cols(jnp.transpose(g, (1, 0, 2)).reshape(d, -1), d)

    def full_rest(gs):
        g_br, g_out, g_up, g_down = gs
        return (jnp.transpose(g_br.reshape(4, N_BRANCH, d // 4, d), (1, 0, 2, 3)).reshape(N_BRANCH, d, d),
                g_out.reshape(d, d), jnp.transpose(g_up, (1, 0, 2)).reshape(d, dff), g_down.reshape(dff, d))

    def with_own(srcs, lands):
        return [lax.dynamic_update_slice(g, s[None], (chip, 0, 0)) for s, g in zip(srcs, lands)]

    def gather_start(tag, l, idx, after):
        srcs = shards_of(l, idx)
        return srcs, _split_start(f"gather_{tag}{l}_start", srcs, lands_of(srcs), _gather_copies, after)

    def gather_finish(tag, l, pending, after):
        srcs, started = pending
        _, lands = _split_wait(f"gather_{tag}{l}_wait", started, _gather_copies, after)
        return with_own(srcs, _gather_forward(f"gather_{tag}{l}_pass", lands))

    def token(pending):
        return pending[1][4][0, 0]

    win, wbr, wout, wup, wdown = ([None] * DEPTH for _ in range(5))
    src_first = shards_of(0, W_IN)
    g_first = with_own(src_first, _gather_plain("gather_in0", src_first))
    win[0] = full_w_in(g_first[0])

    dkh = half // LIN_HEADS
    lr_rows = DEPTH * GATE_RANK
    lr_slab = jnp.zeros((lr_rows, 4, half // 4), F32)
    lr_slab = lax.dynamic_update_slice(lr_slab, (gla_w_lr.reshape(lr_rows, 1, half // 4) * jnp.where(ci == 0, 1.0, 0.0)),
                                       (0, chip, 0))
    wlr_full = _small_allreduce(lr_slab.reshape(-1, LANE)).reshape(DEPTH, GATE_RANK, half)
    wlr_pad = jnp.concatenate([wlr_full, jnp.zeros((DEPTH, LANE - GATE_RANK, half), F32)], axis=1)
    pend_rest = gather_start("rest", 0, REST, wlr_full[0, :1, :1] + g_first[0][0, :1, :1].astype(F32))

    inv = 10000.0 ** (-jnp.arange(0, dkh, 2, dtype=F32) / dkh)
    ang = jnp.arange(t, dtype=F32)[:, None] * inv[None, :]
    cos, sin = jnp.cos(ang), jnp.sin(ang)
    rope_c = jnp.concatenate([cos, cos], axis=1)
    rope_s = jnp.concatenate([-sin, sin], axis=1)
    log_gamma = jnp.log1p(-jnp.exp2(-5.0 - jnp.arange(LIN_HEADS, dtype=F32)))
    lg_tab = jnp.broadcast_to(log_gamma[:, None, None], (LIN_HEADS, 1, dkh))

    def vec(a):
        return a.reshape(1, -1)

    x0, x0b, xh_in, rs_in = _ln_in(x[0], vec(ln_in_g) + token(pend_rest), vec(ln_in_b))
    saved = []
    xl, xlb = x0, x0b
    for l in range(DEPTH):
        p = _mm("proj_in", xlb, win[l], 512, 1792)
        g_rest = gather_finish("rest", l, pend_rest, p)
        wbr[l], wout[l], wup[l], wdown[l] = full_rest(g_rest)
        tok = 0.0
        if l + 1 < DEPTH:
            pend_in = gather_start("in", l + 1, W_IN, g_rest[0])
            tok = token(pend_in)
        bias = _bias_expand(rel_bias[l] + tok)
        attn = _attn_fwd(p, bias, d, off)
        ret_aux = (rope_c, rope_s, lg_tab + tok)
        gla_aux = (p, wlr_pad[l], vec(gla_b_lr[l]) + tok, vec(gla_norm_g[l]))
        o_ret, b_ret, st_ret = _lin_fwd(False, p, ret_aux, d, off)
        o_gla, b_gla, st_gla = _lin_fwd(True, p, gla_aux, d, off)
        tok = 0.0
        if l + 1 < DEPTH:
            g_in = gather_finish("in", l + 1, pend_in, b_gla)
            win[l + 1] = full_w_in(g_in[0])
            pend_rest = gather_start("rest", l + 1, REST, g_in[0])
            tok = token(pend_rest)
        bo = jnp.stack([attn, b_ret, b_gla])
        proj, merged = _merge_fwd(bo, wbr[l], p, off["gates"])
        x1, x1b, xh1, rs1 = _mm_res_ln("out_proj_ln", merged, wout[l], xl, vec(ln1_g[l]) + tok, vec(ln1_b[l]),
                                       256, False)
        u = _mm("mlp_up", x1b, wup[l], 512, 1024)
        x2, x2b, xh2, rs2, act = _mm_res_ln("mlp_down_ln", u, wdown[l], x1, vec(ln2_g[l]), vec(ln2_b[l]), 256, True)
        saved.append(dict(xlb=xlb, p=p, bias=bias, ret_aux=ret_aux, gla_aux=gla_aux, o_ret=o_ret, o_gla=o_gla,
                          st_ret=st_ret, st_gla=st_gla, bo=bo, proj=proj, merged=merged, x1b=x1b, xh1=xh1,
                          rs1=rs1, u=u, xh2=xh2, rs2=rs2, act=act))
        xl, xlb = x2, x2b

    small = {}
    last = saved[-1]
    loss_p, dz2, dz2b, dg, db = _loss_ln_bwd(xl, loss_target[0], last["xh2"], last["rs2"], vec(ln2_g[DEPTH - 1]))
    small["loss"] = loss_p[:, :1]
    grad_x = None

    def scatter_start(tag, l, idx, shards, after):
        theirs = _sibling_halves(f"grad_{tag}{l}_sibling", shards)
        hs = [_add_half("grad_sibling_add", g, th, csel) for g, th in zip(shards, theirs)]
        lands = [_sds((3,) + h.shape[1:], F32) for h in hs]
        return tag, l, idx, _split_start(f"grad_{tag}{l}_scatter_start", hs, lands, _scatter_copies, after)

    adam_out = [None] * len(big_w)

    def scatter_finish(pending, after):
        tag, l, idx, started = pending
        hs, rcv = _split_wait(f"grad_{tag}{l}_scatter_wait", started, _scatter_copies, after)
        sms = [_sum_shards("grad_chip_sum", h, r, psel) for h, r in zip(hs, rcv)]
        for i, own, sib in zip(idx, sms, _sibling_share(f"grad_{tag}{l}_share", sms)):
            adam_out[i] = _adamw_layer("adamw_large", big_w[i], own, sib, csel, big_m[i], big_v[i], l, adam_out[i])
        return adam_out[idx[0]][0]

    in_flight = []

    def scatter(tag, l, idx, shards, after=None):
        pending = scatter_start(tag, l, idx, shards, after)
        in_flight.append(pending)
        if len(in_flight) > 2:
            scatter_finish(in_flight.pop(0), pending[3][4])
        return pending[3][4][0, 0]

    for l in reversed(range(DEPTH)):
        s = saved[l]
        small[("ln2_g", l)], small[("ln2_b", l)] = dg, db
        du = _mm_nt_relu2_bwd(dz2b, wdown[l], s["u"])
        g_wdown = _mm("grad_w_down", s["act"].T, dz2b, 1024, 512)
        g_wup = _mm("grad_w_up", s["x1b"].T, du, 1024, 512)
        dz1, dz1b, dg1, db1 = _mm_nt_res_lnbwd("mlp_up_bwd_ln", du, wup[l], dz2, s["xh1"], s["rs1"],
                                               vec(ln1_g[l]), 256, dff)
        small[("ln1_g", l)], small[("ln1_b", l)] = dg1, db1
        dproj, dgl = _merge_bwd(dz1b, wout[l], s["proj"], s["p"], off["gates"])
        g_wout = _mm("grad_w_out", s["merged"].T, dz1b, 1024, 512)
        dbo = _mm("branch_proj_bwd", dproj, wbr[l], 512, 512, nt=True)
        g_wbr = _mm("grad_w_branch", jnp.transpose(s["bo"], (0, 2, 1)), dproj, 1024, 512)
        tok = scatter("rest", l, REST, [
            jnp.transpose(g_wbr.reshape(N_BRANCH, 4, d // 4, d), (1, 0, 2, 3)).reshape(4, -1, d),
            g_wout.reshape(4, d // 4, d), jnp.transpose(g_wup.reshape(d, 4, d), (1, 0, 2)), g_wdown.reshape(4, d, d)])
        rc, rs_, lg = s["ret_aux"]
        gp, gw, gb, gn_ = s["gla_aux"]
        dq_a, dk_acc, dv_acc, dbias = _attn_bwd(s["p"], s["bias"] + tok, dbo[0], d, off)
        small[("rel_bias", l)] = _bias_reduce(dbias)
        dk_a = dk_acc[2 * QB:].astype(BF16)
        dv_a = dv_acc[2 * QB:].astype(BF16)
        dq_r, dk_r, dv_r, dg_r = _lin_bwd(False, s["p"], (rc, rs_, lg + tok), s["o_ret"], s["st_ret"], dbo[1], d, off)
        dq_g, dk_g, dv_g, dg_g, dpre, dblr, dgn = _lin_bwd(True, s["p"], (gp, gw, gb + tok, gn_), s["o_gla"],
                                                           s["st_gla"], dbo[2], d, off)
        small[("gla_b_lr", l)] = dblr.reshape(1, half)
        small[("gla_norm_g", l)] = jnp.sum(dgn, axis=0)
        dpre_b = dpre.astype(BF16)
        glr_b = s["p"][:, off["glr"]:off["glr"] + LANE].astype(BF16)
        dglr = _mm("gate_lr_bwd", dpre_b, wlr_pad[l], 512, LANE, nt=True, out_dtype=BF16)
        small[("gla_w_lr", l)] = _mm("grad_gla_w_lr", glr_b.T, dpre_b, LANE, half)[:GATE_RANK]
        dp = jnp.concatenate([dq_a, dk_a, dv_a, dq_r, dk_r, dv_r, dg_r, dq_g, dk_g, dv_g, dg_g,
                              dgl[0], dgl[1], dgl[2], dglr, jnp.zeros((t, LANE), BF16)], axis=1)
        if l > 0:
            prev = saved[l - 1]
            xh_p, rs_p, g_p = prev["xh2"], prev["rs2"], vec(ln2_g[l - 1])
            g_win = _mm("grad_w_in", s["xlb"].T, dp, 1024, 896)
            g_p = g_p + scatter("in", l, W_IN, [jnp.transpose(_unpad_cols(g_win, d).reshape(d, 4, -1), (1, 0, 2))])
        else:
            xh_p, rs_p, g_p = xh_in, rs_in, vec(ln_in_g)
        dzp, dzpb, dg, db = _mm_nt_res_lnbwd("proj_in_bwd_ln", dp, win[l], dz1, xh_p, rs_p, g_p, 1024, 1792)
        dz2, dz2b = dzp, dzpb
        grad_x = dzp
    small["ln_in_g"], small["ln_in_b"] = dg, db
    rb_pad = 3 * LANE
    pieces = [small["loss"].reshape(-1), jnp.zeros((LANE - 1,), F32), small["ln_in_g"].reshape(-1),
              small["ln_in_b"].reshape(-1)]
    for l in range(DEPTH):
        rb = jnp.pad(small[("rel_bias", l)], ((0, 0), (0, rb_pad - (2 * REL_CLIP + 1))))
        pieces += [rb.reshape(-1), small[("gla_w_lr", l)].reshape(-1), small[("gla_b_lr", l)].reshape(-1),
                   small[("gla_norm_g", l)].reshape(-1), small[("ln1_g", l)].reshape(-1),
                   small[("ln1_b", l)].reshape(-1), small[("ln2_g", l)].reshape(-1), small[("ln2_b", l)].reshape(-1)]
    sizes = [pc.shape[0] for pc in pieces]
    packed = jnp.concatenate(pieces)
    padn = (-packed.shape[0]) % (8 * LANE)
    packed = jnp.concatenate([packed, jnp.zeros((padn,), F32)]).reshape(-1, LANE)
    red2d = _small_allreduce(packed)
    red = red2d.reshape(-1)
    g_win = _mm("grad_w_in", saved[0]["xlb"].T, dp, 1024, 896)
    scatter("in", 0, W_IN, [jnp.transpose(_unpad_cols(g_win, d).reshape(d, 4, -1), (1, 0, 2))], red2d)
    after = red2d
    while in_flight:
        after = scatter_finish(in_flight.pop(0), after)

    parts, pos = [], 0
    for sz in sizes:
        parts.append(red[pos:pos + sz])
        pos += sz
    loss = parts[0][0]
    g_ln_in_g, g_ln_in_b = parts[2], parts[3]
    per = 8
    g_rel = jnp.stack([parts[4 + per * l].reshape(ATTN_HEADS, rb_pad)[:, :2 * REL_CLIP + 1] for l in range(DEPTH)])
    g_wlr_full = jnp.stack([parts[5 + per * l].reshape(GATE_RANK, half) for l in range(DEPTH)])
    g_wlr = lax.dynamic_slice_in_dim(g_wlr_full, chip * (half // 4), half // 4, axis=2)
    g_blr = jnp.stack([parts[6 + per * l] for l in range(DEPTH)])
    g_gn = jnp.stack([parts[7 + per * l] for l in range(DEPTH)])
    g_ln1g = jnp.stack([parts[8 + per * l] for l in range(DEPTH)])
    g_ln1b = jnp.stack([parts[9 + per * l] for l in range(DEPTH)])
    g_ln2g = jnp.stack([parts[10 + per * l] for l in range(DEPTH)])
    g_ln2b = jnp.stack([parts[11 + per * l] for l in range(DEPTH)])

    grads = [g_ln_in_g, g_ln_in_b, None, g_rel, g_wlr, g_blr, g_gn, None, None, g_ln1g, g_ln1b, None, None,
             g_ln2g, g_ln2b]
    ws = [ln_in_g, ln_in_b, w_in, rel_bias, gla_w_lr, gla_b_lr, gla_norm_g, w_branch, w_out, ln1_g, ln1_b,
          w_up, w_down, ln2_g, ln2_b]
    ms = [m_ln_in_g, m_ln_in_b, m_w_in, m_rel_bias, m_gla_w_lr, m_gla_b_lr, m_gla_norm_g, m_w_branch, m_w_out,
          m_ln1_g, m_ln1_b, m_w_up, m_w_down, m_ln2_g, m_ln2_b]
    vs = [v_ln_in_g, v_ln_in_b, v_w_in, v_rel_bias, v_gla_w_lr, v_gla_b_lr, v_gla_norm_g, v_w_branch, v_w_out,
          v_ln1_g, v_ln1_b, v_w_up, v_w_down, v_ln2_g, v_ln2_b]

    deltas, new_ms, new_vs = [None] * 15, [None] * 15, [None] * 15
    big_idx = [2, 7, 8, 11, 12]
    for i, res in zip(big_idx, adam_out):
        shp = ws[i].shape
        grads[i], deltas[i], new_ms[i], new_vs[i] = (r.reshape(shp) for r in res)
    small_idx = [i for i in range(15) if i not in big_idx]

    def pack(arrs):
        flat_ = jnp.concatenate([arrs[i].reshape(-1) for i in small_idx])
        pad_ = (-flat_.shape[0]) % (8 * LANE)
        return jnp.concatenate([flat_, jnp.ones((pad_,), F32)]).reshape(-1, LANE)

    dl, nm, nv = _adamw("adamw_small", pack(ws), pack(grads), pack(ms), pack(vs))
    pos = 0
    for i in small_idx:
        sz = int(np.prod(ws[i].shape))
        deltas[i] = dl.reshape(-1)[pos:pos + sz].reshape(ws[i].shape)
        new_ms[i] = nm.reshape(-1)[pos:pos + sz].reshape(ws[i].shape)
        new_vs[i] = nv.reshape(-1)[pos:pos + sz].reshape(ws[i].shape)
        pos += sz

    return (loss, grad_x[None], *grads, *deltas, *new_ms, *new_vs)
```

```python
import functools

import numpy as np
import jax
import jax.numpy as jnp
from jax import lax
from jax.experimental import pallas as pl
from jax.experimental.pallas import tpu as pltpu

F32 = jnp.float32
BF16 = jnp.bfloat16
MXU_DTYPE = BF16
HI = lax.Precision.HIGHEST

DEPTH = 2
CHUNK = 64
N_BRANCH = 3
ATTN_HEADS = 8
ATTN_LEFT = 8
REL_CLIP = 2 * CHUNK
LIN_HEADS = 4
GATE_RANK = 16
GATE_NORM = 16.0
LN_EPS = 1e-5
NEG_INF = -1e30
ALPHA = (2 * DEPTH) ** 0.25
ADAM_LR, ADAM_B1, ADAM_B2, ADAM_EPS, ADAM_WD, ADAM_STEP = 0.001, 0.9, 0.999, 1e-08, 0.01, 10

LANE = 128
VMEM_LIMIT = 56 << 20
QB = 256
KW = 3 * QB
LB = 256
MESH_AXES = ("x", "y", "c")
DEV = pl.DeviceIdType.MESH


def _cp(sem):
    return pltpu.CompilerParams(dimension_semantics=sem, vmem_limit_bytes=VMEM_LIMIT)


def _mx(v):
    return v.astype(MXU_DTYPE)


def _dot(a, b):
    return jnp.dot(_mx(a), _mx(b), preferred_element_type=F32)


def _dot_nt(a, b):
    return lax.dot_general(_mx(a), _mx(b), (((1,), (1,)), ((), ())), preferred_element_type=F32)


def _dot_tn(a, b):
    return lax.dot_general(_mx(a), _mx(b), (((0,), (0,)), ((), ())), preferred_element_type=F32)


def _dot_hi(a, b):
    return jnp.dot(a, b, precision=HI, preferred_element_type=F32)


def _sigmoid(v):
    return 1.0 / (1.0 + jnp.exp(-v))


def _sds(shape, dtype):
    return jax.ShapeDtypeStruct(shape, dtype)


def _mm(name, a, b, tm, tn, nt=False, out_dtype=F32):
    batched = a.ndim == 3
    m, k = a.shape[-2:]
    n = b.shape[-2] if nt else b.shape[-1]
    tm, tn = min(tm, m), min(tn, n)

    def body(a_ref, b_ref, o_ref):
        f = _dot_nt if nt else _dot
        o_ref[...] = f(a_ref[...], b_ref[...]).astype(o_ref.dtype)

    rows_inner = (n // tn) * m < (m // tm) * n

    def ij(u, v):
        return (v, u) if rows_inner else (u, v)

    if batched:
        nb = a.shape[0]
        grid = (nb,) + ij(m // tm, n // tn)
        a_spec = pl.BlockSpec((None, tm, k), lambda g, u, v: (g, ij(u, v)[0], 0))
        b_spec = (pl.BlockSpec((None, tn, k), lambda g, u, v: (g, ij(u, v)[1], 0)) if nt
                  else pl.BlockSpec((None, k, tn), lambda g, u, v: (g, 0, ij(u, v)[1])))
        o_spec = pl.BlockSpec((None, tm, tn), lambda g, u, v: (g,) + ij(u, v))
        out_shape = _sds((nb, m, n), out_dtype)
        sem = ("parallel", "parallel", "parallel")
    else:
        grid = ij(m // tm, n // tn)
        a_spec = pl.BlockSpec((tm, k), lambda u, v: (ij(u, v)[0], 0))
        b_spec = (pl.BlockSpec((tn, k), lambda u, v: (ij(u, v)[1], 0)) if nt
                  else pl.BlockSpec((k, tn), lambda u, v: (0, ij(u, v)[1])))
        o_spec = pl.BlockSpec((tm, tn), lambda u, v: ij(u, v))
        out_shape = _sds((m, n), out_dtype)
        sem = ("parallel", "parallel")
    return pl.pallas_call(body, name=name, grid=grid, in_specs=[a_spec, b_spec], out_specs=o_spec,
                          out_shape=out_shape, compiler_params=_cp(sem))(a, b)


def _mm_tn(name, a, b, tm, tn):
    batched = a.ndim == 3
    k, m = a.shape[-2:]
    n = b.shape[-1]
    tm, tn = min(tm, m), min(tn, n)

    def body(a_ref, b_ref, o_ref):
        o_ref[...] = lax.dot_general(_mx(a_ref[...]), _mx(b_ref[...]), (((0,), (0,)), ((), ())),
                                     preferred_element_type=F32)

    if batched:
        nb = a.shape[0]
        grid = (nb, m // tm, n // tn)
        a_spec = pl.BlockSpec((None, k, tm), lambda g, i, j: (g, 0, i))
        b_spec = pl.BlockSpec((None, k, tn), lambda g, i, j: (g, 0, j))
        o_spec = pl.BlockSpec((None, tm, tn), lambda g, i, j: (g, i, j))
        out_shape = _sds((nb, m, n), F32)
    else:
        grid = (m // tm, n // tn)
        a_spec = pl.BlockSpec((k, tm), lambda i, j: (0, i))
        b_spec = pl.BlockSpec((k, tn), lambda i, j: (0, j))
        o_spec = pl.BlockSpec((tm, tn), lambda i, j: (i, j))
        out_shape = _sds((m, n), F32)
    return pl.pallas_call(body, name=name, grid=grid, in_specs=[a_spec, b_spec], out_specs=o_spec,
                          out_shape=out_shape, compiler_params=_cp(("parallel",) * len(grid)))(a, b)


def _ln_rows(y, g, b):
    mu = jnp.mean(y, axis=-1, keepdims=True)
    yc = y - mu
    var = jnp.mean(yc * yc, axis=-1, keepdims=True)
    rs = lax.rsqrt(var + LN_EPS)
    xh = yc * rs
    return xh * g + b, xh, rs


def _ln_in(x, g, b, tm=256):
    t, d = x.shape

    def body(x_ref, g_ref, b_ref, o_ref, ob_ref, xh_ref, rs_ref):
        o, xh, rs = _ln_rows(x_ref[...], g_ref[...], b_ref[...])
        o_ref[...] = o
        ob_ref[...] = o.astype(BF16)
        xh_ref[...] = xh
        rs_ref[...] = rs

    row = pl.BlockSpec((tm, d), lambda i: (i, 0))
    vec = pl.BlockSpec((1, d), lambda i: (0, 0))
    return pl.pallas_call(
        body, name="ln_in", grid=(t // tm,), in_specs=[row, vec, vec],
        out_specs=[row, row, row, pl.BlockSpec((tm, 1), lambda i: (i, 0))],
        out_shape=[_sds((t, d), F32), _sds((t, d), BF16), _sds((t, d), F32), _sds((t, 1), F32)],
        compiler_params=_cp(("parallel",)))(x, g, b)


def _mm_res_ln(name, a, w, res, g, b, tm, relu2):
    t, k = a.shape
    d = w.shape[1]

    def body(a_ref, w_ref, r_ref, g_ref, b_ref, o_ref, ob_ref, xh_ref, rs_ref, *act_ref):
        av = a_ref[...]
        if relu2:
            av = jnp.square(jnp.maximum(av, 0.0))
            act_ref[0][...] = av.astype(BF16)
        y = ALPHA * r_ref[...] + _dot(av, w_ref[...])
        o, xh, rs = _ln_rows(y, g_ref[...], b_ref[...])
        o_ref[...] = o
        ob_ref[...] = o.astype(BF16)
        xh_ref[...] = xh
        rs_ref[...] = rs

    row = pl.BlockSpec((tm, d), lambda i: (i, 0))
    vec = pl.BlockSpec((1, d), lambda i: (0, 0))
    arow = pl.BlockSpec((tm, k), lambda i: (i, 0))
    out_specs = [row, row, row, pl.BlockSpec((tm, 1), lambda i: (i, 0))]
    out_shape = [_sds((t, d), F32), _sds((t, d), BF16), _sds((t, d), F32), _sds((t, 1), F32)]
    if relu2:
        out_specs.append(arow)
        out_shape.append(_sds((t, k), BF16))
    return pl.pallas_call(
        body, name=name, grid=(t // tm,),
        in_specs=[arow, pl.BlockSpec((k, d), lambda i: (0, 0)), row, vec, vec],
        out_specs=out_specs, out_shape=out_shape, compiler_params=_cp(("parallel",)))(a, w, res, g, b)


def _merge_fwd(bo, wb, p, gate_off, tm=512, tn=512):
    _, t, d = bo.shape
    gb = gate_off // tn

    def body(bo_ref, wb_ref, g0, g1, g2, proj_ref, m_ref):
        acc = None
        for n, g_ref in enumerate((g0, g1, g2)):
            pr = _dot(bo_ref[n], wb_ref[n])
            proj_ref[n] = pr
            term = _sigmoid(g_ref[...]) * pr
            acc = term if acc is None else acc + term
        m_ref[...] = acc.astype(BF16)

    gspecs = [pl.BlockSpec((tm, tn), functools.partial(lambda i, j, n: (i, gb + n * (d // tn) + j), n=n))
              for n in range(3)]
    return pl.pallas_call(
        body, name="merge_fwd", grid=(t // tm, d // tn),
        in_specs=[pl.BlockSpec((3, tm, d), lambda i, j: (0, i, 0)),
                  pl.BlockSpec((3, d, tn), lambda i, j: (0, 0, j))] + gspecs,
        out_specs=[pl.BlockSpec((3, tm, tn), lambda i, j: (0, i, j)), pl.BlockSpec((tm, tn), lambda i, j: (i, j))],
        out_shape=[_sds((3, t, d), F32), _sds((t, d), BF16)],
        compiler_params=_cp(("parallel", "parallel")))(bo, wb, p, p, p)


def _merge_bwd(dz, wout, proj, p, gate_off, tm=512, tn=512):
    t, d = dz.shape
    gb = gate_off // tn

    def body(dz_ref, w_ref, proj_ref, g0, g1, g2, dproj_ref, dgl_ref):
        dm = _dot_nt(dz_ref[...], w_ref[...])
        for n, g_ref in enumerate((g0, g1, g2)):
            s = _sigmoid(g_ref[...])
            dproj_ref[n] = (dm * s).astype(BF16)
            dgl_ref[n] = (dm * proj_ref[n] * (s * (1.0 - s))).astype(BF16)

    gspecs = [pl.BlockSpec((tm, tn), functools.partial(lambda i, j, n: (i, gb + n * (d // tn) + j), n=n))
              for n in range(3)]
    dproj, dgl = pl.pallas_call(
        body, name="merge_bwd", grid=(t // tm, d // tn),
        in_specs=[pl.BlockSpec((tm, d), lambda i, j: (i, 0)), pl.BlockSpec((tn, d), lambda i, j: (j, 0)),
                  pl.BlockSpec((3, tm, tn), lambda i, j: (0, i, j))] + gspecs,
        out_specs=[pl.BlockSpec((3, tm, tn), lambda i, j: (0, i, j)),
                   pl.BlockSpec((3, tm, tn), lambda i, j: (0, i, j))],
        out_shape=[_sds((3, t, d), BF16), _sds((3, t, d), BF16)],
        compiler_params=_cp(("parallel", "parallel")))(dz, wout, proj, p, p, p)
    return dproj, dgl


def _mm_nt_relu2_bwd(dz, wdown, u, tm=512, tn=1024):
    t, d = dz.shape
    f = wdown.shape[0]

    def body(dz_ref, w_ref, u_ref, du_ref):
        da = _dot_nt(dz_ref[...], w_ref[...])
        du_ref[...] = (da * (2.0 * jnp.maximum(u_ref[...], 0.0))).astype(BF16)

    return pl.pallas_call(
        body, name="mlp_down_bwd", grid=(t // tm, f // tn),
        in_specs=[pl.BlockSpec((tm, d), lambda i, j: (i, 0)), pl.BlockSpec((tn, d), lambda i, j: (j, 0)),
                  pl.BlockSpec((tm, tn), lambda i, j: (i, j))],
        out_specs=pl.BlockSpec((tm, tn), lambda i, j: (i, j)), out_shape=_sds((t, f), BF16),
        compiler_params=_cp(("parallel", "parallel")))(dz, wdown, u)


def _ln_bwd_rows(dx, xh, rs, g):
    dxh = dx * g
    m1 = jnp.mean(dxh, axis=-1, keepdims=True)
    m2 = jnp.mean(dxh * xh, axis=-1, keepdims=True)
    return rs * (dxh - m1 - xh * m2)


def _mm_nt_res_lnbwd(name, a, w, dres, xh, rs, g, tm, tk):
    t, k = a.shape
    d = w.shape[0]
    nk = k // tk

    def body(a_ref, w_ref, dr_ref, xh_ref, rs_ref, g_ref, dz_ref, dzb_ref, dg_ref, db_ref, acc_ref):
        i, kk = pl.program_id(0), pl.program_id(1)

        @pl.when(kk == 0)
        def _():
            acc_ref[...] = ALPHA * dr_ref[...]

        acc_ref[...] += _dot_nt(a_ref[...], w_ref[...])

        @pl.when(jnp.logical_and(i == 0, kk == 0))
        def _():
            dg_ref[...] = jnp.zeros_like(dg_ref)
            db_ref[...] = jnp.zeros_like(db_ref)

        @pl.when(kk == nk - 1)
        def _():
            dx = acc_ref[...]
            xhv = xh_ref[...]
            dz = _ln_bwd_rows(dx, xhv, rs_ref[...], g_ref[...])
            dz_ref[...] = dz
            dzb_ref[...] = dz.astype(BF16)
            dg_ref[...] += jnp.sum(dx * xhv, axis=0, keepdims=True)
            db_ref[...] += jnp.sum(dx, axis=0, keepdims=True)

    row = pl.BlockSpec((tm, d), lambda i, kk: (i, 0))
    vec = pl.BlockSpec((1, d), lambda i, kk: (0, 0))
    return pl.pallas_call(
        body, name=name, grid=(t // tm, nk),
        in_specs=[pl.BlockSpec((tm, tk), lambda i, kk: (i, kk)), pl.BlockSpec((d, tk), lambda i, kk: (0, kk)),
                  row, row, pl.BlockSpec((tm, 1), lambda i, kk: (i, 0)), vec],
        out_specs=[row, row, vec, vec],
        out_shape=[_sds((t, d), F32), _sds((t, d), BF16), _sds((1, d), F32), _sds((1, d), F32)],
        scratch_shapes=[pltpu.VMEM((tm, d), F32)],
        compiler_params=_cp(("arbitrary", "arbitrary")))(a, w, dres, xh, rs, g)


def _loss_ln_bwd(x2, target, xh, rs, g, tm=256):
    t, d = x2.shape

    def body(x_ref, t_ref, xh_ref, rs_ref, g_ref, loss_ref, dz_ref, dzb_ref, dg_ref, db_ref):
        @pl.when(pl.program_id(0) == 0)
        def _():
            loss_ref[...] = jnp.zeros_like(loss_ref)
            dg_ref[...] = jnp.zeros_like(dg_ref)
            db_ref[...] = jnp.zeros_like(db_ref)

        err = x_ref[...] - t_ref[...]
        per_row = jnp.mean(err * err, axis=-1, keepdims=True)
        loss_ref[...] += 0.5 * jnp.sum(per_row, axis=0, keepdims=True)
        dx = err * (1.0 / d)
        xhv = xh_ref[...]
        dz = _ln_bwd_rows(dx, xhv, rs_ref[...], g_ref[...])
        dz_ref[...] = dz
        dzb_ref[...] = dz.astype(BF16)
        dg_ref[...] += jnp.sum(dx * xhv, axis=0, keepdims=True)
        db_ref[...] += jnp.sum(dx, axis=0, keepdims=True)

    row = pl.BlockSpec((tm, d), lambda i: (i, 0))
    vec = pl.BlockSpec((1, d), lambda i: (0, 0))
    return pl.pallas_call(
        body, name="loss_ln_bwd", grid=(t // tm,),
        in_specs=[row, row, row, pl.BlockSpec((tm, 1), lambda i: (i, 0)), vec],
        out_specs=[pl.BlockSpec((1, LANE), lambda i: (0, 0)), row, row, vec, vec],
        out_shape=[_sds((1, LANE), F32), _sds((t, d), F32), _sds((t, d), BF16), _sds((1, d), F32),
                   _sds((1, d), F32)],
        compiler_params=_cp(("arbitrary",)))(x2, target, xh, rs, g)


HPA = 2


def _attn_scores(q_ref, k_refs, bias_ref, i, dh, hh):
    cols = pl.ds(hh * dh, dh)
    q = q_ref[:, cols] * (dh ** -0.5)
    k = jnp.concatenate([r[:, cols] for r in k_refs], axis=0)
    s = _dot_nt(q, k) + bias_ref[hh]
    col = lax.broadcasted_iota(jnp.int32, s.shape, 1)
    s = jnp.where(col >= (2 - i) * QB, s, NEG_INF)
    m = jnp.max(s, axis=-1, keepdims=True)
    e = jnp.exp(s - m)
    return q, k, e / jnp.sum(e, axis=-1, keepdims=True)


def _attn_specs(dh, off):
    w = HPA * dh
    qcol, kcol, vcol = off["aq"] // w, off["ak"] // w, off["av"] // w
    q_spec = pl.BlockSpec((QB, w), lambda g, i: (i, qcol + g))
    k_specs = [pl.BlockSpec((QB, w), functools.partial(lambda g, i, j: (jnp.maximum(i - 2 + j, 0), kcol + g), j=j))
               for j in range(3)]
    v_specs = [pl.BlockSpec((QB, w), functools.partial(lambda g, i, j: (jnp.maximum(i - 2 + j, 0), vcol + g), j=j))
               for j in range(3)]
    bias_spec = pl.BlockSpec((HPA, QB, KW), lambda g, i: (g, 0, 0))
    return q_spec, k_specs, v_specs, bias_spec


def _attn_fwd(p, bias, d, off):
    t = p.shape[0]
    dh = d // ATTN_HEADS

    def body(q_ref, k0, k1, k2, v0, v1, v2, bias_ref, o_ref):
        for hh in range(HPA):
            cols = pl.ds(hh * dh, dh)
            _, _, pr = _attn_scores(q_ref, (k0, k1, k2), bias_ref, pl.program_id(1), dh, hh)
            v = jnp.concatenate([v0[:, cols], v1[:, cols], v2[:, cols]], axis=0)
            o_ref[:, cols] = _dot(pr, v).astype(o_ref.dtype)

    q_spec, k_specs, v_specs, bias_spec = _attn_specs(dh, off)
    return pl.pallas_call(
        body, name="attn_fwd", grid=(ATTN_HEADS // HPA, t // QB),
        in_specs=[q_spec] + k_specs + v_specs + [bias_spec],
        out_specs=pl.BlockSpec((QB, HPA * dh), lambda g, i: (i, g)), out_shape=_sds((t, d), BF16),
        compiler_params=_cp(("parallel", "parallel")))(p, p, p, p, p, p, p, bias)


def _attn_bwd(p, bias, do, d, off):
    t = p.shape[0]
    dh = d // ATTN_HEADS
    tp = t + 2 * QB

    def body(q_ref, k0, k1, k2, v0, v1, v2, bias_ref, do_ref, dq_ref, dk_ref, dv_ref, dbias_ref):
        i = pl.program_id(1)

        @pl.when(i == 0)
        def _():
            dk_ref[...] = jnp.zeros_like(dk_ref)
            dv_ref[...] = jnp.zeros_like(dv_ref)
            dbias_ref[...] = jnp.zeros_like(dbias_ref)

        rows = pl.ds(pl.multiple_of(i * QB, QB), KW)
        for hh in range(HPA):
            cols = pl.ds(hh * dh, dh)
            q, k, pr = _attn_scores(q_ref, (k0, k1, k2), bias_ref, i, dh, hh)
            v = jnp.concatenate([v0[:, cols], v1[:, cols], v2[:, cols]], axis=0)
            dov = do_ref[:, cols]
            dp = _dot_nt(dov, v)
            delta = jnp.sum(pr * dp, axis=-1, keepdims=True)
            ds = pr * (dp - delta)
            dbias_ref[hh] += ds
            dq_ref[:, cols] = (_dot(ds, k) * (dh ** -0.5)).astype(dq_ref.dtype)
            dk_ref[rows, cols] += _dot_tn(ds, q)
            dv_ref[rows, cols] += _dot_tn(pr, dov)

    q_spec, k_specs, v_specs, bias_spec = _attn_specs(dh, off)
    row_spec = pl.BlockSpec((QB, HPA * dh), lambda g, i: (i, g))
    acc_spec = pl.BlockSpec((tp, HPA * dh), lambda g, i: (0, g))
    return pl.pallas_call(
        body, name="attn_bwd", grid=(ATTN_HEADS // HPA, t // QB),
        in_specs=[q_spec] + k_specs + v_specs + [bias_spec, row_spec],
        out_specs=[row_spec, acc_spec, acc_spec, bias_spec],
        out_shape=[_sds((t, d), BF16), _sds((tp, d), F32), _sds((tp, d), F32),
                   _sds((ATTN_HEADS, QB, KW), F32)],
        compiler_params=_cp(("parallel", "arbitrary")))(p, p, p, p, p, p, p, bias, do)


def _onehot_mm(name, a, b):
    def body(a_ref, b_ref, o_ref):
        o_ref[...] = _dot_hi(a_ref[...], b_ref[...])

    return pl.pallas_call(body, name=name, out_shape=_sds((a.shape[0], b.shape[1]), F32),
                          compiler_params=pltpu.CompilerParams(vmem_limit_bytes=VMEM_LIMIT))(a, b)


def _diag_index():
    ii, jj = np.arange(CHUNK)[:, None], np.arange(CHUNK)[None, :]
    return (ii - jj + CHUNK - 1).reshape(-1)


def _bias_expand(rel_bias):
    h = rel_bias.shape[0]
    nq, nk, shift = QB // CHUNK, KW // CHUNK, (2 * QB) // CHUNK
    nbin, ndc = 3 * LANE, 4
    rb = jnp.pad(rel_bias, ((0, 0), (0, nbin - rel_bias.shape[1])))
    win = np.clip(CHUNK * np.arange(ndc)[:, None] + np.arange(LANE)[None, :] - (CHUNK - 1), -REL_CLIP, REL_CLIP)
    sel = (jnp.arange(nbin)[:, None] == jnp.asarray((win + REL_CLIP).reshape(1, -1))).astype(F32)
    windows = _onehot_mm("bias_windows", rb, sel)
    diag_t = (jnp.arange(LANE)[:, None] == jnp.asarray(_diag_index().reshape(1, -1))).astype(F32)
    blocks = _onehot_mm("bias_blocks", windows.reshape(h * ndc, LANE), diag_t).reshape(h, ndc, CHUNK, CHUNK)
    off_band = jnp.full((h, CHUNK, CHUNK), NEG_INF, F32)
    rows = []
    for ic in range(nq):
        dcs = [ic - jc + shift for jc in range(nk)]
        rows.append(jnp.concatenate([blocks[:, min(dc, ndc - 1)] if 0 <= dc <= ATTN_LEFT else off_band
                                     for dc in dcs], axis=2))
    return jnp.concatenate(rows, axis=1)


def _bias_reduce(dbias):
    h = dbias.shape[0]
    nq, nk = QB // CHUNK, KW // CHUNK
    nbin = 3 * LANE
    blocks = dbias.reshape(h, nq, CHUNK, nk, CHUNK).transpose(0, 1, 3, 2, 4).reshape(h * nq * nk, CHUNK * CHUNK)
    diag = (jnp.asarray(_diag_index().reshape(-1, 1)) == jnp.arange(LANE)[None, :]).astype(F32)
    ic = np.arange(nq)[:, None, None]
    jc = np.arange(nk)[None, :, None]
    dl = np.arange(LANE)[None, None, :] - (CHUNK - 1)
    rel = np.clip(CHUNK * (ic - jc + (2 * QB) // CHUNK) + dl, -REL_CLIP, REL_CLIP) + REL_CLIP
    bins = (jnp.asarray(rel.reshape(-1, 1)) == jnp.arange(nbin)[None, :]).astype(F32)

    diags = _onehot_mm("bias_diag_sums", blocks, diag)
    out = _onehot_mm("bias_bin_sums", diags.reshape(h, nq * nk * LANE), bins)
    return out[:, :2 * REL_CLIP + 1]


def _tri(lower):
    r = lax.broadcasted_iota(jnp.int32, (CHUNK, CHUNK), 0)
    c = lax.broadcasted_iota(jnp.int32, (CHUNK, CHUNK), 1)
    return (r >= c) if lower else (r <= c)


def _lin_prep(gla, q, k, aux):
    dk = q.shape[-1]
    if gla:
        glr, wlr, blr = aux
        q = q * (dk ** -0.5)
        pre = _dot(glr, wlr) + blr
        log_a = (jnp.minimum(pre, 0.0) - jnp.log(1.0 + jnp.exp(-jnp.abs(pre)))) / GATE_NORM
        b = _dot_hi(_tri(True).astype(F32), log_a)
        return q, k, b, pre
    cs, sn, lg = aux
    half = dk // 2
    q = q * cs + pltpu.roll(q, half, 1) * sn
    k = (k * cs + pltpu.roll(k, half, 1) * sn) * (dk ** -0.5)
    pos = lax.broadcasted_iota(jnp.int32, (CHUNK, dk), 0).astype(F32) + 1.0
    return q, k, pos * lg, None


def _lin_chunk(q, k, v, b, st):
    eb, enb = jnp.exp(b), jnp.exp(-b)
    last = b[CHUNK - 1:CHUNK, :]
    qf, kf, qb, kb = q * eb, k * enb, q * enb, k * eb
    kl = k * jnp.exp(last - b)
    s = jnp.where(_tri(True), _dot_nt(qf, kf), _dot_nt(qb, kb))
    o = _dot(s, v) + _dot_nt(qf, st)
    st_new = st * jnp.exp(last) + _dot_tn(v, kl)
    return o, st_new, (eb, enb, last, qf, kf, qb, kb, kl, s)


def _lin_norm_gate(gla, o, gate, gn):
    sg = _sigmoid(gate)
    silu = gate * sg
    if gla:
        r = lax.rsqrt(jnp.mean(o * o, axis=-1, keepdims=True) + LN_EPS)
        hn = o * r
        return silu * (hn * gn), (sg, silu, r, hn)
    mu = jnp.mean(o, axis=-1, keepdims=True)
    oc = o - mu
    r = lax.rsqrt(jnp.mean(oc * oc, axis=-1, keepdims=True) + LN_EPS)
    hn = oc * r
    return silu * hn, (sg, silu, r, hn)


HPS = 2


def _lin_specs(gla, dk, dv, off, rev, nb):
    pre = "g" if gla else "r"
    wk, wv = HPS * dk, HPS * dv
    qc, kc, vc, gc = (off[pre + "q"] // wk, off[pre + "k"] // wk, off[pre + "v"] // wv, off[pre + "g"] // wv)

    def blk(i):
        return nb - 1 - i if rev else i

    specs = [pl.BlockSpec((LB, wk), lambda g, i: (blk(i), qc + g)),
             pl.BlockSpec((LB, wk), lambda g, i: (blk(i), kc + g)),
             pl.BlockSpec((LB, wv), lambda g, i: (blk(i), vc + g)),
             pl.BlockSpec((LB, wv), lambda g, i: (blk(i), gc + g))]
    if gla:
        specs += [pl.BlockSpec((LB, LANE), lambda g, i: (blk(i), off["glr"] // LANE)),
                  pl.BlockSpec((LANE, wk), lambda g, i: (0, g)),
                  pl.BlockSpec((1, wk), lambda g, i: (0, g)),
                  pl.BlockSpec((1, dv), lambda g, i: (0, 0))]
    else:
        specs += [pl.BlockSpec((LB, dk), lambda g, i: (blk(i), 0)),
                  pl.BlockSpec((LB, dk), lambda g, i: (blk(i), 0)),
                  pl.BlockSpec((HPS, 1, dk), lambda g, i: (g, 0, 0))]
    return specs, blk


def _lin_aux(gla, refs, rows, hh, dk):
    if gla:
        glr_ref, wlr_ref, blr_ref, gn_ref = refs
        kcols = pl.ds(hh * dk, dk)
        return (glr_ref[rows, :], wlr_ref[:, kcols], blr_ref[:, kcols]), gn_ref[...]
    cs_ref, sn_ref, lg_ref = refs
    return (cs_ref[rows, :], sn_ref[rows, :], lg_ref[hh]), None


def _lin_fwd(gla, p, aux_arrays, d, off):
    t = p.shape[0]
    dk, dv = d // (2 * LIN_HEADS), d // LIN_HEADS
    nb, cb = t // LB, LB // CHUNK
    naux = len(aux_arrays)

    def body(*refs):
        q_ref, k_ref, v_ref, g_ref = refs[:4]
        aux_refs = refs[4:4 + naux]
        o_ref, bo_ref, st_out_ref, st_ref = refs[4 + naux:]

        @pl.when(pl.program_id(1) == 0)
        def _():
            st_ref[...] = jnp.zeros_like(st_ref)

        st = [st_ref[hh] for hh in range(HPS)]
        for c in range(cb):
            rows = pl.ds(c * CHUNK, CHUNK)
            for hh in range(HPS):
                kcols, vcols = pl.ds(hh * dk, dk), pl.ds(hh * dv, dv)
                aux, gn = _lin_aux(gla, aux_refs, rows, hh, dk)
                q, k, b, _ = _lin_prep(gla, q_ref[rows, kcols], k_ref[rows, kcols], aux)
                st_out_ref[hh, c] = st[hh]
                o, st[hh], _ = _lin_chunk(q, k, v_ref[rows, vcols], b, st[hh])
                o_ref[rows, vcols] = o
                out, _ = _lin_norm_gate(gla, o, g_ref[rows, vcols], gn)
                bo_ref[rows, vcols] = out.astype(BF16)
        for hh in range(HPS):
            st_ref[hh] = st[hh]

    specs, _ = _lin_specs(gla, dk, dv, off, False, nb)
    orow = pl.BlockSpec((LB, HPS * dv), lambda g, i: (i, g))
    return pl.pallas_call(
        body, name="gla_fwd" if gla else "ret_fwd", grid=(LIN_HEADS // HPS, nb), in_specs=specs,
        out_specs=[orow, orow, pl.BlockSpec((HPS, cb, dv, dk), lambda g, i: (g, i, 0, 0))],
        out_shape=[_sds((t, d), F32), _sds((t, d), BF16), _sds((LIN_HEADS, t // CHUNK, dv, dk), F32)],
        scratch_shapes=[pltpu.VMEM((HPS, dv, dk), F32)],
        compiler_params=_cp(("parallel", "arbitrary")))(p, p, p, p, *aux_arrays)


def _lin_bwd(gla, p, aux_arrays, o, states, dbo, d, off):
    t = p.shape[0]
    dk, dv = d // (2 * LIN_HEADS), d // LIN_HEADS
    nb, cb = t // LB, LB // CHUNK
    naux = len(aux_arrays)

    def body(*refs):
        q_ref, k_ref, v_ref, g_ref = refs[:4]
        aux_refs = refs[4:4 + naux]
        o_ref, st_in_ref, dbo_ref = refs[4 + naux:7 + naux]
        outs = refs[7 + naux:]
        dq_ref, dk_ref, dv_ref, dg_ref = outs[:4]
        dst_ref = outs[-1]
        first = pl.program_id(1) == 0

        @pl.when(first)
        def _():
            dst_ref[...] = jnp.zeros_like(dst_ref)

        if gla:
            dpre_ref, dblr_ref, dgn_ref = outs[4:7]

            @pl.when(first)
            def _():
                dblr_ref[...] = jnp.zeros_like(dblr_ref)
                dgn_ref[...] = jnp.zeros_like(dgn_ref)

        dst = [dst_ref[hh] for hh in range(HPS)]
        for c in reversed(range(cb)):
            rows = pl.ds(c * CHUNK, CHUNK)
            for hh in range(HPS):
                kcols, vcols = pl.ds(hh * dk, dk), pl.ds(hh * dv, dv)
                aux, gn = _lin_aux(gla, aux_refs, rows, hh, dk)
                q, k, b, pre = _lin_prep(gla, q_ref[rows, kcols], k_ref[rows, kcols], aux)
                v = v_ref[rows, vcols]
                st = st_in_ref[hh, c]
                _, _, (eb, enb, last, qf, kf, qb, kb, kl, s) = _lin_chunk(q, k, v, b, st)
                gate = g_ref[rows, vcols]
                dout = dbo_ref[rows, vcols]
                _, (sg, silu, r, hn) = _lin_norm_gate(gla, o_ref[rows, vcols], gate, gn)
                dsilu = sg * (1.0 + gate * (1.0 - sg))
                if gla:
                    y = hn * gn
                    dy = dout * silu
                    dg_ref[rows, vcols] = (dout * y * dsilu).astype(BF16)
                    dgn_ref[hh] += jnp.sum(dy * hn, axis=0, keepdims=True)
                    dhn = dy * gn
                    do = r * (dhn - hn * jnp.mean(dhn * hn, axis=-1, keepdims=True))
                else:
                    dhn = dout * silu
                    dg_ref[rows, vcols] = (dout * hn * dsilu).astype(BF16)
                    do = r * (dhn - jnp.mean(dhn, axis=-1, keepdims=True)
                              - hn * jnp.mean(dhn * hn, axis=-1, keepdims=True))
                dstn = dst[hh]
                dec = jnp.exp(last)
                ds = _dot_nt(do, v)
                low = _tri(True)
                dsf = jnp.where(low, ds, 0.0)
                dsb = jnp.where(low, 0.0, ds)
                dvv = _dot_tn(s, do) + _dot_nt(kl, dstn)
                dqf = _dot(dsf, kf) + _dot(do, st)
                dkf = _dot_tn(dsf, qf)
                dqb = _dot(dsb, kb)
                dkb = _dot_tn(dsb, qb)
                dkl = _dot(v, dstn)
                dst[hh] = dstn * dec + _dot_tn(do, qf)
                dq = dqf * eb + dqb * enb
                dkk = dkf * enb + dkb * eb + dkl * jnp.exp(last - b)
                dv_ref[rows, vcols] = dvv.astype(BF16)
                if gla:
                    ddec = jnp.sum(dstn * st, axis=0, keepdims=True)
                    db = dqf * qf - dkf * kf - dqb * qb + dkb * kb - dkl * kl
                    dlast = jnp.sum(dkl * kl, axis=0, keepdims=True) + ddec * dec
                    rowi = lax.broadcasted_iota(jnp.int32, db.shape, 0)
                    db = db + jnp.where(rowi == CHUNK - 1, dlast, 0.0)
                    dlog_a = _dot_hi(_tri(False).astype(F32), db)
                    dpre = dlog_a * (1.0 / GATE_NORM) * (1.0 - _sigmoid(pre))
                    dpre_ref[rows, kcols] = dpre
                    dblr_ref[hh] += jnp.sum(dpre, axis=0, keepdims=True)
                    dq_ref[rows, kcols] = (dq * (dk ** -0.5)).astype(BF16)
                    dk_ref[rows, kcols] = dkk.astype(BF16)
                else:
                    cs, sn, _ = aux
                    half = dk // 2
                    dkk = dkk * (dk ** -0.5)
                    dq_ref[rows, kcols] = (dq * cs + pltpu.roll(dq * sn, half, 1)).astype(BF16)
                    dk_ref[rows, kcols] = (dkk * cs + pltpu.roll(dkk * sn, half, 1)).astype(BF16)
        for hh in range(HPS):
            dst_ref[hh] = dst[hh]

    specs, blk = _lin_specs(gla, dk, dv, off, True, nb)
    vrow = pl.BlockSpec((LB, HPS * dv), lambda g, i: (blk(i), g))
    krow = pl.BlockSpec((LB, HPS * dk), lambda g, i: (blk(i), g))
    specs += [vrow, pl.BlockSpec((HPS, cb, dv, dk), lambda g, i: (g, blk(i), 0, 0)), vrow]
    out_specs = [krow, krow, vrow, vrow]
    out_shape = [_sds((t, d // 2), BF16), _sds((t, d // 2), BF16), _sds((t, d), BF16), _sds((t, d), BF16)]
    if gla:
        out_specs += [krow, pl.BlockSpec((HPS, 1, dk), lambda g, i: (g, 0, 0)),
                      pl.BlockSpec((HPS, 1, dv), lambda g, i: (g, 0, 0))]
        out_shape += [_sds((t, d // 2), F32), _sds((LIN_HEADS, 1, dk), F32), _sds((LIN_HEADS, 1, dv), F32)]
    out_specs.append(pl.BlockSpec((HPS, dv, dk), lambda g, i: (g, 0, 0)))
    out_shape.append(_sds((LIN_HEADS, dv, dk), F32))
    res = pl.pallas_call(
        body, name="gla_bwd" if gla else "ret_bwd", grid=(LIN_HEADS // HPS, nb), in_specs=specs,
        out_specs=out_specs, out_shape=out_shape,
        compiler_params=_cp(("parallel", "arbitrary")))(p, p, p, p, *aux_arrays, o, states, dbo)
    return res[:-1]


def _row_tile(rows, cols):
    cap = max(8, (2 << 20) // (4 * cols))
    t = rows
    while t > cap and t % 2 == 0:
        t //= 2
    return t


def _add_half(name, g, t, sel):
    nchip, hr, cols = t.shape
    tr = _row_tile(hr, cols)
    nb = hr // tr

    def body(sel_ref, g_ref, t_ref, o_ref):
        o_ref[...] = g_ref[...] + t_ref[...]

    half = pl.BlockSpec((None, tr, cols), lambda p, i, s: (p, i, 0))
    gs = pltpu.PrefetchScalarGridSpec(
        num_scalar_prefetch=1, grid=(nchip, nb),
        in_specs=[pl.BlockSpec((None, tr, cols), lambda p, i, s: (p, s[0] * nb + i, 0)), half], out_specs=half)
    return pl.pallas_call(body, name=name, grid_spec=gs, out_shape=_sds(t.shape, F32),
                          compiler_params=_cp(("parallel", "parallel")))(sel, g, t)


def _sum_shards(name, h, rcv, sel):
    _, rows, cols = h.shape
    tr = _row_tile(rows, cols)

    def body(sel_ref, h_ref, r0, r1, r2, o_ref):
        o_ref[...] = ((h_ref[...] + r0[...]) + r1[...]) + r2[...]

    rspecs = [pl.BlockSpec((None, tr, cols), functools.partial(lambda i, s, j: (j, i, 0), j=j)) for j in range(3)]
    gs = pltpu.PrefetchScalarGridSpec(
        num_scalar_prefetch=1, grid=(rows // tr,),
        in_specs=[pl.BlockSpec((None, tr, cols), lambda i, s: (s[0], i, 0))] + rspecs,
        out_specs=pl.BlockSpec((tr, cols), lambda i, s: (i, 0)))
    return pl.pallas_call(body, name=name, grid_spec=gs, out_shape=_sds((rows, cols), F32),
                          compiler_params=_cp(("parallel",)))(sel, h, rcv, rcv, rcv)


def _adamw_math(w, g, m, v):
    c1 = 1.0 - ADAM_B1 ** ADAM_STEP
    c2 = 1.0 - ADAM_B2 ** ADAM_STEP
    nm = ADAM_B1 * m + (1.0 - ADAM_B1) * g
    nv = ADAM_B2 * v + (1.0 - ADAM_B2) * jnp.square(g)
    return -ADAM_LR * ((nm / c1) / (jnp.sqrt(nv / c2) + ADAM_EPS) + ADAM_WD * w), nm, nv


def _adamw(name, w, g, m, v):
    rows, cols = w.shape
    tr = _row_tile(rows, cols)

    def body(w_ref, g_ref, m_ref, v_ref, d_ref, nm_ref, nv_ref):
        d_ref[...], nm_ref[...], nv_ref[...] = _adamw_math(w_ref[...], g_ref[...], m_ref[...], v_ref[...])

    spec = pl.BlockSpec((tr, cols), lambda i: (i, 0))
    return pl.pallas_call(body, name=name, grid=(rows // tr,), in_specs=[spec] * 4, out_specs=[spec] * 3,
                          out_shape=[_sds((rows, cols), F32)] * 3, compiler_params=_cp(("parallel",)))(w, g, m, v)


def _adamw_layer(name, w, g_own, g_sib, sel, m, v, layer, prev):
    depth, rows, cols = w.shape
    tr = _row_tile(rows // 2, cols)
    nbh = rows // 2 // tr
    nprev = 0 if prev is None else 4

    def body(sel_ref, w_ref, own_ref, sib_ref, m_ref, v_ref, *rest):
        go_ref, d_ref, nm_ref, nv_ref = rest[nprev:]
        gv = jnp.where(pl.program_id(0) // nbh == sel_ref[0], own_ref[...], sib_ref[...])
        go_ref[...] = gv
        d_ref[...], nm_ref[...], nv_ref[...] = _adamw_math(w_ref[...], gv, m_ref[...], v_ref[...])

    lay = pl.BlockSpec((None, tr, cols), lambda i, s: (layer, i, 0))
    hlf = pl.BlockSpec((tr, cols), lambda i, s: (i % nbh, 0))
    gs = pltpu.PrefetchScalarGridSpec(
        num_scalar_prefetch=1, grid=(2 * nbh,), in_specs=[lay, hlf, hlf, lay, lay] + [ANY] * nprev,
        out_specs=[lay] * 4)
    args = (sel, w, g_own, g_sib, m, v) + (() if prev is None else tuple(prev))
    return pl.pallas_call(
        body, name=name, grid_spec=gs, out_shape=[_sds((depth, rows, cols), F32)] * 4,
        input_output_aliases={6 + k: k for k in range(nprev)},
        compiler_params=_cp(("parallel",)))(*args)


def _place():
    x, y, c = (lax.axis_index(a) for a in MESH_AXES)
    chips = [(1 - x, y), (x, 1 - y), (1 - x, 1 - y)]
    return x, y, c, chips


def _chip_index(xy):
    return 2 * xy[0] + xy[1]


ANY = pl.BlockSpec(memory_space=pl.ANY)


HBM_SPEC = pl.BlockSpec(memory_space=pltpu.HBM)
SEM = pl.BlockSpec(memory_space=pltpu.SEMAPHORE)
EFFECT = pltpu.SideEffectType.DATAFLOW_SIDE_EFFECTING


def _half(ref, c):
    hr = ref.shape[-2] // 2
    return pl.ds(pl.multiple_of(c * hr, 16), hr)


def _gather_copies(srcs, lands, send, recv):
    x, y, c, chips = _place()
    me = _chip_index((x, y))
    return [pltpu.make_async_remote_copy(src_ref=s.at[_half(s, c)], dst_ref=g.at[me, _half(s, c)],
                                         send_sem=send.at[3 * a + j], recv_sem=recv.at[3 * a + j],
                                         device_id=(*ch, c), device_id_type=DEV)
            for a, (s, g) in enumerate(zip(srcs, lands)) for j, ch in enumerate(chips)]


def _scatter_copies(srcs, lands, send, recv):
    x, y, c, chips = _place()
    return [pltpu.make_async_remote_copy(src_ref=h.at[_chip_index(ch)], dst_ref=r.at[j],
                                         send_sem=send.at[3 * a + j], recv_sem=recv.at[3 * a + j],
                                         device_id=(*ch, c), device_id_type=DEV)
            for a, (h, r) in enumerate(zip(srcs, lands)) for j, ch in enumerate(chips)]


def _in_hbm(a):
    return pltpu.with_memory_space_constraint(a, pltpu.HBM)


def _split_start(name, srcs, land_shapes, copies_fn, after=None):
    ns, nl = len(srcs), len(land_shapes)
    ncp = 3 * ns
    lands = [lax.empty(s.shape, s.dtype) for s in land_shapes]
    behind = [] if after is None else [after]

    def body(*refs):
        src, land = refs[:ns], refs[ns:ns + nl]
        send, recv = refs[ns + nl + len(behind)], refs[ns + nl + len(behind) + 1]
        for cp in copies_fn(src, land, send, recv):
            cp.start()
        refs[-1][...] = jnp.zeros_like(refs[-1])

    bufs = list(srcs) + lands
    outs = pl.pallas_call(
        body, name=name, in_specs=[HBM_SPEC] * (ns + nl) + [ANY] * len(behind),
        out_specs=[SEM, SEM] + [HBM_SPEC] * (ns + nl) + [pl.BlockSpec(memory_space=pltpu.VMEM)],
        out_shape=[pltpu.SemaphoreType.DMA((ncp,)), pltpu.SemaphoreType.DMA((ncp,))]
        + [pltpu.HBM(b.shape, b.dtype) for b in bufs] + [_sds((8, LANE), F32)],
        input_output_aliases={i: 2 + i for i in range(ns + nl)},
        compiler_params=pltpu.CompilerParams(has_side_effects=EFFECT))(*[_in_hbm(b) for b in bufs], *behind)
    return outs[0], outs[1], list(outs[2:2 + ns]), list(outs[2 + ns:2 + ns + nl]), outs[-1]


def _split_wait(name, started, copies_fn, after):
    send, recv, srcs, lands, _ = started
    ns, nl = len(srcs), len(lands)

    def body(*refs):
        src, land = refs[:ns], refs[ns:ns + nl]
        for cp in copies_fn(src, land, refs[ns + nl], refs[ns + nl + 1]):
            cp.wait_send()
            cp.wait_recv()

    bufs = list(srcs) + list(lands)
    outs = pl.pallas_call(
        body, name=name, in_specs=[HBM_SPEC] * (ns + nl) + [SEM, SEM, ANY], out_specs=[HBM_SPEC] * (ns + nl),
        out_shape=[pltpu.HBM(b.shape, b.dtype) for b in bufs],
        input_output_aliases={i: i for i in range(ns + nl)},
        compiler_params=pltpu.CompilerParams(has_side_effects=EFFECT))(*bufs, send, recv, after)
    return list(outs[:ns]), list(outs[ns:])


def _gather_plain(name, srcs):
    n = len(srcs)

    def body(*refs):
        src, land = refs[:n], refs[n:2 * n]
        send, recv, fsend, frecv = refs[2 * n:]
        first = _gather_copies(src, land, send, recv)
        for cp in first:
            cp.start()
        _forward_body(land, first, fsend, frecv)

    return pl.pallas_call(
        body, name=name, in_specs=[ANY] * n, out_specs=[ANY] * n,
        out_shape=[_sds((4,) + s.shape, s.dtype) for s in srcs],
        scratch_shapes=[pltpu.SemaphoreType.DMA((3 * n,))] * 4)(*srcs)


def _forward_body(land, arrivals, fsend, frecv):
    x, y, c, chips = _place()
    n = len(land)
    passed = []
    for a in range(n):
        for j, ch in enumerate(chips):
            if arrivals is not None:
                arrivals[3 * a + j].wait_recv()
            slot = land[a].at[_chip_index(ch), _half(land[a], c)]
            fw = pltpu.make_async_remote_copy(src_ref=slot, dst_ref=slot, send_sem=fsend.at[3 * a + j],
                                              recv_sem=frecv.at[3 * a + j], device_id=(x, y, 1 - c),
                                              device_id_type=DEV)
            fw.start()
            passed.append(fw)
    for a in range(n):
        for j, ch in enumerate(chips):
            slot = land[a].at[_chip_index(ch), _half(land[a], 1 - c)]
            pltpu.make_async_remote_copy(src_ref=slot, dst_ref=slot, send_sem=fsend.at[3 * a + j],
                                         recv_sem=frecv.at[3 * a + j], device_id=(x, y, c),
                                         device_id_type=DEV).wait_recv()
    for cp in passed:
        cp.wait_send()
    if arrivals is not None:
        for cp in arrivals:
            cp.wait_send()


def _gather_forward(name, lands):
    n = len(lands)

    def body(*refs):
        _forward_body(refs[n:2 * n], None, refs[2 * n], refs[2 * n + 1])

    return pl.pallas_call(
        body, name=name, in_specs=[ANY] * n, out_specs=[ANY] * n,
        out_shape=[_sds(g.shape, g.dtype) for g in lands], input_output_aliases={a: a for a in range(n)},
        scratch_shapes=[pltpu.SemaphoreType.DMA((3 * n,))] * 2)(*lands)


def _sibling_halves(name, grs):
    n = len(grs)

    def body(*refs):
        ins, outs = refs[:n], refs[n:2 * n]
        send, recv = refs[2 * n:]
        x, y, c, _ = _place()
        cps = [pltpu.make_async_remote_copy(src_ref=ins[a].at[:, _half(ins[a], 1 - c)], dst_ref=outs[a],
                                            send_sem=send.at[a], recv_sem=recv.at[a], device_id=(x, y, 1 - c),
                                            device_id_type=DEV) for a in range(n)]
        for cp in cps:
            cp.start()
        for cp in cps:
            cp.wait()

    return pl.pallas_call(
        body, name=name, in_specs=[ANY] * n, out_specs=[ANY] * n,
        out_shape=[_sds((g.shape[0], g.shape[1] // 2, g.shape[2]), F32) for g in grs],
        scratch_shapes=[pltpu.SemaphoreType.DMA((n,)), pltpu.SemaphoreType.DMA((n,))])(*grs)


def _sibling_share(name, sms):
    n = len(sms)

    def body(*refs):
        ins, outs = refs[:n], refs[n:2 * n]
        send, recv = refs[2 * n:]
        x, y, c, _ = _place()
        cps = [pltpu.make_async_remote_copy(src_ref=ins[a], dst_ref=outs[a], send_sem=send.at[a],
                                            recv_sem=recv.at[a], device_id=(x, y, 1 - c), device_id_type=DEV)
               for a in range(n)]
        for cp in cps:
            cp.start()
        for cp in cps:
            cp.wait()

    return pl.pallas_call(
        body, name=name, in_specs=[ANY] * n, out_specs=[ANY] * n, out_shape=[_sds(s.shape, F32) for s in sms],
        scratch_shapes=[pltpu.SemaphoreType.DMA((n,))] * 2)(*sms)


def _small_allreduce(v):
    rows = v.shape[0]
    ndev = 8

    def body(v_ref, o_ref, gat_ref, send, recv):
        x, y, c, _ = _place()
        me = 4 * x + 2 * y + c
        cps = []
        for k in range(1, ndev):
            to = (me + k) % ndev
            cp = pltpu.make_async_remote_copy(src_ref=v_ref, dst_ref=gat_ref.at[me], send_sem=send.at[k - 1],
                                              recv_sem=recv.at[me], device_id=(to // 4, (to // 2) % 2, to % 2),
                                              device_id_type=DEV)
            cp.start()
            cps.append(cp)
        gat_ref[me] = v_ref[...]
        for k in range(1, ndev):
            frm = (me + k) % ndev
            pltpu.make_async_remote_copy(src_ref=v_ref, dst_ref=gat_ref.at[frm], send_sem=send.at[k - 1],
                                         recv_sem=recv.at[frm], device_id=(x, y, c), device_id_type=DEV).wait_recv()
        for cp in cps:
            cp.wait_send()
        acc = gat_ref[0]
        for k in range(1, ndev):
            acc = acc + gat_ref[k]
        o_ref[...] = acc

    vm = pl.BlockSpec(memory_space=pltpu.VMEM)
    return pl.pallas_call(
        body, name="small_allreduce", in_specs=[vm], out_specs=vm, out_shape=_sds((rows, LANE), F32),
        scratch_shapes=[pltpu.VMEM((ndev, rows, LANE), F32), pltpu.SemaphoreType.DMA((ndev - 1,)),
                        pltpu.SemaphoreType.DMA((ndev,))])(v)


def _layout(d):
    half = d // 2
    names = [("aq", d), ("ak", d), ("av", d), ("rq", half), ("rk", half), ("rv", d), ("rg", d),
             ("gq", half), ("gk", half), ("gv", d), ("gg", d), ("gates", 3 * d), ("glr", 2 * LANE)]
    off, pos = {}, 0
    for nm, sz in names:
        off[nm] = pos
        pos += sz
    return off, pos


def _pad_cols(w, d):
    a = 8 * d + d
    lr = w[..., a:a + GATE_RANK]
    z = jnp.zeros(w.shape[:-1] + (2 * LANE - GATE_RANK,), w.dtype)
    return jnp.concatenate([w[..., :a], w[..., a + GATE_RANK:], lr, z], axis=-1)


def _unpad_cols(g, d):
    a = 8 * d + d
    return jnp.concatenate([g[..., :a], g[..., a + 3 * d:a + 3 * d + GATE_RANK], g[..., a:a + 3 * d]], axis=-1)


def kernel(x, ln_in_g, ln_in_b, w_in, rel_bias, gla_w_lr, gla_b_lr, gla_norm_g, w_branch, w_out, ln1_g, ln1_b, w_up, w_down, ln2_g, ln2_b, loss_target, m_ln_in_g, m_ln_in_b, m_w_in, m_rel_bias, m_gla_w_lr, m_gla_b_lr, m_gla_norm_g, m_w_branch, m_w_out, m_ln1_g, m_ln1_b, m_w_up, m_w_down, m_ln2_g, m_ln2_b, v_ln_in_g, v_ln_in_b, v_w_in, v_rel_bias, v_gla_w_lr, v_gla_b_lr, v_gla_norm_g, v_w_branch, v_w_out, v_ln1_g, v_ln1_b, v_w_up, v_w_down, v_ln2_g, v_ln2_b):
    t, d = x.shape[1], x.shape[2]
    dff = 4 * d
    half = d // 2
    off, npad = _layout(d)
    xi, yi, ci = (lax.axis_index(a) for a in MESH_AXES)
    chip = 2 * xi + yi
    csel = jnp.reshape(ci, (1,)).astype(jnp.int32)
    psel = jnp.reshape(chip, (1,)).astype(jnp.int32)

    big_w = [w_in, w_branch.reshape(DEPTH, -1, d), w_out, w_up, w_down]
    big_m = [m_w_in, m_w_branch.reshape(DEPTH, -1, d), m_w_out, m_w_up, m_w_down]
    big_v = [v_w_in, v_w_branch.reshape(DEPTH, -1, d), v_w_out, v_w_up, v_w_down]
    W_IN, REST = [0], [1, 2, 3, 4]

    def shards_of(l, idx):
        return [big_w[i][l].astype(BF16) for i in idx]

    def lands_of(srcs):
        return [_sds((4,) + s.shape, s.dtype) for s in srcs]

    def full_w_in(g):
        return _pad_cols(jnp.transpose(g, (1, 0, 2)).reshape(d, -1), d)

    def full_rest(gs):
        g_br, g_out, g_up, g_down = gs
        return (jnp.transpose(g_br.reshape(4, N_BRANCH, d // 4, d), (1, 0, 2, 3)).reshape(N_BRANCH, d, d),
                g_out.reshape(d, d), jnp.transpose(g_up, (1, 0, 2)).reshape(d, dff), g_down.reshape(dff, d))

    def with_own(srcs, lands):
        return [lax.dynamic_update_slice(g, s[None], (chip, 0, 0)) for s, g in zip(srcs, lands)]

    def gather_start(tag, l, idx, after):
        srcs = shards_of(l, idx)
        return srcs, _split_start(f"gather_{tag}{l}_start", srcs, lands_of(srcs), _gather_copies, after)

    def gather_finish(tag, l, pending, after):
        srcs, started = pending
        _, lands = _split_wait(f"gather_{tag}{l}_wait", started, _gather_copies, after)
        return with_own(srcs, _gather_forward(f"gather_{tag}{l}_pass", lands))

    def token(pending):
        return pending[1][4][0, 0]

    win, wbr, wout, wup, wdown = ([None] * DEPTH for _ in range(5))
    src_first = shards_of(0, W_IN)
    g_first = with_own(src_first, _gather_plain("gather_in0", src_first))
    win[0] = full_w_in(g_first[0])

    dkh = half // LIN_HEADS
    lr_rows = DEPTH * GATE_RANK
    lr_slab = jnp.zeros((lr_rows, 4, half // 4), F32)
    lr_slab = lax.dynamic_update_slice(lr_slab, (gla_w_lr.reshape(lr_rows, 1, half // 4) * jnp.where(ci == 0, 1.0, 0.0)),
                                       (0, chip, 0))
    wlr_full = _small_allreduce(lr_slab.reshape(-1, LANE)).reshape(DEPTH, GATE_RANK, half)
    wlr_pad = jnp.concatenate([wlr_full, jnp.zeros((DEPTH, LANE - GATE_RANK, half), F32)], axis=1)
    pend_rest = gather_start("rest", 0, REST, wlr_full[0, :1, :1] + g_first[0][0, :1, :1].astype(F32))

    inv = 10000.0 ** (-jnp.arange(0, dkh, 2, dtype=F32) / dkh)
    ang = jnp.arange(t, dtype=F32)[:, None] * inv[None, :]
    cos, sin = jnp.cos(ang), jnp.sin(ang)
    rope_c = jnp.concatenate([cos, cos], axis=1)
    rope_s = jnp.concatenate([-sin, sin], axis=1)
    log_gamma = jnp.log1p(-jnp.exp2(-5.0 - jnp.arange(LIN_HEADS, dtype=F32)))
    lg_tab = jnp.broadcast_to(log_gamma[:, None, None], (LIN_HEADS, 1, dkh))

    def vec(a):
        return a.reshape(1, -1)

    x0, x0b, xh_in, rs_in = _ln_in(x[0], vec(ln_in_g) + token(pend_rest), vec(ln_in_b))
    saved = []
    xl, xlb = x0, x0b
    for l in range(DEPTH):
        p = _mm("proj_in", xlb, win[l], 512, 1792)
        g_rest = gather_finish("rest", l, pend_rest, p)
        wbr[l], wout[l], wup[l], wdown[l] = full_rest(g_rest)
        tok = 0.0
        if l + 1 < DEPTH:
            pend_in = gather_start("in", l + 1, W_IN, g_rest[0])
            tok = token(pend_in)
        bias = _bias_expand(rel_bias[l] + tok)
        attn = _attn_fwd(p, bias, d, off)
        ret_aux = (rope_c, rope_s, lg_tab + tok)
        gla_aux = (p, wlr_pad[l], vec(gla_b_lr[l]) + tok, vec(gla_norm_g[l]))
        o_ret, b_ret, st_ret = _lin_fwd(False, p, ret_aux, d, off)
        o_gla, b_gla, st_gla = _lin_fwd(True, p, gla_aux, d, off)
        tok = 0.0
        if l + 1 < DEPTH:
            g_in = gather_finish("in", l + 1, pend_in, b_gla)
            win[l + 1] = full_w_in(g_in[0])
            pend_rest = gather_start("rest", l + 1, REST, g_in[0])
            tok = token(pend_rest)
        bo = jnp.stack([attn, b_ret, b_gla])
        proj, merged = _merge_fwd(bo, wbr[l], p, off["gates"])
        x1, x1b, xh1, rs1 = _mm_res_ln("out_proj_ln", merged, wout[l], xl, vec(ln1_g[l]) + tok, vec(ln1_b[l]),
                                       256, False)
        u = _mm("mlp_up", x1b, wup[l], 512, 1024)
        x2, x2b, xh2, rs2, act = _mm_res_ln("mlp_down_ln", u, wdown[l], x1, vec(ln2_g[l]), vec(ln2_b[l]), 256, True)
        saved.append(dict(xlb=xlb, p=p, bias=bias, ret_aux=ret_aux, gla_aux=gla_aux, o_ret=o_ret, o_gla=o_gla,
                          st_ret=st_ret, st_gla=st_gla, bo=bo, proj=proj, merged=merged, x1b=x1b, xh1=xh1,
                          rs1=rs1, u=u, xh2=xh2, rs2=rs2, act=act))
        xl, xlb = x2, x2b

    small = {}
    last = saved[-1]
    loss_p, dz2, dz2b, dg, db = _loss_ln_bwd(xl, loss_target[0], last["xh2"], last["rs2"], vec(ln2_g[DEPTH - 1]))
    small["loss"] = loss_p[:, :1]
    grad_x = None

    def scatter_start(tag, l, idx, shards, after):
        theirs = _sibling_halves(f"grad_{tag}{l}_sibling", shards)
        hs = [_add_half("grad_sibling_add", g, th, csel) for g, th in zip(shards, theirs)]
        lands = [_sds((3,) + h.shape[1:], F32) for h in hs]
        return tag, l, idx, _split_start(f"grad_{tag}{l}_scatter_start", hs, lands, _scatter_copies, after)

    adam_out = [None] * len(big_w)

    def scatter_finish(pending, after):
        tag, l, idx, started = pending
        hs, rcv = _split_wait(f"grad_{tag}{l}_scatter_wait", started, _scatter_copies, after)
        sms = [_sum_shards("grad_chip_sum", h, r, psel) for h, r in zip(hs, rcv)]
        for i, own, sib in zip(idx, sms, _sibling_share(f"grad_{tag}{l}_share", sms)):
            adam_out[i] = _adamw_layer("adamw_large", big_w[i], own, sib, csel, big_m[i], big_v[i], l, adam_out[i])
        return adam_out[idx[0]][0]

    in_flight = []

    def scatter(tag, l, idx, shards, after=None):
        pending = scatter_start(tag, l, idx, shards, after)
        in_flight.append(pending)
        if len(in_flight) > 3:
            scatter_finish(in_flight.pop(0), pending[3][4])
        return pending[3][4][0, 0]

    for l in reversed(range(DEPTH)):
        s = saved[l]
        small[("ln2_g", l)], small[("ln2_b", l)] = dg, db
        du = _mm_nt_relu2_bwd(dz2b, wdown[l], s["u"])
        g_wdown = _mm_tn("grad_w_down", s["act"], dz2b, 512, 512)
        g_wup = _mm_tn("grad_w_up", s["x1b"], du, 512, 512)
        dz1, dz1b, dg1, db1 = _mm_nt_res_lnbwd("mlp_up_bwd_ln", du, wup[l], dz2, s["xh1"], s["rs1"],
                                               vec(ln1_g[l]), 256, dff)
        small[("ln1_g", l)], small[("ln1_b", l)] = dg1, db1
        dproj, dgl = _merge_bwd(dz1b, wout[l], s["proj"], s["p"], off["gates"])
        g_wout = _mm_tn("grad_w_out", s["merged"], dz1b, 512, 512)
        dbo = _mm("branch_proj_bwd", dproj, wbr[l], 512, 512, nt=True)
        g_wbr = _mm_tn("grad_w_branch", s["bo"], dproj, 512, 512)
        tok = scatter("rest", l, REST, [
            jnp.transpose(g_wbr.reshape(N_BRANCH, 4, d // 4, d), (1, 0, 2, 3)).reshape(4, -1, d),
            g_wout.reshape(4, d // 4, d), jnp.transpose(g_wup.reshape(d, 4, d), (1, 0, 2)), g_wdown.reshape(4, d, d)])
        rc, rs_, lg = s["ret_aux"]
        gp, gw, gb, gn_ = s["gla_aux"]
        dq_a, dk_acc, dv_acc, dbias = _attn_bwd(s["p"], s["bias"] + tok, dbo[0], d, off)
        small[("rel_bias", l)] = _bias_reduce(dbias)
        dk_a = dk_acc[2 * QB:].astype(BF16)
        dv_a = dv_acc[2 * QB:].astype(BF16)
        dq_r, dk_r, dv_r, dg_r = _lin_bwd(False, s["p"], (rc, rs_, lg + tok), s["o_ret"], s["st_ret"], dbo[1], d, off)
        dq_g, dk_g, dv_g, dg_g, dpre, dblr, dgn = _lin_bwd(True, s["p"], (gp, gw, gb + tok, gn_), s["o_gla"],
                                                           s["st_gla"], dbo[2], d, off)
        small[("gla_b_lr", l)] = dblr.reshape(1, half)
        small[("gla_norm_g", l)] = jnp.sum(dgn, axis=0)
        dpre_b = dpre.astype(BF16)
        glr_b = s["p"][:, off["glr"]:off["glr"] + LANE].astype(BF16)
        dglr = _mm("gate_lr_bwd", dpre_b, wlr_pad[l], 512, LANE, nt=True, out_dtype=BF16)
        small[("gla_w_lr", l)] = _mm_tn("grad_gla_w_lr", glr_b, dpre_b, LANE, half)[:GATE_RANK]
        dp = jnp.concatenate([dq_a, dk_a, dv_a, dq_r, dk_r, dv_r, dg_r, dq_g, dk_g, dv_g, dg_g,
                              dgl[0], dgl[1], dgl[2], dglr, jnp.zeros((t, LANE), BF16)], axis=1)
        if l > 0:
            prev = saved[l - 1]
            xh_p, rs_p, g_p = prev["xh2"], prev["rs2"], vec(ln2_g[l - 1])
        else:
            xh_p, rs_p, g_p = xh_in, rs_in, vec(ln_in_g)
        g_win = _mm_tn("grad_w_in", s["xlb"], dp, 1024, 896)
        tok = scatter("in", l, W_IN, [jnp.transpose(_unpad_cols(g_win, d).reshape(d, 4, -1), (1, 0, 2))])
        dzp, dzpb, dg, db = _mm_nt_res_lnbwd("proj_in_bwd_ln", dp, win[l], dz1, xh_p, rs_p, g_p + tok, 1024, 1792)
        dz2, dz2b = dzp, dzpb
        grad_x = dzp
    small["ln_in_g"], small["ln_in_b"] = dg, db
    rb_pad = 3 * LANE
    pieces = [small["loss"].reshape(-1), jnp.zeros((LANE - 1,), F32), small["ln_in_g"].reshape(-1),
              small["ln_in_b"].reshape(-1)]
    for l in range(DEPTH):
        rb = jnp.pad(small[("rel_bias", l)], ((0, 0), (0, rb_pad - (2 * REL_CLIP + 1))))
        pieces += [rb.reshape(-1), small[("gla_w_lr", l)].reshape(-1), small[("gla_b_lr", l)].reshape(-1),
                   small[("gla_norm_g", l)].reshape(-1), small[("ln1_g", l)].reshape(-1),
                   small[("ln1_b", l)].reshape(-1), small[("ln2_g", l)].reshape(-1), small[("ln2_b", l)].reshape(-1)]
    sizes = [pc.shape[0] for pc in pieces]
    packed = jnp.concatenate(pieces)
    padn = (-packed.shape[0]) % (8 * LANE)
    packed = jnp.concatenate([packed, jnp.zeros((padn,), F32)]).reshape(-1, LANE)
    red2d = _small_allreduce(packed)
    red = red2d.reshape(-1)
    after = red2d
    while in_flight:
        after = scatter_finish(in_flight.pop(0), after)

    parts, pos = [], 0
    for sz in sizes:
        parts.append(red[pos:pos + sz])
        pos += sz
    loss = parts[0][0]
    g_ln_in_g, g_ln_in_b = parts[2], parts[3]
    per = 8
    g_rel = jnp.stack([parts[4 + per * l].reshape(ATTN_HEADS, rb_pad)[:, :2 * REL_CLIP + 1] for l in range(DEPTH)])
    g_wlr_full = jnp.stack([parts[5 + per * l].reshape(GATE_RANK, half) for l in range(DEPTH)])
    g_wlr = lax.dynamic_slice_in_dim(g_wlr_full, chip * (half // 4), half // 4, axis=2)
    g_blr = jnp.stack([parts[6 + per * l] for l in range(DEPTH)])
    g_gn = jnp.stack([parts[7 + per * l] for l in range(DEPTH)])
    g_ln1g = jnp.stack([parts[8 + per * l] for l in range(DEPTH)])
    g_ln1b = jnp.stack([parts[9 + per * l] for l in range(DEPTH)])
    g_ln2g = jnp.stack([parts[10 + per * l] for l in range(DEPTH)])
    g_ln2b = jnp.stack([parts[11 + per * l] for l in range(DEPTH)])

    grads = [g_ln_in_g, g_ln_in_b, None, g_rel, g_wlr, g_blr, g_gn, None, None, g_ln1g, g_ln1b, None, None,
             g_ln2g, g_ln2b]
    ws = [ln_in_g, ln_in_b, w_in, rel_bias, gla_w_lr, gla_b_lr, gla_norm_g, w_branch, w_out, ln1_g, ln1_b,
          w_up, w_down, ln2_g, ln2_b]
    ms = [m_ln_in_g, m_ln_in_b, m_w_in, m_rel_bias, m_gla_w_lr, m_gla_b_lr, m_gla_norm_g, m_w_branch, m_w_out,
          m_ln1_g, m_ln1_b, m_w_up, m_w_down, m_ln2_g, m_ln2_b]
    vs = [v_ln_in_g, v_ln_in_b, v_w_in, v_rel_bias, v_gla_w_lr, v_gla_b_lr, v_gla_norm_g, v_w_branch, v_w_out,
          v_ln1_g, v_ln1_b, v_w_up, v_w_down, v_ln2_g, v_ln2_b]

    deltas, new_ms, new_vs = [None] * 15, [None] * 15, [None] * 15
    big_idx = [2, 7, 8, 11, 12]
    for i, res in zip(big_idx, adam_out):
        shp = ws[i].shape
        grads[i], deltas[i], new_ms[i], new_vs[i] = (r.reshape(shp) for r in res)
    small_idx = [i for i in range(15) if i not in big_idx]

    def pack(arrs):
        flat_ = jnp.concatenate([arrs[i].reshape(-1) for i in small_idx])
        pad_ = (-flat_.shape[0]) % (8 * LANE)
        return jnp.concatenate([flat_, jnp.ones((pad_,), F32)]).reshape(-1, LANE)

    dl, nm, nv = _adamw("adamw_small", pack(ws), pack(grads), pack(ms), pack(vs))
    pos = 0
    for i in small_idx:
        sz = int(np.prod(ws[i].shape))
        deltas[i] = dl.reshape(-1)[pos:pos + sz].reshape(ws[i].shape)
        new_ms[i] = nm.reshape(-1)[pos:pos + sz].reshape(ws[i].shape)
        new_vs[i] = nv.reshape(-1)[pos:pos + sz].reshape(ws[i].shape)
        pos += sz

    return (loss, grad_x[None], *grads, *deltas, *new_ms, *new_vs)
```

```python
import functools

import numpy as np
import jax
import jax.numpy as jnp
from jax import lax
from jax.experimental import pallas as pl
from jax.experimental.pallas import tpu as pltpu

F32 = jnp.float32
BF16 = jnp.bfloat16
MXU_DTYPE = BF16
HI = lax.Precision.HIGHEST

DEPTH = 2
CHUNK = 64
N_BRANCH = 3
ATTN_HEADS = 8
ATTN_LEFT = 8
REL_CLIP = 2 * CHUNK
LIN_HEADS = 4
GATE_RANK = 16
GATE_NORM = 16.0
LN_EPS = 1e-5
NEG_INF = -1e30
ALPHA = (2 * DEPTH) ** 0.25
ADAM_LR, ADAM_B1, ADAM_B2, ADAM_EPS, ADAM_WD, ADAM_STEP = 0.001, 0.9, 0.999, 1e-08, 0.01, 10

LANE = 128
VMEM_LIMIT = 56 << 20
QB = 256
KW = 3 * QB
LB = 256
MESH_AXES = ("x", "y", "c")
DEV = pl.DeviceIdType.MESH


def _cp(sem):
    return pltpu.CompilerParams(dimension_semantics=sem, vmem_limit_bytes=VMEM_LIMIT)


def _mx(v):
    return v.astype(MXU_DTYPE)


def _dot(a, b):
    return jnp.dot(_mx(a), _mx(b), preferred_element_type=F32)


def _dot_nt(a, b):
    return lax.dot_general(_mx(a), _mx(b), (((1,), (1,)), ((), ())), preferred_element_type=F32)


def _dot_tn(a, b):
    return lax.dot_general(_mx(a), _mx(b), (((0,), (0,)), ((), ())), preferred_element_type=F32)


def _dot_hi(a, b):
    return jnp.dot(a, b, precision=HI, preferred_element_type=F32)


def _sigmoid(v):
    return 1.0 / (1.0 + jnp.exp(-v))


def _sds(shape, dtype):
    return jax.ShapeDtypeStruct(shape, dtype)


def _mm(name, a, b, tm, tn, nt=False, out_dtype=F32):
    batched = a.ndim == 3
    m, k = a.shape[-2:]
    n = b.shape[-2] if nt else b.shape[-1]
    tm, tn = min(tm, m), min(tn, n)

    def body(a_ref, b_ref, o_ref):
        f = _dot_nt if nt else _dot
        o_ref[...] = f(a_ref[...], b_ref[...]).astype(o_ref.dtype)

    rows_inner = (n // tn) * m < (m // tm) * n

    def ij(u, v):
        return (v, u) if rows_inner else (u, v)

    if batched:
        nb = a.shape[0]
        grid = (nb,) + ij(m // tm, n // tn)
        a_spec = pl.BlockSpec((None, tm, k), lambda g, u, v: (g, ij(u, v)[0], 0))
        b_spec = (pl.BlockSpec((None, tn, k), lambda g, u, v: (g, ij(u, v)[1], 0)) if nt
                  else pl.BlockSpec((None, k, tn), lambda g, u, v: (g, 0, ij(u, v)[1])))
        o_spec = pl.BlockSpec((None, tm, tn), lambda g, u, v: (g,) + ij(u, v))
        out_shape = _sds((nb, m, n), out_dtype)
        sem = ("parallel", "parallel", "parallel")
    else:
        grid = ij(m // tm, n // tn)
        a_spec = pl.BlockSpec((tm, k), lambda u, v: (ij(u, v)[0], 0))
        b_spec = (pl.BlockSpec((tn, k), lambda u, v: (ij(u, v)[1], 0)) if nt
                  else pl.BlockSpec((k, tn), lambda u, v: (0, ij(u, v)[1])))
        o_spec = pl.BlockSpec((tm, tn), lambda u, v: ij(u, v))
        out_shape = _sds((m, n), out_dtype)
        sem = ("parallel", "parallel")
    return pl.pallas_call(body, name=name, grid=grid, in_specs=[a_spec, b_spec], out_specs=o_spec,
                          out_shape=out_shape, compiler_params=_cp(sem))(a, b)


def _mm_tn(name, a, b, tm, tn):
    batched = a.ndim == 3
    k, m = a.shape[-2:]
    n = b.shape[-1]
    tm, tn = min(tm, m), min(tn, n)

    def body(a_ref, b_ref, o_ref):
        o_ref[...] = lax.dot_general(_mx(a_ref[...]), _mx(b_ref[...]), (((0,), (0,)), ((), ())),
                                     preferred_element_type=F32)

    if batched:
        nb = a.shape[0]
        grid = (nb, m // tm, n // tn)
        a_spec = pl.BlockSpec((None, k, tm), lambda g, i, j: (g, 0, i))
        b_spec = pl.BlockSpec((None, k, tn), lambda g, i, j: (g, 0, j))
        o_spec = pl.BlockSpec((None, tm, tn), lambda g, i, j: (g, i, j))
        out_shape = _sds((nb, m, n), F32)
    else:
        grid = (m // tm, n // tn)
        a_spec = pl.BlockSpec((k, tm), lambda i, j: (0, i))
        b_spec = pl.BlockSpec((k, tn), lambda i, j: (0, j))
        o_spec = pl.BlockSpec((tm, tn), lambda i, j: (i, j))
        out_shape = _sds((m, n), F32)
    return pl.pallas_call(body, name=name, grid=grid, in_specs=[a_spec, b_spec], out_specs=o_spec,
                          out_shape=out_shape, compiler_params=_cp(("parallel",) * len(grid)))(a, b)


def _ln_rows(y, g, b):
    mu = jnp.mean(y, axis=-1, keepdims=True)
    yc = y - mu
    var = jnp.mean(yc * yc, axis=-1, keepdims=True)
    rs = lax.rsqrt(var + LN_EPS)
    xh = yc * rs
    return xh * g + b, xh, rs


def _ln_in(x, g, b, tm=256):
    t, d = x.shape

    def body(x_ref, g_ref, b_ref, o_ref, ob_ref, xh_ref, rs_ref):
        o, xh, rs = _ln_rows(x_ref[...], g_ref[...], b_ref[...])
        o_ref[...] = o
        ob_ref[...] = o.astype(BF16)
        xh_ref[...] = xh
        rs_ref[...] = rs

    row = pl.BlockSpec((tm, d), lambda i: (i, 0))
    vec = pl.BlockSpec((1, d), lambda i: (0, 0))
    return pl.pallas_call(
        body, name="ln_in", grid=(t // tm,), in_specs=[row, vec, vec],
        out_specs=[row, row, row, pl.BlockSpec((tm, 1), lambda i: (i, 0))],
        out_shape=[_sds((t, d), F32), _sds((t, d), BF16), _sds((t, d), F32), _sds((t, 1), F32)],
        compiler_params=_cp(("parallel",)))(x, g, b)


def _mm_res_ln(name, a, w, res, g, b, tm, relu2):
    t, k = a.shape
    d = w.shape[1]

    def body(a_ref, w_ref, r_ref, g_ref, b_ref, o_ref, ob_ref, xh_ref, rs_ref, *act_ref):
        av = a_ref[...]
        if relu2:
            av = jnp.square(jnp.maximum(av, 0.0))
            act_ref[0][...] = av.astype(BF16)
        y = ALPHA * r_ref[...] + _dot(av, w_ref[...])
        o, xh, rs = _ln_rows(y, g_ref[...], b_ref[...])
        o_ref[...] = o
        ob_ref[...] = o.astype(BF16)
        xh_ref[...] = xh
        rs_ref[...] = rs

    row = pl.BlockSpec((tm, d), lambda i: (i, 0))
    vec = pl.BlockSpec((1, d), lambda i: (0, 0))
    arow = pl.BlockSpec((tm, k), lambda i: (i, 0))
    out_specs = [row, row, row, pl.BlockSpec((tm, 1), lambda i: (i, 0))]
    out_shape = [_sds((t, d), F32), _sds((t, d), BF16), _sds((t, d), F32), _sds((t, 1), F32)]
    if relu2:
        out_specs.append(arow)
        out_shape.append(_sds((t, k), BF16))
    return pl.pallas_call(
        body, name=name, grid=(t // tm,),
        in_specs=[arow, pl.BlockSpec((k, d), lambda i: (0, 0)), row, vec, vec],
        out_specs=out_specs, out_shape=out_shape, compiler_params=_cp(("parallel",)))(a, w, res, g, b)


def _merge_fwd(bo, wb, p, gate_off, tm=512, tn=512):
    _, t, d = bo.shape
    gb = gate_off // tn

    def body(bo_ref, wb_ref, g0, g1, g2, proj_ref, m_ref):
        acc = None
        for n, g_ref in enumerate((g0, g1, g2)):
            pr = _dot(bo_ref[n], wb_ref[n])
            proj_ref[n] = pr
            term = _sigmoid(g_ref[...]) * pr
            acc = term if acc is None else acc + term
        m_ref[...] = acc.astype(BF16)

    gspecs = [pl.BlockSpec((tm, tn), functools.partial(lambda i, j, n: (i, gb + n * (d // tn) + j), n=n))
              for n in range(3)]
    return pl.pallas_call(
        body, name="merge_fwd", grid=(t // tm, d // tn),
        in_specs=[pl.BlockSpec((3, tm, d), lambda i, j: (0, i, 0)),
                  pl.BlockSpec((3, d, tn), lambda i, j: (0, 0, j))] + gspecs,
        out_specs=[pl.BlockSpec((3, tm, tn), lambda i, j: (0, i, j)), pl.BlockSpec((tm, tn), lambda i, j: (i, j))],
        out_shape=[_sds((3, t, d), F32), _sds((t, d), BF16)],
        compiler_params=_cp(("parallel", "parallel")))(bo, wb, p, p, p)


def _merge_bwd(dz, wout, proj, p, gate_off, tm=512, tn=512):
    t, d = dz.shape
    gb = gate_off // tn

    def body(dz_ref, w_ref, proj_ref, g0, g1, g2, dproj_ref, dgl_ref):
        dm = _dot_nt(dz_ref[...], w_ref[...])
        for n, g_ref in enumerate((g0, g1, g2)):
            s = _sigmoid(g_ref[...])
            dproj_ref[n] = (dm * s).astype(BF16)
            dgl_ref[n] = (dm * proj_ref[n] * (s * (1.0 - s))).astype(BF16)

    gspecs = [pl.BlockSpec((tm, tn), functools.partial(lambda i, j, n: (i, gb + n * (d // tn) + j), n=n))
              for n in range(3)]
    dproj, dgl = pl.pallas_call(
        body, name="merge_bwd", grid=(t // tm, d // tn),
        in_specs=[pl.BlockSpec((tm, d), lambda i, j: (i, 0)), pl.BlockSpec((tn, d), lambda i, j: (j, 0)),
                  pl.BlockSpec((3, tm, tn), lambda i, j: (0, i, j))] + gspecs,
        out_specs=[pl.BlockSpec((3, tm, tn), lambda i, j: (0, i, j)),
                   pl.BlockSpec((3, tm, tn), lambda i, j: (0, i, j))],
        out_shape=[_sds((3, t, d), BF16), _sds((3, t, d), BF16)],
        compiler_params=_cp(("parallel", "parallel")))(dz, wout, proj, p, p, p)
    return dproj, dgl


def _mm_nt_relu2_bwd(dz, wdown, u, tm=512, tn=1024):
    t, d = dz.shape
    f = wdown.shape[0]

    def body(dz_ref, w_ref, u_ref, du_ref):
        da = _dot_nt(dz_ref[...], w_ref[...])
        du_ref[...] = (da * (2.0 * jnp.maximum(u_ref[...], 0.0))).astype(BF16)

    return pl.pallas_call(
        body, name="mlp_down_bwd", grid=(t // tm, f // tn),
        in_specs=[pl.BlockSpec((tm, d), lambda i, j: (i, 0)), pl.BlockSpec((tn, d), lambda i, j: (j, 0)),
                  pl.BlockSpec((tm, tn), lambda i, j: (i, j))],
        out_specs=pl.BlockSpec((tm, tn), lambda i, j: (i, j)), out_shape=_sds((t, f), BF16),
        compiler_params=_cp(("parallel", "parallel")))(dz, wdown, u)


def _ln_bwd_rows(dx, xh, rs, g):
    dxh = dx * g
    m1 = jnp.mean(dxh, axis=-1, keepdims=True)
    m2 = jnp.mean(dxh * xh, axis=-1, keepdims=True)
    return rs * (dxh - m1 - xh * m2)


def _mm_nt_res_lnbwd(name, a, w, dres, xh, rs, g, tm, tk):
    t, k = a.shape
    d = w.shape[0]
    nk = k // tk

    def body(a_ref, w_ref, dr_ref, xh_ref, rs_ref, g_ref, dz_ref, dzb_ref, dg_ref, db_ref, acc_ref):
        i, kk = pl.program_id(0), pl.program_id(1)

        @pl.when(kk == 0)
        def _():
            acc_ref[...] = ALPHA * dr_ref[...]

        acc_ref[...] += _dot_nt(a_ref[...], w_ref[...])

        @pl.when(jnp.logical_and(i == 0, kk == 0))
        def _():
            dg_ref[...] = jnp.zeros_like(dg_ref)
            db_ref[...] = jnp.zeros_like(db_ref)

        @pl.when(kk == nk - 1)
        def _():
            dx = acc_ref[...]
            xhv = xh_ref[...]
            dz = _ln_bwd_rows(dx, xhv, rs_ref[...], g_ref[...])
            dz_ref[...] = dz
            dzb_ref[...] = dz.astype(BF16)
            dg_ref[...] += jnp.sum(dx * xhv, axis=0, keepdims=True)
            db_ref[...] += jnp.sum(dx, axis=0, keepdims=True)

    row = pl.BlockSpec((tm, d), lambda i, kk: (i, 0))
    vec = pl.BlockSpec((1, d), lambda i, kk: (0, 0))
    return pl.pallas_call(
        body, name=name, grid=(t // tm, nk),
        in_specs=[pl.BlockSpec((tm, tk), lambda i, kk: (i, kk)), pl.BlockSpec((d, tk), lambda i, kk: (0, kk)),
                  row, row, pl.BlockSpec((tm, 1), lambda i, kk: (i, 0)), vec],
        out_specs=[row, row, vec, vec],
        out_shape=[_sds((t, d), F32), _sds((t, d), BF16), _sds((1, d), F32), _sds((1, d), F32)],
        scratch_shapes=[pltpu.VMEM((tm, d), F32)],
        compiler_params=_cp(("arbitrary", "arbitrary")))(a, w, dres, xh, rs, g)


def _loss_ln_bwd(x2, target, xh, rs, g, tm=256):
    t, d = x2.shape

    def body(x_ref, t_ref, xh_ref, rs_ref, g_ref, loss_ref, dz_ref, dzb_ref, dg_ref, db_ref):
        @pl.when(pl.program_id(0) == 0)
        def _():
            loss_ref[...] = jnp.zeros_like(loss_ref)
            dg_ref[...] = jnp.zeros_like(dg_ref)
            db_ref[...] = jnp.zeros_like(db_ref)

        err = x_ref[...] - t_ref[...]
        per_row = jnp.mean(err * err, axis=-1, keepdims=True)
        loss_ref[...] += 0.5 * jnp.sum(per_row, axis=0, keepdims=True)
        dx = err * (1.0 / d)
        xhv = xh_ref[...]
        dz = _ln_bwd_rows(dx, xhv, rs_ref[...], g_ref[...])
        dz_ref[...] = dz
        dzb_ref[...] = dz.astype(BF16)
        dg_ref[...] += jnp.sum(dx * xhv, axis=0, keepdims=True)
        db_ref[...] += jnp.sum(dx, axis=0, keepdims=True)

    row = pl.BlockSpec((tm, d), lambda i: (i, 0))
    vec = pl.BlockSpec((1, d), lambda i: (0, 0))
    return pl.pallas_call(
        body, name="loss_ln_bwd", grid=(t // tm,),
        in_specs=[row, row, row, pl.BlockSpec((tm, 1), lambda i: (i, 0)), vec],
        out_specs=[pl.BlockSpec((1, LANE), lambda i: (0, 0)), row, row, vec, vec],
        out_shape=[_sds((1, LANE), F32), _sds((t, d), F32), _sds((t, d), BF16), _sds((1, d), F32),
                   _sds((1, d), F32)],
        compiler_params=_cp(("arbitrary",)))(x2, target, xh, rs, g)


HPA = 2


def _attn_scores(q_ref, k_refs, bias_ref, i, dh, hh):
    cols = pl.ds(hh * dh, dh)
    q = q_ref[:, cols] * (dh ** -0.5)
    k = jnp.concatenate([r[:, cols] for r in k_refs], axis=0)
    s = _dot_nt(q, k) + bias_ref[hh]
    col = lax.broadcasted_iota(jnp.int32, s.shape, 1)
    s = jnp.where(col >= (2 - i) * QB, s, NEG_INF)
    m = jnp.max(s, axis=-1, keepdims=True)
    e = jnp.exp(s - m)
    return q, k, e / jnp.sum(e, axis=-1, keepdims=True)


def _attn_specs(dh, off):
    w = HPA * dh
    qcol, kcol, vcol = off["aq"] // w, off["ak"] // w, off["av"] // w
    q_spec = pl.BlockSpec((QB, w), lambda g, i: (i, qcol + g))
    k_specs = [pl.BlockSpec((QB, w), functools.partial(lambda g, i, j: (jnp.maximum(i - 2 + j, 0), kcol + g), j=j))
               for j in range(3)]
    v_specs = [pl.BlockSpec((QB, w), functools.partial(lambda g, i, j: (jnp.maximum(i - 2 + j, 0), vcol + g), j=j))
               for j in range(3)]
    bias_spec = pl.BlockSpec((HPA, QB, KW), lambda g, i: (g, 0, 0))
    return q_spec, k_specs, v_specs, bias_spec


def _attn_fwd(p, bias, d, off):
    t = p.shape[0]
    dh = d // ATTN_HEADS

    def body(q_ref, k0, k1, k2, v0, v1, v2, bias_ref, o_ref):
        for hh in range(HPA):
            cols = pl.ds(hh * dh, dh)
            _, _, pr = _attn_scores(q_ref, (k0, k1, k2), bias_ref, pl.program_id(1), dh, hh)
            v = jnp.concatenate([v0[:, cols], v1[:, cols], v2[:, cols]], axis=0)
            o_ref[:, cols] = _dot(pr, v).astype(o_ref.dtype)

    q_spec, k_specs, v_specs, bias_spec = _attn_specs(dh, off)
    return pl.pallas_call(
        body, name="attn_fwd", grid=(ATTN_HEADS // HPA, t // QB),
        in_specs=[q_spec] + k_specs + v_specs + [bias_spec],
        out_specs=pl.BlockSpec((QB, HPA * dh), lambda g, i: (i, g)), out_shape=_sds((t, d), BF16),
        compiler_params=_cp(("parallel", "parallel")))(p, p, p, p, p, p, p, bias)


def _attn_bwd(p, bias, do, d, off):
    t = p.shape[0]
    dh = d // ATTN_HEADS
    tp = t + 2 * QB

    def body(q_ref, k0, k1, k2, v0, v1, v2, bias_ref, do_ref, dq_ref, dk_ref, dv_ref, dbias_ref):
        i = pl.program_id(1)

        @pl.when(i == 0)
        def _():
            dk_ref[...] = jnp.zeros_like(dk_ref)
            dv_ref[...] = jnp.zeros_like(dv_ref)
            dbias_ref[...] = jnp.zeros_like(dbias_ref)

        rows = pl.ds(pl.multiple_of(i * QB, QB), KW)
        for hh in range(HPA):
            cols = pl.ds(hh * dh, dh)
            q, k, pr = _attn_scores(q_ref, (k0, k1, k2), bias_ref, i, dh, hh)
            v = jnp.concatenate([v0[:, cols], v1[:, cols], v2[:, cols]], axis=0)
            dov = do_ref[:, cols]
            dp = _dot_nt(dov, v)
            delta = jnp.sum(pr * dp, axis=-1, keepdims=True)
            ds = pr * (dp - delta)
            dbias_ref[hh] += ds
            dq_ref[:, cols] = (_dot(ds, k) * (dh ** -0.5)).astype(dq_ref.dtype)
            dk_ref[rows, cols] += _dot_tn(ds, q)
            dv_ref[rows, cols] += _dot_tn(pr, dov)

    q_spec, k_specs, v_specs, bias_spec = _attn_specs(dh, off)
    row_spec = pl.BlockSpec((QB, HPA * dh), lambda g, i: (i, g))
    acc_spec = pl.BlockSpec((tp, HPA * dh), lambda g, i: (0, g))
    return pl.pallas_call(
        body, name="attn_bwd", grid=(ATTN_HEADS // HPA, t // QB),
        in_specs=[q_spec] + k_specs + v_specs + [bias_spec, row_spec],
        out_specs=[row_spec, acc_spec, acc_spec, bias_spec],
        out_shape=[_sds((t, d), BF16), _sds((tp, d), F32), _sds((tp, d), F32),
                   _sds((ATTN_HEADS, QB, KW), F32)],
        compiler_params=_cp(("parallel", "arbitrary")))(p, p, p, p, p, p, p, bias, do)


def _onehot_mm(name, a, b):
    def body(a_ref, b_ref, o_ref):
        o_ref[...] = _dot_hi(a_ref[...], b_ref[...])

    return pl.pallas_call(body, name=name, out_shape=_sds((a.shape[0], b.shape[1]), F32),
                          compiler_params=pltpu.CompilerParams(vmem_limit_bytes=VMEM_LIMIT))(a, b)


def _diag_index():
    ii, jj = np.arange(CHUNK)[:, None], np.arange(CHUNK)[None, :]
    return (ii - jj + CHUNK - 1).reshape(-1)


def _bias_expand(rel_bias):
    h = rel_bias.shape[0]
    nq, nk, shift = QB // CHUNK, KW // CHUNK, (2 * QB) // CHUNK
    nbin, ndc = 3 * LANE, 4
    rb = jnp.pad(rel_bias, ((0, 0), (0, nbin - rel_bias.shape[1])))
    win = np.clip(CHUNK * np.arange(ndc)[:, None] + np.arange(LANE)[None, :] - (CHUNK - 1), -REL_CLIP, REL_CLIP)
    sel = (jnp.arange(nbin)[:, None] == jnp.asarray((win + REL_CLIP).reshape(1, -1))).astype(F32)
    windows = _onehot_mm("bias_windows", rb, sel)
    diag_t = (jnp.arange(LANE)[:, None] == jnp.asarray(_diag_index().reshape(1, -1))).astype(F32)
    blocks = _onehot_mm("bias_blocks", windows.reshape(h * ndc, LANE), diag_t).reshape(h, ndc, CHUNK, CHUNK)
    off_band = jnp.full((h, CHUNK, CHUNK), NEG_INF, F32)
    rows = []
    for ic in range(nq):
        dcs = [ic - jc + shift for jc in range(nk)]
        rows.append(jnp.concatenate([blocks[:, min(dc, ndc - 1)] if 0 <= dc <= ATTN_LEFT else off_band
                                     for dc in dcs], axis=2))
    return jnp.concatenate(rows, axis=1)


def _bias_reduce(dbias):
    h = dbias.shape[0]
    nq, nk = QB // CHUNK, KW // CHUNK
    nbin = 3 * LANE
    blocks = dbias.reshape(h, nq, CHUNK, nk, CHUNK).transpose(0, 1, 3, 2, 4).reshape(h * nq * nk, CHUNK * CHUNK)
    diag = (jnp.asarray(_diag_index().reshape(-1, 1)) == jnp.arange(LANE)[None, :]).astype(F32)
    ic = np.arange(nq)[:, None, None]
    jc = np.arange(nk)[None, :, None]
    dl = np.arange(LANE)[None, None, :] - (CHUNK - 1)
    rel = np.clip(CHUNK * (ic - jc + (2 * QB) // CHUNK) + dl, -REL_CLIP, REL_CLIP) + REL_CLIP
    bins = (jnp.asarray(rel.reshape(-1, 1)) == jnp.arange(nbin)[None, :]).astype(F32)

    diags = _onehot_mm("bias_diag_sums", blocks, diag)
    out = _onehot_mm("bias_bin_sums", diags.reshape(h, nq * nk * LANE), bins)
    return out[:, :2 * REL_CLIP + 1]


def _tri(lower):
    r = lax.broadcasted_iota(jnp.int32, (CHUNK, CHUNK), 0)
    c = lax.broadcasted_iota(jnp.int32, (CHUNK, CHUNK), 1)
    return (r >= c) if lower else (r <= c)


def _lin_prep(gla, q, k, aux):
    dk = q.shape[-1]
    if gla:
        glr, wlr, blr = aux
        q = q * (dk ** -0.5)
        pre = _dot(glr, wlr) + blr
        log_a = (jnp.minimum(pre, 0.0) - jnp.log(1.0 + jnp.exp(-jnp.abs(pre)))) / GATE_NORM
        b = _dot_hi(_tri(True).astype(F32), log_a)
        return q, k, b, pre
    cs, sn, lg = aux
    half = dk // 2
    q = q * cs + pltpu.roll(q, half, 1) * sn
    k = (k * cs + pltpu.roll(k, half, 1) * sn) * (dk ** -0.5)
    pos = lax.broadcasted_iota(jnp.int32, (CHUNK, dk), 0).astype(F32) + 1.0
    return q, k, pos * lg, None


def _lin_chunk(q, k, v, b, st):
    eb, enb = jnp.exp(b), jnp.exp(-b)
    last = b[CHUNK - 1:CHUNK, :]
    qf, kf, qb, kb = q * eb, k * enb, q * enb, k * eb
    kl = k * jnp.exp(last - b)
    s = jnp.where(_tri(True), _dot_nt(qf, kf), _dot_nt(qb, kb))
    o = _dot(s, v) + _dot_nt(qf, st)
    st_new = st * jnp.exp(last) + _dot_tn(v, kl)
    return o, st_new, (eb, enb, last, qf, kf, qb, kb, kl, s)


def _lin_norm_gate(gla, o, gate, gn):
    sg = _sigmoid(gate)
    silu = gate * sg
    if gla:
        r = lax.rsqrt(jnp.mean(o * o, axis=-1, keepdims=True) + LN_EPS)
        hn = o * r
        return silu * (hn * gn), (sg, silu, r, hn)
    mu = jnp.mean(o, axis=-1, keepdims=True)
    oc = o - mu
    r = lax.rsqrt(jnp.mean(oc * oc, axis=-1, keepdims=True) + LN_EPS)
    hn = oc * r
    return silu * hn, (sg, silu, r, hn)


HPS = 2


def _lin_specs(gla, dk, dv, off, rev, nb):
    pre = "g" if gla else "r"
    wk, wv = HPS * dk, HPS * dv
    qc, kc, vc, gc = (off[pre + "q"] // wk, off[pre + "k"] // wk, off[pre + "v"] // wv, off[pre + "g"] // wv)

    def blk(i):
        return nb - 1 - i if rev else i

    specs = [pl.BlockSpec((LB, wk), lambda g, i: (blk(i), qc + g)),
             pl.BlockSpec((LB, wk), lambda g, i: (blk(i), kc + g)),
             pl.BlockSpec((LB, wv), lambda g, i: (blk(i), vc + g)),
             pl.BlockSpec((LB, wv), lambda g, i: (blk(i), gc + g))]
    if gla:
        specs += [pl.BlockSpec((LB, LANE), lambda g, i: (blk(i), off["glr"] // LANE)),
                  pl.BlockSpec((LANE, wk), lambda g, i: (0, g)),
                  pl.BlockSpec((1, wk), lambda g, i: (0, g)),
                  pl.BlockSpec((1, dv), lambda g, i: (0, 0))]
    else:
        specs += [pl.BlockSpec((LB, dk), lambda g, i: (blk(i), 0)),
                  pl.BlockSpec((LB, dk), lambda g, i: (blk(i), 0)),
                  pl.BlockSpec((HPS, 1, dk), lambda g, i: (g, 0, 0))]
    return specs, blk


def _lin_aux(gla, refs, rows, hh, dk):
    if gla:
        glr_ref, wlr_ref, blr_ref, gn_ref = refs
        kcols = pl.ds(hh * dk, dk)
        return (glr_ref[rows, :], wlr_ref[:, kcols], blr_ref[:, kcols]), gn_ref[...]
    cs_ref, sn_ref, lg_ref = refs
    return (cs_ref[rows, :], sn_ref[rows, :], lg_ref[hh]), None


def _lin_fwd(gla, p, aux_arrays, d, off):
    t = p.shape[0]
    dk, dv = d // (2 * LIN_HEADS), d // LIN_HEADS
    nb, cb = t // LB, LB // CHUNK
    naux = len(aux_arrays)

    def body(*refs):
        q_ref, k_ref, v_ref, g_ref = refs[:4]
        aux_refs = refs[4:4 + naux]
        o_ref, bo_ref, st_out_ref, st_ref = refs[4 + naux:]

        @pl.when(pl.program_id(1) == 0)
        def _():
            st_ref[...] = jnp.zeros_like(st_ref)

        st = [st_ref[hh] for hh in range(HPS)]
        for c in range(cb):
            rows = pl.ds(c * CHUNK, CHUNK)
            for hh in range(HPS):
                kcols, vcols = pl.ds(hh * dk, dk), pl.ds(hh * dv, dv)
                aux, gn = _lin_aux(gla, aux_refs, rows, hh, dk)
                q, k, b, _ = _lin_prep(gla, q_ref[rows, kcols], k_ref[rows, kcols], aux)
                st_out_ref[hh, c] = st[hh]
                o, st[hh], _ = _lin_chunk(q, k, v_ref[rows, vcols], b, st[hh])
                o_ref[rows, vcols] = o
                out, _ = _lin_norm_gate(gla, o, g_ref[rows, vcols], gn)
                bo_ref[rows, vcols] = out.astype(BF16)
        for hh in range(HPS):
            st_ref[hh] = st[hh]

    specs, _ = _lin_specs(gla, dk, dv, off, False, nb)
    orow = pl.BlockSpec((LB, HPS * dv), lambda g, i: (i, g))
    return pl.pallas_call(
        body, name="gla_fwd" if gla else "ret_fwd", grid=(LIN_HEADS // HPS, nb), in_specs=specs,
        out_specs=[orow, orow, pl.BlockSpec((HPS, cb, dv, dk), lambda g, i: (g, i, 0, 0))],
        out_shape=[_sds((t, d), F32), _sds((t, d), BF16), _sds((LIN_HEADS, t // CHUNK, dv, dk), F32)],
        scratch_shapes=[pltpu.VMEM((HPS, dv, dk), F32)],
        compiler_params=_cp(("parallel", "arbitrary")))(p, p, p, p, *aux_arrays)


def _lin_bwd(gla, p, aux_arrays, o, states, dbo, d, off):
    t = p.shape[0]
    dk, dv = d // (2 * LIN_HEADS), d // LIN_HEADS
    nb, cb = t // LB, LB // CHUNK
    naux = len(aux_arrays)

    def body(*refs):
        q_ref, k_ref, v_ref, g_ref = refs[:4]
        aux_refs = refs[4:4 + naux]
        o_ref, st_in_ref, dbo_ref = refs[4 + naux:7 + naux]
        outs = refs[7 + naux:]
        dq_ref, dk_ref, dv_ref, dg_ref = outs[:4]
        dst_ref = outs[-1]
        first = pl.program_id(1) == 0

        @pl.when(first)
        def _():
            dst_ref[...] = jnp.zeros_like(dst_ref)

        if gla:
            dpre_ref, dblr_ref, dgn_ref = outs[4:7]

            @pl.when(first)
            def _():
                dblr_ref[...] = jnp.zeros_like(dblr_ref)
                dgn_ref[...] = jnp.zeros_like(dgn_ref)

        dst = [dst_ref[hh] for hh in range(HPS)]
        for c in reversed(range(cb)):
            rows = pl.ds(c * CHUNK, CHUNK)
            for hh in range(HPS):
                kcols, vcols = pl.ds(hh * dk, dk), pl.ds(hh * dv, dv)
                aux, gn = _lin_aux(gla, aux_refs, rows, hh, dk)
                q, k, b, pre = _lin_prep(gla, q_ref[rows, kcols], k_ref[rows, kcols], aux)
                v = v_ref[rows, vcols]
                st = st_in_ref[hh, c]
                _, _, (eb, enb, last, qf, kf, qb, kb, kl, s) = _lin_chunk(q, k, v, b, st)
                gate = g_ref[rows, vcols]
                dout = dbo_ref[rows, vcols]
                _, (sg, silu, r, hn) = _lin_norm_gate(gla, o_ref[rows, vcols], gate, gn)
                dsilu = sg * (1.0 + gate * (1.0 - sg))
                if gla:
                    y = hn * gn
                    dy = dout * silu
                    dg_ref[rows, vcols] = (dout * y * dsilu).astype(BF16)
                    dgn_ref[hh] += jnp.sum(dy * hn, axis=0, keepdims=True)
                    dhn = dy * gn
                    do = r * (dhn - hn * jnp.mean(dhn * hn, axis=-1, keepdims=True))
                else:
                    dhn = dout * silu
                    dg_ref[rows, vcols] = (dout * hn * dsilu).astype(BF16)
                    do = r * (dhn - jnp.mean(dhn, axis=-1, keepdims=True)
                              - hn * jnp.mean(dhn * hn, axis=-1, keepdims=True))
                dstn = dst[hh]
                dec = jnp.exp(last)
                ds = _dot_nt(do, v)
                low = _tri(True)
                dsf = jnp.where(low, ds, 0.0)
                dsb = jnp.where(low, 0.0, ds)
                dvv = _dot_tn(s, do) + _dot_nt(kl, dstn)
                dqf = _dot(dsf, kf) + _dot(do, st)
                dkf = _dot_tn(dsf, qf)
                dqb = _dot(dsb, kb)
                dkb = _dot_tn(dsb, qb)
                dkl = _dot(v, dstn)
                dst[hh] = dstn * dec + _dot_tn(do, qf)
                dq = dqf * eb + dqb * enb
                dkk = dkf * enb + dkb * eb + dkl * jnp.exp(last - b)
                dv_ref[rows, vcols] = dvv.astype(BF16)
                if gla:
                    ddec = jnp.sum(dstn * st, axis=0, keepdims=True)
                    db = dqf * qf - dkf * kf - dqb * qb + dkb * kb - dkl * kl
                    dlast = jnp.sum(dkl * kl, axis=0, keepdims=True) + ddec * dec
                    rowi = lax.broadcasted_iota(jnp.int32, db.shape, 0)
                    db = db + jnp.where(rowi == CHUNK - 1, dlast, 0.0)
                    dlog_a = _dot_hi(_tri(False).astype(F32), db)
                    dpre = dlog_a * (1.0 / GATE_NORM) * (1.0 - _sigmoid(pre))
                    dpre_ref[rows, kcols] = dpre
                    dblr_ref[hh] += jnp.sum(dpre, axis=0, keepdims=True)
                    dq_ref[rows, kcols] = (dq * (dk ** -0.5)).astype(BF16)
                    dk_ref[rows, kcols] = dkk.astype(BF16)
                else:
                    cs, sn, _ = aux
                    half = dk // 2
                    dkk = dkk * (dk ** -0.5)
                    dq_ref[rows, kcols] = (dq * cs + pltpu.roll(dq * sn, half, 1)).astype(BF16)
                    dk_ref[rows, kcols] = (dkk * cs + pltpu.roll(dkk * sn, half, 1)).astype(BF16)
        for hh in range(HPS):
            dst_ref[hh] = dst[hh]

    specs, blk = _lin_specs(gla, dk, dv, off, True, nb)
    vrow = pl.BlockSpec((LB, HPS * dv), lambda g, i: (blk(i), g))
    krow = pl.BlockSpec((LB, HPS * dk), lambda g, i: (blk(i), g))
    specs += [vrow, pl.BlockSpec((HPS, cb, dv, dk), lambda g, i: (g, blk(i), 0, 0)), vrow]
    out_specs = [krow, krow, vrow, vrow]
    out_shape = [_sds((t, d // 2), BF16), _sds((t, d // 2), BF16), _sds((t, d), BF16), _sds((t, d), BF16)]
    if gla:
        out_specs += [krow, pl.BlockSpec((HPS, 1, dk), lambda g, i: (g, 0, 0)),
                      pl.BlockSpec((HPS, 1, dv), lambda g, i: (g, 0, 0))]
        out_shape += [_sds((t, d // 2), F32), _sds((LIN_HEADS, 1, dk), F32), _sds((LIN_HEADS, 1, dv), F32)]
    out_specs.append(pl.BlockSpec((HPS, dv, dk), lambda g, i: (g, 0, 0)))
    out_shape.append(_sds((LIN_HEADS, dv, dk), F32))
    res = pl.pallas_call(
        body, name="gla_bwd" if gla else "ret_bwd", grid=(LIN_HEADS // HPS, nb), in_specs=specs,
        out_specs=out_specs, out_shape=out_shape,
        compiler_params=_cp(("parallel", "arbitrary")))(p, p, p, p, *aux_arrays, o, states, dbo)
    return res[:-1]


def _row_tile(rows, cols):
    cap = max(8, (2 << 20) // (4 * cols))
    t = rows
    while t > cap and t % 2 == 0:
        t //= 2
    return t


def _add_half(name, g, t, sel):
    nchip, hr, cols = t.shape
    tr = _row_tile(hr, cols)
    nb = hr // tr

    def body(sel_ref, g_ref, t_ref, o_ref):
        o_ref[...] = g_ref[...] + t_ref[...]

    half = pl.BlockSpec((None, tr, cols), lambda p, i, s: (p, i, 0))
    gs = pltpu.PrefetchScalarGridSpec(
        num_scalar_prefetch=1, grid=(nchip, nb),
        in_specs=[pl.BlockSpec((None, tr, cols), lambda p, i, s: (p, s[0] * nb + i, 0)), half], out_specs=half)
    return pl.pallas_call(body, name=name, grid_spec=gs, out_shape=_sds(t.shape, F32),
                          compiler_params=_cp(("parallel", "parallel")))(sel, g, t)


def _sum_shards(name, h, rcv, sel):
    _, rows, cols = h.shape
    tr = _row_tile(rows, cols)

    def body(sel_ref, h_ref, r0, r1, r2, o_ref):
        o_ref[...] = ((h_ref[...] + r0[...]) + r1[...]) + r2[...]

    rspecs = [pl.BlockSpec((None, tr, cols), functools.partial(lambda i, s, j: (j, i, 0), j=j)) for j in range(3)]
    gs = pltpu.PrefetchScalarGridSpec(
        num_scalar_prefetch=1, grid=(rows // tr,),
        in_specs=[pl.BlockSpec((None, tr, cols), lambda i, s: (s[0], i, 0))] + rspecs,
        out_specs=pl.BlockSpec((tr, cols), lambda i, s: (i, 0)))
    return pl.pallas_call(body, name=name, grid_spec=gs, out_shape=_sds((rows, cols), F32),
                          compiler_params=_cp(("parallel",)))(sel, h, rcv, rcv, rcv)


def _adamw_math(w, g, m, v):
    c1 = 1.0 - ADAM_B1 ** ADAM_STEP
    c2 = 1.0 - ADAM_B2 ** ADAM_STEP
    nm = ADAM_B1 * m + (1.0 - ADAM_B1) * g
    nv = ADAM_B2 * v + (1.0 - ADAM_B2) * jnp.square(g)
    return -ADAM_LR * ((nm / c1) / (jnp.sqrt(nv / c2) + ADAM_EPS) + ADAM_WD * w), nm, nv


def _adamw(name, w, g, m, v):
    rows, cols = w.shape
    tr = _row_tile(rows, cols)

    def body(w_ref, g_ref, m_ref, v_ref, d_ref, nm_ref, nv_ref):
        d_ref[...], nm_ref[...], nv_ref[...] = _adamw_math(w_ref[...], g_ref[...], m_ref[...], v_ref[...])

    spec = pl.BlockSpec((tr, cols), lambda i: (i, 0))
    return pl.pallas_call(body, name=name, grid=(rows // tr,), in_specs=[spec] * 4, out_specs=[spec] * 3,
                          out_shape=[_sds((rows, cols), F32)] * 3, compiler_params=_cp(("parallel",)))(w, g, m, v)


def _adamw_layer(name, w, g_own, g_sib, sel, m, v, layer, prev):
    depth, rows, cols = w.shape
    tr = _row_tile(rows // 2, cols)
    nbh = rows // 2 // tr
    nprev = 0 if prev is None else 4

    def body(sel_ref, w_ref, own_ref, sib_ref, m_ref, v_ref, *rest):
        go_ref, d_ref, nm_ref, nv_ref = rest[nprev:]
        gv = jnp.where(pl.program_id(0) // nbh == sel_ref[0], own_ref[...], sib_ref[...])
        go_ref[...] = gv
        d_ref[...], nm_ref[...], nv_ref[...] = _adamw_math(w_ref[...], gv, m_ref[...], v_ref[...])

    lay = pl.BlockSpec((None, tr, cols), lambda i, s: (layer, i, 0))
    hlf = pl.BlockSpec((tr, cols), lambda i, s: (i % nbh, 0))
    gs = pltpu.PrefetchScalarGridSpec(
        num_scalar_prefetch=1, grid=(2 * nbh,), in_specs=[lay, hlf, hlf, lay, lay] + [ANY] * nprev,
        out_specs=[lay] * 4)
    args = (sel, w, g_own, g_sib, m, v) + (() if prev is None else tuple(prev))
    return pl.pallas_call(
        body, name=name, grid_spec=gs, out_shape=[_sds((depth, rows, cols), F32)] * 4,
        input_output_aliases={6 + k: k for k in range(nprev)},
        compiler_params=_cp(("parallel",)))(*args)


def _place():
    x, y, c = (lax.axis_index(a) for a in MESH_AXES)
    chips = [(1 - x, y), (x, 1 - y), (1 - x, 1 - y)]
    return x, y, c, chips


def _chip_index(xy):
    return 2 * xy[0] + xy[1]


ANY = pl.BlockSpec(memory_space=pl.ANY)


HBM_SPEC = pl.BlockSpec(memory_space=pltpu.HBM)
SEM = pl.BlockSpec(memory_space=pltpu.SEMAPHORE)
EFFECT = pltpu.SideEffectType.DATAFLOW_SIDE_EFFECTING


def _half(ref, c):
    hr = ref.shape[-2] // 2
    return pl.ds(pl.multiple_of(c * hr, 16), hr)


def _gather_copies(srcs, lands, send, recv):
    x, y, c, chips = _place()
    me = _chip_index((x, y))
    return [pltpu.make_async_remote_copy(src_ref=s.at[_half(s, c)], dst_ref=g.at[me, _half(s, c)],
                                         send_sem=send.at[3 * a + j], recv_sem=recv.at[3 * a + j],
                                         device_id=(*ch, c), device_id_type=DEV)
            for a, (s, g) in enumerate(zip(srcs, lands)) for j, ch in enumerate(chips)]


def _scatter_copies(srcs, lands, send, recv):
    x, y, c, chips = _place()
    return [pltpu.make_async_remote_copy(src_ref=h.at[_chip_index(ch)], dst_ref=r.at[j],
                                         send_sem=send.at[3 * a + j], recv_sem=recv.at[3 * a + j],
                                         device_id=(*ch, c), device_id_type=DEV)
            for a, (h, r) in enumerate(zip(srcs, lands)) for j, ch in enumerate(chips)]


def _in_hbm(a):
    return pltpu.with_memory_space_constraint(a, pltpu.HBM)


def _split_start(name, srcs, land_shapes, copies_fn, after=None):
    ns, nl = len(srcs), len(land_shapes)
    ncp = 3 * ns
    lands = [lax.empty(s.shape, s.dtype) for s in land_shapes]
    behind = [] if after is None else [after]

    def body(*refs):
        src, land = refs[:ns], refs[ns:ns + nl]
        send, recv = refs[ns + nl + len(behind)], refs[ns + nl + len(behind) + 1]
        for cp in copies_fn(src, land, send, recv):
            cp.start()
        refs[-1][...] = jnp.zeros_like(refs[-1])

    bufs = list(srcs) + lands
    outs = pl.pallas_call(
        body, name=name, in_specs=[HBM_SPEC] * (ns + nl) + [ANY] * len(behind),
        out_specs=[SEM, SEM] + [HBM_SPEC] * (ns + nl) + [pl.BlockSpec(memory_space=pltpu.VMEM)],
        out_shape=[pltpu.SemaphoreType.DMA((ncp,)), pltpu.SemaphoreType.DMA((ncp,))]
        + [pltpu.HBM(b.shape, b.dtype) for b in bufs] + [_sds((8, LANE), F32)],
        input_output_aliases={i: 2 + i for i in range(ns + nl)},
        compiler_params=pltpu.CompilerParams(has_side_effects=EFFECT))(*[_in_hbm(b) for b in bufs], *behind)
    return outs[0], outs[1], list(outs[2:2 + ns]), list(outs[2 + ns:2 + ns + nl]), outs[-1]


def _split_wait(name, started, copies_fn, after):
    send, recv, srcs, lands, _ = started
    ns, nl = len(srcs), len(lands)

    def body(*refs):
        src, land = refs[:ns], refs[ns:ns + nl]
        for cp in copies_fn(src, land, refs[ns + nl], refs[ns + nl + 1]):
            cp.wait_send()
            cp.wait_recv()

    bufs = list(srcs) + list(lands)
    outs = pl.pallas_call(
        body, name=name, in_specs=[HBM_SPEC] * (ns + nl) + [SEM, SEM, ANY], out_specs=[HBM_SPEC] * (ns + nl),
        out_shape=[pltpu.HBM(b.shape, b.dtype) for b in bufs],
        input_output_aliases={i: i for i in range(ns + nl)},
        compiler_params=pltpu.CompilerParams(has_side_effects=EFFECT))(*bufs, send, recv, after)
    return list(outs[:ns]), list(outs[ns:])


def _gather_plain(name, srcs):
    n = len(srcs)

    def body(*refs):
        src, land = refs[:n], refs[n:2 * n]
        send, recv, fsend, frecv = refs[2 * n:]
        first = _gather_copies(src, land, send, recv)
        for cp in first:
            cp.start()
        _forward_body(land, first, fsend, frecv)

    return pl.pallas_call(
        body, name=name, in_specs=[ANY] * n, out_specs=[ANY] * n,
        out_shape=[_sds((4,) + s.shape, s.dtype) for s in srcs],
        scratch_shapes=[pltpu.SemaphoreType.DMA((3 * n,))] * 4)(*srcs)


def _forward_body(land, arrivals, fsend, frecv):
    x, y, c, chips = _place()
    n = len(land)
    passed = []
    for a in range(n):
        for j, ch in enumerate(chips):
            if arrivals is not None:
                arrivals[3 * a + j].wait_recv()
            slot = land[a].at[_chip_index(ch), _half(land[a], c)]
            fw = pltpu.make_async_remote_copy(src_ref=slot, dst_ref=slot, send_sem=fsend.at[3 * a + j],
                                              recv_sem=frecv.at[3 * a + j], device_id=(x, y, 1 - c),
                                              device_id_type=DEV)
            fw.start()
            passed.append(fw)
    for a in range(n):
        for j, ch in enumerate(chips):
            slot = land[a].at[_chip_index(ch), _half(land[a], 1 - c)]
            pltpu.make_async_remote_copy(src_ref=slot, dst_ref=slot, send_sem=fsend.at[3 * a + j],
                                         recv_sem=frecv.at[3 * a + j], device_id=(x, y, c),
                                         device_id_type=DEV).wait_recv()
    for cp in passed:
        cp.wait_send()
    if arrivals is not None:
        for cp in arrivals:
            cp.wait_send()


def _gather_forward(name, lands):
    n = len(lands)

    def body(*refs):
        _forward_body(refs[n:2 * n], None, refs[2 * n], refs[2 * n + 1])

    return pl.pallas_call(
        body, name=name, in_specs=[ANY] * n, out_specs=[ANY] * n,
        out_shape=[_sds(g.shape, g.dtype) for g in lands], input_output_aliases={a: a for a in range(n)},
        scratch_shapes=[pltpu.SemaphoreType.DMA((3 * n,))] * 2)(*lands)


def _sibling_halves(name, grs):
    n = len(grs)

    def body(*refs):
        ins, outs = refs[:n], refs[n:2 * n]
        send, recv = refs[2 * n:]
        x, y, c, _ = _place()
        cps = [pltpu.make_async_remote_copy(src_ref=ins[a].at[:, _half(ins[a], 1 - c)], dst_ref=outs[a],
                                            send_sem=send.at[a], recv_sem=recv.at[a], device_id=(x, y, 1 - c),
                                            device_id_type=DEV) for a in range(n)]
        for cp in cps:
            cp.start()
        for cp in cps:
            cp.wait()

    return pl.pallas_call(
        body, name=name, in_specs=[ANY] * n, out_specs=[ANY] * n,
        out_shape=[_sds((g.shape[0], g.shape[1] // 2, g.shape[2]), F32) for g in grs],
        scratch_shapes=[pltpu.SemaphoreType.DMA((n,)), pltpu.SemaphoreType.DMA((n,))])(*grs)


def _sibling_share(name, sms):
    n = len(sms)

    def body(*refs):
        ins, outs = refs[:n], refs[n:2 * n]
        send, recv = refs[2 * n:]
        x, y, c, _ = _place()
        cps = [pltpu.make_async_remote_copy(src_ref=ins[a], dst_ref=outs[a], send_sem=send.at[a],
                                            recv_sem=recv.at[a], device_id=(x, y, 1 - c), device_id_type=DEV)
               for a in range(n)]
        for cp in cps:
            cp.start()
        for cp in cps:
            cp.wait()

    return pl.pallas_call(
        body, name=name, in_specs=[ANY] * n, out_specs=[ANY] * n, out_shape=[_sds(s.shape, F32) for s in sms],
        scratch_shapes=[pltpu.SemaphoreType.DMA((n,))] * 2)(*sms)


def _small_allreduce(v, after=None):
    rows = v.shape[0]
    ndev = 8
    behind = [] if after is None else [after]

    def body(v_ref, *rest):
        o_ref, gat_ref, send, recv = rest[len(behind):]
        x, y, c, _ = _place()
        me = 4 * x + 2 * y + c
        cps = []
        for k in range(1, ndev):
            to = (me + k) % ndev
            cp = pltpu.make_async_remote_copy(src_ref=v_ref, dst_ref=gat_ref.at[me], send_sem=send.at[k - 1],
                                              recv_sem=recv.at[me], device_id=(to // 4, (to // 2) % 2, to % 2),
                                              device_id_type=DEV)
            cp.start()
            cps.append(cp)
        gat_ref[me] = v_ref[...]
        for k in range(1, ndev):
            frm = (me + k) % ndev
            pltpu.make_async_remote_copy(src_ref=v_ref, dst_ref=gat_ref.at[frm], send_sem=send.at[k - 1],
                                         recv_sem=recv.at[frm], device_id=(x, y, c), device_id_type=DEV).wait_recv()
        for cp in cps:
            cp.wait_send()
        acc = gat_ref[0]
        for k in range(1, ndev):
            acc = acc + gat_ref[k]
        o_ref[...] = acc

    vm = pl.BlockSpec(memory_space=pltpu.VMEM)
    return pl.pallas_call(
        body, name="small_allreduce", in_specs=[vm] + [ANY] * len(behind), out_specs=vm,
        out_shape=_sds((rows, LANE), F32),
        scratch_shapes=[pltpu.VMEM((ndev, rows, LANE), F32), pltpu.SemaphoreType.DMA((ndev - 1,)),
                        pltpu.SemaphoreType.DMA((ndev,))])(v, *behind)


def _layout(d):
    half = d // 2
    names = [("aq", d), ("ak", d), ("av", d), ("rq", half), ("rk", half), ("rv", d), ("rg", d),
             ("gq", half), ("gk", half), ("gv", d), ("gg", d), ("gates", 3 * d), ("glr", 2 * LANE)]
    off, pos = {}, 0
    for nm, sz in names:
        off[nm] = pos
        pos += sz
    return off, pos


def _pad_cols(w, d):
    a = 8 * d + d
    lr = w[..., a:a + GATE_RANK]
    z = jnp.zeros(w.shape[:-1] + (2 * LANE - GATE_RANK,), w.dtype)
    return jnp.concatenate([w[..., :a], w[..., a + GATE_RANK:], lr, z], axis=-1)


def _unpad_cols(g, d):
    a = 8 * d + d
    return jnp.concatenate([g[..., :a], g[..., a + 3 * d:a + 3 * d + GATE_RANK], g[..., a:a + 3 * d]], axis=-1)


def kernel(x, ln_in_g, ln_in_b, w_in, rel_bias, gla_w_lr, gla_b_lr, gla_norm_g, w_branch, w_out, ln1_g, ln1_b, w_up, w_down, ln2_g, ln2_b, loss_target, m_ln_in_g, m_ln_in_b, m_w_in, m_rel_bias, m_gla_w_lr, m_gla_b_lr, m_gla_norm_g, m_w_branch, m_w_out, m_ln1_g, m_ln1_b, m_w_up, m_w_down, m_ln2_g, m_ln2_b, v_ln_in_g, v_ln_in_b, v_w_in, v_rel_bias, v_gla_w_lr, v_gla_b_lr, v_gla_norm_g, v_w_branch, v_w_out, v_ln1_g, v_ln1_b, v_w_up, v_w_down, v_ln2_g, v_ln2_b):
    t, d = x.shape[1], x.shape[2]
    dff = 4 * d
    half = d // 2
    off, npad = _layout(d)
    xi, yi, ci = (lax.axis_index(a) for a in MESH_AXES)
    chip = 2 * xi + yi
    csel = jnp.reshape(ci, (1,)).astype(jnp.int32)
    psel = jnp.reshape(chip, (1,)).astype(jnp.int32)

    big_w = [w_in, w_branch.reshape(DEPTH, -1, d), w_out, w_up, w_down]
    big_m = [m_w_in, m_w_branch.reshape(DEPTH, -1, d), m_w_out, m_w_up, m_w_down]
    big_v = [v_w_in, v_w_branch.reshape(DEPTH, -1, d), v_w_out, v_w_up, v_w_down]
    W_IN, REST = [0], [1, 2, 3, 4]

    def shards_of(l, idx):
        return [big_w[i][l].astype(BF16) for i in idx]

    def lands_of(srcs):
        return [_sds((4,) + s.shape, s.dtype) for s in srcs]

    def full_w_in(g):
        return _pad_cols(jnp.transpose(g, (1, 0, 2)).reshape(d, -1), d)

    def full_rest(gs):
        g_br, g_out, g_up, g_down = gs
        return (jnp.transpose(g_br.reshape(4, N_BRANCH, d // 4, d), (1, 0, 2, 3)).reshape(N_BRANCH, d, d),
                g_out.reshape(d, d), jnp.transpose(g_up, (1, 0, 2)).reshape(d, dff), g_down.reshape(dff, d))

    def with_own(srcs, lands):
        return [lax.dynamic_update_slice(g, s[None], (chip, 0, 0)) for s, g in zip(srcs, lands)]

    def gather_start(tag, l, idx, after):
        srcs = shards_of(l, idx)
        return srcs, _split_start(f"gather_{tag}{l}_start", srcs, lands_of(srcs), _gather_copies, after)

    def gather_finish(tag, l, pending, after):
        srcs, started = pending
        _, lands = _split_wait(f"gather_{tag}{l}_wait", started, _gather_copies, after)
        return with_own(srcs, _gather_forward(f"gather_{tag}{l}_pass", lands))

    def token(pending):
        return pending[1][4][0, 0]

    win, wbr, wout, wup, wdown = ([None] * DEPTH for _ in range(5))
    src_first = shards_of(0, W_IN)
    g_first = with_own(src_first, _gather_plain("gather_in0", src_first))
    win[0] = full_w_in(g_first[0])

    dkh = half // LIN_HEADS
    lr_rows = DEPTH * GATE_RANK
    lr_slab = jnp.zeros((lr_rows, 4, half // 4), F32)
    lr_slab = lax.dynamic_update_slice(lr_slab, (gla_w_lr.reshape(lr_rows, 1, half // 4) * jnp.where(ci == 0, 1.0, 0.0)),
                                       (0, chip, 0))
    wlr_full = _small_allreduce(lr_slab.reshape(-1, LANE)).reshape(DEPTH, GATE_RANK, half)
    wlr_pad = jnp.concatenate([wlr_full, jnp.zeros((DEPTH, LANE - GATE_RANK, half), F32)], axis=1)
    pend_rest = gather_start("rest", 0, REST, wlr_full[0, :1, :1] + g_first[0][0, :1, :1].astype(F32))

    inv = 10000.0 ** (-jnp.arange(0, dkh, 2, dtype=F32) / dkh)
    ang = jnp.arange(t, dtype=F32)[:, None] * inv[None, :]
    cos, sin = jnp.cos(ang), jnp.sin(ang)
    rope_c = jnp.concatenate([cos, cos], axis=1)
    rope_s = jnp.concatenate([-sin, sin], axis=1)
    log_gamma = jnp.log1p(-jnp.exp2(-5.0 - jnp.arange(LIN_HEADS, dtype=F32)))
    lg_tab = jnp.broadcast_to(log_gamma[:, None, None], (LIN_HEADS, 1, dkh))

    def vec(a):
        return a.reshape(1, -1)

    x0, x0b, xh_in, rs_in = _ln_in(x[0], vec(ln_in_g) + token(pend_rest), vec(ln_in_b))
    saved = []
    xl, xlb = x0, x0b
    for l in range(DEPTH):
        p = _mm("proj_in", xlb, win[l], 512, 1792)
        g_rest = gather_finish("rest", l, pend_rest, p)
        wbr[l], wout[l], wup[l], wdown[l] = full_rest(g_rest)
        tok = 0.0
        if l + 1 < DEPTH:
            pend_in = gather_start("in", l + 1, W_IN, g_rest[0])
            tok = token(pend_in)
        bias = _bias_expand(rel_bias[l] + tok)
        attn = _attn_fwd(p, bias, d, off)
        ret_aux = (rope_c, rope_s, lg_tab + tok)
        gla_aux = (p, wlr_pad[l], vec(gla_b_lr[l]) + tok, vec(gla_norm_g[l]))
        o_ret, b_ret, st_ret = _lin_fwd(False, p, ret_aux, d, off)
        o_gla, b_gla, st_gla = _lin_fwd(True, p, gla_aux, d, off)
        tok = 0.0
        if l + 1 < DEPTH:
            g_in = gather_finish("in", l + 1, pend_in, b_gla)
            win[l + 1] = full_w_in(g_in[0])
            pend_rest = gather_start("rest", l + 1, REST, g_in[0])
            tok = token(pend_rest)
        bo = jnp.stack([attn, b_ret, b_gla])
        proj, merged = _merge_fwd(bo, wbr[l], p, off["gates"])
        x1, x1b, xh1, rs1 = _mm_res_ln("out_proj_ln", merged, wout[l], xl, vec(ln1_g[l]) + tok, vec(ln1_b[l]),
                                       256, False)
        u = _mm("mlp_up", x1b, wup[l], 1024, 1024)
        x2, x2b, xh2, rs2, act = _mm_res_ln("mlp_down_ln", u, wdown[l], x1, vec(ln2_g[l]), vec(ln2_b[l]), 256, True)
        saved.append(dict(xlb=xlb, p=p, bias=bias, ret_aux=ret_aux, gla_aux=gla_aux, o_ret=o_ret, o_gla=o_gla,
                          st_ret=st_ret, st_gla=st_gla, bo=bo, proj=proj, merged=merged, x1b=x1b, xh1=xh1,
                          rs1=rs1, u=u, xh2=xh2, rs2=rs2, act=act))
        xl, xlb = x2, x2b

    small = {}
    last = saved[-1]
    loss_p, dz2, dz2b, dg, db = _loss_ln_bwd(xl, loss_target[0], last["xh2"], last["rs2"], vec(ln2_g[DEPTH - 1]))
    small["loss"] = loss_p[:, :1]
    grad_x = None

    def scatter_start(tag, l, idx, shards, after):
        theirs = _sibling_halves(f"grad_{tag}{l}_sibling", shards)
        hs = [_add_half("grad_sibling_add", g, th, csel) for g, th in zip(shards, theirs)]
        lands = [_sds((3,) + h.shape[1:], F32) for h in hs]
        return tag, l, idx, _split_start(f"grad_{tag}{l}_scatter_start", hs, lands, _scatter_copies, after)

    adam_out = [None] * len(big_w)

    def scatter_finish(pending, after):
        tag, l, idx, started = pending
        hs, rcv = _split_wait(f"grad_{tag}{l}_scatter_wait", started, _scatter_copies, after)
        sms = [_sum_shards("grad_chip_sum", h, r, psel) for h, r in zip(hs, rcv)]
        for i, own, sib in zip(idx, sms, _sibling_share(f"grad_{tag}{l}_share", sms)):
            adam_out[i] = _adamw_layer("adamw_large", big_w[i], own, sib, csel, big_m[i], big_v[i], l, adam_out[i])
        return adam_out[idx[0]][0]

    in_flight = []

    def scatter(tag, l, idx, shards, after=None):
        pending = scatter_start(tag, l, idx, shards, after)
        in_flight.append(pending)
        if len(in_flight) > 3:
            scatter_finish(in_flight.pop(0), pending[3][4])
        return pending[3][4][0, 0]

    for l in reversed(range(DEPTH)):
        s = saved[l]
        small[("ln2_g", l)], small[("ln2_b", l)] = dg, db
        du = _mm_nt_relu2_bwd(dz2b, wdown[l], s["u"])
        g_wdown = _mm_tn("grad_w_down", s["act"], dz2b, 512, 512)
        g_wup = _mm_tn("grad_w_up", s["x1b"], du, 512, 512)
        dz1, dz1b, dg1, db1 = _mm_nt_res_lnbwd("mlp_up_bwd_ln", du, wup[l], dz2, s["xh1"], s["rs1"],
                                               vec(ln1_g[l]), 256, dff)
        small[("ln1_g", l)], small[("ln1_b", l)] = dg1, db1
        dproj, dgl = _merge_bwd(dz1b, wout[l], s["proj"], s["p"], off["gates"])
        g_wout = _mm_tn("grad_w_out", s["merged"], dz1b, 512, 512)
        dbo = _mm("branch_proj_bwd", dproj, wbr[l], 1024, 1024, nt=True)
        g_wbr = _mm_tn("grad_w_branch", s["bo"], dproj, 512, 512)
        tok = scatter("rest", l, REST, [
            jnp.transpose(g_wbr.reshape(N_BRANCH, 4, d // 4, d), (1, 0, 2, 3)).reshape(4, -1, d),
            g_wout.reshape(4, d // 4, d), jnp.transpose(g_wup.reshape(d, 4, d), (1, 0, 2)), g_wdown.reshape(4, d, d)])
        rc, rs_, lg = s["ret_aux"]
        gp, gw, gb, gn_ = s["gla_aux"]
        dq_a, dk_acc, dv_acc, dbias = _attn_bwd(s["p"], s["bias"] + tok, dbo[0], d, off)
        small[("rel_bias", l)] = _bias_reduce(dbias)
        dk_a = dk_acc[2 * QB:].astype(BF16)
        dv_a = dv_acc[2 * QB:].astype(BF16)
        dq_r, dk_r, dv_r, dg_r = _lin_bwd(False, s["p"], (rc, rs_, lg + tok), s["o_ret"], s["st_ret"], dbo[1], d, off)
        dq_g, dk_g, dv_g, dg_g, dpre, dblr, dgn = _lin_bwd(True, s["p"], (gp, gw, gb + tok, gn_), s["o_gla"],
                                                           s["st_gla"], dbo[2], d, off)
        small[("gla_b_lr", l)] = dblr.reshape(1, half)
        small[("gla_norm_g", l)] = jnp.sum(dgn, axis=0)
        dpre_b = dpre.astype(BF16)
        glr_b = s["p"][:, off["glr"]:off["glr"] + LANE].astype(BF16)
        dglr = _mm("gate_lr_bwd", dpre_b, wlr_pad[l], 512, LANE, nt=True, out_dtype=BF16)
        small[("gla_w_lr", l)] = _mm_tn("grad_gla_w_lr", glr_b, dpre_b, LANE, half)[:GATE_RANK]
        dp = jnp.concatenate([dq_a, dk_a, dv_a, dq_r, dk_r, dv_r, dg_r, dq_g, dk_g, dv_g, dg_g,
                              dgl[0], dgl[1], dgl[2], dglr, jnp.zeros((t, LANE), BF16)], axis=1)
        if l > 0:
            prev = saved[l - 1]
            xh_p, rs_p, g_p = prev["xh2"], prev["rs2"], vec(ln2_g[l - 1])
        else:
            xh_p, rs_p, g_p = xh_in, rs_in, vec(ln_in_g)
        g_win = _mm_tn("grad_w_in", s["xlb"], dp, 1024, 896)
        tok = scatter("in", l, W_IN, [jnp.transpose(_unpad_cols(g_win, d).reshape(d, 4, -1), (1, 0, 2))])
        dzp, dzpb, dg, db = _mm_nt_res_lnbwd("proj_in_bwd_ln", dp, win[l], dz1, xh_p, rs_p, g_p + tok, 1024, 1792)
        dz2, dz2b = dzp, dzpb
        grad_x = dzp
    after = grad_x
    while in_flight:
        after = scatter_finish(in_flight.pop(0), after)
    small["ln_in_g"], small["ln_in_b"] = dg, db
    rb_pad = 3 * LANE
    pieces = [small["loss"].reshape(-1), jnp.zeros((LANE - 1,), F32), small["ln_in_g"].reshape(-1),
              small["ln_in_b"].reshape(-1)]
    for l in range(DEPTH):
        rb = jnp.pad(small[("rel_bias", l)], ((0, 0), (0, rb_pad - (2 * REL_CLIP + 1))))
        pieces += [rb.reshape(-1), small[("gla_w_lr", l)].reshape(-1), small[("gla_b_lr", l)].reshape(-1),
                   small[("gla_norm_g", l)].reshape(-1), small[("ln1_g", l)].reshape(-1),
                   small[("ln1_b", l)].reshape(-1), small[("ln2_g", l)].reshape(-1), small[("ln2_b", l)].reshape(-1)]
    sizes = [pc.shape[0] for pc in pieces]
    packed = jnp.concatenate(pieces)
    padn = (-packed.shape[0]) % (8 * LANE)
    packed = jnp.concatenate([packed, jnp.zeros((padn,), F32)]).reshape(-1, LANE)
    red = _small_allreduce(packed, after).reshape(-1)

    parts, pos = [], 0
    for sz in sizes:
        parts.append(red[pos:pos + sz])
        pos += sz
    loss = parts[0][0]
    g_ln_in_g, g_ln_in_b = parts[2], parts[3]
    per = 8
    g_rel = jnp.stack([parts[4 + per * l].reshape(ATTN_HEADS, rb_pad)[:, :2 * REL_CLIP + 1] for l in range(DEPTH)])
    g_wlr_full = jnp.stack([parts[5 + per * l].reshape(GATE_RANK, half) for l in range(DEPTH)])
    g_wlr = lax.dynamic_slice_in_dim(g_wlr_full, chip * (half // 4), half // 4, axis=2)
    g_blr = jnp.stack([parts[6 + per * l] for l in range(DEPTH)])
    g_gn = jnp.stack([parts[7 + per * l] for l in range(DEPTH)])
    g_ln1g = jnp.stack([parts[8 + per * l] for l in range(DEPTH)])
    g_ln1b = jnp.stack([parts[9 + per * l] for l in range(DEPTH)])
    g_ln2g = jnp.stack([parts[10 + per * l] for l in range(DEPTH)])
    g_ln2b = jnp.stack([parts[11 + per * l] for l in range(DEPTH)])

    grads = [g_ln_in_g, g_ln_in_b, None, g_rel, g_wlr, g_blr, g_gn, None, None, g_ln1g, g_ln1b, None, None,
             g_ln2g, g_ln2b]
    ws = [ln_in_g, ln_in_b, w_in, rel_bias, gla_w_lr, gla_b_lr, gla_norm_g, w_branch, w_out, ln1_g, ln1_b,
          w_up, w_down, ln2_g, ln2_b]
    ms = [m_ln_in_g, m_ln_in_b, m_w_in, m_rel_bias, m_gla_w_lr, m_gla_b_lr, m_gla_norm_g, m_w_branch, m_w_out,
          m_ln1_g, m_ln1_b, m_w_up, m_w_down, m_ln2_g, m_ln2_b]
    vs = [v_ln_in_g, v_ln_in_b, v_w_in, v_rel_bias, v_gla_w_lr, v_gla_b_lr, v_gla_norm_g, v_w_branch, v_w_out,
          v_ln1_g, v_ln1_b, v_w_up, v_w_down, v_ln2_g, v_ln2_b]

    deltas, new_ms, new_vs = [None] * 15, [None] * 15, [None] * 15
    big_idx = [2, 7, 8, 11, 12]
    for i, res in zip(big_idx, adam_out):
        shp = ws[i].shape
        grads[i], deltas[i], new_ms[i], new_vs[i] = (r.reshape(shp) for r in res)
    small_idx = [i for i in range(15) if i not in big_idx]

    def pack(arrs):
        flat_ = jnp.concatenate([arrs[i].reshape(-1) for i in small_idx])
        pad_ = (-flat_.shape[0]) % (8 * LANE)
        return jnp.concatenate([flat_, jnp.ones((pad_,), F32)]).reshape(-1, LANE)

    dl, nm, nv = _adamw("adamw_small", pack(ws), pack(grads), pack(ms), pack(vs))
    pos = 0
    for i in small_idx:
        sz = int(np.prod(ws[i].shape))
        deltas[i] = dl.reshape(-1)[pos:pos + sz].reshape(ws[i].shape)
        new_ms[i] = nm.reshape(-1)[pos:pos + sz].reshape(ws[i].shape)
        new_vs[i] = nv.reshape(-1)[pos:pos + sz].reshape(ws[i].shape)
        pos += sz

    return (loss, grad_x[None], *grads, *deltas, *new_ms, *new_vs)
```

```python
import functools

import numpy as np
import jax
import jax.numpy as jnp
from jax import lax
from jax.experimental import pallas as pl
from jax.experimental.pallas import tpu as pltpu

F32 = jnp.float32
BF16 = jnp.bfloat16
MXU_DTYPE = BF16
HI = lax.Precision.HIGHEST

DEPTH = 2
CHUNK = 64
N_BRANCH = 3
ATTN_HEADS = 8
ATTN_LEFT = 8
REL_CLIP = 2 * CHUNK
LIN_HEADS = 4
GATE_RANK = 16
GATE_NORM = 16.0
LN_EPS = 1e-5
NEG_INF = -1e30
ALPHA = (2 * DEPTH) ** 0.25
ADAM_LR, ADAM_B1, ADAM_B2, ADAM_EPS, ADAM_WD, ADAM_STEP = 0.001, 0.9, 0.999, 1e-08, 0.01, 10

LANE = 128
VMEM_LIMIT = 56 << 20
QB = 256
KW = 3 * QB
LB = 256
MESH_AXES = ("x", "y", "c")
DEV = pl.DeviceIdType.MESH


def _cp(sem):
    return pltpu.CompilerParams(dimension_semantics=sem, vmem_limit_bytes=VMEM_LIMIT)


def _mx(v):
    return v.astype(MXU_DTYPE)


def _dot(a, b):
    return jnp.dot(_mx(a), _mx(b), preferred_element_type=F32)


def _dot_nt(a, b):
    return lax.dot_general(_mx(a), _mx(b), (((1,), (1,)), ((), ())), preferred_element_type=F32)


def _dot_tn(a, b):
    return lax.dot_general(_mx(a), _mx(b), (((0,), (0,)), ((), ())), preferred_element_type=F32)


def _dot_hi(a, b):
    return jnp.dot(a, b, precision=HI, preferred_element_type=F32)


def _sigmoid(v):
    return 1.0 / (1.0 + jnp.exp(-v))


def _sds(shape, dtype):
    return jax.ShapeDtypeStruct(shape, dtype)


def _mm(name, a, b, tm, tn, nt=False, out_dtype=F32):
    batched = a.ndim == 3
    m, k = a.shape[-2:]
    n = b.shape[-2] if nt else b.shape[-1]
    tm, tn = min(tm, m), min(tn, n)

    def body(a_ref, b_ref, o_ref):
        f = _dot_nt if nt else _dot
        o_ref[...] = f(a_ref[...], b_ref[...]).astype(o_ref.dtype)

    rows_inner = (n // tn) * m < (m // tm) * n

    def ij(u, v):
        return (v, u) if rows_inner else (u, v)

    if batched:
        nb = a.shape[0]
        grid = (nb,) + ij(m // tm, n // tn)
        a_spec = pl.BlockSpec((None, tm, k), lambda g, u, v: (g, ij(u, v)[0], 0))
        b_spec = (pl.BlockSpec((None, tn, k), lambda g, u, v: (g, ij(u, v)[1], 0)) if nt
                  else pl.BlockSpec((None, k, tn), lambda g, u, v: (g, 0, ij(u, v)[1])))
        o_spec = pl.BlockSpec((None, tm, tn), lambda g, u, v: (g,) + ij(u, v))
        out_shape = _sds((nb, m, n), out_dtype)
        sem = ("parallel", "parallel", "parallel")
    else:
        grid = ij(m // tm, n // tn)
        a_spec = pl.BlockSpec((tm, k), lambda u, v: (ij(u, v)[0], 0))
        b_spec = (pl.BlockSpec((tn, k), lambda u, v: (ij(u, v)[1], 0)) if nt
                  else pl.BlockSpec((k, tn), lambda u, v: (0, ij(u, v)[1])))
        o_spec = pl.BlockSpec((tm, tn), lambda u, v: ij(u, v))
        out_shape = _sds((m, n), out_dtype)
        sem = ("parallel", "parallel")
    return pl.pallas_call(body, name=name, grid=grid, in_specs=[a_spec, b_spec], out_specs=o_spec,
                          out_shape=out_shape, compiler_params=_cp(sem))(a, b)


def _mm_tn(name, a, b, tm, tn):
    batched = a.ndim == 3
    k, m = a.shape[-2:]
    n = b.shape[-1]
    tm, tn = min(tm, m), min(tn, n)

    def body(a_ref, b_ref, o_ref):
        o_ref[...] = lax.dot_general(_mx(a_ref[...]), _mx(b_ref[...]), (((0,), (0,)), ((), ())),
                                     preferred_element_type=F32)

    if batched:
        nb = a.shape[0]
        grid = (nb, m // tm, n // tn)
        a_spec = pl.BlockSpec((None, k, tm), lambda g, i, j: (g, 0, i))
        b_spec = pl.BlockSpec((None, k, tn), lambda g, i, j: (g, 0, j))
        o_spec = pl.BlockSpec((None, tm, tn), lambda g, i, j: (g, i, j))
        out_shape = _sds((nb, m, n), F32)
    else:
        grid = (m // tm, n // tn)
        a_spec = pl.BlockSpec((k, tm), lambda i, j: (0, i))
        b_spec = pl.BlockSpec((k, tn), lambda i, j: (0, j))
        o_spec = pl.BlockSpec((tm, tn), lambda i, j: (i, j))
        out_shape = _sds((m, n), F32)
    return pl.pallas_call(body, name=name, grid=grid, in_specs=[a_spec, b_spec], out_specs=o_spec,
                          out_shape=out_shape, compiler_params=_cp(("parallel",) * len(grid)))(a, b)


def _ln_rows(y, g, b):
    mu = jnp.mean(y, axis=-1, keepdims=True)
    yc = y - mu
    var = jnp.mean(yc * yc, axis=-1, keepdims=True)
    rs = lax.rsqrt(var + LN_EPS)
    xh = yc * rs
    return xh * g + b, xh, rs


def _ln_in(x, g, b, tm=256):
    t, d = x.shape

    def body(x_ref, g_ref, b_ref, o_ref, ob_ref, xh_ref, rs_ref):
        o, xh, rs = _ln_rows(x_ref[...], g_ref[...], b_ref[...])
        o_ref[...] = o
        ob_ref[...] = o.astype(BF16)
        xh_ref[...] = xh
        rs_ref[...] = rs

    row = pl.BlockSpec((tm, d), lambda i: (i, 0))
    vec = pl.BlockSpec((1, d), lambda i: (0, 0))
    return pl.pallas_call(
        body, name="ln_in", grid=(t // tm,), in_specs=[row, vec, vec],
        out_specs=[row, row, row, pl.BlockSpec((tm, 1), lambda i: (i, 0))],
        out_shape=[_sds((t, d), F32), _sds((t, d), BF16), _sds((t, d), F32), _sds((t, 1), F32)],
        compiler_params=_cp(("parallel",)))(x, g, b)


def _mm_res_ln(name, a, w, res, g, b, tm, relu2):
    t, k = a.shape
    d = w.shape[1]

    def body(a_ref, w_ref, r_ref, g_ref, b_ref, o_ref, ob_ref, xh_ref, rs_ref, *act_ref):
        av = a_ref[...]
        if relu2:
            av = jnp.square(jnp.maximum(av, 0.0))
            act_ref[0][...] = av.astype(BF16)
        y = ALPHA * r_ref[...] + _dot(av, w_ref[...])
        o, xh, rs = _ln_rows(y, g_ref[...], b_ref[...])
        o_ref[...] = o
        ob_ref[...] = o.astype(BF16)
        xh_ref[...] = xh
        rs_ref[...] = rs

    row = pl.BlockSpec((tm, d), lambda i: (i, 0))
    vec = pl.BlockSpec((1, d), lambda i: (0, 0))
    arow = pl.BlockSpec((tm, k), lambda i: (i, 0))
    out_specs = [row, row, row, pl.BlockSpec((tm, 1), lambda i: (i, 0))]
    out_shape = [_sds((t, d), F32), _sds((t, d), BF16), _sds((t, d), F32), _sds((t, 1), F32)]
    if relu2:
        out_specs.append(arow)
        out_shape.append(_sds((t, k), BF16))
    return pl.pallas_call(
        body, name=name, grid=(t // tm,),
        in_specs=[arow, pl.BlockSpec((k, d), lambda i: (0, 0)), row, vec, vec],
        out_specs=out_specs, out_shape=out_shape, compiler_params=_cp(("parallel",)))(a, w, res, g, b)


def _merge_fwd(bo, wb, p, gate_off, tm=512, tn=512):
    _, t, d = bo.shape
    gb = gate_off // tn

    def body(bo_ref, wb_ref, g0, g1, g2, proj_ref, m_ref):
        acc = None
        for n, g_ref in enumerate((g0, g1, g2)):
            pr = _dot(bo_ref[n], wb_ref[n])
            proj_ref[n] = pr
            term = _sigmoid(g_ref[...]) * pr
            acc = term if acc is None else acc + term
        m_ref[...] = acc.astype(BF16)

    gspecs = [pl.BlockSpec((tm, tn), functools.partial(lambda i, j, n: (i, gb + n * (d // tn) + j), n=n))
              for n in range(3)]
    return pl.pallas_call(
        body, name="merge_fwd", grid=(t // tm, d // tn),
        in_specs=[pl.BlockSpec((3, tm, d), lambda i, j: (0, i, 0)),
                  pl.BlockSpec((3, d, tn), lambda i, j: (0, 0, j))] + gspecs,
        out_specs=[pl.BlockSpec((3, tm, tn), lambda i, j: (0, i, j)), pl.BlockSpec((tm, tn), lambda i, j: (i, j))],
        out_shape=[_sds((3, t, d), F32), _sds((t, d), BF16)],
        compiler_params=_cp(("parallel", "parallel")))(bo, wb, p, p, p)


def _merge_bwd(dz, wout, proj, p, gate_off, tm=512, tn=512):
    t, d = dz.shape
    gb = gate_off // tn

    def body(dz_ref, w_ref, proj_ref, g0, g1, g2, dproj_ref, dgl_ref):
        dm = _dot_nt(dz_ref[...], w_ref[...])
        for n, g_ref in enumerate((g0, g1, g2)):
            s = _sigmoid(g_ref[...])
            dproj_ref[n] = (dm * s).astype(BF16)
            dgl_ref[n] = (dm * proj_ref[n] * (s * (1.0 - s))).astype(BF16)

    gspecs = [pl.BlockSpec((tm, tn), functools.partial(lambda i, j, n: (i, gb + n * (d // tn) + j), n=n))
              for n in range(3)]
    dproj, dgl = pl.pallas_call(
        body, name="merge_bwd", grid=(t // tm, d // tn),
        in_specs=[pl.BlockSpec((tm, d), lambda i, j: (i, 0)), pl.BlockSpec((tn, d), lambda i, j: (j, 0)),
                  pl.BlockSpec((3, tm, tn), lambda i, j: (0, i, j))] + gspecs,
        out_specs=[pl.BlockSpec((3, tm, tn), lambda i, j: (0, i, j)),
                   pl.BlockSpec((3, tm, tn), lambda i, j: (0, i, j))],
        out_shape=[_sds((3, t, d), BF16), _sds((3, t, d), BF16)],
        compiler_params=_cp(("parallel", "parallel")))(dz, wout, proj, p, p, p)
    return dproj, dgl


def _mm_nt_relu2_bwd(dz, wdown, u, tm=512, tn=1024):
    t, d = dz.shape
    f = wdown.shape[0]

    def body(dz_ref, w_ref, u_ref, du_ref):
        da = _dot_nt(dz_ref[...], w_ref[...])
        du_ref[...] = (da * (2.0 * jnp.maximum(u_ref[...], 0.0))).astype(BF16)

    return pl.pallas_call(
        body, name="mlp_down_bwd", grid=(t // tm, f // tn),
        in_specs=[pl.BlockSpec((tm, d), lambda i, j: (i, 0)), pl.BlockSpec((tn, d), lambda i, j: (j, 0)),
                  pl.BlockSpec((tm, tn), lambda i, j: (i, j))],
        out_specs=pl.BlockSpec((tm, tn), lambda i, j: (i, j)), out_shape=_sds((t, f), BF16),
        compiler_params=_cp(("parallel", "parallel")))(dz, wdown, u)


def _ln_bwd_rows(dx, xh, rs, g):
    dxh = dx * g
    m1 = jnp.mean(dxh, axis=-1, keepdims=True)
    m2 = jnp.mean(dxh * xh, axis=-1, keepdims=True)
    return rs * (dxh - m1 - xh * m2)


def _mm_nt_res_lnbwd(name, a, w, dres, xh, rs, g, tm, tk):
    t, k = a.shape
    d = w.shape[0]
    nk = k // tk

    def body(a_ref, w_ref, dr_ref, xh_ref, rs_ref, g_ref, dz_ref, dzb_ref, dg_ref, db_ref, acc_ref):
        i, kk = pl.program_id(0), pl.program_id(1)

        @pl.when(kk == 0)
        def _():
            acc_ref[...] = ALPHA * dr_ref[...]

        acc_ref[...] += _dot_nt(a_ref[...], w_ref[...])

        @pl.when(jnp.logical_and(i == 0, kk == 0))
        def _():
            dg_ref[...] = jnp.zeros_like(dg_ref)
            db_ref[...] = jnp.zeros_like(db_ref)

        @pl.when(kk == nk - 1)
        def _():
            dx = acc_ref[...]
            xhv = xh_ref[...]
            dz = _ln_bwd_rows(dx, xhv, rs_ref[...], g_ref[...])
            dz_ref[...] = dz
            dzb_ref[...] = dz.astype(BF16)
            dg_ref[...] += jnp.sum(dx * xhv, axis=0, keepdims=True)
            db_ref[...] += jnp.sum(dx, axis=0, keepdims=True)

    row = pl.BlockSpec((tm, d), lambda i, kk: (i, 0))
    vec = pl.BlockSpec((1, d), lambda i, kk: (0, 0))
    return pl.pallas_call(
        body, name=name, grid=(t // tm, nk),
        in_specs=[pl.BlockSpec((tm, tk), lambda i, kk: (i, kk)), pl.BlockSpec((d, tk), lambda i, kk: (0, kk)),
                  row, row, pl.BlockSpec((tm, 1), lambda i, kk: (i, 0)), vec],
        out_specs=[row, row, vec, vec],
        out_shape=[_sds((t, d), F32), _sds((t, d), BF16), _sds((1, d), F32), _sds((1, d), F32)],
        scratch_shapes=[pltpu.VMEM((tm, d), F32)],
        compiler_params=_cp(("arbitrary", "arbitrary")))(a, w, dres, xh, rs, g)


def _loss_ln_bwd(x2, target, xh, rs, g, tm=256):
    t, d = x2.shape

    def body(x_ref, t_ref, xh_ref, rs_ref, g_ref, loss_ref, dz_ref, dzb_ref, dg_ref, db_ref):
        @pl.when(pl.program_id(0) == 0)
        def _():
            loss_ref[...] = jnp.zeros_like(loss_ref)
            dg_ref[...] = jnp.zeros_like(dg_ref)
            db_ref[...] = jnp.zeros_like(db_ref)

        err = x_ref[...] - t_ref[...]
        per_row = jnp.mean(err * err, axis=-1, keepdims=True)
        loss_ref[...] += 0.5 * jnp.sum(per_row, axis=0, keepdims=True)
        dx = err * (1.0 / d)
        xhv = xh_ref[...]
        dz = _ln_bwd_rows(dx, xhv, rs_ref[...], g_ref[...])
        dz_ref[...] = dz
        dzb_ref[...] = dz.astype(BF16)
        dg_ref[...] += jnp.sum(dx * xhv, axis=0, keepdims=True)
        db_ref[...] += jnp.sum(dx, axis=0, keepdims=True)

    row = pl.BlockSpec((tm, d), lambda i: (i, 0))
    vec = pl.BlockSpec((1, d), lambda i: (0, 0))
    return pl.pallas_call(
        body, name="loss_ln_bwd", grid=(t // tm,),
        in_specs=[row, row, row, pl.BlockSpec((tm, 1), lambda i: (i, 0)), vec],
        out_specs=[pl.BlockSpec((1, LANE), lambda i: (0, 0)), row, row, vec, vec],
        out_shape=[_sds((1, LANE), F32), _sds((t, d), F32), _sds((t, d), BF16), _sds((1, d), F32),
                   _sds((1, d), F32)],
        compiler_params=_cp(("arbitrary",)))(x2, target, xh, rs, g)


HPA = 2


def _attn_scores(q_ref, k_refs, bias_ref, i, dh, hh):
    cols = pl.ds(hh * dh, dh)
    q = q_ref[:, cols] * (dh ** -0.5)
    k = jnp.concatenate([r[:, cols] for r in k_refs], axis=0)
    s = _dot_nt(q, k) + bias_ref[hh]
    col = lax.broadcasted_iota(jnp.int32, s.shape, 1)
    s = jnp.where(col >= (2 - i) * QB, s, NEG_INF)
    m = jnp.max(s, axis=-1, keepdims=True)
    e = jnp.exp(s - m)
    return q, k, e / jnp.sum(e, axis=-1, keepdims=True)


def _attn_specs(dh, off):
    w = HPA * dh
    qcol, kcol, vcol = off["aq"] // w, off["ak"] // w, off["av"] // w
    q_spec = pl.BlockSpec((QB, w), lambda g, i: (i, qcol + g))
    k_specs = [pl.BlockSpec((QB, w), functools.partial(lambda g, i, j: (jnp.maximum(i - 2 + j, 0), kcol + g), j=j))
               for j in range(3)]
    v_specs = [pl.BlockSpec((QB, w), functools.partial(lambda g, i, j: (jnp.maximum(i - 2 + j, 0), vcol + g), j=j))
               for j in range(3)]
    bias_spec = pl.BlockSpec((HPA, QB, KW), lambda g, i: (g, 0, 0))
    return q_spec, k_specs, v_specs, bias_spec


def _attn_fwd(p, bias, d, off):
    t = p.shape[0]
    dh = d // ATTN_HEADS

    def body(q_ref, k0, k1, k2, v0, v1, v2, bias_ref, o_ref):
        for hh in range(HPA):
            cols = pl.ds(hh * dh, dh)
            _, _, pr = _attn_scores(q_ref, (k0, k1, k2), bias_ref, pl.program_id(1), dh, hh)
            v = jnp.concatenate([v0[:, cols], v1[:, cols], v2[:, cols]], axis=0)
            o_ref[:, cols] = _dot(pr, v).astype(o_ref.dtype)

    q_spec, k_specs, v_specs, bias_spec = _attn_specs(dh, off)
    return pl.pallas_call(
        body, name="attn_fwd", grid=(ATTN_HEADS // HPA, t // QB),
        in_specs=[q_spec] + k_specs + v_specs + [bias_spec],
        out_specs=pl.BlockSpec((QB, HPA * dh), lambda g, i: (i, g)), out_shape=_sds((t, d), BF16),
        compiler_params=_cp(("parallel", "parallel")))(p, p, p, p, p, p, p, bias)


def _attn_bwd(p, bias, do, d, off):
    t = p.shape[0]
    dh = d // ATTN_HEADS
    tp = t + 2 * QB

    def body(q_ref, k0, k1, k2, v0, v1, v2, bias_ref, do_ref, dq_ref, dk_ref, dv_ref, dbias_ref):
        i = pl.program_id(1)

        @pl.when(i == 0)
        def _():
            dk_ref[...] = jnp.zeros_like(dk_ref)
            dv_ref[...] = jnp.zeros_like(dv_ref)
            dbias_ref[...] = jnp.zeros_like(dbias_ref)

        rows = pl.ds(pl.multiple_of(i * QB, QB), KW)
        for hh in range(HPA):
            cols = pl.ds(hh * dh, dh)
            q, k, pr = _attn_scores(q_ref, (k0, k1, k2), bias_ref, i, dh, hh)
            v = jnp.concatenate([v0[:, cols], v1[:, cols], v2[:, cols]], axis=0)
            dov = do_ref[:, cols]
            dp = _dot_nt(dov, v)
            delta = jnp.sum(pr * dp, axis=-1, keepdims=True)
            ds = pr * (dp - delta)
            dbias_ref[hh] += ds
            dq_ref[:, cols] = (_dot(ds, k) * (dh ** -0.5)).astype(dq_ref.dtype)
            dk_ref[rows, cols] += _dot_tn(ds, q)
            dv_ref[rows, cols] += _dot_tn(pr, dov)

    q_spec, k_specs, v_specs, bias_spec = _attn_specs(dh, off)
    row_spec = pl.BlockSpec((QB, HPA * dh), lambda g, i: (i, g))
    acc_spec = pl.BlockSpec((tp, HPA * dh), lambda g, i: (0, g))
    return pl.pallas_call(
        body, name="attn_bwd", grid=(ATTN_HEADS // HPA, t // QB),
        in_specs=[q_spec] + k_specs + v_specs + [bias_spec, row_spec],
        out_specs=[row_spec, acc_spec, acc_spec, bias_spec],
        out_shape=[_sds((t, d), BF16), _sds((tp, d), F32), _sds((tp, d), F32),
                   _sds((ATTN_HEADS, QB, KW), F32)],
        compiler_params=_cp(("parallel", "arbitrary")))(p, p, p, p, p, p, p, bias, do)


def _onehot_mm(name, a, b):
    def body(a_ref, b_ref, o_ref):
        o_ref[...] = _dot_hi(a_ref[...], b_ref[...])

    return pl.pallas_call(body, name=name, out_shape=_sds((a.shape[0], b.shape[1]), F32),
                          compiler_params=pltpu.CompilerParams(vmem_limit_bytes=VMEM_LIMIT))(a, b)


def _diag_index():
    ii, jj = np.arange(CHUNK)[:, None], np.arange(CHUNK)[None, :]
    return (ii - jj + CHUNK - 1).reshape(-1)


def _bias_expand(rel_bias):
    h = rel_bias.shape[0]
    nq, nk, shift = QB // CHUNK, KW // CHUNK, (2 * QB) // CHUNK
    nbin, ndc = 3 * LANE, 4
    rb = jnp.pad(rel_bias, ((0, 0), (0, nbin - rel_bias.shape[1])))
    win = np.clip(CHUNK * np.arange(ndc)[:, None] + np.arange(LANE)[None, :] - (CHUNK - 1), -REL_CLIP, REL_CLIP)
    sel = (jnp.arange(nbin)[:, None] == jnp.asarray((win + REL_CLIP).reshape(1, -1))).astype(F32)
    windows = _onehot_mm("bias_windows", rb, sel)
    diag_t = (jnp.arange(LANE)[:, None] == jnp.asarray(_diag_index().reshape(1, -1))).astype(F32)
    blocks = _onehot_mm("bias_blocks", windows.reshape(h * ndc, LANE), diag_t).reshape(h, ndc, CHUNK, CHUNK)
    off_band = jnp.full((h, CHUNK, CHUNK), NEG_INF, F32)
    rows = []
    for ic in range(nq):
        dcs = [ic - jc + shift for jc in range(nk)]
        rows.append(jnp.concatenate([blocks[:, min(dc, ndc - 1)] if 0 <= dc <= ATTN_LEFT else off_band
                                     for dc in dcs], axis=2))
    return jnp.concatenate(rows, axis=1)


def _bias_reduce(dbias):
    h = dbias.shape[0]
    nq, nk = QB // CHUNK, KW // CHUNK
    nbin = 3 * LANE
    blocks = dbias.reshape(h, nq, CHUNK, nk, CHUNK).transpose(0, 1, 3, 2, 4).reshape(h * nq * nk, CHUNK * CHUNK)
    diag = (jnp.asarray(_diag_index().reshape(-1, 1)) == jnp.arange(LANE)[None, :]).astype(F32)
    ic = np.arange(nq)[:, None, None]
    jc = np.arange(nk)[None, :, None]
    dl = np.arange(LANE)[None, None, :] - (CHUNK - 1)
    rel = np.clip(CHUNK * (ic - jc + (2 * QB) // CHUNK) + dl, -REL_CLIP, REL_CLIP) + REL_CLIP
    bins = (jnp.asarray(rel.reshape(-1, 1)) == jnp.arange(nbin)[None, :]).astype(F32)

    diags = _onehot_mm("bias_diag_sums", blocks, diag)
    out = _onehot_mm("bias_bin_sums", diags.reshape(h, nq * nk * LANE), bins)
    return out[:, :2 * REL_CLIP + 1]


def _chunk_masks():
    r = lax.broadcasted_iota(jnp.int32, (LB, LB), 0)
    c = lax.broadcasted_iota(jnp.int32, (LB, LB), 1)
    return (r // CHUNK) == (c // CHUNK), r >= c, r <= c


def _chunks(a):
    return [a[c * CHUNK:(c + 1) * CHUNK] for c in range(LB // CHUNK)]


def _per_chunk(a, f):
    return jnp.concatenate([jnp.broadcast_to(f(c), c.shape) for c in _chunks(a)], axis=0)


def _dot_sel(sel, x):
    def top(v):
        return lax.bitcast_convert_type(lax.bitcast_convert_type(v, jnp.int32) & jnp.int32(-65536), F32)

    hi = top(x)
    mid = top(x - hi)
    lo = (x - hi) - mid
    d = functools.partial(jnp.dot, sel.astype(jnp.bfloat16), preferred_element_type=F32)
    return d(hi.astype(jnp.bfloat16)) + d(mid.astype(jnp.bfloat16)) + d(lo.astype(jnp.bfloat16))


def _lin_block(gla, q, k, v, aux):
    dk = q.shape[-1]
    same, low, up = _chunk_masks()
    ones = same.astype(F32)
    if gla:
        glr, wlr, blr = aux
        q = q * (dk ** -0.5)
        pre = _dot(glr, wlr) + blr
        log_a = (jnp.minimum(pre, 0.0) - jnp.log(1.0 + jnp.exp(-jnp.abs(pre)))) / GATE_NORM
        b = _dot_sel(jnp.where(low, ones, 0.0), log_a)
        lastb = _per_chunk(b, lambda c: c[CHUNK - 1:])
    else:
        cs, sn, lg = aux
        pre = None
        half = dk // 2
        q = q * cs + pltpu.roll(q, half, 1) * sn
        k = (k * cs + pltpu.roll(k, half, 1) * sn) * (dk ** -0.5)
        pos = (lax.broadcasted_iota(jnp.int32, (LB, dk), 0) % CHUNK).astype(F32) + 1.0
        b = pos * lg
        lastb = jnp.broadcast_to(float(CHUNK) * lg, b.shape)
    eb, enb, el, dec = jnp.exp(b), jnp.exp(-b), jnp.exp(lastb - b), jnp.exp(lastb)
    qf, kf, qb, kb, kl = q * eb, k * enb, q * enb, k * eb, k * el
    s = jnp.where(same, jnp.where(low, _dot_nt(qf, kf), _dot_nt(qb, kb)), 0.0)
    return dict(pre=pre, eb=eb, enb=enb, el=el, dec=dec, qf=qf, kf=kf, qb=qb, kb=kb, kl=kl, s=s,
                same=same, low=low, up=up, ones=ones)


def _lin_norm_gate(gla, o, gate, gn):
    sg = _sigmoid(gate)
    silu = gate * sg
    if gla:
        r = lax.rsqrt(jnp.mean(o * o, axis=-1, keepdims=True) + LN_EPS)
        hn = o * r
        return silu * (hn * gn), (sg, silu, r, hn)
    mu = jnp.mean(o, axis=-1, keepdims=True)
    oc = o - mu
    r = lax.rsqrt(jnp.mean(oc * oc, axis=-1, keepdims=True) + LN_EPS)
    hn = oc * r
    return silu * hn, (sg, silu, r, hn)


HPS = 2


def _lin_specs(gla, dk, dv, off, rev, nb):
    pre = "g" if gla else "r"
    wk, wv = HPS * dk, HPS * dv
    qc, kc, vc, gc = (off[pre + "q"] // wk, off[pre + "k"] // wk, off[pre + "v"] // wv, off[pre + "g"] // wv)

    def blk(i):
        return nb - 1 - i if rev else i

    specs = [pl.BlockSpec((LB, wk), lambda g, i: (blk(i), qc + g)),
             pl.BlockSpec((LB, wk), lambda g, i: (blk(i), kc + g)),
             pl.BlockSpec((LB, wv), lambda g, i: (blk(i), vc + g)),
             pl.BlockSpec((LB, wv), lambda g, i: (blk(i), gc + g))]
    if gla:
        specs += [pl.BlockSpec((LB, LANE), lambda g, i: (blk(i), off["glr"] // LANE)),
                  pl.BlockSpec((LANE, wk), lambda g, i: (0, g)),
                  pl.BlockSpec((1, wk), lambda g, i: (0, g)),
                  pl.BlockSpec((1, dv), lambda g, i: (0, 0))]
    else:
        specs += [pl.BlockSpec((LB, dk), lambda g, i: (blk(i), 0)),
                  pl.BlockSpec((LB, dk), lambda g, i: (blk(i), 0)),
                  pl.BlockSpec((HPS, 1, dk), lambda g, i: (g, 0, 0))]
    return specs, blk


def _lin_aux(gla, refs, rows, hh, dk):
    if gla:
        glr_ref, wlr_ref, blr_ref, gn_ref = refs
        kcols = pl.ds(hh * dk, dk)
        return (glr_ref[rows, :], wlr_ref[:, kcols], blr_ref[:, kcols]), gn_ref[...]
    cs_ref, sn_ref, lg_ref = refs
    return (cs_ref[rows, :], sn_ref[rows, :], lg_ref[hh]), None


def _lin_fwd(gla, p, aux_arrays, d, off):
    t = p.shape[0]
    dk, dv = d // (2 * LIN_HEADS), d // LIN_HEADS
    nb, cb = t // LB, LB // CHUNK
    naux = len(aux_arrays)

    def body(*refs):
        q_ref, k_ref, v_ref, g_ref = refs[:4]
        aux_refs = refs[4:4 + naux]
        o_ref, bo_ref, st_out_ref, st_ref = refs[4 + naux:]

        @pl.when(pl.program_id(1) == 0)
        def _():
            st_ref[...] = jnp.zeros_like(st_ref)

        rows = slice(None)
        for hh in range(HPS):
            kcols, vcols = pl.ds(hh * dk, dk), pl.ds(hh * dv, dv)
            aux, gn = _lin_aux(gla, aux_refs, rows, hh, dk)
            v = v_ref[:, vcols]
            blk = _lin_block(gla, q_ref[:, kcols], k_ref[:, kcols], v, aux)
            st = st_ref[hh]
            inter = []
            for c, (qf, kl, dec, vc) in enumerate(zip(_chunks(blk["qf"]), _chunks(blk["kl"]), _chunks(blk["dec"]),
                                                      _chunks(v))):
                st_out_ref[hh, c] = st
                inter.append(_dot_nt(qf, st))
                st = st * dec[:1] + _dot_tn(vc, kl)
            st_ref[hh] = st
            o = _dot(blk["s"], v) + jnp.concatenate(inter, axis=0)
            o_ref[:, vcols] = o
            out, _ = _lin_norm_gate(gla, o, g_ref[:, vcols], gn)
            bo_ref[:, vcols] = out.astype(BF16)

    specs, _ = _lin_specs(gla, dk, dv, off, False, nb)
    orow = pl.BlockSpec((LB, HPS * dv), lambda g, i: (i, g))
    return pl.pallas_call(
        body, name="gla_fwd" if gla else "ret_fwd", grid=(LIN_HEADS // HPS, nb), in_specs=specs,
        out_specs=[orow, orow, pl.BlockSpec((HPS, cb, dv, dk), lambda g, i: (g, i, 0, 0))],
        out_shape=[_sds((t, d), F32), _sds((t, d), BF16), _sds((LIN_HEADS, t // CHUNK, dv, dk), F32)],
        scratch_shapes=[pltpu.VMEM((HPS, dv, dk), F32)],
        compiler_params=_cp(("parallel", "arbitrary")))(p, p, p, p, *aux_arrays)


def _lin_bwd(gla, p, aux_arrays, o, states, dbo, d, off):
    t = p.shape[0]
    dk, dv = d // (2 * LIN_HEADS), d // LIN_HEADS
    nb, cb = t // LB, LB // CHUNK
    naux = len(aux_arrays)

    def body(*refs):
        q_ref, k_ref, v_ref, g_ref = refs[:4]
        aux_refs = refs[4:4 + naux]
        o_ref, st_in_ref, dbo_ref = refs[4 + naux:7 + naux]
        outs = refs[7 + naux:]
        dq_ref, dk_ref, dv_ref, dg_ref = outs[:4]
        dst_ref = outs[-1]
        first = pl.program_id(1) == 0

        @pl.when(first)
        def _():
            dst_ref[...] = jnp.zeros_like(dst_ref)

        if gla:
            dpre_ref, dblr_ref, dgn_ref = outs[4:7]

            @pl.when(first)
            def _():
                dblr_ref[...] = jnp.zeros_like(dblr_ref)
                dgn_ref[...] = jnp.zeros_like(dgn_ref)

        rows = slice(None)
        for hh in range(HPS):
            kcols, vcols = pl.ds(hh * dk, dk), pl.ds(hh * dv, dv)
            aux, gn = _lin_aux(gla, aux_refs, rows, hh, dk)
            v = v_ref[:, vcols]
            bk = _lin_block(gla, q_ref[:, kcols], k_ref[:, kcols], v, aux)
            eb, enb, el, dec = bk["eb"], bk["enb"], bk["el"], bk["dec"]
            qf, kf, qb, kb, kl, s = bk["qf"], bk["kf"], bk["qb"], bk["kb"], bk["kl"], bk["s"]
            gate = g_ref[:, vcols]
            dout = dbo_ref[:, vcols]
            _, (sg, silu, r, hn) = _lin_norm_gate(gla, o_ref[:, vcols], gate, gn)
            dsilu = sg * (1.0 + gate * (1.0 - sg))
            if gla:
                y = hn * gn
                dy = dout * silu
                dg_ref[:, vcols] = (dout * y * dsilu).astype(BF16)
                dgn_ref[hh] += jnp.sum(dy * hn, axis=0, keepdims=True)
                dhn = dy * gn
                do = r * (dhn - hn * jnp.mean(dhn * hn, axis=-1, keepdims=True))
            else:
                dhn = dout * silu
                dg_ref[:, vcols] = (dout * hn * dsilu).astype(BF16)
                do = r * (dhn - jnp.mean(dhn, axis=-1, keepdims=True)
                          - hn * jnp.mean(dhn * hn, axis=-1, keepdims=True))
            ds = jnp.where(bk["same"], _dot_nt(do, v), 0.0)
            dsf = jnp.where(bk["low"], ds, 0.0)
            dsb = ds - dsf
            dvv = _dot_tn(s, do)
            dqf = _dot(dsf, kf)
            dkf = _dot_tn(dsf, qf)
            dqb = _dot(dsb, kb)
            dkb = _dot_tn(dsb, qb)
            dst = dst_ref[hh]
            dv_st, dqf_st, dkl_c, ddec_c = [], [], [], []
            parts = zip(reversed(range(cb)), reversed(_chunks(do)), reversed(_chunks(v)), reversed(_chunks(qf)),
                        reversed(_chunks(kl)), reversed(_chunks(dec)))
            for c, do_c, v_c, qf_c, kl_c, dec_c in parts:
                st = st_in_ref[hh, c]
                dv_st.append(_dot_nt(kl_c, dst))
                dkl_c.append(_dot(v_c, dst))
                dqf_st.append(_dot(do_c, st))
                ddec_c.append(jnp.broadcast_to(jnp.sum(dst * st, axis=0, keepdims=True), (CHUNK, dk)))
                dst = dst * dec_c[:1] + _dot_tn(do_c, qf_c)
            dst_ref[hh] = dst

            def cat(pieces):
                return jnp.concatenate(pieces[::-1], axis=0)

            dvv = dvv + cat(dv_st)
            dqf = dqf + cat(dqf_st)
            dkl = cat(dkl_c)
            dq = dqf * eb + dqb * enb
            dkk = dkf * enb + dkb * eb + dkl * el
            dv_ref[:, vcols] = dvv.astype(BF16)
            if gla:
                db = dqf * qf - dkf * kf - dqb * qb + dkb * kb - dkl * kl
                dlast = _per_chunk(dkl * kl, lambda c: jnp.sum(c, axis=0, keepdims=True)) + cat(ddec_c) * dec
                dlog_a = _dot_sel(jnp.where(bk["up"], bk["ones"], 0.0), db) + dlast
                dpre = dlog_a * (1.0 / GATE_NORM) * (1.0 - _sigmoid(bk["pre"]))
                dpre_ref[:, kcols] = dpre
                dblr_ref[hh] += jnp.sum(dpre, axis=0, keepdims=True)
                dq_ref[:, kcols] = (dq * (dk ** -0.5)).astype(BF16)
                dk_ref[:, kcols] = dkk.astype(BF16)
            else:
                cs, sn, _ = aux
                half = dk // 2
                dkk = dkk * (dk ** -0.5)
                dq_ref[:, kcols] = (dq * cs + pltpu.roll(dq * sn, half, 1)).astype(BF16)
                dk_ref[:, kcols] = (dkk * cs + pltpu.roll(dkk * sn, half, 1)).astype(BF16)

    specs, blk = _lin_specs(gla, dk, dv, off, True, nb)
    vrow = pl.BlockSpec((LB, HPS * dv), lambda g, i: (blk(i), g))
    krow = pl.BlockSpec((LB, HPS * dk), lambda g, i: (blk(i), g))
    specs += [vrow, pl.BlockSpec((HPS, cb, dv, dk), lambda g, i: (g, blk(i), 0, 0)), vrow]
    out_specs = [krow, krow, vrow, vrow]
    out_shape = [_sds((t, d // 2), BF16), _sds((t, d // 2), BF16), _sds((t, d), BF16), _sds((t, d), BF16)]
    if gla:
        out_specs += [krow, pl.BlockSpec((HPS, 1, dk), lambda g, i: (g, 0, 0)),
                      pl.BlockSpec((HPS, 1, dv), lambda g, i: (g, 0, 0))]
        out_shape += [_sds((t, d // 2), F32), _sds((LIN_HEADS, 1, dk), F32), _sds((LIN_HEADS, 1, dv), F32)]
    out_specs.append(pl.BlockSpec((HPS, dv, dk), lambda g, i: (g, 0, 0)))
    out_shape.append(_sds((LIN_HEADS, dv, dk), F32))
    res = pl.pallas_call(
        body, name="gla_bwd" if gla else "ret_bwd", grid=(LIN_HEADS // HPS, nb), in_specs=specs,
        out_specs=out_specs, out_shape=out_shape,
        compiler_params=_cp(("parallel", "arbitrary")))(p, p, p, p, *aux_arrays, o, states, dbo)
    return res[:-1]


def _row_tile(rows, cols):
    cap = max(8, (2 << 20) // (4 * cols))
    t = rows
    while t > cap and t % 2 == 0:
        t //= 2
    return t


def _add_half(name, g, t, sel):
    nchip, hr, cols = t.shape
    tr = _row_tile(hr, cols)
    nb = hr // tr

    def body(sel_ref, g_ref, t_ref, o_ref):
        o_ref[...] = g_ref[...] + t_ref[...]

    half = pl.BlockSpec((None, tr, cols), lambda p, i, s: (p, i, 0))
    gs = pltpu.PrefetchScalarGridSpec(
        num_scalar_prefetch=1, grid=(nchip, nb),
        in_specs=[pl.BlockSpec((None, tr, cols), lambda p, i, s: (p, s[0] * nb + i, 0)), half], out_specs=half)
    return pl.pallas_call(body, name=name, grid_spec=gs, out_shape=_sds(t.shape, F32),
                          compiler_params=_cp(("parallel", "parallel")))(sel, g, t)


def _sum_shards(name, h, rcv, sel):
    _, rows, cols = h.shape
    tr = _row_tile(rows, cols)

    def body(sel_ref, h_ref, r0, r1, r2, o_ref):
        o_ref[...] = ((h_ref[...] + r0[...]) + r1[...]) + r2[...]

    rspecs = [pl.BlockSpec((None, tr, cols), functools.partial(lambda i, s, j: (j, i, 0), j=j)) for j in range(3)]
    gs = pltpu.PrefetchScalarGridSpec(
        num_scalar_prefetch=1, grid=(rows // tr,),
        in_specs=[pl.BlockSpec((None, tr, cols), lambda i, s: (s[0], i, 0))] + rspecs,
        out_specs=pl.BlockSpec((tr, cols), lambda i, s: (i, 0)))
    return pl.pallas_call(body, name=name, grid_spec=gs, out_shape=_sds((rows, cols), F32),
                          compiler_params=_cp(("parallel",)))(sel, h, rcv, rcv, rcv)


def _adamw_math(w, g, m, v):
    c1 = 1.0 - ADAM_B1 ** ADAM_STEP
    c2 = 1.0 - ADAM_B2 ** ADAM_STEP
    nm = ADAM_B1 * m + (1.0 - ADAM_B1) * g
    nv = ADAM_B2 * v + (1.0 - ADAM_B2) * jnp.square(g)
    return -ADAM_LR * ((nm / c1) / (jnp.sqrt(nv / c2) + ADAM_EPS) + ADAM_WD * w), nm, nv


def _adamw(name, w, g, m, v):
    rows, cols = w.shape
    tr = _row_tile(rows, cols)

    def body(w_ref, g_ref, m_ref, v_ref, d_ref, nm_ref, nv_ref):
        d_ref[...], nm_ref[...], nv_ref[...] = _adamw_math(w_ref[...], g_ref[...], m_ref[...], v_ref[...])

    spec = pl.BlockSpec((tr, cols), lambda i: (i, 0))
    return pl.pallas_call(body, name=name, grid=(rows // tr,), in_specs=[spec] * 4, out_specs=[spec] * 3,
                          out_shape=[_sds((rows, cols), F32)] * 3, compiler_params=_cp(("parallel",)))(w, g, m, v)


def _adamw_layer(name, w, g_own, g_sib, sel, m, v, layer, prev):
    depth, rows, cols = w.shape
    tr = _row_tile(rows // 2, cols)
    nbh = rows // 2 // tr
    nprev = 0 if prev is None else 4

    def body(sel_ref, w_ref, own_ref, sib_ref, m_ref, v_ref, *rest):
        go_ref, d_ref, nm_ref, nv_ref = rest[nprev:]
        gv = jnp.where(pl.program_id(0) // nbh == sel_ref[0], own_ref[...], sib_ref[...])
        go_ref[...] = gv
        d_ref[...], nm_ref[...], nv_ref[...] = _adamw_math(w_ref[...], gv, m_ref[...], v_ref[...])

    lay = pl.BlockSpec((None, tr, cols), lambda i, s: (layer, i, 0))
    hlf = pl.BlockSpec((tr, cols), lambda i, s: (i % nbh, 0))
    gs = pltpu.PrefetchScalarGridSpec(
        num_scalar_prefetch=1, grid=(2 * nbh,), in_specs=[lay, hlf, hlf, lay, lay] + [ANY] * nprev,
        out_specs=[lay] * 4)
    args = (sel, w, g_own, g_sib, m, v) + (() if prev is None else tuple(prev))
    return pl.pallas_call(
        body, name=name, grid_spec=gs, out_shape=[_sds((depth, rows, cols), F32)] * 4,
        input_output_aliases={6 + k: k for k in range(nprev)},
        compiler_params=_cp(("parallel",)))(*args)


def _place():
    x, y, c = (lax.axis_index(a) for a in MESH_AXES)
    chips = [(1 - x, y), (x, 1 - y), (1 - x, 1 - y)]
    return x, y, c, chips


def _chip_index(xy):
    return 2 * xy[0] + xy[1]


ANY = pl.BlockSpec(memory_space=pl.ANY)


HBM_SPEC = pl.BlockSpec(memory_space=pltpu.HBM)
SEM = pl.BlockSpec(memory_space=pltpu.SEMAPHORE)
EFFECT = pltpu.SideEffectType.DATAFLOW_SIDE_EFFECTING


def _half(ref, c):
    hr = ref.shape[-2] // 2
    return pl.ds(pl.multiple_of(c * hr, 16), hr)


def _gather_copies(srcs, lands, send, recv):
    x, y, c, chips = _place()
    me = _chip_index((x, y))
    return [pltpu.make_async_remote_copy(src_ref=s.at[_half(s, c)], dst_ref=g.at[me, _half(s, c)],
                                         send_sem=send.at[3 * a + j], recv_sem=recv.at[3 * a + j],
                                         device_id=(*ch, c), device_id_type=DEV)
            for a, (s, g) in enumerate(zip(srcs, lands)) for j, ch in enumerate(chips)]


def _scatter_copies(srcs, lands, send, recv):
    x, y, c, chips = _place()
    return [pltpu.make_async_remote_copy(src_ref=h.at[_chip_index(ch)], dst_ref=r.at[j],
                                         send_sem=send.at[3 * a + j], recv_sem=recv.at[3 * a + j],
                                         device_id=(*ch, c), device_id_type=DEV)
            for a, (h, r) in enumerate(zip(srcs, lands)) for j, ch in enumerate(chips)]


def _in_hbm(a):
    return pltpu.with_memory_space_constraint(a, pltpu.HBM)


def _split_start(name, srcs, land_shapes, copies_fn, after=None):
    ns, nl = len(srcs), len(land_shapes)
    ncp = 3 * ns
    lands = [lax.empty(s.shape, s.dtype) for s in land_shapes]
    behind = [] if after is None else [after]

    def body(*refs):
        src, land = refs[:ns], refs[ns:ns + nl]
        send, recv = refs[ns + nl + len(behind)], refs[ns + nl + len(behind) + 1]
        for cp in copies_fn(src, land, send, recv):
            cp.start()
        refs[-1][...] = jnp.zeros_like(refs[-1])

    bufs = list(srcs) + lands
    outs = pl.pallas_call(
        body, name=name, in_specs=[HBM_SPEC] * (ns + nl) + [ANY] * len(behind),
        out_specs=[SEM, SEM] + [HBM_SPEC] * (ns + nl) + [pl.BlockSpec(memory_space=pltpu.VMEM)],
        out_shape=[pltpu.SemaphoreType.DMA((ncp,)), pltpu.SemaphoreType.DMA((ncp,))]
        + [pltpu.HBM(b.shape, b.dtype) for b in bufs] + [_sds((8, LANE), F32)],
        input_output_aliases={i: 2 + i for i in range(ns + nl)},
        compiler_params=pltpu.CompilerParams(has_side_effects=EFFECT))(*[_in_hbm(b) for b in bufs], *behind)
    return outs[0], outs[1], list(outs[2:2 + ns]), list(outs[2 + ns:2 + ns + nl]), outs[-1]


def _split_wait(name, started, copies_fn, after):
    send, recv, srcs, lands, _ = started
    ns, nl = len(srcs), len(lands)

    def body(*refs):
        src, land = refs[:ns], refs[ns:ns + nl]
        for cp in copies_fn(src, land, refs[ns + nl], refs[ns + nl + 1]):
            cp.wait_send()
            cp.wait_recv()

    bufs = list(srcs) + list(lands)
    outs = pl.pallas_call(
        body, name=name, in_specs=[HBM_SPEC] * (ns + nl) + [SEM, SEM, ANY], out_specs=[HBM_SPEC] * (ns + nl),
        out_shape=[pltpu.HBM(b.shape, b.dtype) for b in bufs],
        input_output_aliases={i: i for i in range(ns + nl)},
        compiler_params=pltpu.CompilerParams(has_side_effects=EFFECT))(*bufs, send, recv, after)
    return list(outs[:ns]), list(outs[ns:])


def _gather_plain(name, srcs):
    n = len(srcs)

    def body(*refs):
        src, land = refs[:n], refs[n:2 * n]
        send, recv, fsend, frecv = refs[2 * n:]
        first = _gather_copies(src, land, send, recv)
        for cp in first:
            cp.start()
        _forward_body(land, first, fsend, frecv)

    return pl.pallas_call(
        body, name=name, in_specs=[ANY] * n, out_specs=[ANY] * n,
        out_shape=[_sds((4,) + s.shape, s.dtype) for s in srcs],
        scratch_shapes=[pltpu.SemaphoreType.DMA((3 * n,))] * 4)(*srcs)


def _forward_body(land, arrivals, fsend, frecv):
    x, y, c, chips = _place()
    n = len(land)
    passed = []
    for a in range(n):
        for j, ch in enumerate(chips):
            if arrivals is not None:
                arrivals[3 * a + j].wait_recv()
            slot = land[a].at[_chip_index(ch), _half(land[a], c)]
            fw = pltpu.make_async_remote_copy(src_ref=slot, dst_ref=slot, send_sem=fsend.at[3 * a + j],
                                              recv_sem=frecv.at[3 * a + j], device_id=(x, y, 1 - c),
                                              device_id_type=DEV)
            fw.start()
            passed.append(fw)
    for a in range(n):
        for j, ch in enumerate(chips):
            slot = land[a].at[_chip_index(ch), _half(land[a], 1 - c)]
            pltpu.make_async_remote_copy(src_ref=slot, dst_ref=slot, send_sem=fsend.at[3 * a + j],
                                         recv_sem=frecv.at[3 * a + j], device_id=(x, y, c),
                                         device_id_type=DEV).wait_recv()
    for cp in passed:
        cp.wait_send()
    if arrivals is not None:
        for cp in arrivals:
            cp.wait_send()


def _gather_forward(name, lands):
    n = len(lands)

    def body(*refs):
        _forward_body(refs[n:2 * n], None, refs[2 * n], refs[2 * n + 1])

    return pl.pallas_call(
        body, name=name, in_specs=[ANY] * n, out_specs=[ANY] * n,
        out_shape=[_sds(g.shape, g.dtype) for g in lands], input_output_aliases={a: a for a in range(n)},
        scratch_shapes=[pltpu.SemaphoreType.DMA((3 * n,))] * 2)(*lands)


def _sibling_halves(name, grs):
    n = len(grs)

    def body(*refs):
        ins, outs = refs[:n], refs[n:2 * n]
        send, recv = refs[2 * n:]
        x, y, c, _ = _place()
        cps = [pltpu.make_async_remote_copy(src_ref=ins[a].at[:, _half(ins[a], 1 - c)], dst_ref=outs[a],
                                            send_sem=send.at[a], recv_sem=recv.at[a], device_id=(x, y, 1 - c),
                                            device_id_type=DEV) for a in range(n)]
        for cp in cps:
            cp.start()
        for cp in cps:
            cp.wait()

    return pl.pallas_call(
        body, name=name, in_specs=[ANY] * n, out_specs=[ANY] * n,
        out_shape=[_sds((g.shape[0], g.shape[1] // 2, g.shape[2]), F32) for g in grs],
        scratch_shapes=[pltpu.SemaphoreType.DMA((n,)), pltpu.SemaphoreType.DMA((n,))])(*grs)


def _sibling_share(name, sms):
    n = len(sms)

    def body(*refs):
        ins, outs = refs[:n], refs[n:2 * n]
        send, recv = refs[2 * n:]
        x, y, c, _ = _place()
        cps = [pltpu.make_async_remote_copy(src_ref=ins[a], dst_ref=outs[a], send_sem=send.at[a],
                                            recv_sem=recv.at[a], device_id=(x, y, 1 - c), device_id_type=DEV)
               for a in range(n)]
        for cp in cps:
            cp.start()
        for cp in cps:
            cp.wait()

    return pl.pallas_call(
        body, name=name, in_specs=[ANY] * n, out_specs=[ANY] * n, out_shape=[_sds(s.shape, F32) for s in sms],
        scratch_shapes=[pltpu.SemaphoreType.DMA((n,))] * 2)(*sms)


def _small_allreduce(v, after=None):
    rows = v.shape[0]
    ndev = 8
    behind = [] if after is None else [after]

    def body(v_ref, *rest):
        o_ref, gat_ref, send, recv = rest[len(behind):]
        x, y, c, _ = _place()
        me = 4 * x + 2 * y + c
        cps = []
        for k in range(1, ndev):
            to = (me + k) % ndev
            cp = pltpu.make_async_remote_copy(src_ref=v_ref, dst_ref=gat_ref.at[me], send_sem=send.at[k - 1],
                                              recv_sem=recv.at[me], device_id=(to // 4, (to // 2) % 2, to % 2),
                                              device_id_type=DEV)
            cp.start()
            cps.append(cp)
        gat_ref[me] = v_ref[...]
        for k in range(1, ndev):
            frm = (me + k) % ndev
            pltpu.make_async_remote_copy(src_ref=v_ref, dst_ref=gat_ref.at[frm], send_sem=send.at[k - 1],
                                         recv_sem=recv.at[frm], device_id=(x, y, c), device_id_type=DEV).wait_recv()
        for cp in cps:
            cp.wait_send()
        acc = gat_ref[0]
        for k in range(1, ndev):
            acc = acc + gat_ref[k]
        o_ref[...] = acc

    vm = pl.BlockSpec(memory_space=pltpu.VMEM)
    return pl.pallas_call(
        body, name="small_allreduce", in_specs=[vm] + [ANY] * len(behind), out_specs=vm,
        out_shape=_sds((rows, LANE), F32),
        scratch_shapes=[pltpu.VMEM((ndev, rows, LANE), F32), pltpu.SemaphoreType.DMA((ndev - 1,)),
                        pltpu.SemaphoreType.DMA((ndev,))])(v, *behind)


def _layout(d):
    half = d // 2
    names = [("aq", d), ("ak", d), ("av", d), ("rq", half), ("rk", half), ("rv", d), ("rg", d),
             ("gq", half), ("gk", half), ("gv", d), ("gg", d), ("gates", 3 * d), ("glr", 2 * LANE)]
    off, pos = {}, 0
    for nm, sz in names:
        off[nm] = pos
        pos += sz
    return off, pos


def _pad_cols(w, d):
    a = 8 * d + d
    lr = w[..., a:a + GATE_RANK]
    z = jnp.zeros(w.shape[:-1] + (2 * LANE - GATE_RANK,), w.dtype)
    return jnp.concatenate([w[..., :a], w[..., a + GATE_RANK:], lr, z], axis=-1)


def _unpad_cols(g, d):
    a = 8 * d + d
    return jnp.concatenate([g[..., :a], g[..., a + 3 * d:a + 3 * d + GATE_RANK], g[..., a:a + 3 * d]], axis=-1)


def kernel(x, ln_in_g, ln_in_b, w_in, rel_bias, gla_w_lr, gla_b_lr, gla_norm_g, w_branch, w_out, ln1_g, ln1_b, w_up, w_down, ln2_g, ln2_b, loss_target, m_ln_in_g, m_ln_in_b, m_w_in, m_rel_bias, m_gla_w_lr, m_gla_b_lr, m_gla_norm_g, m_w_branch, m_w_out, m_ln1_g, m_ln1_b, m_w_up, m_w_down, m_ln2_g, m_ln2_b, v_ln_in_g, v_ln_in_b, v_w_in, v_rel_bias, v_gla_w_lr, v_gla_b_lr, v_gla_norm_g, v_w_branch, v_w_out, v_ln1_g, v_ln1_b, v_w_up, v_w_down, v_ln2_g, v_ln2_b):
    t, d = x.shape[1], x.shape[2]
    dff = 4 * d
    half = d // 2
    off, npad = _layout(d)
    xi, yi, ci = (lax.axis_index(a) for a in MESH_AXES)
    chip = 2 * xi + yi
    csel = jnp.reshape(ci, (1,)).astype(jnp.int32)
    psel = jnp.reshape(chip, (1,)).astype(jnp.int32)

    big_w = [w_in, w_branch.reshape(DEPTH, -1, d), w_out, w_up, w_down]
    big_m = [m_w_in, m_w_branch.reshape(DEPTH, -1, d), m_w_out, m_w_up, m_w_down]
    big_v = [v_w_in, v_w_branch.reshape(DEPTH, -1, d), v_w_out, v_w_up, v_w_down]
    W_IN, REST = [0], [1, 2, 3, 4]

    def shards_of(l, idx):
        return [big_w[i][l].astype(BF16) for i in idx]

    def lands_of(srcs):
        return [_sds((4,) + s.shape, s.dtype) for s in srcs]

    def full_w_in(g):
        return _pad_cols(jnp.transpose(g, (1, 0, 2)).reshape(d, -1), d)

    def full_rest(gs):
        g_br, g_out, g_up, g_down = gs
        return (jnp.transpose(g_br.reshape(4, N_BRANCH, d // 4, d), (1, 0, 2, 3)).reshape(N_BRANCH, d, d),
                g_out.reshape(d, d), jnp.transpose(g_up, (1, 0, 2)).reshape(d, dff), g_down.reshape(dff, d))

    def with_own(srcs, lands):
        return [lax.dynamic_update_slice(g, s[None], (chip, 0, 0)) for s, g in zip(srcs, lands)]

    def gather_start(tag, l, idx, after):
        srcs = shards_of(l, idx)
        return srcs, _split_start(f"gather_{tag}{l}_start", srcs, lands_of(srcs), _gather_copies, after)

    def gather_finish(tag, l, pending, after):
        srcs, started = pending
        _, lands = _split_wait(f"gather_{tag}{l}_wait", started, _gather_copies, after)
        return with_own(srcs, _gather_forward(f"gather_{tag}{l}_pass", lands))

    def token(pending):
        return pending[1][4][0, 0]

    win, wbr, wout, wup, wdown = ([None] * DEPTH for _ in range(5))
    src_first = shards_of(0, W_IN)
    g_first = with_own(src_first, _gather_plain("gather_in0", src_first))
    win[0] = full_w_in(g_first[0])

    dkh = half // LIN_HEADS
    lr_rows = DEPTH * GATE_RANK
    lr_slab = jnp.zeros((lr_rows, 4, half // 4), F32)
    lr_slab = lax.dynamic_update_slice(lr_slab, (gla_w_lr.reshape(lr_rows, 1, half // 4) * jnp.where(ci == 0, 1.0, 0.0)),
                                       (0, chip, 0))
    wlr_full = _small_allreduce(lr_slab.reshape(-1, LANE)).reshape(DEPTH, GATE_RANK, half)
    wlr_pad = jnp.concatenate([wlr_full, jnp.zeros((DEPTH, LANE - GATE_RANK, half), F32)], axis=1)
    pend_rest = gather_start("rest", 0, REST, wlr_full[0, :1, :1] + g_first[0][0, :1, :1].astype(F32))

    inv = 10000.0 ** (-jnp.arange(0, dkh, 2, dtype=F32) / dkh)
    ang = jnp.arange(t, dtype=F32)[:, None] * inv[None, :]
    cos, sin = jnp.cos(ang), jnp.sin(ang)
    rope_c = jnp.concatenate([cos, cos], axis=1)
    rope_s = jnp.concatenate([-sin, sin], axis=1)
    log_gamma = jnp.log1p(-jnp.exp2(-5.0 - jnp.arange(LIN_HEADS, dtype=F32)))
    lg_tab = jnp.broadcast_to(log_gamma[:, None, None], (LIN_HEADS, 1, dkh))

    def vec(a):
        return a.reshape(1, -1)

    x0, x0b, xh_in, rs_in = _ln_in(x[0], vec(ln_in_g) + token(pend_rest), vec(ln_in_b))
    saved = []
    xl, xlb = x0, x0b
    for l in range(DEPTH):
        p = _mm("proj_in", xlb, win[l], 512, 1792)
        g_rest = gather_finish("rest", l, pend_rest, p)
        wbr[l], wout[l], wup[l], wdown[l] = full_rest(g_rest)
        tok = 0.0
        if l + 1 < DEPTH:
            pend_in = gather_start("in", l + 1, W_IN, g_rest[0])
            tok = token(pend_in)
        bias = _bias_expand(rel_bias[l] + tok)
        attn = _attn_fwd(p, bias, d, off)
        ret_aux = (rope_c, rope_s, lg_tab + tok)
        gla_aux = (p, wlr_pad[l], vec(gla_b_lr[l]) + tok, vec(gla_norm_g[l]))
        o_ret, b_ret, st_ret = _lin_fwd(False, p, ret_aux, d, off)
        o_gla, b_gla, st_gla = _lin_fwd(True, p, gla_aux, d, off)
        tok = 0.0
        if l + 1 < DEPTH:
            g_in = gather_finish("in", l + 1, pend_in, b_gla)
            win[l + 1] = full_w_in(g_in[0])
            pend_rest = gather_start("rest", l + 1, REST, g_in[0])
            tok = token(pend_rest)
        bo = jnp.stack([attn, b_ret, b_gla])
        proj, merged = _merge_fwd(bo, wbr[l], p, off["gates"])
        x1, x1b, xh1, rs1 = _mm_res_ln("out_proj_ln", merged, wout[l], xl, vec(ln1_g[l]) + tok, vec(ln1_b[l]),
                                       256, False)
        u = _mm("mlp_up", x1b, wup[l], 1024, 1024)
        x2, x2b, xh2, rs2, act = _mm_res_ln("mlp_down_ln", u, wdown[l], x1, vec(ln2_g[l]), vec(ln2_b[l]), 256, True)
        saved.append(dict(xlb=xlb, p=p, bias=bias, ret_aux=ret_aux, gla_aux=gla_aux, o_ret=o_ret, o_gla=o_gla,
                          st_ret=st_ret, st_gla=st_gla, bo=bo, proj=proj, merged=merged, x1b=x1b, xh1=xh1,
                          rs1=rs1, u=u, xh2=xh2, rs2=rs2, act=act))
        xl, xlb = x2, x2b

    small = {}
    last = saved[-1]
    loss_p, dz2, dz2b, dg, db = _loss_ln_bwd(xl, loss_target[0], last["xh2"], last["rs2"], vec(ln2_g[DEPTH - 1]))
    small["loss"] = loss_p[:, :1]
    grad_x = None

    def scatter_start(tag, l, idx, shards, after):
        theirs = _sibling_halves(f"grad_{tag}{l}_sibling", shards)
        hs = [_add_half("grad_sibling_add", g, th, csel) for g, th in zip(shards, theirs)]
        lands = [_sds((3,) + h.shape[1:], F32) for h in hs]
        return tag, l, idx, _split_start(f"grad_{tag}{l}_scatter_start", hs, lands, _scatter_copies, after)

    adam_out = [None] * len(big_w)

    def scatter_finish(pending, after):
        tag, l, idx, started = pending
        hs, rcv = _split_wait(f"grad_{tag}{l}_scatter_wait", started, _scatter_copies, after)
        sms = [_sum_shards("grad_chip_sum", h, r, psel) for h, r in zip(hs, rcv)]
        for i, own, sib in zip(idx, sms, _sibling_share(f"grad_{tag}{l}_share", sms)):
            adam_out[i] = _adamw_layer("adamw_large", big_w[i], own, sib, csel, big_m[i], big_v[i], l, adam_out[i])
        return adam_out[idx[0]][0]

    in_flight = []

    def scatter(tag, l, idx, shards, after=None):
        pending = scatter_start(tag, l, idx, shards, after)
        in_flight.append(pending)
        if len(in_flight) > 3:
            scatter_finish(in_flight.pop(0), pending[3][4])
        return pending[3][4][0, 0]

    for l in reversed(range(DEPTH)):
        s = saved[l]
        small[("ln2_g", l)], small[("ln2_b", l)] = dg, db
        du = _mm_nt_relu2_bwd(dz2b, wdown[l], s["u"])
        g_wdown = _mm_tn("grad_w_down", s["act"], dz2b, 512, 512)
        g_wup = _mm_tn("grad_w_up", s["x1b"], du, 512, 512)
        dz1, dz1b, dg1, db1 = _mm_nt_res_lnbwd("mlp_up_bwd_ln", du, wup[l], dz2, s["xh1"], s["rs1"],
                                               vec(ln1_g[l]), 256, dff)
        small[("ln1_g", l)], small[("ln1_b", l)] = dg1, db1
        dproj, dgl = _merge_bwd(dz1b, wout[l], s["proj"], s["p"], off["gates"])
        g_wout = _mm_tn("grad_w_out", s["merged"], dz1b, 512, 512)
        dbo = _mm("branch_proj_bwd", dproj, wbr[l], 1024, 1024, nt=True)
        g_wbr = _mm_tn("grad_w_branch", s["bo"], dproj, 512, 512)
        tok = scatter("rest", l, REST, [
            jnp.transpose(g_wbr.reshape(N_BRANCH, 4, d // 4, d), (1, 0, 2, 3)).reshape(4, -1, d),
            g_wout.reshape(4, d // 4, d), jnp.transpose(g_wup.reshape(d, 4, d), (1, 0, 2)), g_wdown.reshape(4, d, d)])
        rc, rs_, lg = s["ret_aux"]
        gp, gw, gb, gn_ = s["gla_aux"]
        dq_a, dk_acc, dv_acc, dbias = _attn_bwd(s["p"], s["bias"] + tok, dbo[0], d, off)
        small[("rel_bias", l)] = _bias_reduce(dbias)
        dk_a = dk_acc[2 * QB:].astype(BF16)
        dv_a = dv_acc[2 * QB:].astype(BF16)
        dq_r, dk_r, dv_r, dg_r = _lin_bwd(False, s["p"], (rc, rs_, lg + tok), s["o_ret"], s["st_ret"], dbo[1], d, off)
        dq_g, dk_g, dv_g, dg_g, dpre, dblr, dgn = _lin_bwd(True, s["p"], (gp, gw, gb + tok, gn_), s["o_gla"],
                                                           s["st_gla"], dbo[2], d, off)
        small[("gla_b_lr", l)] = dblr.reshape(1, half)
        small[("gla_norm_g", l)] = jnp.sum(dgn, axis=0)
        dpre_b = dpre.astype(BF16)
        glr_b = s["p"][:, off["glr"]:off["glr"] + LANE].astype(BF16)
        dglr = _mm("gate_lr_bwd", dpre_b, wlr_pad[l], 512, LANE, nt=True, out_dtype=BF16)
        small[("gla_w_lr", l)] = _mm_tn("grad_gla_w_lr", glr_b, dpre_b, LANE, half)[:GATE_RANK]
        dp = jnp.concatenate([dq_a, dk_a, dv_a, dq_r, dk_r, dv_r, dg_r, dq_g, dk_g, dv_g, dg_g,
                              dgl[0], dgl[1], dgl[2], dglr, jnp.zeros((t, LANE), BF16)], axis=1)
        if l > 0:
            prev = saved[l - 1]
            xh_p, rs_p, g_p = prev["xh2"], prev["rs2"], vec(ln2_g[l - 1])
        else:
            xh_p, rs_p, g_p = xh_in, rs_in, vec(ln_in_g)
        g_win = _mm_tn("grad_w_in", s["xlb"], dp, 1024, 896)
        tok = scatter("in", l, W_IN, [jnp.transpose(_unpad_cols(g_win, d).reshape(d, 4, -1), (1, 0, 2))])
        dzp, dzpb, dg, db = _mm_nt_res_lnbwd("proj_in_bwd_ln", dp, win[l], dz1, xh_p, rs_p, g_p + tok, 1024, 1792)
        dz2, dz2b = dzp, dzpb
        grad_x = dzp
    after = grad_x
    while in_flight:
        after = scatter_finish(in_flight.pop(0), after)
    small["ln_in_g"], small["ln_in_b"] = dg, db
    rb_pad = 3 * LANE
    pieces = [small["loss"].reshape(-1), jnp.zeros((LANE - 1,), F32), small["ln_in_g"].reshape(-1),
              small["ln_in_b"].reshape(-1)]
    for l in range(DEPTH):
        rb = jnp.pad(small[("rel_bias", l)], ((0, 0), (0, rb_pad - (2 * REL_CLIP + 1))))
        pieces += [rb.reshape(-1), small[("gla_w_lr", l)].reshape(-1), small[("gla_b_lr", l)].reshape(-1),
                   small[("gla_norm_g", l)].reshape(-1), small[("ln1_g", l)].reshape(-1),
                   small[("ln1_b", l)].reshape(-1), small[("ln2_g", l)].reshape(-1), small[("ln2_b", l)].reshape(-1)]
    sizes = [pc.shape[0] for pc in pieces]
    packed = jnp.concatenate(pieces)
    padn = (-packed.shape[0]) % (8 * LANE)
    packed = jnp.concatenate([packed, jnp.zeros((padn,), F32)]).reshape(-1, LANE)
    red = _small_allreduce(packed, after).reshape(-1)

    parts, pos = [], 0
    for sz in sizes:
        parts.append(red[pos:pos + sz])
        pos += sz
    loss = parts[0][0]
    g_ln_in_g, g_ln_in_b = parts[2], parts[3]
    per = 8
    g_rel = jnp.stack([parts[4 + per * l].reshape(ATTN_HEADS, rb_pad)[:, :2 * REL_CLIP + 1] for l in range(DEPTH)])
    g_wlr_full = jnp.stack([parts[5 + per * l].reshape(GATE_RANK, half) for l in range(DEPTH)])
    g_wlr = lax.dynamic_slice_in_dim(g_wlr_full, chip * (half // 4), half // 4, axis=2)
    g_blr = jnp.stack([parts[6 + per * l] for l in range(DEPTH)])
    g_gn = jnp.stack([parts[7 + per * l] for l in range(DEPTH)])
    g_ln1g = jnp.stack([parts[8 + per * l] for l in range(DEPTH)])
    g_ln1b = jnp.stack([parts[9 + per * l] for l in range(DEPTH)])
    g_ln2g = jnp.stack([parts[10 + per * l] for l in range(DEPTH)])
    g_ln2b = jnp.stack([parts[11 + per * l] for l in range(DEPTH)])

    grads = [g_ln_in_g, g_ln_in_b, None, g_rel, g_wlr, g_blr, g_gn, None, None, g_ln1g, g_ln1b, None, None,
             g_ln2g, g_ln2b]
    ws = [ln_in_g, ln_in_b, w_in, rel_bias, gla_w_lr, gla_b_lr, gla_norm_g, w_branch, w_out, ln1_g, ln1_b,
          w_up, w_down, ln2_g, ln2_b]
    ms = [m_ln_in_g, m_ln_in_b, m_w_in, m_rel_bias, m_gla_w_lr, m_gla_b_lr, m_gla_norm_g, m_w_branch, m_w_out,
          m_ln1_g, m_ln1_b, m_w_up, m_w_down, m_ln2_g, m_ln2_b]
    vs = [v_ln_in_g, v_ln_in_b, v_w_in, v_rel_bias, v_gla_w_lr, v_gla_b_lr, v_gla_norm_g, v_w_branch, v_w_out,
          v_ln1_g, v_ln1_b, v_w_up, v_w_down, v_ln2_g, v_ln2_b]

    deltas, new_ms, new_vs = [None] * 15, [None] * 15, [None] * 15
    big_idx = [2, 7, 8, 11, 12]
    for i, res in zip(big_idx, adam_out):
        shp = ws[i].shape
        grads[i], deltas[i], new_ms[i], new_vs[i] = (r.reshape(shp) for r in res)
    small_idx = [i for i in range(15) if i not in big_idx]

    def pack(arrs):
        flat_ = jnp.concatenate([arrs[i].reshape(-1) for i in small_idx])
        pad_ = (-flat_.shape[0]) % (8 * LANE)
        return jnp.concatenate([flat_, jnp.ones((pad_,), F32)]).reshape(-1, LANE)

    dl, nm, nv = _adamw("adamw_small", pack(ws), pack(grads), pack(ms), pack(vs))
    pos = 0
    for i in small_idx:
        sz = int(np.prod(ws[i].shape))
        deltas[i] = dl.reshape(-1)[pos:pos + sz].reshape(ws[i].shape)
        new_ms[i] = nm.reshape(-1)[pos:pos + sz].reshape(ws[i].shape)
        new_vs[i] = nv.reshape(-1)[pos:pos + sz].reshape(ws[i].shape)
        pos += sz

    return (loss, grad_x[None], *grads, *deltas, *new_ms, *new_vs)
```

```python
import functools

import numpy as np
import jax
import jax.numpy as jnp
from jax import lax
from jax.experimental import pallas as pl
from jax.experimental.pallas import tpu as pltpu

F32 = jnp.float32
BF16 = jnp.bfloat16
MXU_DTYPE = BF16
HI = lax.Precision.HIGHEST

DEPTH = 2
CHUNK = 64
N_BRANCH = 3
ATTN_HEADS = 8
ATTN_LEFT = 8
REL_CLIP = 2 * CHUNK
LIN_HEADS = 4
GATE_RANK = 16
GATE_NORM = 16.0
LN_EPS = 1e-5
NEG_INF = -1e30
ALPHA = (2 * DEPTH) ** 0.25
ADAM_LR, ADAM_B1, ADAM_B2, ADAM_EPS, ADAM_WD, ADAM_STEP = 0.001, 0.9, 0.999, 1e-08, 0.01, 10

LANE = 128
VMEM_LIMIT = 56 << 20
QB = 256
KW = 3 * QB
LB = 256
MESH_AXES = ("x", "y", "c")
DEV = pl.DeviceIdType.MESH


def _cp(sem):
    return pltpu.CompilerParams(dimension_semantics=sem, vmem_limit_bytes=VMEM_LIMIT)


def _mx(v):
    return v.astype(MXU_DTYPE)


def _dot(a, b):
    return jnp.dot(_mx(a), _mx(b), preferred_element_type=F32)


def _dot_nt(a, b):
    return lax.dot_general(_mx(a), _mx(b), (((1,), (1,)), ((), ())), preferred_element_type=F32)


def _dot_tn(a, b):
    return lax.dot_general(_mx(a), _mx(b), (((0,), (0,)), ((), ())), preferred_element_type=F32)


def _dot_hi(a, b):
    return jnp.dot(a, b, precision=HI, preferred_element_type=F32)


def _sigmoid(v):
    return 1.0 / (1.0 + jnp.exp(-v))


def _sds(shape, dtype):
    return jax.ShapeDtypeStruct(shape, dtype)


def _mm(name, a, b, tm, tn, nt=False, out_dtype=F32):
    batched = a.ndim == 3
    m, k = a.shape[-2:]
    n = b.shape[-2] if nt else b.shape[-1]
    tm, tn = min(tm, m), min(tn, n)

    def body(a_ref, b_ref, o_ref):
        f = _dot_nt if nt else _dot
        o_ref[...] = f(a_ref[...], b_ref[...]).astype(o_ref.dtype)

    rows_inner = (n // tn) * m < (m // tm) * n

    def ij(u, v):
        return (v, u) if rows_inner else (u, v)

    if batched:
        nb = a.shape[0]
        grid = (nb,) + ij(m // tm, n // tn)
        a_spec = pl.BlockSpec((None, tm, k), lambda g, u, v: (g, ij(u, v)[0], 0))
        b_spec = (pl.BlockSpec((None, tn, k), lambda g, u, v: (g, ij(u, v)[1], 0)) if nt
                  else pl.BlockSpec((None, k, tn), lambda g, u, v: (g, 0, ij(u, v)[1])))
        o_spec = pl.BlockSpec((None, tm, tn), lambda g, u, v: (g,) + ij(u, v))
        out_shape = _sds((nb, m, n), out_dtype)
        sem = ("parallel", "parallel", "parallel")
    else:
        grid = ij(m // tm, n // tn)
        a_spec = pl.BlockSpec((tm, k), lambda u, v: (ij(u, v)[0], 0))
        b_spec = (pl.BlockSpec((tn, k), lambda u, v: (ij(u, v)[1], 0)) if nt
                  else pl.BlockSpec((k, tn), lambda u, v: (0, ij(u, v)[1])))
        o_spec = pl.BlockSpec((tm, tn), lambda u, v: ij(u, v))
        out_shape = _sds((m, n), out_dtype)
        sem = ("parallel", "parallel")
    return pl.pallas_call(body, name=name, grid=grid, in_specs=[a_spec, b_spec], out_specs=o_spec,
                          out_shape=out_shape, compiler_params=_cp(sem))(a, b)


def _mm_tn(name, a, b, tm, tn, shard=None):
    batched = a.ndim == 3
    k, m = a.shape[-2:]
    n = b.shape[-1]
    tm, tn = min(tm, m), min(tn, n)

    def body(a_ref, b_ref, o_ref):
        o_ref[...] = lax.dot_general(_mx(a_ref[...]), _mx(b_ref[...]), (((0,), (0,)), ((), ())),
                                     preferred_element_type=F32)

    if batched:
        nb = a.shape[0]
        grid = (nb, m // tm, n // tn)
        a_spec = pl.BlockSpec((None, k, tm), lambda g, i, j: (g, 0, i))
        b_spec = pl.BlockSpec((None, k, tn), lambda g, i, j: (g, 0, j))
        if shard == "rows":
            assert 4 * tm == m
            o_spec = pl.BlockSpec((None, tm, tn), lambda g, i, j: (i, g, j))
            out_shape = _sds((4, nb * tm, n), F32)
        else:
            o_spec = pl.BlockSpec((None, tm, tn), lambda g, i, j: (g, i, j))
            out_shape = _sds((nb, m, n), F32)
    else:
        grid = (m // tm, n // tn)
        a_spec = pl.BlockSpec((k, tm), lambda i, j: (0, i))
        b_spec = pl.BlockSpec((k, tn), lambda i, j: (0, j))
        if shard == "cols":
            per = n // 4 // tn
            o_spec = pl.BlockSpec((None, tm, tn), lambda i, j: (j // per, i, j % per))
            out_shape = _sds((4, m, n // 4), F32)
        else:
            o_spec = pl.BlockSpec((tm, tn), lambda i, j: (i, j))
            out_shape = _sds((m, n), F32)
    return pl.pallas_call(body, name=name, grid=grid, in_specs=[a_spec, b_spec], out_specs=o_spec,
                          out_shape=out_shape, compiler_params=_cp(("parallel",) * len(grid)))(a, b)


def _ln_rows(y, g, b):
    mu = jnp.mean(y, axis=-1, keepdims=True)
    yc = y - mu
    var = jnp.mean(yc * yc, axis=-1, keepdims=True)
    rs = lax.rsqrt(var + LN_EPS)
    xh = yc * rs
    return xh * g + b, xh, rs


def _ln_in(x, g, b, tm=256):
    t, d = x.shape

    def body(x_ref, g_ref, b_ref, o_ref, ob_ref, xh_ref, rs_ref):
        o, xh, rs = _ln_rows(x_ref[...], g_ref[...], b_ref[...])
        o_ref[...] = o
        ob_ref[...] = o.astype(BF16)
        xh_ref[...] = xh
        rs_ref[...] = rs

    row = pl.BlockSpec((tm, d), lambda i: (i, 0))
    vec = pl.BlockSpec((1, d), lambda i: (0, 0))
    return pl.pallas_call(
        body, name="ln_in", grid=(t // tm,), in_specs=[row, vec, vec],
        out_specs=[row, row, row, pl.BlockSpec((tm, 1), lambda i: (i, 0))],
        out_shape=[_sds((t, d), F32), _sds((t, d), BF16), _sds((t, d), F32), _sds((t, 1), F32)],
        compiler_params=_cp(("parallel",)))(x, g, b)


def _mm_res_ln(name, a, w, res, g, b, tm, relu2):
    t, k = a.shape
    d = w.shape[1]

    def body(a_ref, w_ref, r_ref, g_ref, b_ref, o_ref, ob_ref, xh_ref, rs_ref, *act_ref):
        av = a_ref[...]
        if relu2:
            av = jnp.square(jnp.maximum(av, 0.0))
            act_ref[0][...] = av.astype(BF16)
        y = ALPHA * r_ref[...] + _dot(av, w_ref[...])
        o, xh, rs = _ln_rows(y, g_ref[...], b_ref[...])
        o_ref[...] = o
        ob_ref[...] = o.astype(BF16)
        xh_ref[...] = xh
        rs_ref[...] = rs

    row = pl.BlockSpec((tm, d), lambda i: (i, 0))
    vec = pl.BlockSpec((1, d), lambda i: (0, 0))
    arow = pl.BlockSpec((tm, k), lambda i: (i, 0))
    out_specs = [row, row, row, pl.BlockSpec((tm, 1), lambda i: (i, 0))]
    out_shape = [_sds((t, d), F32), _sds((t, d), BF16), _sds((t, d), F32), _sds((t, 1), F32)]
    if relu2:
        out_specs.append(arow)
        out_shape.append(_sds((t, k), BF16))
    return pl.pallas_call(
        body, name=name, grid=(t // tm,),
        in_specs=[arow, pl.BlockSpec((k, d), lambda i: (0, 0)), row, vec, vec],
        out_specs=out_specs, out_shape=out_shape, compiler_params=_cp(("parallel",)))(a, w, res, g, b)


def _merge_fwd(bo, wb, p, gate_off, tm=512, tn=512):
    _, t, d = bo.shape
    gb = gate_off // tn

    def body(bo_ref, wb_ref, g0, g1, g2, proj_ref, m_ref):
        acc = None
        for n, g_ref in enumerate((g0, g1, g2)):
            pr = _dot(bo_ref[n], wb_ref[n])
            proj_ref[n] = pr
            term = _sigmoid(g_ref[...]) * pr
            acc = term if acc is None else acc + term
        m_ref[...] = acc.astype(BF16)

    gspecs = [pl.BlockSpec((tm, tn), functools.partial(lambda i, j, n: (i, gb + n * (d // tn) + j), n=n))
              for n in range(3)]
    return pl.pallas_call(
        body, name="merge_fwd", grid=(t // tm, d // tn),
        in_specs=[pl.BlockSpec((3, tm, d), lambda i, j: (0, i, 0)),
                  pl.BlockSpec((3, d, tn), lambda i, j: (0, 0, j))] + gspecs,
        out_specs=[pl.BlockSpec((3, tm, tn), lambda i, j: (0, i, j)), pl.BlockSpec((tm, tn), lambda i, j: (i, j))],
        out_shape=[_sds((3, t, d), F32), _sds((t, d), BF16)],
        compiler_params=_cp(("parallel", "parallel")))(bo, wb, p, p, p)


def _merge_bwd(dz, wout, proj, p, gate_off, tm=512, tn=512):
    t, d = dz.shape
    gb = gate_off // tn

    def body(dz_ref, w_ref, proj_ref, g0, g1, g2, dproj_ref, dgl_ref):
        dm = _dot_nt(dz_ref[...], w_ref[...])
        for n, g_ref in enumerate((g0, g1, g2)):
            s = _sigmoid(g_ref[...])
            dproj_ref[n] = (dm * s).astype(BF16)
            dgl_ref[n] = (dm * proj_ref[n] * (s * (1.0 - s))).astype(BF16)

    gspecs = [pl.BlockSpec((tm, tn), functools.partial(lambda i, j, n: (i, gb + n * (d // tn) + j), n=n))
              for n in range(3)]
    dproj, dgl = pl.pallas_call(
        body, name="merge_bwd", grid=(t // tm, d // tn),
        in_specs=[pl.BlockSpec((tm, d), lambda i, j: (i, 0)), pl.BlockSpec((tn, d), lambda i, j: (j, 0)),
                  pl.BlockSpec((3, tm, tn), lambda i, j: (0, i, j))] + gspecs,
        out_specs=[pl.BlockSpec((3, tm, tn), lambda i, j: (0, i, j)),
                   pl.BlockSpec((3, tm, tn), lambda i, j: (0, i, j))],
        out_shape=[_sds((3, t, d), BF16), _sds((3, t, d), BF16)],
        compiler_params=_cp(("parallel", "parallel")))(dz, wout, proj, p, p, p)
    return dproj, dgl


def _mm_nt_relu2_bwd(dz, wdown, u, tm=512, tn=1024):
    t, d = dz.shape
    f = wdown.shape[0]

    def body(dz_ref, w_ref, u_ref, du_ref):
        da = _dot_nt(dz_ref[...], w_ref[...])
        du_ref[...] = (da * (2.0 * jnp.maximum(u_ref[...], 0.0))).astype(BF16)

    return pl.pallas_call(
        body, name="mlp_down_bwd", grid=(t // tm, f // tn),
        in_specs=[pl.BlockSpec((tm, d), lambda i, j: (i, 0)), pl.BlockSpec((tn, d), lambda i, j: (j, 0)),
                  pl.BlockSpec((tm, tn), lambda i, j: (i, j))],
        out_specs=pl.BlockSpec((tm, tn), lambda i, j: (i, j)), out_shape=_sds((t, f), BF16),
        compiler_params=_cp(("parallel", "parallel")))(dz, wdown, u)


def _ln_bwd_rows(dx, xh, rs, g):
    dxh = dx * g
    m1 = jnp.mean(dxh, axis=-1, keepdims=True)
    m2 = jnp.mean(dxh * xh, axis=-1, keepdims=True)
    return rs * (dxh - m1 - xh * m2)


def _mm_nt_res_lnbwd(name, a, w, dres, xh, rs, g, tm, tk):
    t, k = a.shape
    d = w.shape[0]
    nk = k // tk

    def body(a_ref, w_ref, dr_ref, xh_ref, rs_ref, g_ref, dz_ref, dzb_ref, dg_ref, db_ref, acc_ref):
        i, kk = pl.program_id(0), pl.program_id(1)

        @pl.when(kk == 0)
        def _():
            acc_ref[...] = ALPHA * dr_ref[...]

        acc_ref[...] += _dot_nt(a_ref[...], w_ref[...])

        @pl.when(jnp.logical_and(i == 0, kk == 0))
        def _():
            dg_ref[...] = jnp.zeros_like(dg_ref)
            db_ref[...] = jnp.zeros_like(db_ref)

        @pl.when(kk == nk - 1)
        def _():
            dx = acc_ref[...]
            xhv = xh_ref[...]
            dz = _ln_bwd_rows(dx, xhv, rs_ref[...], g_ref[...])
            dz_ref[...] = dz
            dzb_ref[...] = dz.astype(BF16)
            dg_ref[...] += jnp.sum(dx * xhv, axis=0, keepdims=True)
            db_ref[...] += jnp.sum(dx, axis=0, keepdims=True)

    row = pl.BlockSpec((tm, d), lambda i, kk: (i, 0))
    vec = pl.BlockSpec((1, d), lambda i, kk: (0, 0))
    return pl.pallas_call(
        body, name=name, grid=(t // tm, nk),
        in_specs=[pl.BlockSpec((tm, tk), lambda i, kk: (i, kk)), pl.BlockSpec((d, tk), lambda i, kk: (0, kk)),
                  row, row, pl.BlockSpec((tm, 1), lambda i, kk: (i, 0)), vec],
        out_specs=[row, row, vec, vec],
        out_shape=[_sds((t, d), F32), _sds((t, d), BF16), _sds((1, d), F32), _sds((1, d), F32)],
        scratch_shapes=[pltpu.VMEM((tm, d), F32)],
        compiler_params=_cp(("arbitrary", "arbitrary")))(a, w, dres, xh, rs, g)


def _loss_ln_bwd(x2, target, xh, rs, g, tm=256):
    t, d = x2.shape

    def body(x_ref, t_ref, xh_ref, rs_ref, g_ref, loss_ref, dz_ref, dzb_ref, dg_ref, db_ref):
        @pl.when(pl.program_id(0) == 0)
        def _():
            loss_ref[...] = jnp.zeros_like(loss_ref)
            dg_ref[...] = jnp.zeros_like(dg_ref)
            db_ref[...] = jnp.zeros_like(db_ref)

        err = x_ref[...] - t_ref[...]
        per_row = jnp.mean(err * err, axis=-1, keepdims=True)
        loss_ref[...] += 0.5 * jnp.sum(per_row, axis=0, keepdims=True)
        dx = err * (1.0 / d)
        xhv = xh_ref[...]
        dz = _ln_bwd_rows(dx, xhv, rs_ref[...], g_ref[...])
        dz_ref[...] = dz
        dzb_ref[...] = dz.astype(BF16)
        dg_ref[...] += jnp.sum(dx * xhv, axis=0, keepdims=True)
        db_ref[...] += jnp.sum(dx, axis=0, keepdims=True)

    row = pl.BlockSpec((tm, d), lambda i: (i, 0))
    vec = pl.BlockSpec((1, d), lambda i: (0, 0))
    return pl.pallas_call(
        body, name="loss_ln_bwd", grid=(t // tm,),
        in_specs=[row, row, row, pl.BlockSpec((tm, 1), lambda i: (i, 0)), vec],
        out_specs=[pl.BlockSpec((1, LANE), lambda i: (0, 0)), row, row, vec, vec],
        out_shape=[_sds((1, LANE), F32), _sds((t, d), F32), _sds((t, d), BF16), _sds((1, d), F32),
                   _sds((1, d), F32)],
        compiler_params=_cp(("arbitrary",)))(x2, target, xh, rs, g)


HPA = 2


def _attn_scores(q_ref, k_refs, bias_ref, i, dh, hh):
    cols = pl.ds(hh * dh, dh)
    q = q_ref[:, cols] * (dh ** -0.5)
    k = jnp.concatenate([r[:, cols] for r in k_refs], axis=0)
    s = _dot_nt(q, k) + bias_ref[hh]
    col = lax.broadcasted_iota(jnp.int32, s.shape, 1)
    s = jnp.where(col >= (2 - i) * QB, s, NEG_INF)
    m = jnp.max(s, axis=-1, keepdims=True)
    e = jnp.exp(s - m)
    return q, k, e / jnp.sum(e, axis=-1, keepdims=True)


def _attn_specs(dh, off):
    w = HPA * dh
    qcol, kcol, vcol = off["aq"] // w, off["ak"] // w, off["av"] // w
    q_spec = pl.BlockSpec((QB, w), lambda g, i: (i, qcol + g))
    k_specs = [pl.BlockSpec((QB, w), functools.partial(lambda g, i, j: (jnp.maximum(i - 2 + j, 0), kcol + g), j=j))
               for j in range(3)]
    v_specs = [pl.BlockSpec((QB, w), functools.partial(lambda g, i, j: (jnp.maximum(i - 2 + j, 0), vcol + g), j=j))
               for j in range(3)]
    bias_spec = pl.BlockSpec((HPA, QB, KW), lambda g, i: (g, 0, 0))
    return q_spec, k_specs, v_specs, bias_spec


def _attn_fwd(p, bias, d, off):
    t = p.shape[0]
    dh = d // ATTN_HEADS

    def body(q_ref, k0, k1, k2, v0, v1, v2, bias_ref, o_ref):
        for hh in range(HPA):
            cols = pl.ds(hh * dh, dh)
            _, _, pr = _attn_scores(q_ref, (k0, k1, k2), bias_ref, pl.program_id(1), dh, hh)
            v = jnp.concatenate([v0[:, cols], v1[:, cols], v2[:, cols]], axis=0)
            o_ref[:, cols] = _dot(pr, v).astype(o_ref.dtype)

    q_spec, k_specs, v_specs, bias_spec = _attn_specs(dh, off)
    return pl.pallas_call(
        body, name="attn_fwd", grid=(ATTN_HEADS // HPA, t // QB),
        in_specs=[q_spec] + k_specs + v_specs + [bias_spec],
        out_specs=pl.BlockSpec((QB, HPA * dh), lambda g, i: (i, g)), out_shape=_sds((t, d), BF16),
        compiler_params=_cp(("parallel", "parallel")))(p, p, p, p, p, p, p, bias)


def _attn_bwd(p, bias, do, d, off):
    t = p.shape[0]
    dh = d // ATTN_HEADS
    tp = t + 2 * QB

    def body(q_ref, k0, k1, k2, v0, v1, v2, bias_ref, do_ref, dq_ref, dk_ref, dv_ref, dbias_ref):
        i = pl.program_id(1)

        @pl.when(i == 0)
        def _():
            dk_ref[...] = jnp.zeros_like(dk_ref)
            dv_ref[...] = jnp.zeros_like(dv_ref)
            dbias_ref[...] = jnp.zeros_like(dbias_ref)

        rows = pl.ds(pl.multiple_of(i * QB, QB), KW)
        for hh in range(HPA):
            cols = pl.ds(hh * dh, dh)
            q, k, pr = _attn_scores(q_ref, (k0, k1, k2), bias_ref, i, dh, hh)
            v = jnp.concatenate([v0[:, cols], v1[:, cols], v2[:, cols]], axis=0)
            dov = do_ref[:, cols]
            dp = _dot_nt(dov, v)
            delta = jnp.sum(pr * dp, axis=-1, keepdims=True)
            ds = pr * (dp - delta)
            dbias_ref[hh] += ds
            dq_ref[:, cols] = (_dot(ds, k) * (dh ** -0.5)).astype(dq_ref.dtype)
            dk_ref[rows, cols] += _dot_tn(ds, q)
            dv_ref[rows, cols] += _dot_tn(pr, dov)

    q_spec, k_specs, v_specs, bias_spec = _attn_specs(dh, off)
    row_spec = pl.BlockSpec((QB, HPA * dh), lambda g, i: (i, g))
    acc_spec = pl.BlockSpec((tp, HPA * dh), lambda g, i: (0, g))
    return pl.pallas_call(
        body, name="attn_bwd", grid=(ATTN_HEADS // HPA, t // QB),
        in_specs=[q_spec] + k_specs + v_specs + [bias_spec, row_spec],
        out_specs=[row_spec, acc_spec, acc_spec, bias_spec],
        out_shape=[_sds((t, d), BF16), _sds((tp, d), F32), _sds((tp, d), F32),
                   _sds((ATTN_HEADS, QB, KW), F32)],
        compiler_params=_cp(("parallel", "arbitrary")))(p, p, p, p, p, p, p, bias, do)


def _onehot_mm(name, a, b):
    def body(a_ref, b_ref, o_ref):
        o_ref[...] = _dot_hi(a_ref[...], b_ref[...])

    return pl.pallas_call(body, name=name, out_shape=_sds((a.shape[0], b.shape[1]), F32),
                          compiler_params=pltpu.CompilerParams(vmem_limit_bytes=VMEM_LIMIT))(a, b)


def _diag_index():
    ii, jj = np.arange(CHUNK)[:, None], np.arange(CHUNK)[None, :]
    return (ii - jj + CHUNK - 1).reshape(-1)


def _bias_expand(rel_bias):
    h = rel_bias.shape[0]
    nq, nk, shift = QB // CHUNK, KW // CHUNK, (2 * QB) // CHUNK
    nbin, ndc = 3 * LANE, 4
    rb = jnp.pad(rel_bias, ((0, 0), (0, nbin - rel_bias.shape[1])))
    win = np.clip(CHUNK * np.arange(ndc)[:, None] + np.arange(LANE)[None, :] - (CHUNK - 1), -REL_CLIP, REL_CLIP)
    sel = (jnp.arange(nbin)[:, None] == jnp.asarray((win + REL_CLIP).reshape(1, -1))).astype(F32)
    windows = _onehot_mm("bias_windows", rb, sel)
    diag_t = (jnp.arange(LANE)[:, None] == jnp.asarray(_diag_index().reshape(1, -1))).astype(F32)
    blocks = _onehot_mm("bias_blocks", windows.reshape(h * ndc, LANE), diag_t).reshape(h, ndc, CHUNK, CHUNK)
    off_band = jnp.full((h, CHUNK, CHUNK), NEG_INF, F32)
    rows = []
    for ic in range(nq):
        dcs = [ic - jc + shift for jc in range(nk)]
        rows.append(jnp.concatenate([blocks[:, min(dc, ndc - 1)] if 0 <= dc <= ATTN_LEFT else off_band
                                     for dc in dcs], axis=2))
    return jnp.concatenate(rows, axis=1)


def _bias_reduce(dbias):
    h = dbias.shape[0]
    nq, nk = QB // CHUNK, KW // CHUNK
    nbin = 3 * LANE
    blocks = dbias.reshape(h, nq, CHUNK, nk, CHUNK).transpose(0, 1, 3, 2, 4).reshape(h * nq * nk, CHUNK * CHUNK)
    diag = (jnp.asarray(_diag_index().reshape(-1, 1)) == jnp.arange(LANE)[None, :]).astype(F32)
    ic = np.arange(nq)[:, None, None]
    jc = np.arange(nk)[None, :, None]
    dl = np.arange(LANE)[None, None, :] - (CHUNK - 1)
    rel = np.clip(CHUNK * (ic - jc + (2 * QB) // CHUNK) + dl, -REL_CLIP, REL_CLIP) + REL_CLIP
    bins = (jnp.asarray(rel.reshape(-1, 1)) == jnp.arange(nbin)[None, :]).astype(F32)

    diags = _onehot_mm("bias_diag_sums", blocks, diag)
    out = _onehot_mm("bias_bin_sums", diags.reshape(h, nq * nk * LANE), bins)
    return out[:, :2 * REL_CLIP + 1]


def _chunk_masks():
    r = lax.broadcasted_iota(jnp.int32, (LB, LB), 0)
    c = lax.broadcasted_iota(jnp.int32, (LB, LB), 1)
    return (r // CHUNK) == (c // CHUNK), r >= c, r <= c


def _chunks(a):
    return [a[c * CHUNK:(c + 1) * CHUNK] for c in range(LB // CHUNK)]


def _per_chunk(a, f):
    return jnp.concatenate([jnp.broadcast_to(f(c), c.shape) for c in _chunks(a)], axis=0)


def _dot_sel(sel, x):
    def top(v):
        return lax.bitcast_convert_type(lax.bitcast_convert_type(v, jnp.int32) & jnp.int32(-65536), F32)

    hi = top(x)
    mid = top(x - hi)
    lo = (x - hi) - mid
    d = functools.partial(jnp.dot, sel.astype(jnp.bfloat16), preferred_element_type=F32)
    return d(hi.astype(jnp.bfloat16)) + d(mid.astype(jnp.bfloat16)) + d(lo.astype(jnp.bfloat16))


def _lin_block(gla, q, k, v, aux):
    dk = q.shape[-1]
    same, low, up = _chunk_masks()
    ones = same.astype(F32)
    if gla:
        glr, wlr, blr = aux
        q = q * (dk ** -0.5)
        pre = _dot(glr, wlr) + blr
        log_a = (jnp.minimum(pre, 0.0) - jnp.log(1.0 + jnp.exp(-jnp.abs(pre)))) / GATE_NORM
        b = _dot_sel(jnp.where(low, ones, 0.0), log_a)
        lastb = _per_chunk(b, lambda c: c[CHUNK - 1:])
    else:
        cs, sn, lg = aux
        pre = None
        half = dk // 2
        q = q * cs + pltpu.roll(q, half, 1) * sn
        k = (k * cs + pltpu.roll(k, half, 1) * sn) * (dk ** -0.5)
        pos = (lax.broadcasted_iota(jnp.int32, (LB, dk), 0) % CHUNK).astype(F32) + 1.0
        b = pos * lg
        lastb = jnp.broadcast_to(float(CHUNK) * lg, b.shape)
    eb, enb, el, dec = jnp.exp(b), jnp.exp(-b), jnp.exp(lastb - b), jnp.exp(lastb)
    qf, kf, qb, kb, kl = q * eb, k * enb, q * enb, k * eb, k * el
    s = jnp.where(same, jnp.where(low, _dot_nt(qf, kf), _dot_nt(qb, kb)), 0.0)
    return dict(pre=pre, eb=eb, enb=enb, el=el, dec=dec, qf=qf, kf=kf, qb=qb, kb=kb, kl=kl, s=s,
                same=same, low=low, up=up, ones=ones)


def _lin_norm_gate(gla, o, gate, gn):
    sg = _sigmoid(gate)
    silu = gate * sg
    if gla:
        r = lax.rsqrt(jnp.mean(o * o, axis=-1, keepdims=True) + LN_EPS)
        hn = o * r
        return silu * (hn * gn), (sg, silu, r, hn)
    mu = jnp.mean(o, axis=-1, keepdims=True)
    oc = o - mu
    r = lax.rsqrt(jnp.mean(oc * oc, axis=-1, keepdims=True) + LN_EPS)
    hn = oc * r
    return silu * hn, (sg, silu, r, hn)


HPS = 2


def _lin_specs(gla, dk, dv, off, rev, nb):
    pre = "g" if gla else "r"
    wk, wv = HPS * dk, HPS * dv
    qc, kc, vc, gc = (off[pre + "q"] // wk, off[pre + "k"] // wk, off[pre + "v"] // wv, off[pre + "g"] // wv)

    def blk(i):
        return nb - 1 - i if rev else i

    specs = [pl.BlockSpec((LB, wk), lambda g, i: (blk(i), qc + g)),
             pl.BlockSpec((LB, wk), lambda g, i: (blk(i), kc + g)),
             pl.BlockSpec((LB, wv), lambda g, i: (blk(i), vc + g)),
             pl.BlockSpec((LB, wv), lambda g, i: (blk(i), gc + g))]
    if gla:
        specs += [pl.BlockSpec((LB, LANE), lambda g, i: (blk(i), off["glr"] // LANE)),
                  pl.BlockSpec((LANE, wk), lambda g, i: (0, g)),
                  pl.BlockSpec((1, wk), lambda g, i: (0, g)),
                  pl.BlockSpec((1, dv), lambda g, i: (0, 0))]
    else:
        specs += [pl.BlockSpec((LB, dk), lambda g, i: (blk(i), 0)),
                  pl.BlockSpec((LB, dk), lambda g, i: (blk(i), 0)),
                  pl.BlockSpec((HPS, 1, dk), lambda g, i: (g, 0, 0))]
    return specs, blk


def _lin_aux(gla, refs, rows, hh, dk):
    if gla:
        glr_ref, wlr_ref, blr_ref, gn_ref = refs
        kcols = pl.ds(hh * dk, dk)
        return (glr_ref[rows, :], wlr_ref[:, kcols], blr_ref[:, kcols]), gn_ref[...]
    cs_ref, sn_ref, lg_ref = refs
    return (cs_ref[rows, :], sn_ref[rows, :], lg_ref[hh]), None


def _lin_fwd(gla, p, aux_arrays, d, off):
    t = p.shape[0]
    dk, dv = d // (2 * LIN_HEADS), d // LIN_HEADS
    nb, cb = t // LB, LB // CHUNK
    naux = len(aux_arrays)

    def body(*refs):
        q_ref, k_ref, v_ref, g_ref = refs[:4]
        aux_refs = refs[4:4 + naux]
        o_ref, bo_ref, st_out_ref, st_ref = refs[4 + naux:]

        @pl.when(pl.program_id(1) == 0)
        def _():
            st_ref[...] = jnp.zeros_like(st_ref)

        rows = slice(None)
        for hh in range(HPS):
            kcols, vcols = pl.ds(hh * dk, dk), pl.ds(hh * dv, dv)
            aux, gn = _lin_aux(gla, aux_refs, rows, hh, dk)
            v = v_ref[:, vcols]
            blk = _lin_block(gla, q_ref[:, kcols], k_ref[:, kcols], v, aux)
            st = st_ref[hh]
            inter = []
            for c, (qf, kl, dec, vc) in enumerate(zip(_chunks(blk["qf"]), _chunks(blk["kl"]), _chunks(blk["dec"]),
                                                      _chunks(v))):
                st_out_ref[hh, c] = st
                inter.append(_dot_nt(qf, st))
                st = st * dec[:1] + _dot_tn(vc, kl)
            st_ref[hh] = st
            o = _dot(blk["s"], v) + jnp.concatenate(inter, axis=0)
            o_ref[:, vcols] = o
            out, _ = _lin_norm_gate(gla, o, g_ref[:, vcols], gn)
            bo_ref[:, vcols] = out.astype(BF16)

    specs, _ = _lin_specs(gla, dk, dv, off, False, nb)
    orow = pl.BlockSpec((LB, HPS * dv), lambda g, i: (i, g))
    return pl.pallas_call(
        body, name="gla_fwd" if gla else "ret_fwd", grid=(LIN_HEADS // HPS, nb), in_specs=specs,
        out_specs=[orow, orow, pl.BlockSpec((HPS, cb, dv, dk), lambda g, i: (g, i, 0, 0))],
        out_shape=[_sds((t, d), F32), _sds((t, d), BF16), _sds((LIN_HEADS, t // CHUNK, dv, dk), F32)],
        scratch_shapes=[pltpu.VMEM((HPS, dv, dk), F32)],
        compiler_params=_cp(("parallel", "arbitrary")))(p, p, p, p, *aux_arrays)


def _lin_bwd(gla, p, aux_arrays, o, states, dbo, d, off):
    t = p.shape[0]
    dk, dv = d // (2 * LIN_HEADS), d // LIN_HEADS
    nb, cb = t // LB, LB // CHUNK
    naux = len(aux_arrays)

    def body(*refs):
        q_ref, k_ref, v_ref, g_ref = refs[:4]
        aux_refs = refs[4:4 + naux]
        o_ref, st_in_ref, dbo_ref = refs[4 + naux:7 + naux]
        outs = refs[7 + naux:]
        dq_ref, dk_ref, dv_ref, dg_ref = outs[:4]
        dst_ref = outs[-1]
        first = pl.program_id(1) == 0

        @pl.when(first)
        def _():
            dst_ref[...] = jnp.zeros_like(dst_ref)

        if gla:
            dpre_ref, dblr_ref, dgn_ref = outs[4:7]

            @pl.when(first)
            def _():
                dblr_ref[...] = jnp.zeros_like(dblr_ref)
                dgn_ref[...] = jnp.zeros_like(dgn_ref)

        rows = slice(None)
        for hh in range(HPS):
            kcols, vcols = pl.ds(hh * dk, dk), pl.ds(hh * dv, dv)
            aux, gn = _lin_aux(gla, aux_refs, rows, hh, dk)
            v = v_ref[:, vcols]
            bk = _lin_block(gla, q_ref[:, kcols], k_ref[:, kcols], v, aux)
            eb, enb, el, dec = bk["eb"], bk["enb"], bk["el"], bk["dec"]
            qf, kf, qb, kb, kl, s = bk["qf"], bk["kf"], bk["qb"], bk["kb"], bk["kl"], bk["s"]
            gate = g_ref[:, vcols]
            dout = dbo_ref[:, vcols]
            _, (sg, silu, r, hn) = _lin_norm_gate(gla, o_ref[:, vcols], gate, gn)
            dsilu = sg * (1.0 + gate * (1.0 - sg))
            if gla:
                y = hn * gn
                dy = dout * silu
                dg_ref[:, vcols] = (dout * y * dsilu).astype(BF16)
                dgn_ref[hh] += jnp.sum(dy * hn, axis=0, keepdims=True)
                dhn = dy * gn
                do = r * (dhn - hn * jnp.mean(dhn * hn, axis=-1, keepdims=True))
            else:
                dhn = dout * silu
                dg_ref[:, vcols] = (dout * hn * dsilu).astype(BF16)
                do = r * (dhn - jnp.mean(dhn, axis=-1, keepdims=True)
                          - hn * jnp.mean(dhn * hn, axis=-1, keepdims=True))
            ds = jnp.where(bk["same"], _dot_nt(do, v), 0.0)
            dsf = jnp.where(bk["low"], ds, 0.0)
            dsb = ds - dsf
            dvv = _dot_tn(s, do)
            dqf = _dot(dsf, kf)
            dkf = _dot_tn(dsf, qf)
            dqb = _dot(dsb, kb)
            dkb = _dot_tn(dsb, qb)
            dst = dst_ref[hh]
            dv_st, dqf_st, dkl_c, ddec_c = [], [], [], []
            parts = zip(reversed(range(cb)), reversed(_chunks(do)), reversed(_chunks(v)), reversed(_chunks(qf)),
                        reversed(_chunks(kl)), reversed(_chunks(dec)))
            for c, do_c, v_c, qf_c, kl_c, dec_c in parts:
                st = st_in_ref[hh, c]
                dv_st.append(_dot_nt(kl_c, dst))
                dkl_c.append(_dot(v_c, dst))
                dqf_st.append(_dot(do_c, st))
                ddec_c.append(jnp.broadcast_to(jnp.sum(dst * st, axis=0, keepdims=True), (CHUNK, dk)))
                dst = dst * dec_c[:1] + _dot_tn(do_c, qf_c)
            dst_ref[hh] = dst

            def cat(pieces):
                return jnp.concatenate(pieces[::-1], axis=0)

            dvv = dvv + cat(dv_st)
            dqf = dqf + cat(dqf_st)
            dkl = cat(dkl_c)
            dq = dqf * eb + dqb * enb
            dkk = dkf * enb + dkb * eb + dkl * el
            dv_ref[:, vcols] = dvv.astype(BF16)
            if gla:
                db = dqf * qf - dkf * kf - dqb * qb + dkb * kb - dkl * kl
                dlast = _per_chunk(dkl * kl, lambda c: jnp.sum(c, axis=0, keepdims=True)) + cat(ddec_c) * dec
                dlog_a = _dot_sel(jnp.where(bk["up"], bk["ones"], 0.0), db) + dlast
                dpre = dlog_a * (1.0 / GATE_NORM) * (1.0 - _sigmoid(bk["pre"]))
                dpre_ref[:, kcols] = dpre
                dblr_ref[hh] += jnp.sum(dpre, axis=0, keepdims=True)
                dq_ref[:, kcols] = (dq * (dk ** -0.5)).astype(BF16)
                dk_ref[:, kcols] = dkk.astype(BF16)
            else:
                cs, sn, _ = aux
                half = dk // 2
                dkk = dkk * (dk ** -0.5)
                dq_ref[:, kcols] = (dq * cs + pltpu.roll(dq * sn, half, 1)).astype(BF16)
                dk_ref[:, kcols] = (dkk * cs + pltpu.roll(dkk * sn, half, 1)).astype(BF16)

    specs, blk = _lin_specs(gla, dk, dv, off, True, nb)
    vrow = pl.BlockSpec((LB, HPS * dv), lambda g, i: (blk(i), g))
    krow = pl.BlockSpec((LB, HPS * dk), lambda g, i: (blk(i), g))
    specs += [vrow, pl.BlockSpec((HPS, cb, dv, dk), lambda g, i: (g, blk(i), 0, 0)), vrow]
    out_specs = [krow, krow, vrow, vrow]
    out_shape = [_sds((t, d // 2), BF16), _sds((t, d // 2), BF16), _sds((t, d), BF16), _sds((t, d), BF16)]
    if gla:
        out_specs += [krow, pl.BlockSpec((HPS, 1, dk), lambda g, i: (g, 0, 0)),
                      pl.BlockSpec((HPS, 1, dv), lambda g, i: (g, 0, 0))]
        out_shape += [_sds((t, d // 2), F32), _sds((LIN_HEADS, 1, dk), F32), _sds((LIN_HEADS, 1, dv), F32)]
    out_specs.append(pl.BlockSpec((HPS, dv, dk), lambda g, i: (g, 0, 0)))
    out_shape.append(_sds((LIN_HEADS, dv, dk), F32))
    res = pl.pallas_call(
        body, name="gla_bwd" if gla else "ret_bwd", grid=(LIN_HEADS // HPS, nb), in_specs=specs,
        out_specs=out_specs, out_shape=out_shape,
        compiler_params=_cp(("parallel", "arbitrary")))(p, p, p, p, *aux_arrays, o, states, dbo)
    return res[:-1]


def _row_tile(rows, cols):
    cap = max(8, (2 << 20) // (4 * cols))
    t = rows
    while t > cap and t % 2 == 0:
        t //= 2
    return t


def _add_half(name, g, t, sel):
    nchip, hr, cols = t.shape
    tr = _row_tile(hr, cols)
    nb = hr // tr

    def body(sel_ref, g_ref, t_ref, o_ref):
        o_ref[...] = g_ref[...] + t_ref[...]

    half = pl.BlockSpec((None, tr, cols), lambda p, i, s: (p, i, 0))
    gs = pltpu.PrefetchScalarGridSpec(
        num_scalar_prefetch=1, grid=(nchip, nb),
        in_specs=[pl.BlockSpec((None, tr, cols), lambda p, i, s: (p, s[0] * nb + i, 0)), half], out_specs=half)
    return pl.pallas_call(body, name=name, grid_spec=gs, out_shape=_sds(t.shape, F32),
                          compiler_params=_cp(("parallel", "parallel")))(sel, g, t)


def _sum_shards(name, h, rcv, sel):
    _, rows, cols = h.shape
    tr = _row_tile(rows, cols)

    def body(sel_ref, h_ref, r0, r1, r2, o_ref):
        o_ref[...] = ((h_ref[...] + r0[...]) + r1[...]) + r2[...]

    rspecs = [pl.BlockSpec((None, tr, cols), functools.partial(lambda i, s, j: (j, i, 0), j=j)) for j in range(3)]
    gs = pltpu.PrefetchScalarGridSpec(
        num_scalar_prefetch=1, grid=(rows // tr,),
        in_specs=[pl.BlockSpec((None, tr, cols), lambda i, s: (s[0], i, 0))] + rspecs,
        out_specs=pl.BlockSpec((tr, cols), lambda i, s: (i, 0)))
    return pl.pallas_call(body, name=name, grid_spec=gs, out_shape=_sds((rows, cols), F32),
                          compiler_params=_cp(("parallel",)))(sel, h, rcv, rcv, rcv)


def _adamw_math(w, g, m, v):
    c1 = 1.0 - ADAM_B1 ** ADAM_STEP
    c2 = 1.0 - ADAM_B2 ** ADAM_STEP
    nm = ADAM_B1 * m + (1.0 - ADAM_B1) * g
    nv = ADAM_B2 * v + (1.0 - ADAM_B2) * jnp.square(g)
    return -ADAM_LR * ((nm / c1) / (jnp.sqrt(nv / c2) + ADAM_EPS) + ADAM_WD * w), nm, nv


def _adamw(name, w, g, m, v):
    rows, cols = w.shape
    tr = _row_tile(rows, cols)

    def body(w_ref, g_ref, m_ref, v_ref, d_ref, nm_ref, nv_ref):
        d_ref[...], nm_ref[...], nv_ref[...] = _adamw_math(w_ref[...], g_ref[...], m_ref[...], v_ref[...])

    spec = pl.BlockSpec((tr, cols), lambda i: (i, 0))
    return pl.pallas_call(body, name=name, grid=(rows // tr,), in_specs=[spec] * 4, out_specs=[spec] * 3,
                          out_shape=[_sds((rows, cols), F32)] * 3, compiler_params=_cp(("parallel",)))(w, g, m, v)


def _adamw_layer(name, w, g_own, g_sib, sel, m, v, layer, prev):
    depth, rows, cols = w.shape
    tr = _row_tile(rows // 2, cols)
    nbh = rows // 2 // tr
    nprev = 0 if prev is None else 4

    def body(sel_ref, w_ref, own_ref, sib_ref, m_ref, v_ref, *rest):
        go_ref, d_ref, nm_ref, nv_ref = rest[nprev:]
        gv = jnp.where(pl.program_id(0) // nbh == sel_ref[0], own_ref[...], sib_ref[...])
        go_ref[...] = gv
        d_ref[...], nm_ref[...], nv_ref[...] = _adamw_math(w_ref[...], gv, m_ref[...], v_ref[...])

    lay = pl.BlockSpec((None, tr, cols), lambda i, s: (layer, i, 0))
    hlf = pl.BlockSpec((tr, cols), lambda i, s: (i % nbh, 0))
    gs = pltpu.PrefetchScalarGridSpec(
        num_scalar_prefetch=1, grid=(2 * nbh,), in_specs=[lay, hlf, hlf, lay, lay] + [ANY] * nprev,
        out_specs=[lay] * 4)
    args = (sel, w, g_own, g_sib, m, v) + (() if prev is None else tuple(prev))
    return pl.pallas_call(
        body, name=name, grid_spec=gs, out_shape=[_sds((depth, rows, cols), F32)] * 4,
        input_output_aliases={6 + k: k for k in range(nprev)},
        compiler_params=_cp(("parallel",)))(*args)


def _place():
    x, y, c = (lax.axis_index(a) for a in MESH_AXES)
    chips = [(1 - x, y), (x, 1 - y), (1 - x, 1 - y)]
    return x, y, c, chips


def _chip_index(xy):
    return 2 * xy[0] + xy[1]


ANY = pl.BlockSpec(memory_space=pl.ANY)


HBM_SPEC = pl.BlockSpec(memory_space=pltpu.HBM)
SEM = pl.BlockSpec(memory_space=pltpu.SEMAPHORE)
EFFECT = pltpu.SideEffectType.DATAFLOW_SIDE_EFFECTING


def _half(ref, c):
    hr = ref.shape[-2] // 2
    return pl.ds(pl.multiple_of(c * hr, 16), hr)


def _gather_copies(srcs, lands, send, recv):
    x, y, c, chips = _place()
    me = _chip_index((x, y))
    return [pltpu.make_async_remote_copy(src_ref=s.at[_half(s, c)], dst_ref=g.at[me, _half(s, c)],
                                         send_sem=send.at[3 * a + j], recv_sem=recv.at[3 * a + j],
                                         device_id=(*ch, c), device_id_type=DEV)
            for a, (s, g) in enumerate(zip(srcs, lands)) for j, ch in enumerate(chips)]


def _scatter_copies(srcs, lands, send, recv):
    x, y, c, chips = _place()
    return [pltpu.make_async_remote_copy(src_ref=h.at[_chip_index(ch)], dst_ref=r.at[j],
                                         send_sem=send.at[3 * a + j], recv_sem=recv.at[3 * a + j],
                                         device_id=(*ch, c), device_id_type=DEV)
            for a, (h, r) in enumerate(zip(srcs, lands)) for j, ch in enumerate(chips)]


def _in_hbm(a):
    return pltpu.with_memory_space_constraint(a, pltpu.HBM)


def _split_start(name, srcs, land_shapes, copies_fn, after=None):
    ns, nl = len(srcs), len(land_shapes)
    ncp = 3 * ns
    lands = [lax.empty(s.shape, s.dtype) for s in land_shapes]
    behind = [] if after is None else [after]

    def body(*refs):
        src, land = refs[:ns], refs[ns:ns + nl]
        send, recv = refs[ns + nl + len(behind)], refs[ns + nl + len(behind) + 1]
        for cp in copies_fn(src, land, send, recv):
            cp.start()
        refs[-1][...] = jnp.zeros_like(refs[-1])

    bufs = list(srcs) + lands
    outs = pl.pallas_call(
        body, name=name, in_specs=[HBM_SPEC] * (ns + nl) + [ANY] * len(behind),
        out_specs=[SEM, SEM] + [HBM_SPEC] * (ns + nl) + [pl.BlockSpec(memory_space=pltpu.VMEM)],
        out_shape=[pltpu.SemaphoreType.DMA((ncp,)), pltpu.SemaphoreType.DMA((ncp,))]
        + [pltpu.HBM(b.shape, b.dtype) for b in bufs] + [_sds((8, LANE), F32)],
        input_output_aliases={i: 2 + i for i in range(ns + nl)},
        compiler_params=pltpu.CompilerParams(has_side_effects=EFFECT))(*[_in_hbm(b) for b in bufs], *behind)
    return outs[0], outs[1], list(outs[2:2 + ns]), list(outs[2 + ns:2 + ns + nl]), outs[-1]


def _split_wait(name, started, copies_fn, after):
    send, recv, srcs, lands, _ = started
    ns, nl = len(srcs), len(lands)

    def body(*refs):
        src, land = refs[:ns], refs[ns:ns + nl]
        for cp in copies_fn(src, land, refs[ns + nl], refs[ns + nl + 1]):
            cp.wait_send()
            cp.wait_recv()

    bufs = list(srcs) + list(lands)
    outs = pl.pallas_call(
        body, name=name, in_specs=[HBM_SPEC] * (ns + nl) + [SEM, SEM, ANY], out_specs=[HBM_SPEC] * (ns + nl),
        out_shape=[pltpu.HBM(b.shape, b.dtype) for b in bufs],
        input_output_aliases={i: i for i in range(ns + nl)},
        compiler_params=pltpu.CompilerParams(has_side_effects=EFFECT))(*bufs, send, recv, after)
    return list(outs[:ns]), list(outs[ns:])


def _gather_plain(name, srcs):
    n = len(srcs)

    def body(*refs):
        src, land = refs[:n], refs[n:2 * n]
        send, recv, fsend, frecv = refs[2 * n:]
        first = _gather_copies(src, land, send, recv)
        for cp in first:
            cp.start()
        _forward_body(land, first, fsend, frecv)

    return pl.pallas_call(
        body, name=name, in_specs=[ANY] * n, out_specs=[ANY] * n,
        out_shape=[_sds((4,) + s.shape, s.dtype) for s in srcs],
        scratch_shapes=[pltpu.SemaphoreType.DMA((3 * n,))] * 4)(*srcs)


def _forward_body(land, arrivals, fsend, frecv):
    x, y, c, chips = _place()
    n = len(land)
    passed = []
    for a in range(n):
        for j, ch in enumerate(chips):
            if arrivals is not None:
                arrivals[3 * a + j].wait_recv()
            slot = land[a].at[_chip_index(ch), _half(land[a], c)]
            fw = pltpu.make_async_remote_copy(src_ref=slot, dst_ref=slot, send_sem=fsend.at[3 * a + j],
                                              recv_sem=frecv.at[3 * a + j], device_id=(x, y, 1 - c),
                                              device_id_type=DEV)
            fw.start()
            passed.append(fw)
    for a in range(n):
        for j, ch in enumerate(chips):
            slot = land[a].at[_chip_index(ch), _half(land[a], 1 - c)]
            pltpu.make_async_remote_copy(src_ref=slot, dst_ref=slot, send_sem=fsend.at[3 * a + j],
                                         recv_sem=frecv.at[3 * a + j], device_id=(x, y, c),
                                         device_id_type=DEV).wait_recv()
    for cp in passed:
        cp.wait_send()
    if arrivals is not None:
        for cp in arrivals:
            cp.wait_send()


def _gather_forward(name, lands):
    n = len(lands)

    def body(*refs):
        _forward_body(refs[n:2 * n], None, refs[2 * n], refs[2 * n + 1])

    return pl.pallas_call(
        body, name=name, in_specs=[ANY] * n, out_specs=[ANY] * n,
        out_shape=[_sds(g.shape, g.dtype) for g in lands], input_output_aliases={a: a for a in range(n)},
        scratch_shapes=[pltpu.SemaphoreType.DMA((3 * n,))] * 2)(*lands)


def _sibling_halves(name, grs):
    n = len(grs)

    def body(*refs):
        ins, outs = refs[:n], refs[n:2 * n]
        send, recv = refs[2 * n:]
        x, y, c, _ = _place()
        cps = [pltpu.make_async_remote_copy(src_ref=ins[a].at[:, _half(ins[a], 1 - c)], dst_ref=outs[a],
                                            send_sem=send.at[a], recv_sem=recv.at[a], device_id=(x, y, 1 - c),
                                            device_id_type=DEV) for a in range(n)]
        for cp in cps:
            cp.start()
        for cp in cps:
            cp.wait()

    return pl.pallas_call(
        body, name=name, in_specs=[ANY] * n, out_specs=[ANY] * n,
        out_shape=[_sds((g.shape[0], g.shape[1] // 2, g.shape[2]), F32) for g in grs],
        scratch_shapes=[pltpu.SemaphoreType.DMA((n,)), pltpu.SemaphoreType.DMA((n,))])(*grs)


def _sibling_share(name, sms):
    n = len(sms)

    def body(*refs):
        ins, outs = refs[:n], refs[n:2 * n]
        send, recv = refs[2 * n:]
        x, y, c, _ = _place()
        cps = [pltpu.make_async_remote_copy(src_ref=ins[a], dst_ref=outs[a], send_sem=send.at[a],
                                            recv_sem=recv.at[a], device_id=(x, y, 1 - c), device_id_type=DEV)
               for a in range(n)]
        for cp in cps:
            cp.start()
        for cp in cps:
            cp.wait()

    return pl.pallas_call(
        body, name=name, in_specs=[ANY] * n, out_specs=[ANY] * n, out_shape=[_sds(s.shape, F32) for s in sms],
        scratch_shapes=[pltpu.SemaphoreType.DMA((n,))] * 2)(*sms)


def _small_allreduce(v, after=None):
    rows = v.shape[0]
    ndev = 8
    behind = [] if after is None else [after]

    def body(v_ref, *rest):
        o_ref, gat_ref, send, recv = rest[len(behind):]
        x, y, c, _ = _place()
        me = 4 * x + 2 * y + c
        cps = []
        for k in range(1, ndev):
            to = (me + k) % ndev
            cp = pltpu.make_async_remote_copy(src_ref=v_ref, dst_ref=gat_ref.at[me], send_sem=send.at[k - 1],
                                              recv_sem=recv.at[me], device_id=(to // 4, (to // 2) % 2, to % 2),
                                              device_id_type=DEV)
            cp.start()
            cps.append(cp)
        gat_ref[me] = v_ref[...]
        for k in range(1, ndev):
            frm = (me + k) % ndev
            pltpu.make_async_remote_copy(src_ref=v_ref, dst_ref=gat_ref.at[frm], send_sem=send.at[k - 1],
                                         recv_sem=recv.at[frm], device_id=(x, y, c), device_id_type=DEV).wait_recv()
        for cp in cps:
            cp.wait_send()
        acc = gat_ref[0]
        for k in range(1, ndev):
            acc = acc + gat_ref[k]
        o_ref[...] = acc

    vm = pl.BlockSpec(memory_space=pltpu.VMEM)
    return pl.pallas_call(
        body, name="small_allreduce", in_specs=[vm] + [ANY] * len(behind), out_specs=vm,
        out_shape=_sds((rows, LANE), F32),
        scratch_shapes=[pltpu.VMEM((ndev, rows, LANE), F32), pltpu.SemaphoreType.DMA((ndev - 1,)),
                        pltpu.SemaphoreType.DMA((ndev,))])(v, *behind)


def _layout(d):
    half = d // 2
    names = [("aq", d), ("ak", d), ("av", d), ("rq", half), ("rk", half), ("rv", d), ("rg", d),
             ("gq", half), ("gk", half), ("gv", d), ("gg", d), ("gates", 3 * d), ("glr", 2 * LANE)]
    off, pos = {}, 0
    for nm, sz in names:
        off[nm] = pos
        pos += sz
    return off, pos


def _unpad_cols(g, d):
    a = 8 * d + d
    return jnp.concatenate([g[..., :a], g[..., a + 3 * d:a + 3 * d + GATE_RANK], g[..., a:a + 3 * d]], axis=-1)


def kernel(x, ln_in_g, ln_in_b, w_in, rel_bias, gla_w_lr, gla_b_lr, gla_norm_g, w_branch, w_out, ln1_g, ln1_b, w_up, w_down, ln2_g, ln2_b, loss_target, m_ln_in_g, m_ln_in_b, m_w_in, m_rel_bias, m_gla_w_lr, m_gla_b_lr, m_gla_norm_g, m_w_branch, m_w_out, m_ln1_g, m_ln1_b, m_w_up, m_w_down, m_ln2_g, m_ln2_b, v_ln_in_g, v_ln_in_b, v_w_in, v_rel_bias, v_gla_w_lr, v_gla_b_lr, v_gla_norm_g, v_w_branch, v_w_out, v_ln1_g, v_ln1_b, v_w_up, v_w_down, v_ln2_g, v_ln2_b):
    t, d = x.shape[1], x.shape[2]
    dff = 4 * d
    half = d // 2
    off, npad = _layout(d)
    xi, yi, ci = (lax.axis_index(a) for a in MESH_AXES)
    chip = 2 * xi + yi
    csel = jnp.reshape(ci, (1,)).astype(jnp.int32)
    psel = jnp.reshape(chip, (1,)).astype(jnp.int32)

    big_w = [w_in, w_branch.reshape(DEPTH, -1, d), w_out, w_up, w_down]
    big_m = [m_w_in, m_w_branch.reshape(DEPTH, -1, d), m_w_out, m_w_up, m_w_down]
    big_v = [v_w_in, v_w_branch.reshape(DEPTH, -1, d), v_w_out, v_w_up, v_w_down]
    W_IN, REST = [0], [1, 2, 3, 4]

    def shards_of(l, idx):
        return [big_w[i][l].astype(BF16) for i in idx]

    def lands_of(srcs):
        return [_sds((4,) + s.shape, s.dtype) for s in srcs]

    def full_w_in(g):
        per = g.shape[2]
        a = 8 * d + d

        def run(lo, hi):
            cuts = [(max(lo, c * per), min(hi, (c + 1) * per), c) for c in range(4)]
            return [g[c, :, x - c * per:y - c * per] for x, y, c in cuts if x < y]

        zeros = jnp.zeros((d, 2 * LANE - GATE_RANK), g.dtype)
        return jnp.concatenate(run(0, a) + run(a + GATE_RANK, 4 * per) + run(a, a + GATE_RANK) + [zeros], axis=1)

    def full_rest(gs):
        g_br, g_out, g_up, g_down = gs
        return (jnp.transpose(g_br.reshape(4, N_BRANCH, d // 4, d), (1, 0, 2, 3)).reshape(N_BRANCH, d, d),
                g_out.reshape(d, d), jnp.transpose(g_up, (1, 0, 2)).reshape(d, dff), g_down.reshape(dff, d))

    def with_own(srcs, lands):
        return [lax.dynamic_update_slice(g, s[None], (chip, 0, 0)) for s, g in zip(srcs, lands)]

    def gather_start(tag, l, idx, after):
        srcs = shards_of(l, idx)
        return srcs, _split_start(f"gather_{tag}{l}_start", srcs, lands_of(srcs), _gather_copies, after)

    def gather_finish(tag, l, pending, after):
        srcs, started = pending
        _, lands = _split_wait(f"gather_{tag}{l}_wait", started, _gather_copies, after)
        return with_own(srcs, _gather_forward(f"gather_{tag}{l}_pass", lands))

    def token(pending):
        return pending[1][4][0, 0]

    win, wbr, wout, wup, wdown = ([None] * DEPTH for _ in range(5))
    src_first = shards_of(0, W_IN)
    g_first = with_own(src_first, _gather_plain("gather_in0", src_first))
    win[0] = full_w_in(g_first[0])

    dkh = half // LIN_HEADS
    lr_rows = DEPTH * GATE_RANK
    lr_slab = jnp.zeros((lr_rows, 4, half // 4), F32)
    lr_slab = lax.dynamic_update_slice(lr_slab, (gla_w_lr.reshape(lr_rows, 1, half // 4) * jnp.where(ci == 0, 1.0, 0.0)),
                                       (0, chip, 0))
    wlr_full = _small_allreduce(lr_slab.reshape(-1, LANE)).reshape(DEPTH, GATE_RANK, half)
    wlr_pad = jnp.concatenate([wlr_full, jnp.zeros((DEPTH, LANE - GATE_RANK, half), F32)], axis=1)
    pend_rest = gather_start("rest", 0, REST, wlr_full[0, :1, :1] + g_first[0][0, :1, :1].astype(F32))

    inv = 10000.0 ** (-jnp.arange(0, dkh, 2, dtype=F32) / dkh)
    ang = jnp.arange(t, dtype=F32)[:, None] * inv[None, :]
    cos, sin = jnp.cos(ang), jnp.sin(ang)
    rope_c = jnp.concatenate([cos, cos], axis=1)
    rope_s = jnp.concatenate([-sin, sin], axis=1)
    log_gamma = jnp.log1p(-jnp.exp2(-5.0 - jnp.arange(LIN_HEADS, dtype=F32)))
    lg_tab = jnp.broadcast_to(log_gamma[:, None, None], (LIN_HEADS, 1, dkh))

    def vec(a):
        return a.reshape(1, -1)

    x0, x0b, xh_in, rs_in = _ln_in(x[0], vec(ln_in_g) + token(pend_rest), vec(ln_in_b))
    saved = []
    xl, xlb = x0, x0b
    for l in range(DEPTH):
        p = _mm("proj_in", xlb, win[l], 512, 1792)
        g_rest = gather_finish("rest", l, pend_rest, p)
        wbr[l], wout[l], wup[l], wdown[l] = full_rest(g_rest)
        tok = 0.0
        if l + 1 < DEPTH:
            pend_in = gather_start("in", l + 1, W_IN, g_rest[0])
            tok = token(pend_in)
        bias = _bias_expand(rel_bias[l] + tok)
        attn = _attn_fwd(p, bias, d, off)
        ret_aux = (rope_c, rope_s, lg_tab + tok)
        gla_aux = (p, wlr_pad[l], vec(gla_b_lr[l]) + tok, vec(gla_norm_g[l]))
        o_ret, b_ret, st_ret = _lin_fwd(False, p, ret_aux, d, off)
        o_gla, b_gla, st_gla = _lin_fwd(True, p, gla_aux, d, off)
        bo = jnp.stack([attn, b_ret, b_gla])
        tok = 0.0
        if l + 1 < DEPTH:
            g_in = gather_finish("in", l + 1, pend_in, bo)
            win[l + 1] = full_w_in(g_in[0])
            pend_rest = gather_start("rest", l + 1, REST, g_in[0])
            tok = token(pend_rest)
        proj, merged = _merge_fwd(bo, wbr[l], p, off["gates"])
        x1, x1b, xh1, rs1 = _mm_res_ln("out_proj_ln", merged, wout[l], xl, vec(ln1_g[l]) + tok, vec(ln1_b[l]),
                                       256, False)
        u = _mm("mlp_up", x1b, wup[l], 1024, 1024)
        x2, x2b, xh2, rs2, act = _mm_res_ln("mlp_down_ln", u, wdown[l], x1, vec(ln2_g[l]), vec(ln2_b[l]), 256, True)
        saved.append(dict(xlb=xlb, p=p, bias=bias, ret_aux=ret_aux, gla_aux=gla_aux, o_ret=o_ret, o_gla=o_gla,
                          st_ret=st_ret, st_gla=st_gla, bo=bo, proj=proj, merged=merged, x1b=x1b, xh1=xh1,
                          rs1=rs1, u=u, xh2=xh2, rs2=rs2, act=act))
        xl, xlb = x2, x2b

    small = {}
    last = saved[-1]
    loss_p, dz2, dz2b, dg, db = _loss_ln_bwd(xl, loss_target[0], last["xh2"], last["rs2"], vec(ln2_g[DEPTH - 1]))
    small["loss"] = loss_p[:, :1]
    grad_x = None

    def scatter_start(tag, l, idx, shards, after):
        theirs = _sibling_halves(f"grad_{tag}{l}_sibling", shards)
        hs = [_add_half("grad_sibling_add", g, th, csel) for g, th in zip(shards, theirs)]
        lands = [_sds((3,) + h.shape[1:], F32) for h in hs]
        return tag, l, idx, _split_start(f"grad_{tag}{l}_scatter_start", hs, lands, _scatter_copies, after)

    adam_out = [None] * len(big_w)

    def scatter_finish(pending, after):
        tag, l, idx, started = pending
        hs, rcv = _split_wait(f"grad_{tag}{l}_scatter_wait", started, _scatter_copies, after)
        sms = [_sum_shards("grad_chip_sum", h, r, psel) for h, r in zip(hs, rcv)]
        for i, own, sib in zip(idx, sms, _sibling_share(f"grad_{tag}{l}_share", sms)):
            adam_out[i] = _adamw_layer("adamw_large", big_w[i], own, sib, csel, big_m[i], big_v[i], l, adam_out[i])
        return adam_out[idx[0]][0]

    in_flight = []

    def scatter(tag, l, idx, shards, after=None):
        pending = scatter_start(tag, l, idx, shards, after)
        in_flight.append(pending)
        if len(in_flight) > 3:
            scatter_finish(in_flight.pop(0), pending[3][4])
        return pending[3][4][0, 0]

    for l in reversed(range(DEPTH)):
        s = saved[l]
        small[("ln2_g", l)], small[("ln2_b", l)] = dg, db
        du = _mm_nt_relu2_bwd(dz2b, wdown[l], s["u"])
        g_wdown = _mm_tn("grad_w_down", s["act"], dz2b, 512, 512)
        g_wup = _mm_tn("grad_w_up", s["x1b"], du, 512, 512, shard="cols")
        dz1, dz1b, dg1, db1 = _mm_nt_res_lnbwd("mlp_up_bwd_ln", du, wup[l], dz2, s["xh1"], s["rs1"],
                                               vec(ln1_g[l]), 256, dff)
        small[("ln1_g", l)], small[("ln1_b", l)] = dg1, db1
        dproj, dgl = _merge_bwd(dz1b, wout[l], s["proj"], s["p"], off["gates"])
        g_wout = _mm_tn("grad_w_out", s["merged"], dz1b, 512, 512)
        dbo = _mm("branch_proj_bwd", dproj, wbr[l], 1024, 1024, nt=True)
        g_wbr = _mm_tn("grad_w_branch", s["bo"], dproj, d // 4, 1024, shard="rows")
        tok = scatter("rest", l, REST, [g_wbr, g_wout.reshape(4, d // 4, d), g_wup, g_wdown.reshape(4, d, d)])
        rc, rs_, lg = s["ret_aux"]
        gp, gw, gb, gn_ = s["gla_aux"]
        dq_a, dk_acc, dv_acc, dbias = _attn_bwd(s["p"], s["bias"] + tok, dbo[0], d, off)
        small[("rel_bias", l)] = _bias_reduce(dbias)
        dk_a = dk_acc[2 * QB:].astype(BF16)
        dv_a = dv_acc[2 * QB:].astype(BF16)
        dq_r, dk_r, dv_r, dg_r = _lin_bwd(False, s["p"], (rc, rs_, lg + tok), s["o_ret"], s["st_ret"], dbo[1], d, off)
        dq_g, dk_g, dv_g, dg_g, dpre, dblr, dgn = _lin_bwd(True, s["p"], (gp, gw, gb + tok, gn_), s["o_gla"],
                                                           s["st_gla"], dbo[2], d, off)
        small[("gla_b_lr", l)] = dblr.reshape(1, half)
        small[("gla_norm_g", l)] = jnp.sum(dgn, axis=0)
        dpre_b = dpre.astype(BF16)
        glr_b = s["p"][:, off["glr"]:off["glr"] + LANE].astype(BF16)
        dglr = _mm("gate_lr_bwd", dpre_b, wlr_pad[l], 512, LANE, nt=True, out_dtype=BF16)
        small[("gla_w_lr", l)] = _mm_tn("grad_gla_w_lr", glr_b, dpre_b, LANE, half)[:GATE_RANK]
        dp = jnp.concatenate([dq_a, dk_a, dv_a, dq_r, dk_r, dv_r, dg_r, dq_g, dk_g, dv_g, dg_g,
                              dgl[0], dgl[1], dgl[2], dglr, jnp.zeros((t, LANE), BF16)], axis=1)
        if l > 0:
            prev = saved[l - 1]
            xh_p, rs_p, g_p = prev["xh2"], prev["rs2"], vec(ln2_g[l - 1])
        else:
            xh_p, rs_p, g_p = xh_in, rs_in, vec(ln_in_g)
        g_win = _mm_tn("grad_w_in", s["xlb"], dp, 1024, 896)
        tok = scatter("in", l, W_IN, [jnp.transpose(_unpad_cols(g_win, d).reshape(d, 4, -1), (1, 0, 2))])
        dzp, dzpb, dg, db = _mm_nt_res_lnbwd("proj_in_bwd_ln", dp, win[l], dz1, xh_p, rs_p, g_p + tok, 1024, 1792)
        dz2, dz2b = dzp, dzpb
        grad_x = dzp
    after = grad_x
    while in_flight:
        after = scatter_finish(in_flight.pop(0), after)
    small["ln_in_g"], small["ln_in_b"] = dg, db
    rb_pad = 3 * LANE
    pieces = [small["loss"].reshape(-1), jnp.zeros((LANE - 1,), F32), small["ln_in_g"].reshape(-1),
              small["ln_in_b"].reshape(-1)]
    for l in range(DEPTH):
        rb = jnp.pad(small[("rel_bias", l)], ((0, 0), (0, rb_pad - (2 * REL_CLIP + 1))))
        pieces += [rb.reshape(-1), small[("gla_w_lr", l)].reshape(-1), small[("gla_b_lr", l)].reshape(-1),
                   small[("gla_norm_g", l)].reshape(-1), small[("ln1_g", l)].reshape(-1),
                   small[("ln1_b", l)].reshape(-1), small[("ln2_g", l)].reshape(-1), small[("ln2_b", l)].reshape(-1)]
    sizes = [pc.shape[0] for pc in pieces]
    packed = jnp.concatenate(pieces)
    padn = (-packed.shape[0]) % (8 * LANE)
    packed = jnp.concatenate([packed, jnp.zeros((padn,), F32)]).reshape(-1, LANE)
    red = _small_allreduce(packed, after).reshape(-1)

    parts, pos = [], 0
    for sz in sizes:
        parts.append(red[pos:pos + sz])
        pos += sz
    loss = parts[0][0]
    g_ln_in_g, g_ln_in_b = parts[2], parts[3]
    per = 8
    g_rel = jnp.stack([parts[4 + per * l].reshape(ATTN_HEADS, rb_pad)[:, :2 * REL_CLIP + 1] for l in range(DEPTH)])
    g_wlr_full = jnp.stack([parts[5 + per * l].reshape(GATE_RANK, half) for l in range(DEPTH)])
    g_wlr = lax.dynamic_slice_in_dim(g_wlr_full, chip * (half // 4), half // 4, axis=2)
    g_blr = jnp.stack([parts[6 + per * l] for l in range(DEPTH)])
    g_gn = jnp.stack([parts[7 + per * l] for l in range(DEPTH)])
    g_ln1g = jnp.stack([parts[8 + per * l] for l in range(DEPTH)])
    g_ln1b = jnp.stack([parts[9 + per * l] for l in range(DEPTH)])
    g_ln2g = jnp.stack([parts[10 + per * l] for l in range(DEPTH)])
    g_ln2b = jnp.stack([parts[11 + per * l] for l in range(DEPTH)])

    grads = [g_ln_in_g, g_ln_in_b, None, g_rel, g_wlr, g_blr, g_gn, None, None, g_ln1g, g_ln1b, None, None,
             g_ln2g, g_ln2b]
    ws = [ln_in_g, ln_in_b, w_in, rel_bias, gla_w_lr, gla_b_lr, gla_norm_g, w_branch, w_out, ln1_g, ln1_b,
          w_up, w_down, ln2_g, ln2_b]
    ms = [m_ln_in_g, m_ln_in_b, m_w_in, m_rel_bias, m_gla_w_lr, m_gla_b_lr, m_gla_norm_g, m_w_branch, m_w_out,
          m_ln1_g, m_ln1_b, m_w_up, m_w_down, m_ln2_g, m_ln2_b]
    vs = [v_ln_in_g, v_ln_in_b, v_w_in, v_rel_bias, v_gla_w_lr, v_gla_b_lr, v_gla_norm_g, v_w_branch, v_w_out,
          v_ln1_g, v_ln1_b, v_w_up, v_w_down, v_ln2_g, v_ln2_b]

    deltas, new_ms, new_vs = [None] * 15, [None] * 15, [None] * 15
    big_idx = [2, 7, 8, 11, 12]
    for i, res in zip(big_idx, adam_out):
        shp = ws[i].shape
        grads[i], deltas[i], new_ms[i], new_vs[i] = (r.reshape(shp) for r in res)
    small_idx = [i for i in range(15) if i not in big_idx]

    def pack(arrs):
        flat_ = jnp.concatenate([arrs[i].reshape(-1) for i in small_idx])
        pad_ = (-flat_.shape[0]) % (8 * LANE)
        return jnp.concatenate([flat_, jnp.ones((pad_,), F32)]).reshape(-1, LANE)

    dl, nm, nv = _adamw("adamw_small", pack(ws), pack(grads), pack(ms), pack(vs))
    pos = 0
    for i in small_idx:
        sz = int(np.prod(ws[i].shape))
        deltas[i] = dl.reshape(-1)[pos:pos + sz].reshape(ws[i].shape)
        new_ms[i] = nm.reshape(-1)[pos:pos + sz].reshape(ws[i].shape)
        new_vs[i] = nv.reshape(-1)[pos:pos + sz].reshape(ws[i].shape)
        pos += sz

    return (loss, grad_x[None], *grads, *deltas, *new_ms, *new_vs)
```

```python
import functools

import numpy as np
import jax
import jax.numpy as jnp
from jax import lax
from jax.experimental import pallas as pl
from jax.experimental.pallas import tpu as pltpu

F32 = jnp.float32
BF16 = jnp.bfloat16
MXU_DTYPE = BF16
HI = lax.Precision.HIGHEST

DEPTH = 2
CHUNK = 64
N_BRANCH = 3
ATTN_HEADS = 8
ATTN_LEFT = 8
REL_CLIP = 2 * CHUNK
LIN_HEADS = 4
GATE_RANK = 16
GATE_NORM = 16.0
LN_EPS = 1e-5
NEG_INF = -1e30
ALPHA = (2 * DEPTH) ** 0.25
ADAM_LR, ADAM_B1, ADAM_B2, ADAM_EPS, ADAM_WD, ADAM_STEP = 0.001, 0.9, 0.999, 1e-08, 0.01, 10

LANE = 128
VMEM_LIMIT = 56 << 20
QB = 256
KW = 3 * QB
LB = 256
MESH_AXES = ("x", "y", "c")
DEV = pl.DeviceIdType.MESH


def _cp(sem):
    return pltpu.CompilerParams(dimension_semantics=sem, vmem_limit_bytes=VMEM_LIMIT)


def _mx(v):
    return v.astype(MXU_DTYPE)


def _dot(a, b):
    return jnp.dot(_mx(a), _mx(b), preferred_element_type=F32)


def _dot_nt(a, b):
    return lax.dot_general(_mx(a), _mx(b), (((1,), (1,)), ((), ())), preferred_element_type=F32)


def _dot_tn(a, b):
    return lax.dot_general(_mx(a), _mx(b), (((0,), (0,)), ((), ())), preferred_element_type=F32)


def _dot_hi(a, b):
    return jnp.dot(a, b, precision=HI, preferred_element_type=F32)


def _sigmoid(v):
    return 1.0 / (1.0 + jnp.exp(-v))


def _sds(shape, dtype):
    return jax.ShapeDtypeStruct(shape, dtype)


def _mm(name, a, b, tm, tn, nt=False, out_dtype=F32):
    batched = a.ndim == 3
    m, k = a.shape[-2:]
    n = b.shape[-2] if nt else b.shape[-1]
    tm, tn = min(tm, m), min(tn, n)

    def body(a_ref, b_ref, o_ref):
        f = _dot_nt if nt else _dot
        o_ref[...] = f(a_ref[...], b_ref[...]).astype(o_ref.dtype)

    rows_inner = (n // tn) * m < (m // tm) * n

    def ij(u, v):
        return (v, u) if rows_inner else (u, v)

    if batched:
        nb = a.shape[0]
        grid = (nb,) + ij(m // tm, n // tn)
        a_spec = pl.BlockSpec((None, tm, k), lambda g, u, v: (g, ij(u, v)[0], 0))
        b_spec = (pl.BlockSpec((None, tn, k), lambda g, u, v: (g, ij(u, v)[1], 0)) if nt
                  else pl.BlockSpec((None, k, tn), lambda g, u, v: (g, 0, ij(u, v)[1])))
        o_spec = pl.BlockSpec((None, tm, tn), lambda g, u, v: (g,) + ij(u, v))
        out_shape = _sds((nb, m, n), out_dtype)
        sem = ("parallel", "parallel", "parallel")
    else:
        grid = ij(m // tm, n // tn)
        a_spec = pl.BlockSpec((tm, k), lambda u, v: (ij(u, v)[0], 0))
        b_spec = (pl.BlockSpec((tn, k), lambda u, v: (ij(u, v)[1], 0)) if nt
                  else pl.BlockSpec((k, tn), lambda u, v: (0, ij(u, v)[1])))
        o_spec = pl.BlockSpec((tm, tn), lambda u, v: ij(u, v))
        out_shape = _sds((m, n), out_dtype)
        sem = ("parallel", "parallel")
    return pl.pallas_call(body, name=name, grid=grid, in_specs=[a_spec, b_spec], out_specs=o_spec,
                          out_shape=out_shape, compiler_params=_cp(sem))(a, b)


def _mm_tn(name, a, b, tm, tn, shard=None):
    batched = a.ndim == 3
    k, m = a.shape[-2:]
    n = b.shape[-1]
    tm, tn = min(tm, m), min(tn, n)

    def body(a_ref, b_ref, o_ref):
        o_ref[...] = lax.dot_general(_mx(a_ref[...]), _mx(b_ref[...]), (((0,), (0,)), ((), ())),
                                     preferred_element_type=F32)

    if batched:
        nb = a.shape[0]
        grid = (nb, m // tm, n // tn)
        a_spec = pl.BlockSpec((None, k, tm), lambda g, i, j: (g, 0, i))
        b_spec = pl.BlockSpec((None, k, tn), lambda g, i, j: (g, 0, j))
        if shard == "rows":
            assert 4 * tm == m
            o_spec = pl.BlockSpec((None, tm, tn), lambda g, i, j: (i, g, j))
            out_shape = _sds((4, nb * tm, n), F32)
        else:
            o_spec = pl.BlockSpec((None, tm, tn), lambda g, i, j: (g, i, j))
            out_shape = _sds((nb, m, n), F32)
    else:
        grid = (m // tm, n // tn)
        a_spec = pl.BlockSpec((k, tm), lambda i, j: (0, i))
        b_spec = pl.BlockSpec((k, tn), lambda i, j: (0, j))
        if shard == "cols":
            per = n // 4 // tn
            o_spec = pl.BlockSpec((None, tm, tn), lambda i, j: (j // per, i, j % per))
            out_shape = _sds((4, m, n // 4), F32)
        else:
            o_spec = pl.BlockSpec((tm, tn), lambda i, j: (i, j))
            out_shape = _sds((m, n), F32)
    return pl.pallas_call(body, name=name, grid=grid, in_specs=[a_spec, b_spec], out_specs=o_spec,
                          out_shape=out_shape, compiler_params=_cp(("parallel",) * len(grid)))(a, b)


def _ln_rows(y, g, b):
    mu = jnp.mean(y, axis=-1, keepdims=True)
    yc = y - mu
    var = jnp.mean(yc * yc, axis=-1, keepdims=True)
    rs = lax.rsqrt(var + LN_EPS)
    xh = yc * rs
    return xh * g + b, xh, rs


def _ln_in(x, g, b, tm=256):
    t, d = x.shape

    def body(x_ref, g_ref, b_ref, o_ref, ob_ref, xh_ref, rs_ref):
        o, xh, rs = _ln_rows(x_ref[...], g_ref[...], b_ref[...])
        o_ref[...] = o
        ob_ref[...] = o.astype(BF16)
        xh_ref[...] = xh
        rs_ref[...] = rs

    row = pl.BlockSpec((tm, d), lambda i: (i, 0))
    vec = pl.BlockSpec((1, d), lambda i: (0, 0))
    return pl.pallas_call(
        body, name="ln_in", grid=(t // tm,), in_specs=[row, vec, vec],
        out_specs=[row, row, row, pl.BlockSpec((tm, 1), lambda i: (i, 0))],
        out_shape=[_sds((t, d), F32), _sds((t, d), BF16), _sds((t, d), F32), _sds((t, 1), F32)],
        compiler_params=_cp(("parallel",)))(x, g, b)


def _mm_res_ln(name, a, w, res, g, b, tm, relu2):
    t, k = a.shape
    d = w.shape[1]

    def body(a_ref, w_ref, r_ref, g_ref, b_ref, o_ref, ob_ref, xh_ref, rs_ref, *act_ref):
        av = a_ref[...]
        if relu2:
            av = jnp.square(jnp.maximum(av, 0.0))
            act_ref[0][...] = av.astype(BF16)
        y = ALPHA * r_ref[...] + _dot(av, w_ref[...])
        o, xh, rs = _ln_rows(y, g_ref[...], b_ref[...])
        o_ref[...] = o
        ob_ref[...] = o.astype(BF16)
        xh_ref[...] = xh
        rs_ref[...] = rs

    row = pl.BlockSpec((tm, d), lambda i: (i, 0))
    vec = pl.BlockSpec((1, d), lambda i: (0, 0))
    arow = pl.BlockSpec((tm, k), lambda i: (i, 0))
    out_specs = [row, row, row, pl.BlockSpec((tm, 1), lambda i: (i, 0))]
    out_shape = [_sds((t, d), F32), _sds((t, d), BF16), _sds((t, d), F32), _sds((t, 1), F32)]
    if relu2:
        out_specs.append(arow)
        out_shape.append(_sds((t, k), BF16))
    return pl.pallas_call(
        body, name=name, grid=(t // tm,),
        in_specs=[arow, pl.BlockSpec((k, d), lambda i: (0, 0)), row, vec, vec],
        out_specs=out_specs, out_shape=out_shape, compiler_params=_cp(("parallel",)))(a, w, res, g, b)


def _merge_fwd(bo, wb, p, gate_off, tm=512, tn=512):
    _, t, d = bo.shape
    gb = gate_off // tn

    def body(bo_ref, wb_ref, g0, g1, g2, proj_ref, m_ref):
        acc = None
        for n, g_ref in enumerate((g0, g1, g2)):
            pr = _dot(bo_ref[n], wb_ref[n])
            proj_ref[n] = pr
            term = _sigmoid(g_ref[...]) * pr
            acc = term if acc is None else acc + term
        m_ref[...] = acc.astype(BF16)

    gspecs = [pl.BlockSpec((tm, tn), functools.partial(lambda i, j, n: (i, gb + n * (d // tn) + j), n=n))
              for n in range(3)]
    return pl.pallas_call(
        body, name="merge_fwd", grid=(t // tm, d // tn),
        in_specs=[pl.BlockSpec((3, tm, d), lambda i, j: (0, i, 0)),
                  pl.BlockSpec((3, d, tn), lambda i, j: (0, 0, j))] + gspecs,
        out_specs=[pl.BlockSpec((3, tm, tn), lambda i, j: (0, i, j)), pl.BlockSpec((tm, tn), lambda i, j: (i, j))],
        out_shape=[_sds((3, t, d), F32), _sds((t, d), BF16)],
        compiler_params=_cp(("parallel", "parallel")))(bo, wb, p, p, p)


def _merge_bwd(dz, wout, proj, p, gate_off, tm=512, tn=512):
    t, d = dz.shape
    gb = gate_off // tn

    def body(dz_ref, w_ref, proj_ref, g0, g1, g2, dproj_ref, dgl_ref):
        dm = _dot_nt(dz_ref[...], w_ref[...])
        for n, g_ref in enumerate((g0, g1, g2)):
            s = _sigmoid(g_ref[...])
            dproj_ref[n] = (dm * s).astype(BF16)
            dgl_ref[n] = (dm * proj_ref[n] * (s * (1.0 - s))).astype(BF16)

    gspecs = [pl.BlockSpec((tm, tn), functools.partial(lambda i, j, n: (i, gb + n * (d // tn) + j), n=n))
              for n in range(3)]
    dproj, dgl = pl.pallas_call(
        body, name="merge_bwd", grid=(t // tm, d // tn),
        in_specs=[pl.BlockSpec((tm, d), lambda i, j: (i, 0)), pl.BlockSpec((tn, d), lambda i, j: (j, 0)),
                  pl.BlockSpec((3, tm, tn), lambda i, j: (0, i, j))] + gspecs,
        out_specs=[pl.BlockSpec((3, tm, tn), lambda i, j: (0, i, j)),
                   pl.BlockSpec((3, tm, tn), lambda i, j: (0, i, j))],
        out_shape=[_sds((3, t, d), BF16), _sds((3, t, d), BF16)],
        compiler_params=_cp(("parallel", "parallel")))(dz, wout, proj, p, p, p)
    return dproj, dgl


def _mm_nt_relu2_bwd(dz, wdown, u, tm=512, tn=1024):
    t, d = dz.shape
    f = wdown.shape[0]

    def body(dz_ref, w_ref, u_ref, du_ref):
        da = _dot_nt(dz_ref[...], w_ref[...])
        du_ref[...] = (da * (2.0 * jnp.maximum(u_ref[...], 0.0))).astype(BF16)

    return pl.pallas_call(
        body, name="mlp_down_bwd", grid=(t // tm, f // tn),
        in_specs=[pl.BlockSpec((tm, d), lambda i, j: (i, 0)), pl.BlockSpec((tn, d), lambda i, j: (j, 0)),
                  pl.BlockSpec((tm, tn), lambda i, j: (i, j))],
        out_specs=pl.BlockSpec((tm, tn), lambda i, j: (i, j)), out_shape=_sds((t, f), BF16),
        compiler_params=_cp(("parallel", "parallel")))(dz, wdown, u)


def _ln_bwd_rows(dx, xh, rs, g):
    dxh = dx * g
    m1 = jnp.mean(dxh, axis=-1, keepdims=True)
    m2 = jnp.mean(dxh * xh, axis=-1, keepdims=True)
    return rs * (dxh - m1 - xh * m2)


def _mm_nt_res_lnbwd(name, a, w, dres, xh, rs, g, tm, tk):
    t, k = a.shape
    d = w.shape[0]
    nk = k // tk

    def body(a_ref, w_ref, dr_ref, xh_ref, rs_ref, g_ref, dz_ref, dzb_ref, dg_ref, db_ref, acc_ref):
        i, kk = pl.program_id(0), pl.program_id(1)

        @pl.when(kk == 0)
        def _():
            acc_ref[...] = ALPHA * dr_ref[...]

        acc_ref[...] += _dot_nt(a_ref[...], w_ref[...])

        @pl.when(jnp.logical_and(i == 0, kk == 0))
        def _():
            dg_ref[...] = jnp.zeros_like(dg_ref)
            db_ref[...] = jnp.zeros_like(db_ref)

        @pl.when(kk == nk - 1)
        def _():
            dx = acc_ref[...]
            xhv = xh_ref[...]
            dz = _ln_bwd_rows(dx, xhv, rs_ref[...], g_ref[...])
            dz_ref[...] = dz
            dzb_ref[...] = dz.astype(BF16)
            dg_ref[...] += jnp.sum(dx * xhv, axis=0, keepdims=True)
            db_ref[...] += jnp.sum(dx, axis=0, keepdims=True)

    row = pl.BlockSpec((tm, d), lambda i, kk: (i, 0))
    vec = pl.BlockSpec((1, d), lambda i, kk: (0, 0))
    return pl.pallas_call(
        body, name=name, grid=(t // tm, nk),
        in_specs=[pl.BlockSpec((tm, tk), lambda i, kk: (i, kk)), pl.BlockSpec((d, tk), lambda i, kk: (0, kk)),
                  row, row, pl.BlockSpec((tm, 1), lambda i, kk: (i, 0)), vec],
        out_specs=[row, row, vec, vec],
        out_shape=[_sds((t, d), F32), _sds((t, d), BF16), _sds((1, d), F32), _sds((1, d), F32)],
        scratch_shapes=[pltpu.VMEM((tm, d), F32)],
        compiler_params=_cp(("arbitrary", "arbitrary")))(a, w, dres, xh, rs, g)


def _loss_ln_bwd(x2, target, xh, rs, g, tm=256):
    t, d = x2.shape

    def body(x_ref, t_ref, xh_ref, rs_ref, g_ref, loss_ref, dz_ref, dzb_ref, dg_ref, db_ref):
        @pl.when(pl.program_id(0) == 0)
        def _():
            loss_ref[...] = jnp.zeros_like(loss_ref)
            dg_ref[...] = jnp.zeros_like(dg_ref)
            db_ref[...] = jnp.zeros_like(db_ref)

        err = x_ref[...] - t_ref[...]
        per_row = jnp.mean(err * err, axis=-1, keepdims=True)
        loss_ref[...] += 0.5 * jnp.sum(per_row, axis=0, keepdims=True)
        dx = err * (1.0 / d)
        xhv = xh_ref[...]
        dz = _ln_bwd_rows(dx, xhv, rs_ref[...], g_ref[...])
        dz_ref[...] = dz
        dzb_ref[...] = dz.astype(BF16)
        dg_ref[...] += jnp.sum(dx * xhv, axis=0, keepdims=True)
        db_ref[...] += jnp.sum(dx, axis=0, keepdims=True)

    row = pl.BlockSpec((tm, d), lambda i: (i, 0))
    vec = pl.BlockSpec((1, d), lambda i: (0, 0))
    return pl.pallas_call(
        body, name="loss_ln_bwd", grid=(t // tm,),
        in_specs=[row, row, row, pl.BlockSpec((tm, 1), lambda i: (i, 0)), vec],
        out_specs=[pl.BlockSpec((1, LANE), lambda i: (0, 0)), row, row, vec, vec],
        out_shape=[_sds((1, LANE), F32), _sds((t, d), F32), _sds((t, d), BF16), _sds((1, d), F32),
                   _sds((1, d), F32)],
        compiler_params=_cp(("arbitrary",)))(x2, target, xh, rs, g)


HPA = 2


def _attn_scores(q_ref, k_refs, bias_ref, i, dh, hh):
    cols = pl.ds(hh * dh, dh)
    q = q_ref[:, cols] * (dh ** -0.5)
    k = jnp.concatenate([r[:, cols] for r in k_refs], axis=0)
    s = _dot_nt(q, k) + bias_ref[hh]
    col = lax.broadcasted_iota(jnp.int32, s.shape, 1)
    s = jnp.where(col >= (2 - i) * QB, s, NEG_INF)
    m = jnp.max(s, axis=-1, keepdims=True)
    e = jnp.exp(s - m)
    return q, k, e / jnp.sum(e, axis=-1, keepdims=True)


def _attn_specs(dh, off):
    w = HPA * dh
    qcol, kcol, vcol = off["aq"] // w, off["ak"] // w, off["av"] // w
    q_spec = pl.BlockSpec((QB, w), lambda g, i: (i, qcol + g))
    k_specs = [pl.BlockSpec((QB, w), functools.partial(lambda g, i, j: (jnp.maximum(i - 2 + j, 0), kcol + g), j=j))
               for j in range(3)]
    v_specs = [pl.BlockSpec((QB, w), functools.partial(lambda g, i, j: (jnp.maximum(i - 2 + j, 0), vcol + g), j=j))
               for j in range(3)]
    bias_spec = pl.BlockSpec((HPA, QB, KW), lambda g, i: (g, 0, 0))
    return q_spec, k_specs, v_specs, bias_spec


def _attn_fwd(p, bias, d, off):
    t = p.shape[0]
    dh = d // ATTN_HEADS

    def body(q_ref, k0, k1, k2, v0, v1, v2, bias_ref, o_ref):
        for hh in range(HPA):
            cols = pl.ds(hh * dh, dh)
            _, _, pr = _attn_scores(q_ref, (k0, k1, k2), bias_ref, pl.program_id(1), dh, hh)
            v = jnp.concatenate([v0[:, cols], v1[:, cols], v2[:, cols]], axis=0)
            o_ref[:, cols] = _dot(pr, v).astype(o_ref.dtype)

    q_spec, k_specs, v_specs, bias_spec = _attn_specs(dh, off)
    return pl.pallas_call(
        body, name="attn_fwd", grid=(ATTN_HEADS // HPA, t // QB),
        in_specs=[q_spec] + k_specs + v_specs + [bias_spec],
        out_specs=pl.BlockSpec((QB, HPA * dh), lambda g, i: (i, g)), out_shape=_sds((t, d), BF16),
        compiler_params=_cp(("parallel", "parallel")))(p, p, p, p, p, p, p, bias)


def _attn_bwd(p, bias, do, d, off):
    t = p.shape[0]
    dh = d // ATTN_HEADS
    tp = t + 2 * QB

    def body(q_ref, k0, k1, k2, v0, v1, v2, bias_ref, do_ref, dq_ref, dk_ref, dv_ref, dbias_ref):
        i = pl.program_id(1)

        @pl.when(i == 0)
        def _():
            dk_ref[...] = jnp.zeros_like(dk_ref)
            dv_ref[...] = jnp.zeros_like(dv_ref)
            dbias_ref[...] = jnp.zeros_like(dbias_ref)

        rows = pl.ds(pl.multiple_of(i * QB, QB), KW)
        for hh in range(HPA):
            cols = pl.ds(hh * dh, dh)
            q, k, pr = _attn_scores(q_ref, (k0, k1, k2), bias_ref, i, dh, hh)
            v = jnp.concatenate([v0[:, cols], v1[:, cols], v2[:, cols]], axis=0)
            dov = do_ref[:, cols]
            dp = _dot_nt(dov, v)
            delta = jnp.sum(pr * dp, axis=-1, keepdims=True)
            ds = pr * (dp - delta)
            dbias_ref[hh] += ds
            dq_ref[:, cols] = (_dot(ds, k) * (dh ** -0.5)).astype(dq_ref.dtype)
            dk_ref[rows, cols] += _dot_tn(ds, q)
            dv_ref[rows, cols] += _dot_tn(pr, dov)

    q_spec, k_specs, v_specs, bias_spec = _attn_specs(dh, off)
    row_spec = pl.BlockSpec((QB, HPA * dh), lambda g, i: (i, g))
    acc_spec = pl.BlockSpec((tp, HPA * dh), lambda g, i: (0, g))
    return pl.pallas_call(
        body, name="attn_bwd", grid=(ATTN_HEADS // HPA, t // QB),
        in_specs=[q_spec] + k_specs + v_specs + [bias_spec, row_spec],
        out_specs=[row_spec, acc_spec, acc_spec, bias_spec],
        out_shape=[_sds((t, d), BF16), _sds((tp, d), F32), _sds((tp, d), F32),
                   _sds((ATTN_HEADS, QB, KW), F32)],
        compiler_params=_cp(("parallel", "arbitrary")))(p, p, p, p, p, p, p, bias, do)


def _onehot_mm(name, a, b):
    def body(a_ref, b_ref, o_ref):
        o_ref[...] = _dot_hi(a_ref[...], b_ref[...])

    return pl.pallas_call(body, name=name, out_shape=_sds((a.shape[0], b.shape[1]), F32),
                          compiler_params=pltpu.CompilerParams(vmem_limit_bytes=VMEM_LIMIT))(a, b)


def _diag_index():
    ii, jj = np.arange(CHUNK)[:, None], np.arange(CHUNK)[None, :]
    return (ii - jj + CHUNK - 1).reshape(-1)


def _bias_expand(rel_bias):
    h = rel_bias.shape[0]
    nq, nk, shift = QB // CHUNK, KW // CHUNK, (2 * QB) // CHUNK
    nbin, ndc = 3 * LANE, 4
    rb = jnp.pad(rel_bias, ((0, 0), (0, nbin - rel_bias.shape[1])))
    win = np.clip(CHUNK * np.arange(ndc)[:, None] + np.arange(LANE)[None, :] - (CHUNK - 1), -REL_CLIP, REL_CLIP)
    sel = (jnp.arange(nbin)[:, None] == jnp.asarray((win + REL_CLIP).reshape(1, -1))).astype(F32)
    windows = _onehot_mm("bias_windows", rb, sel)
    diag_t = (jnp.arange(LANE)[:, None] == jnp.asarray(_diag_index().reshape(1, -1))).astype(F32)
    blocks = _onehot_mm("bias_blocks", windows.reshape(h * ndc, LANE), diag_t).reshape(h, ndc, CHUNK, CHUNK)
    off_band = jnp.full((h, CHUNK, CHUNK), NEG_INF, F32)
    rows = []
    for ic in range(nq):
        dcs = [ic - jc + shift for jc in range(nk)]
        rows.append(jnp.concatenate([blocks[:, min(dc, ndc - 1)] if 0 <= dc <= ATTN_LEFT else off_band
                                     for dc in dcs], axis=2))
    return jnp.concatenate(rows, axis=1)


def _bias_reduce(dbias):
    h = dbias.shape[0]
    nq, nk = QB // CHUNK, KW // CHUNK
    nbin = 3 * LANE
    blocks = dbias.reshape(h, nq, CHUNK, nk, CHUNK).transpose(0, 1, 3, 2, 4).reshape(h * nq * nk, CHUNK * CHUNK)
    diag = (jnp.asarray(_diag_index().reshape(-1, 1)) == jnp.arange(LANE)[None, :]).astype(F32)
    ic = np.arange(nq)[:, None, None]
    jc = np.arange(nk)[None, :, None]
    dl = np.arange(LANE)[None, None, :] - (CHUNK - 1)
    rel = np.clip(CHUNK * (ic - jc + (2 * QB) // CHUNK) + dl, -REL_CLIP, REL_CLIP) + REL_CLIP
    bins = (jnp.asarray(rel.reshape(-1, 1)) == jnp.arange(nbin)[None, :]).astype(F32)

    diags = _onehot_mm("bias_diag_sums", blocks, diag)
    out = _onehot_mm("bias_bin_sums", diags.reshape(h, nq * nk * LANE), bins)
    return out[:, :2 * REL_CLIP + 1]


def _chunk_masks():
    r = lax.broadcasted_iota(jnp.int32, (LB, LB), 0)
    c = lax.broadcasted_iota(jnp.int32, (LB, LB), 1)
    return (r // CHUNK) == (c // CHUNK), r >= c, r <= c


def _chunks(a):
    return [a[c * CHUNK:(c + 1) * CHUNK] for c in range(LB // CHUNK)]


def _per_chunk(a, f):
    return jnp.concatenate([jnp.broadcast_to(f(c), c.shape) for c in _chunks(a)], axis=0)


def _dot_sel(sel, x):
    def top(v):
        return lax.bitcast_convert_type(lax.bitcast_convert_type(v, jnp.int32) & jnp.int32(-65536), F32)

    hi = top(x)
    mid = top(x - hi)
    lo = (x - hi) - mid
    d = functools.partial(jnp.dot, sel.astype(jnp.bfloat16), preferred_element_type=F32)
    return d(hi.astype(jnp.bfloat16)) + d(mid.astype(jnp.bfloat16)) + d(lo.astype(jnp.bfloat16))


def _lin_block(gla, q, k, v, aux):
    dk = q.shape[-1]
    same, low, up = _chunk_masks()
    ones = same.astype(F32)
    if gla:
        glr, wlr, blr = aux
        q = q * (dk ** -0.5)
        pre = _dot(glr, wlr) + blr
        log_a = (jnp.minimum(pre, 0.0) - jnp.log(1.0 + jnp.exp(-jnp.abs(pre)))) / GATE_NORM
        b = _dot_sel(jnp.where(low, ones, 0.0), log_a)
        lastb = _per_chunk(b, lambda c: c[CHUNK - 1:])
    else:
        cs, sn, lg = aux
        pre = None
        half = dk // 2
        q = q * cs + pltpu.roll(q, half, 1) * sn
        k = (k * cs + pltpu.roll(k, half, 1) * sn) * (dk ** -0.5)
        pos = (lax.broadcasted_iota(jnp.int32, (LB, dk), 0) % CHUNK).astype(F32) + 1.0
        b = pos * lg
        lastb = jnp.broadcast_to(float(CHUNK) * lg, b.shape)
    eb, enb, el, dec = jnp.exp(b), jnp.exp(-b), jnp.exp(lastb - b), jnp.exp(lastb)
    qf, kf, qb, kb, kl = q * eb, k * enb, q * enb, k * eb, k * el
    s = jnp.where(same, jnp.where(low, _dot_nt(qf, kf), _dot_nt(qb, kb)), 0.0)
    return dict(pre=pre, eb=eb, enb=enb, el=el, dec=dec, qf=qf, kf=kf, qb=qb, kb=kb, kl=kl, s=s,
                same=same, low=low, up=up, ones=ones)


def _lin_norm_gate(gla, o, gate, gn):
    sg = _sigmoid(gate)
    silu = gate * sg
    if gla:
        r = lax.rsqrt(jnp.mean(o * o, axis=-1, keepdims=True) + LN_EPS)
        hn = o * r
        return silu * (hn * gn), (sg, silu, r, hn)
    mu = jnp.mean(o, axis=-1, keepdims=True)
    oc = o - mu
    r = lax.rsqrt(jnp.mean(oc * oc, axis=-1, keepdims=True) + LN_EPS)
    hn = oc * r
    return silu * hn, (sg, silu, r, hn)


HPS = 2


def _lin_specs(gla, dk, dv, off, rev, nb):
    pre = "g" if gla else "r"
    wk, wv = HPS * dk, HPS * dv
    qc, kc, vc, gc = (off[pre + "q"] // wk, off[pre + "k"] // wk, off[pre + "v"] // wv, off[pre + "g"] // wv)

    def blk(i):
        return nb - 1 - i if rev else i

    specs = [pl.BlockSpec((LB, wk), lambda g, i: (blk(i), qc + g)),
             pl.BlockSpec((LB, wk), lambda g, i: (blk(i), kc + g)),
             pl.BlockSpec((LB, wv), lambda g, i: (blk(i), vc + g)),
             pl.BlockSpec((LB, wv), lambda g, i: (blk(i), gc + g))]
    if gla:
        specs += [pl.BlockSpec((LB, LANE), lambda g, i: (blk(i), off["glr"] // LANE)),
                  pl.BlockSpec((LANE, wk), lambda g, i: (0, g)),
                  pl.BlockSpec((1, wk), lambda g, i: (0, g)),
                  pl.BlockSpec((1, dv), lambda g, i: (0, 0))]
    else:
        specs += [pl.BlockSpec((LB, dk), lambda g, i: (blk(i), 0)),
                  pl.BlockSpec((LB, dk), lambda g, i: (blk(i), 0)),
                  pl.BlockSpec((HPS, 1, dk), lambda g, i: (g, 0, 0))]
    return specs, blk


def _lin_aux(gla, refs, rows, hh, dk):
    if gla:
        glr_ref, wlr_ref, blr_ref, gn_ref = refs
        kcols = pl.ds(hh * dk, dk)
        return (glr_ref[rows, :], wlr_ref[:, kcols], blr_ref[:, kcols]), gn_ref[...]
    cs_ref, sn_ref, lg_ref = refs
    return (cs_ref[rows, :], sn_ref[rows, :], lg_ref[hh]), None


def _lin_fwd(gla, p, aux_arrays, d, off):
    t = p.shape[0]
    dk, dv = d // (2 * LIN_HEADS), d // LIN_HEADS
    nb, cb = t // LB, LB // CHUNK
    naux = len(aux_arrays)

    def body(*refs):
        q_ref, k_ref, v_ref, g_ref = refs[:4]
        aux_refs = refs[4:4 + naux]
        o_ref, bo_ref, st_out_ref, st_ref = refs[4 + naux:]

        @pl.when(pl.program_id(1) == 0)
        def _():
            st_ref[...] = jnp.zeros_like(st_ref)

        rows = slice(None)
        for hh in range(HPS):
            kcols, vcols = pl.ds(hh * dk, dk), pl.ds(hh * dv, dv)
            aux, gn = _lin_aux(gla, aux_refs, rows, hh, dk)
            v = v_ref[:, vcols]
            blk = _lin_block(gla, q_ref[:, kcols], k_ref[:, kcols], v, aux)
            st = st_ref[hh]
            inter = []
            for c, (qf, kl, dec, vc) in enumerate(zip(_chunks(blk["qf"]), _chunks(blk["kl"]), _chunks(blk["dec"]),
                                                      _chunks(v))):
                st_out_ref[hh, c] = st
                inter.append(_dot_nt(qf, st))
                st = st * dec[:1] + _dot_tn(vc, kl)
            st_ref[hh] = st
            o = _dot(blk["s"], v) + jnp.concatenate(inter, axis=0)
            o_ref[:, vcols] = o
            out, _ = _lin_norm_gate(gla, o, g_ref[:, vcols], gn)
            bo_ref[:, vcols] = out.astype(BF16)

    specs, _ = _lin_specs(gla, dk, dv, off, False, nb)
    orow = pl.BlockSpec((LB, HPS * dv), lambda g, i: (i, g))
    return pl.pallas_call(
        body, name="gla_fwd" if gla else "ret_fwd", grid=(LIN_HEADS // HPS, nb), in_specs=specs,
        out_specs=[orow, orow, pl.BlockSpec((HPS, cb, dv, dk), lambda g, i: (g, i, 0, 0))],
        out_shape=[_sds((t, d), F32), _sds((t, d), BF16), _sds((LIN_HEADS, t // CHUNK, dv, dk), F32)],
        scratch_shapes=[pltpu.VMEM((HPS, dv, dk), F32)],
        compiler_params=_cp(("parallel", "arbitrary")))(p, p, p, p, *aux_arrays)


def _lin_bwd(gla, p, aux_arrays, o, states, dbo, d, off):
    t = p.shape[0]
    dk, dv = d // (2 * LIN_HEADS), d // LIN_HEADS
    nb, cb = t // LB, LB // CHUNK
    naux = len(aux_arrays)

    def body(*refs):
        q_ref, k_ref, v_ref, g_ref = refs[:4]
        aux_refs = refs[4:4 + naux]
        o_ref, st_in_ref, dbo_ref = refs[4 + naux:7 + naux]
        outs = refs[7 + naux:]
        dq_ref, dk_ref, dv_ref, dg_ref = outs[:4]
        dst_ref = outs[-1]
        first = pl.program_id(1) == 0

        @pl.when(first)
        def _():
            dst_ref[...] = jnp.zeros_like(dst_ref)

        if gla:
            dpre_ref, dblr_ref, dgn_ref = outs[4:7]

            @pl.when(first)
            def _():
                dblr_ref[...] = jnp.zeros_like(dblr_ref)
                dgn_ref[...] = jnp.zeros_like(dgn_ref)

        rows = slice(None)
        for hh in range(HPS):
            kcols, vcols = pl.ds(hh * dk, dk), pl.ds(hh * dv, dv)
            aux, gn = _lin_aux(gla, aux_refs, rows, hh, dk)
            v = v_ref[:, vcols]
            bk = _lin_block(gla, q_ref[:, kcols], k_ref[:, kcols], v, aux)
            eb, enb, el, dec = bk["eb"], bk["enb"], bk["el"], bk["dec"]
            qf, kf, qb, kb, kl, s = bk["qf"], bk["kf"], bk["qb"], bk["kb"], bk["kl"], bk["s"]
            gate = g_ref[:, vcols]
            dout = dbo_ref[:, vcols]
            _, (sg, silu, r, hn) = _lin_norm_gate(gla, o_ref[:, vcols], gate, gn)
            dsilu = sg * (1.0 + gate * (1.0 - sg))
            if gla:
                y = hn * gn
                dy = dout * silu
                dg_ref[:, vcols] = (dout * y * dsilu).astype(BF16)
                dgn_ref[hh] += jnp.sum(dy * hn, axis=0, keepdims=True)
                dhn = dy * gn
                do = r * (dhn - hn * jnp.mean(dhn * hn, axis=-1, keepdims=True))
            else:
                dhn = dout * silu
                dg_ref[:, vcols] = (dout * hn * dsilu).astype(BF16)
                do = r * (dhn - jnp.mean(dhn, axis=-1, keepdims=True)
                          - hn * jnp.mean(dhn * hn, axis=-1, keepdims=True))
            ds = jnp.where(bk["same"], _dot_nt(do, v), 0.0)
            dsf = jnp.where(bk["low"], ds, 0.0)
            dsb = ds - dsf
            dvv = _dot_tn(s, do)
            dqf = _dot(dsf, kf)
            dkf = _dot_tn(dsf, qf)
            dqb = _dot(dsb, kb)
            dkb = _dot_tn(dsb, qb)
            dst = dst_ref[hh]
            dv_st, dqf_st, dkl_c, ddec_c = [], [], [], []
            parts = zip(reversed(range(cb)), reversed(_chunks(do)), reversed(_chunks(v)), reversed(_chunks(qf)),
                        reversed(_chunks(kl)), reversed(_chunks(dec)))
            for c, do_c, v_c, qf_c, kl_c, dec_c in parts:
                st = st_in_ref[hh, c]
                dv_st.append(_dot_nt(kl_c, dst))
                dkl_c.append(_dot(v_c, dst))
                dqf_st.append(_dot(do_c, st))
                ddec_c.append(jnp.broadcast_to(jnp.sum(dst * st, axis=0, keepdims=True), (CHUNK, dk)))
                dst = dst * dec_c[:1] + _dot_tn(do_c, qf_c)
            dst_ref[hh] = dst

            def cat(pieces):
                return jnp.concatenate(pieces[::-1], axis=0)

            dvv = dvv + cat(dv_st)
            dqf = dqf + cat(dqf_st)
            dkl = cat(dkl_c)
            dq = dqf * eb + dqb * enb
            dkk = dkf * enb + dkb * eb + dkl * el
            dv_ref[:, vcols] = dvv.astype(BF16)
            if gla:
                db = dqf * qf - dkf * kf - dqb * qb + dkb * kb - dkl * kl
                dlast = _per_chunk(dkl * kl, lambda c: jnp.sum(c, axis=0, keepdims=True)) + cat(ddec_c) * dec
                dlog_a = _dot_sel(jnp.where(bk["up"], bk["ones"], 0.0), db) + dlast
                dpre = dlog_a * (1.0 / GATE_NORM) * (1.0 - _sigmoid(bk["pre"]))
                dpre_ref[:, kcols] = dpre
                dblr_ref[hh] += jnp.sum(dpre, axis=0, keepdims=True)
                dq_ref[:, kcols] = (dq * (dk ** -0.5)).astype(BF16)
                dk_ref[:, kcols] = dkk.astype(BF16)
            else:
                cs, sn, _ = aux
                half = dk // 2
                dkk = dkk * (dk ** -0.5)
                dq_ref[:, kcols] = (dq * cs + pltpu.roll(dq * sn, half, 1)).astype(BF16)
                dk_ref[:, kcols] = (dkk * cs + pltpu.roll(dkk * sn, half, 1)).astype(BF16)

    specs, blk = _lin_specs(gla, dk, dv, off, True, nb)
    vrow = pl.BlockSpec((LB, HPS * dv), lambda g, i: (blk(i), g))
    krow = pl.BlockSpec((LB, HPS * dk), lambda g, i: (blk(i), g))
    specs += [vrow, pl.BlockSpec((HPS, cb, dv, dk), lambda g, i: (g, blk(i), 0, 0)), vrow]
    out_specs = [krow, krow, vrow, vrow]
    out_shape = [_sds((t, d // 2), BF16), _sds((t, d // 2), BF16), _sds((t, d), BF16), _sds((t, d), BF16)]
    if gla:
        out_specs += [krow, pl.BlockSpec((HPS, 1, dk), lambda g, i: (g, 0, 0)),
                      pl.BlockSpec((HPS, 1, dv), lambda g, i: (g, 0, 0))]
        out_shape += [_sds((t, d // 2), F32), _sds((LIN_HEADS, 1, dk), F32), _sds((LIN_HEADS, 1, dv), F32)]
    out_specs.append(pl.BlockSpec((HPS, dv, dk), lambda g, i: (g, 0, 0)))
    out_shape.append(_sds((LIN_HEADS, dv, dk), F32))
    res = pl.pallas_call(
        body, name="gla_bwd" if gla else "ret_bwd", grid=(LIN_HEADS // HPS, nb), in_specs=specs,
        out_specs=out_specs, out_shape=out_shape,
        compiler_params=_cp(("parallel", "arbitrary")))(p, p, p, p, *aux_arrays, o, states, dbo)
    return res[:-1]


def _row_tile(rows, cols):
    cap = max(8, (2 << 20) // (4 * cols))
    t = rows
    while t > cap and t % 2 == 0:
        t //= 2
    return t


def _add_half(name, g, t, sel):
    nchip, hr, cols = t.shape
    tr = _row_tile(hr, cols)
    nb = hr // tr

    def body(sel_ref, g_ref, t_ref, o_ref):
        o_ref[...] = g_ref[...] + t_ref[...]

    half = pl.BlockSpec((None, tr, cols), lambda p, i, s: (p, i, 0))
    gs = pltpu.PrefetchScalarGridSpec(
        num_scalar_prefetch=1, grid=(nchip, nb),
        in_specs=[pl.BlockSpec((None, tr, cols), lambda p, i, s: (p, s[0] * nb + i, 0)), half], out_specs=half)
    return pl.pallas_call(body, name=name, grid_spec=gs, out_shape=_sds(t.shape, F32),
                          compiler_params=_cp(("parallel", "parallel")))(sel, g, t)


def _sum_shards(name, h, rcv, sel):
    _, rows, cols = h.shape
    tr = _row_tile(rows, cols)

    def body(sel_ref, h_ref, r0, r1, r2, o_ref):
        o_ref[...] = ((h_ref[...] + r0[...]) + r1[...]) + r2[...]

    rspecs = [pl.BlockSpec((None, tr, cols), functools.partial(lambda i, s, j: (j, i, 0), j=j)) for j in range(3)]
    gs = pltpu.PrefetchScalarGridSpec(
        num_scalar_prefetch=1, grid=(rows // tr,),
        in_specs=[pl.BlockSpec((None, tr, cols), lambda i, s: (s[0], i, 0))] + rspecs,
        out_specs=pl.BlockSpec((tr, cols), lambda i, s: (i, 0)))
    return pl.pallas_call(body, name=name, grid_spec=gs, out_shape=_sds((rows, cols), F32),
                          compiler_params=_cp(("parallel",)))(sel, h, rcv, rcv, rcv)


def _adamw_math(w, g, m, v):
    c1 = 1.0 - ADAM_B1 ** ADAM_STEP
    c2 = 1.0 - ADAM_B2 ** ADAM_STEP
    nm = ADAM_B1 * m + (1.0 - ADAM_B1) * g
    nv = ADAM_B2 * v + (1.0 - ADAM_B2) * jnp.square(g)
    return -ADAM_LR * ((nm / c1) / (jnp.sqrt(nv / c2) + ADAM_EPS) + ADAM_WD * w), nm, nv


def _adamw(name, w, g, m, v):
    rows, cols = w.shape
    tr = _row_tile(rows, cols)

    def body(w_ref, g_ref, m_ref, v_ref, d_ref, nm_ref, nv_ref):
        d_ref[...], nm_ref[...], nv_ref[...] = _adamw_math(w_ref[...], g_ref[...], m_ref[...], v_ref[...])

    spec = pl.BlockSpec((tr, cols), lambda i: (i, 0))
    return pl.pallas_call(body, name=name, grid=(rows // tr,), in_specs=[spec] * 4, out_specs=[spec] * 3,
                          out_shape=[_sds((rows, cols), F32)] * 3, compiler_params=_cp(("parallel",)))(w, g, m, v)


def _adamw_layer(name, w, g_own, g_sib, sel, m, v, layer, prev):
    depth, rows, cols = w.shape
    tr = _row_tile(rows // 2, cols)
    nbh = rows // 2 // tr
    nprev = 0 if prev is None else 4

    def body(sel_ref, w_ref, own_ref, sib_ref, m_ref, v_ref, *rest):
        go_ref, d_ref, nm_ref, nv_ref = rest[nprev:]
        gv = jnp.where(pl.program_id(0) // nbh == sel_ref[0], own_ref[...], sib_ref[...])
        go_ref[...] = gv
        d_ref[...], nm_ref[...], nv_ref[...] = _adamw_math(w_ref[...], gv, m_ref[...], v_ref[...])

    lay = pl.BlockSpec((None, tr, cols), lambda i, s: (layer, i, 0))
    hlf = pl.BlockSpec((tr, cols), lambda i, s: (i % nbh, 0))
    gs = pltpu.PrefetchScalarGridSpec(
        num_scalar_prefetch=1, grid=(2 * nbh,), in_specs=[lay, hlf, hlf, lay, lay] + [ANY] * nprev,
        out_specs=[lay] * 4)
    args = (sel, w, g_own, g_sib, m, v) + (() if prev is None else tuple(prev))
    return pl.pallas_call(
        body, name=name, grid_spec=gs, out_shape=[_sds((depth, rows, cols), F32)] * 4,
        input_output_aliases={6 + k: k for k in range(nprev)},
        compiler_params=_cp(("parallel",)))(*args)


def _adamw_colmajor(name, wt, mt, vt, halves, sel):
    c_dim, depth, r_dim = wt.shape
    hr = r_dim // 2

    def body(sel_ref, w_ref, m_ref, v_ref, *rest):
        g_refs, (go_ref, d_ref, nm_ref, nv_ref) = rest[:2 * depth], rest[2 * depth:]
        own_first = sel_ref[0] == 0
        for l in range(depth):
            own, sib = g_refs[2 * l][...], g_refs[2 * l + 1][...]
            g = jnp.concatenate([jnp.where(own_first, own, sib), jnp.where(own_first, sib, own)], axis=0).T
            go_ref[:, l, :] = g
            d_ref[:, l, :], nm_ref[:, l, :], nv_ref[:, l, :] = _adamw_math(w_ref[:, l, :], g, m_ref[:, l, :],
                                                                          v_ref[:, l, :])

    col = pl.BlockSpec((LANE, depth, r_dim), lambda j, s: (j, 0, 0))
    gs = pltpu.PrefetchScalarGridSpec(
        num_scalar_prefetch=1, grid=(c_dim // LANE,),
        in_specs=[col] * 3 + [pl.BlockSpec((hr, LANE), lambda j, s: (0, j))] * (2 * depth), out_specs=[col] * 4)
    flat = [h for pair in halves for h in pair]
    return pl.pallas_call(body, name=name, grid_spec=gs, out_shape=[_sds(wt.shape, F32)] * 4,
                          compiler_params=_cp(("parallel",)))(sel, wt, mt, vt, *flat)


def _adamw_tail(name, wt, mt, vt, gt_tail, prev):
    c_dim, depth, r_dim = wt.shape
    nt = gt_tail.shape[0]

    def body(w_ref, m_ref, v_ref, g_ref, *rest):
        go_ref, d_ref, nm_ref, nv_ref = rest[4:]
        g = g_ref[...]
        go_ref[...] = g
        d_ref[...], nm_ref[...], nv_ref[...] = _adamw_math(w_ref[...], g, m_ref[...], v_ref[...])

    tail = pl.BlockSpec((nt, depth, r_dim), lambda i: (c_dim // nt - 1, 0, 0))
    return pl.pallas_call(
        body, name=name, grid=(1,), in_specs=[tail] * 3 + [pl.BlockSpec((nt, depth, r_dim), lambda i: (0, 0, 0))]
        + [ANY] * 4, out_specs=[tail] * 4, out_shape=[_sds(wt.shape, F32)] * 4,
        input_output_aliases={4 + k: k for k in range(4)},
        compiler_params=_cp(("arbitrary",)))(wt, mt, vt, gt_tail, *prev)


def _place():
    x, y, c = (lax.axis_index(a) for a in MESH_AXES)
    chips = [(1 - x, y), (x, 1 - y), (1 - x, 1 - y)]
    return x, y, c, chips


def _chip_index(xy):
    return 2 * xy[0] + xy[1]


ANY = pl.BlockSpec(memory_space=pl.ANY)


HBM_SPEC = pl.BlockSpec(memory_space=pltpu.HBM)
SEM = pl.BlockSpec(memory_space=pltpu.SEMAPHORE)
EFFECT = pltpu.SideEffectType.DATAFLOW_SIDE_EFFECTING


def _half(ref, c):
    hr = ref.shape[-2] // 2
    return pl.ds(pl.multiple_of(c * hr, 16), hr)


def _gather_copies(srcs, lands, send, recv):
    x, y, c, chips = _place()
    me = _chip_index((x, y))
    return [pltpu.make_async_remote_copy(src_ref=s.at[_half(s, c)], dst_ref=g.at[me, _half(s, c)],
                                         send_sem=send.at[3 * a + j], recv_sem=recv.at[3 * a + j],
                                         device_id=(*ch, c), device_id_type=DEV)
            for a, (s, g) in enumerate(zip(srcs, lands)) for j, ch in enumerate(chips)]


def _scatter_copies(srcs, lands, send, recv):
    x, y, c, chips = _place()
    return [pltpu.make_async_remote_copy(src_ref=h.at[_chip_index(ch)], dst_ref=r.at[j],
                                         send_sem=send.at[3 * a + j], recv_sem=recv.at[3 * a + j],
                                         device_id=(*ch, c), device_id_type=DEV)
            for a, (h, r) in enumerate(zip(srcs, lands)) for j, ch in enumerate(chips)]


def _in_hbm(a):
    return pltpu.with_memory_space_constraint(a, pltpu.HBM)


def _split_start(name, srcs, land_shapes, copies_fn, after=None):
    ns, nl = len(srcs), len(land_shapes)
    ncp = 3 * ns
    lands = [lax.empty(s.shape, s.dtype) for s in land_shapes]
    behind = [] if after is None else [after]

    def body(*refs):
        src, land = refs[:ns], refs[ns:ns + nl]
        send, recv = refs[ns + nl + len(behind)], refs[ns + nl + len(behind) + 1]
        for cp in copies_fn(src, land, send, recv):
            cp.start()
        refs[-1][...] = jnp.zeros_like(refs[-1])

    bufs = list(srcs) + lands
    outs = pl.pallas_call(
        body, name=name, in_specs=[HBM_SPEC] * (ns + nl) + [ANY] * len(behind),
        out_specs=[SEM, SEM] + [HBM_SPEC] * (ns + nl) + [pl.BlockSpec(memory_space=pltpu.VMEM)],
        out_shape=[pltpu.SemaphoreType.DMA((ncp,)), pltpu.SemaphoreType.DMA((ncp,))]
        + [pltpu.HBM(b.shape, b.dtype) for b in bufs] + [_sds((8, LANE), F32)],
        input_output_aliases={i: 2 + i for i in range(ns + nl)},
        compiler_params=pltpu.CompilerParams(has_side_effects=EFFECT))(*[_in_hbm(b) for b in bufs], *behind)
    return outs[0], outs[1], list(outs[2:2 + ns]), list(outs[2 + ns:2 + ns + nl]), outs[-1]


def _split_wait(name, started, copies_fn, after):
    send, recv, srcs, lands, _ = started
    ns, nl = len(srcs), len(lands)

    def body(*refs):
        src, land = refs[:ns], refs[ns:ns + nl]
        for cp in copies_fn(src, land, refs[ns + nl], refs[ns + nl + 1]):
            cp.wait_send()
            cp.wait_recv()

    bufs = list(srcs) + list(lands)
    outs = pl.pallas_call(
        body, name=name, in_specs=[HBM_SPEC] * (ns + nl) + [SEM, SEM, ANY], out_specs=[HBM_SPEC] * (ns + nl),
        out_shape=[pltpu.HBM(b.shape, b.dtype) for b in bufs],
        input_output_aliases={i: i for i in range(ns + nl)},
        compiler_params=pltpu.CompilerParams(has_side_effects=EFFECT))(*bufs, send, recv, after)
    return list(outs[:ns]), list(outs[ns:])


def _gather_plain(name, srcs):
    n = len(srcs)

    def body(*refs):
        src, land = refs[:n], refs[n:2 * n]
        send, recv, fsend, frecv = refs[2 * n:]
        first = _gather_copies(src, land, send, recv)
        for cp in first:
            cp.start()
        _forward_body(land, first, fsend, frecv)

    return pl.pallas_call(
        body, name=name, in_specs=[ANY] * n, out_specs=[ANY] * n,
        out_shape=[_sds((4,) + s.shape, s.dtype) for s in srcs],
        scratch_shapes=[pltpu.SemaphoreType.DMA((3 * n,))] * 4)(*srcs)


def _forward_body(land, arrivals, fsend, frecv):
    x, y, c, chips = _place()
    n = len(land)
    passed = []
    for a in range(n):
        for j, ch in enumerate(chips):
            if arrivals is not None:
                arrivals[3 * a + j].wait_recv()
            slot = land[a].at[_chip_index(ch), _half(land[a], c)]
            fw = pltpu.make_async_remote_copy(src_ref=slot, dst_ref=slot, send_sem=fsend.at[3 * a + j],
                                              recv_sem=frecv.at[3 * a + j], device_id=(x, y, 1 - c),
                                              device_id_type=DEV)
            fw.start()
            passed.append(fw)
    for a in range(n):
        for j, ch in enumerate(chips):
            slot = land[a].at[_chip_index(ch), _half(land[a], 1 - c)]
            pltpu.make_async_remote_copy(src_ref=slot, dst_ref=slot, send_sem=fsend.at[3 * a + j],
                                         recv_sem=frecv.at[3 * a + j], device_id=(x, y, c),
                                         device_id_type=DEV).wait_recv()
    for cp in passed:
        cp.wait_send()
    if arrivals is not None:
        for cp in arrivals:
            cp.wait_send()


def _gather_forward(name, lands):
    n = len(lands)

    def body(*refs):
        _forward_body(refs[n:2 * n], None, refs[2 * n], refs[2 * n + 1])

    return pl.pallas_call(
        body, name=name, in_specs=[ANY] * n, out_specs=[ANY] * n,
        out_shape=[_sds(g.shape, g.dtype) for g in lands], input_output_aliases={a: a for a in range(n)},
        scratch_shapes=[pltpu.SemaphoreType.DMA((3 * n,))] * 2)(*lands)


def _sibling_halves(name, grs):
    n = len(grs)

    def body(*refs):
        ins, outs = refs[:n], refs[n:2 * n]
        send, recv = refs[2 * n:]
        x, y, c, _ = _place()
        cps = [pltpu.make_async_remote_copy(src_ref=ins[a].at[:, _half(ins[a], 1 - c)], dst_ref=outs[a],
                                            send_sem=send.at[a], recv_sem=recv.at[a], device_id=(x, y, 1 - c),
                                            device_id_type=DEV) for a in range(n)]
        for cp in cps:
            cp.start()
        for cp in cps:
            cp.wait()

    return pl.pallas_call(
        body, name=name, in_specs=[ANY] * n, out_specs=[ANY] * n,
        out_shape=[_sds((g.shape[0], g.shape[1] // 2, g.shape[2]), F32) for g in grs],
        scratch_shapes=[pltpu.SemaphoreType.DMA((n,)), pltpu.SemaphoreType.DMA((n,))])(*grs)


def _sibling_share(name, sms):
    n = len(sms)

    def body(*refs):
        ins, outs = refs[:n], refs[n:2 * n]
        send, recv = refs[2 * n:]
        x, y, c, _ = _place()
        cps = [pltpu.make_async_remote_copy(src_ref=ins[a], dst_ref=outs[a], send_sem=send.at[a],
                                            recv_sem=recv.at[a], device_id=(x, y, 1 - c), device_id_type=DEV)
               for a in range(n)]
        for cp in cps:
            cp.start()
        for cp in cps:
            cp.wait()

    return pl.pallas_call(
        body, name=name, in_specs=[ANY] * n, out_specs=[ANY] * n, out_shape=[_sds(s.shape, F32) for s in sms],
        scratch_shapes=[pltpu.SemaphoreType.DMA((n,))] * 2)(*sms)


def _small_allreduce(v, after=None):
    rows = v.shape[0]
    ndev = 8
    behind = [] if after is None else [after]

    def body(v_ref, *rest):
        o_ref, gat_ref, send, recv = rest[len(behind):]
        x, y, c, _ = _place()
        me = 4 * x + 2 * y + c
        cps = []
        for k in range(1, ndev):
            to = (me + k) % ndev
            cp = pltpu.make_async_remote_copy(src_ref=v_ref, dst_ref=gat_ref.at[me], send_sem=send.at[k - 1],
                                              recv_sem=recv.at[me], device_id=(to // 4, (to // 2) % 2, to % 2),
                                              device_id_type=DEV)
            cp.start()
            cps.append(cp)
        gat_ref[me] = v_ref[...]
        for k in range(1, ndev):
            frm = (me + k) % ndev
            pltpu.make_async_remote_copy(src_ref=v_ref, dst_ref=gat_ref.at[frm], send_sem=send.at[k - 1],
                                         recv_sem=recv.at[frm], device_id=(x, y, c), device_id_type=DEV).wait_recv()
        for cp in cps:
            cp.wait_send()
        acc = gat_ref[0]
        for k in range(1, ndev):
            acc = acc + gat_ref[k]
        o_ref[...] = acc

    vm = pl.BlockSpec(memory_space=pltpu.VMEM)
    return pl.pallas_call(
        body, name="small_allreduce", in_specs=[vm] + [ANY] * len(behind), out_specs=vm,
        out_shape=_sds((rows, LANE), F32),
        scratch_shapes=[pltpu.VMEM((ndev, rows, LANE), F32), pltpu.SemaphoreType.DMA((ndev - 1,)),
                        pltpu.SemaphoreType.DMA((ndev,))])(v, *behind)


def _layout(d):
    half = d // 2
    names = [("aq", d), ("ak", d), ("av", d), ("rq", half), ("rk", half), ("rv", d), ("rg", d),
             ("gq", half), ("gk", half), ("gv", d), ("gg", d), ("gates", 3 * d), ("glr", 2 * LANE)]
    off, pos = {}, 0
    for nm, sz in names:
        off[nm] = pos
        pos += sz
    return off, pos


def _unpad_cols(g, d):
    a = 8 * d + d
    return jnp.concatenate([g[..., :a], g[..., a + 3 * d:a + 3 * d + GATE_RANK], g[..., a:a + 3 * d]], axis=-1)


def kernel(x, ln_in_g, ln_in_b, w_in, rel_bias, gla_w_lr, gla_b_lr, gla_norm_g, w_branch, w_out, ln1_g, ln1_b, w_up, w_down, ln2_g, ln2_b, loss_target, m_ln_in_g, m_ln_in_b, m_w_in, m_rel_bias, m_gla_w_lr, m_gla_b_lr, m_gla_norm_g, m_w_branch, m_w_out, m_ln1_g, m_ln1_b, m_w_up, m_w_down, m_ln2_g, m_ln2_b, v_ln_in_g, v_ln_in_b, v_w_in, v_rel_bias, v_gla_w_lr, v_gla_b_lr, v_gla_norm_g, v_w_branch, v_w_out, v_ln1_g, v_ln1_b, v_w_up, v_w_down, v_ln2_g, v_ln2_b):
    t, d = x.shape[1], x.shape[2]
    dff = 4 * d
    half = d // 2
    off, npad = _layout(d)
    xi, yi, ci = (lax.axis_index(a) for a in MESH_AXES)
    chip = 2 * xi + yi
    csel = jnp.reshape(ci, (1,)).astype(jnp.int32)
    psel = jnp.reshape(chip, (1,)).astype(jnp.int32)

    big_w = [w_in, w_branch.reshape(DEPTH, -1, d), w_out, w_up, w_down]
    big_m = [m_w_in, m_w_branch.reshape(DEPTH, -1, d), m_w_out, m_w_up, m_w_down]
    big_v = [v_w_in, v_w_branch.reshape(DEPTH, -1, d), v_w_out, v_w_up, v_w_down]
    W_IN, REST = [0], [1, 2, 3, 4]

    def shards_of(l, idx):
        return [big_w[i][l].astype(BF16) for i in idx]

    def lands_of(srcs):
        return [_sds((4,) + s.shape, s.dtype) for s in srcs]

    def full_w_in(g):
        per = g.shape[2]
        a = 8 * d + d

        def run(lo, hi):
            cuts = [(max(lo, c * per), min(hi, (c + 1) * per), c) for c in range(4)]
            return [g[c, :, x - c * per:y - c * per] for x, y, c in cuts if x < y]

        zeros = jnp.zeros((d, 2 * LANE - GATE_RANK), g.dtype)
        return jnp.concatenate(run(0, a) + run(a + GATE_RANK, 4 * per) + run(a, a + GATE_RANK) + [zeros], axis=1)

    def full_rest(gs):
        g_br, g_out, g_up, g_down = gs
        return (jnp.transpose(g_br.reshape(4, N_BRANCH, d // 4, d), (1, 0, 2, 3)).reshape(N_BRANCH, d, d),
                g_out.reshape(d, d), jnp.transpose(g_up, (1, 0, 2)).reshape(d, dff), g_down.reshape(dff, d))

    def with_own(srcs, lands):
        return [lax.dynamic_update_slice(g, s[None], (chip, 0, 0)) for s, g in zip(srcs, lands)]

    def gather_start(tag, l, idx, after):
        srcs = shards_of(l, idx)
        return srcs, _split_start(f"gather_{tag}{l}_start", srcs, lands_of(srcs), _gather_copies, after)

    def gather_finish(tag, l, pending, after):
        srcs, started = pending
        _, lands = _split_wait(f"gather_{tag}{l}_wait", started, _gather_copies, after)
        return with_own(srcs, _gather_forward(f"gather_{tag}{l}_pass", lands))

    def token(pending):
        return pending[1][4][0, 0]

    win, wbr, wout, wup, wdown = ([None] * DEPTH for _ in range(5))
    src_first = shards_of(0, W_IN)
    g_first = with_own(src_first, _gather_plain("gather_in0", src_first))
    win[0] = full_w_in(g_first[0])

    dkh = half // LIN_HEADS
    lr_rows = DEPTH * GATE_RANK
    lr_slab = jnp.zeros((lr_rows, 4, half // 4), F32)
    lr_slab = lax.dynamic_update_slice(lr_slab, (gla_w_lr.reshape(lr_rows, 1, half // 4) * jnp.where(ci == 0, 1.0, 0.0)),
                                       (0, chip, 0))
    wlr_full = _small_allreduce(lr_slab.reshape(-1, LANE)).reshape(DEPTH, GATE_RANK, half)
    wlr_pad = jnp.concatenate([wlr_full, jnp.zeros((DEPTH, LANE - GATE_RANK, half), F32)], axis=1)
    pend_rest = gather_start("rest", 0, REST, wlr_full[0, :1, :1] + g_first[0][0, :1, :1].astype(F32))

    inv = 10000.0 ** (-jnp.arange(0, dkh, 2, dtype=F32) / dkh)
    ang = jnp.arange(t, dtype=F32)[:, None] * inv[None, :]
    cos, sin = jnp.cos(ang), jnp.sin(ang)
    rope_c = jnp.concatenate([cos, cos], axis=1)
    rope_s = jnp.concatenate([-sin, sin], axis=1)
    log_gamma = jnp.log1p(-jnp.exp2(-5.0 - jnp.arange(LIN_HEADS, dtype=F32)))
    lg_tab = jnp.broadcast_to(log_gamma[:, None, None], (LIN_HEADS, 1, dkh))

    def vec(a):
        return a.reshape(1, -1)

    x0, x0b, xh_in, rs_in = _ln_in(x[0], vec(ln_in_g) + token(pend_rest), vec(ln_in_b))
    saved = []
    xl, xlb = x0, x0b
    for l in range(DEPTH):
        p = _mm("proj_in", xlb, win[l], 512, 1792)
        g_rest = gather_finish("rest", l, pend_rest, p)
        wbr[l], wout[l], wup[l], wdown[l] = full_rest(g_rest)
        tok = 0.0
        if l + 1 < DEPTH:
            pend_in = gather_start("in", l + 1, W_IN, g_rest[0])
            tok = token(pend_in)
        bias = _bias_expand(rel_bias[l] + tok)
        attn = _attn_fwd(p, bias, d, off)
        ret_aux = (rope_c, rope_s, lg_tab + tok)
        gla_aux = (p, wlr_pad[l], vec(gla_b_lr[l]) + tok, vec(gla_norm_g[l]))
        o_ret, b_ret, st_ret = _lin_fwd(False, p, ret_aux, d, off)
        o_gla, b_gla, st_gla = _lin_fwd(True, p, gla_aux, d, off)
        bo = jnp.stack([attn, b_ret, b_gla])
        tok = 0.0
        if l + 1 < DEPTH:
            g_in = gather_finish("in", l + 1, pend_in, bo)
            win[l + 1] = full_w_in(g_in[0])
            pend_rest = gather_start("rest", l + 1, REST, g_in[0])
            tok = token(pend_rest)
        proj, merged = _merge_fwd(bo, wbr[l], p, off["gates"])
        x1, x1b, xh1, rs1 = _mm_res_ln("out_proj_ln", merged, wout[l], xl, vec(ln1_g[l]) + tok, vec(ln1_b[l]),
                                       256, False)
        u = _mm("mlp_up", x1b, wup[l], 1024, 1024)
        x2, x2b, xh2, rs2, act = _mm_res_ln("mlp_down_ln", u, wdown[l], x1, vec(ln2_g[l]), vec(ln2_b[l]), 256, True)
        saved.append(dict(xlb=xlb, p=p, bias=bias, ret_aux=ret_aux, gla_aux=gla_aux, o_ret=o_ret, o_gla=o_gla,
                          st_ret=st_ret, st_gla=st_gla, bo=bo, proj=proj, merged=merged, x1b=x1b, xh1=xh1,
                          rs1=rs1, u=u, xh2=xh2, rs2=rs2, act=act))
        xl, xlb = x2, x2b

    small = {}
    last = saved[-1]
    loss_p, dz2, dz2b, dg, db = _loss_ln_bwd(xl, loss_target[0], last["xh2"], last["rs2"], vec(ln2_g[DEPTH - 1]))
    small["loss"] = loss_p[:, :1]
    grad_x = None

    def scatter_start(tag, l, idx, shards, after):
        theirs = _sibling_halves(f"grad_{tag}{l}_sibling", shards)
        hs = [_add_half("grad_sibling_add", g, th, csel) for g, th in zip(shards, theirs)]
        lands = [_sds((3,) + h.shape[1:], F32) for h in hs]
        return tag, l, idx, _split_start(f"grad_{tag}{l}_scatter_start", hs, lands, _scatter_copies, after)

    adam_out = [None] * len(big_w)
    w_in_halves = [None] * DEPTH

    def scatter_finish(pending, after):
        tag, l, idx, started = pending
        hs, rcv = _split_wait(f"grad_{tag}{l}_scatter_wait", started, _scatter_copies, after)
        sms = [_sum_shards("grad_chip_sum", h, r, psel) for h, r in zip(hs, rcv)]
        for i, own, sib in zip(idx, sms, _sibling_share(f"grad_{tag}{l}_share", sms)):
            if i == W_IN[0]:
                w_in_halves[l] = (own, sib)
            else:
                adam_out[i] = _adamw_layer("adamw_large", big_w[i], own, sib, csel, big_m[i], big_v[i], l,
                                           adam_out[i])
        return sms[0]

    in_flight = []

    def scatter(tag, l, idx, shards, after=None):
        pending = scatter_start(tag, l, idx, shards, after)
        in_flight.append(pending)
        if len(in_flight) > 3:
            scatter_finish(in_flight.pop(0), pending[3][4])
        return pending[3][4][0, 0]

    for l in reversed(range(DEPTH)):
        s = saved[l]
        small[("ln2_g", l)], small[("ln2_b", l)] = dg, db
        du = _mm_nt_relu2_bwd(dz2b, wdown[l], s["u"])
        g_wdown = _mm_tn("grad_w_down", s["act"], dz2b, 512, 512)
        g_wup = _mm_tn("grad_w_up", s["x1b"], du, 512, 512, shard="cols")
        dz1, dz1b, dg1, db1 = _mm_nt_res_lnbwd("mlp_up_bwd_ln", du, wup[l], dz2, s["xh1"], s["rs1"],
                                               vec(ln1_g[l]), 256, dff)
        small[("ln1_g", l)], small[("ln1_b", l)] = dg1, db1
        dproj, dgl = _merge_bwd(dz1b, wout[l], s["proj"], s["p"], off["gates"])
        g_wout = _mm_tn("grad_w_out", s["merged"], dz1b, 512, 512)
        dbo = _mm("branch_proj_bwd", dproj, wbr[l], 1024, 1024, nt=True)
        g_wbr = _mm_tn("grad_w_branch", s["bo"], dproj, d // 4, 1024, shard="rows")
        tok = scatter("rest", l, REST, [g_wbr, g_wout.reshape(4, d // 4, d), g_wup, g_wdown.reshape(4, d, d)])
        rc, rs_, lg = s["ret_aux"]
        gp, gw, gb, gn_ = s["gla_aux"]
        dq_a, dk_acc, dv_acc, dbias = _attn_bwd(s["p"], s["bias"] + tok, dbo[0], d, off)
        small[("rel_bias", l)] = _bias_reduce(dbias)
        dk_a = dk_acc[2 * QB:].astype(BF16)
        dv_a = dv_acc[2 * QB:].astype(BF16)
        dq_r, dk_r, dv_r, dg_r = _lin_bwd(False, s["p"], (rc, rs_, lg + tok), s["o_ret"], s["st_ret"], dbo[1], d, off)
        dq_g, dk_g, dv_g, dg_g, dpre, dblr, dgn = _lin_bwd(True, s["p"], (gp, gw, gb + tok, gn_), s["o_gla"],
                                                           s["st_gla"], dbo[2], d, off)
        small[("gla_b_lr", l)] = dblr.reshape(1, half)
        small[("gla_norm_g", l)] = jnp.sum(dgn, axis=0)
        dpre_b = dpre.astype(BF16)
        glr_b = s["p"][:, off["glr"]:off["glr"] + LANE].astype(BF16)
        dglr = _mm("gate_lr_bwd", dpre_b, wlr_pad[l], 512, LANE, nt=True, out_dtype=BF16)
        small[("gla_w_lr", l)] = _mm_tn("grad_gla_w_lr", glr_b, dpre_b, LANE, half)[:GATE_RANK]
        dp = jnp.concatenate([dq_a, dk_a, dv_a, dq_r, dk_r, dv_r, dg_r, dq_g, dk_g, dv_g, dg_g,
                              dgl[0], dgl[1], dgl[2], dglr, jnp.zeros((t, LANE), BF16)], axis=1)
        if l > 0:
            prev = saved[l - 1]
            xh_p, rs_p, g_p = prev["xh2"], prev["rs2"], vec(ln2_g[l - 1])
        else:
            xh_p, rs_p, g_p = xh_in, rs_in, vec(ln_in_g)
        g_win = _mm_tn("grad_w_in", s["xlb"], dp, 1024, 896)
        tok = scatter("in", l, W_IN, [jnp.transpose(_unpad_cols(g_win, d).reshape(d, 4, -1), (1, 0, 2))])
        dzp, dzpb, dg, db = _mm_nt_res_lnbwd("proj_in_bwd_ln", dp, win[l], dz1, xh_p, rs_p, g_p + tok, 1024, 1792)
        dz2, dz2b = dzp, dzpb
        grad_x = dzp
    after = grad_x
    while in_flight:
        after = scatter_finish(in_flight.pop(0), after)
    wt, mt, vt = (jnp.transpose(a, (2, 0, 1)) for a in (big_w[0], big_m[0], big_v[0]))
    ntail = wt.shape[0] % LANE
    tails = [jnp.where(ci == 0, jnp.concatenate([own[:, -ntail:], sib[:, -ntail:]]),
                       jnp.concatenate([sib[:, -ntail:], own[:, -ntail:]])).T for own, sib in w_in_halves]
    adam_t = _adamw_tail("adamw_w_in_tail", wt, mt, vt, jnp.stack(tails, axis=1),
                         _adamw_colmajor("adamw_w_in", wt, mt, vt, w_in_halves, csel))
    adam_out[0] = [jnp.transpose(r, (1, 2, 0)) for r in adam_t]
    small["ln_in_g"], small["ln_in_b"] = dg, db
    rb_pad = 3 * LANE
    pieces = [small["loss"].reshape(-1), jnp.zeros((LANE - 1,), F32), small["ln_in_g"].reshape(-1),
              small["ln_in_b"].reshape(-1)]
    for l in range(DEPTH):
        rb = jnp.pad(small[("rel_bias", l)], ((0, 0), (0, rb_pad - (2 * REL_CLIP + 1))))
        pieces += [rb.reshape(-1), small[("gla_w_lr", l)].reshape(-1), small[("gla_b_lr", l)].reshape(-1),
                   small[("gla_norm_g", l)].reshape(-1), small[("ln1_g", l)].reshape(-1),
                   small[("ln1_b", l)].reshape(-1), small[("ln2_g", l)].reshape(-1), small[("ln2_b", l)].reshape(-1)]
    sizes = [pc.shape[0] for pc in pieces]
    packed = jnp.concatenate(pieces)
    padn = (-packed.shape[0]) % (8 * LANE)
    packed = jnp.concatenate([packed, jnp.zeros((padn,), F32)]).reshape(-1, LANE)
    red = _small_allreduce(packed, after).reshape(-1)

    parts, pos = [], 0
    for sz in sizes:
        parts.append(red[pos:pos + sz])
        pos += sz
    loss = parts[0][0]
    g_ln_in_g, g_ln_in_b = parts[2], parts[3]
    per = 8
    g_rel = jnp.stack([parts[4 + per * l].reshape(ATTN_HEADS, rb_pad)[:, :2 * REL_CLIP + 1] for l in range(DEPTH)])
    g_wlr_full = jnp.stack([parts[5 + per * l].reshape(GATE_RANK, half) for l in range(DEPTH)])
    g_wlr = lax.dynamic_slice_in_dim(g_wlr_full, chip * (half // 4), half // 4, axis=2)
    g_blr = jnp.stack([parts[6 + per * l] for l in range(DEPTH)])
    g_gn = jnp.stack([parts[7 + per * l] for l in range(DEPTH)])
    g_ln1g = jnp.stack([parts[8 + per * l] for l in range(DEPTH)])
    g_ln1b = jnp.stack([parts[9 + per * l] for l in range(DEPTH)])
    g_ln2g = jnp.stack([parts[10 + per * l] for l in range(DEPTH)])
    g_ln2b = jnp.stack([parts[11 + per * l] for l in range(DEPTH)])

    grads = [g_ln_in_g, g_ln_in_b, None, g_rel, g_wlr, g_blr, g_gn, None, None, g_ln1g, g_ln1b, None, None,
             g_ln2g, g_ln2b]
    ws = [ln_in_g, ln_in_b, w_in, rel_bias, gla_w_lr, gla_b_lr, gla_norm_g, w_branch, w_out, ln1_g, ln1_b,
          w_up, w_down, ln2_g, ln2_b]
    ms = [m_ln_in_g, m_ln_in_b, m_w_in, m_rel_bias, m_gla_w_lr, m_gla_b_lr, m_gla_norm_g, m_w_branch, m_w_out,
          m_ln1_g, m_ln1_b, m_w_up, m_w_down, m_ln2_g, m_ln2_b]
    vs = [v_ln_in_g, v_ln_in_b, v_w_in, v_rel_bias, v_gla_w_lr, v_gla_b_lr, v_gla_norm_g, v_w_branch, v_w_out,
          v_ln1_g, v_ln1_b, v_w_up, v_w_down, v_ln2_g, v_ln2_b]

    deltas, new_ms, new_vs = [None] * 15, [None] * 15, [None] * 15
    big_idx = [2, 7, 8, 11, 12]
    for i, res in zip(big_idx, adam_out):
        shp = ws[i].shape
        grads[i], deltas[i], new_ms[i], new_vs[i] = (r.reshape(shp) for r in res)
    small_idx = [i for i in range(15) if i not in big_idx]

    def pack(arrs):
        flat_ = jnp.concatenate([arrs[i].reshape(-1) for i in small_idx])
        pad_ = (-flat_.shape[0]) % (8 * LANE)
        return jnp.concatenate([flat_, jnp.ones((pad_,), F32)]).reshape(-1, LANE)

    dl, nm, nv = _adamw("adamw_small", pack(ws), pack(grads), pack(ms), pack(vs))
    pos = 0
    for i in small_idx:
        sz = int(np.prod(ws[i].shape))
        deltas[i] = dl.reshape(-1)[pos:pos + sz].reshape(ws[i].shape)
        new_ms[i] = nm.reshape(-1)[pos:pos + sz].reshape(ws[i].shape)
        new_vs[i] = nv.reshape(-1)[pos:pos + sz].reshape(ws[i].shape)
        pos += sz

    return (loss, grad_x[None], *grads, *deltas, *new_ms, *new_vs)
```

```python
import functools

import numpy as np
import jax
import jax.numpy as jnp
from jax import lax
from jax.experimental import pallas as pl
from jax.experimental.pallas import tpu as pltpu

F32 = jnp.float32
BF16 = jnp.bfloat16
MXU_DTYPE = BF16
HI = lax.Precision.HIGHEST

DEPTH = 2
CHUNK = 64
N_BRANCH = 3
ATTN_HEADS = 8
ATTN_LEFT = 8
REL_CLIP = 2 * CHUNK
LIN_HEADS = 4
GATE_RANK = 16
GATE_NORM = 16.0
LN_EPS = 1e-5
NEG_INF = -1e30
ALPHA = (2 * DEPTH) ** 0.25
ADAM_LR, ADAM_B1, ADAM_B2, ADAM_EPS, ADAM_WD, ADAM_STEP = 0.001, 0.9, 0.999, 1e-08, 0.01, 10

LANE = 128
VMEM_LIMIT = 56 << 20
QB = 256
KW = 3 * QB
LB = 256
MESH_AXES = ("x", "y", "c")
DEV = pl.DeviceIdType.MESH


def _cp(sem):
    return pltpu.CompilerParams(dimension_semantics=sem, vmem_limit_bytes=VMEM_LIMIT)


def _mx(v):
    return v.astype(MXU_DTYPE)


def _dot(a, b):
    return jnp.dot(_mx(a), _mx(b), preferred_element_type=F32)


def _dot_nt(a, b):
    return lax.dot_general(_mx(a), _mx(b), (((1,), (1,)), ((), ())), preferred_element_type=F32)


def _dot_tn(a, b):
    return lax.dot_general(_mx(a), _mx(b), (((0,), (0,)), ((), ())), preferred_element_type=F32)


def _dot_hi(a, b):
    return jnp.dot(a, b, precision=HI, preferred_element_type=F32)


def _sigmoid(v):
    return 1.0 / (1.0 + jnp.exp(-v))


def _sds(shape, dtype):
    return jax.ShapeDtypeStruct(shape, dtype)


def _mm(name, a, b, tm, tn, nt=False, out_dtype=F32):
    batched = a.ndim == 3
    m, k = a.shape[-2:]
    n = b.shape[-2] if nt else b.shape[-1]
    tm, tn = min(tm, m), min(tn, n)

    def body(a_ref, b_ref, o_ref):
        f = _dot_nt if nt else _dot
        o_ref[...] = f(a_ref[...], b_ref[...]).astype(o_ref.dtype)

    rows_inner = (n // tn) * m < (m // tm) * n

    def ij(u, v):
        return (v, u) if rows_inner else (u, v)

    if batched:
        nb = a.shape[0]
        grid = (nb,) + ij(m // tm, n // tn)
        a_spec = pl.BlockSpec((None, tm, k), lambda g, u, v: (g, ij(u, v)[0], 0))
        b_spec = (pl.BlockSpec((None, tn, k), lambda g, u, v: (g, ij(u, v)[1], 0)) if nt
                  else pl.BlockSpec((None, k, tn), lambda g, u, v: (g, 0, ij(u, v)[1])))
        o_spec = pl.BlockSpec((None, tm, tn), lambda g, u, v: (g,) + ij(u, v))
        out_shape = _sds((nb, m, n), out_dtype)
        sem = ("parallel", "parallel", "parallel")
    else:
        grid = ij(m // tm, n // tn)
        a_spec = pl.BlockSpec((tm, k), lambda u, v: (ij(u, v)[0], 0))
        b_spec = (pl.BlockSpec((tn, k), lambda u, v: (ij(u, v)[1], 0)) if nt
                  else pl.BlockSpec((k, tn), lambda u, v: (0, ij(u, v)[1])))
        o_spec = pl.BlockSpec((tm, tn), lambda u, v: ij(u, v))
        out_shape = _sds((m, n), out_dtype)
        sem = ("parallel", "parallel")
    return pl.pallas_call(body, name=name, grid=grid, in_specs=[a_spec, b_spec], out_specs=o_spec,
                          out_shape=out_shape, compiler_params=_cp(sem))(a, b)


def _mm_tn(name, a, b, tm, tn, shard=None):
    batched = a.ndim == 3
    k, m = a.shape[-2:]
    n = b.shape[-1]
    tm, tn = min(tm, m), min(tn, n)

    def body(a_ref, b_ref, o_ref):
        o_ref[...] = lax.dot_general(_mx(a_ref[...]), _mx(b_ref[...]), (((0,), (0,)), ((), ())),
                                     preferred_element_type=F32)

    if batched:
        nb = a.shape[0]
        grid = (nb, m // tm, n // tn)
        a_spec = pl.BlockSpec((None, k, tm), lambda g, i, j: (g, 0, i))
        b_spec = pl.BlockSpec((None, k, tn), lambda g, i, j: (g, 0, j))
        if shard == "rows":
            assert 4 * tm == m
            o_spec = pl.BlockSpec((None, tm, tn), lambda g, i, j: (i, g, j))
            out_shape = _sds((4, nb * tm, n), F32)
        else:
            o_spec = pl.BlockSpec((None, tm, tn), lambda g, i, j: (g, i, j))
            out_shape = _sds((nb, m, n), F32)
    else:
        grid = (m // tm, n // tn)
        a_spec = pl.BlockSpec((k, tm), lambda i, j: (0, i))
        b_spec = pl.BlockSpec((k, tn), lambda i, j: (0, j))
        if shard == "cols":
            per = n // 4 // tn
            o_spec = pl.BlockSpec((None, tm, tn), lambda i, j: (j // per, i, j % per))
            out_shape = _sds((4, m, n // 4), F32)
        else:
            o_spec = pl.BlockSpec((tm, tn), lambda i, j: (i, j))
            out_shape = _sds((m, n), F32)
    return pl.pallas_call(body, name=name, grid=grid, in_specs=[a_spec, b_spec], out_specs=o_spec,
                          out_shape=out_shape, compiler_params=_cp(("parallel",) * len(grid)))(a, b)


def _ln_rows(y, g, b):
    mu = jnp.mean(y, axis=-1, keepdims=True)
    yc = y - mu
    var = jnp.mean(yc * yc, axis=-1, keepdims=True)
    rs = lax.rsqrt(var + LN_EPS)
    xh = yc * rs
    return xh * g + b, xh, rs


def _ln_in(x, g, b, tm=256):
    t, d = x.shape

    def body(x_ref, g_ref, b_ref, o_ref, ob_ref, xh_ref, rs_ref):
        o, xh, rs = _ln_rows(x_ref[...], g_ref[...], b_ref[...])
        o_ref[...] = o
        ob_ref[...] = o.astype(BF16)
        xh_ref[...] = xh
        rs_ref[...] = rs

    row = pl.BlockSpec((tm, d), lambda i: (i, 0))
    vec = pl.BlockSpec((1, d), lambda i: (0, 0))
    return pl.pallas_call(
        body, name="ln_in", grid=(t // tm,), in_specs=[row, vec, vec],
        out_specs=[row, row, row, pl.BlockSpec((tm, 1), lambda i: (i, 0))],
        out_shape=[_sds((t, d), F32), _sds((t, d), BF16), _sds((t, d), F32), _sds((t, 1), F32)],
        compiler_params=_cp(("parallel",)))(x, g, b)


def _mm_res_ln(name, a, w, res, g, b, tm, relu2):
    t, k = a.shape
    d = w.shape[1]

    def body(a_ref, w_ref, r_ref, g_ref, b_ref, o_ref, ob_ref, xh_ref, rs_ref, *act_ref):
        av = a_ref[...]
        if relu2:
            av = jnp.square(jnp.maximum(av, 0.0))
            act_ref[0][...] = av.astype(BF16)
        y = ALPHA * r_ref[...] + _dot(av, w_ref[...])
        o, xh, rs = _ln_rows(y, g_ref[...], b_ref[...])
        o_ref[...] = o
        ob_ref[...] = o.astype(BF16)
        xh_ref[...] = xh
        rs_ref[...] = rs

    row = pl.BlockSpec((tm, d), lambda i: (i, 0))
    vec = pl.BlockSpec((1, d), lambda i: (0, 0))
    arow = pl.BlockSpec((tm, k), lambda i: (i, 0))
    out_specs = [row, row, row, pl.BlockSpec((tm, 1), lambda i: (i, 0))]
    out_shape = [_sds((t, d), F32), _sds((t, d), BF16), _sds((t, d), F32), _sds((t, 1), F32)]
    if relu2:
        out_specs.append(arow)
        out_shape.append(_sds((t, k), BF16))
    return pl.pallas_call(
        body, name=name, grid=(t // tm,),
        in_specs=[arow, pl.BlockSpec((k, d), lambda i: (0, 0)), row, vec, vec],
        out_specs=out_specs, out_shape=out_shape, compiler_params=_cp(("parallel",)))(a, w, res, g, b)


def _merge_fwd(bo, wb, p, gate_off, tm=512, tn=512):
    _, t, d = bo.shape
    gb = gate_off // tn

    def body(bo_ref, wb_ref, g0, g1, g2, proj_ref, m_ref):
        acc = None
        for n, g_ref in enumerate((g0, g1, g2)):
            pr = _dot(bo_ref[n], wb_ref[n])
            proj_ref[n] = pr
            term = _sigmoid(g_ref[...]) * pr
            acc = term if acc is None else acc + term
        m_ref[...] = acc.astype(BF16)

    gspecs = [pl.BlockSpec((tm, tn), functools.partial(lambda i, j, n: (i, gb + n * (d // tn) + j), n=n))
              for n in range(3)]
    return pl.pallas_call(
        body, name="merge_fwd", grid=(t // tm, d // tn),
        in_specs=[pl.BlockSpec((3, tm, d), lambda i, j: (0, i, 0)),
                  pl.BlockSpec((3, d, tn), lambda i, j: (0, 0, j))] + gspecs,
        out_specs=[pl.BlockSpec((3, tm, tn), lambda i, j: (0, i, j)), pl.BlockSpec((tm, tn), lambda i, j: (i, j))],
        out_shape=[_sds((3, t, d), F32), _sds((t, d), BF16)],
        compiler_params=_cp(("parallel", "parallel")))(bo, wb, p, p, p)


def _merge_bwd(dz, wout, proj, p, gate_off, tm=512, tn=512):
    t, d = dz.shape
    gb = gate_off // tn

    def body(dz_ref, w_ref, proj_ref, g0, g1, g2, dproj_ref, dgl_ref):
        dm = _dot_nt(dz_ref[...], w_ref[...])
        for n, g_ref in enumerate((g0, g1, g2)):
            s = _sigmoid(g_ref[...])
            dproj_ref[n] = (dm * s).astype(BF16)
            dgl_ref[n] = (dm * proj_ref[n] * (s * (1.0 - s))).astype(BF16)

    gspecs = [pl.BlockSpec((tm, tn), functools.partial(lambda i, j, n: (i, gb + n * (d // tn) + j), n=n))
              for n in range(3)]
    dproj, dgl = pl.pallas_call(
        body, name="merge_bwd", grid=(t // tm, d // tn),
        in_specs=[pl.BlockSpec((tm, d), lambda i, j: (i, 0)), pl.BlockSpec((tn, d), lambda i, j: (j, 0)),
                  pl.BlockSpec((3, tm, tn), lambda i, j: (0, i, j))] + gspecs,
        out_specs=[pl.BlockSpec((3, tm, tn), lambda i, j: (0, i, j)),
                   pl.BlockSpec((3, tm, tn), lambda i, j: (0, i, j))],
        out_shape=[_sds((3, t, d), BF16), _sds((3, t, d), BF16)],
        compiler_params=_cp(("parallel", "parallel")))(dz, wout, proj, p, p, p)
    return dproj, dgl


def _mm_nt_relu2_bwd(dz, wdown, u, tm=512, tn=1024):
    t, d = dz.shape
    f = wdown.shape[0]

    def body(dz_ref, w_ref, u_ref, du_ref):
        da = _dot_nt(dz_ref[...], w_ref[...])
        du_ref[...] = (da * (2.0 * jnp.maximum(u_ref[...], 0.0))).astype(BF16)

    return pl.pallas_call(
        body, name="mlp_down_bwd", grid=(t // tm, f // tn),
        in_specs=[pl.BlockSpec((tm, d), lambda i, j: (i, 0)), pl.BlockSpec((tn, d), lambda i, j: (j, 0)),
                  pl.BlockSpec((tm, tn), lambda i, j: (i, j))],
        out_specs=pl.BlockSpec((tm, tn), lambda i, j: (i, j)), out_shape=_sds((t, f), BF16),
        compiler_params=_cp(("parallel", "parallel")))(dz, wdown, u)


def _ln_bwd_rows(dx, xh, rs, g):
    dxh = dx * g
    m1 = jnp.mean(dxh, axis=-1, keepdims=True)
    m2 = jnp.mean(dxh * xh, axis=-1, keepdims=True)
    return rs * (dxh - m1 - xh * m2)


def _mm_nt_res_lnbwd(name, a, w, dres, xh, rs, g, tm, tk):
    t, k = a.shape
    d = w.shape[0]
    nk = k // tk

    def body(a_ref, w_ref, dr_ref, xh_ref, rs_ref, g_ref, dz_ref, dzb_ref, dg_ref, db_ref, acc_ref):
        i, kk = pl.program_id(0), pl.program_id(1)

        @pl.when(kk == 0)
        def _():
            acc_ref[...] = ALPHA * dr_ref[...]

        acc_ref[...] += _dot_nt(a_ref[...], w_ref[...])

        @pl.when(jnp.logical_and(i == 0, kk == 0))
        def _():
            dg_ref[...] = jnp.zeros_like(dg_ref)
            db_ref[...] = jnp.zeros_like(db_ref)

        @pl.when(kk == nk - 1)
        def _():
            dx = acc_ref[...]
            xhv = xh_ref[...]
            dz = _ln_bwd_rows(dx, xhv, rs_ref[...], g_ref[...])
            dz_ref[...] = dz
            dzb_ref[...] = dz.astype(BF16)
            dg_ref[...] += jnp.sum(dx * xhv, axis=0, keepdims=True)
            db_ref[...] += jnp.sum(dx, axis=0, keepdims=True)

    row = pl.BlockSpec((tm, d), lambda i, kk: (i, 0))
    vec = pl.BlockSpec((1, d), lambda i, kk: (0, 0))
    return pl.pallas_call(
        body, name=name, grid=(t // tm, nk),
        in_specs=[pl.BlockSpec((tm, tk), lambda i, kk: (i, kk)), pl.BlockSpec((d, tk), lambda i, kk: (0, kk)),
                  row, row, pl.BlockSpec((tm, 1), lambda i, kk: (i, 0)), vec],
        out_specs=[row, row, vec, vec],
        out_shape=[_sds((t, d), F32), _sds((t, d), BF16), _sds((1, d), F32), _sds((1, d), F32)],
        scratch_shapes=[pltpu.VMEM((tm, d), F32)],
        compiler_params=_cp(("arbitrary", "arbitrary")))(a, w, dres, xh, rs, g)


def _loss_ln_bwd(x2, target, xh, rs, g, tm=256):
    t, d = x2.shape

    def body(x_ref, t_ref, xh_ref, rs_ref, g_ref, loss_ref, dz_ref, dzb_ref, dg_ref, db_ref):
        @pl.when(pl.program_id(0) == 0)
        def _():
            loss_ref[...] = jnp.zeros_like(loss_ref)
            dg_ref[...] = jnp.zeros_like(dg_ref)
            db_ref[...] = jnp.zeros_like(db_ref)

        err = x_ref[...] - t_ref[...]
        per_row = jnp.mean(err * err, axis=-1, keepdims=True)
        loss_ref[...] += 0.5 * jnp.sum(per_row, axis=0, keepdims=True)
        dx = err * (1.0 / d)
        xhv = xh_ref[...]
        dz = _ln_bwd_rows(dx, xhv, rs_ref[...], g_ref[...])
        dz_ref[...] = dz
        dzb_ref[...] = dz.astype(BF16)
        dg_ref[...] += jnp.sum(dx * xhv, axis=0, keepdims=True)
        db_ref[...] += jnp.sum(dx, axis=0, keepdims=True)

    row = pl.BlockSpec((tm, d), lambda i: (i, 0))
    vec = pl.BlockSpec((1, d), lambda i: (0, 0))
    return pl.pallas_call(
        body, name="loss_ln_bwd", grid=(t // tm,),
        in_specs=[row, row, row, pl.BlockSpec((tm, 1), lambda i: (i, 0)), vec],
        out_specs=[pl.BlockSpec((1, LANE), lambda i: (0, 0)), row, row, vec, vec],
        out_shape=[_sds((1, LANE), F32), _sds((t, d), F32), _sds((t, d), BF16), _sds((1, d), F32),
                   _sds((1, d), F32)],
        compiler_params=_cp(("arbitrary",)))(x2, target, xh, rs, g)


HPA = 2


def _attn_scores(q_ref, k_refs, bias_ref, i, dh, hh):
    cols = pl.ds(hh * dh, dh)
    q = q_ref[:, cols] * (dh ** -0.5)
    k = jnp.concatenate([r[:, cols] for r in k_refs], axis=0)
    s = _dot_nt(q, k) + bias_ref[hh]
    col = lax.broadcasted_iota(jnp.int32, s.shape, 1)
    s = jnp.where(col >= (2 - i) * QB, s, NEG_INF)
    m = jnp.max(s, axis=-1, keepdims=True)
    e = jnp.exp(s - m)
    return q, k, e / jnp.sum(e, axis=-1, keepdims=True)


def _attn_specs(dh, off):
    w = HPA * dh
    qcol, kcol, vcol = off["aq"] // w, off["ak"] // w, off["av"] // w
    q_spec = pl.BlockSpec((QB, w), lambda g, i: (i, qcol + g))
    k_specs = [pl.BlockSpec((QB, w), functools.partial(lambda g, i, j: (jnp.maximum(i - 2 + j, 0), kcol + g), j=j))
               for j in range(3)]
    v_specs = [pl.BlockSpec((QB, w), functools.partial(lambda g, i, j: (jnp.maximum(i - 2 + j, 0), vcol + g), j=j))
               for j in range(3)]
    bias_spec = pl.BlockSpec((HPA, QB, KW), lambda g, i: (g, 0, 0))
    return q_spec, k_specs, v_specs, bias_spec


def _attn_fwd(p, bias, d, off):
    t = p.shape[0]
    dh = d // ATTN_HEADS

    def body(q_ref, k0, k1, k2, v0, v1, v2, bias_ref, o_ref):
        for hh in range(HPA):
            cols = pl.ds(hh * dh, dh)
            _, _, pr = _attn_scores(q_ref, (k0, k1, k2), bias_ref, pl.program_id(1), dh, hh)
            v = jnp.concatenate([v0[:, cols], v1[:, cols], v2[:, cols]], axis=0)
            o_ref[:, cols] = _dot(pr, v).astype(o_ref.dtype)

    q_spec, k_specs, v_specs, bias_spec = _attn_specs(dh, off)
    return pl.pallas_call(
        body, name="attn_fwd", grid=(ATTN_HEADS // HPA, t // QB),
        in_specs=[q_spec] + k_specs + v_specs + [bias_spec],
        out_specs=pl.BlockSpec((None, QB, HPA * dh), lambda g, i: (0, i, g)),
        out_shape=_sds((N_BRANCH, t, d), BF16),
        compiler_params=_cp(("parallel", "parallel")))(p, p, p, p, p, p, p, bias)


def _attn_bwd(p, bias, do, d, off):
    t = p.shape[0]
    dh = d // ATTN_HEADS
    tp = t + 2 * QB

    def body(q_ref, k0, k1, k2, v0, v1, v2, bias_ref, do_ref, dq_ref, dk_ref, dv_ref, dbias_ref):
        i = pl.program_id(1)

        @pl.when(i == 0)
        def _():
            dk_ref[...] = jnp.zeros_like(dk_ref)
            dv_ref[...] = jnp.zeros_like(dv_ref)
            dbias_ref[...] = jnp.zeros_like(dbias_ref)

        rows = pl.ds(pl.multiple_of(i * QB, QB), KW)
        for hh in range(HPA):
            cols = pl.ds(hh * dh, dh)
            q, k, pr = _attn_scores(q_ref, (k0, k1, k2), bias_ref, i, dh, hh)
            v = jnp.concatenate([v0[:, cols], v1[:, cols], v2[:, cols]], axis=0)
            dov = do_ref[:, cols]
            dp = _dot_nt(dov, v)
            delta = jnp.sum(pr * dp, axis=-1, keepdims=True)
            ds = pr * (dp - delta)
            dbias_ref[hh] += ds
            dq_ref[:, cols] = (_dot(ds, k) * (dh ** -0.5)).astype(dq_ref.dtype)
            dk_ref[rows, cols] += _dot_tn(ds, q)
            dv_ref[rows, cols] += _dot_tn(pr, dov)

    q_spec, k_specs, v_specs, bias_spec = _attn_specs(dh, off)
    row_spec = pl.BlockSpec((QB, HPA * dh), lambda g, i: (i, g))
    acc_spec = pl.BlockSpec((tp, HPA * dh), lambda g, i: (0, g))
    return pl.pallas_call(
        body, name="attn_bwd", grid=(ATTN_HEADS // HPA, t // QB),
        in_specs=[q_spec] + k_specs + v_specs + [bias_spec,
                                                 pl.BlockSpec((None, QB, HPA * dh), lambda g, i: (0, i, g))],
        out_specs=[row_spec, acc_spec, acc_spec, bias_spec],
        out_shape=[_sds((t, d), BF16), _sds((tp, d), F32), _sds((tp, d), F32),
                   _sds((ATTN_HEADS, QB, KW), F32)],
        compiler_params=_cp(("parallel", "arbitrary")))(p, p, p, p, p, p, p, bias, do)


def _onehot_mm(name, a, b):
    def body(a_ref, b_ref, o_ref):
        o_ref[...] = _dot_hi(a_ref[...], b_ref[...])

    return pl.pallas_call(body, name=name, out_shape=_sds((a.shape[0], b.shape[1]), F32),
                          compiler_params=pltpu.CompilerParams(vmem_limit_bytes=VMEM_LIMIT))(a, b)


def _diag_index():
    ii, jj = np.arange(CHUNK)[:, None], np.arange(CHUNK)[None, :]
    return (ii - jj + CHUNK - 1).reshape(-1)


def _bias_expand(rel_bias):
    h = rel_bias.shape[0]
    nq, nk, shift = QB // CHUNK, KW // CHUNK, (2 * QB) // CHUNK
    nbin, ndc = 3 * LANE, 4
    rb = jnp.pad(rel_bias, ((0, 0), (0, nbin - rel_bias.shape[1])))
    win = np.clip(CHUNK * np.arange(ndc)[:, None] + np.arange(LANE)[None, :] - (CHUNK - 1), -REL_CLIP, REL_CLIP)
    sel = (jnp.arange(nbin)[:, None] == jnp.asarray((win + REL_CLIP).reshape(1, -1))).astype(F32)
    windows = _onehot_mm("bias_windows", rb, sel)
    diag_t = (jnp.arange(LANE)[:, None] == jnp.asarray(_diag_index().reshape(1, -1))).astype(F32)
    blocks = _onehot_mm("bias_blocks", windows.reshape(h * ndc, LANE), diag_t).reshape(h, ndc, CHUNK, CHUNK)
    off_band = jnp.full((h, CHUNK, CHUNK), NEG_INF, F32)
    rows = []
    for ic in range(nq):
        dcs = [ic - jc + shift for jc in range(nk)]
        rows.append(jnp.concatenate([blocks[:, min(dc, ndc - 1)] if 0 <= dc <= ATTN_LEFT else off_band
                                     for dc in dcs], axis=2))
    return jnp.concatenate(rows, axis=1)


def _bias_reduce(dbias):
    h = dbias.shape[0]
    nq, nk = QB // CHUNK, KW // CHUNK
    nbin = 3 * LANE
    blocks = dbias.reshape(h, nq, CHUNK, nk, CHUNK).transpose(0, 1, 3, 2, 4).reshape(h * nq * nk, CHUNK * CHUNK)
    diag = (jnp.asarray(_diag_index().reshape(-1, 1)) == jnp.arange(LANE)[None, :]).astype(F32)
    ic = np.arange(nq)[:, None, None]
    jc = np.arange(nk)[None, :, None]
    dl = np.arange(LANE)[None, None, :] - (CHUNK - 1)
    rel = np.clip(CHUNK * (ic - jc + (2 * QB) // CHUNK) + dl, -REL_CLIP, REL_CLIP) + REL_CLIP
    bins = (jnp.asarray(rel.reshape(-1, 1)) == jnp.arange(nbin)[None, :]).astype(F32)

    diags = _onehot_mm("bias_diag_sums", blocks, diag)
    out = _onehot_mm("bias_bin_sums", diags.reshape(h, nq * nk * LANE), bins)
    return out[:, :2 * REL_CLIP + 1]


def _chunk_masks():
    r = lax.broadcasted_iota(jnp.int32, (LB, LB), 0)
    c = lax.broadcasted_iota(jnp.int32, (LB, LB), 1)
    return (r // CHUNK) == (c // CHUNK), r >= c, r <= c


def _chunks(a):
    return [a[c * CHUNK:(c + 1) * CHUNK] for c in range(LB // CHUNK)]


def _per_chunk(a, f):
    return jnp.concatenate([jnp.broadcast_to(f(c), c.shape) for c in _chunks(a)], axis=0)


def _dot_sel(sel, x):
    def top(v):
        return lax.bitcast_convert_type(lax.bitcast_convert_type(v, jnp.int32) & jnp.int32(-65536), F32)

    hi = top(x)
    mid = top(x - hi)
    lo = (x - hi) - mid
    d = functools.partial(jnp.dot, sel.astype(jnp.bfloat16), preferred_element_type=F32)
    return d(hi.astype(jnp.bfloat16)) + d(mid.astype(jnp.bfloat16)) + d(lo.astype(jnp.bfloat16))


def _lin_block(gla, q, k, v, aux):
    dk = q.shape[-1]
    same, low, up = _chunk_masks()
    ones = same.astype(F32)
    if gla:
        glr, wlr, blr = aux
        q = q * (dk ** -0.5)
        pre = _dot(glr, wlr) + blr
        log_a = (jnp.minimum(pre, 0.0) - jnp.log(1.0 + jnp.exp(-jnp.abs(pre)))) / GATE_NORM
        b = _dot_sel(jnp.where(low, ones, 0.0), log_a)
        lastb = _per_chunk(b, lambda c: c[CHUNK - 1:])
    else:
        cs, sn, lg = aux
        pre = None
        half = dk // 2
        q = q * cs + pltpu.roll(q, half, 1) * sn
        k = (k * cs + pltpu.roll(k, half, 1) * sn) * (dk ** -0.5)
        pos = (lax.broadcasted_iota(jnp.int32, (LB, dk), 0) % CHUNK).astype(F32) + 1.0
        b = pos * lg
        lastb = jnp.broadcast_to(float(CHUNK) * lg, b.shape)
    eb, enb, el, dec = jnp.exp(b), jnp.exp(-b), jnp.exp(lastb - b), jnp.exp(lastb)
    qf, kf, qb, kb, kl = q * eb, k * enb, q * enb, k * eb, k * el
    s = jnp.where(same, jnp.where(low, _dot_nt(qf, kf), _dot_nt(qb, kb)), 0.0)
    return dict(pre=pre, eb=eb, enb=enb, el=el, dec=dec, qf=qf, kf=kf, qb=qb, kb=kb, kl=kl, s=s,
                same=same, low=low, up=up, ones=ones)


def _lin_norm_gate(gla, o, gate, gn):
    sg = _sigmoid(gate)
    silu = gate * sg
    if gla:
        r = lax.rsqrt(jnp.mean(o * o, axis=-1, keepdims=True) + LN_EPS)
        hn = o * r
        return silu * (hn * gn), (sg, silu, r, hn)
    mu = jnp.mean(o, axis=-1, keepdims=True)
    oc = o - mu
    r = lax.rsqrt(jnp.mean(oc * oc, axis=-1, keepdims=True) + LN_EPS)
    hn = oc * r
    return silu * hn, (sg, silu, r, hn)


HPS = 2


def _lin_specs(gla, dk, dv, off, rev, nb):
    pre = "g" if gla else "r"
    wk, wv = HPS * dk, HPS * dv
    qc, kc, vc, gc = (off[pre + "q"] // wk, off[pre + "k"] // wk, off[pre + "v"] // wv, off[pre + "g"] // wv)

    def blk(i):
        return nb - 1 - i if rev else i

    specs = [pl.BlockSpec((LB, wk), lambda g, i: (blk(i), qc + g)),
             pl.BlockSpec((LB, wk), lambda g, i: (blk(i), kc + g)),
             pl.BlockSpec((LB, wv), lambda g, i: (blk(i), vc + g)),
             pl.BlockSpec((LB, wv), lambda g, i: (blk(i), gc + g))]
    if gla:
        specs += [pl.BlockSpec((LB, LANE), lambda g, i: (blk(i), off["glr"] // LANE)),
                  pl.BlockSpec((LANE, wk), lambda g, i: (0, g)),
                  pl.BlockSpec((1, wk), lambda g, i: (0, g)),
                  pl.BlockSpec((1, dv), lambda g, i: (0, 0))]
    else:
        specs += [pl.BlockSpec((LB, dk), lambda g, i: (blk(i), 0)),
                  pl.BlockSpec((LB, dk), lambda g, i: (blk(i), 0)),
                  pl.BlockSpec((HPS, 1, dk), lambda g, i: (g, 0, 0))]
    return specs, blk


def _lin_aux(gla, refs, rows, hh, dk):
    if gla:
        glr_ref, wlr_ref, blr_ref, gn_ref = refs
        kcols = pl.ds(hh * dk, dk)
        return (glr_ref[rows, :], wlr_ref[:, kcols], blr_ref[:, kcols]), gn_ref[...]
    cs_ref, sn_ref, lg_ref = refs
    return (cs_ref[rows, :], sn_ref[rows, :], lg_ref[hh]), None


def _lin_fwd(gla, p, aux_arrays, d, off, branches, slot):
    t = p.shape[0]
    dk, dv = d // (2 * LIN_HEADS), d // LIN_HEADS
    nb, cb = t // LB, LB // CHUNK
    naux = len(aux_arrays)

    def body(*refs):
        q_ref, k_ref, v_ref, g_ref = refs[:4]
        aux_refs = refs[4:4 + naux]
        o_ref, bo_ref, st_out_ref, st_ref = refs[5 + naux:]

        @pl.when(pl.program_id(1) == 0)
        def _():
            st_ref[...] = jnp.zeros_like(st_ref)

        rows = slice(None)
        for hh in range(HPS):
            kcols, vcols = pl.ds(hh * dk, dk), pl.ds(hh * dv, dv)
            aux, gn = _lin_aux(gla, aux_refs, rows, hh, dk)
            v = v_ref[:, vcols]
            blk = _lin_block(gla, q_ref[:, kcols], k_ref[:, kcols], v, aux)
            st = st_ref[hh]
            inter = []
            for c, (qf, kl, dec, vc) in enumerate(zip(_chunks(blk["qf"]), _chunks(blk["kl"]), _chunks(blk["dec"]),
                                                      _chunks(v))):
                st_out_ref[hh, c] = st
                inter.append(_dot_nt(qf, st))
                st = st * dec[:1] + _dot_tn(vc, kl)
            st_ref[hh] = st
            o = _dot(blk["s"], v) + jnp.concatenate(inter, axis=0)
            o_ref[:, vcols] = o
            out, _ = _lin_norm_gate(gla, o, g_ref[:, vcols], gn)
            bo_ref[:, vcols] = out.astype(BF16)

    specs, _ = _lin_specs(gla, dk, dv, off, False, nb)
    orow = pl.BlockSpec((LB, HPS * dv), lambda g, i: (i, g))
    return pl.pallas_call(
        body, name="gla_fwd" if gla else "ret_fwd", grid=(LIN_HEADS // HPS, nb), in_specs=specs + [ANY],
        out_specs=[orow, pl.BlockSpec((None, LB, HPS * dv), lambda g, i: (slot, i, g)),
                   pl.BlockSpec((HPS, cb, dv, dk), lambda g, i: (g, i, 0, 0))],
        out_shape=[_sds((t, d), F32), _sds(branches.shape, BF16), _sds((LIN_HEADS, t // CHUNK, dv, dk), F32)],
        scratch_shapes=[pltpu.VMEM((HPS, dv, dk), F32)], input_output_aliases={4 + naux: 1},
        compiler_params=_cp(("parallel", "arbitrary")))(p, p, p, p, *aux_arrays, branches)


def _lin_bwd(gla, p, aux_arrays, o, states, dbo, slot, d, off):
    t = p.shape[0]
    dk, dv = d // (2 * LIN_HEADS), d // LIN_HEADS
    nb, cb = t // LB, LB // CHUNK
    naux = len(aux_arrays)

    def body(*refs):
        q_ref, k_ref, v_ref, g_ref = refs[:4]
        aux_refs = refs[4:4 + naux]
        o_ref, st_in_ref, dbo_ref = refs[4 + naux:7 + naux]
        outs = refs[7 + naux:]
        dq_ref, dk_ref, dv_ref, dg_ref = outs[:4]
        dst_ref = outs[-1]
        first = pl.program_id(1) == 0

        @pl.when(first)
        def _():
            dst_ref[...] = jnp.zeros_like(dst_ref)

        if gla:
            dpre_ref, dblr_ref, dgn_ref = outs[4:7]

            @pl.when(first)
            def _():
                dblr_ref[...] = jnp.zeros_like(dblr_ref)
                dgn_ref[...] = jnp.zeros_like(dgn_ref)

        rows = slice(None)
        for hh in range(HPS):
            kcols, vcols = pl.ds(hh * dk, dk), pl.ds(hh * dv, dv)
            aux, gn = _lin_aux(gla, aux_refs, rows, hh, dk)
            v = v_ref[:, vcols]
            bk = _lin_block(gla, q_ref[:, kcols], k_ref[:, kcols], v, aux)
            eb, enb, el, dec = bk["eb"], bk["enb"], bk["el"], bk["dec"]
            qf, kf, qb, kb, kl, s = bk["qf"], bk["kf"], bk["qb"], bk["kb"], bk["kl"], bk["s"]
            gate = g_ref[:, vcols]
            dout = dbo_ref[:, vcols]
            _, (sg, silu, r, hn) = _lin_norm_gate(gla, o_ref[:, vcols], gate, gn)
            dsilu = sg * (1.0 + gate * (1.0 - sg))
            if gla:
                y = hn * gn
                dy = dout * silu
                dg_ref[:, vcols] = (dout * y * dsilu).astype(BF16)
                dgn_ref[hh] += jnp.sum(dy * hn, axis=0, keepdims=True)
                dhn = dy * gn
                do = r * (dhn - hn * jnp.mean(dhn * hn, axis=-1, keepdims=True))
            else:
                dhn = dout * silu
                dg_ref[:, vcols] = (dout * hn * dsilu).astype(BF16)
                do = r * (dhn - jnp.mean(dhn, axis=-1, keepdims=True)
                          - hn * jnp.mean(dhn * hn, axis=-1, keepdims=True))
            ds = jnp.where(bk["same"], _dot_nt(do, v), 0.0)
            dsf = jnp.where(bk["low"], ds, 0.0)
            dsb = ds - dsf
            dvv = _dot_tn(s, do)
            dqf = _dot(dsf, kf)
            dkf = _dot_tn(dsf, qf)
            dqb = _dot(dsb, kb)
            dkb = _dot_tn(dsb, qb)
            dst = dst_ref[hh]
            dv_st, dqf_st, dkl_c, ddec_c = [], [], [], []
            parts = zip(reversed(range(cb)), reversed(_chunks(do)), reversed(_chunks(v)), reversed(_chunks(qf)),
                        reversed(_chunks(kl)), reversed(_chunks(dec)))
            for c, do_c, v_c, qf_c, kl_c, dec_c in parts:
                st = st_in_ref[hh, c]
                dv_st.append(_dot_nt(kl_c, dst))
                dkl_c.append(_dot(v_c, dst))
                dqf_st.append(_dot(do_c, st))
                ddec_c.append(jnp.broadcast_to(jnp.sum(dst * st, axis=0, keepdims=True), (CHUNK, dk)))
                dst = dst * dec_c[:1] + _dot_tn(do_c, qf_c)
            dst_ref[hh] = dst

            def cat(pieces):
                return jnp.concatenate(pieces[::-1], axis=0)

            dvv = dvv + cat(dv_st)
            dqf = dqf + cat(dqf_st)
            dkl = cat(dkl_c)
            dq = dqf * eb + dqb * enb
            dkk = dkf * enb + dkb * eb + dkl * el
            dv_ref[:, vcols] = dvv.astype(BF16)
            if gla:
                db = dqf * qf - dkf * kf - dqb * qb + dkb * kb - dkl * kl
                dlast = _per_chunk(dkl * kl, lambda c: jnp.sum(c, axis=0, keepdims=True)) + cat(ddec_c) * dec
                dlog_a = _dot_sel(jnp.where(bk["up"], bk["ones"], 0.0), db) + dlast
                dpre = dlog_a * (1.0 / GATE_NORM) * (1.0 - _sigmoid(bk["pre"]))
                dpre_ref[:, kcols] = dpre
                dblr_ref[hh] += jnp.sum(dpre, axis=0, keepdims=True)
                dq_ref[:, kcols] = (dq * (dk ** -0.5)).astype(BF16)
                dk_ref[:, kcols] = dkk.astype(BF16)
            else:
                cs, sn, _ = aux
                half = dk // 2
                dkk = dkk * (dk ** -0.5)
                dq_ref[:, kcols] = (dq * cs + pltpu.roll(dq * sn, half, 1)).astype(BF16)
                dk_ref[:, kcols] = (dkk * cs + pltpu.roll(dkk * sn, half, 1)).astype(BF16)

    specs, blk = _lin_specs(gla, dk, dv, off, True, nb)
    vrow = pl.BlockSpec((LB, HPS * dv), lambda g, i: (blk(i), g))
    krow = pl.BlockSpec((LB, HPS * dk), lambda g, i: (blk(i), g))
    specs += [vrow, pl.BlockSpec((HPS, cb, dv, dk), lambda g, i: (g, blk(i), 0, 0)),
              pl.BlockSpec((None, LB, HPS * dv), lambda g, i: (slot, blk(i), g))]
    out_specs = [krow, krow, vrow, vrow]
    out_shape = [_sds((t, d // 2), BF16), _sds((t, d // 2), BF16), _sds((t, d), BF16), _sds((t, d), BF16)]
    if gla:
        out_specs += [krow, pl.BlockSpec((HPS, 1, dk), lambda g, i: (g, 0, 0)),
                      pl.BlockSpec((HPS, 1, dv), lambda g, i: (g, 0, 0))]
        out_shape += [_sds((t, d // 2), F32), _sds((LIN_HEADS, 1, dk), F32), _sds((LIN_HEADS, 1, dv), F32)]
    out_specs.append(pl.BlockSpec((HPS, dv, dk), lambda g, i: (g, 0, 0)))
    out_shape.append(_sds((LIN_HEADS, dv, dk), F32))
    res = pl.pallas_call(
        body, name="gla_bwd" if gla else "ret_bwd", grid=(LIN_HEADS // HPS, nb), in_specs=specs,
        out_specs=out_specs, out_shape=out_shape,
        compiler_params=_cp(("parallel", "arbitrary")))(p, p, p, p, *aux_arrays, o, states, dbo)
    return res[:-1]


def _row_tile(rows, cols):
    cap = max(8, (2 << 20) // (4 * cols))
    t = rows
    while t > cap and t % 2 == 0:
        t //= 2
    return t


def _add_half(name, g, t, sel):
    nchip, hr, cols = t.shape
    tr = _row_tile(hr, cols)
    nb = hr // tr

    def body(sel_ref, g_ref, t_ref, o_ref):
        o_ref[...] = g_ref[...] + t_ref[...]

    half = pl.BlockSpec((None, tr, cols), lambda p, i, s: (p, i, 0))
    gs = pltpu.PrefetchScalarGridSpec(
        num_scalar_prefetch=1, grid=(nchip, nb),
        in_specs=[pl.BlockSpec((None, tr, cols), lambda p, i, s: (p, s[0] * nb + i, 0)), half], out_specs=half)
    return pl.pallas_call(body, name=name, grid_spec=gs, out_shape=_sds(t.shape, F32),
                          compiler_params=_cp(("parallel", "parallel")))(sel, g, t)


def _sum_shards(name, h, rcv, sel):
    _, rows, cols = h.shape
    tr = _row_tile(rows, cols)

    def body(sel_ref, h_ref, r0, r1, r2, o_ref):
        o_ref[...] = ((h_ref[...] + r0[...]) + r1[...]) + r2[...]

    rspecs = [pl.BlockSpec((None, tr, cols), functools.partial(lambda i, s, j: (j, i, 0), j=j)) for j in range(3)]
    gs = pltpu.PrefetchScalarGridSpec(
        num_scalar_prefetch=1, grid=(rows // tr,),
        in_specs=[pl.BlockSpec((None, tr, cols), lambda i, s: (s[0], i, 0))] + rspecs,
        out_specs=pl.BlockSpec((tr, cols), lambda i, s: (i, 0)))
    return pl.pallas_call(body, name=name, grid_spec=gs, out_shape=_sds((rows, cols), F32),
                          compiler_params=_cp(("parallel",)))(sel, h, rcv, rcv, rcv)


def _adamw_math(w, g, m, v):
    c1 = 1.0 - ADAM_B1 ** ADAM_STEP
    c2 = 1.0 - ADAM_B2 ** ADAM_STEP
    nm = ADAM_B1 * m + (1.0 - ADAM_B1) * g
    nv = ADAM_B2 * v + (1.0 - ADAM_B2) * jnp.square(g)
    return -ADAM_LR * ((nm / c1) / (jnp.sqrt(nv / c2) + ADAM_EPS) + ADAM_WD * w), nm, nv


def _adamw(name, w, g, m, v):
    rows, cols = w.shape
    tr = _row_tile(rows, cols)

    def body(w_ref, g_ref, m_ref, v_ref, d_ref, nm_ref, nv_ref):
        d_ref[...], nm_ref[...], nv_ref[...] = _adamw_math(w_ref[...], g_ref[...], m_ref[...], v_ref[...])

    spec = pl.BlockSpec((tr, cols), lambda i: (i, 0))
    return pl.pallas_call(body, name=name, grid=(rows // tr,), in_specs=[spec] * 4, out_specs=[spec] * 3,
                          out_shape=[_sds((rows, cols), F32)] * 3, compiler_params=_cp(("parallel",)))(w, g, m, v)


def _adamw_layer(name, w, g_own, g_sib, sel, m, v, layer, prev):
    depth, rows, cols = w.shape
    tr = _row_tile(rows // 2, cols)
    nbh = rows // 2 // tr
    nprev = 0 if prev is None else 4

    def body(sel_ref, w_ref, own_ref, sib_ref, m_ref, v_ref, *rest):
        go_ref, d_ref, nm_ref, nv_ref = rest[nprev:]
        gv = jnp.where(pl.program_id(0) // nbh == sel_ref[0], own_ref[...], sib_ref[...])
        go_ref[...] = gv
        d_ref[...], nm_ref[...], nv_ref[...] = _adamw_math(w_ref[...], gv, m_ref[...], v_ref[...])

    lay = pl.BlockSpec((None, tr, cols), lambda i, s: (layer, i, 0))
    hlf = pl.BlockSpec((tr, cols), lambda i, s: (i % nbh, 0))
    gs = pltpu.PrefetchScalarGridSpec(
        num_scalar_prefetch=1, grid=(2 * nbh,), in_specs=[lay, hlf, hlf, lay, lay] + [ANY] * nprev,
        out_specs=[lay] * 4)
    args = (sel, w, g_own, g_sib, m, v) + (() if prev is None else tuple(prev))
    return pl.pallas_call(
        body, name=name, grid_spec=gs, out_shape=[_sds((depth, rows, cols), F32)] * 4,
        input_output_aliases={6 + k: k for k in range(nprev)},
        compiler_params=_cp(("parallel",)))(*args)


def _adamw_colmajor(name, wt, mt, vt, halves, sel):
    c_dim, depth, r_dim = wt.shape
    hr = r_dim // 2

    def body(sel_ref, w_ref, m_ref, v_ref, *rest):
        g_refs, (go_ref, d_ref, nm_ref, nv_ref) = rest[:2 * depth], rest[2 * depth:]
        own_first = sel_ref[0] == 0
        for l in range(depth):
            own, sib = g_refs[2 * l][...], g_refs[2 * l + 1][...]
            g = jnp.concatenate([jnp.where(own_first, own, sib), jnp.where(own_first, sib, own)], axis=0).T
            go_ref[:, l, :] = g
            d_ref[:, l, :], nm_ref[:, l, :], nv_ref[:, l, :] = _adamw_math(w_ref[:, l, :], g, m_ref[:, l, :],
                                                                          v_ref[:, l, :])

    col = pl.BlockSpec((LANE, depth, r_dim), lambda j, s: (j, 0, 0))
    gs = pltpu.PrefetchScalarGridSpec(
        num_scalar_prefetch=1, grid=(c_dim // LANE,),
        in_specs=[col] * 3 + [pl.BlockSpec((hr, LANE), lambda j, s: (0, j))] * (2 * depth), out_specs=[col] * 4)
    flat = [h for pair in halves for h in pair]
    return pl.pallas_call(body, name=name, grid_spec=gs, out_shape=[_sds(wt.shape, F32)] * 4,
                          compiler_params=_cp(("parallel",)))(sel, wt, mt, vt, *flat)


def _adamw_tail(name, wt, mt, vt, gt_tail, prev):
    c_dim, depth, r_dim = wt.shape
    nt = gt_tail.shape[0]

    def body(w_ref, m_ref, v_ref, g_ref, *rest):
        go_ref, d_ref, nm_ref, nv_ref = rest[4:]
        g = g_ref[...]
        go_ref[...] = g
        d_ref[...], nm_ref[...], nv_ref[...] = _adamw_math(w_ref[...], g, m_ref[...], v_ref[...])

    tail = pl.BlockSpec((nt, depth, r_dim), lambda i: (c_dim // nt - 1, 0, 0))
    return pl.pallas_call(
        body, name=name, grid=(1,), in_specs=[tail] * 3 + [pl.BlockSpec((nt, depth, r_dim), lambda i: (0, 0, 0))]
        + [ANY] * 4, out_specs=[tail] * 4, out_shape=[_sds(wt.shape, F32)] * 4,
        input_output_aliases={4 + k: k for k in range(4)},
        compiler_params=_cp(("arbitrary",)))(wt, mt, vt, gt_tail, *prev)


def _place():
    x, y, c = (lax.axis_index(a) for a in MESH_AXES)
    chips = [(1 - x, y), (x, 1 - y), (1 - x, 1 - y)]
    return x, y, c, chips


def _chip_index(xy):
    return 2 * xy[0] + xy[1]


ANY = pl.BlockSpec(memory_space=pl.ANY)


HBM_SPEC = pl.BlockSpec(memory_space=pltpu.HBM)
SEM = pl.BlockSpec(memory_space=pltpu.SEMAPHORE)
EFFECT = pltpu.SideEffectType.DATAFLOW_SIDE_EFFECTING


def _half(ref, c):
    hr = ref.shape[-2] // 2
    return pl.ds(pl.multiple_of(c * hr, 16), hr)


def _gather_copies(srcs, lands, send, recv):
    x, y, c, chips = _place()
    me = _chip_index((x, y))
    return [pltpu.make_async_remote_copy(src_ref=s.at[_half(s, c)], dst_ref=g.at[me, _half(s, c)],
                                         send_sem=send.at[3 * a + j], recv_sem=recv.at[3 * a + j],
                                         device_id=(*ch, c), device_id_type=DEV)
            for a, (s, g) in enumerate(zip(srcs, lands)) for j, ch in enumerate(chips)]


def _scatter_copies(srcs, lands, send, recv):
    x, y, c, chips = _place()
    return [pltpu.make_async_remote_copy(src_ref=h.at[_chip_index(ch)], dst_ref=r.at[j],
                                         send_sem=send.at[3 * a + j], recv_sem=recv.at[3 * a + j],
                                         device_id=(*ch, c), device_id_type=DEV)
            for a, (h, r) in enumerate(zip(srcs, lands)) for j, ch in enumerate(chips)]


def _in_hbm(a):
    return pltpu.with_memory_space_constraint(a, pltpu.HBM)


def _split_start(name, srcs, land_shapes, copies_fn, after=None):
    ns, nl = len(srcs), len(land_shapes)
    ncp = 3 * ns
    lands = [lax.empty(s.shape, s.dtype) for s in land_shapes]
    behind = [] if after is None else [after]

    def body(*refs):
        src, land = refs[:ns], refs[ns:ns + nl]
        send, recv = refs[ns + nl + len(behind)], refs[ns + nl + len(behind) + 1]
        for cp in copies_fn(src, land, send, recv):
            cp.start()
        refs[-1][...] = jnp.zeros_like(refs[-1])

    bufs = list(srcs) + lands
    outs = pl.pallas_call(
        body, name=name, in_specs=[HBM_SPEC] * (ns + nl) + [ANY] * len(behind),
        out_specs=[SEM, SEM] + [HBM_SPEC] * (ns + nl) + [pl.BlockSpec(memory_space=pltpu.VMEM)],
        out_shape=[pltpu.SemaphoreType.DMA((ncp,)), pltpu.SemaphoreType.DMA((ncp,))]
        + [pltpu.HBM(b.shape, b.dtype) for b in bufs] + [_sds((8, LANE), F32)],
        input_output_aliases={i: 2 + i for i in range(ns + nl)},
        compiler_params=pltpu.CompilerParams(has_side_effects=EFFECT))(*[_in_hbm(b) for b in bufs], *behind)
    return outs[0], outs[1], list(outs[2:2 + ns]), list(outs[2 + ns:2 + ns + nl]), outs[-1]


def _split_wait(name, started, copies_fn, after):
    send, recv, srcs, lands, _ = started
    ns, nl = len(srcs), len(lands)

    def body(*refs):
        src, land = refs[:ns], refs[ns:ns + nl]
        for cp in copies_fn(src, land, refs[ns + nl], refs[ns + nl + 1]):
            cp.wait_send()
            cp.wait_recv()

    bufs = list(srcs) + list(lands)
    outs = pl.pallas_call(
        body, name=name, in_specs=[HBM_SPEC] * (ns + nl) + [SEM, SEM, ANY], out_specs=[HBM_SPEC] * (ns + nl),
        out_shape=[pltpu.HBM(b.shape, b.dtype) for b in bufs],
        input_output_aliases={i: i for i in range(ns + nl)},
        compiler_params=pltpu.CompilerParams(has_side_effects=EFFECT))(*bufs, send, recv, after)
    return list(outs[:ns]), list(outs[ns:])


def _gather_plain(name, srcs):
    n = len(srcs)

    def body(*refs):
        src, land = refs[:n], refs[n:2 * n]
        send, recv, fsend, frecv = refs[2 * n:]
        first = _gather_copies(src, land, send, recv)
        for cp in first:
            cp.start()
        _forward_body(land, first, fsend, frecv)

    return pl.pallas_call(
        body, name=name, in_specs=[ANY] * n, out_specs=[ANY] * n,
        out_shape=[_sds((4,) + s.shape, s.dtype) for s in srcs],
        scratch_shapes=[pltpu.SemaphoreType.DMA((3 * n,))] * 4)(*srcs)


def _forward_body(land, arrivals, fsend, frecv):
    x, y, c, chips = _place()
    n = len(land)
    passed = []
    for a in range(n):
        for j, ch in enumerate(chips):
            if arrivals is not None:
                arrivals[3 * a + j].wait_recv()
            slot = land[a].at[_chip_index(ch), _half(land[a], c)]
            fw = pltpu.make_async_remote_copy(src_ref=slot, dst_ref=slot, send_sem=fsend.at[3 * a + j],
                                              recv_sem=frecv.at[3 * a + j], device_id=(x, y, 1 - c),
                                              device_id_type=DEV)
            fw.start()
            passed.append(fw)
    for a in range(n):
        for j, ch in enumerate(chips):
            slot = land[a].at[_chip_index(ch), _half(land[a], 1 - c)]
            pltpu.make_async_remote_copy(src_ref=slot, dst_ref=slot, send_sem=fsend.at[3 * a + j],
                                         recv_sem=frecv.at[3 * a + j], device_id=(x, y, c),
                                         device_id_type=DEV).wait_recv()
    for cp in passed:
        cp.wait_send()
    if arrivals is not None:
        for cp in arrivals:
            cp.wait_send()


def _gather_forward(name, lands):
    n = len(lands)

    def body(*refs):
        _forward_body(refs[n:2 * n], None, refs[2 * n], refs[2 * n + 1])

    return pl.pallas_call(
        body, name=name, in_specs=[ANY] * n, out_specs=[ANY] * n,
        out_shape=[_sds(g.shape, g.dtype) for g in lands], input_output_aliases={a: a for a in range(n)},
        scratch_shapes=[pltpu.SemaphoreType.DMA((3 * n,))] * 2)(*lands)


def _sibling_halves(name, grs):
    n = len(grs)

    def body(*refs):
        ins, outs = refs[:n], refs[n:2 * n]
        send, recv = refs[2 * n:]
        x, y, c, _ = _place()
        cps = [pltpu.make_async_remote_copy(src_ref=ins[a].at[:, _half(ins[a], 1 - c)], dst_ref=outs[a],
                                            send_sem=send.at[a], recv_sem=recv.at[a], device_id=(x, y, 1 - c),
                                            device_id_type=DEV) for a in range(n)]
        for cp in cps:
            cp.start()
        for cp in cps:
            cp.wait()

    return pl.pallas_call(
        body, name=name, in_specs=[ANY] * n, out_specs=[ANY] * n,
        out_shape=[_sds((g.shape[0], g.shape[1] // 2, g.shape[2]), F32) for g in grs],
        scratch_shapes=[pltpu.SemaphoreType.DMA((n,)), pltpu.SemaphoreType.DMA((n,))])(*grs)


def _sibling_share(name, sms):
    n = len(sms)

    def body(*refs):
        ins, outs = refs[:n], refs[n:2 * n]
        send, recv = refs[2 * n:]
        x, y, c, _ = _place()
        cps = [pltpu.make_async_remote_copy(src_ref=ins[a], dst_ref=outs[a], send_sem=send.at[a],
                                            recv_sem=recv.at[a], device_id=(x, y, 1 - c), device_id_type=DEV)
               for a in range(n)]
        for cp in cps:
            cp.start()
        for cp in cps:
            cp.wait()

    return pl.pallas_call(
        body, name=name, in_specs=[ANY] * n, out_specs=[ANY] * n, out_shape=[_sds(s.shape, F32) for s in sms],
        scratch_shapes=[pltpu.SemaphoreType.DMA((n,))] * 2)(*sms)


def _small_allreduce(v, after=None):
    rows = v.shape[0]
    ndev = 8
    behind = [] if after is None else [after]

    def body(v_ref, *rest):
        o_ref, gat_ref, send, recv = rest[len(behind):]
        x, y, c, _ = _place()
        me = 4 * x + 2 * y + c
        cps = []
        for k in range(1, ndev):
            to = (me + k) % ndev
            cp = pltpu.make_async_remote_copy(src_ref=v_ref, dst_ref=gat_ref.at[me], send_sem=send.at[k - 1],
                                              recv_sem=recv.at[me], device_id=(to // 4, (to // 2) % 2, to % 2),
                                              device_id_type=DEV)
            cp.start()
            cps.append(cp)
        gat_ref[me] = v_ref[...]
        for k in range(1, ndev):
            frm = (me + k) % ndev
            pltpu.make_async_remote_copy(src_ref=v_ref, dst_ref=gat_ref.at[frm], send_sem=send.at[k - 1],
                                         recv_sem=recv.at[frm], device_id=(x, y, c), device_id_type=DEV).wait_recv()
        for cp in cps:
            cp.wait_send()
        acc = gat_ref[0]
        for k in range(1, ndev):
            acc = acc + gat_ref[k]
        o_ref[...] = acc

    vm = pl.BlockSpec(memory_space=pltpu.VMEM)
    return pl.pallas_call(
        body, name="small_allreduce", in_specs=[vm] + [ANY] * len(behind), out_specs=vm,
        out_shape=_sds((rows, LANE), F32),
        scratch_shapes=[pltpu.VMEM((ndev, rows, LANE), F32), pltpu.SemaphoreType.DMA((ndev - 1,)),
                        pltpu.SemaphoreType.DMA((ndev,))])(v, *behind)


def _layout(d):
    half = d // 2
    names = [("aq", d), ("ak", d), ("av", d), ("rq", half), ("rk", half), ("rv", d), ("rg", d),
             ("gq", half), ("gk", half), ("gv", d), ("gg", d), ("gates", 3 * d), ("glr", 2 * LANE)]
    off, pos = {}, 0
    for nm, sz in names:
        off[nm] = pos
        pos += sz
    return off, pos


def _unpad_cols(g, d):
    a = 8 * d + d
    return jnp.concatenate([g[..., :a], g[..., a + 3 * d:a + 3 * d + GATE_RANK], g[..., a:a + 3 * d]], axis=-1)


def kernel(x, ln_in_g, ln_in_b, w_in, rel_bias, gla_w_lr, gla_b_lr, gla_norm_g, w_branch, w_out, ln1_g, ln1_b, w_up, w_down, ln2_g, ln2_b, loss_target, m_ln_in_g, m_ln_in_b, m_w_in, m_rel_bias, m_gla_w_lr, m_gla_b_lr, m_gla_norm_g, m_w_branch, m_w_out, m_ln1_g, m_ln1_b, m_w_up, m_w_down, m_ln2_g, m_ln2_b, v_ln_in_g, v_ln_in_b, v_w_in, v_rel_bias, v_gla_w_lr, v_gla_b_lr, v_gla_norm_g, v_w_branch, v_w_out, v_ln1_g, v_ln1_b, v_w_up, v_w_down, v_ln2_g, v_ln2_b):
    t, d = x.shape[1], x.shape[2]
    dff = 4 * d
    half = d // 2
    off, npad = _layout(d)
    xi, yi, ci = (lax.axis_index(a) for a in MESH_AXES)
    chip = 2 * xi + yi
    csel = jnp.reshape(ci, (1,)).astype(jnp.int32)
    psel = jnp.reshape(chip, (1,)).astype(jnp.int32)

    big_w = [w_in, w_branch.reshape(DEPTH, -1, d), w_out, w_up, w_down]
    big_m = [m_w_in, m_w_branch.reshape(DEPTH, -1, d), m_w_out, m_w_up, m_w_down]
    big_v = [v_w_in, v_w_branch.reshape(DEPTH, -1, d), v_w_out, v_w_up, v_w_down]
    W_IN, REST = [0], [1, 2, 3, 4]

    def shards_of(l, idx):
        return [big_w[i][l].astype(BF16) for i in idx]

    def lands_of(srcs):
        return [_sds((4,) + s.shape, s.dtype) for s in srcs]

    def full_w_in(g):
        per = g.shape[2]
        a = 8 * d + d

        def run(lo, hi):
            cuts = [(max(lo, c * per), min(hi, (c + 1) * per), c) for c in range(4)]
            return [g[c, :, x - c * per:y - c * per] for x, y, c in cuts if x < y]

        zeros = jnp.zeros((d, 2 * LANE - GATE_RANK), g.dtype)
        return jnp.concatenate(run(0, a) + run(a + GATE_RANK, 4 * per) + run(a, a + GATE_RANK) + [zeros], axis=1)

    def full_rest(gs):
        g_br, g_out, g_up, g_down = gs
        return (jnp.transpose(g_br.reshape(4, N_BRANCH, d // 4, d), (1, 0, 2, 3)).reshape(N_BRANCH, d, d),
                g_out.reshape(d, d), jnp.transpose(g_up, (1, 0, 2)).reshape(d, dff), g_down.reshape(dff, d))

    def with_own(srcs, lands):
        return [lax.dynamic_update_slice(g, s[None], (chip, 0, 0)) for s, g in zip(srcs, lands)]

    def gather_start(tag, l, idx, after):
        srcs = shards_of(l, idx)
        return srcs, _split_start(f"gather_{tag}{l}_start", srcs, lands_of(srcs), _gather_copies, after)

    def gather_finish(tag, l, pending, after):
        srcs, started = pending
        _, lands = _split_wait(f"gather_{tag}{l}_wait", started, _gather_copies, after)
        return with_own(srcs, _gather_forward(f"gather_{tag}{l}_pass", lands))

    def token(pending):
        return pending[1][4][0, 0]

    win, wbr, wout, wup, wdown = ([None] * DEPTH for _ in range(5))
    src_first = shards_of(0, W_IN)
    g_first = with_own(src_first, _gather_plain("gather_in0", src_first))
    win[0] = full_w_in(g_first[0])

    dkh = half // LIN_HEADS
    lr_rows = DEPTH * GATE_RANK
    lr_slab = jnp.zeros((lr_rows, 4, half // 4), F32)
    lr_slab = lax.dynamic_update_slice(lr_slab, (gla_w_lr.reshape(lr_rows, 1, half // 4) * jnp.where(ci == 0, 1.0, 0.0)),
                                       (0, chip, 0))
    wlr_full = _small_allreduce(lr_slab.reshape(-1, LANE)).reshape(DEPTH, GATE_RANK, half)
    wlr_pad = jnp.concatenate([wlr_full, jnp.zeros((DEPTH, LANE - GATE_RANK, half), F32)], axis=1)
    pend_rest = gather_start("rest", 0, REST, wlr_full[0, :1, :1] + g_first[0][0, :1, :1].astype(F32))

    inv = 10000.0 ** (-jnp.arange(0, dkh, 2, dtype=F32) / dkh)
    ang = jnp.arange(t, dtype=F32)[:, None] * inv[None, :]
    cos, sin = jnp.cos(ang), jnp.sin(ang)
    rope_c = jnp.concatenate([cos, cos], axis=1)
    rope_s = jnp.concatenate([-sin, sin], axis=1)
    log_gamma = jnp.log1p(-jnp.exp2(-5.0 - jnp.arange(LIN_HEADS, dtype=F32)))
    lg_tab = jnp.broadcast_to(log_gamma[:, None, None], (LIN_HEADS, 1, dkh))

    def vec(a):
        return a.reshape(1, -1)

    x0, x0b, xh_in, rs_in = _ln_in(x[0], vec(ln_in_g) + token(pend_rest), vec(ln_in_b))
    saved = []
    xl, xlb = x0, x0b
    for l in range(DEPTH):
        p = _mm("proj_in", xlb, win[l], 512, 1792)
        g_rest = gather_finish("rest", l, pend_rest, p)
        wbr[l], wout[l], wup[l], wdown[l] = full_rest(g_rest)
        tok = 0.0
        if l + 1 < DEPTH:
            pend_in = gather_start("in", l + 1, W_IN, g_rest[0])
            tok = token(pend_in)
        bias = _bias_expand(rel_bias[l] + tok)
        bo = _attn_fwd(p, bias, d, off)
        ret_aux = (rope_c, rope_s, lg_tab + tok)
        gla_aux = (p, wlr_pad[l], vec(gla_b_lr[l]) + tok, vec(gla_norm_g[l]))
        o_ret, bo, st_ret = _lin_fwd(False, p, ret_aux, d, off, bo, 1)
        o_gla, bo, st_gla = _lin_fwd(True, p, gla_aux, d, off, bo, 2)
        tok = 0.0
        if l + 1 < DEPTH:
            g_in = gather_finish("in", l + 1, pend_in, bo)
            win[l + 1] = full_w_in(g_in[0])
            pend_rest = gather_start("rest", l + 1, REST, g_in[0])
            tok = token(pend_rest)
        proj, merged = _merge_fwd(bo, wbr[l], p, off["gates"])
        x1, x1b, xh1, rs1 = _mm_res_ln("out_proj_ln", merged, wout[l], xl, vec(ln1_g[l]) + tok, vec(ln1_b[l]),
                                       256, False)
        u = _mm("mlp_up", x1b, wup[l], 1024, 1024)
        x2, x2b, xh2, rs2, act = _mm_res_ln("mlp_down_ln", u, wdown[l], x1, vec(ln2_g[l]), vec(ln2_b[l]), 256, True)
        saved.append(dict(xlb=xlb, p=p, bias=bias, ret_aux=ret_aux, gla_aux=gla_aux, o_ret=o_ret, o_gla=o_gla,
                          st_ret=st_ret, st_gla=st_gla, bo=bo, proj=proj, merged=merged, x1b=x1b, xh1=xh1,
                          rs1=rs1, u=u, xh2=xh2, rs2=rs2, act=act))
        xl, xlb = x2, x2b

    small = {}
    last = saved[-1]
    loss_p, dz2, dz2b, dg, db = _loss_ln_bwd(xl, loss_target[0], last["xh2"], last["rs2"], vec(ln2_g[DEPTH - 1]))
    small["loss"] = loss_p[:, :1]
    grad_x = None

    def scatter_start(tag, l, idx, shards, after):
        theirs = _sibling_halves(f"grad_{tag}{l}_sibling", shards)
        hs = [_add_half("grad_sibling_add", g, th, csel) for g, th in zip(shards, theirs)]
        lands = [_sds((3,) + h.shape[1:], F32) for h in hs]
        return tag, l, idx, _split_start(f"grad_{tag}{l}_scatter_start", hs, lands, _scatter_copies, after)

    adam_out = [None] * len(big_w)
    w_in_halves = [None] * DEPTH

    def scatter_finish(pending, after):
        tag, l, idx, started = pending
        hs, rcv = _split_wait(f"grad_{tag}{l}_scatter_wait", started, _scatter_copies, after)
        sms = [_sum_shards("grad_chip_sum", h, r, psel) for h, r in zip(hs, rcv)]
        last = None
        for i, own, sib in zip(idx, sms, _sibling_share(f"grad_{tag}{l}_share", sms)):
            if i == W_IN[0]:
                w_in_halves[l] = (own, sib)
                last = sib
            else:
                adam_out[i] = _adamw_layer("adamw_large", big_w[i], own, sib, csel, big_m[i], big_v[i], l,
                                           adam_out[i])
                last = adam_out[i][0]
        return last

    in_flight = []

    def scatter(tag, l, idx, shards, after=None):
        pending = scatter_start(tag, l, idx, shards, after)
        in_flight.append(pending)
        if len(in_flight) > 3:
            scatter_finish(in_flight.pop(0), pending[3][4])
        return pending[3][4][0, 0]

    for l in reversed(range(DEPTH)):
        s = saved[l]
        small[("ln2_g", l)], small[("ln2_b", l)] = dg, db
        du = _mm_nt_relu2_bwd(dz2b, wdown[l], s["u"])
        g_wdown = _mm_tn("grad_w_down", s["act"], dz2b, 512, 512)
        g_wup = _mm_tn("grad_w_up", s["x1b"], du, 512, 512, shard="cols")
        dz1, dz1b, dg1, db1 = _mm_nt_res_lnbwd("mlp_up_bwd_ln", du, wup[l], dz2, s["xh1"], s["rs1"],
                                               vec(ln1_g[l]), 256, dff)
        small[("ln1_g", l)], small[("ln1_b", l)] = dg1, db1
        dproj, dgl = _merge_bwd(dz1b, wout[l], s["proj"], s["p"], off["gates"])
        g_wout = _mm_tn("grad_w_out", s["merged"], dz1b, 512, 512)
        dbo = _mm("branch_proj_bwd", dproj, wbr[l], 1024, 1024, nt=True)
        g_wbr = _mm_tn("grad_w_branch", s["bo"], dproj, d // 4, 1024, shard="rows")
        tok = scatter("rest", l, REST, [g_wbr, g_wout.reshape(4, d // 4, d), g_wup, g_wdown.reshape(4, d, d)])
        rc, rs_, lg = s["ret_aux"]
        gp, gw, gb, gn_ = s["gla_aux"]
        dq_a, dk_acc, dv_acc, dbias = _attn_bwd(s["p"], s["bias"] + tok, dbo, d, off)
        small[("rel_bias", l)] = _bias_reduce(dbias)
        dk_a = dk_acc[2 * QB:].astype(BF16)
        dv_a = dv_acc[2 * QB:].astype(BF16)
        dq_r, dk_r, dv_r, dg_r = _lin_bwd(False, s["p"], (rc, rs_, lg + tok), s["o_ret"], s["st_ret"], dbo, 1, d, off)
        dq_g, dk_g, dv_g, dg_g, dpre, dblr, dgn = _lin_bwd(True, s["p"], (gp, gw, gb + tok, gn_), s["o_gla"],
                                                           s["st_gla"], dbo, 2, d, off)
        small[("gla_b_lr", l)] = dblr.reshape(1, half)
        small[("gla_norm_g", l)] = jnp.sum(dgn, axis=0)
        dpre_b = dpre.astype(BF16)
        glr_b = s["p"][:, off["glr"]:off["glr"] + LANE].astype(BF16)
        dglr = _mm("gate_lr_bwd", dpre_b, wlr_pad[l], 512, LANE, nt=True, out_dtype=BF16)
        small[("gla_w_lr", l)] = _mm_tn("grad_gla_w_lr", glr_b, dpre_b, LANE, half)[:GATE_RANK]
        dp = jnp.concatenate([dq_a, dk_a, dv_a, dq_r, dk_r, dv_r, dg_r, dq_g, dk_g, dv_g, dg_g,
                              dgl[0], dgl[1], dgl[2], dglr, jnp.zeros((t, LANE), BF16)], axis=1)
        if l > 0:
            prev = saved[l - 1]
            xh_p, rs_p, g_p = prev["xh2"], prev["rs2"], vec(ln2_g[l - 1])
        else:
            xh_p, rs_p, g_p = xh_in, rs_in, vec(ln_in_g)
        g_win = _mm_tn("grad_w_in", s["xlb"], dp, 1024, 896)
        tok = scatter("in", l, W_IN, [jnp.transpose(_unpad_cols(g_win, d).reshape(d, 4, -1), (1, 0, 2))])
        dzp, dzpb, dg, db = _mm_nt_res_lnbwd("proj_in_bwd_ln", dp, win[l], dz1, xh_p, rs_p, g_p + tok, 1024, 1792)
        dz2, dz2b = dzp, dzpb
        grad_x = dzp
    after = grad_x
    while in_flight:
        after = scatter_finish(in_flight.pop(0), after)
    wt, mt, vt = (jnp.transpose(a, (2, 0, 1)) for a in (big_w[0], big_m[0], big_v[0]))
    ntail = wt.shape[0] % LANE
    tails = [jnp.where(ci == 0, jnp.concatenate([own[:, -ntail:], sib[:, -ntail:]]),
                       jnp.concatenate([sib[:, -ntail:], own[:, -ntail:]])).T for own, sib in w_in_halves]
    adam_t = _adamw_tail("adamw_w_in_tail", wt, mt, vt, jnp.stack(tails, axis=1),
                         _adamw_colmajor("adamw_w_in", wt, mt, vt, w_in_halves, csel))
    adam_out[0] = [jnp.transpose(r, (1, 2, 0)) for r in adam_t]
    small["ln_in_g"], small["ln_in_b"] = dg, db
    rb_pad = 3 * LANE
    pieces = [small["loss"].reshape(-1), jnp.zeros((LANE - 1,), F32), small["ln_in_g"].reshape(-1),
              small["ln_in_b"].reshape(-1)]
    for l in range(DEPTH):
        rb = jnp.pad(small[("rel_bias", l)], ((0, 0), (0, rb_pad - (2 * REL_CLIP + 1))))
        pieces += [rb.reshape(-1), small[("gla_w_lr", l)].reshape(-1), small[("gla_b_lr", l)].reshape(-1),
                   small[("gla_norm_g", l)].reshape(-1), small[("ln1_g", l)].reshape(-1),
                   small[("ln1_b", l)].reshape(-1), small[("ln2_g", l)].reshape(-1), small[("ln2_b", l)].reshape(-1)]
    sizes = [pc.shape[0] for pc in pieces]
    packed = jnp.concatenate(pieces)
    padn = (-packed.shape[0]) % (8 * LANE)
    packed = jnp.concatenate([packed, jnp.zeros((padn,), F32)]).reshape(-1, LANE)
    red = _small_allreduce(packed, after).reshape(-1)

    parts, pos = [], 0
    for sz in sizes:
        parts.append(red[pos:pos + sz])
        pos += sz
    loss = parts[0][0]
    g_ln_in_g, g_ln_in_b = parts[2], parts[3]
    per = 8
    g_rel = jnp.stack([parts[4 + per * l].reshape(ATTN_HEADS, rb_pad)[:, :2 * REL_CLIP + 1] for l in range(DEPTH)])
    g_wlr_full = jnp.stack([parts[5 + per * l].reshape(GATE_RANK, half) for l in range(DEPTH)])
    g_wlr = lax.dynamic_slice_in_dim(g_wlr_full, chip * (half // 4), half // 4, axis=2)
    g_blr = jnp.stack([parts[6 + per * l] for l in range(DEPTH)])
    g_gn = jnp.stack([parts[7 + per * l] for l in range(DEPTH)])
    g_ln1g = jnp.stack([parts[8 + per * l] for l in range(DEPTH)])
    g_ln1b = jnp.stack([parts[9 + per * l] for l in range(DEPTH)])
    g_ln2g = jnp.stack([parts[10 + per * l] for l in range(DEPTH)])
    g_ln2b = jnp.stack([parts[11 + per * l] for l in range(DEPTH)])

    grads = [g_ln_in_g, g_ln_in_b, None, g_rel, g_wlr, g_blr, g_gn, None, None, g_ln1g, g_ln1b, None, None,
             g_ln2g, g_ln2b]
    ws = [ln_in_g, ln_in_b, w_in, rel_bias, gla_w_lr, gla_b_lr, gla_norm_g, w_branch, w_out, ln1_g, ln1_b,
          w_up, w_down, ln2_g, ln2_b]
    ms = [m_ln_in_g, m_ln_in_b, m_w_in, m_rel_bias, m_gla_w_lr, m_gla_b_lr, m_gla_norm_g, m_w_branch, m_w_out,
          m_ln1_g, m_ln1_b, m_w_up, m_w_down, m_ln2_g, m_ln2_b]
    vs = [v_ln_in_g, v_ln_in_b, v_w_in, v_rel_bias, v_gla_w_lr, v_gla_b_lr, v_gla_norm_g, v_w_branch, v_w_out,
          v_ln1_g, v_ln1_b, v_w_up, v_w_down, v_ln2_g, v_ln2_b]

    deltas, new_ms, new_vs = [None] * 15, [None] * 15, [None] * 15
    big_idx = [2, 7, 8, 11, 12]
    for i, res in zip(big_idx, adam_out):
        shp = ws[i].shape
        grads[i], deltas[i], new_ms[i], new_vs[i] = (r.reshape(shp) for r in res)
    small_idx = [i for i in range(15) if i not in big_idx]

    def pack(arrs):
        flat_ = jnp.concatenate([arrs[i].reshape(-1) for i in small_idx])
        pad_ = (-flat_.shape[0]) % (8 * LANE)
        return jnp.concatenate([flat_, jnp.ones((pad_,), F32)]).reshape(-1, LANE)

    dl, nm, nv = _adamw("adamw_small", pack(ws), pack(grads), pack(ms), pack(vs))
    pos = 0
    for i in small_idx:
        sz = int(np.prod(ws[i].shape))
        deltas[i] = dl.reshape(-1)[pos:pos + sz].reshape(ws[i].shape)
        new_ms[i] = nm.reshape(-1)[pos:pos + sz].reshape(ws[i].shape)
        new_vs[i] = nv.reshape(-1)[pos:pos + sz].reshape(ws[i].shape)
        pos += sz

    return (loss, grad_x[None], *grads, *deltas, *new_ms, *new_vs)
```

```python
import functools

import numpy as np
import jax
import jax.numpy as jnp
from jax import lax
from jax.experimental import pallas as pl
from jax.experimental.pallas import tpu as pltpu

F32 = jnp.float32
BF16 = jnp.bfloat16
MXU_DTYPE = BF16
HI = lax.Precision.HIGHEST

DEPTH = 2
CHUNK = 64
N_BRANCH = 3
ATTN_HEADS = 8
ATTN_LEFT = 8
REL_CLIP = 2 * CHUNK
LIN_HEADS = 4
GATE_RANK = 16
GATE_NORM = 16.0
LN_EPS = 1e-5
NEG_INF = -1e30
ALPHA = (2 * DEPTH) ** 0.25
ADAM_LR, ADAM_B1, ADAM_B2, ADAM_EPS, ADAM_WD, ADAM_STEP = 0.001, 0.9, 0.999, 1e-08, 0.01, 10

LANE = 128
VMEM_LIMIT = 56 << 20
QB = 256
KW = 3 * QB
LB = 256
MESH_AXES = ("x", "y", "c")
DEV = pl.DeviceIdType.MESH


def _cp(sem):
    return pltpu.CompilerParams(dimension_semantics=sem, vmem_limit_bytes=VMEM_LIMIT)


def _mx(v):
    return v.astype(MXU_DTYPE)


def _dot(a, b):
    return jnp.dot(_mx(a), _mx(b), preferred_element_type=F32)


def _dot_nt(a, b):
    return lax.dot_general(_mx(a), _mx(b), (((1,), (1,)), ((), ())), preferred_element_type=F32)


def _dot_tn(a, b):
    return lax.dot_general(_mx(a), _mx(b), (((0,), (0,)), ((), ())), preferred_element_type=F32)


def _dot_hi(a, b):
    return jnp.dot(a, b, precision=HI, preferred_element_type=F32)


def _sigmoid(v):
    return 1.0 / (1.0 + jnp.exp(-v))


def _sds(shape, dtype):
    return jax.ShapeDtypeStruct(shape, dtype)


def _mm(name, a, b, tm, tn, nt=False, out_dtype=F32):
    batched = a.ndim == 3
    m, k = a.shape[-2:]
    n = b.shape[-2] if nt else b.shape[-1]
    tm, tn = min(tm, m), min(tn, n)

    def body(a_ref, b_ref, o_ref):
        f = _dot_nt if nt else _dot
        o_ref[...] = f(a_ref[...], b_ref[...]).astype(o_ref.dtype)

    rows_inner = (n // tn) * m < (m // tm) * n

    def ij(u, v):
        return (v, u) if rows_inner else (u, v)

    if batched:
        nb = a.shape[0]
        grid = (nb,) + ij(m // tm, n // tn)
        a_spec = pl.BlockSpec((None, tm, k), lambda g, u, v: (g, ij(u, v)[0], 0))
        b_spec = (pl.BlockSpec((None, tn, k), lambda g, u, v: (g, ij(u, v)[1], 0)) if nt
                  else pl.BlockSpec((None, k, tn), lambda g, u, v: (g, 0, ij(u, v)[1])))
        o_spec = pl.BlockSpec((None, tm, tn), lambda g, u, v: (g,) + ij(u, v))
        out_shape = _sds((nb, m, n), out_dtype)
        sem = ("parallel", "parallel", "parallel")
    else:
        grid = ij(m // tm, n // tn)
        a_spec = pl.BlockSpec((tm, k), lambda u, v: (ij(u, v)[0], 0))
        b_spec = (pl.BlockSpec((tn, k), lambda u, v: (ij(u, v)[1], 0)) if nt
                  else pl.BlockSpec((k, tn), lambda u, v: (0, ij(u, v)[1])))
        o_spec = pl.BlockSpec((tm, tn), lambda u, v: ij(u, v))
        out_shape = _sds((m, n), out_dtype)
        sem = ("parallel", "parallel")
    return pl.pallas_call(body, name=name, grid=grid, in_specs=[a_spec, b_spec], out_specs=o_spec,
                          out_shape=out_shape, compiler_params=_cp(sem))(a, b)


def _mm_tn(name, a, b, tm, tn, shard=None):
    batched = a.ndim == 3
    k, m = a.shape[-2:]
    n = b.shape[-1]
    tm, tn = min(tm, m), min(tn, n)

    def body(a_ref, b_ref, o_ref):
        o_ref[...] = lax.dot_general(_mx(a_ref[...]), _mx(b_ref[...]), (((0,), (0,)), ((), ())),
                                     preferred_element_type=F32)

    if batched:
        nb = a.shape[0]
        grid = (nb, m // tm, n // tn)
        a_spec = pl.BlockSpec((None, k, tm), lambda g, i, j: (g, 0, i))
        b_spec = pl.BlockSpec((None, k, tn), lambda g, i, j: (g, 0, j))
        if shard == "rows":
            assert 4 * tm == m
            o_spec = pl.BlockSpec((None, tm, tn), lambda g, i, j: (i, g, j))
            out_shape = _sds((4, nb * tm, n), F32)
        else:
            o_spec = pl.BlockSpec((None, tm, tn), lambda g, i, j: (g, i, j))
            out_shape = _sds((nb, m, n), F32)
    else:
        grid = (m // tm, n // tn)
        a_spec = pl.BlockSpec((k, tm), lambda i, j: (0, i))
        b_spec = pl.BlockSpec((k, tn), lambda i, j: (0, j))
        if shard == "cols":
            per = n // 4 // tn
            o_spec = pl.BlockSpec((None, tm, tn), lambda i, j: (j // per, i, j % per))
            out_shape = _sds((4, m, n // 4), F32)
        else:
            o_spec = pl.BlockSpec((tm, tn), lambda i, j: (i, j))
            out_shape = _sds((m, n), F32)
    return pl.pallas_call(body, name=name, grid=grid, in_specs=[a_spec, b_spec], out_specs=o_spec,
                          out_shape=out_shape, compiler_params=_cp(("parallel",) * len(grid)))(a, b)


def _ln_rows(y, g, b):
    mu = jnp.mean(y, axis=-1, keepdims=True)
    yc = y - mu
    var = jnp.mean(yc * yc, axis=-1, keepdims=True)
    rs = lax.rsqrt(var + LN_EPS)
    xh = yc * rs
    return xh * g + b, xh, rs


def _ln_in(x, g, b, tm=256):
    t, d = x.shape

    def body(x_ref, g_ref, b_ref, o_ref, ob_ref, xh_ref, rs_ref):
        o, xh, rs = _ln_rows(x_ref[...], g_ref[...], b_ref[...])
        o_ref[...] = o
        ob_ref[...] = o.astype(BF16)
        xh_ref[...] = xh
        rs_ref[...] = rs

    row = pl.BlockSpec((tm, d), lambda i: (i, 0))
    vec = pl.BlockSpec((1, d), lambda i: (0, 0))
    return pl.pallas_call(
        body, name="ln_in", grid=(t // tm,), in_specs=[row, vec, vec],
        out_specs=[row, row, row, pl.BlockSpec((tm, 1), lambda i: (i, 0))],
        out_shape=[_sds((t, d), F32), _sds((t, d), BF16), _sds((t, d), F32), _sds((t, 1), F32)],
        compiler_params=_cp(("parallel",)))(x, g, b)


def _mm_res_ln(name, a, w, res, g, b, tm, relu2):
    t, k = a.shape
    d = w.shape[1]

    def body(a_ref, w_ref, r_ref, g_ref, b_ref, o_ref, ob_ref, xh_ref, rs_ref, *act_ref):
        av = a_ref[...]
        if relu2:
            av = jnp.square(jnp.maximum(av, 0.0))
            act_ref[0][...] = av.astype(BF16)
        y = ALPHA * r_ref[...] + _dot(av, w_ref[...])
        o, xh, rs = _ln_rows(y, g_ref[...], b_ref[...])
        o_ref[...] = o
        ob_ref[...] = o.astype(BF16)
        xh_ref[...] = xh
        rs_ref[...] = rs

    row = pl.BlockSpec((tm, d), lambda i: (i, 0))
    vec = pl.BlockSpec((1, d), lambda i: (0, 0))
    arow = pl.BlockSpec((tm, k), lambda i: (i, 0))
    out_specs = [row, row, row, pl.BlockSpec((tm, 1), lambda i: (i, 0))]
    out_shape = [_sds((t, d), F32), _sds((t, d), BF16), _sds((t, d), F32), _sds((t, 1), F32)]
    if relu2:
        out_specs.append(arow)
        out_shape.append(_sds((t, k), BF16))
    return pl.pallas_call(
        body, name=name, grid=(t // tm,),
        in_specs=[arow, pl.BlockSpec((k, d), lambda i: (0, 0)), row, vec, vec],
        out_specs=out_specs, out_shape=out_shape, compiler_params=_cp(("parallel",)))(a, w, res, g, b)


def _merge_fwd(bo, wb, p, gate_off, tm=512, tn=512):
    _, t, d = bo.shape
    gb = gate_off // tn

    def body(bo_ref, wb_ref, g0, g1, g2, proj_ref, m_ref):
        acc = None
        for n, g_ref in enumerate((g0, g1, g2)):
            pr = _dot(bo_ref[n], wb_ref[n])
            proj_ref[n] = pr
            term = _sigmoid(g_ref[...]) * pr
            acc = term if acc is None else acc + term
        m_ref[...] = acc.astype(BF16)

    gspecs = [pl.BlockSpec((tm, tn), functools.partial(lambda i, j, n: (i, gb + n * (d // tn) + j), n=n))
              for n in range(3)]
    return pl.pallas_call(
        body, name="merge_fwd", grid=(t // tm, d // tn),
        in_specs=[pl.BlockSpec((3, tm, d), lambda i, j: (0, i, 0)),
                  pl.BlockSpec((3, d, tn), lambda i, j: (0, 0, j))] + gspecs,
        out_specs=[pl.BlockSpec((3, tm, tn), lambda i, j: (0, i, j)), pl.BlockSpec((tm, tn), lambda i, j: (i, j))],
        out_shape=[_sds((3, t, d), F32), _sds((t, d), BF16)],
        compiler_params=_cp(("parallel", "parallel")))(bo, wb, p, p, p)


def _merge_bwd(dz, wout, proj, p, gate_off, tm=512, tn=512):
    t, d = dz.shape
    gb = gate_off // tn

    def body(dz_ref, w_ref, proj_ref, g0, g1, g2, dproj_ref, dgl_ref):
        dm = _dot_nt(dz_ref[...], w_ref[...])
        for n, g_ref in enumerate((g0, g1, g2)):
            s = _sigmoid(g_ref[...])
            dproj_ref[n] = (dm * s).astype(BF16)
            dgl_ref[n] = (dm * proj_ref[n] * (s * (1.0 - s))).astype(BF16)

    gspecs = [pl.BlockSpec((tm, tn), functools.partial(lambda i, j, n: (i, gb + n * (d // tn) + j), n=n))
              for n in range(3)]
    dproj, dgl = pl.pallas_call(
        body, name="merge_bwd", grid=(t // tm, d // tn),
        in_specs=[pl.BlockSpec((tm, d), lambda i, j: (i, 0)), pl.BlockSpec((tn, d), lambda i, j: (j, 0)),
                  pl.BlockSpec((3, tm, tn), lambda i, j: (0, i, j))] + gspecs,
        out_specs=[pl.BlockSpec((3, tm, tn), lambda i, j: (0, i, j)),
                   pl.BlockSpec((3, tm, tn), lambda i, j: (0, i, j))],
        out_shape=[_sds((3, t, d), BF16), _sds((3, t, d), BF16)],
        compiler_params=_cp(("parallel", "parallel")))(dz, wout, proj, p, p, p)
    return dproj, dgl


def _mm_nt_relu2_bwd(dz, wdown, u, tm=512, tn=1024):
    t, d = dz.shape
    f = wdown.shape[0]

    def body(dz_ref, w_ref, u_ref, du_ref):
        da = _dot_nt(dz_ref[...], w_ref[...])
        du_ref[...] = (da * (2.0 * jnp.maximum(u_ref[...], 0.0))).astype(BF16)

    return pl.pallas_call(
        body, name="mlp_down_bwd", grid=(t // tm, f // tn),
        in_specs=[pl.BlockSpec((tm, d), lambda i, j: (i, 0)), pl.BlockSpec((tn, d), lambda i, j: (j, 0)),
                  pl.BlockSpec((tm, tn), lambda i, j: (i, j))],
        out_specs=pl.BlockSpec((tm, tn), lambda i, j: (i, j)), out_shape=_sds((t, f), BF16),
        compiler_params=_cp(("parallel", "parallel")))(dz, wdown, u)


def _ln_bwd_rows(dx, xh, rs, g):
    dxh = dx * g
    m1 = jnp.mean(dxh, axis=-1, keepdims=True)
    m2 = jnp.mean(dxh * xh, axis=-1, keepdims=True)
    return rs * (dxh - m1 - xh * m2)


def _mm_nt_res_lnbwd(name, a, w, dres, xh, rs, g, tm, tk):
    t, k = a.shape
    d = w.shape[0]
    nk = k // tk

    def body(a_ref, w_ref, dr_ref, xh_ref, rs_ref, g_ref, dz_ref, dzb_ref, dg_ref, db_ref, acc_ref):
        i, kk = pl.program_id(0), pl.program_id(1)

        @pl.when(kk == 0)
        def _():
            acc_ref[...] = ALPHA * dr_ref[...]

        acc_ref[...] += _dot_nt(a_ref[...], w_ref[...])

        @pl.when(jnp.logical_and(i == 0, kk == 0))
        def _():
            dg_ref[...] = jnp.zeros_like(dg_ref)
            db_ref[...] = jnp.zeros_like(db_ref)

        @pl.when(kk == nk - 1)
        def _():
            dx = acc_ref[...]
            xhv = xh_ref[...]
            dz = _ln_bwd_rows(dx, xhv, rs_ref[...], g_ref[...])
            dz_ref[...] = dz
            dzb_ref[...] = dz.astype(BF16)
            dg_ref[...] += jnp.sum(dx * xhv, axis=0, keepdims=True)
            db_ref[...] += jnp.sum(dx, axis=0, keepdims=True)

    row = pl.BlockSpec((tm, d), lambda i, kk: (i, 0))
    vec = pl.BlockSpec((1, d), lambda i, kk: (0, 0))
    return pl.pallas_call(
        body, name=name, grid=(t // tm, nk),
        in_specs=[pl.BlockSpec((tm, tk), lambda i, kk: (i, kk)), pl.BlockSpec((d, tk), lambda i, kk: (0, kk)),
                  row, row, pl.BlockSpec((tm, 1), lambda i, kk: (i, 0)), vec],
        out_specs=[row, row, vec, vec],
        out_shape=[_sds((t, d), F32), _sds((t, d), BF16), _sds((1, d), F32), _sds((1, d), F32)],
        scratch_shapes=[pltpu.VMEM((tm, d), F32)],
        compiler_params=_cp(("arbitrary", "arbitrary")))(a, w, dres, xh, rs, g)


def _loss_ln_bwd(x2, target, xh, rs, g, tm=256):
    t, d = x2.shape

    def body(x_ref, t_ref, xh_ref, rs_ref, g_ref, loss_ref, dz_ref, dzb_ref, dg_ref, db_ref):
        @pl.when(pl.program_id(0) == 0)
        def _():
            loss_ref[...] = jnp.zeros_like(loss_ref)
            dg_ref[...] = jnp.zeros_like(dg_ref)
            db_ref[...] = jnp.zeros_like(db_ref)

        err = x_ref[...] - t_ref[...]
        per_row = jnp.mean(err * err, axis=-1, keepdims=True)
        loss_ref[...] += 0.5 * jnp.sum(per_row, axis=0, keepdims=True)
        dx = err * (1.0 / d)
        xhv = xh_ref[...]
        dz = _ln_bwd_rows(dx, xhv, rs_ref[...], g_ref[...])
        dz_ref[...] = dz
        dzb_ref[...] = dz.astype(BF16)
        dg_ref[...] += jnp.sum(dx * xhv, axis=0, keepdims=True)
        db_ref[...] += jnp.sum(dx, axis=0, keepdims=True)

    row = pl.BlockSpec((tm, d), lambda i: (i, 0))
    vec = pl.BlockSpec((1, d), lambda i: (0, 0))
    return pl.pallas_call(
        body, name="loss_ln_bwd", grid=(t // tm,),
        in_specs=[row, row, row, pl.BlockSpec((tm, 1), lambda i: (i, 0)), vec],
        out_specs=[pl.BlockSpec((1, LANE), lambda i: (0, 0)), row, row, vec, vec],
        out_shape=[_sds((1, LANE), F32), _sds((t, d), F32), _sds((t, d), BF16), _sds((1, d), F32),
                   _sds((1, d), F32)],
        compiler_params=_cp(("arbitrary",)))(x2, target, xh, rs, g)


HPA = 2


STRIP = 16


def _attn_scores(q_ref, k_refs, bias_ref, i, dh, hh):
    cols = pl.ds(hh * dh, dh)
    q = q_ref[:, cols] * (dh ** -0.5)
    k = jnp.concatenate([r[:, cols] for r in k_refs], axis=0)
    s = _dot_nt(q, k)
    before_start = lax.broadcasted_iota(jnp.int32, (STRIP, KW), 1) < (2 - i) * QB
    strips = []
    for r in range(0, QB, STRIP):
        ss = jnp.where(before_start, NEG_INF, s[r:r + STRIP] + bias_ref[hh, r:r + STRIP])
        e = jnp.exp(ss - jnp.max(ss, axis=-1, keepdims=True))
        strips.append(e / jnp.sum(e, axis=-1, keepdims=True))
    return q, k, strips


def _attn_specs(dh, off):
    w = HPA * dh
    qcol, kcol, vcol = off["aq"] // w, off["ak"] // w, off["av"] // w
    q_spec = pl.BlockSpec((QB, w), lambda g, i: (i, qcol + g))
    k_specs = [pl.BlockSpec((QB, w), functools.partial(lambda g, i, j: (jnp.maximum(i - 2 + j, 0), kcol + g), j=j))
               for j in range(3)]
    v_specs = [pl.BlockSpec((QB, w), functools.partial(lambda g, i, j: (jnp.maximum(i - 2 + j, 0), vcol + g), j=j))
               for j in range(3)]
    bias_spec = pl.BlockSpec((HPA, QB, KW), lambda g, i: (g, 0, 0))
    return q_spec, k_specs, v_specs, bias_spec


def _attn_fwd(p, bias, d, off):
    t = p.shape[0]
    dh = d // ATTN_HEADS

    def body(q_ref, k0, k1, k2, v0, v1, v2, bias_ref, o_ref):
        for hh in range(HPA):
            cols = pl.ds(hh * dh, dh)
            _, _, strips = _attn_scores(q_ref, (k0, k1, k2), bias_ref, pl.program_id(1), dh, hh)
            pr = jnp.concatenate([_mx(ps) for ps in strips], axis=0)
            v = jnp.concatenate([v0[:, cols], v1[:, cols], v2[:, cols]], axis=0)
            o_ref[:, cols] = _dot(pr, v).astype(o_ref.dtype)

    q_spec, k_specs, v_specs, bias_spec = _attn_specs(dh, off)
    return pl.pallas_call(
        body, name="attn_fwd", grid=(ATTN_HEADS // HPA, t // QB),
        in_specs=[q_spec] + k_specs + v_specs + [bias_spec],
        out_specs=pl.BlockSpec((None, QB, HPA * dh), lambda g, i: (0, i, g)),
        out_shape=_sds((N_BRANCH, t, d), BF16),
        compiler_params=_cp(("parallel", "parallel")))(p, p, p, p, p, p, p, bias)


def _attn_bwd(p, bias, do, d, off):
    t = p.shape[0]
    dh = d // ATTN_HEADS
    tp = t + 2 * QB

    def body(q_ref, k0, k1, k2, v0, v1, v2, bias_ref, do_ref, dq_ref, dk_ref, dv_ref, dbias_ref):
        i = pl.program_id(1)

        @pl.when(i == 0)
        def _():
            dk_ref[...] = jnp.zeros_like(dk_ref)
            dv_ref[...] = jnp.zeros_like(dv_ref)
            dbias_ref[...] = jnp.zeros_like(dbias_ref)

        rows = pl.ds(pl.multiple_of(i * QB, QB), KW)
        for hh in range(HPA):
            cols = pl.ds(hh * dh, dh)
            q, k, strips = _attn_scores(q_ref, (k0, k1, k2), bias_ref, i, dh, hh)
            v = jnp.concatenate([v0[:, cols], v1[:, cols], v2[:, cols]], axis=0)
            dov = do_ref[:, cols]
            dp = _dot_nt(dov, v)
            ds_strips = []
            for n, ps in enumerate(strips):
                r = n * STRIP
                dps = dp[r:r + STRIP]
                dss = ps * (dps - jnp.sum(ps * dps, axis=-1, keepdims=True))
                dbias_ref[hh, r:r + STRIP] += dss
                ds_strips.append(_mx(dss))
            ds = jnp.concatenate(ds_strips, axis=0)
            pr = jnp.concatenate([_mx(ps) for ps in strips], axis=0)
            dq_ref[:, cols] = (_dot(ds, k) * (dh ** -0.5)).astype(dq_ref.dtype)
            dk_ref[rows, cols] += _dot_tn(ds, q)
            dv_ref[rows, cols] += _dot_tn(pr, dov)

    q_spec, k_specs, v_specs, bias_spec = _attn_specs(dh, off)
    row_spec = pl.BlockSpec((QB, HPA * dh), lambda g, i: (i, g))
    acc_spec = pl.BlockSpec((tp, HPA * dh), lambda g, i: (0, g))
    return pl.pallas_call(
        body, name="attn_bwd", grid=(ATTN_HEADS // HPA, t // QB),
        in_specs=[q_spec] + k_specs + v_specs + [bias_spec,
                                                 pl.BlockSpec((None, QB, HPA * dh), lambda g, i: (0, i, g))],
        out_specs=[row_spec, acc_spec, acc_spec, bias_spec],
        out_shape=[_sds((t, d), BF16), _sds((tp, d), F32), _sds((tp, d), F32),
                   _sds((ATTN_HEADS, QB, KW), F32)],
        compiler_params=_cp(("parallel", "arbitrary")))(p, p, p, p, p, p, p, bias, do)


def _onehot_mm(name, a, b):
    def body(a_ref, b_ref, o_ref):
        o_ref[...] = _dot_hi(a_ref[...], b_ref[...])

    return pl.pallas_call(body, name=name, out_shape=_sds((a.shape[0], b.shape[1]), F32),
                          compiler_params=pltpu.CompilerParams(vmem_limit_bytes=VMEM_LIMIT))(a, b)


def _diag_index():
    ii, jj = np.arange(CHUNK)[:, None], np.arange(CHUNK)[None, :]
    return (ii - jj + CHUNK - 1).reshape(-1)


def _bias_expand(rel_bias):
    h = rel_bias.shape[0]
    nq, nk, shift = QB // CHUNK, KW // CHUNK, (2 * QB) // CHUNK
    nbin, ndc = 3 * LANE, 4
    rb = jnp.pad(rel_bias, ((0, 0), (0, nbin - rel_bias.shape[1])))
    win = np.clip(CHUNK * np.arange(ndc)[:, None] + np.arange(LANE)[None, :] - (CHUNK - 1), -REL_CLIP, REL_CLIP)
    sel = (jnp.arange(nbin)[:, None] == jnp.asarray((win + REL_CLIP).reshape(1, -1))).astype(F32)
    windows = _onehot_mm("bias_windows", rb, sel)
    diag_t = (jnp.arange(LANE)[:, None] == jnp.asarray(_diag_index().reshape(1, -1))).astype(F32)
    blocks = _onehot_mm("bias_blocks", windows.reshape(h * ndc, LANE), diag_t).reshape(h, ndc, CHUNK, CHUNK)
    off_band = jnp.full((h, CHUNK, CHUNK), NEG_INF, F32)
    rows = []
    for ic in range(nq):
        dcs = [ic - jc + shift for jc in range(nk)]
        rows.append(jnp.concatenate([blocks[:, min(dc, ndc - 1)] if 0 <= dc <= ATTN_LEFT else off_band
                                     for dc in dcs], axis=2))
    return jnp.concatenate(rows, axis=1)


def _bias_reduce(dbias):
    h = dbias.shape[0]
    nq, nk = QB // CHUNK, KW // CHUNK
    nbin = 3 * LANE
    blocks = dbias.reshape(h, nq, CHUNK, nk, CHUNK).transpose(0, 1, 3, 2, 4).reshape(h * nq * nk, CHUNK * CHUNK)
    diag = (jnp.asarray(_diag_index().reshape(-1, 1)) == jnp.arange(LANE)[None, :]).astype(F32)
    ic = np.arange(nq)[:, None, None]
    jc = np.arange(nk)[None, :, None]
    dl = np.arange(LANE)[None, None, :] - (CHUNK - 1)
    rel = np.clip(CHUNK * (ic - jc + (2 * QB) // CHUNK) + dl, -REL_CLIP, REL_CLIP) + REL_CLIP
    bins = (jnp.asarray(rel.reshape(-1, 1)) == jnp.arange(nbin)[None, :]).astype(F32)

    diags = _onehot_mm("bias_diag_sums", blocks, diag)
    out = _onehot_mm("bias_bin_sums", diags.reshape(h, nq * nk * LANE), bins)
    return out[:, :2 * REL_CLIP + 1]


def _chunk_masks():
    r = lax.broadcasted_iota(jnp.int32, (LB, LB), 0)
    c = lax.broadcasted_iota(jnp.int32, (LB, LB), 1)
    return (r // CHUNK) == (c // CHUNK), r >= c, r <= c


def _chunks(a):
    return [a[c * CHUNK:(c + 1) * CHUNK] for c in range(LB // CHUNK)]


def _per_chunk(a, f):
    return jnp.concatenate([jnp.broadcast_to(f(c), c.shape) for c in _chunks(a)], axis=0)


def _dot_sel(sel, x):
    def top(v):
        return lax.bitcast_convert_type(lax.bitcast_convert_type(v, jnp.int32) & jnp.int32(-65536), F32)

    hi = top(x)
    mid = top(x - hi)
    lo = (x - hi) - mid
    d = functools.partial(jnp.dot, sel.astype(jnp.bfloat16), preferred_element_type=F32)
    return d(hi.astype(jnp.bfloat16)) + d(mid.astype(jnp.bfloat16)) + d(lo.astype(jnp.bfloat16))


def _lin_block(gla, q, k, v, aux):
    dk = q.shape[-1]
    same, low, up = _chunk_masks()
    ones = same.astype(F32)
    if gla:
        glr, wlr, blr = aux
        q = q * (dk ** -0.5)
        pre = _dot(glr, wlr) + blr
        log_a = (jnp.minimum(pre, 0.0) - jnp.log(1.0 + jnp.exp(-jnp.abs(pre)))) / GATE_NORM
        b = _dot_sel(jnp.where(low, ones, 0.0), log_a)
        lastb = _per_chunk(b, lambda c: c[CHUNK - 1:])
    else:
        cs, sn, lg = aux
        pre = None
        half = dk // 2
        q = q * cs + pltpu.roll(q, half, 1) * sn
        k = (k * cs + pltpu.roll(k, half, 1) * sn) * (dk ** -0.5)
        pos = (lax.broadcasted_iota(jnp.int32, (LB, dk), 0) % CHUNK).astype(F32) + 1.0
        b = pos * lg
        lastb = jnp.broadcast_to(float(CHUNK) * lg, b.shape)
    eb, enb, el, dec = jnp.exp(b), jnp.exp(-b), jnp.exp(lastb - b), jnp.exp(lastb)
    qf, kf, qb, kb, kl = q * eb, k * enb, q * enb, k * eb, k * el
    s = jnp.where(same, jnp.where(low, _dot_nt(qf, kf), _dot_nt(qb, kb)), 0.0)
    return dict(pre=pre, eb=eb, enb=enb, el=el, dec=dec, qf=qf, kf=kf, qb=qb, kb=kb, kl=kl, s=s,
                same=same, low=low, up=up, ones=ones)


def _lin_norm_gate(gla, o, gate, gn):
    sg = _sigmoid(gate)
    silu = gate * sg
    if gla:
        r = lax.rsqrt(jnp.mean(o * o, axis=-1, keepdims=True) + LN_EPS)
        hn = o * r
        return silu * (hn * gn), (sg, silu, r, hn)
    mu = jnp.mean(o, axis=-1, keepdims=True)
    oc = o - mu
    r = lax.rsqrt(jnp.mean(oc * oc, axis=-1, keepdims=True) + LN_EPS)
    hn = oc * r
    return silu * hn, (sg, silu, r, hn)


HPS = 2


def _lin_specs(gla, dk, dv, off, rev, nb):
    pre = "g" if gla else "r"
    wk, wv = HPS * dk, HPS * dv
    qc, kc, vc, gc = (off[pre + "q"] // wk, off[pre + "k"] // wk, off[pre + "v"] // wv, off[pre + "g"] // wv)

    def blk(i):
        return nb - 1 - i if rev else i

    specs = [pl.BlockSpec((LB, wk), lambda g, i: (blk(i), qc + g)),
             pl.BlockSpec((LB, wk), lambda g, i: (blk(i), kc + g)),
             pl.BlockSpec((LB, wv), lambda g, i: (blk(i), vc + g)),
             pl.BlockSpec((LB, wv), lambda g, i: (blk(i), gc + g))]
    if gla:
        specs += [pl.BlockSpec((LB, LANE), lambda g, i: (blk(i), off["glr"] // LANE)),
                  pl.BlockSpec((LANE, wk), lambda g, i: (0, g)),
                  pl.BlockSpec((1, wk), lambda g, i: (0, g)),
                  pl.BlockSpec((1, dv), lambda g, i: (0, 0))]
    else:
        specs += [pl.BlockSpec((LB, dk), lambda g, i: (blk(i), 0)),
                  pl.BlockSpec((LB, dk), lambda g, i: (blk(i), 0)),
                  pl.BlockSpec((HPS, 1, dk), lambda g, i: (g, 0, 0))]
    return specs, blk


def _lin_aux(gla, refs, rows, hh, dk):
    if gla:
        glr_ref, wlr_ref, blr_ref, gn_ref = refs
        kcols = pl.ds(hh * dk, dk)
        return (glr_ref[rows, :], wlr_ref[:, kcols], blr_ref[:, kcols]), gn_ref[...]
    cs_ref, sn_ref, lg_ref = refs
    return (cs_ref[rows, :], sn_ref[rows, :], lg_ref[hh]), None


def _lin_fwd(gla, p, aux_arrays, d, off, branches, slot):
    t = p.shape[0]
    dk, dv = d // (2 * LIN_HEADS), d // LIN_HEADS
    nb, cb = t // LB, LB // CHUNK
    naux = len(aux_arrays)

    def body(*refs):
        q_ref, k_ref, v_ref, g_ref = refs[:4]
        aux_refs = refs[4:4 + naux]
        o_ref, bo_ref, st_out_ref, st_ref = refs[5 + naux:]

        @pl.when(pl.program_id(1) == 0)
        def _():
            st_ref[...] = jnp.zeros_like(st_ref)

        rows = slice(None)
        for hh in range(HPS):
            kcols, vcols = pl.ds(hh * dk, dk), pl.ds(hh * dv, dv)
            aux, gn = _lin_aux(gla, aux_refs, rows, hh, dk)
            v = v_ref[:, vcols]
            blk = _lin_block(gla, q_ref[:, kcols], k_ref[:, kcols], v, aux)
            st = st_ref[hh]
            inter = []
            for c, (qf, kl, dec, vc) in enumerate(zip(_chunks(blk["qf"]), _chunks(blk["kl"]), _chunks(blk["dec"]),
                                                      _chunks(v))):
                st_out_ref[hh, c] = st
                inter.append(_dot_nt(qf, st))
                st = st * dec[:1] + _dot_tn(vc, kl)
            st_ref[hh] = st
            o = _dot(blk["s"], v) + jnp.concatenate(inter, axis=0)
            o_ref[:, vcols] = o
            out, _ = _lin_norm_gate(gla, o, g_ref[:, vcols], gn)
            bo_ref[:, vcols] = out.astype(BF16)

    specs, _ = _lin_specs(gla, dk, dv, off, False, nb)
    orow = pl.BlockSpec((LB, HPS * dv), lambda g, i: (i, g))
    return pl.pallas_call(
        body, name="gla_fwd" if gla else "ret_fwd", grid=(LIN_HEADS // HPS, nb), in_specs=specs + [ANY],
        out_specs=[orow, pl.BlockSpec((None, LB, HPS * dv), lambda g, i: (slot, i, g)),
                   pl.BlockSpec((HPS, cb, dv, dk), lambda g, i: (g, i, 0, 0))],
        out_shape=[_sds((t, d), F32), _sds(branches.shape, BF16), _sds((LIN_HEADS, t // CHUNK, dv, dk), F32)],
        scratch_shapes=[pltpu.VMEM((HPS, dv, dk), F32)], input_output_aliases={4 + naux: 1},
        compiler_params=_cp(("parallel", "arbitrary")))(p, p, p, p, *aux_arrays, branches)


def _lin_bwd(gla, p, aux_arrays, o, states, dbo, slot, d, off):
    t = p.shape[0]
    dk, dv = d // (2 * LIN_HEADS), d // LIN_HEADS
    nb, cb = t // LB, LB // CHUNK
    naux = len(aux_arrays)

    def body(*refs):
        q_ref, k_ref, v_ref, g_ref = refs[:4]
        aux_refs = refs[4:4 + naux]
        o_ref, st_in_ref, dbo_ref = refs[4 + naux:7 + naux]
        outs = refs[7 + naux:]
        dq_ref, dk_ref, dv_ref, dg_ref = outs[:4]
        dst_ref = outs[-1]
        first = pl.program_id(1) == 0

        @pl.when(first)
        def _():
            dst_ref[...] = jnp.zeros_like(dst_ref)

        if gla:
            dpre_ref, dblr_ref, dgn_ref = outs[4:7]

            @pl.when(first)
            def _():
                dblr_ref[...] = jnp.zeros_like(dblr_ref)
                dgn_ref[...] = jnp.zeros_like(dgn_ref)

        rows = slice(None)
        for hh in range(HPS):
            kcols, vcols = pl.ds(hh * dk, dk), pl.ds(hh * dv, dv)
            aux, gn = _lin_aux(gla, aux_refs, rows, hh, dk)
            v = v_ref[:, vcols]
            bk = _lin_block(gla, q_ref[:, kcols], k_ref[:, kcols], v, aux)
            eb, enb, el, dec = bk["eb"], bk["enb"], bk["el"], bk["dec"]
            qf, kf, qb, kb, kl, s = bk["qf"], bk["kf"], bk["qb"], bk["kb"], bk["kl"], bk["s"]
            gate = g_ref[:, vcols]
            dout = dbo_ref[:, vcols]
            _, (sg, silu, r, hn) = _lin_norm_gate(gla, o_ref[:, vcols], gate, gn)
            dsilu = sg * (1.0 + gate * (1.0 - sg))
            if gla:
                y = hn * gn
                dy = dout * silu
                dg_ref[:, vcols] = (dout * y * dsilu).astype(BF16)
                dgn_ref[hh] += jnp.sum(dy * hn, axis=0, keepdims=True)
                dhn = dy * gn
                do = r * (dhn - hn * jnp.mean(dhn * hn, axis=-1, keepdims=True))
            else:
                dhn = dout * silu
                dg_ref[:, vcols] = (dout * hn * dsilu).astype(BF16)
                do = r * (dhn - jnp.mean(dhn, axis=-1, keepdims=True)
                          - hn * jnp.mean(dhn * hn, axis=-1, keepdims=True))
            ds = jnp.where(bk["same"], _dot_nt(do, v), 0.0)
            dsf = jnp.where(bk["low"], ds, 0.0)
            dsb = ds - dsf
            dvv = _dot_tn(s, do)
            dqf = _dot(dsf, kf)
            dkf = _dot_tn(dsf, qf)
            dqb = _dot(dsb, kb)
            dkb = _dot_tn(dsb, qb)
            dst = dst_ref[hh]
            dv_st, dqf_st, dkl_c, ddec_c = [], [], [], []
            parts = zip(reversed(range(cb)), reversed(_chunks(do)), reversed(_chunks(v)), reversed(_chunks(qf)),
                        reversed(_chunks(kl)), reversed(_chunks(dec)))
            for c, do_c, v_c, qf_c, kl_c, dec_c in parts:
                st = st_in_ref[hh, c]
                dv_st.append(_dot_nt(kl_c, dst))
                dkl_c.append(_dot(v_c, dst))
                dqf_st.append(_dot(do_c, st))
                ddec_c.append(jnp.broadcast_to(jnp.sum(dst * st, axis=0, keepdims=True), (CHUNK, dk)))
                dst = dst * dec_c[:1] + _dot_tn(do_c, qf_c)
            dst_ref[hh] = dst

            def cat(pieces):
                return jnp.concatenate(pieces[::-1], axis=0)

            dvv = dvv + cat(dv_st)
            dqf = dqf + cat(dqf_st)
            dkl = cat(dkl_c)
            dq = dqf * eb + dqb * enb
            dkk = dkf * enb + dkb * eb + dkl * el
            dv_ref[:, vcols] = dvv.astype(BF16)
            if gla:
                db = dqf * qf - dkf * kf - dqb * qb + dkb * kb - dkl * kl
                dlast = _per_chunk(dkl * kl, lambda c: jnp.sum(c, axis=0, keepdims=True)) + cat(ddec_c) * dec
                dlog_a = _dot_sel(jnp.where(bk["up"], bk["ones"], 0.0), db) + dlast
                dpre = dlog_a * (1.0 / GATE_NORM) * (1.0 - _sigmoid(bk["pre"]))
                dpre_ref[:, kcols] = dpre
                dblr_ref[hh] += jnp.sum(dpre, axis=0, keepdims=True)
                dq_ref[:, kcols] = (dq * (dk ** -0.5)).astype(BF16)
                dk_ref[:, kcols] = dkk.astype(BF16)
            else:
                cs, sn, _ = aux
                half = dk // 2
                dkk = dkk * (dk ** -0.5)
                dq_ref[:, kcols] = (dq * cs + pltpu.roll(dq * sn, half, 1)).astype(BF16)
                dk_ref[:, kcols] = (dkk * cs + pltpu.roll(dkk * sn, half, 1)).astype(BF16)

    specs, blk = _lin_specs(gla, dk, dv, off, True, nb)
    vrow = pl.BlockSpec((LB, HPS * dv), lambda g, i: (blk(i), g))
    krow = pl.BlockSpec((LB, HPS * dk), lambda g, i: (blk(i), g))
    specs += [vrow, pl.BlockSpec((HPS, cb, dv, dk), lambda g, i: (g, blk(i), 0, 0)),
              pl.BlockSpec((None, LB, HPS * dv), lambda g, i: (slot, blk(i), g))]
    out_specs = [krow, krow, vrow, vrow]
    out_shape = [_sds((t, d // 2), BF16), _sds((t, d // 2), BF16), _sds((t, d), BF16), _sds((t, d), BF16)]
    if gla:
        out_specs += [krow, pl.BlockSpec((HPS, 1, dk), lambda g, i: (g, 0, 0)),
                      pl.BlockSpec((HPS, 1, dv), lambda g, i: (g, 0, 0))]
        out_shape += [_sds((t, d // 2), F32), _sds((LIN_HEADS, 1, dk), F32), _sds((LIN_HEADS, 1, dv), F32)]
    out_specs.append(pl.BlockSpec((HPS, dv, dk), lambda g, i: (g, 0, 0)))
    out_shape.append(_sds((LIN_HEADS, dv, dk), F32))
    res = pl.pallas_call(
        body, name="gla_bwd" if gla else "ret_bwd", grid=(LIN_HEADS // HPS, nb), in_specs=specs,
        out_specs=out_specs, out_shape=out_shape,
        compiler_params=_cp(("parallel", "arbitrary")))(p, p, p, p, *aux_arrays, o, states, dbo)
    return res[:-1]


def _row_tile(rows, cols):
    cap = max(8, (2 << 20) // (4 * cols))
    t = rows
    while t > cap and t % 2 == 0:
        t //= 2
    return t


def _add_half(name, g, t, sel):
    nchip, hr, cols = t.shape
    tr = _row_tile(hr, cols)
    nb = hr // tr

    def body(sel_ref, g_ref, t_ref, o_ref):
        o_ref[...] = g_ref[...] + t_ref[...]

    half = pl.BlockSpec((None, tr, cols), lambda p, i, s: (p, i, 0))
    gs = pltpu.PrefetchScalarGridSpec(
        num_scalar_prefetch=1, grid=(nchip, nb),
        in_specs=[pl.BlockSpec((None, tr, cols), lambda p, i, s: (p, s[0] * nb + i, 0)), half], out_specs=half)
    return pl.pallas_call(body, name=name, grid_spec=gs, out_shape=_sds(t.shape, F32),
                          compiler_params=_cp(("parallel", "parallel")))(sel, g, t)


def _sum_shards(name, h, rcv, sel):
    _, rows, cols = h.shape
    tr = _row_tile(rows, cols)

    def body(sel_ref, h_ref, r0, r1, r2, o_ref):
        o_ref[...] = ((h_ref[...] + r0[...]) + r1[...]) + r2[...]

    rspecs = [pl.BlockSpec((None, tr, cols), functools.partial(lambda i, s, j: (j, i, 0), j=j)) for j in range(3)]
    gs = pltpu.PrefetchScalarGridSpec(
        num_scalar_prefetch=1, grid=(rows // tr,),
        in_specs=[pl.BlockSpec((None, tr, cols), lambda i, s: (s[0], i, 0))] + rspecs,
        out_specs=pl.BlockSpec((tr, cols), lambda i, s: (i, 0)))
    return pl.pallas_call(body, name=name, grid_spec=gs, out_shape=_sds((rows, cols), F32),
                          compiler_params=_cp(("parallel",)))(sel, h, rcv, rcv, rcv)


def _adamw_math(w, g, m, v):
    c1 = 1.0 - ADAM_B1 ** ADAM_STEP
    c2 = 1.0 - ADAM_B2 ** ADAM_STEP
    nm = ADAM_B1 * m + (1.0 - ADAM_B1) * g
    nv = ADAM_B2 * v + (1.0 - ADAM_B2) * jnp.square(g)
    return -ADAM_LR * ((nm / c1) / (jnp.sqrt(nv / c2) + ADAM_EPS) + ADAM_WD * w), nm, nv


def _adamw(name, w, g, m, v):
    rows, cols = w.shape
    tr = _row_tile(rows, cols)

    def body(w_ref, g_ref, m_ref, v_ref, d_ref, nm_ref, nv_ref):
        d_ref[...], nm_ref[...], nv_ref[...] = _adamw_math(w_ref[...], g_ref[...], m_ref[...], v_ref[...])

    spec = pl.BlockSpec((tr, cols), lambda i: (i, 0))
    return pl.pallas_call(body, name=name, grid=(rows // tr,), in_specs=[spec] * 4, out_specs=[spec] * 3,
                          out_shape=[_sds((rows, cols), F32)] * 3, compiler_params=_cp(("parallel",)))(w, g, m, v)


def _adamw_layer(name, w, g_own, g_sib, sel, m, v, layer, prev):
    depth, rows, cols = w.shape
    tr = _row_tile(rows // 2, cols)
    nbh = rows // 2 // tr
    nprev = 0 if prev is None else 4

    def body(sel_ref, w_ref, own_ref, sib_ref, m_ref, v_ref, *rest):
        go_ref, d_ref, nm_ref, nv_ref = rest[nprev:]
        gv = jnp.where(pl.program_id(0) // nbh == sel_ref[0], own_ref[...], sib_ref[...])
        go_ref[...] = gv
        d_ref[...], nm_ref[...], nv_ref[...] = _adamw_math(w_ref[...], gv, m_ref[...], v_ref[...])

    lay = pl.BlockSpec((None, tr, cols), lambda i, s: (layer, i, 0))
    hlf = pl.BlockSpec((tr, cols), lambda i, s: (i % nbh, 0))
    gs = pltpu.PrefetchScalarGridSpec(
        num_scalar_prefetch=1, grid=(2 * nbh,), in_specs=[lay, hlf, hlf, lay, lay] + [ANY] * nprev,
        out_specs=[lay] * 4)
    args = (sel, w, g_own, g_sib, m, v) + (() if prev is None else tuple(prev))
    return pl.pallas_call(
        body, name=name, grid_spec=gs, out_shape=[_sds((depth, rows, cols), F32)] * 4,
        input_output_aliases={6 + k: k for k in range(nprev)},
        compiler_params=_cp(("parallel",)))(*args)


def _adamw_colmajor(name, wt, mt, vt, halves, sel):
    c_dim, depth, r_dim = wt.shape
    hr = r_dim // 2

    def body(sel_ref, w_ref, m_ref, v_ref, *rest):
        g_refs, (go_ref, d_ref, nm_ref, nv_ref) = rest[:2 * depth], rest[2 * depth:]
        own_first = sel_ref[0] == 0
        for l in range(depth):
            own, sib = g_refs[2 * l][...], g_refs[2 * l + 1][...]
            g = jnp.concatenate([jnp.where(own_first, own, sib), jnp.where(own_first, sib, own)], axis=0).T
            go_ref[:, l, :] = g
            d_ref[:, l, :], nm_ref[:, l, :], nv_ref[:, l, :] = _adamw_math(w_ref[:, l, :], g, m_ref[:, l, :],
                                                                          v_ref[:, l, :])

    col = pl.BlockSpec((LANE, depth, r_dim), lambda j, s: (j, 0, 0))
    gs = pltpu.PrefetchScalarGridSpec(
        num_scalar_prefetch=1, grid=(c_dim // LANE,),
        in_specs=[col] * 3 + [pl.BlockSpec((hr, LANE), lambda j, s: (0, j))] * (2 * depth), out_specs=[col] * 4)
    flat = [h for pair in halves for h in pair]
    return pl.pallas_call(body, name=name, grid_spec=gs, out_shape=[_sds(wt.shape, F32)] * 4,
                          compiler_params=_cp(("parallel",)))(sel, wt, mt, vt, *flat)


def _adamw_tail(name, wt, mt, vt, gt_tail, prev):
    c_dim, depth, r_dim = wt.shape
    nt = gt_tail.shape[0]

    def body(w_ref, m_ref, v_ref, g_ref, *rest):
        go_ref, d_ref, nm_ref, nv_ref = rest[4:]
        g = g_ref[...]
        go_ref[...] = g
        d_ref[...], nm_ref[...], nv_ref[...] = _adamw_math(w_ref[...], g, m_ref[...], v_ref[...])

    tail = pl.BlockSpec((nt, depth, r_dim), lambda i: (c_dim // nt - 1, 0, 0))
    return pl.pallas_call(
        body, name=name, grid=(1,), in_specs=[tail] * 3 + [pl.BlockSpec((nt, depth, r_dim), lambda i: (0, 0, 0))]
        + [ANY] * 4, out_specs=[tail] * 4, out_shape=[_sds(wt.shape, F32)] * 4,
        input_output_aliases={4 + k: k for k in range(4)},
        compiler_params=_cp(("arbitrary",)))(wt, mt, vt, gt_tail, *prev)


def _place():
    x, y, c = (lax.axis_index(a) for a in MESH_AXES)
    chips = [(1 - x, y), (x, 1 - y), (1 - x, 1 - y)]
    return x, y, c, chips


def _chip_index(xy):
    return 2 * xy[0] + xy[1]


ANY = pl.BlockSpec(memory_space=pl.ANY)


HBM_SPEC = pl.BlockSpec(memory_space=pltpu.HBM)
SEM = pl.BlockSpec(memory_space=pltpu.SEMAPHORE)
EFFECT = pltpu.SideEffectType.DATAFLOW_SIDE_EFFECTING


def _half(ref, c):
    hr = ref.shape[-2] // 2
    return pl.ds(pl.multiple_of(c * hr, 16), hr)


def _gather_copies(srcs, lands, send, recv):
    x, y, c, chips = _place()
    me = _chip_index((x, y))
    return [pltpu.make_async_remote_copy(src_ref=s.at[_half(s, c)], dst_ref=g.at[me, _half(s, c)],
                                         send_sem=send.at[3 * a + j], recv_sem=recv.at[3 * a + j],
                                         device_id=(*ch, c), device_id_type=DEV)
            for a, (s, g) in enumerate(zip(srcs, lands)) for j, ch in enumerate(chips)]


def _scatter_copies(srcs, lands, send, recv):
    x, y, c, chips = _place()
    return [pltpu.make_async_remote_copy(src_ref=h.at[_chip_index(ch)], dst_ref=r.at[j],
                                         send_sem=send.at[3 * a + j], recv_sem=recv.at[3 * a + j],
                                         device_id=(*ch, c), device_id_type=DEV)
            for a, (h, r) in enumerate(zip(srcs, lands)) for j, ch in enumerate(chips)]


def _in_hbm(a):
    return pltpu.with_memory_space_constraint(a, pltpu.HBM)


def _split_start(name, srcs, land_shapes, copies_fn, after=None, per_src=3):
    ns, nl = len(srcs), len(land_shapes)
    ncp = per_src * ns
    lands = [lax.empty(s.shape, s.dtype) for s in land_shapes]
    behind = [] if after is None else [after]

    def body(*refs):
        src, land = refs[:ns], refs[ns:ns + nl]
        send, recv = refs[ns + nl + len(behind)], refs[ns + nl + len(behind) + 1]
        for cp in copies_fn(src, land, send, recv):
            cp.start()
        refs[-1][...] = jnp.zeros_like(refs[-1])

    bufs = list(srcs) + lands
    outs = pl.pallas_call(
        body, name=name, in_specs=[HBM_SPEC] * (ns + nl) + [ANY] * len(behind),
        out_specs=[SEM, SEM] + [HBM_SPEC] * (ns + nl) + [pl.BlockSpec(memory_space=pltpu.VMEM)],
        out_shape=[pltpu.SemaphoreType.DMA((ncp,)), pltpu.SemaphoreType.DMA((ncp,))]
        + [pltpu.HBM(b.shape, b.dtype) for b in bufs] + [_sds((8, LANE), F32)],
        input_output_aliases={i: 2 + i for i in range(ns + nl)},
        compiler_params=pltpu.CompilerParams(has_side_effects=EFFECT))(*[_in_hbm(b) for b in bufs], *behind)
    return outs[0], outs[1], list(outs[2:2 + ns]), list(outs[2 + ns:2 + ns + nl]), outs[-1]


def _split_wait(name, started, copies_fn, after):
    send, recv, srcs, lands, _ = started
    ns, nl = len(srcs), len(lands)

    def body(*refs):
        src, land = refs[:ns], refs[ns:ns + nl]
        for cp in copies_fn(src, land, refs[ns + nl], refs[ns + nl + 1]):
            cp.wait_send()
            cp.wait_recv()

    bufs = list(srcs) + list(lands)
    outs = pl.pallas_call(
        body, name=name, in_specs=[HBM_SPEC] * (ns + nl) + [SEM, SEM, ANY], out_specs=[HBM_SPEC] * (ns + nl),
        out_shape=[pltpu.HBM(b.shape, b.dtype) for b in bufs],
        input_output_aliases={i: i for i in range(ns + nl)},
        compiler_params=pltpu.CompilerParams(has_side_effects=EFFECT))(*bufs, send, recv, after)
    return list(outs[:ns]), list(outs[ns:])


def _gather_plain(name, srcs):
    n = len(srcs)

    def body(*refs):
        src, land = refs[:n], refs[n:2 * n]
        send, recv, fsend, frecv = refs[2 * n:]
        first = _gather_copies(src, land, send, recv)
        for cp in first:
            cp.start()
        _forward_body(land, first, fsend, frecv)

    return pl.pallas_call(
        body, name=name, in_specs=[ANY] * n, out_specs=[ANY] * n,
        out_shape=[_sds((4,) + s.shape, s.dtype) for s in srcs],
        scratch_shapes=[pltpu.SemaphoreType.DMA((3 * n,))] * 4)(*srcs)


def _forward_body(land, arrivals, fsend, frecv):
    x, y, c, chips = _place()
    n = len(land)
    passed = []
    for a in range(n):
        for j, ch in enumerate(chips):
            if arrivals is not None:
                arrivals[3 * a + j].wait_recv()
            slot = land[a].at[_chip_index(ch), _half(land[a], c)]
            fw = pltpu.make_async_remote_copy(src_ref=slot, dst_ref=slot, send_sem=fsend.at[3 * a + j],
                                              recv_sem=frecv.at[3 * a + j], device_id=(x, y, 1 - c),
                                              device_id_type=DEV)
            fw.start()
            passed.append(fw)
    for a in range(n):
        for j, ch in enumerate(chips):
            slot = land[a].at[_chip_index(ch), _half(land[a], 1 - c)]
            pltpu.make_async_remote_copy(src_ref=slot, dst_ref=slot, send_sem=fsend.at[3 * a + j],
                                         recv_sem=frecv.at[3 * a + j], device_id=(x, y, c),
                                         device_id_type=DEV).wait_recv()
    for cp in passed:
        cp.wait_send()
    if arrivals is not None:
        for cp in arrivals:
            cp.wait_send()


def _gather_forward(name, lands):
    n = len(lands)

    def body(*refs):
        _forward_body(refs[n:2 * n], None, refs[2 * n], refs[2 * n + 1])

    return pl.pallas_call(
        body, name=name, in_specs=[ANY] * n, out_specs=[ANY] * n,
        out_shape=[_sds(g.shape, g.dtype) for g in lands], input_output_aliases={a: a for a in range(n)},
        scratch_shapes=[pltpu.SemaphoreType.DMA((3 * n,))] * 2)(*lands)


def _sibling_copies(srcs, lands, send, recv):
    x, y, c, _ = _place()
    return [pltpu.make_async_remote_copy(src_ref=g.at[:, _half(g, 1 - c)], dst_ref=t, send_sem=send.at[a],
                                         recv_sem=recv.at[a], device_id=(x, y, 1 - c), device_id_type=DEV)
            for a, (g, t) in enumerate(zip(srcs, lands))]


def _sibling_share(name, sms):
    n = len(sms)

    def body(*refs):
        ins, outs = refs[:n], refs[n:2 * n]
        send, recv = refs[2 * n:]
        x, y, c, _ = _place()
        cps = [pltpu.make_async_remote_copy(src_ref=ins[a], dst_ref=outs[a], send_sem=send.at[a],
                                            recv_sem=recv.at[a], device_id=(x, y, 1 - c), device_id_type=DEV)
               for a in range(n)]
        for cp in cps:
            cp.start()
        for cp in cps:
            cp.wait()

    return pl.pallas_call(
        body, name=name, in_specs=[ANY] * n, out_specs=[ANY] * n, out_shape=[_sds(s.shape, F32) for s in sms],
        scratch_shapes=[pltpu.SemaphoreType.DMA((n,))] * 2)(*sms)


def _small_allreduce(v, after=None):
    rows = v.shape[0]
    ndev = 8
    behind = [] if after is None else [after]

    def body(v_ref, *rest):
        o_ref, gat_ref, send, recv = rest[len(behind):]
        x, y, c, _ = _place()
        me = 4 * x + 2 * y + c
        cps = []
        for k in range(1, ndev):
            to = (me + k) % ndev
            cp = pltpu.make_async_remote_copy(src_ref=v_ref, dst_ref=gat_ref.at[me], send_sem=send.at[k - 1],
                                              recv_sem=recv.at[me], device_id=(to // 4, (to // 2) % 2, to % 2),
                                              device_id_type=DEV)
            cp.start()
            cps.append(cp)
        gat_ref[me] = v_ref[...]
        for k in range(1, ndev):
            frm = (me + k) % ndev
            pltpu.make_async_remote_copy(src_ref=v_ref, dst_ref=gat_ref.at[frm], send_sem=send.at[k - 1],
                                         recv_sem=recv.at[frm], device_id=(x, y, c), device_id_type=DEV).wait_recv()
        for cp in cps:
            cp.wait_send()
        acc = gat_ref[0]
        for k in range(1, ndev):
            acc = acc + gat_ref[k]
        o_ref[...] = acc

    vm = pl.BlockSpec(memory_space=pltpu.VMEM)
    return pl.pallas_call(
        body, name="small_allreduce", in_specs=[vm] + [ANY] * len(behind), out_specs=vm,
        out_shape=_sds((rows, LANE), F32),
        scratch_shapes=[pltpu.VMEM((ndev, rows, LANE), F32), pltpu.SemaphoreType.DMA((ndev - 1,)),
                        pltpu.SemaphoreType.DMA((ndev,))])(v, *behind)


def _layout(d):
    half = d // 2
    names = [("aq", d), ("ak", d), ("av", d), ("rq", half), ("rk", half), ("rv", d), ("rg", d),
             ("gq", half), ("gk", half), ("gv", d), ("gg", d), ("gates", 3 * d), ("glr", 2 * LANE)]
    off, pos = {}, 0
    for nm, sz in names:
        off[nm] = pos
        pos += sz
    return off, pos


def _unpad_cols(g, d):
    a = 8 * d + d
    return jnp.concatenate([g[..., :a], g[..., a + 3 * d:a + 3 * d + GATE_RANK], g[..., a:a + 3 * d]], axis=-1)


def kernel(x, ln_in_g, ln_in_b, w_in, rel_bias, gla_w_lr, gla_b_lr, gla_norm_g, w_branch, w_out, ln1_g, ln1_b, w_up, w_down, ln2_g, ln2_b, loss_target, m_ln_in_g, m_ln_in_b, m_w_in, m_rel_bias, m_gla_w_lr, m_gla_b_lr, m_gla_norm_g, m_w_branch, m_w_out, m_ln1_g, m_ln1_b, m_w_up, m_w_down, m_ln2_g, m_ln2_b, v_ln_in_g, v_ln_in_b, v_w_in, v_rel_bias, v_gla_w_lr, v_gla_b_lr, v_gla_norm_g, v_w_branch, v_w_out, v_ln1_g, v_ln1_b, v_w_up, v_w_down, v_ln2_g, v_ln2_b):
    t, d = x.shape[1], x.shape[2]
    dff = 4 * d
    half = d // 2
    off, npad = _layout(d)
    xi, yi, ci = (lax.axis_index(a) for a in MESH_AXES)
    chip = 2 * xi + yi
    csel = jnp.reshape(ci, (1,)).astype(jnp.int32)
    psel = jnp.reshape(chip, (1,)).astype(jnp.int32)

    big_w = [w_in, w_branch.reshape(DEPTH, -1, d), w_out, w_up, w_down]
    big_m = [m_w_in, m_w_branch.reshape(DEPTH, -1, d), m_w_out, m_w_up, m_w_down]
    big_v = [v_w_in, v_w_branch.reshape(DEPTH, -1, d), v_w_out, v_w_up, v_w_down]
    W_IN, REST = [0], [1, 2, 3, 4]

    def shards_of(l, idx):
        return [big_w[i][l].astype(BF16) for i in idx]

    def lands_of(srcs):
        return [_sds((4,) + s.shape, s.dtype) for s in srcs]

    def full_w_in(g):
        per = g.shape[2]
        a = 8 * d + d

        def run(lo, hi):
            cuts = [(max(lo, c * per), min(hi, (c + 1) * per), c) for c in range(4)]
            return [g[c, :, x - c * per:y - c * per] for x, y, c in cuts if x < y]

        zeros = jnp.zeros((d, 2 * LANE - GATE_RANK), g.dtype)
        return jnp.concatenate(run(0, a) + run(a + GATE_RANK, 4 * per) + run(a, a + GATE_RANK) + [zeros], axis=1)

    def full_rest(gs):
        g_br, g_out, g_up, g_down = gs
        return (jnp.transpose(g_br.reshape(4, N_BRANCH, d // 4, d), (1, 0, 2, 3)).reshape(N_BRANCH, d, d),
                g_out.reshape(d, d), jnp.transpose(g_up, (1, 0, 2)).reshape(d, dff), g_down.reshape(dff, d))

    def with_own(srcs, lands):
        return [lax.dynamic_update_slice(g, s[None], (chip, 0, 0)) for s, g in zip(srcs, lands)]

    def gather_start(tag, l, idx, after):
        srcs = shards_of(l, idx)
        return srcs, _split_start(f"gather_{tag}{l}_start", srcs, lands_of(srcs), _gather_copies, after)

    def gather_finish(tag, l, pending, after):
        srcs, started = pending
        _, lands = _split_wait(f"gather_{tag}{l}_wait", started, _gather_copies, after)
        return with_own(srcs, _gather_forward(f"gather_{tag}{l}_pass", lands))

    def token(pending):
        return pending[1][4][0, 0]

    win, wbr, wout, wup, wdown = ([None] * DEPTH for _ in range(5))
    src_first = shards_of(0, W_IN)
    g_first = with_own(src_first, _gather_plain("gather_in0", src_first))
    win[0] = full_w_in(g_first[0])

    dkh = half // LIN_HEADS
    lr_rows = DEPTH * GATE_RANK
    lr_slab = jnp.zeros((lr_rows, 4, half // 4), F32)
    lr_slab = lax.dynamic_update_slice(lr_slab, (gla_w_lr.reshape(lr_rows, 1, half // 4) * jnp.where(ci == 0, 1.0, 0.0)),
                                       (0, chip, 0))
    wlr_full = _small_allreduce(lr_slab.reshape(-1, LANE)).reshape(DEPTH, GATE_RANK, half)
    wlr_pad = jnp.concatenate([wlr_full, jnp.zeros((DEPTH, LANE - GATE_RANK, half), F32)], axis=1)
    pend_rest = gather_start("rest", 0, REST, wlr_full[0, :1, :1] + g_first[0][0, :1, :1].astype(F32))

    inv = 10000.0 ** (-jnp.arange(0, dkh, 2, dtype=F32) / dkh)
    ang = jnp.arange(t, dtype=F32)[:, None] * inv[None, :]
    cos, sin = jnp.cos(ang), jnp.sin(ang)
    rope_c = jnp.concatenate([cos, cos], axis=1)
    rope_s = jnp.concatenate([-sin, sin], axis=1)
    log_gamma = jnp.log1p(-jnp.exp2(-5.0 - jnp.arange(LIN_HEADS, dtype=F32)))
    lg_tab = jnp.broadcast_to(log_gamma[:, None, None], (LIN_HEADS, 1, dkh))

    def vec(a):
        return a.reshape(1, -1)

    x0, x0b, xh_in, rs_in = _ln_in(x[0], vec(ln_in_g) + token(pend_rest), vec(ln_in_b))
    saved = []
    xl, xlb = x0, x0b
    for l in range(DEPTH):
        p = _mm("proj_in", xlb, win[l], 512, 1792)
        g_rest = gather_finish("rest", l, pend_rest, p)
        wbr[l], wout[l], wup[l], wdown[l] = full_rest(g_rest)
        tok = 0.0
        if l + 1 < DEPTH:
            pend_in = gather_start("in", l + 1, W_IN, g_rest[0])
            tok = token(pend_in)
        bias = _bias_expand(rel_bias[l] + tok)
        bo = _attn_fwd(p, bias, d, off)
        ret_aux = (rope_c, rope_s, lg_tab + tok)
        gla_aux = (p, wlr_pad[l], vec(gla_b_lr[l]) + tok, vec(gla_norm_g[l]))
        o_ret, bo, st_ret = _lin_fwd(False, p, ret_aux, d, off, bo, 1)
        o_gla, bo, st_gla = _lin_fwd(True, p, gla_aux, d, off, bo, 2)
        tok = 0.0
        if l + 1 < DEPTH:
            g_in = gather_finish("in", l + 1, pend_in, bo)
            win[l + 1] = full_w_in(g_in[0])
            pend_rest = gather_start("rest", l + 1, REST, g_in[0])
            tok = token(pend_rest)
        proj, merged = _merge_fwd(bo, wbr[l], p, off["gates"])
        x1, x1b, xh1, rs1 = _mm_res_ln("out_proj_ln", merged, wout[l], xl, vec(ln1_g[l]) + tok, vec(ln1_b[l]),
                                       256, False)
        u = _mm("mlp_up", x1b, wup[l], 1024, 1024)
        x2, x2b, xh2, rs2, act = _mm_res_ln("mlp_down_ln", u, wdown[l], x1, vec(ln2_g[l]), vec(ln2_b[l]), 256, True)
        saved.append(dict(xlb=xlb, p=p, bias=bias, ret_aux=ret_aux, gla_aux=gla_aux, o_ret=o_ret, o_gla=o_gla,
                          st_ret=st_ret, st_gla=st_gla, bo=bo, proj=proj, merged=merged, x1b=x1b, xh1=xh1,
                          rs1=rs1, u=u, xh2=xh2, rs2=rs2, act=act))
        xl, xlb = x2, x2b

    small = {}
    last = saved[-1]
    loss_p, dz2, dz2b, dg, db = _loss_ln_bwd(xl, loss_target[0], last["xh2"], last["rs2"], vec(ln2_g[DEPTH - 1]))
    small["loss"] = loss_p[:, :1]
    grad_x = None

    def sibling_start(tag, l, idx, shards):
        lands = [_sds((g.shape[0], g.shape[1] // 2, g.shape[2]), F32) for g in shards]
        return tag, l, idx, _split_start(f"grad_{tag}{l}_sibling_start", shards, lands, _sibling_copies, per_src=1)

    def scatter_start(sibling, after):
        tag, l, idx, started = sibling
        shards, theirs = _split_wait(f"grad_{tag}{l}_sibling_wait", started, _sibling_copies, after)
        hs = [_add_half("grad_sibling_add", g, th, csel) for g, th in zip(shards, theirs)]
        lands = [_sds((3,) + h.shape[1:], F32) for h in hs]
        return tag, l, idx, _split_start(f"grad_{tag}{l}_scatter_start", hs, lands, _scatter_copies)

    adam_out = [None] * len(big_w)
    w_in_halves = [None] * DEPTH

    def scatter_finish(pending, after):
        tag, l, idx, started = pending
        hs, rcv = _split_wait(f"grad_{tag}{l}_scatter_wait", started, _scatter_copies, after)
        sms = [_sum_shards("grad_chip_sum", h, r, psel) for h, r in zip(hs, rcv)]
        last = None
        for i, own, sib in zip(idx, sms, _sibling_share(f"grad_{tag}{l}_share", sms)):
            if i == W_IN[0]:
                w_in_halves[l] = (own, sib)
                last = sib
            else:
                adam_out[i] = _adamw_layer("adamw_large", big_w[i], own, sib, csel, big_m[i], big_v[i], l,
                                           adam_out[i])
                last = adam_out[i][0]
        return last

    in_flight = []

    def scatter(sibling, after):
        pending = scatter_start(sibling, after)
        in_flight.append(pending)
        if len(in_flight) > 3:
            scatter_finish(in_flight.pop(0), pending[3][4])
        return pending[3][4][0, 0]

    def token_of(sibling):
        return sibling[3][4][0, 0]

    for l in reversed(range(DEPTH)):
        s = saved[l]
        small[("ln2_g", l)], small[("ln2_b", l)] = dg, db
        du = _mm_nt_relu2_bwd(dz2b, wdown[l], s["u"])
        g_wdown = _mm_tn("grad_w_down", s["act"], dz2b, 512, 512)
        g_wup = _mm_tn("grad_w_up", s["x1b"], du, 512, 512, shard="cols")
        dz1, dz1b, dg1, db1 = _mm_nt_res_lnbwd("mlp_up_bwd_ln", du, wup[l], dz2, s["xh1"], s["rs1"],
                                               vec(ln1_g[l]), 256, dff)
        small[("ln1_g", l)], small[("ln1_b", l)] = dg1, db1
        dproj, dgl = _merge_bwd(dz1b, wout[l], s["proj"], s["p"], off["gates"])
        g_wout = _mm_tn("grad_w_out", s["merged"], dz1b, 512, 512)
        dbo = _mm("branch_proj_bwd", dproj, wbr[l], 1024, 1024, nt=True)
        g_wbr = _mm_tn("grad_w_branch", s["bo"], dproj, d // 4, 1024, shard="rows")
        sib = sibling_start("rest", l, REST, [g_wbr, g_wout.reshape(4, d // 4, d), g_wup, g_wdown.reshape(4, d, d)])
        rc, rs_, lg = s["ret_aux"]
        gp, gw, gb, gn_ = s["gla_aux"]
        dq_a, dk_acc, dv_acc, dbias = _attn_bwd(s["p"], s["bias"] + token_of(sib), dbo, d, off)
        tok = scatter(sib, dq_a)
        small[("rel_bias", l)] = _bias_reduce(dbias)
        dk_a = dk_acc[2 * QB:].astype(BF16)
        dv_a = dv_acc[2 * QB:].astype(BF16)
        dq_r, dk_r, dv_r, dg_r = _lin_bwd(False, s["p"], (rc, rs_, lg + tok), s["o_ret"], s["st_ret"], dbo, 1, d, off)
        dq_g, dk_g, dv_g, dg_g, dpre, dblr, dgn = _lin_bwd(True, s["p"], (gp, gw, gb + tok, gn_), s["o_gla"],
                                                           s["st_gla"], dbo, 2, d, off)
        small[("gla_b_lr", l)] = dblr.reshape(1, half)
        small[("gla_norm_g", l)] = jnp.sum(dgn, axis=0)
        dpre_b = dpre.astype(BF16)
        glr_b = s["p"][:, off["glr"]:off["glr"] + LANE].astype(BF16)
        dglr = _mm("gate_lr_bwd", dpre_b, wlr_pad[l], 512, LANE, nt=True, out_dtype=BF16)
        small[("gla_w_lr", l)] = _mm_tn("grad_gla_w_lr", glr_b, dpre_b, LANE, half)[:GATE_RANK]
        dp = jnp.concatenate([dq_a, dk_a, dv_a, dq_r, dk_r, dv_r, dg_r, dq_g, dk_g, dv_g, dg_g,
                              dgl[0], dgl[1], dgl[2], dglr, jnp.zeros((t, LANE), BF16)], axis=1)
        if l > 0:
            prev = saved[l - 1]
            xh_p, rs_p, g_p = prev["xh2"], prev["rs2"], vec(ln2_g[l - 1])
        else:
            xh_p, rs_p, g_p = xh_in, rs_in, vec(ln_in_g)
        g_win = _mm_tn("grad_w_in", s["xlb"], dp, 1024, 896)
        sib = sibling_start("in", l, W_IN, [jnp.transpose(_unpad_cols(g_win, d).reshape(d, 4, -1), (1, 0, 2))])
        if l > 0:
            tok = token_of(sib)
        else:
            tok = scatter(sib, sib[3][4])
        dzp, dzpb, dg, db = _mm_nt_res_lnbwd("proj_in_bwd_ln", dp, win[l], dz1, xh_p, rs_p, g_p + tok, 1024, 1792)
        if l > 0:
            scatter(sib, dzp)
        dz2, dz2b = dzp, dzpb
        grad_x = dzp
    after = grad_x
    while in_flight:
        after = scatter_finish(in_flight.pop(0), after)
    wt, mt, vt = (jnp.transpose(a, (2, 0, 1)) for a in (big_w[0], big_m[0], big_v[0]))
    ntail = wt.shape[0] % LANE
    tails = [jnp.where(ci == 0, jnp.concatenate([own[:, -ntail:], sib[:, -ntail:]]),
                       jnp.concatenate([sib[:, -ntail:], own[:, -ntail:]])).T for own, sib in w_in_halves]
    adam_t = _adamw_tail("adamw_w_in_tail", wt, mt, vt, jnp.stack(tails, axis=1),
                         _adamw_colmajor("adamw_w_in", wt, mt, vt, w_in_halves, csel))
    adam_out[0] = [jnp.transpose(r, (1, 2, 0)) for r in adam_t]
    small["ln_in_g"], small["ln_in_b"] = dg, db
    rb_pad = 3 * LANE
    pieces = [small["loss"].reshape(-1), jnp.zeros((LANE - 1,), F32), small["ln_in_g"].reshape(-1),
              small["ln_in_b"].reshape(-1)]
    for l in range(DEPTH):
        rb = jnp.pad(small[("rel_bias", l)], ((0, 0), (0, rb_pad - (2 * REL_CLIP + 1))))
        pieces += [rb.reshape(-1), small[("gla_w_lr", l)].reshape(-1), small[("gla_b_lr", l)].reshape(-1),
                   small[("gla_norm_g", l)].reshape(-1), small[("ln1_g", l)].reshape(-1),
                   small[("ln1_b", l)].reshape(-1), small[("ln2_g", l)].reshape(-1), small[("ln2_b", l)].reshape(-1)]
    sizes = [pc.shape[0] for pc in pieces]
    packed = jnp.concatenate(pieces)
    padn = (-packed.shape[0]) % (8 * LANE)
    packed = jnp.concatenate([packed, jnp.zeros((padn,), F32)]).reshape(-1, LANE)
    red = _small_allreduce(packed, after).reshape(-1)

    parts, pos = [], 0
    for sz in sizes:
        parts.append(red[pos:pos + sz])
        pos += sz
    loss = parts[0][0]
    g_ln_in_g, g_ln_in_b = parts[2], parts[3]
    per = 8
    g_rel = jnp.stack([parts[4 + per * l].reshape(ATTN_HEADS, rb_pad)[:, :2 * REL_CLIP + 1] for l in range(DEPTH)])
    g_wlr_full = jnp.stack([parts[5 + per * l].reshape(GATE_RANK, half) for l in range(DEPTH)])
    g_wlr = lax.dynamic_slice_in_dim(g_wlr_full, chip * (half // 4), half // 4, axis=2)
    g_blr = jnp.stack([parts[6 + per * l] for l in range(DEPTH)])
    g_gn = jnp.stack([parts[7 + per * l] for l in range(DEPTH)])
    g_ln1g = jnp.stack([parts[8 + per * l] for l in range(DEPTH)])
    g_ln1b = jnp.stack([parts[9 + per * l] for l in range(DEPTH)])
    g_ln2g = jnp.stack([parts[10 + per * l] for l in range(DEPTH)])
    g_ln2b = jnp.stack([parts[11 + per * l] for l in range(DEPTH)])

    grads = [g_ln_in_g, g_ln_in_b, None, g_rel, g_wlr, g_blr, g_gn, None, None, g_ln1g, g_ln1b, None, None,
             g_ln2g, g_ln2b]
    ws = [ln_in_g, ln_in_b, w_in, rel_bias, gla_w_lr, gla_b_lr, gla_norm_g, w_branch, w_out, ln1_g, ln1_b,
          w_up, w_down, ln2_g, ln2_b]
    ms = [m_ln_in_g, m_ln_in_b, m_w_in, m_rel_bias, m_gla_w_lr, m_gla_b_lr, m_gla_norm_g, m_w_branch, m_w_out,
          m_ln1_g, m_ln1_b, m_w_up, m_w_down, m_ln2_g, m_ln2_b]
    vs = [v_ln_in_g, v_ln_in_b, v_w_in, v_rel_bias, v_gla_w_lr, v_gla_b_lr, v_gla_norm_g, v_w_branch, v_w_out,
          v_ln1_g, v_ln1_b, v_w_up, v_w_down, v_ln2_g, v_ln2_b]

    deltas, new_ms, new_vs = [None] * 15, [None] * 15, [None] * 15
    big_idx = [2, 7, 8, 11, 12]
    for i, res in zip(big_idx, adam_out):
        shp = ws[i].shape
        grads[i], deltas[i], new_ms[i], new_vs[i] = (r.reshape(shp) for r in res)
    small_idx = [i for i in range(15) if i not in big_idx]

    def pack(arrs):
        flat_ = jnp.concatenate([arrs[i].reshape(-1) for i in small_idx])
        pad_ = (-flat_.shape[0]) % (8 * LANE)
        return jnp.concatenate([flat_, jnp.ones((pad_,), F32)]).reshape(-1, LANE)

    dl, nm, nv = _adamw("adamw_small", pack(ws), pack(grads), pack(ms), pack(vs))
    pos = 0
    for i in small_idx:
        sz = int(np.prod(ws[i].shape))
        deltas[i] = dl.reshape(-1)[pos:pos + sz].reshape(ws[i].shape)
        new_ms[i] = nm.reshape(-1)[pos:pos + sz].reshape(ws[i].shape)
        new_vs[i] = nv.reshape(-1)[pos:pos + sz].reshape(ws[i].shape)
        pos += sz

    return (loss, grad_x[None], *grads, *deltas, *new_ms, *new_vs)
```

```python
import functools

import numpy as np
import jax
import jax.numpy as jnp
from jax import lax
from jax.experimental import pallas as pl
from jax.experimental.pallas import tpu as pltpu

F32 = jnp.float32
BF16 = jnp.bfloat16
MXU_DTYPE = BF16
HI = lax.Precision.HIGHEST

DEPTH = 2
CHUNK = 64
N_BRANCH = 3
ATTN_HEADS = 8
ATTN_LEFT = 8
REL_CLIP = 2 * CHUNK
LIN_HEADS = 4
GATE_RANK = 16
GATE_NORM = 16.0
LN_EPS = 1e-5
NEG_INF = -1e30
ALPHA = (2 * DEPTH) ** 0.25
ADAM_LR, ADAM_B1, ADAM_B2, ADAM_EPS, ADAM_WD, ADAM_STEP = 0.001, 0.9, 0.999, 1e-08, 0.01, 10

LANE = 128
VMEM_LIMIT = 56 << 20
QB = 256
KW = 3 * QB
LB = 256
MESH_AXES = ("x", "y", "c")
DEV = pl.DeviceIdType.MESH


def _cp(sem):
    return pltpu.CompilerParams(dimension_semantics=sem, vmem_limit_bytes=VMEM_LIMIT)


def _mx(v):
    return v.astype(MXU_DTYPE)


def _dot(a, b):
    return jnp.dot(_mx(a), _mx(b), preferred_element_type=F32)


def _dot_nt(a, b):
    return lax.dot_general(_mx(a), _mx(b), (((1,), (1,)), ((), ())), preferred_element_type=F32)


def _dot_tn(a, b):
    return lax.dot_general(_mx(a), _mx(b), (((0,), (0,)), ((), ())), preferred_element_type=F32)


def _dot_hi(a, b):
    return jnp.dot(a, b, precision=HI, preferred_element_type=F32)


def _sigmoid(v):
    return 1.0 / (1.0 + jnp.exp(-v))


def _sds(shape, dtype):
    return jax.ShapeDtypeStruct(shape, dtype)


def _mm(name, a, b, tm, tn, nt=False, out_dtype=F32):
    batched = a.ndim == 3
    m, k = a.shape[-2:]
    n = b.shape[-2] if nt else b.shape[-1]
    tm, tn = min(tm, m), min(tn, n)

    def body(a_ref, b_ref, o_ref):
        f = _dot_nt if nt else _dot
        o_ref[...] = f(a_ref[...], b_ref[...]).astype(o_ref.dtype)

    rows_inner = (n // tn) * m < (m // tm) * n

    def ij(u, v):
        return (v, u) if rows_inner else (u, v)

    if batched:
        nb = a.shape[0]
        grid = (nb,) + ij(m // tm, n // tn)
        a_spec = pl.BlockSpec((None, tm, k), lambda g, u, v: (g, ij(u, v)[0], 0))
        b_spec = (pl.BlockSpec((None, tn, k), lambda g, u, v: (g, ij(u, v)[1], 0)) if nt
                  else pl.BlockSpec((None, k, tn), lambda g, u, v: (g, 0, ij(u, v)[1])))
        o_spec = pl.BlockSpec((None, tm, tn), lambda g, u, v: (g,) + ij(u, v))
        out_shape = _sds((nb, m, n), out_dtype)
        sem = ("parallel", "parallel", "parallel")
    else:
        grid = ij(m // tm, n // tn)
        a_spec = pl.BlockSpec((tm, k), lambda u, v: (ij(u, v)[0], 0))
        b_spec = (pl.BlockSpec((tn, k), lambda u, v: (ij(u, v)[1], 0)) if nt
                  else pl.BlockSpec((k, tn), lambda u, v: (0, ij(u, v)[1])))
        o_spec = pl.BlockSpec((tm, tn), lambda u, v: ij(u, v))
        out_shape = _sds((m, n), out_dtype)
        sem = ("parallel", "parallel")
    return pl.pallas_call(body, name=name, grid=grid, in_specs=[a_spec, b_spec], out_specs=o_spec,
                          out_shape=out_shape, compiler_params=_cp(sem))(a, b)


def _mm_tn(name, a, b, tm, tn, shard=None):
    batched = a.ndim == 3
    k, m = a.shape[-2:]
    n = b.shape[-1]
    tm, tn = min(tm, m), min(tn, n)

    def body(a_ref, b_ref, o_ref):
        o_ref[...] = lax.dot_general(_mx(a_ref[...]), _mx(b_ref[...]), (((0,), (0,)), ((), ())),
                                     preferred_element_type=F32)

    if batched:
        nb = a.shape[0]
        grid = (nb, m // tm, n // tn)
        a_spec = pl.BlockSpec((None, k, tm), lambda g, i, j: (g, 0, i))
        b_spec = pl.BlockSpec((None, k, tn), lambda g, i, j: (g, 0, j))
        if shard == "rows":
            assert 4 * tm == m
            o_spec = pl.BlockSpec((None, tm, tn), lambda g, i, j: (i, g, j))
            out_shape = _sds((4, nb * tm, n), F32)
        else:
            o_spec = pl.BlockSpec((None, tm, tn), lambda g, i, j: (g, i, j))
            out_shape = _sds((nb, m, n), F32)
    else:
        grid = (m // tm, n // tn)
        a_spec = pl.BlockSpec((k, tm), lambda i, j: (0, i))
        b_spec = pl.BlockSpec((k, tn), lambda i, j: (0, j))
        if shard == "cols":
            per = n // 4 // tn
            o_spec = pl.BlockSpec((None, tm, tn), lambda i, j: (j // per, i, j % per))
            out_shape = _sds((4, m, n // 4), F32)
        else:
            o_spec = pl.BlockSpec((tm, tn), lambda i, j: (i, j))
            out_shape = _sds((m, n), F32)
    return pl.pallas_call(body, name=name, grid=grid, in_specs=[a_spec, b_spec], out_specs=o_spec,
                          out_shape=out_shape, compiler_params=_cp(("parallel",) * len(grid)))(a, b)


def _ln_rows(y, g, b):
    mu = jnp.mean(y, axis=-1, keepdims=True)
    yc = y - mu
    var = jnp.mean(yc * yc, axis=-1, keepdims=True)
    rs = lax.rsqrt(var + LN_EPS)
    xh = yc * rs
    return xh * g + b, xh, rs


def _ln_in(x, g, b, tm=256):
    t, d = x.shape

    def body(x_ref, g_ref, b_ref, o_ref, ob_ref, xh_ref, rs_ref):
        o, xh, rs = _ln_rows(x_ref[...], g_ref[...], b_ref[...])
        o_ref[...] = o
        ob_ref[...] = o.astype(BF16)
        xh_ref[...] = xh
        rs_ref[...] = rs

    row = pl.BlockSpec((tm, d), lambda i: (i, 0))
    vec = pl.BlockSpec((1, d), lambda i: (0, 0))
    return pl.pallas_call(
        body, name="ln_in", grid=(t // tm,), in_specs=[row, vec, vec],
        out_specs=[row, row, row, pl.BlockSpec((tm, 1), lambda i: (i, 0))],
        out_shape=[_sds((t, d), F32), _sds((t, d), BF16), _sds((t, d), F32), _sds((t, 1), F32)],
        compiler_params=_cp(("parallel",)))(x, g, b)


def _mm_res_ln(name, a, w, res, g, b, tm, relu2):
    t, k = a.shape
    d = w.shape[1]

    def body(a_ref, w_ref, r_ref, g_ref, b_ref, o_ref, ob_ref, xh_ref, rs_ref, *act_ref):
        av = a_ref[...]
        if relu2:
            av = jnp.square(jnp.maximum(av, 0.0))
            act_ref[0][...] = av.astype(BF16)
        y = ALPHA * r_ref[...] + _dot(av, w_ref[...])
        o, xh, rs = _ln_rows(y, g_ref[...], b_ref[...])
        o_ref[...] = o
        ob_ref[...] = o.astype(BF16)
        xh_ref[...] = xh
        rs_ref[...] = rs

    row = pl.BlockSpec((tm, d), lambda i: (i, 0))
    vec = pl.BlockSpec((1, d), lambda i: (0, 0))
    arow = pl.BlockSpec((tm, k), lambda i: (i, 0))
    out_specs = [row, row, row, pl.BlockSpec((tm, 1), lambda i: (i, 0))]
    out_shape = [_sds((t, d), F32), _sds((t, d), BF16), _sds((t, d), F32), _sds((t, 1), F32)]
    if relu2:
        out_specs.append(arow)
        out_shape.append(_sds((t, k), BF16))
    return pl.pallas_call(
        body, name=name, grid=(t // tm,),
        in_specs=[arow, pl.BlockSpec((k, d), lambda i: (0, 0)), row, vec, vec],
        out_specs=out_specs, out_shape=out_shape, compiler_params=_cp(("parallel",)))(a, w, res, g, b)


def _merge_fwd(bo, wb, p, gate_off, tm=512, tn=512):
    _, t, d = bo.shape
    gb = gate_off // tn

    def body(bo_ref, wb_ref, g0, g1, g2, proj_ref, m_ref):
        acc = None
        for n, g_ref in enumerate((g0, g1, g2)):
            pr = _dot(bo_ref[n], wb_ref[n])
            proj_ref[n] = pr
            term = _sigmoid(g_ref[...]) * pr
            acc = term if acc is None else acc + term
        m_ref[...] = acc.astype(BF16)

    gspecs = [pl.BlockSpec((tm, tn), functools.partial(lambda i, j, n: (i, gb + n * (d // tn) + j), n=n))
              for n in range(3)]
    return pl.pallas_call(
        body, name="merge_fwd", grid=(t // tm, d // tn),
        in_specs=[pl.BlockSpec((3, tm, d), lambda i, j: (0, i, 0)),
                  pl.BlockSpec((3, d, tn), lambda i, j: (0, 0, j))] + gspecs,
        out_specs=[pl.BlockSpec((3, tm, tn), lambda i, j: (0, i, j)), pl.BlockSpec((tm, tn), lambda i, j: (i, j))],
        out_shape=[_sds((3, t, d), F32), _sds((t, d), BF16)],
        compiler_params=_cp(("parallel", "parallel")))(bo, wb, p, p, p)


def _merge_bwd(dz, wout, proj, p, gate_off, tm=512, tn=512):
    t, d = dz.shape
    gb = gate_off // tn

    def body(dz_ref, w_ref, proj_ref, g0, g1, g2, dproj_ref, dgl_ref):
        dm = _dot_nt(dz_ref[...], w_ref[...])
        for n, g_ref in enumerate((g0, g1, g2)):
            s = _sigmoid(g_ref[...])
            dproj_ref[n] = (dm * s).astype(BF16)
            dgl_ref[n] = (dm * proj_ref[n] * (s * (1.0 - s))).astype(BF16)

    gspecs = [pl.BlockSpec((tm, tn), functools.partial(lambda i, j, n: (i, gb + n * (d // tn) + j), n=n))
              for n in range(3)]
    dproj, dgl = pl.pallas_call(
        body, name="merge_bwd", grid=(t // tm, d // tn),
        in_specs=[pl.BlockSpec((tm, d), lambda i, j: (i, 0)), pl.BlockSpec((tn, d), lambda i, j: (j, 0)),
                  pl.BlockSpec((3, tm, tn), lambda i, j: (0, i, j))] + gspecs,
        out_specs=[pl.BlockSpec((3, tm, tn), lambda i, j: (0, i, j)),
                   pl.BlockSpec((3, tm, tn), lambda i, j: (0, i, j))],
        out_shape=[_sds((3, t, d), BF16), _sds((3, t, d), BF16)],
        compiler_params=_cp(("parallel", "parallel")))(dz, wout, proj, p, p, p)
    return dproj, dgl


def _mm_nt_relu2_bwd(dz, wdown, u, tm=512, tn=1024):
    t, d = dz.shape
    f = wdown.shape[0]

    def body(dz_ref, w_ref, u_ref, du_ref):
        da = _dot_nt(dz_ref[...], w_ref[...])
        du_ref[...] = (da * (2.0 * jnp.maximum(u_ref[...], 0.0))).astype(BF16)

    return pl.pallas_call(
        body, name="mlp_down_bwd", grid=(t // tm, f // tn),
        in_specs=[pl.BlockSpec((tm, d), lambda i, j: (i, 0)), pl.BlockSpec((tn, d), lambda i, j: (j, 0)),
                  pl.BlockSpec((tm, tn), lambda i, j: (i, j))],
        out_specs=pl.BlockSpec((tm, tn), lambda i, j: (i, j)), out_shape=_sds((t, f), BF16),
        compiler_params=_cp(("parallel", "parallel")))(dz, wdown, u)


def _ln_bwd_rows(dx, xh, rs, g):
    dxh = dx * g
    m1 = jnp.mean(dxh, axis=-1, keepdims=True)
    m2 = jnp.mean(dxh * xh, axis=-1, keepdims=True)
    return rs * (dxh - m1 - xh * m2)


def _mm_nt_res_lnbwd(name, a, w, dres, xh, rs, g, tm, tk):
    t, k = a.shape
    d = w.shape[0]
    nk = k // tk

    def body(a_ref, w_ref, dr_ref, xh_ref, rs_ref, g_ref, dz_ref, dzb_ref, dg_ref, db_ref, acc_ref):
        i, kk = pl.program_id(0), pl.program_id(1)

        @pl.when(kk == 0)
        def _():
            acc_ref[...] = ALPHA * dr_ref[...]

        acc_ref[...] += _dot_nt(a_ref[...], w_ref[...])

        @pl.when(jnp.logical_and(i == 0, kk == 0))
        def _():
            dg_ref[...] = jnp.zeros_like(dg_ref)
            db_ref[...] = jnp.zeros_like(db_ref)

        @pl.when(kk == nk - 1)
        def _():
            dx = acc_ref[...]
            xhv = xh_ref[...]
            dz = _ln_bwd_rows(dx, xhv, rs_ref[...], g_ref[...])
            dz_ref[...] = dz
            dzb_ref[...] = dz.astype(BF16)
            dg_ref[...] += jnp.sum(dx * xhv, axis=0, keepdims=True)
            db_ref[...] += jnp.sum(dx, axis=0, keepdims=True)

    row = pl.BlockSpec((tm, d), lambda i, kk: (i, 0))
    vec = pl.BlockSpec((1, d), lambda i, kk: (0, 0))
    return pl.pallas_call(
        body, name=name, grid=(t // tm, nk),
        in_specs=[pl.BlockSpec((tm, tk), lambda i, kk: (i, kk)), pl.BlockSpec((d, tk), lambda i, kk: (0, kk)),
                  row, row, pl.BlockSpec((tm, 1), lambda i, kk: (i, 0)), vec],
        out_specs=[row, row, vec, vec],
        out_shape=[_sds((t, d), F32), _sds((t, d), BF16), _sds((1, d), F32), _sds((1, d), F32)],
        scratch_shapes=[pltpu.VMEM((tm, d), F32)],
        compiler_params=_cp(("arbitrary", "arbitrary")))(a, w, dres, xh, rs, g)


def _loss_ln_bwd(x2, target, xh, rs, g, tm=256):
    t, d = x2.shape

    def body(x_ref, t_ref, xh_ref, rs_ref, g_ref, loss_ref, dz_ref, dzb_ref, dg_ref, db_ref):
        @pl.when(pl.program_id(0) == 0)
        def _():
            loss_ref[...] = jnp.zeros_like(loss_ref)
            dg_ref[...] = jnp.zeros_like(dg_ref)
            db_ref[...] = jnp.zeros_like(db_ref)

        err = x_ref[...] - t_ref[...]
        per_row = jnp.mean(err * err, axis=-1, keepdims=True)
        loss_ref[...] += 0.5 * jnp.sum(per_row, axis=0, keepdims=True)
        dx = err * (1.0 / d)
        xhv = xh_ref[...]
        dz = _ln_bwd_rows(dx, xhv, rs_ref[...], g_ref[...])
        dz_ref[...] = dz
        dzb_ref[...] = dz.astype(BF16)
        dg_ref[...] += jnp.sum(dx * xhv, axis=0, keepdims=True)
        db_ref[...] += jnp.sum(dx, axis=0, keepdims=True)

    row = pl.BlockSpec((tm, d), lambda i: (i, 0))
    vec = pl.BlockSpec((1, d), lambda i: (0, 0))
    return pl.pallas_call(
        body, name="loss_ln_bwd", grid=(t // tm,),
        in_specs=[row, row, row, pl.BlockSpec((tm, 1), lambda i: (i, 0)), vec],
        out_specs=[pl.BlockSpec((1, LANE), lambda i: (0, 0)), row, row, vec, vec],
        out_shape=[_sds((1, LANE), F32), _sds((t, d), F32), _sds((t, d), BF16), _sds((1, d), F32),
                   _sds((1, d), F32)],
        compiler_params=_cp(("arbitrary",)))(x2, target, xh, rs, g)


HPA_FWD = 8
HPA = 2


STRIP = 16


def _attn_scores(q_ref, k_refs, bias_ref, i, dh, hh):
    cols = pl.ds(hh * dh, dh)
    q = q_ref[:, cols] * (dh ** -0.5)
    k = jnp.concatenate([r[:, cols] for r in k_refs], axis=0)
    s = _dot_nt(q, k)
    before_start = lax.broadcasted_iota(jnp.int32, (STRIP, KW), 1) < (2 - i) * QB
    strips = []
    for r in range(0, QB, STRIP):
        ss = jnp.where(before_start, NEG_INF, s[r:r + STRIP] + bias_ref[hh, r:r + STRIP])
        e = jnp.exp(ss - jnp.max(ss, axis=-1, keepdims=True))
        strips.append(e / jnp.sum(e, axis=-1, keepdims=True))
    return q, k, strips


def _attn_specs(dh, off, hp):
    w = hp * dh
    qcol, kcol, vcol = off["aq"] // w, off["ak"] // w, off["av"] // w
    q_spec = pl.BlockSpec((QB, w), lambda g, i: (i, qcol + g))
    k_specs = [pl.BlockSpec((QB, w), functools.partial(lambda g, i, j: (jnp.maximum(i - 2 + j, 0), kcol + g), j=j))
               for j in range(3)]
    v_specs = [pl.BlockSpec((QB, w), functools.partial(lambda g, i, j: (jnp.maximum(i - 2 + j, 0), vcol + g), j=j))
               for j in range(3)]
    bias_spec = pl.BlockSpec((hp, QB, KW), lambda g, i: (g, 0, 0))
    return q_spec, k_specs, v_specs, bias_spec


def _attn_fwd(p, bias, d, off):
    t = p.shape[0]
    dh = d // ATTN_HEADS

    def body(q_ref, k0, k1, k2, v0, v1, v2, bias_ref, o_ref):
        for hh in range(HPA_FWD):
            cols = pl.ds(hh * dh, dh)
            _, _, strips = _attn_scores(q_ref, (k0, k1, k2), bias_ref, pl.program_id(1), dh, hh)
            pr = jnp.concatenate([_mx(ps) for ps in strips], axis=0)
            v = jnp.concatenate([v0[:, cols], v1[:, cols], v2[:, cols]], axis=0)
            o_ref[:, cols] = _dot(pr, v).astype(o_ref.dtype)

    q_spec, k_specs, v_specs, bias_spec = _attn_specs(dh, off, HPA_FWD)
    return pl.pallas_call(
        body, name="attn_fwd", grid=(ATTN_HEADS // HPA_FWD, t // QB),
        in_specs=[q_spec] + k_specs + v_specs + [bias_spec],
        out_specs=pl.BlockSpec((None, QB, HPA_FWD * dh), lambda g, i: (0, i, g)),
        out_shape=_sds((N_BRANCH, t, d), BF16),
        compiler_params=_cp(("parallel", "parallel")))(p, p, p, p, p, p, p, bias)


def _attn_bwd(p, bias, do, d, off):
    t = p.shape[0]
    dh = d // ATTN_HEADS
    tp = t + 2 * QB

    def body(q_ref, k0, k1, k2, v0, v1, v2, bias_ref, do_ref, dq_ref, dk_ref, dv_ref, dbias_ref):
        i = pl.program_id(1)

        @pl.when(i == 0)
        def _():
            dk_ref[...] = jnp.zeros_like(dk_ref)
            dv_ref[...] = jnp.zeros_like(dv_ref)
            dbias_ref[...] = jnp.zeros_like(dbias_ref)

        rows = pl.ds(pl.multiple_of(i * QB, QB), KW)
        for hh in range(HPA):
            cols = pl.ds(hh * dh, dh)
            q, k, strips = _attn_scores(q_ref, (k0, k1, k2), bias_ref, i, dh, hh)
            v = jnp.concatenate([v0[:, cols], v1[:, cols], v2[:, cols]], axis=0)
            dov = do_ref[:, cols]
            dp = _dot_nt(dov, v)
            ds_strips = []
            for n, ps in enumerate(strips):
                r = n * STRIP
                dps = dp[r:r + STRIP]
                dss = ps * (dps - jnp.sum(ps * dps, axis=-1, keepdims=True))
                dbias_ref[hh, r:r + STRIP] += dss
                ds_strips.append(_mx(dss))
            ds = jnp.concatenate(ds_strips, axis=0)
            pr = jnp.concatenate([_mx(ps) for ps in strips], axis=0)
            dq_ref[:, cols] = (_dot(ds, k) * (dh ** -0.5)).astype(dq_ref.dtype)
            dk_ref[rows, cols] += _dot_tn(ds, q)
            dv_ref[rows, cols] += _dot_tn(pr, dov)

    q_spec, k_specs, v_specs, bias_spec = _attn_specs(dh, off, HPA)
    row_spec = pl.BlockSpec((QB, HPA * dh), lambda g, i: (i, g))
    acc_spec = pl.BlockSpec((tp, HPA * dh), lambda g, i: (0, g))
    return pl.pallas_call(
        body, name="attn_bwd", grid=(ATTN_HEADS // HPA, t // QB),
        in_specs=[q_spec] + k_specs + v_specs + [bias_spec,
                                                 pl.BlockSpec((None, QB, HPA * dh), lambda g, i: (0, i, g))],
        out_specs=[row_spec, acc_spec, acc_spec, bias_spec],
        out_shape=[_sds((t, d), BF16), _sds((tp, d), F32), _sds((tp, d), F32),
                   _sds((ATTN_HEADS, QB, KW), F32)],
        compiler_params=_cp(("parallel", "arbitrary")))(p, p, p, p, p, p, p, bias, do)


def _onehot_mm(name, a, b):
    def body(a_ref, b_ref, o_ref):
        o_ref[...] = _dot_hi(a_ref[...], b_ref[...])

    return pl.pallas_call(body, name=name, out_shape=_sds((a.shape[0], b.shape[1]), F32),
                          compiler_params=pltpu.CompilerParams(vmem_limit_bytes=VMEM_LIMIT))(a, b)


def _diag_index():
    ii, jj = np.arange(CHUNK)[:, None], np.arange(CHUNK)[None, :]
    return (ii - jj + CHUNK - 1).reshape(-1)


def _bias_expand(rel_bias):
    h = rel_bias.shape[0]
    nq, nk, shift = QB // CHUNK, KW // CHUNK, (2 * QB) // CHUNK
    nbin, ndc = 3 * LANE, 4
    rb = jnp.pad(rel_bias, ((0, 0), (0, nbin - rel_bias.shape[1])))
    win = np.clip(CHUNK * np.arange(ndc)[:, None] + np.arange(LANE)[None, :] - (CHUNK - 1), -REL_CLIP, REL_CLIP)
    sel = (jnp.arange(nbin)[:, None] == jnp.asarray((win + REL_CLIP).reshape(1, -1))).astype(F32)
    windows = _onehot_mm("bias_windows", rb, sel)
    diag_t = (jnp.arange(LANE)[:, None] == jnp.asarray(_diag_index().reshape(1, -1))).astype(F32)
    blocks = _onehot_mm("bias_blocks", windows.reshape(h * ndc, LANE), diag_t).reshape(h, ndc, CHUNK, CHUNK)
    off_band = jnp.full((h, CHUNK, CHUNK), NEG_INF, F32)
    rows = []
    for ic in range(nq):
        dcs = [ic - jc + shift for jc in range(nk)]
        rows.append(jnp.concatenate([blocks[:, min(dc, ndc - 1)] if 0 <= dc <= ATTN_LEFT else off_band
                                     for dc in dcs], axis=2))
    return jnp.concatenate(rows, axis=1)


def _bias_reduce(dbias):
    h = dbias.shape[0]
    nq, nk = QB // CHUNK, KW // CHUNK
    nbin = 3 * LANE
    blocks = dbias.reshape(h, nq, CHUNK, nk, CHUNK).transpose(0, 1, 3, 2, 4).reshape(h * nq * nk, CHUNK * CHUNK)
    diag = (jnp.asarray(_diag_index().reshape(-1, 1)) == jnp.arange(LANE)[None, :]).astype(F32)
    ic = np.arange(nq)[:, None, None]
    jc = np.arange(nk)[None, :, None]
    dl = np.arange(LANE)[None, None, :] - (CHUNK - 1)
    rel = np.clip(CHUNK * (ic - jc + (2 * QB) // CHUNK) + dl, -REL_CLIP, REL_CLIP) + REL_CLIP
    bins = (jnp.asarray(rel.reshape(-1, 1)) == jnp.arange(nbin)[None, :]).astype(F32)

    diags = _onehot_mm("bias_diag_sums", blocks, diag)
    out = _onehot_mm("bias_bin_sums", diags.reshape(h, nq * nk * LANE), bins)
    return out[:, :2 * REL_CLIP + 1]


def _chunk_masks():
    r = lax.broadcasted_iota(jnp.int32, (LB, LB), 0)
    c = lax.broadcasted_iota(jnp.int32, (LB, LB), 1)
    return (r // CHUNK) == (c // CHUNK), r >= c, r <= c


def _chunks(a):
    return [a[c * CHUNK:(c + 1) * CHUNK] for c in range(LB // CHUNK)]


def _per_chunk(a, f):
    return jnp.concatenate([jnp.broadcast_to(f(c), c.shape) for c in _chunks(a)], axis=0)


def _dot_sel(sel, x):
    def top(v):
        return lax.bitcast_convert_type(lax.bitcast_convert_type(v, jnp.int32) & jnp.int32(-65536), F32)

    hi = top(x)
    mid = top(x - hi)
    lo = (x - hi) - mid
    d = functools.partial(jnp.dot, sel.astype(jnp.bfloat16), preferred_element_type=F32)
    return d(hi.astype(jnp.bfloat16)) + d(mid.astype(jnp.bfloat16)) + d(lo.astype(jnp.bfloat16))


def _lin_block(gla, q, k, v, aux):
    dk = q.shape[-1]
    same, low, up = _chunk_masks()
    ones = same.astype(F32)
    if gla:
        glr, wlr, blr = aux
        q = q * (dk ** -0.5)
        pre = _dot(glr, wlr) + blr
        log_a = (jnp.minimum(pre, 0.0) - jnp.log(1.0 + jnp.exp(-jnp.abs(pre)))) / GATE_NORM
        b = _dot_sel(jnp.where(low, ones, 0.0), log_a)
        lastb = _per_chunk(b, lambda c: c[CHUNK - 1:])
    else:
        cs, sn, lg = aux
        pre = None
        half = dk // 2
        q = q * cs + pltpu.roll(q, half, 1) * sn
        k = (k * cs + pltpu.roll(k, half, 1) * sn) * (dk ** -0.5)
        pos = (lax.broadcasted_iota(jnp.int32, (LB, dk), 0) % CHUNK).astype(F32) + 1.0
        b = pos * lg
        lastb = jnp.broadcast_to(float(CHUNK) * lg, b.shape)
    eb, enb, el, dec = jnp.exp(b), jnp.exp(-b), jnp.exp(lastb - b), jnp.exp(lastb)
    qf, kf, qb, kb, kl = q * eb, k * enb, q * enb, k * eb, k * el
    s = jnp.where(same, jnp.where(low, _dot_nt(qf, kf), _dot_nt(qb, kb)), 0.0)
    return dict(pre=pre, eb=eb, enb=enb, el=el, dec=dec, qf=qf, kf=kf, qb=qb, kb=kb, kl=kl, s=s,
                same=same, low=low, up=up, ones=ones)


def _lin_norm_gate(gla, o, gate, gn):
    sg = _sigmoid(gate)
    silu = gate * sg
    if gla:
        r = lax.rsqrt(jnp.mean(o * o, axis=-1, keepdims=True) + LN_EPS)
        hn = o * r
        return silu * (hn * gn), (sg, silu, r, hn)
    mu = jnp.mean(o, axis=-1, keepdims=True)
    oc = o - mu
    r = lax.rsqrt(jnp.mean(oc * oc, axis=-1, keepdims=True) + LN_EPS)
    hn = oc * r
    return silu * hn, (sg, silu, r, hn)


HPS = 4


def _lin_specs(gla, dk, dv, off, rev, nb):
    pre = "g" if gla else "r"
    wk, wv = HPS * dk, HPS * dv
    qc, kc, vc, gc = (off[pre + "q"] // wk, off[pre + "k"] // wk, off[pre + "v"] // wv, off[pre + "g"] // wv)

    def blk(i):
        return nb - 1 - i if rev else i

    specs = [pl.BlockSpec((LB, wk), lambda g, i: (blk(i), qc + g)),
             pl.BlockSpec((LB, wk), lambda g, i: (blk(i), kc + g)),
             pl.BlockSpec((LB, wv), lambda g, i: (blk(i), vc + g)),
             pl.BlockSpec((LB, wv), lambda g, i: (blk(i), gc + g))]
    if gla:
        specs += [pl.BlockSpec((LB, LANE), lambda g, i: (blk(i), off["glr"] // LANE)),
                  pl.BlockSpec((LANE, wk), lambda g, i: (0, g)),
                  pl.BlockSpec((1, wk), lambda g, i: (0, g)),
                  pl.BlockSpec((1, dv), lambda g, i: (0, 0))]
    else:
        specs += [pl.BlockSpec((LB, dk), lambda g, i: (blk(i), 0)),
                  pl.BlockSpec((LB, dk), lambda g, i: (blk(i), 0)),
                  pl.BlockSpec((HPS, 1, dk), lambda g, i: (g, 0, 0))]
    return specs, blk


def _lin_aux(gla, refs, rows, hh, dk):
    if gla:
        glr_ref, wlr_ref, blr_ref, gn_ref = refs
        kcols = pl.ds(hh * dk, dk)
        return (glr_ref[rows, :], wlr_ref[:, kcols], blr_ref[:, kcols]), gn_ref[...]
    cs_ref, sn_ref, lg_ref = refs
    return (cs_ref[rows, :], sn_ref[rows, :], lg_ref[hh]), None


def _lin_fwd(gla, p, aux_arrays, d, off, branches, slot):
    t = p.shape[0]
    dk, dv = d // (2 * LIN_HEADS), d // LIN_HEADS
    nb, cb = t // LB, LB // CHUNK
    naux = len(aux_arrays)

    def body(*refs):
        q_ref, k_ref, v_ref, g_ref = refs[:4]
        aux_refs = refs[4:4 + naux]
        o_ref, bo_ref, st_out_ref, st_ref = refs[5 + naux:]

        @pl.when(pl.program_id(1) == 0)
        def _():
            st_ref[...] = jnp.zeros_like(st_ref)

        rows = slice(None)
        for hh in range(HPS):
            kcols, vcols = pl.ds(hh * dk, dk), pl.ds(hh * dv, dv)
            aux, gn = _lin_aux(gla, aux_refs, rows, hh, dk)
            v = v_ref[:, vcols]
            blk = _lin_block(gla, q_ref[:, kcols], k_ref[:, kcols], v, aux)
            st = st_ref[hh]
            inter = []
            for c, (qf, kl, dec, vc) in enumerate(zip(_chunks(blk["qf"]), _chunks(blk["kl"]), _chunks(blk["dec"]),
                                                      _chunks(v))):
                st_out_ref[hh, c] = st
                inter.append(_dot_nt(qf, st))
                st = st * dec[:1] + _dot_tn(vc, kl)
            st_ref[hh] = st
            o = _dot(blk["s"], v) + jnp.concatenate(inter, axis=0)
            o_ref[:, vcols] = o
            out, _ = _lin_norm_gate(gla, o, g_ref[:, vcols], gn)
            bo_ref[:, vcols] = out.astype(BF16)

    specs, _ = _lin_specs(gla, dk, dv, off, False, nb)
    orow = pl.BlockSpec((LB, HPS * dv), lambda g, i: (i, g))
    return pl.pallas_call(
        body, name="gla_fwd" if gla else "ret_fwd", grid=(LIN_HEADS // HPS, nb), in_specs=specs + [ANY],
        out_specs=[orow, pl.BlockSpec((None, LB, HPS * dv), lambda g, i: (slot, i, g)),
                   pl.BlockSpec((HPS, cb, dv, dk), lambda g, i: (g, i, 0, 0))],
        out_shape=[_sds((t, d), F32), _sds(branches.shape, BF16), _sds((LIN_HEADS, t // CHUNK, dv, dk), F32)],
        scratch_shapes=[pltpu.VMEM((HPS, dv, dk), F32)], input_output_aliases={4 + naux: 1},
        compiler_params=_cp(("parallel", "arbitrary")))(p, p, p, p, *aux_arrays, branches)


def _lin_bwd(gla, p, aux_arrays, o, states, dbo, slot, d, off):
    t = p.shape[0]
    dk, dv = d // (2 * LIN_HEADS), d // LIN_HEADS
    nb, cb = t // LB, LB // CHUNK
    naux = len(aux_arrays)

    def body(*refs):
        q_ref, k_ref, v_ref, g_ref = refs[:4]
        aux_refs = refs[4:4 + naux]
        o_ref, st_in_ref, dbo_ref = refs[4 + naux:7 + naux]
        outs = refs[7 + naux:]
        dq_ref, dk_ref, dv_ref, dg_ref = outs[:4]
        dst_ref = outs[-1]
        first = pl.program_id(1) == 0

        @pl.when(first)
        def _():
            dst_ref[...] = jnp.zeros_like(dst_ref)

        if gla:
            dpre_ref, dblr_ref, dgn_ref = outs[4:7]

            @pl.when(first)
            def _():
                dblr_ref[...] = jnp.zeros_like(dblr_ref)
                dgn_ref[...] = jnp.zeros_like(dgn_ref)

        rows = slice(None)
        for hh in range(HPS):
            kcols, vcols = pl.ds(hh * dk, dk), pl.ds(hh * dv, dv)
            aux, gn = _lin_aux(gla, aux_refs, rows, hh, dk)
            v = v_ref[:, vcols]
            bk = _lin_block(gla, q_ref[:, kcols], k_ref[:, kcols], v, aux)
            eb, enb, el, dec = bk["eb"], bk["enb"], bk["el"], bk["dec"]
            qf, kf, qb, kb, kl, s = bk["qf"], bk["kf"], bk["qb"], bk["kb"], bk["kl"], bk["s"]
            gate = g_ref[:, vcols]
            dout = dbo_ref[:, vcols]
            _, (sg, silu, r, hn) = _lin_norm_gate(gla, o_ref[:, vcols], gate, gn)
            dsilu = sg * (1.0 + gate * (1.0 - sg))
            if gla:
                y = hn * gn
                dy = dout * silu
                dg_ref[:, vcols] = (dout * y * dsilu).astype(BF16)
                dgn_ref[hh] += jnp.sum(dy * hn, axis=0, keepdims=True)
                dhn = dy * gn
                do = r * (dhn - hn * jnp.mean(dhn * hn, axis=-1, keepdims=True))
            else:
                dhn = dout * silu
                dg_ref[:, vcols] = (dout * hn * dsilu).astype(BF16)
                do = r * (dhn - jnp.mean(dhn, axis=-1, keepdims=True)
                          - hn * jnp.mean(dhn * hn, axis=-1, keepdims=True))
            ds = jnp.where(bk["same"], _dot_nt(do, v), 0.0)
            dsf = jnp.where(bk["low"], ds, 0.0)
            dsb = ds - dsf
            dvv = _dot_tn(s, do)
            dqf = _dot(dsf, kf)
            dkf = _dot_tn(dsf, qf)
            dqb = _dot(dsb, kb)
            dkb = _dot_tn(dsb, qb)
            dst = dst_ref[hh]
            dv_st, dqf_st, dkl_c, ddec_c = [], [], [], []
            parts = zip(reversed(range(cb)), reversed(_chunks(do)), reversed(_chunks(v)), reversed(_chunks(qf)),
                        reversed(_chunks(kl)), reversed(_chunks(dec)))
            for c, do_c, v_c, qf_c, kl_c, dec_c in parts:
                st = st_in_ref[hh, c]
                dv_st.append(_dot_nt(kl_c, dst))
                dkl_c.append(_dot(v_c, dst))
                dqf_st.append(_dot(do_c, st))
                ddec_c.append(jnp.broadcast_to(jnp.sum(dst * st, axis=0, keepdims=True), (CHUNK, dk)))
                dst = dst * dec_c[:1] + _dot_tn(do_c, qf_c)
            dst_ref[hh] = dst

            def cat(pieces):
                return jnp.concatenate(pieces[::-1], axis=0)

            dvv = dvv + cat(dv_st)
            dqf = dqf + cat(dqf_st)
            dkl = cat(dkl_c)
            dq = dqf * eb + dqb * enb
            dkk = dkf * enb + dkb * eb + dkl * el
            dv_ref[:, vcols] = dvv.astype(BF16)
            if gla:
                db = dqf * qf - dkf * kf - dqb * qb + dkb * kb - dkl * kl
                dlast = _per_chunk(dkl * kl, lambda c: jnp.sum(c, axis=0, keepdims=True)) + cat(ddec_c) * dec
                dlog_a = _dot_sel(jnp.where(bk["up"], bk["ones"], 0.0), db) + dlast
                dpre = dlog_a * (1.0 / GATE_NORM) * (1.0 - _sigmoid(bk["pre"]))
                dpre_ref[:, kcols] = dpre
                dblr_ref[hh] += jnp.sum(dpre, axis=0, keepdims=True)
                dq_ref[:, kcols] = (dq * (dk ** -0.5)).astype(BF16)
                dk_ref[:, kcols] = dkk.astype(BF16)
            else:
                cs, sn, _ = aux
                half = dk // 2
                dkk = dkk * (dk ** -0.5)
                dq_ref[:, kcols] = (dq * cs + pltpu.roll(dq * sn, half, 1)).astype(BF16)
                dk_ref[:, kcols] = (dkk * cs + pltpu.roll(dkk * sn, half, 1)).astype(BF16)

    specs, blk = _lin_specs(gla, dk, dv, off, True, nb)
    vrow = pl.BlockSpec((LB, HPS * dv), lambda g, i: (blk(i), g))
    krow = pl.BlockSpec((LB, HPS * dk), lambda g, i: (blk(i), g))
    specs += [vrow, pl.BlockSpec((HPS, cb, dv, dk), lambda g, i: (g, blk(i), 0, 0)),
              pl.BlockSpec((None, LB, HPS * dv), lambda g, i: (slot, blk(i), g))]
    out_specs = [krow, krow, vrow, vrow]
    out_shape = [_sds((t, d // 2), BF16), _sds((t, d // 2), BF16), _sds((t, d), BF16), _sds((t, d), BF16)]
    if gla:
        out_specs += [krow, pl.BlockSpec((HPS, 1, dk), lambda g, i: (g, 0, 0)),
                      pl.BlockSpec((HPS, 1, dv), lambda g, i: (g, 0, 0))]
        out_shape += [_sds((t, d // 2), F32), _sds((LIN_HEADS, 1, dk), F32), _sds((LIN_HEADS, 1, dv), F32)]
    out_specs.append(pl.BlockSpec((HPS, dv, dk), lambda g, i: (g, 0, 0)))
    out_shape.append(_sds((LIN_HEADS, dv, dk), F32))
    res = pl.pallas_call(
        body, name="gla_bwd" if gla else "ret_bwd", grid=(LIN_HEADS // HPS, nb), in_specs=specs,
        out_specs=out_specs, out_shape=out_shape,
        compiler_params=_cp(("parallel", "arbitrary")))(p, p, p, p, *aux_arrays, o, states, dbo)
    return res[:-1]


def _row_tile(rows, cols):
    cap = max(8, (2 << 20) // (4 * cols))
    t = rows
    while t > cap and t % 2 == 0:
        t //= 2
    return t


def _add_half(name, g, t, sel):
    nchip, hr, cols = t.shape
    tr = _row_tile(hr, cols)
    nb = hr // tr

    def body(sel_ref, g_ref, t_ref, o_ref):
        o_ref[...] = g_ref[...] + t_ref[...]

    half = pl.BlockSpec((None, tr, cols), lambda p, i, s: (p, i, 0))
    gs = pltpu.PrefetchScalarGridSpec(
        num_scalar_prefetch=1, grid=(nchip, nb),
        in_specs=[pl.BlockSpec((None, tr, cols), lambda p, i, s: (p, s[0] * nb + i, 0)), half], out_specs=half)
    return pl.pallas_call(body, name=name, grid_spec=gs, out_shape=_sds(t.shape, F32),
                          compiler_params=_cp(("parallel", "parallel")))(sel, g, t)


def _sum_shards(name, h, rcv, sel):
    _, rows, cols = h.shape
    tr = _row_tile(rows, cols)

    def body(sel_ref, h_ref, r0, r1, r2, o_ref):
        o_ref[...] = ((h_ref[...] + r0[...]) + r1[...]) + r2[...]

    rspecs = [pl.BlockSpec((None, tr, cols), functools.partial(lambda i, s, j: (j, i, 0), j=j)) for j in range(3)]
    gs = pltpu.PrefetchScalarGridSpec(
        num_scalar_prefetch=1, grid=(rows // tr,),
        in_specs=[pl.BlockSpec((None, tr, cols), lambda i, s: (s[0], i, 0))] + rspecs,
        out_specs=pl.BlockSpec((tr, cols), lambda i, s: (i, 0)))
    return pl.pallas_call(body, name=name, grid_spec=gs, out_shape=_sds((rows, cols), F32),
                          compiler_params=_cp(("parallel",)))(sel, h, rcv, rcv, rcv)


def _adamw_math(w, g, m, v):
    c1 = 1.0 - ADAM_B1 ** ADAM_STEP
    c2 = 1.0 - ADAM_B2 ** ADAM_STEP
    nm = ADAM_B1 * m + (1.0 - ADAM_B1) * g
    nv = ADAM_B2 * v + (1.0 - ADAM_B2) * jnp.square(g)
    return -ADAM_LR * ((nm / c1) / (jnp.sqrt(nv / c2) + ADAM_EPS) + ADAM_WD * w), nm, nv


def _adamw(name, w, g, m, v):
    rows, cols = w.shape
    tr = _row_tile(rows, cols)

    def body(w_ref, g_ref, m_ref, v_ref, d_ref, nm_ref, nv_ref):
        d_ref[...], nm_ref[...], nv_ref[...] = _adamw_math(w_ref[...], g_ref[...], m_ref[...], v_ref[...])

    spec = pl.BlockSpec((tr, cols), lambda i: (i, 0))
    return pl.pallas_call(body, name=name, grid=(rows // tr,), in_specs=[spec] * 4, out_specs=[spec] * 3,
                          out_shape=[_sds((rows, cols), F32)] * 3, compiler_params=_cp(("parallel",)))(w, g, m, v)


def _adamw_layer(name, w, g_own, g_sib, sel, m, v, layer, prev):
    depth, rows, cols = w.shape
    tr = _row_tile(rows // 2, cols)
    nbh = rows // 2 // tr
    nprev = 0 if prev is None else 4

    def body(sel_ref, w_ref, own_ref, sib_ref, m_ref, v_ref, *rest):
        go_ref, d_ref, nm_ref, nv_ref = rest[nprev:]
        gv = jnp.where(pl.program_id(0) // nbh == sel_ref[0], own_ref[...], sib_ref[...])
        go_ref[...] = gv
        d_ref[...], nm_ref[...], nv_ref[...] = _adamw_math(w_ref[...], gv, m_ref[...], v_ref[...])

    lay = pl.BlockSpec((None, tr, cols), lambda i, s: (layer, i, 0))
    hlf = pl.BlockSpec((tr, cols), lambda i, s: (i % nbh, 0))
    gs = pltpu.PrefetchScalarGridSpec(
        num_scalar_prefetch=1, grid=(2 * nbh,), in_specs=[lay, hlf, hlf, lay, lay] + [ANY] * nprev,
        out_specs=[lay] * 4)
    args = (sel, w, g_own, g_sib, m, v) + (() if prev is None else tuple(prev))
    return pl.pallas_call(
        body, name=name, grid_spec=gs, out_shape=[_sds((depth, rows, cols), F32)] * 4,
        input_output_aliases={6 + k: k for k in range(nprev)},
        compiler_params=_cp(("parallel",)))(*args)


def _adamw_colmajor(name, wt, mt, vt, halves, sel):
    c_dim, depth, r_dim = wt.shape
    hr = r_dim // 2

    def body(sel_ref, w_ref, m_ref, v_ref, *rest):
        g_refs, (go_ref, d_ref, nm_ref, nv_ref) = rest[:2 * depth], rest[2 * depth:]
        own_first = sel_ref[0] == 0
        for l in range(depth):
            own, sib = g_refs[2 * l][...], g_refs[2 * l + 1][...]
            g = jnp.concatenate([jnp.where(own_first, own, sib), jnp.where(own_first, sib, own)], axis=0).T
            go_ref[:, l, :] = g
            d_ref[:, l, :], nm_ref[:, l, :], nv_ref[:, l, :] = _adamw_math(w_ref[:, l, :], g, m_ref[:, l, :],
                                                                          v_ref[:, l, :])

    col = pl.BlockSpec((LANE, depth, r_dim), lambda j, s: (j, 0, 0))
    gs = pltpu.PrefetchScalarGridSpec(
        num_scalar_prefetch=1, grid=(c_dim // LANE,),
        in_specs=[col] * 3 + [pl.BlockSpec((hr, LANE), lambda j, s: (0, j))] * (2 * depth), out_specs=[col] * 4)
    flat = [h for pair in halves for h in pair]
    return pl.pallas_call(body, name=name, grid_spec=gs, out_shape=[_sds(wt.shape, F32)] * 4,
                          compiler_params=_cp(("parallel",)))(sel, wt, mt, vt, *flat)


def _adamw_tail(name, wt, mt, vt, gt_tail, prev):
    c_dim, depth, r_dim = wt.shape
    nt = gt_tail.shape[0]

    def body(w_ref, m_ref, v_ref, g_ref, *rest):
        go_ref, d_ref, nm_ref, nv_ref = rest[4:]
        g = g_ref[...]
        go_ref[...] = g
        d_ref[...], nm_ref[...], nv_ref[...] = _adamw_math(w_ref[...], g, m_ref[...], v_ref[...])

    tail = pl.BlockSpec((nt, depth, r_dim), lambda i: (c_dim // nt - 1, 0, 0))
    return pl.pallas_call(
        body, name=name, grid=(1,), in_specs=[tail] * 3 + [pl.BlockSpec((nt, depth, r_dim), lambda i: (0, 0, 0))]
        + [ANY] * 4, out_specs=[tail] * 4, out_shape=[_sds(wt.shape, F32)] * 4,
        input_output_aliases={4 + k: k for k in range(4)},
        compiler_params=_cp(("arbitrary",)))(wt, mt, vt, gt_tail, *prev)


def _place():
    x, y, c = (lax.axis_index(a) for a in MESH_AXES)
    chips = [(1 - x, y), (x, 1 - y), (1 - x, 1 - y)]
    return x, y, c, chips


def _chip_index(xy):
    return 2 * xy[0] + xy[1]


ANY = pl.BlockSpec(memory_space=pl.ANY)


HBM_SPEC = pl.BlockSpec(memory_space=pltpu.HBM)
SEM = pl.BlockSpec(memory_space=pltpu.SEMAPHORE)
EFFECT = pltpu.SideEffectType.DATAFLOW_SIDE_EFFECTING


def _half(ref, c):
    hr = ref.shape[-2] // 2
    return pl.ds(pl.multiple_of(c * hr, 16), hr)


def _gather_copies(srcs, lands, send, recv):
    x, y, c, chips = _place()
    me = _chip_index((x, y))
    return [pltpu.make_async_remote_copy(src_ref=s.at[_half(s, c)], dst_ref=g.at[me, _half(s, c)],
                                         send_sem=send.at[3 * a + j], recv_sem=recv.at[3 * a + j],
                                         device_id=(*ch, c), device_id_type=DEV)
            for a, (s, g) in enumerate(zip(srcs, lands)) for j, ch in enumerate(chips)]


def _scatter_copies(srcs, lands, send, recv):
    x, y, c, chips = _place()
    return [pltpu.make_async_remote_copy(src_ref=h.at[_chip_index(ch)], dst_ref=r.at[j],
                                         send_sem=send.at[3 * a + j], recv_sem=recv.at[3 * a + j],
                                         device_id=(*ch, c), device_id_type=DEV)
            for a, (h, r) in enumerate(zip(srcs, lands)) for j, ch in enumerate(chips)]


def _in_hbm(a):
    return pltpu.with_memory_space_constraint(a, pltpu.HBM)


def _split_start(name, srcs, land_shapes, copies_fn, after=None, per_src=3):
    ns, nl = len(srcs), len(land_shapes)
    ncp = per_src * ns
    lands = [lax.empty(s.shape, s.dtype) for s in land_shapes]
    behind = [] if after is None else [after]

    def body(*refs):
        src, land = refs[:ns], refs[ns:ns + nl]
        send, recv = refs[ns + nl + len(behind)], refs[ns + nl + len(behind) + 1]
        for cp in copies_fn(src, land, send, recv):
            cp.start()
        refs[-1][...] = jnp.zeros_like(refs[-1])

    bufs = list(srcs) + lands
    outs = pl.pallas_call(
        body, name=name, in_specs=[HBM_SPEC] * (ns + nl) + [ANY] * len(behind),
        out_specs=[SEM, SEM] + [HBM_SPEC] * (ns + nl) + [pl.BlockSpec(memory_space=pltpu.VMEM)],
        out_shape=[pltpu.SemaphoreType.DMA((ncp,)), pltpu.SemaphoreType.DMA((ncp,))]
        + [pltpu.HBM(b.shape, b.dtype) for b in bufs] + [_sds((8, LANE), F32)],
        input_output_aliases={i: 2 + i for i in range(ns + nl)},
        compiler_params=pltpu.CompilerParams(has_side_effects=EFFECT))(*[_in_hbm(b) for b in bufs], *behind)
    return outs[0], outs[1], list(outs[2:2 + ns]), list(outs[2 + ns:2 + ns + nl]), outs[-1]


def _split_wait(name, started, copies_fn, after):
    send, recv, srcs, lands, _ = started
    ns, nl = len(srcs), len(lands)

    def body(*refs):
        src, land = refs[:ns], refs[ns:ns + nl]
        for cp in copies_fn(src, land, refs[ns + nl], refs[ns + nl + 1]):
            cp.wait_send()
            cp.wait_recv()

    bufs = list(srcs) + list(lands)
    outs = pl.pallas_call(
        body, name=name, in_specs=[HBM_SPEC] * (ns + nl) + [SEM, SEM, ANY], out_specs=[HBM_SPEC] * (ns + nl),
        out_shape=[pltpu.HBM(b.shape, b.dtype) for b in bufs],
        input_output_aliases={i: i for i in range(ns + nl)},
        compiler_params=pltpu.CompilerParams(has_side_effects=EFFECT))(*bufs, send, recv, after)
    return list(outs[:ns]), list(outs[ns:])


def _gather_plain(name, srcs):
    n = len(srcs)

    def body(*refs):
        src, land = refs[:n], refs[n:2 * n]
        send, recv, fsend, frecv = refs[2 * n:]
        first = _gather_copies(src, land, send, recv)
        for cp in first:
            cp.start()
        _forward_body(land, first, fsend, frecv)

    return pl.pallas_call(
        body, name=name, in_specs=[ANY] * n, out_specs=[ANY] * n,
        out_shape=[_sds((4,) + s.shape, s.dtype) for s in srcs],
        scratch_shapes=[pltpu.SemaphoreType.DMA((3 * n,))] * 4)(*srcs)


def _forward_body(land, arrivals, fsend, frecv):
    x, y, c, chips = _place()
    n = len(land)
    passed = []
    for a in range(n):
        for j, ch in enumerate(chips):
            if arrivals is not None:
                arrivals[3 * a + j].wait_recv()
            slot = land[a].at[_chip_index(ch), _half(land[a], c)]
            fw = pltpu.make_async_remote_copy(src_ref=slot, dst_ref=slot, send_sem=fsend.at[3 * a + j],
                                              recv_sem=frecv.at[3 * a + j], device_id=(x, y, 1 - c),
                                              device_id_type=DEV)
            fw.start()
            passed.append(fw)
    for a in range(n):
        for j, ch in enumerate(chips):
            slot = land[a].at[_chip_index(ch), _half(land[a], 1 - c)]
            pltpu.make_async_remote_copy(src_ref=slot, dst_ref=slot, send_sem=fsend.at[3 * a + j],
                                         recv_sem=frecv.at[3 * a + j], device_id=(x, y, c),
                                         device_id_type=DEV).wait_recv()
    for cp in passed:
        cp.wait_send()
    if arrivals is not None:
        for cp in arrivals:
            cp.wait_send()


def _gather_forward(name, lands):
    n = len(lands)

    def body(*refs):
        _forward_body(refs[n:2 * n], None, refs[2 * n], refs[2 * n + 1])

    return pl.pallas_call(
        body, name=name, in_specs=[ANY] * n, out_specs=[ANY] * n,
        out_shape=[_sds(g.shape, g.dtype) for g in lands], input_output_aliases={a: a for a in range(n)},
        scratch_shapes=[pltpu.SemaphoreType.DMA((3 * n,))] * 2)(*lands)


def _sibling_copies(srcs, lands, send, recv):
    x, y, c, _ = _place()
    return [pltpu.make_async_remote_copy(src_ref=g.at[:, _half(g, 1 - c)], dst_ref=t, send_sem=send.at[a],
                                         recv_sem=recv.at[a], device_id=(x, y, 1 - c), device_id_type=DEV)
            for a, (g, t) in enumerate(zip(srcs, lands))]


def _sibling_share(name, sms):
    n = len(sms)

    def body(*refs):
        ins, outs = refs[:n], refs[n:2 * n]
        send, recv = refs[2 * n:]
        x, y, c, _ = _place()
        cps = [pltpu.make_async_remote_copy(src_ref=ins[a], dst_ref=outs[a], send_sem=send.at[a],
                                            recv_sem=recv.at[a], device_id=(x, y, 1 - c), device_id_type=DEV)
               for a in range(n)]
        for cp in cps:
            cp.start()
        for cp in cps:
            cp.wait()

    return pl.pallas_call(
        body, name=name, in_specs=[ANY] * n, out_specs=[ANY] * n, out_shape=[_sds(s.shape, F32) for s in sms],
        scratch_shapes=[pltpu.SemaphoreType.DMA((n,))] * 2)(*sms)


def _small_allreduce(v, after=None):
    rows = v.shape[0]
    ndev = 8
    behind = [] if after is None else [after]

    def body(v_ref, *rest):
        o_ref, gat_ref, send, recv = rest[len(behind):]
        x, y, c, _ = _place()
        me = 4 * x + 2 * y + c
        cps = []
        for k in range(1, ndev):
            to = (me + k) % ndev
            cp = pltpu.make_async_remote_copy(src_ref=v_ref, dst_ref=gat_ref.at[me], send_sem=send.at[k - 1],
                                              recv_sem=recv.at[me], device_id=(to // 4, (to // 2) % 2, to % 2),
                                              device_id_type=DEV)
            cp.start()
            cps.append(cp)
        gat_ref[me] = v_ref[...]
        for k in range(1, ndev):
            frm = (me + k) % ndev
            pltpu.make_async_remote_copy(src_ref=v_ref, dst_ref=gat_ref.at[frm], send_sem=send.at[k - 1],
                                         recv_sem=recv.at[frm], device_id=(x, y, c), device_id_type=DEV).wait_recv()
        for cp in cps:
            cp.wait_send()
        acc = gat_ref[0]
        for k in range(1, ndev):
            acc = acc + gat_ref[k]
        o_ref[...] = acc

    vm = pl.BlockSpec(memory_space=pltpu.VMEM)
    return pl.pallas_call(
        body, name="small_allreduce", in_specs=[vm] + [ANY] * len(behind), out_specs=vm,
        out_shape=_sds((rows, LANE), F32),
        scratch_shapes=[pltpu.VMEM((ndev, rows, LANE), F32), pltpu.SemaphoreType.DMA((ndev - 1,)),
                        pltpu.SemaphoreType.DMA((ndev,))])(v, *behind)


def _layout(d):
    half = d // 2
    names = [("aq", d), ("ak", d), ("av", d), ("rq", half), ("rk", half), ("rv", d), ("rg", d),
             ("gq", half), ("gk", half), ("gv", d), ("gg", d), ("gates", 3 * d), ("glr", 2 * LANE)]
    off, pos = {}, 0
    for nm, sz in names:
        off[nm] = pos
        pos += sz
    return off, pos


def _unpad_cols(g, d):
    a = 8 * d + d
    return jnp.concatenate([g[..., :a], g[..., a + 3 * d:a + 3 * d + GATE_RANK], g[..., a:a + 3 * d]], axis=-1)


def kernel(x, ln_in_g, ln_in_b, w_in, rel_bias, gla_w_lr, gla_b_lr, gla_norm_g, w_branch, w_out, ln1_g, ln1_b, w_up, w_down, ln2_g, ln2_b, loss_target, m_ln_in_g, m_ln_in_b, m_w_in, m_rel_bias, m_gla_w_lr, m_gla_b_lr, m_gla_norm_g, m_w_branch, m_w_out, m_ln1_g, m_ln1_b, m_w_up, m_w_down, m_ln2_g, m_ln2_b, v_ln_in_g, v_ln_in_b, v_w_in, v_rel_bias, v_gla_w_lr, v_gla_b_lr, v_gla_norm_g, v_w_branch, v_w_out, v_ln1_g, v_ln1_b, v_w_up, v_w_down, v_ln2_g, v_ln2_b):
    t, d = x.shape[1], x.shape[2]
    dff = 4 * d
    half = d // 2
    off, npad = _layout(d)
    xi, yi, ci = (lax.axis_index(a) for a in MESH_AXES)
    chip = 2 * xi + yi
    csel = jnp.reshape(ci, (1,)).astype(jnp.int32)
    psel = jnp.reshape(chip, (1,)).astype(jnp.int32)

    big_w = [w_in, w_branch.reshape(DEPTH, -1, d), w_out, w_up, w_down]
    big_m = [m_w_in, m_w_branch.reshape(DEPTH, -1, d), m_w_out, m_w_up, m_w_down]
    big_v = [v_w_in, v_w_branch.reshape(DEPTH, -1, d), v_w_out, v_w_up, v_w_down]
    W_IN, REST = [0], [1, 2, 3, 4]

    def shards_of(l, idx):
        return [big_w[i][l].astype(BF16) for i in idx]

    def lands_of(srcs):
        return [_sds((4,) + s.shape, s.dtype) for s in srcs]

    def full_w_in(g):
        per = g.shape[2]
        a = 8 * d + d

        def run(lo, hi):
            cuts = [(max(lo, c * per), min(hi, (c + 1) * per), c) for c in range(4)]
            return [g[c, :, x - c * per:y - c * per] for x, y, c in cuts if x < y]

        zeros = jnp.zeros((d, 2 * LANE - GATE_RANK), g.dtype)
        return jnp.concatenate(run(0, a) + run(a + GATE_RANK, 4 * per) + run(a, a + GATE_RANK) + [zeros], axis=1)

    def full_rest(gs):
        g_br, g_out, g_up, g_down = gs
        return (jnp.transpose(g_br.reshape(4, N_BRANCH, d // 4, d), (1, 0, 2, 3)).reshape(N_BRANCH, d, d),
                g_out.reshape(d, d), jnp.transpose(g_up, (1, 0, 2)).reshape(d, dff), g_down.reshape(dff, d))

    def with_own(srcs, lands):
        return [lax.dynamic_update_slice(g, s[None], (chip, 0, 0)) for s, g in zip(srcs, lands)]

    def gather_start(tag, l, idx, after):
        srcs = shards_of(l, idx)
        return srcs, _split_start(f"gather_{tag}{l}_start", srcs, lands_of(srcs), _gather_copies, after)

    def gather_finish(tag, l, pending, after):
        srcs, started = pending
        _, lands = _split_wait(f"gather_{tag}{l}_wait", started, _gather_copies, after)
        return with_own(srcs, _gather_forward(f"gather_{tag}{l}_pass", lands))

    def token(pending):
        return pending[1][4][0, 0]

    win, wbr, wout, wup, wdown = ([None] * DEPTH for _ in range(5))
    src_first = shards_of(0, W_IN)
    g_first = with_own(src_first, _gather_plain("gather_in0", src_first))
    win[0] = full_w_in(g_first[0])

    dkh = half // LIN_HEADS
    lr_rows = DEPTH * GATE_RANK
    lr_slab = jnp.zeros((lr_rows, 4, half // 4), F32)
    lr_slab = lax.dynamic_update_slice(lr_slab, (gla_w_lr.reshape(lr_rows, 1, half // 4) * jnp.where(ci == 0, 1.0, 0.0)),
                                       (0, chip, 0))
    wlr_full = _small_allreduce(lr_slab.reshape(-1, LANE)).reshape(DEPTH, GATE_RANK, half)
    wlr_pad = jnp.concatenate([wlr_full, jnp.zeros((DEPTH, LANE - GATE_RANK, half), F32)], axis=1)
    pend_rest = gather_start("rest", 0, REST, wlr_full[0, :1, :1] + g_first[0][0, :1, :1].astype(F32))

    inv = 10000.0 ** (-jnp.arange(0, dkh, 2, dtype=F32) / dkh)
    ang = jnp.arange(t, dtype=F32)[:, None] * inv[None, :]
    cos, sin = jnp.cos(ang), jnp.sin(ang)
    rope_c = jnp.concatenate([cos, cos], axis=1)
    rope_s = jnp.concatenate([-sin, sin], axis=1)
    log_gamma = jnp.log1p(-jnp.exp2(-5.0 - jnp.arange(LIN_HEADS, dtype=F32)))
    lg_tab = jnp.broadcast_to(log_gamma[:, None, None], (LIN_HEADS, 1, dkh))

    def vec(a):
        return a.reshape(1, -1)

    x0, x0b, xh_in, rs_in = _ln_in(x[0], vec(ln_in_g) + token(pend_rest), vec(ln_in_b))
    saved = []
    xl, xlb = x0, x0b
    for l in range(DEPTH):
        p = _mm("proj_in", xlb, win[l], 512, 1792)
        g_rest = gather_finish("rest", l, pend_rest, p)
        wbr[l], wout[l], wup[l], wdown[l] = full_rest(g_rest)
        tok = 0.0
        if l + 1 < DEPTH:
            pend_in = gather_start("in", l + 1, W_IN, g_rest[0])
            tok = token(pend_in)
        bias = _bias_expand(rel_bias[l] + tok)
        bo = _attn_fwd(p, bias, d, off)
        ret_aux = (rope_c, rope_s, lg_tab + tok)
        gla_aux = (p, wlr_pad[l], vec(gla_b_lr[l]) + tok, vec(gla_norm_g[l]))
        o_ret, bo, st_ret = _lin_fwd(False, p, ret_aux, d, off, bo, 1)
        o_gla, bo, st_gla = _lin_fwd(True, p, gla_aux, d, off, bo, 2)
        tok = 0.0
        if l + 1 < DEPTH:
            g_in = gather_finish("in", l + 1, pend_in, bo)
            win[l + 1] = full_w_in(g_in[0])
            pend_rest = gather_start("rest", l + 1, REST, g_in[0])
            tok = token(pend_rest)
        proj, merged = _merge_fwd(bo, wbr[l], p, off["gates"])
        x1, x1b, xh1, rs1 = _mm_res_ln("out_proj_ln", merged, wout[l], xl, vec(ln1_g[l]) + tok, vec(ln1_b[l]),
                                       256, False)
        u = _mm("mlp_up", x1b, wup[l], 1024, 1024)
        x2, x2b, xh2, rs2, act = _mm_res_ln("mlp_down_ln", u, wdown[l], x1, vec(ln2_g[l]), vec(ln2_b[l]), 256, True)
        saved.append(dict(xlb=xlb, p=p, bias=bias, ret_aux=ret_aux, gla_aux=gla_aux, o_ret=o_ret, o_gla=o_gla,
                          st_ret=st_ret, st_gla=st_gla, bo=bo, proj=proj, merged=merged, x1b=x1b, xh1=xh1,
                          rs1=rs1, u=u, xh2=xh2, rs2=rs2, act=act))
        xl, xlb = x2, x2b

    small = {}
    last = saved[-1]
    loss_p, dz2, dz2b, dg, db = _loss_ln_bwd(xl, loss_target[0], last["xh2"], last["rs2"], vec(ln2_g[DEPTH - 1]))
    small["loss"] = loss_p[:, :1]
    grad_x = None

    def sibling_start(tag, l, idx, shards):
        lands = [_sds((g.shape[0], g.shape[1] // 2, g.shape[2]), F32) for g in shards]
        return tag, l, idx, _split_start(f"grad_{tag}{l}_sibling_start", shards, lands, _sibling_copies, per_src=1)

    def scatter_start(sibling, after):
        tag, l, idx, started = sibling
        shards, theirs = _split_wait(f"grad_{tag}{l}_sibling_wait", started, _sibling_copies, after)
        hs = [_add_half("grad_sibling_add", g, th, csel) for g, th in zip(shards, theirs)]
        lands = [_sds((3,) + h.shape[1:], F32) for h in hs]
        return tag, l, idx, _split_start(f"grad_{tag}{l}_scatter_start", hs, lands, _scatter_copies)

    adam_out = [None] * len(big_w)
    w_in_halves = [None] * DEPTH

    def scatter_finish(pending, after):
        tag, l, idx, started = pending
        hs, rcv = _split_wait(f"grad_{tag}{l}_scatter_wait", started, _scatter_copies, after)
        sms = [_sum_shards("grad_chip_sum", h, r, psel) for h, r in zip(hs, rcv)]
        last = None
        for i, own, sib in zip(idx, sms, _sibling_share(f"grad_{tag}{l}_share", sms)):
            if i == W_IN[0]:
                w_in_halves[l] = (own, sib)
                last = sib
            else:
                adam_out[i] = _adamw_layer("adamw_large", big_w[i], own, sib, csel, big_m[i], big_v[i], l,
                                           adam_out[i])
                last = adam_out[i][0]
        return last

    in_flight = []

    def scatter(sibling, after):
        pending = scatter_start(sibling, after)
        in_flight.append(pending)
        if len(in_flight) > 3:
            scatter_finish(in_flight.pop(0), pending[3][4])
        return pending[3][4][0, 0]

    def token_of(sibling):
        return sibling[3][4][0, 0]

    carry_tok = 0.0
    for l in reversed(range(DEPTH)):
        s = saved[l]
        small[("ln2_g", l)], small[("ln2_b", l)] = dg, db
        du = _mm_nt_relu2_bwd(dz2b, wdown[l], s["u"])
        g_wdown = _mm_tn("grad_w_down", s["act"], dz2b, 512, 512)
        g_wup = _mm_tn("grad_w_up", s["x1b"], du, 512, 512, shard="cols")
        dz1, dz1b, dg1, db1 = _mm_nt_res_lnbwd("mlp_up_bwd_ln", du, wup[l], dz2, s["xh1"], s["rs1"],
                                               vec(ln1_g[l]) + carry_tok, 256, dff)
        small[("ln1_g", l)], small[("ln1_b", l)] = dg1, db1
        dproj, dgl = _merge_bwd(dz1b, wout[l], s["proj"], s["p"], off["gates"])
        g_wout = _mm_tn("grad_w_out", s["merged"], dz1b, 512, 512)
        dbo = _mm("branch_proj_bwd", dproj, wbr[l], 1024, 1024, nt=True)
        g_wbr = _mm_tn("grad_w_branch", s["bo"], dproj, d // 4, 1024, shard="rows")
        sib = sibling_start("rest", l, REST, [g_wbr, g_wout.reshape(4, d // 4, d), g_wup, g_wdown.reshape(4, d, d)])
        rc, rs_, lg = s["ret_aux"]
        gp, gw, gb, gn_ = s["gla_aux"]
        dq_a, dk_acc, dv_acc, dbias = _attn_bwd(s["p"], s["bias"] + token_of(sib), dbo, d, off)
        tok = scatter(sib, dq_a)
        small[("rel_bias", l)] = _bias_reduce(dbias)
        dk_a = dk_acc[2 * QB:].astype(BF16)
        dv_a = dv_acc[2 * QB:].astype(BF16)
        dq_r, dk_r, dv_r, dg_r = _lin_bwd(False, s["p"], (rc, rs_, lg + tok), s["o_ret"], s["st_ret"], dbo, 1, d, off)
        dq_g, dk_g, dv_g, dg_g, dpre, dblr, dgn = _lin_bwd(True, s["p"], (gp, gw, gb + tok, gn_), s["o_gla"],
                                                           s["st_gla"], dbo, 2, d, off)
        small[("gla_b_lr", l)] = dblr.reshape(1, half)
        small[("gla_norm_g", l)] = jnp.sum(dgn, axis=0)
        dpre_b = dpre.astype(BF16)
        glr_b = s["p"][:, off["glr"]:off["glr"] + LANE].astype(BF16)
        dglr = _mm("gate_lr_bwd", dpre_b, wlr_pad[l], 512, LANE, nt=True, out_dtype=BF16)
        small[("gla_w_lr", l)] = _mm_tn("grad_gla_w_lr", glr_b, dpre_b, LANE, half)[:GATE_RANK]
        dp = jnp.concatenate([dq_a, dk_a, dv_a, dq_r, dk_r, dv_r, dg_r, dq_g, dk_g, dv_g, dg_g,
                              dgl[0], dgl[1], dgl[2], dglr, jnp.zeros((t, LANE), BF16)], axis=1)
        if l > 0:
            prev = saved[l - 1]
            xh_p, rs_p, g_p = prev["xh2"], prev["rs2"], vec(ln2_g[l - 1])
        else:
            xh_p, rs_p, g_p = xh_in, rs_in, vec(ln_in_g)
        g_win = _mm_tn("grad_w_in", s["xlb"], dp, 1024, 896)
        sib = sibling_start("in", l, W_IN, [jnp.transpose(_unpad_cols(g_win, d).reshape(d, 4, -1), (1, 0, 2))])
        if l > 0:
            tok = token_of(sib)
        else:
            tok = scatter(sib, sib[3][4])
        dzp, dzpb, dg, db = _mm_nt_res_lnbwd("proj_in_bwd_ln", dp, win[l], dz1, xh_p, rs_p, g_p + tok, 1024, 1792)
        if l > 0:
            carry_tok = scatter(sib, dzp)
        dz2, dz2b = dzp, dzpb
        grad_x = dzp
    after = grad_x
    while in_flight:
        after = scatter_finish(in_flight.pop(0), after)
    wt, mt, vt = (jnp.transpose(a, (2, 0, 1)) for a in (big_w[0], big_m[0], big_v[0]))
    ntail = wt.shape[0] % LANE
    tails = [jnp.where(ci == 0, jnp.concatenate([own[:, -ntail:], sib[:, -ntail:]]),
                       jnp.concatenate([sib[:, -ntail:], own[:, -ntail:]])).T for own, sib in w_in_halves]
    adam_t = _adamw_tail("adamw_w_in_tail", wt, mt, vt, jnp.stack(tails, axis=1),
                         _adamw_colmajor("adamw_w_in", wt, mt, vt, w_in_halves, csel))
    adam_out[0] = [jnp.transpose(r, (1, 2, 0)) for r in adam_t]
    small["ln_in_g"], small["ln_in_b"] = dg, db
    rb_pad = 3 * LANE
    pieces = [small["loss"].reshape(-1), jnp.zeros((LANE - 1,), F32), small["ln_in_g"].reshape(-1),
              small["ln_in_b"].reshape(-1)]
    for l in range(DEPTH):
        rb = jnp.pad(small[("rel_bias", l)], ((0, 0), (0, rb_pad - (2 * REL_CLIP + 1))))
        pieces += [rb.reshape(-1), small[("gla_w_lr", l)].reshape(-1), small[("gla_b_lr", l)].reshape(-1),
                   small[("gla_norm_g", l)].reshape(-1), small[("ln1_g", l)].reshape(-1),
                   small[("ln1_b", l)].reshape(-1), small[("ln2_g", l)].reshape(-1), small[("ln2_b", l)].reshape(-1)]
    sizes = [pc.shape[0] for pc in pieces]
    packed = jnp.concatenate(pieces)
    padn = (-packed.shape[0]) % (8 * LANE)
    packed = jnp.concatenate([packed, jnp.zeros((padn,), F32)]).reshape(-1, LANE)
    red = _small_allreduce(packed, after).reshape(-1)

    parts, pos = [], 0
    for sz in sizes:
        parts.append(red[pos:pos + sz])
        pos += sz
    loss = parts[0][0]
    g_ln_in_g, g_ln_in_b = parts[2], parts[3]
    per = 8
    g_rel = jnp.stack([parts[4 + per * l].reshape(ATTN_HEADS, rb_pad)[:, :2 * REL_CLIP + 1] for l in range(DEPTH)])
    g_wlr_full = jnp.stack([parts[5 + per * l].reshape(GATE_RANK, half) for l in range(DEPTH)])
    g_wlr = lax.dynamic_slice_in_dim(g_wlr_full, chip * (half // 4), half // 4, axis=2)
    g_blr = jnp.stack([parts[6 + per * l] for l in range(DEPTH)])
    g_gn = jnp.stack([parts[7 + per * l] for l in range(DEPTH)])
    g_ln1g = jnp.stack([parts[8 + per * l] for l in range(DEPTH)])
    g_ln1b = jnp.stack([parts[9 + per * l] for l in range(DEPTH)])
    g_ln2g = jnp.stack([parts[10 + per * l] for l in range(DEPTH)])
    g_ln2b = jnp.stack([parts[11 + per * l] for l in range(DEPTH)])

    grads = [g_ln_in_g, g_ln_in_b, None, g_rel, g_wlr, g_blr, g_gn, None, None, g_ln1g, g_ln1b, None, None,
             g_ln2g, g_ln2b]
    ws = [ln_in_g, ln_in_b, w_in, rel_bias, gla_w_lr, gla_b_lr, gla_norm_g, w_branch, w_out, ln1_g, ln1_b,
          w_up, w_down, ln2_g, ln2_b]
    ms = [m_ln_in_g, m_ln_in_b, m_w_in, m_rel_bias, m_gla_w_lr, m_gla_b_lr, m_gla_norm_g, m_w_branch, m_w_out,
          m_ln1_g, m_ln1_b, m_w_up, m_w_down, m_ln2_g, m_ln2_b]
    vs = [v_ln_in_g, v_ln_in_b, v_w_in, v_rel_bias, v_gla_w_lr, v_gla_b_lr, v_gla_norm_g, v_w_branch, v_w_out,
          v_ln1_g, v_ln1_b, v_w_up, v_w_down, v_ln2_g, v_ln2_b]

    deltas, new_ms, new_vs = [None] * 15, [None] * 15, [None] * 15
    big_idx = [2, 7, 8, 11, 12]
    for i, res in zip(big_idx, adam_out):
        shp = ws[i].shape
        grads[i], deltas[i], new_ms[i], new_vs[i] = (r.reshape(shp) for r in res)
    small_idx = [i for i in range(15) if i not in big_idx]

    def pack(arrs):
        flat_ = jnp.concatenate([arrs[i].reshape(-1) for i in small_idx])
        pad_ = (-flat_.shape[0]) % (8 * LANE)
        return jnp.concatenate([flat_, jnp.ones((pad_,), F32)]).reshape(-1, LANE)

    dl, nm, nv = _adamw("adamw_small", pack(ws), pack(grads), pack(ms), pack(vs))
    pos = 0
    for i in small_idx:
        sz = int(np.prod(ws[i].shape))
        deltas[i] = dl.reshape(-1)[pos:pos + sz].reshape(ws[i].shape)
        new_ms[i] = nm.reshape(-1)[pos:pos + sz].reshape(ws[i].shape)
        new_vs[i] = nv.reshape(-1)[pos:pos + sz].reshape(ws[i].shape)
        pos += sz

    return (loss, grad_x[None], *grads, *deltas, *new_ms, *new_vs)
```

```python
import functools

import numpy as np
import jax
import jax.numpy as jnp
from jax import lax
from jax.experimental import pallas as pl
from jax.experimental.pallas import tpu as pltpu

F32 = jnp.float32
BF16 = jnp.bfloat16
MXU_DTYPE = BF16
HI = lax.Precision.HIGHEST

DEPTH = 2
CHUNK = 64
N_BRANCH = 3
ATTN_HEADS = 8
ATTN_LEFT = 8
REL_CLIP = 2 * CHUNK
LIN_HEADS = 4
GATE_RANK = 16
GATE_NORM = 16.0
LN_EPS = 1e-5
NEG_INF = -1e30
ALPHA = (2 * DEPTH) ** 0.25
ADAM_LR, ADAM_B1, ADAM_B2, ADAM_EPS, ADAM_WD, ADAM_STEP = 0.001, 0.9, 0.999, 1e-08, 0.01, 10

LANE = 128
VMEM_LIMIT = 56 << 20
QB = 256
KW = 3 * QB
LB = 256
MESH_AXES = ("x", "y", "c")
DEV = pl.DeviceIdType.MESH


def _cp(sem):
    return pltpu.CompilerParams(dimension_semantics=sem, vmem_limit_bytes=VMEM_LIMIT)


def _mx(v):
    return v.astype(MXU_DTYPE)


def _dot(a, b):
    return jnp.dot(_mx(a), _mx(b), preferred_element_type=F32)


def _dot_nt(a, b):
    return lax.dot_general(_mx(a), _mx(b), (((1,), (1,)), ((), ())), preferred_element_type=F32)


def _dot_tn(a, b):
    return lax.dot_general(_mx(a), _mx(b), (((0,), (0,)), ((), ())), preferred_element_type=F32)


def _dot_hi(a, b):
    return jnp.dot(a, b, precision=HI, preferred_element_type=F32)


def _sigmoid(v):
    return 1.0 / (1.0 + jnp.exp(-v))


def _sds(shape, dtype):
    return jax.ShapeDtypeStruct(shape, dtype)


def _mm(name, a, b, tm, tn, nt=False, out_dtype=F32):
    batched = a.ndim == 3
    m, k = a.shape[-2:]
    n = b.shape[-2] if nt else b.shape[-1]
    tm, tn = min(tm, m), min(tn, n)

    def body(a_ref, b_ref, o_ref):
        f = _dot_nt if nt else _dot
        o_ref[...] = f(a_ref[...], b_ref[...]).astype(o_ref.dtype)

    rows_inner = (n // tn) * m < (m // tm) * n

    def ij(u, v):
        return (v, u) if rows_inner else (u, v)

    if batched:
        nb = a.shape[0]
        grid = (nb,) + ij(m // tm, n // tn)
        a_spec = pl.BlockSpec((None, tm, k), lambda g, u, v: (g, ij(u, v)[0], 0))
        b_spec = (pl.BlockSpec((None, tn, k), lambda g, u, v: (g, ij(u, v)[1], 0)) if nt
                  else pl.BlockSpec((None, k, tn), lambda g, u, v: (g, 0, ij(u, v)[1])))
        o_spec = pl.BlockSpec((None, tm, tn), lambda g, u, v: (g,) + ij(u, v))
        out_shape = _sds((nb, m, n), out_dtype)
        sem = ("parallel", "parallel", "parallel")
    else:
        grid = ij(m // tm, n // tn)
        a_spec = pl.BlockSpec((tm, k), lambda u, v: (ij(u, v)[0], 0))
        b_spec = (pl.BlockSpec((tn, k), lambda u, v: (ij(u, v)[1], 0)) if nt
                  else pl.BlockSpec((k, tn), lambda u, v: (0, ij(u, v)[1])))
        o_spec = pl.BlockSpec((tm, tn), lambda u, v: ij(u, v))
        out_shape = _sds((m, n), out_dtype)
        sem = ("parallel", "parallel")
    return pl.pallas_call(body, name=name, grid=grid, in_specs=[a_spec, b_spec], out_specs=o_spec,
                          out_shape=out_shape, compiler_params=_cp(sem))(a, b)


def _mm_tn(name, a, b, tm, tn, shard=None):
    batched = a.ndim == 3
    k, m = a.shape[-2:]
    n = b.shape[-1]
    tm, tn = min(tm, m), min(tn, n)

    def body(a_ref, b_ref, o_ref):
        o_ref[...] = lax.dot_general(_mx(a_ref[...]), _mx(b_ref[...]), (((0,), (0,)), ((), ())),
                                     preferred_element_type=F32)

    if batched:
        nb = a.shape[0]
        grid = (nb, m // tm, n // tn)
        a_spec = pl.BlockSpec((None, k, tm), lambda g, i, j: (g, 0, i))
        b_spec = pl.BlockSpec((None, k, tn), lambda g, i, j: (g, 0, j))
        if shard == "rows":
            assert 4 * tm == m
            o_spec = pl.BlockSpec((None, tm, tn), lambda g, i, j: (i, g, j))
            out_shape = _sds((4, nb * tm, n), F32)
        else:
            o_spec = pl.BlockSpec((None, tm, tn), lambda g, i, j: (g, i, j))
            out_shape = _sds((nb, m, n), F32)
    else:
        grid = (m // tm, n // tn)
        a_spec = pl.BlockSpec((k, tm), lambda i, j: (0, i))
        b_spec = pl.BlockSpec((k, tn), lambda i, j: (0, j))
        if shard == "cols":
            per = n // 4 // tn
            o_spec = pl.BlockSpec((None, tm, tn), lambda i, j: (j // per, i, j % per))
            out_shape = _sds((4, m, n // 4), F32)
        else:
            o_spec = pl.BlockSpec((tm, tn), lambda i, j: (i, j))
            out_shape = _sds((m, n), F32)
    return pl.pallas_call(body, name=name, grid=grid, in_specs=[a_spec, b_spec], out_specs=o_spec,
                          out_shape=out_shape, compiler_params=_cp(("parallel",) * len(grid)))(a, b)


def _ln_rows(y, g, b):
    mu = jnp.mean(y, axis=-1, keepdims=True)
    yc = y - mu
    var = jnp.mean(yc * yc, axis=-1, keepdims=True)
    rs = lax.rsqrt(var + LN_EPS)
    xh = yc * rs
    return xh * g + b, xh, rs


def _ln_in(x, g, b, tm=256):
    t, d = x.shape

    def body(x_ref, g_ref, b_ref, o_ref, ob_ref, xh_ref, rs_ref):
        o, xh, rs = _ln_rows(x_ref[...], g_ref[...], b_ref[...])
        o_ref[...] = o
        ob_ref[...] = o.astype(BF16)
        xh_ref[...] = xh
        rs_ref[...] = rs

    row = pl.BlockSpec((tm, d), lambda i: (i, 0))
    vec = pl.BlockSpec((1, d), lambda i: (0, 0))
    return pl.pallas_call(
        body, name="ln_in", grid=(t // tm,), in_specs=[row, vec, vec],
        out_specs=[row, row, row, pl.BlockSpec((tm, 1), lambda i: (i, 0))],
        out_shape=[_sds((t, d), F32), _sds((t, d), BF16), _sds((t, d), F32), _sds((t, 1), F32)],
        compiler_params=_cp(("parallel",)))(x, g, b)


def _mm_res_ln(name, a, w, res, g, b, tm, relu2):
    t, k = a.shape
    d = w.shape[1]

    def body(a_ref, w_ref, r_ref, g_ref, b_ref, o_ref, ob_ref, xh_ref, rs_ref, *act_ref):
        av = a_ref[...]
        if relu2:
            av = jnp.square(jnp.maximum(av, 0.0))
            act_ref[0][...] = av.astype(BF16)
        y = ALPHA * r_ref[...] + _dot(av, w_ref[...])
        o, xh, rs = _ln_rows(y, g_ref[...], b_ref[...])
        o_ref[...] = o
        ob_ref[...] = o.astype(BF16)
        xh_ref[...] = xh
        rs_ref[...] = rs

    row = pl.BlockSpec((tm, d), lambda i: (i, 0))
    vec = pl.BlockSpec((1, d), lambda i: (0, 0))
    arow = pl.BlockSpec((tm, k), lambda i: (i, 0))
    out_specs = [row, row, row, pl.BlockSpec((tm, 1), lambda i: (i, 0))]
    out_shape = [_sds((t, d), F32), _sds((t, d), BF16), _sds((t, d), F32), _sds((t, 1), F32)]
    if relu2:
        out_specs.append(arow)
        out_shape.append(_sds((t, k), BF16))
    return pl.pallas_call(
        body, name=name, grid=(t // tm,),
        in_specs=[arow, pl.BlockSpec((k, d), lambda i: (0, 0)), row, vec, vec],
        out_specs=out_specs, out_shape=out_shape, compiler_params=_cp(("parallel",)))(a, w, res, g, b)


def _merge_fwd(bo, wb, p, gate_off, tm=512, tn=512):
    _, t, d = bo.shape
    gb = gate_off // tn

    def body(bo_ref, wb_ref, g0, g1, g2, proj_ref, m_ref):
        acc = None
        for n, g_ref in enumerate((g0, g1, g2)):
            pr = _dot(bo_ref[n], wb_ref[n])
            proj_ref[n] = pr
            term = _sigmoid(g_ref[...]) * pr
            acc = term if acc is None else acc + term
        m_ref[...] = acc.astype(BF16)

    gspecs = [pl.BlockSpec((tm, tn), functools.partial(lambda i, j, n: (i, gb + n * (d // tn) + j), n=n))
              for n in range(3)]
    return pl.pallas_call(
        body, name="merge_fwd", grid=(t // tm, d // tn),
        in_specs=[pl.BlockSpec((3, tm, d), lambda i, j: (0, i, 0)),
                  pl.BlockSpec((3, d, tn), lambda i, j: (0, 0, j))] + gspecs,
        out_specs=[pl.BlockSpec((3, tm, tn), lambda i, j: (0, i, j)), pl.BlockSpec((tm, tn), lambda i, j: (i, j))],
        out_shape=[_sds((3, t, d), F32), _sds((t, d), BF16)],
        compiler_params=_cp(("parallel", "parallel")))(bo, wb, p, p, p)


def _merge_bwd(dz, wout, proj, p, gate_off, tm=512, tn=512):
    t, d = dz.shape
    gb = gate_off // tn

    def body(dz_ref, w_ref, proj_ref, g0, g1, g2, dproj_ref, dgl_ref):
        dm = _dot_nt(dz_ref[...], w_ref[...])
        for n, g_ref in enumerate((g0, g1, g2)):
            s = _sigmoid(g_ref[...])
            dproj_ref[n] = (dm * s).astype(BF16)
            dgl_ref[n] = (dm * proj_ref[n] * (s * (1.0 - s))).astype(BF16)

    gspecs = [pl.BlockSpec((tm, tn), functools.partial(lambda i, j, n: (i, gb + n * (d // tn) + j), n=n))
              for n in range(3)]
    dproj, dgl = pl.pallas_call(
        body, name="merge_bwd", grid=(t // tm, d // tn),
        in_specs=[pl.BlockSpec((tm, d), lambda i, j: (i, 0)), pl.BlockSpec((tn, d), lambda i, j: (j, 0)),
                  pl.BlockSpec((3, tm, tn), lambda i, j: (0, i, j))] + gspecs,
        out_specs=[pl.BlockSpec((3, tm, tn), lambda i, j: (0, i, j)),
                   pl.BlockSpec((3, tm, tn), lambda i, j: (0, i, j))],
        out_shape=[_sds((3, t, d), BF16), _sds((3, t, d), BF16)],
        compiler_params=_cp(("parallel", "parallel")))(dz, wout, proj, p, p, p)
    return dproj, dgl


def _mm_nt_relu2_bwd(dz, wdown, u, tm=512, tn=1024):
    t, d = dz.shape
    f = wdown.shape[0]

    def body(dz_ref, w_ref, u_ref, du_ref):
        da = _dot_nt(dz_ref[...], w_ref[...])
        du_ref[...] = (da * (2.0 * jnp.maximum(u_ref[...], 0.0))).astype(BF16)

    return pl.pallas_call(
        body, name="mlp_down_bwd", grid=(t // tm, f // tn),
        in_specs=[pl.BlockSpec((tm, d), lambda i, j: (i, 0)), pl.BlockSpec((tn, d), lambda i, j: (j, 0)),
                  pl.BlockSpec((tm, tn), lambda i, j: (i, j))],
        out_specs=pl.BlockSpec((tm, tn), lambda i, j: (i, j)), out_shape=_sds((t, f), BF16),
        compiler_params=_cp(("parallel", "parallel")))(dz, wdown, u)


def _ln_bwd_rows(dx, xh, rs, g):
    dxh = dx * g
    m1 = jnp.mean(dxh, axis=-1, keepdims=True)
    m2 = jnp.mean(dxh * xh, axis=-1, keepdims=True)
    return rs * (dxh - m1 - xh * m2)


def _mm_nt_res_lnbwd(name, a, w, dres, xh, rs, g, tm, tk):
    t, k = a.shape
    d = w.shape[0]
    nk = k // tk

    def body(a_ref, w_ref, dr_ref, xh_ref, rs_ref, g_ref, dz_ref, dzb_ref, dg_ref, db_ref, acc_ref):
        i, kk = pl.program_id(0), pl.program_id(1)

        @pl.when(kk == 0)
        def _():
            acc_ref[...] = ALPHA * dr_ref[...]

        acc_ref[...] += _dot_nt(a_ref[...], w_ref[...])

        @pl.when(jnp.logical_and(i == 0, kk == 0))
        def _():
            dg_ref[...] = jnp.zeros_like(dg_ref)
            db_ref[...] = jnp.zeros_like(db_ref)

        @pl.when(kk == nk - 1)
        def _():
            dx = acc_ref[...]
            xhv = xh_ref[...]
            dz = _ln_bwd_rows(dx, xhv, rs_ref[...], g_ref[...])
            dz_ref[...] = dz
            dzb_ref[...] = dz.astype(BF16)
            dg_ref[...] += jnp.sum(dx * xhv, axis=0, keepdims=True)
            db_ref[...] += jnp.sum(dx, axis=0, keepdims=True)

    row = pl.BlockSpec((tm, d), lambda i, kk: (i, 0))
    vec = pl.BlockSpec((1, d), lambda i, kk: (0, 0))
    return pl.pallas_call(
        body, name=name, grid=(t // tm, nk),
        in_specs=[pl.BlockSpec((tm, tk), lambda i, kk: (i, kk)), pl.BlockSpec((d, tk), lambda i, kk: (0, kk)),
                  row, row, pl.BlockSpec((tm, 1), lambda i, kk: (i, 0)), vec],
        out_specs=[row, row, vec, vec],
        out_shape=[_sds((t, d), F32), _sds((t, d), BF16), _sds((1, d), F32), _sds((1, d), F32)],
        scratch_shapes=[pltpu.VMEM((tm, d), F32)],
        compiler_params=_cp(("arbitrary", "arbitrary")))(a, w, dres, xh, rs, g)


def _loss_ln_bwd(x2, target, xh, rs, g, tm=256):
    t, d = x2.shape

    def body(x_ref, t_ref, xh_ref, rs_ref, g_ref, loss_ref, dz_ref, dzb_ref, dg_ref, db_ref):
        @pl.when(pl.program_id(0) == 0)
        def _():
            loss_ref[...] = jnp.zeros_like(loss_ref)
            dg_ref[...] = jnp.zeros_like(dg_ref)
            db_ref[...] = jnp.zeros_like(db_ref)

        err = x_ref[...] - t_ref[...]
        per_row = jnp.mean(err * err, axis=-1, keepdims=True)
        loss_ref[...] += 0.5 * jnp.sum(per_row, axis=0, keepdims=True)
        dx = err * (1.0 / d)
        xhv = xh_ref[...]
        dz = _ln_bwd_rows(dx, xhv, rs_ref[...], g_ref[...])
        dz_ref[...] = dz
        dzb_ref[...] = dz.astype(BF16)
        dg_ref[...] += jnp.sum(dx * xhv, axis=0, keepdims=True)
        db_ref[...] += jnp.sum(dx, axis=0, keepdims=True)

    row = pl.BlockSpec((tm, d), lambda i: (i, 0))
    vec = pl.BlockSpec((1, d), lambda i: (0, 0))
    return pl.pallas_call(
        body, name="loss_ln_bwd", grid=(t // tm,),
        in_specs=[row, row, row, pl.BlockSpec((tm, 1), lambda i: (i, 0)), vec],
        out_specs=[pl.BlockSpec((1, LANE), lambda i: (0, 0)), row, row, vec, vec],
        out_shape=[_sds((1, LANE), F32), _sds((t, d), F32), _sds((t, d), BF16), _sds((1, d), F32),
                   _sds((1, d), F32)],
        compiler_params=_cp(("arbitrary",)))(x2, target, xh, rs, g)


HPA_FWD = 8
HPA = 2


STRIP = 16


def _attn_scores(q_ref, k_refs, bias_ref, i, dh, hh):
    cols = pl.ds(hh * dh, dh)
    q = q_ref[:, cols] * (dh ** -0.5)
    k = jnp.concatenate([r[:, cols] for r in k_refs], axis=0)
    s = _dot_nt(q, k)
    before_start = lax.broadcasted_iota(jnp.int32, (STRIP, KW), 1) < (2 - i) * QB
    strips = []
    for r in range(0, QB, STRIP):
        ss = jnp.where(before_start, NEG_INF, s[r:r + STRIP] + bias_ref[hh, r:r + STRIP])
        e = jnp.exp(ss - jnp.max(ss, axis=-1, keepdims=True))
        strips.append(e / jnp.sum(e, axis=-1, keepdims=True))
    return q, k, strips


def _attn_specs(dh, off, hp):
    w = hp * dh
    qcol, kcol, vcol = off["aq"] // w, off["ak"] // w, off["av"] // w
    q_spec = pl.BlockSpec((QB, w), lambda g, i: (i, qcol + g))
    k_specs = [pl.BlockSpec((QB, w), functools.partial(lambda g, i, j: (jnp.maximum(i - 2 + j, 0), kcol + g), j=j))
               for j in range(3)]
    v_specs = [pl.BlockSpec((QB, w), functools.partial(lambda g, i, j: (jnp.maximum(i - 2 + j, 0), vcol + g), j=j))
               for j in range(3)]
    bias_spec = pl.BlockSpec((hp, QB, KW), lambda g, i: (g, 0, 0))
    return q_spec, k_specs, v_specs, bias_spec


def _attn_fwd(p, bias, d, off):
    t = p.shape[0]
    dh = d // ATTN_HEADS

    def body(q_ref, k0, k1, k2, v0, v1, v2, bias_ref, o_ref):
        for hh in range(HPA_FWD):
            cols = pl.ds(hh * dh, dh)
            _, _, strips = _attn_scores(q_ref, (k0, k1, k2), bias_ref, pl.program_id(1), dh, hh)
            pr = jnp.concatenate([_mx(ps) for ps in strips], axis=0)
            v = jnp.concatenate([v0[:, cols], v1[:, cols], v2[:, cols]], axis=0)
            o_ref[:, cols] = _dot(pr, v).astype(o_ref.dtype)

    q_spec, k_specs, v_specs, bias_spec = _attn_specs(dh, off, HPA_FWD)
    return pl.pallas_call(
        body, name="attn_fwd", grid=(ATTN_HEADS // HPA_FWD, t // QB),
        in_specs=[q_spec] + k_specs + v_specs + [bias_spec],
        out_specs=pl.BlockSpec((None, QB, HPA_FWD * dh), lambda g, i: (0, i, g)),
        out_shape=_sds((N_BRANCH, t, d), BF16),
        compiler_params=_cp(("parallel", "parallel")))(p, p, p, p, p, p, p, bias)


def _attn_bwd(p, bias, do, d, off):
    t = p.shape[0]
    dh = d // ATTN_HEADS

    def body(q_ref, k0, k1, k2, v0, v1, v2, bias_ref, do_ref, dq_ref, dk_ref, dv_ref, dbias_ref):
        i = pl.program_id(1)

        @pl.when(i == 0)
        def _():
            dk_ref[...] = jnp.zeros_like(dk_ref)
            dv_ref[...] = jnp.zeros_like(dv_ref)
            dbias_ref[...] = jnp.zeros_like(dbias_ref)

        def accumulate(acc_ref, cols, win):
            for first in range(2):
                @pl.when(i == first)
                def _():
                    acc_ref[:(first + 1) * QB, cols] += win[(2 - first) * QB:]

            @pl.when(i >= 2)
            def _():
                acc_ref[pl.ds(pl.multiple_of((i - 2) * QB, QB), KW), cols] += win

        for hh in range(HPA):
            cols = pl.ds(hh * dh, dh)
            q, k, strips = _attn_scores(q_ref, (k0, k1, k2), bias_ref, i, dh, hh)
            v = jnp.concatenate([v0[:, cols], v1[:, cols], v2[:, cols]], axis=0)
            dov = do_ref[:, cols]
            dp = _dot_nt(dov, v)
            ds_strips = []
            for n, ps in enumerate(strips):
                r = n * STRIP
                dps = dp[r:r + STRIP]
                dss = ps * (dps - jnp.sum(ps * dps, axis=-1, keepdims=True))
                dbias_ref[hh, r:r + STRIP] += dss
                ds_strips.append(_mx(dss))
            ds = jnp.concatenate(ds_strips, axis=0)
            pr = jnp.concatenate([_mx(ps) for ps in strips], axis=0)
            dq_ref[:, cols] = (_dot(ds, k) * (dh ** -0.5)).astype(dq_ref.dtype)
            accumulate(dk_ref, cols, _dot_tn(ds, q))
            accumulate(dv_ref, cols, _dot_tn(pr, dov))

    q_spec, k_specs, v_specs, bias_spec = _attn_specs(dh, off, HPA)
    row_spec = pl.BlockSpec((QB, HPA * dh), lambda g, i: (i, g))
    acc_spec = pl.BlockSpec((t, HPA * dh), lambda g, i: (0, g))
    return pl.pallas_call(
        body, name="attn_bwd", grid=(ATTN_HEADS // HPA, t // QB),
        in_specs=[q_spec] + k_specs + v_specs + [bias_spec,
                                                 pl.BlockSpec((None, QB, HPA * dh), lambda g, i: (0, i, g))],
        out_specs=[row_spec, acc_spec, acc_spec, bias_spec],
        out_shape=[_sds((t, d), BF16), _sds((t, d), F32), _sds((t, d), F32), _sds((ATTN_HEADS, QB, KW), F32)],
        compiler_params=_cp(("parallel", "arbitrary")))(p, p, p, p, p, p, p, bias, do)


def _onehot_mm(name, a, b):
    def body(a_ref, b_ref, o_ref):
        o_ref[...] = _dot_hi(a_ref[...], b_ref[...])

    return pl.pallas_call(body, name=name, out_shape=_sds((a.shape[0], b.shape[1]), F32),
                          compiler_params=pltpu.CompilerParams(vmem_limit_bytes=VMEM_LIMIT))(a, b)


def _diag_index():
    ii, jj = np.arange(CHUNK)[:, None], np.arange(CHUNK)[None, :]
    return (ii - jj + CHUNK - 1).reshape(-1)


def _bias_expand(rel_bias):
    h = rel_bias.shape[0]
    nq, nk, shift = QB // CHUNK, KW // CHUNK, (2 * QB) // CHUNK
    nbin, ndc = 3 * LANE, 4
    rb = jnp.pad(rel_bias, ((0, 0), (0, nbin - rel_bias.shape[1])))
    win = np.clip(CHUNK * np.arange(ndc)[:, None] + np.arange(LANE)[None, :] - (CHUNK - 1), -REL_CLIP, REL_CLIP)
    sel = (jnp.arange(nbin)[:, None] == jnp.asarray((win + REL_CLIP).reshape(1, -1))).astype(F32)
    windows = _onehot_mm("bias_windows", rb, sel)
    diag_t = (jnp.arange(LANE)[:, None] == jnp.asarray(_diag_index().reshape(1, -1))).astype(F32)
    blocks = _onehot_mm("bias_blocks", windows.reshape(h * ndc, LANE), diag_t).reshape(h, ndc, CHUNK, CHUNK)
    off_band = jnp.full((h, CHUNK, CHUNK), NEG_INF, F32)
    rows = []
    for ic in range(nq):
        dcs = [ic - jc + shift for jc in range(nk)]
        rows.append(jnp.concatenate([blocks[:, min(dc, ndc - 1)] if 0 <= dc <= ATTN_LEFT else off_band
                                     for dc in dcs], axis=2))
    return jnp.concatenate(rows, axis=1)


def _bias_reduce(dbias):
    h = dbias.shape[0]
    nq, nk = QB // CHUNK, KW // CHUNK
    nbin = 3 * LANE
    blocks = dbias.reshape(h, nq, CHUNK, nk, CHUNK).transpose(0, 1, 3, 2, 4).reshape(h * nq * nk, CHUNK * CHUNK)
    diag = (jnp.asarray(_diag_index().reshape(-1, 1)) == jnp.arange(LANE)[None, :]).astype(F32)
    ic = np.arange(nq)[:, None, None]
    jc = np.arange(nk)[None, :, None]
    dl = np.arange(LANE)[None, None, :] - (CHUNK - 1)
    rel = np.clip(CHUNK * (ic - jc + (2 * QB) // CHUNK) + dl, -REL_CLIP, REL_CLIP) + REL_CLIP
    bins = (jnp.asarray(rel.reshape(-1, 1)) == jnp.arange(nbin)[None, :]).astype(F32)

    diags = _onehot_mm("bias_diag_sums", blocks, diag)
    out = _onehot_mm("bias_bin_sums", diags.reshape(h, nq * nk * LANE), bins)
    return out[:, :2 * REL_CLIP + 1]


def _chunk_masks():
    r = lax.broadcasted_iota(jnp.int32, (LB, LB), 0)
    c = lax.broadcasted_iota(jnp.int32, (LB, LB), 1)
    return (r // CHUNK) == (c // CHUNK), r >= c, r <= c


def _chunks(a):
    return [a[c * CHUNK:(c + 1) * CHUNK] for c in range(LB // CHUNK)]


def _per_chunk(a, f):
    return jnp.concatenate([jnp.broadcast_to(f(c), c.shape) for c in _chunks(a)], axis=0)


def _dot_sel(sel, x):
    def top(v):
        return lax.bitcast_convert_type(lax.bitcast_convert_type(v, jnp.int32) & jnp.int32(-65536), F32)

    hi = top(x)
    mid = top(x - hi)
    lo = (x - hi) - mid
    d = functools.partial(jnp.dot, sel.astype(jnp.bfloat16), preferred_element_type=F32)
    return d(hi.astype(jnp.bfloat16)) + d(mid.astype(jnp.bfloat16)) + d(lo.astype(jnp.bfloat16))


def _lin_block(gla, q, k, v, aux):
    dk = q.shape[-1]
    same, low, up = _chunk_masks()
    ones = same.astype(F32)
    if gla:
        glr, wlr, blr = aux
        q = q * (dk ** -0.5)
        pre = _dot(glr, wlr) + blr
        log_a = (jnp.minimum(pre, 0.0) - jnp.log(1.0 + jnp.exp(-jnp.abs(pre)))) / GATE_NORM
        b = _dot_sel(jnp.where(low, ones, 0.0), log_a)
        lastb = _per_chunk(b, lambda c: c[CHUNK - 1:])
    else:
        cs, sn, lg = aux
        pre = None
        half = dk // 2
        q = q * cs + pltpu.roll(q, half, 1) * sn
        k = (k * cs + pltpu.roll(k, half, 1) * sn) * (dk ** -0.5)
        pos = (lax.broadcasted_iota(jnp.int32, (LB, dk), 0) % CHUNK).astype(F32) + 1.0
        b = pos * lg
        lastb = jnp.broadcast_to(float(CHUNK) * lg, b.shape)
    eb, enb, el, dec = jnp.exp(b), jnp.exp(-b), jnp.exp(lastb - b), jnp.exp(lastb)
    qf, kf, qb, kb, kl = q * eb, k * enb, q * enb, k * eb, k * el
    s = jnp.where(same, jnp.where(low, _dot_nt(qf, kf), _dot_nt(qb, kb)), 0.0)
    return dict(pre=pre, eb=eb, enb=enb, el=el, dec=dec, qf=qf, kf=kf, qb=qb, kb=kb, kl=kl, s=s,
                same=same, low=low, up=up, ones=ones)


def _lin_norm_gate(gla, o, gate, gn):
    sg = _sigmoid(gate)
    silu = gate * sg
    if gla:
        r = lax.rsqrt(jnp.mean(o * o, axis=-1, keepdims=True) + LN_EPS)
        hn = o * r
        return silu * (hn * gn), (sg, silu, r, hn)
    mu = jnp.mean(o, axis=-1, keepdims=True)
    oc = o - mu
    r = lax.rsqrt(jnp.mean(oc * oc, axis=-1, keepdims=True) + LN_EPS)
    hn = oc * r
    return silu * hn, (sg, silu, r, hn)


HPS = 4


def _lin_specs(gla, dk, dv, off, rev, nb):
    pre = "g" if gla else "r"
    wk, wv = HPS * dk, HPS * dv
    qc, kc, vc, gc = (off[pre + "q"] // wk, off[pre + "k"] // wk, off[pre + "v"] // wv, off[pre + "g"] // wv)

    def blk(i):
        return nb - 1 - i if rev else i

    specs = [pl.BlockSpec((LB, wk), lambda g, i: (blk(i), qc + g)),
             pl.BlockSpec((LB, wk), lambda g, i: (blk(i), kc + g)),
             pl.BlockSpec((LB, wv), lambda g, i: (blk(i), vc + g)),
             pl.BlockSpec((LB, wv), lambda g, i: (blk(i), gc + g))]
    if gla:
        specs += [pl.BlockSpec((LB, LANE), lambda g, i: (blk(i), off["glr"] // LANE)),
                  pl.BlockSpec((LANE, wk), lambda g, i: (0, g)),
                  pl.BlockSpec((1, wk), lambda g, i: (0, g)),
                  pl.BlockSpec((1, dv), lambda g, i: (0, 0))]
    else:
        specs += [pl.BlockSpec((LB, dk), lambda g, i: (blk(i), 0)),
                  pl.BlockSpec((LB, dk), lambda g, i: (blk(i), 0)),
                  pl.BlockSpec((HPS, 1, dk), lambda g, i: (g, 0, 0))]
    return specs, blk


def _lin_aux(gla, refs, rows, hh, dk):
    if gla:
        glr_ref, wlr_ref, blr_ref, gn_ref = refs
        kcols = pl.ds(hh * dk, dk)
        return (glr_ref[rows, :], wlr_ref[:, kcols], blr_ref[:, kcols]), gn_ref[...]
    cs_ref, sn_ref, lg_ref = refs
    return (cs_ref[rows, :], sn_ref[rows, :], lg_ref[hh]), None


def _lin_fwd(gla, p, aux_arrays, d, off, branches, slot):
    t = p.shape[0]
    dk, dv = d // (2 * LIN_HEADS), d // LIN_HEADS
    nb, cb = t // LB, LB // CHUNK
    naux = len(aux_arrays)

    def body(*refs):
        q_ref, k_ref, v_ref, g_ref = refs[:4]
        aux_refs = refs[4:4 + naux]
        o_ref, bo_ref, st_out_ref, st_ref = refs[5 + naux:]

        @pl.when(pl.program_id(1) == 0)
        def _():
            st_ref[...] = jnp.zeros_like(st_ref)

        rows = slice(None)
        for hh in range(HPS):
            kcols, vcols = pl.ds(hh * dk, dk), pl.ds(hh * dv, dv)
            aux, gn = _lin_aux(gla, aux_refs, rows, hh, dk)
            v = v_ref[:, vcols]
            blk = _lin_block(gla, q_ref[:, kcols], k_ref[:, kcols], v, aux)
            st = st_ref[hh]
            inter = []
            for c, (qf, kl, dec, vc) in enumerate(zip(_chunks(blk["qf"]), _chunks(blk["kl"]), _chunks(blk["dec"]),
                                                      _chunks(v))):
                st_out_ref[hh, c] = st
                inter.append(_dot_nt(qf, st))
                st = st * dec[:1] + _dot_tn(vc, kl)
            st_ref[hh] = st
            o = _dot(blk["s"], v) + jnp.concatenate(inter, axis=0)
            o_ref[:, vcols] = o
            out, _ = _lin_norm_gate(gla, o, g_ref[:, vcols], gn)
            bo_ref[:, vcols] = out.astype(BF16)

    specs, _ = _lin_specs(gla, dk, dv, off, False, nb)
    orow = pl.BlockSpec((LB, HPS * dv), lambda g, i: (i, g))
    return pl.pallas_call(
        body, name="gla_fwd" if gla else "ret_fwd", grid=(LIN_HEADS // HPS, nb), in_specs=specs + [ANY],
        out_specs=[orow, pl.BlockSpec((None, LB, HPS * dv), lambda g, i: (slot, i, g)),
                   pl.BlockSpec((HPS, cb, dv, dk), lambda g, i: (g, i, 0, 0))],
        out_shape=[_sds((t, d), F32), _sds(branches.shape, BF16), _sds((LIN_HEADS, t // CHUNK, dv, dk), F32)],
        scratch_shapes=[pltpu.VMEM((HPS, dv, dk), F32)], input_output_aliases={4 + naux: 1},
        compiler_params=_cp(("parallel", "arbitrary")))(p, p, p, p, *aux_arrays, branches)


def _lin_bwd(gla, p, aux_arrays, o, states, dbo, slot, d, off):
    t = p.shape[0]
    dk, dv = d // (2 * LIN_HEADS), d // LIN_HEADS
    nb, cb = t // LB, LB // CHUNK
    naux = len(aux_arrays)

    def body(*refs):
        q_ref, k_ref, v_ref, g_ref = refs[:4]
        aux_refs = refs[4:4 + naux]
        o_ref, st_in_ref, dbo_ref = refs[4 + naux:7 + naux]
        outs = refs[7 + naux:]
        dq_ref, dk_ref, dv_ref, dg_ref = outs[:4]
        dst_ref = outs[-1]
        first = pl.program_id(1) == 0

        @pl.when(first)
        def _():
            dst_ref[...] = jnp.zeros_like(dst_ref)

        if gla:
            dpre_ref, dblr_ref, dgn_ref = outs[4:7]

            @pl.when(first)
            def _():
                dblr_ref[...] = jnp.zeros_like(dblr_ref)
                dgn_ref[...] = jnp.zeros_like(dgn_ref)

        rows = slice(None)
        for hh in range(HPS):
            kcols, vcols = pl.ds(hh * dk, dk), pl.ds(hh * dv, dv)
            aux, gn = _lin_aux(gla, aux_refs, rows, hh, dk)
            v = v_ref[:, vcols]
            bk = _lin_block(gla, q_ref[:, kcols], k_ref[:, kcols], v, aux)
            eb, enb, el, dec = bk["eb"], bk["enb"], bk["el"], bk["dec"]
            qf, kf, qb, kb, kl, s = bk["qf"], bk["kf"], bk["qb"], bk["kb"], bk["kl"], bk["s"]
            gate = g_ref[:, vcols]
            dout = dbo_ref[:, vcols]
            _, (sg, silu, r, hn) = _lin_norm_gate(gla, o_ref[:, vcols], gate, gn)
            dsilu = sg * (1.0 + gate * (1.0 - sg))
            if gla:
                y = hn * gn
                dy = dout * silu
                dg_ref[:, vcols] = (dout * y * dsilu).astype(BF16)
                dgn_ref[hh] += jnp.sum(dy * hn, axis=0, keepdims=True)
                dhn = dy * gn
                do = r * (dhn - hn * jnp.mean(dhn * hn, axis=-1, keepdims=True))
            else:
                dhn = dout * silu
                dg_ref[:, vcols] = (dout * hn * dsilu).astype(BF16)
                do = r * (dhn - jnp.mean(dhn, axis=-1, keepdims=True)
                          - hn * jnp.mean(dhn * hn, axis=-1, keepdims=True))
            ds = jnp.where(bk["same"], _dot_nt(do, v), 0.0)
            dsf = jnp.where(bk["low"], ds, 0.0)
            dsb = ds - dsf
            dvv = _dot_tn(s, do)
            dqf = _dot(dsf, kf)
            dkf = _dot_tn(dsf, qf)
            dqb = _dot(dsb, kb)
            dkb = _dot_tn(dsb, qb)
            dst = dst_ref[hh]
            dv_st, dqf_st, dkl_c, ddec_c = [], [], [], []
            parts = zip(reversed(range(cb)), reversed(_chunks(do)), reversed(_chunks(v)), reversed(_chunks(qf)),
                        reversed(_chunks(kl)), reversed(_chunks(dec)))
            for c, do_c, v_c, qf_c, kl_c, dec_c in parts:
                st = st_in_ref[hh, c]
                dv_st.append(_dot_nt(kl_c, dst))
                dkl_c.append(_dot(v_c, dst))
                dqf_st.append(_dot(do_c, st))
                ddec_c.append(jnp.broadcast_to(jnp.sum(dst * st, axis=0, keepdims=True), (CHUNK, dk)))
                dst = dst * dec_c[:1] + _dot_tn(do_c, qf_c)
            dst_ref[hh] = dst

            def cat(pieces):
                return jnp.concatenate(pieces[::-1], axis=0)

            dvv = dvv + cat(dv_st)
            dqf = dqf + cat(dqf_st)
            dkl = cat(dkl_c)
            dq = dqf * eb + dqb * enb
            dkk = dkf * enb + dkb * eb + dkl * el
            dv_ref[:, vcols] = dvv.astype(BF16)
            if gla:
                db = dqf * qf - dkf * kf - dqb * qb + dkb * kb - dkl * kl
                dlast = _per_chunk(dkl * kl, lambda c: jnp.sum(c, axis=0, keepdims=True)) + cat(ddec_c) * dec
                dlog_a = _dot_sel(jnp.where(bk["up"], bk["ones"], 0.0), db) + dlast
                dpre = dlog_a * (1.0 / GATE_NORM) * (1.0 - _sigmoid(bk["pre"]))
                dpre_ref[:, kcols] = dpre
                dblr_ref[hh] += jnp.sum(dpre, axis=0, keepdims=True)
                dq_ref[:, kcols] = (dq * (dk ** -0.5)).astype(BF16)
                dk_ref[:, kcols] = dkk.astype(BF16)
            else:
                cs, sn, _ = aux
                half = dk // 2
                dkk = dkk * (dk ** -0.5)
                dq_ref[:, kcols] = (dq * cs + pltpu.roll(dq * sn, half, 1)).astype(BF16)
                dk_ref[:, kcols] = (dkk * cs + pltpu.roll(dkk * sn, half, 1)).astype(BF16)

    specs, blk = _lin_specs(gla, dk, dv, off, True, nb)
    vrow = pl.BlockSpec((LB, HPS * dv), lambda g, i: (blk(i), g))
    krow = pl.BlockSpec((LB, HPS * dk), lambda g, i: (blk(i), g))
    specs += [vrow, pl.BlockSpec((HPS, cb, dv, dk), lambda g, i: (g, blk(i), 0, 0)),
              pl.BlockSpec((None, LB, HPS * dv), lambda g, i: (slot, blk(i), g))]
    out_specs = [krow, krow, vrow, vrow]
    out_shape = [_sds((t, d // 2), BF16), _sds((t, d // 2), BF16), _sds((t, d), BF16), _sds((t, d), BF16)]
    if gla:
        out_specs += [krow, pl.BlockSpec((HPS, 1, dk), lambda g, i: (g, 0, 0)),
                      pl.BlockSpec((HPS, 1, dv), lambda g, i: (g, 0, 0))]
        out_shape += [_sds((t, d // 2), F32), _sds((LIN_HEADS, 1, dk), F32), _sds((LIN_HEADS, 1, dv), F32)]
    out_specs.append(pl.BlockSpec((HPS, dv, dk), lambda g, i: (g, 0, 0)))
    out_shape.append(_sds((LIN_HEADS, dv, dk), F32))
    res = pl.pallas_call(
        body, name="gla_bwd" if gla else "ret_bwd", grid=(LIN_HEADS // HPS, nb), in_specs=specs,
        out_specs=out_specs, out_shape=out_shape,
        compiler_params=_cp(("parallel", "arbitrary")))(p, p, p, p, *aux_arrays, o, states, dbo)
    return res[:-1]


def _row_tile(rows, cols):
    cap = max(8, (2 << 20) // (4 * cols))
    t = rows
    while t > cap and t % 2 == 0:
        t //= 2
    return t


def _add_half(name, g, t, sel):
    nchip, hr, cols = t.shape
    tr = _row_tile(hr, cols)
    nb = hr // tr

    def body(sel_ref, g_ref, t_ref, o_ref):
        o_ref[...] = g_ref[...] + t_ref[...]

    half = pl.BlockSpec((None, tr, cols), lambda p, i, s: (p, i, 0))
    gs = pltpu.PrefetchScalarGridSpec(
        num_scalar_prefetch=1, grid=(nchip, nb),
        in_specs=[pl.BlockSpec((None, tr, cols), lambda p, i, s: (p, s[0] * nb + i, 0)), half], out_specs=half)
    return pl.pallas_call(body, name=name, grid_spec=gs, out_shape=_sds(t.shape, F32),
                          compiler_params=_cp(("parallel", "parallel")))(sel, g, t)


def _sum_shards(name, h, rcv, sel):
    _, rows, cols = h.shape
    tr = _row_tile(rows, cols)

    def body(sel_ref, h_ref, r0, r1, r2, o_ref):
        o_ref[...] = ((h_ref[...] + r0[...]) + r1[...]) + r2[...]

    rspecs = [pl.BlockSpec((None, tr, cols), functools.partial(lambda i, s, j: (j, i, 0), j=j)) for j in range(3)]
    gs = pltpu.PrefetchScalarGridSpec(
        num_scalar_prefetch=1, grid=(rows // tr,),
        in_specs=[pl.BlockSpec((None, tr, cols), lambda i, s: (s[0], i, 0))] + rspecs,
        out_specs=pl.BlockSpec((tr, cols), lambda i, s: (i, 0)))
    return pl.pallas_call(body, name=name, grid_spec=gs, out_shape=_sds((rows, cols), F32),
                          compiler_params=_cp(("parallel",)))(sel, h, rcv, rcv, rcv)


def _adamw_math(w, g, m, v):
    c1 = 1.0 - ADAM_B1 ** ADAM_STEP
    c2 = 1.0 - ADAM_B2 ** ADAM_STEP
    nm = ADAM_B1 * m + (1.0 - ADAM_B1) * g
    nv = ADAM_B2 * v + (1.0 - ADAM_B2) * jnp.square(g)
    return -ADAM_LR * ((nm / c1) / (jnp.sqrt(nv / c2) + ADAM_EPS) + ADAM_WD * w), nm, nv


def _adamw(name, w, g, m, v):
    rows, cols = w.shape
    tr = _row_tile(rows, cols)

    def body(w_ref, g_ref, m_ref, v_ref, d_ref, nm_ref, nv_ref):
        d_ref[...], nm_ref[...], nv_ref[...] = _adamw_math(w_ref[...], g_ref[...], m_ref[...], v_ref[...])

    spec = pl.BlockSpec((tr, cols), lambda i: (i, 0))
    return pl.pallas_call(body, name=name, grid=(rows // tr,), in_specs=[spec] * 4, out_specs=[spec] * 3,
                          out_shape=[_sds((rows, cols), F32)] * 3, compiler_params=_cp(("parallel",)))(w, g, m, v)


def _adamw_layer(name, w, g_own, g_sib, sel, m, v, layer, prev):
    depth, rows, cols = w.shape
    tr = _row_tile(rows // 2, cols)
    nbh = rows // 2 // tr
    nprev = 0 if prev is None else 4

    def body(sel_ref, w_ref, own_ref, sib_ref, m_ref, v_ref, *rest):
        go_ref, d_ref, nm_ref, nv_ref = rest[nprev:]
        gv = jnp.where(pl.program_id(0) // nbh == sel_ref[0], own_ref[...], sib_ref[...])
        go_ref[...] = gv
        d_ref[...], nm_ref[...], nv_ref[...] = _adamw_math(w_ref[...], gv, m_ref[...], v_ref[...])

    lay = pl.BlockSpec((None, tr, cols), lambda i, s: (layer, i, 0))
    hlf = pl.BlockSpec((tr, cols), lambda i, s: (i % nbh, 0))
    gs = pltpu.PrefetchScalarGridSpec(
        num_scalar_prefetch=1, grid=(2 * nbh,), in_specs=[lay, hlf, hlf, lay, lay] + [ANY] * nprev,
        out_specs=[lay] * 4)
    args = (sel, w, g_own, g_sib, m, v) + (() if prev is None else tuple(prev))
    return pl.pallas_call(
        body, name=name, grid_spec=gs, out_shape=[_sds((depth, rows, cols), F32)] * 4,
        input_output_aliases={6 + k: k for k in range(nprev)},
        compiler_params=_cp(("parallel",)))(*args)


def _adamw_colmajor(name, wt, mt, vt, halves, sel):
    c_dim, depth, r_dim = wt.shape
    hr = r_dim // 2

    def body(sel_ref, w_ref, m_ref, v_ref, *rest):
        g_refs, (go_ref, d_ref, nm_ref, nv_ref) = rest[:2 * depth], rest[2 * depth:]
        own_first = sel_ref[0] == 0
        for l in range(depth):
            own, sib = g_refs[2 * l][...], g_refs[2 * l + 1][...]
            g = jnp.concatenate([jnp.where(own_first, own, sib), jnp.where(own_first, sib, own)], axis=0).T
            go_ref[:, l, :] = g
            d_ref[:, l, :], nm_ref[:, l, :], nv_ref[:, l, :] = _adamw_math(w_ref[:, l, :], g, m_ref[:, l, :],
                                                                          v_ref[:, l, :])

    col = pl.BlockSpec((LANE, depth, r_dim), lambda j, s: (j, 0, 0))
    gs = pltpu.PrefetchScalarGridSpec(
        num_scalar_prefetch=1, grid=(c_dim // LANE,),
        in_specs=[col] * 3 + [pl.BlockSpec((hr, LANE), lambda j, s: (0, j))] * (2 * depth), out_specs=[col] * 4)
    flat = [h for pair in halves for h in pair]
    return pl.pallas_call(body, name=name, grid_spec=gs, out_shape=[_sds(wt.shape, F32)] * 4,
                          compiler_params=_cp(("parallel",)))(sel, wt, mt, vt, *flat)


def _adamw_tail(name, wt, mt, vt, gt_tail, prev):
    c_dim, depth, r_dim = wt.shape
    nt = gt_tail.shape[0]

    def body(w_ref, m_ref, v_ref, g_ref, *rest):
        go_ref, d_ref, nm_ref, nv_ref = rest[4:]
        g = g_ref[...]
        go_ref[...] = g
        d_ref[...], nm_ref[...], nv_ref[...] = _adamw_math(w_ref[...], g, m_ref[...], v_ref[...])

    tail = pl.BlockSpec((nt, depth, r_dim), lambda i: (c_dim // nt - 1, 0, 0))
    return pl.pallas_call(
        body, name=name, grid=(1,), in_specs=[tail] * 3 + [pl.BlockSpec((nt, depth, r_dim), lambda i: (0, 0, 0))]
        + [ANY] * 4, out_specs=[tail] * 4, out_shape=[_sds(wt.shape, F32)] * 4,
        input_output_aliases={4 + k: k for k in range(4)},
        compiler_params=_cp(("arbitrary",)))(wt, mt, vt, gt_tail, *prev)


def _place():
    x, y, c = (lax.axis_index(a) for a in MESH_AXES)
    chips = [(1 - x, y), (x, 1 - y), (1 - x, 1 - y)]
    return x, y, c, chips


def _chip_index(xy):
    return 2 * xy[0] + xy[1]


ANY = pl.BlockSpec(memory_space=pl.ANY)


HBM_SPEC = pl.BlockSpec(memory_space=pltpu.HBM)
SEM = pl.BlockSpec(memory_space=pltpu.SEMAPHORE)
EFFECT = pltpu.SideEffectType.DATAFLOW_SIDE_EFFECTING


def _half(ref, c):
    hr = ref.shape[-2] // 2
    return pl.ds(pl.multiple_of(c * hr, 16), hr)


def _gather_copies(srcs, lands, send, recv):
    x, y, c, chips = _place()
    me = _chip_index((x, y))
    return [pltpu.make_async_remote_copy(src_ref=s.at[_half(s, c)], dst_ref=g.at[me, _half(s, c)],
                                         send_sem=send.at[3 * a + j], recv_sem=recv.at[3 * a + j],
                                         device_id=(*ch, c), device_id_type=DEV)
            for a, (s, g) in enumerate(zip(srcs, lands)) for j, ch in enumerate(chips)]


def _scatter_copies(srcs, lands, send, recv):
    x, y, c, chips = _place()
    return [pltpu.make_async_remote_copy(src_ref=h.at[_chip_index(ch)], dst_ref=r.at[j],
                                         send_sem=send.at[3 * a + j], recv_sem=recv.at[3 * a + j],
                                         device_id=(*ch, c), device_id_type=DEV)
            for a, (h, r) in enumerate(zip(srcs, lands)) for j, ch in enumerate(chips)]


def _in_hbm(a):
    return pltpu.with_memory_space_constraint(a, pltpu.HBM)


def _split_start(name, srcs, land_shapes, copies_fn, after=None, per_src=3):
    ns, nl = len(srcs), len(land_shapes)
    ncp = per_src * ns
    lands = [lax.empty(s.shape, s.dtype) for s in land_shapes]
    behind = [] if after is None else [after]

    def body(*refs):
        src, land = refs[:ns], refs[ns:ns + nl]
        send, recv = refs[ns + nl + len(behind)], refs[ns + nl + len(behind) + 1]
        for cp in copies_fn(src, land, send, recv):
            cp.start()
        refs[-1][...] = jnp.zeros_like(refs[-1])

    bufs = list(srcs) + lands
    outs = pl.pallas_call(
        body, name=name, in_specs=[HBM_SPEC] * (ns + nl) + [ANY] * len(behind),
        out_specs=[SEM, SEM] + [HBM_SPEC] * (ns + nl) + [pl.BlockSpec(memory_space=pltpu.VMEM)],
        out_shape=[pltpu.SemaphoreType.DMA((ncp,)), pltpu.SemaphoreType.DMA((ncp,))]
        + [pltpu.HBM(b.shape, b.dtype) for b in bufs] + [_sds((8, LANE), F32)],
        input_output_aliases={i: 2 + i for i in range(ns + nl)},
        compiler_params=pltpu.CompilerParams(has_side_effects=EFFECT))(*[_in_hbm(b) for b in bufs], *behind)
    return outs[0], outs[1], list(outs[2:2 + ns]), list(outs[2 + ns:2 + ns + nl]), outs[-1]


def _split_wait(name, started, copies_fn, after):
    send, recv, srcs, lands, _ = started
    ns, nl = len(srcs), len(lands)

    def body(*refs):
        src, land = refs[:ns], refs[ns:ns + nl]
        for cp in copies_fn(src, land, refs[ns + nl], refs[ns + nl + 1]):
            cp.wait_send()
            cp.wait_recv()

    bufs = list(srcs) + list(lands)
    outs = pl.pallas_call(
        body, name=name, in_specs=[HBM_SPEC] * (ns + nl) + [SEM, SEM, ANY], out_specs=[HBM_SPEC] * (ns + nl),
        out_shape=[pltpu.HBM(b.shape, b.dtype) for b in bufs],
        input_output_aliases={i: i for i in range(ns + nl)},
        compiler_params=pltpu.CompilerParams(has_side_effects=EFFECT))(*bufs, send, recv, after)
    return list(outs[:ns]), list(outs[ns:])


def _gather_plain(name, srcs):
    n = len(srcs)

    def body(*refs):
        src, land = refs[:n], refs[n:2 * n]
        send, recv, fsend, frecv = refs[2 * n:]
        first = _gather_copies(src, land, send, recv)
        for cp in first:
            cp.start()
        _forward_body(land, first, fsend, frecv)

    return pl.pallas_call(
        body, name=name, in_specs=[ANY] * n, out_specs=[ANY] * n,
        out_shape=[_sds((4,) + s.shape, s.dtype) for s in srcs],
        scratch_shapes=[pltpu.SemaphoreType.DMA((3 * n,))] * 4)(*srcs)


def _forward_body(land, arrivals, fsend, frecv):
    x, y, c, chips = _place()
    n = len(land)
    passed = []
    for a in range(n):
        for j, ch in enumerate(chips):
            if arrivals is not None:
                arrivals[3 * a + j].wait_recv()
            slot = land[a].at[_chip_index(ch), _half(land[a], c)]
            fw = pltpu.make_async_remote_copy(src_ref=slot, dst_ref=slot, send_sem=fsend.at[3 * a + j],
                                              recv_sem=frecv.at[3 * a + j], device_id=(x, y, 1 - c),
                                              device_id_type=DEV)
            fw.start()
            passed.append(fw)
    for a in range(n):
        for j, ch in enumerate(chips):
            slot = land[a].at[_chip_index(ch), _half(land[a], 1 - c)]
            pltpu.make_async_remote_copy(src_ref=slot, dst_ref=slot, send_sem=fsend.at[3 * a + j],
                                         recv_sem=frecv.at[3 * a + j], device_id=(x, y, c),
                                         device_id_type=DEV).wait_recv()
    for cp in passed:
        cp.wait_send()
    if arrivals is not None:
        for cp in arrivals:
            cp.wait_send()


def _gather_forward(name, lands):
    n = len(lands)

    def body(*refs):
        _forward_body(refs[n:2 * n], None, refs[2 * n], refs[2 * n + 1])

    return pl.pallas_call(
        body, name=name, in_specs=[ANY] * n, out_specs=[ANY] * n,
        out_shape=[_sds(g.shape, g.dtype) for g in lands], input_output_aliases={a: a for a in range(n)},
        scratch_shapes=[pltpu.SemaphoreType.DMA((3 * n,))] * 2)(*lands)


def _sibling_copies(srcs, lands, send, recv):
    x, y, c, _ = _place()
    return [pltpu.make_async_remote_copy(src_ref=g.at[:, _half(g, 1 - c)], dst_ref=t, send_sem=send.at[a],
                                         recv_sem=recv.at[a], device_id=(x, y, 1 - c), device_id_type=DEV)
            for a, (g, t) in enumerate(zip(srcs, lands))]


def _sibling_share(name, sms):
    n = len(sms)

    def body(*refs):
        ins, outs = refs[:n], refs[n:2 * n]
        send, recv = refs[2 * n:]
        x, y, c, _ = _place()
        cps = [pltpu.make_async_remote_copy(src_ref=ins[a], dst_ref=outs[a], send_sem=send.at[a],
                                            recv_sem=recv.at[a], device_id=(x, y, 1 - c), device_id_type=DEV)
               for a in range(n)]
        for cp in cps:
            cp.start()
        for cp in cps:
            cp.wait()

    return pl.pallas_call(
        body, name=name, in_specs=[ANY] * n, out_specs=[ANY] * n, out_shape=[_sds(s.shape, F32) for s in sms],
        scratch_shapes=[pltpu.SemaphoreType.DMA((n,))] * 2)(*sms)


def _small_allreduce(v, after=None):
    rows = v.shape[0]
    ndev = 8
    behind = [] if after is None else [after]

    def body(v_ref, *rest):
        o_ref, gat_ref, send, recv = rest[len(behind):]
        x, y, c, _ = _place()
        me = 4 * x + 2 * y + c
        cps = []
        for k in range(1, ndev):
            to = (me + k) % ndev
            cp = pltpu.make_async_remote_copy(src_ref=v_ref, dst_ref=gat_ref.at[me], send_sem=send.at[k - 1],
                                              recv_sem=recv.at[me], device_id=(to // 4, (to // 2) % 2, to % 2),
                                              device_id_type=DEV)
            cp.start()
            cps.append(cp)
        gat_ref[me] = v_ref[...]
        for k in range(1, ndev):
            frm = (me + k) % ndev
            pltpu.make_async_remote_copy(src_ref=v_ref, dst_ref=gat_ref.at[frm], send_sem=send.at[k - 1],
                                         recv_sem=recv.at[frm], device_id=(x, y, c), device_id_type=DEV).wait_recv()
        for cp in cps:
            cp.wait_send()
        acc = gat_ref[0]
        for k in range(1, ndev):
            acc = acc + gat_ref[k]
        o_ref[...] = acc

    vm = pl.BlockSpec(memory_space=pltpu.VMEM)
    return pl.pallas_call(
        body, name="small_allreduce", in_specs=[vm] + [ANY] * len(behind), out_specs=vm,
        out_shape=_sds((rows, LANE), F32),
        scratch_shapes=[pltpu.VMEM((ndev, rows, LANE), F32), pltpu.SemaphoreType.DMA((ndev - 1,)),
                        pltpu.SemaphoreType.DMA((ndev,))])(v, *behind)


def _layout(d):
    half = d // 2
    names = [("aq", d), ("ak", d), ("av", d), ("rq", half), ("rk", half), ("rv", d), ("rg", d),
             ("gq", half), ("gk", half), ("gv", d), ("gg", d), ("gates", 3 * d), ("glr", 2 * LANE)]
    off, pos = {}, 0
    for nm, sz in names:
        off[nm] = pos
        pos += sz
    return off, pos


def _unpad_cols(g, d):
    a = 8 * d + d
    return jnp.concatenate([g[..., :a], g[..., a + 3 * d:a + 3 * d + GATE_RANK], g[..., a:a + 3 * d]], axis=-1)


def kernel(x, ln_in_g, ln_in_b, w_in, rel_bias, gla_w_lr, gla_b_lr, gla_norm_g, w_branch, w_out, ln1_g, ln1_b, w_up, w_down, ln2_g, ln2_b, loss_target, m_ln_in_g, m_ln_in_b, m_w_in, m_rel_bias, m_gla_w_lr, m_gla_b_lr, m_gla_norm_g, m_w_branch, m_w_out, m_ln1_g, m_ln1_b, m_w_up, m_w_down, m_ln2_g, m_ln2_b, v_ln_in_g, v_ln_in_b, v_w_in, v_rel_bias, v_gla_w_lr, v_gla_b_lr, v_gla_norm_g, v_w_branch, v_w_out, v_ln1_g, v_ln1_b, v_w_up, v_w_down, v_ln2_g, v_ln2_b):
    t, d = x.shape[1], x.shape[2]
    dff = 4 * d
    half = d // 2
    off, npad = _layout(d)
    xi, yi, ci = (lax.axis_index(a) for a in MESH_AXES)
    chip = 2 * xi + yi
    csel = jnp.reshape(ci, (1,)).astype(jnp.int32)
    psel = jnp.reshape(chip, (1,)).astype(jnp.int32)

    big_w = [w_in, w_branch.reshape(DEPTH, -1, d), w_out, w_up, w_down]
    big_m = [m_w_in, m_w_branch.reshape(DEPTH, -1, d), m_w_out, m_w_up, m_w_down]
    big_v = [v_w_in, v_w_branch.reshape(DEPTH, -1, d), v_w_out, v_w_up, v_w_down]
    W_IN, REST = [0], [1, 2, 3, 4]

    def shards_of(l, idx):
        return [big_w[i][l].astype(BF16) for i in idx]

    def lands_of(srcs):
        return [_sds((4,) + s.shape, s.dtype) for s in srcs]

    def full_w_in(g):
        per = g.shape[2]
        a = 8 * d + d

        def run(lo, hi):
            cuts = [(max(lo, c * per), min(hi, (c + 1) * per), c) for c in range(4)]
            return [g[c, :, x - c * per:y - c * per] for x, y, c in cuts if x < y]

        zeros = jnp.zeros((d, 2 * LANE - GATE_RANK), g.dtype)
        return jnp.concatenate(run(0, a) + run(a + GATE_RANK, 4 * per) + run(a, a + GATE_RANK) + [zeros], axis=1)

    def full_rest(gs):
        g_br, g_out, g_up, g_down = gs
        return (jnp.transpose(g_br.reshape(4, N_BRANCH, d // 4, d), (1, 0, 2, 3)).reshape(N_BRANCH, d, d),
                g_out.reshape(d, d), jnp.transpose(g_up, (1, 0, 2)).reshape(d, dff), g_down.reshape(dff, d))

    def with_own(srcs, lands):
        return [lax.dynamic_update_slice(g, s[None], (chip, 0, 0)) for s, g in zip(srcs, lands)]

    def gather_start(tag, l, idx, after):
        srcs = shards_of(l, idx)
        return srcs, _split_start(f"gather_{tag}{l}_start", srcs, lands_of(srcs), _gather_copies, after)

    def gather_finish(tag, l, pending, after):
        srcs, started = pending
        _, lands = _split_wait(f"gather_{tag}{l}_wait", started, _gather_copies, after)
        return with_own(srcs, _gather_forward(f"gather_{tag}{l}_pass", lands))

    def token(pending):
        return pending[1][4][0, 0]

    win, wbr, wout, wup, wdown = ([None] * DEPTH for _ in range(5))
    src_first = shards_of(0, W_IN)
    g_first = with_own(src_first, _gather_plain("gather_in0", src_first))
    win[0] = full_w_in(g_first[0])

    dkh = half // LIN_HEADS
    lr_rows = DEPTH * GATE_RANK
    lr_slab = jnp.zeros((lr_rows, 4, half // 4), F32)
    lr_slab = lax.dynamic_update_slice(lr_slab, (gla_w_lr.reshape(lr_rows, 1, half // 4) * jnp.where(ci == 0, 1.0, 0.0)),
                                       (0, chip, 0))
    wlr_full = _small_allreduce(lr_slab.reshape(-1, LANE)).reshape(DEPTH, GATE_RANK, half)
    wlr_pad = jnp.concatenate([wlr_full, jnp.zeros((DEPTH, LANE - GATE_RANK, half), F32)], axis=1)
    pend_rest = gather_start("rest", 0, REST, wlr_full[0, :1, :1] + g_first[0][0, :1, :1].astype(F32))

    inv = 10000.0 ** (-jnp.arange(0, dkh, 2, dtype=F32) / dkh)
    ang = jnp.arange(t, dtype=F32)[:, None] * inv[None, :]
    cos, sin = jnp.cos(ang), jnp.sin(ang)
    rope_c = jnp.concatenate([cos, cos], axis=1)
    rope_s = jnp.concatenate([-sin, sin], axis=1)
    log_gamma = jnp.log1p(-jnp.exp2(-5.0 - jnp.arange(LIN_HEADS, dtype=F32)))
    lg_tab = jnp.broadcast_to(log_gamma[:, None, None], (LIN_HEADS, 1, dkh))

    def vec(a):
        return a.reshape(1, -1)

    x0, x0b, xh_in, rs_in = _ln_in(x[0], vec(ln_in_g) + token(pend_rest), vec(ln_in_b))
    saved = []
    xl, xlb = x0, x0b
    for l in range(DEPTH):
        p = _mm("proj_in", xlb, win[l], 512, 1792)
        g_rest = gather_finish("rest", l, pend_rest, p)
        wbr[l], wout[l], wup[l], wdown[l] = full_rest(g_rest)
        tok = 0.0
        if l + 1 < DEPTH:
            pend_in = gather_start("in", l + 1, W_IN, g_rest[0])
            tok = token(pend_in)
        bias = _bias_expand(rel_bias[l] + tok)
        bo = _attn_fwd(p, bias, d, off)
        ret_aux = (rope_c, rope_s, lg_tab + tok)
        gla_aux = (p, wlr_pad[l], vec(gla_b_lr[l]) + tok, vec(gla_norm_g[l]))
        o_ret, bo, st_ret = _lin_fwd(False, p, ret_aux, d, off, bo, 1)
        o_gla, bo, st_gla = _lin_fwd(True, p, gla_aux, d, off, bo, 2)
        tok = 0.0
        if l + 1 < DEPTH:
            g_in = gather_finish("in", l + 1, pend_in, bo)
            win[l + 1] = full_w_in(g_in[0])
            pend_rest = gather_start("rest", l + 1, REST, g_in[0])
            tok = token(pend_rest)
        proj, merged = _merge_fwd(bo, wbr[l], p, off["gates"])
        x1, x1b, xh1, rs1 = _mm_res_ln("out_proj_ln", merged, wout[l], xl, vec(ln1_g[l]) + tok, vec(ln1_b[l]),
                                       256, False)
        u = _mm("mlp_up", x1b, wup[l], 1024, 1024)
        x2, x2b, xh2, rs2, act = _mm_res_ln("mlp_down_ln", u, wdown[l], x1, vec(ln2_g[l]), vec(ln2_b[l]), 256, True)
        saved.append(dict(xlb=xlb, p=p, bias=bias, ret_aux=ret_aux, gla_aux=gla_aux, o_ret=o_ret, o_gla=o_gla,
                          st_ret=st_ret, st_gla=st_gla, bo=bo, proj=proj, merged=merged, x1b=x1b, xh1=xh1,
                          rs1=rs1, u=u, xh2=xh2, rs2=rs2, act=act))
        xl, xlb = x2, x2b

    small = {}
    last = saved[-1]
    loss_p, dz2, dz2b, dg, db = _loss_ln_bwd(xl, loss_target[0], last["xh2"], last["rs2"], vec(ln2_g[DEPTH - 1]))
    small["loss"] = loss_p[:, :1]
    grad_x = None

    def sibling_start(tag, l, idx, shards):
        lands = [_sds((g.shape[0], g.shape[1] // 2, g.shape[2]), F32) for g in shards]
        return tag, l, idx, _split_start(f"grad_{tag}{l}_sibling_start", shards, lands, _sibling_copies, per_src=1)

    def scatter_start(sibling, after):
        tag, l, idx, started = sibling
        shards, theirs = _split_wait(f"grad_{tag}{l}_sibling_wait", started, _sibling_copies, after)
        hs = [_add_half("grad_sibling_add", g, th, csel) for g, th in zip(shards, theirs)]
        lands = [_sds((3,) + h.shape[1:], F32) for h in hs]
        return tag, l, idx, _split_start(f"grad_{tag}{l}_scatter_start", hs, lands, _scatter_copies)

    adam_out = [None] * len(big_w)
    w_in_halves = [None] * DEPTH

    def scatter_finish(pending, after):
        tag, l, idx, started = pending
        hs, rcv = _split_wait(f"grad_{tag}{l}_scatter_wait", started, _scatter_copies, after)
        sms = [_sum_shards("grad_chip_sum", h, r, psel) for h, r in zip(hs, rcv)]
        last = None
        for i, own, sib in zip(idx, sms, _sibling_share(f"grad_{tag}{l}_share", sms)):
            if i == W_IN[0]:
                w_in_halves[l] = (own, sib)
                last = sib
            else:
                adam_out[i] = _adamw_layer("adamw_large", big_w[i], own, sib, csel, big_m[i], big_v[i], l,
                                           adam_out[i])
                last = adam_out[i][0]
        return last

    in_flight = []

    def scatter(sibling, after):
        pending = scatter_start(sibling, after)
        in_flight.append(pending)
        if len(in_flight) > 3:
            scatter_finish(in_flight.pop(0), pending[3][4])
        return pending[3][4][0, 0]

    def token_of(sibling):
        return sibling[3][4][0, 0]

    carry_tok = 0.0
    for l in reversed(range(DEPTH)):
        s = saved[l]
        small[("ln2_g", l)], small[("ln2_b", l)] = dg, db
        du = _mm_nt_relu2_bwd(dz2b, wdown[l], s["u"])
        g_wdown = _mm_tn("grad_w_down", s["act"], dz2b, 512, 512)
        g_wup = _mm_tn("grad_w_up", s["x1b"], du, 512, 512, shard="cols")
        dz1, dz1b, dg1, db1 = _mm_nt_res_lnbwd("mlp_up_bwd_ln", du, wup[l], dz2, s["xh1"], s["rs1"],
                                               vec(ln1_g[l]) + carry_tok, 256, dff)
        small[("ln1_g", l)], small[("ln1_b", l)] = dg1, db1
        dproj, dgl = _merge_bwd(dz1b, wout[l], s["proj"], s["p"], off["gates"])
        g_wout = _mm_tn("grad_w_out", s["merged"], dz1b, 512, 512)
        dbo = _mm("branch_proj_bwd", dproj, wbr[l], 1024, 1024, nt=True)
        g_wbr = _mm_tn("grad_w_branch", s["bo"], dproj, d // 4, 1024, shard="rows")
        sib = sibling_start("rest", l, REST, [g_wbr, g_wout.reshape(4, d // 4, d), g_wup, g_wdown.reshape(4, d, d)])
        rc, rs_, lg = s["ret_aux"]
        gp, gw, gb, gn_ = s["gla_aux"]
        dq_a, dk_acc, dv_acc, dbias = _attn_bwd(s["p"], s["bias"] + token_of(sib), dbo, d, off)
        tok = scatter(sib, dq_a)
        small[("rel_bias", l)] = _bias_reduce(dbias)
        dk_a = dk_acc.astype(BF16)
        dv_a = dv_acc.astype(BF16)
        dq_r, dk_r, dv_r, dg_r = _lin_bwd(False, s["p"], (rc, rs_, lg + tok), s["o_ret"], s["st_ret"], dbo, 1, d, off)
        dq_g, dk_g, dv_g, dg_g, dpre, dblr, dgn = _lin_bwd(True, s["p"], (gp, gw, gb + tok, gn_), s["o_gla"],
                                                           s["st_gla"], dbo, 2, d, off)
        small[("gla_b_lr", l)] = dblr.reshape(1, half)
        small[("gla_norm_g", l)] = jnp.sum(dgn, axis=0)
        dpre_b = dpre.astype(BF16)
        glr_b = s["p"][:, off["glr"]:off["glr"] + LANE].astype(BF16)
        dglr = _mm("gate_lr_bwd", dpre_b, wlr_pad[l], 512, LANE, nt=True, out_dtype=BF16)
        small[("gla_w_lr", l)] = _mm_tn("grad_gla_w_lr", glr_b, dpre_b, LANE, half)[:GATE_RANK]
        dp = jnp.concatenate([dq_a, dk_a, dv_a, dq_r, dk_r, dv_r, dg_r, dq_g, dk_g, dv_g, dg_g,
                              dgl[0], dgl[1], dgl[2], dglr, jnp.zeros((t, LANE), BF16)], axis=1)
        if l > 0:
            prev = saved[l - 1]
            xh_p, rs_p, g_p = prev["xh2"], prev["rs2"], vec(ln2_g[l - 1])
        else:
            xh_p, rs_p, g_p = xh_in, rs_in, vec(ln_in_g)
        g_win = _mm_tn("grad_w_in", s["xlb"], dp, 1024, 896)
        sib = sibling_start("in", l, W_IN, [jnp.transpose(_unpad_cols(g_win, d).reshape(d, 4, -1), (1, 0, 2))])
        if l > 0:
            tok = token_of(sib)
        else:
            tok = scatter(sib, sib[3][4])
        dzp, dzpb, dg, db = _mm_nt_res_lnbwd("proj_in_bwd_ln", dp, win[l], dz1, xh_p, rs_p, g_p + tok, 1024, 1792)
        if l > 0:
            carry_tok = scatter(sib, dzp)
        dz2, dz2b = dzp, dzpb
        grad_x = dzp
    after = grad_x
    while in_flight:
        after = scatter_finish(in_flight.pop(0), after)
    wt, mt, vt = (jnp.transpose(a, (2, 0, 1)) for a in (big_w[0], big_m[0], big_v[0]))
    ntail = wt.shape[0] % LANE
    tails = [jnp.where(ci == 0, jnp.concatenate([own[:, -ntail:], sib[:, -ntail:]]),
                       jnp.concatenate([sib[:, -ntail:], own[:, -ntail:]])).T for own, sib in w_in_halves]
    adam_t = _adamw_tail("adamw_w_in_tail", wt, mt, vt, jnp.stack(tails, axis=1),
                         _adamw_colmajor("adamw_w_in", wt, mt, vt, w_in_halves, csel))
    adam_out[0] = [jnp.transpose(r, (1, 2, 0)) for r in adam_t]
    small["ln_in_g"], small["ln_in_b"] = dg, db
    rb_pad = 3 * LANE
    pieces = [small["loss"].reshape(-1), jnp.zeros((LANE - 1,), F32), small["ln_in_g"].reshape(-1),
              small["ln_in_b"].reshape(-1)]
    for l in range(DEPTH):
        rb = jnp.pad(small[("rel_bias", l)], ((0, 0), (0, rb_pad - (2 * REL_CLIP + 1))))
        pieces += [rb.reshape(-1), small[("gla_w_lr", l)].reshape(-1), small[("gla_b_lr", l)].reshape(-1),
                   small[("gla_norm_g", l)].reshape(-1), small[("ln1_g", l)].reshape(-1),
                   small[("ln1_b", l)].reshape(-1), small[("ln2_g", l)].reshape(-1), small[("ln2_b", l)].reshape(-1)]
    sizes = [pc.shape[0] for pc in pieces]
    packed = jnp.concatenate(pieces)
    padn = (-packed.shape[0]) % (8 * LANE)
    packed = jnp.concatenate([packed, jnp.zeros((padn,), F32)]).reshape(-1, LANE)
    red = _small_allreduce(packed, after).reshape(-1)

    parts, pos = [], 0
    for sz in sizes:
        parts.append(red[pos:pos + sz])
        pos += sz
    loss = parts[0][0]
    g_ln_in_g, g_ln_in_b = parts[2], parts[3]
    per = 8
    g_rel = jnp.stack([parts[4 + per * l].reshape(ATTN_HEADS, rb_pad)[:, :2 * REL_CLIP + 1] for l in range(DEPTH)])
    g_wlr_full = jnp.stack([parts[5 + per * l].reshape(GATE_RANK, half) for l in range(DEPTH)])
    g_wlr = lax.dynamic_slice_in_dim(g_wlr_full, chip * (half // 4), half // 4, axis=2)
    g_blr = jnp.stack([parts[6 + per * l] for l in range(DEPTH)])
    g_gn = jnp.stack([parts[7 + per * l] for l in range(DEPTH)])
    g_ln1g = jnp.stack([parts[8 + per * l] for l in range(DEPTH)])
    g_ln1b = jnp.stack([parts[9 + per * l] for l in range(DEPTH)])
    g_ln2g = jnp.stack([parts[10 + per * l] for l in range(DEPTH)])
    g_ln2b = jnp.stack([parts[11 + per * l] for l in range(DEPTH)])

    grads = [g_ln_in_g, g_ln_in_b, None, g_rel, g_wlr, g_blr, g_gn, None, None, g_ln1g, g_ln1b, None, None,
             g_ln2g, g_ln2b]
    ws = [ln_in_g, ln_in_b, w_in, rel_bias, gla_w_lr, gla_b_lr, gla_norm_g, w_branch, w_out, ln1_g, ln1_b,
          w_up, w_down, ln2_g, ln2_b]
    ms = [m_ln_in_g, m_ln_in_b, m_w_in, m_rel_bias, m_gla_w_lr, m_gla_b_lr, m_gla_norm_g, m_w_branch, m_w_out,
          m_ln1_g, m_ln1_b, m_w_up, m_w_down, m_ln2_g, m_ln2_b]
    vs = [v_ln_in_g, v_ln_in_b, v_w_in, v_rel_bias, v_gla_w_lr, v_gla_b_lr, v_gla_norm_g, v_w_branch, v_w_out,
          v_ln1_g, v_ln1_b, v_w_up, v_w_down, v_ln2_g, v_ln2_b]

    deltas, new_ms, new_vs = [None] * 15, [None] * 15, [None] * 15
    big_idx = [2, 7, 8, 11, 12]
    for i, res in zip(big_idx, adam_out):
        shp = ws[i].shape
        grads[i], deltas[i], new_ms[i], new_vs[i] = (r.reshape(shp) for r in res)
    small_idx = [i for i in range(15) if i not in big_idx]

    def pack(arrs):
        flat_ = jnp.concatenate([arrs[i].reshape(-1) for i in small_idx])
        pad_ = (-flat_.shape[0]) % (8 * LANE)
        return jnp.concatenate([flat_, jnp.ones((pad_,), F32)]).reshape(-1, LANE)

    dl, nm, nv = _adamw("adamw_small", pack(ws), pack(grads), pack(ms), pack(vs))
    pos = 0
    for i in small_idx:
        sz = int(np.prod(ws[i].shape))
        deltas[i] = dl.reshape(-1)[pos:pos + sz].reshape(ws[i].shape)
        new_ms[i] = nm.reshape(-1)[pos:pos + sz].reshape(ws[i].shape)
        new_vs[i] = nv.reshape(-1)[pos:pos + sz].reshape(ws[i].shape)
        pos += sz

    return (loss, grad_x[None], *grads, *deltas, *new_ms, *new_vs)
```

```python
import functools

import numpy as np
import jax
import jax.numpy as jnp
from jax import lax
from jax.experimental import pallas as pl
from jax.experimental.pallas import tpu as pltpu

F32 = jnp.float32
BF16 = jnp.bfloat16
MXU_DTYPE = BF16
HI = lax.Precision.HIGHEST

DEPTH = 2
CHUNK = 64
N_BRANCH = 3
ATTN_HEADS = 8
ATTN_LEFT = 8
REL_CLIP = 2 * CHUNK
LIN_HEADS = 4
GATE_RANK = 16
GATE_NORM = 16.0
LN_EPS = 1e-5
NEG_INF = -1e30
ALPHA = (2 * DEPTH) ** 0.25
ADAM_LR, ADAM_B1, ADAM_B2, ADAM_EPS, ADAM_WD, ADAM_STEP = 0.001, 0.9, 0.999, 1e-08, 0.01, 10

LANE = 128
VMEM_LIMIT = 56 << 20
QB = 256
KW = 3 * QB
LB = 256
MESH_AXES = ("x", "y", "c")
DEV = pl.DeviceIdType.MESH


def _cp(sem):
    return pltpu.CompilerParams(dimension_semantics=sem, vmem_limit_bytes=VMEM_LIMIT)


def _mx(v):
    return v.astype(MXU_DTYPE)


def _dot(a, b):
    return jnp.dot(_mx(a), _mx(b), preferred_element_type=F32)


def _dot_nt(a, b):
    return lax.dot_general(_mx(a), _mx(b), (((1,), (1,)), ((), ())), preferred_element_type=F32)


def _dot_tn(a, b):
    return lax.dot_general(_mx(a), _mx(b), (((0,), (0,)), ((), ())), preferred_element_type=F32)


def _dot_hi(a, b):
    return jnp.dot(a, b, precision=HI, preferred_element_type=F32)


def _sigmoid(v):
    return 1.0 / (1.0 + jnp.exp(-v))


def _sds(shape, dtype):
    return jax.ShapeDtypeStruct(shape, dtype)


def _mm(name, a, b, tm, tn, nt=False, out_dtype=F32):
    batched = a.ndim == 3
    m, k = a.shape[-2:]
    n = b.shape[-2] if nt else b.shape[-1]
    tm, tn = min(tm, m), min(tn, n)

    def body(a_ref, b_ref, o_ref):
        f = _dot_nt if nt else _dot
        o_ref[...] = f(a_ref[...], b_ref[...]).astype(o_ref.dtype)

    rows_inner = (n // tn) * m < (m // tm) * n

    def ij(u, v):
        return (v, u) if rows_inner else (u, v)

    if batched:
        nb = a.shape[0]
        grid = (nb,) + ij(m // tm, n // tn)
        a_spec = pl.BlockSpec((None, tm, k), lambda g, u, v: (g, ij(u, v)[0], 0))
        b_spec = (pl.BlockSpec((None, tn, k), lambda g, u, v: (g, ij(u, v)[1], 0)) if nt
                  else pl.BlockSpec((None, k, tn), lambda g, u, v: (g, 0, ij(u, v)[1])))
        o_spec = pl.BlockSpec((None, tm, tn), lambda g, u, v: (g,) + ij(u, v))
        out_shape = _sds((nb, m, n), out_dtype)
        sem = ("parallel", "parallel", "parallel")
    else:
        grid = ij(m // tm, n // tn)
        a_spec = pl.BlockSpec((tm, k), lambda u, v: (ij(u, v)[0], 0))
        b_spec = (pl.BlockSpec((tn, k), lambda u, v: (ij(u, v)[1], 0)) if nt
                  else pl.BlockSpec((k, tn), lambda u, v: (0, ij(u, v)[1])))
        o_spec = pl.BlockSpec((tm, tn), lambda u, v: ij(u, v))
        out_shape = _sds((m, n), out_dtype)
        sem = ("parallel", "parallel")
    return pl.pallas_call(body, name=name, grid=grid, in_specs=[a_spec, b_spec], out_specs=o_spec,
                          out_shape=out_shape, compiler_params=_cp(sem))(a, b)


def _mm_tn(name, a, b, tm, tn, shard=None):
    batched = a.ndim == 3
    k, m = a.shape[-2:]
    n = b.shape[-1]
    tm, tn = min(tm, m), min(tn, n)

    def body(a_ref, b_ref, o_ref):
        o_ref[...] = lax.dot_general(_mx(a_ref[...]), _mx(b_ref[...]), (((0,), (0,)), ((), ())),
                                     preferred_element_type=F32)

    if batched:
        nb = a.shape[0]
        grid = (nb, m // tm, n // tn)
        a_spec = pl.BlockSpec((None, k, tm), lambda g, i, j: (g, 0, i))
        b_spec = pl.BlockSpec((None, k, tn), lambda g, i, j: (g, 0, j))
        if shard == "rows":
            assert 4 * tm == m
            o_spec = pl.BlockSpec((None, tm, tn), lambda g, i, j: (i, g, j))
            out_shape = _sds((4, nb * tm, n), F32)
        else:
            o_spec = pl.BlockSpec((None, tm, tn), lambda g, i, j: (g, i, j))
            out_shape = _sds((nb, m, n), F32)
    else:
        grid = (m // tm, n // tn)
        a_spec = pl.BlockSpec((k, tm), lambda i, j: (0, i))
        b_spec = pl.BlockSpec((k, tn), lambda i, j: (0, j))
        if shard == "cols":
            per = n // 4 // tn
            o_spec = pl.BlockSpec((None, tm, tn), lambda i, j: (j // per, i, j % per))
            out_shape = _sds((4, m, n // 4), F32)
        else:
            o_spec = pl.BlockSpec((tm, tn), lambda i, j: (i, j))
            out_shape = _sds((m, n), F32)
    return pl.pallas_call(body, name=name, grid=grid, in_specs=[a_spec, b_spec], out_specs=o_spec,
                          out_shape=out_shape, compiler_params=_cp(("parallel",) * len(grid)))(a, b)


def _ln_rows(y, g, b):
    mu = jnp.mean(y, axis=-1, keepdims=True)
    yc = y - mu
    var = jnp.mean(yc * yc, axis=-1, keepdims=True)
    rs = lax.rsqrt(var + LN_EPS)
    xh = yc * rs
    return xh * g + b, xh, rs


def _ln_in(x, g, b, tm=256):
    t, d = x.shape

    def body(x_ref, g_ref, b_ref, o_ref, ob_ref, xh_ref, rs_ref):
        o, xh, rs = _ln_rows(x_ref[...], g_ref[...], b_ref[...])
        o_ref[...] = o
        ob_ref[...] = o.astype(BF16)
        xh_ref[...] = xh
        rs_ref[...] = rs

    row = pl.BlockSpec((tm, d), lambda i: (i, 0))
    vec = pl.BlockSpec((1, d), lambda i: (0, 0))
    return pl.pallas_call(
        body, name="ln_in", grid=(t // tm,), in_specs=[row, vec, vec],
        out_specs=[row, row, row, pl.BlockSpec((tm, 1), lambda i: (i, 0))],
        out_shape=[_sds((t, d), F32), _sds((t, d), BF16), _sds((t, d), F32), _sds((t, 1), F32)],
        compiler_params=_cp(("parallel",)))(x, g, b)


def _mm_res_ln(name, a, w, res, g, b, tm, relu2):
    t, k = a.shape
    d = w.shape[1]

    def body(a_ref, w_ref, r_ref, g_ref, b_ref, o_ref, ob_ref, xh_ref, rs_ref, *act_ref):
        av = a_ref[...]
        if relu2:
            av = jnp.square(jnp.maximum(av, 0.0))
            act_ref[0][...] = av.astype(BF16)
        y = ALPHA * r_ref[...] + _dot(av, w_ref[...])
        o, xh, rs = _ln_rows(y, g_ref[...], b_ref[...])
        o_ref[...] = o
        ob_ref[...] = o.astype(BF16)
        xh_ref[...] = xh
        rs_ref[...] = rs

    row = pl.BlockSpec((tm, d), lambda i: (i, 0))
    vec = pl.BlockSpec((1, d), lambda i: (0, 0))
    arow = pl.BlockSpec((tm, k), lambda i: (i, 0))
    out_specs = [row, row, row, pl.BlockSpec((tm, 1), lambda i: (i, 0))]
    out_shape = [_sds((t, d), F32), _sds((t, d), BF16), _sds((t, d), F32), _sds((t, 1), F32)]
    if relu2:
        out_specs.append(arow)
        out_shape.append(_sds((t, k), BF16))
    return pl.pallas_call(
        body, name=name, grid=(t // tm,),
        in_specs=[arow, pl.BlockSpec((k, d), lambda i: (0, 0)), row, vec, vec],
        out_specs=out_specs, out_shape=out_shape, compiler_params=_cp(("parallel",)))(a, w, res, g, b)


def _merge_fwd(bo, wb, p, gate_off, tm=512, tn=512):
    _, t, d = bo.shape
    gb = gate_off // tn

    def body(bo_ref, wb_ref, g0, g1, g2, proj_ref, m_ref):
        acc = None
        for n, g_ref in enumerate((g0, g1, g2)):
            pr = _dot(bo_ref[n], wb_ref[n])
            proj_ref[n] = pr
            term = _sigmoid(g_ref[...]) * pr
            acc = term if acc is None else acc + term
        m_ref[...] = acc.astype(BF16)

    gspecs = [pl.BlockSpec((tm, tn), functools.partial(lambda i, j, n: (i, gb + n * (d // tn) + j), n=n))
              for n in range(3)]
    return pl.pallas_call(
        body, name="merge_fwd", grid=(t // tm, d // tn),
        in_specs=[pl.BlockSpec((3, tm, d), lambda i, j: (0, i, 0)),
                  pl.BlockSpec((3, d, tn), lambda i, j: (0, 0, j))] + gspecs,
        out_specs=[pl.BlockSpec((3, tm, tn), lambda i, j: (0, i, j)), pl.BlockSpec((tm, tn), lambda i, j: (i, j))],
        out_shape=[_sds((3, t, d), F32), _sds((t, d), BF16)],
        compiler_params=_cp(("parallel", "parallel")))(bo, wb, p, p, p)


def _merge_bwd(dz, wout, proj, p, gate_off, npad, tm=256):
    t, d = dz.shape

    def body(dz_ref, w_ref, proj_ref, g0, g1, g2, dproj_ref, dp_ref):
        dm = _dot_nt(dz_ref[...], w_ref[...])
        for n, g_ref in enumerate((g0, g1, g2)):
            s = _sigmoid(g_ref[...])
            dproj_ref[n] = (dm * s).astype(BF16)
            dp_ref[:, n * d:(n + 1) * d] = (dm * proj_ref[n] * (s * (1.0 - s))).astype(BF16)

    gspecs = [pl.BlockSpec((tm, d), functools.partial(lambda i, n: (i, gate_off // d + n), n=n)) for n in range(3)]
    return pl.pallas_call(
        body, name="merge_bwd", grid=(t // tm,),
        in_specs=[pl.BlockSpec((tm, d), lambda i: (i, 0)), pl.BlockSpec((d, d), lambda i: (0, 0)),
                  pl.BlockSpec((3, tm, d), lambda i: (0, i, 0))] + gspecs,
        out_specs=[pl.BlockSpec((3, tm, d), lambda i: (0, i, 0)),
                   pl.BlockSpec((tm, 3 * d), lambda i: (i, gate_off // (3 * d)))],
        out_shape=[_sds((3, t, d), BF16), _sds((t, npad), BF16)],
        compiler_params=_cp(("parallel",)))(dz, wout, proj, p, p, p)


def _gate_lr_bwd(dpre, wlr, dp, col_off, tm=512):
    t, k = dpre.shape
    w = wlr.shape[0]

    def body(a_ref, w_ref, dp_in, o_ref):
        o_ref[...] = _dot_nt(a_ref[...], w_ref[...]).astype(BF16)

    return pl.pallas_call(
        body, name="gate_lr_bwd", grid=(t // tm,),
        in_specs=[pl.BlockSpec((tm, k), lambda i: (i, 0)), pl.BlockSpec((w, k), lambda i: (0, 0)), ANY],
        out_specs=pl.BlockSpec((tm, w), lambda i: (i, col_off // w)), out_shape=_sds(dp.shape, BF16),
        input_output_aliases={2: 0}, compiler_params=_cp(("parallel",)))(dpre, wlr, dp)


def _mm_nt_relu2_bwd(dz, wdown, u, tm=512, tn=1024):
    t, d = dz.shape
    f = wdown.shape[0]

    def body(dz_ref, w_ref, u_ref, du_ref):
        da = _dot_nt(dz_ref[...], w_ref[...])
        du_ref[...] = (da * (2.0 * jnp.maximum(u_ref[...], 0.0))).astype(BF16)

    return pl.pallas_call(
        body, name="mlp_down_bwd", grid=(t // tm, f // tn),
        in_specs=[pl.BlockSpec((tm, d), lambda i, j: (i, 0)), pl.BlockSpec((tn, d), lambda i, j: (j, 0)),
                  pl.BlockSpec((tm, tn), lambda i, j: (i, j))],
        out_specs=pl.BlockSpec((tm, tn), lambda i, j: (i, j)), out_shape=_sds((t, f), BF16),
        compiler_params=_cp(("parallel", "parallel")))(dz, wdown, u)


def _ln_bwd_rows(dx, xh, rs, g):
    dxh = dx * g
    m1 = jnp.mean(dxh, axis=-1, keepdims=True)
    m2 = jnp.mean(dxh * xh, axis=-1, keepdims=True)
    return rs * (dxh - m1 - xh * m2)


def _mm_nt_res_lnbwd(name, a, w, dres, xh, rs, g, tm, tk):
    t, k = a.shape
    d = w.shape[0]
    nk = k // tk

    def body(a_ref, w_ref, dr_ref, xh_ref, rs_ref, g_ref, dz_ref, dzb_ref, dg_ref, db_ref, acc_ref):
        i, kk = pl.program_id(0), pl.program_id(1)

        @pl.when(kk == 0)
        def _():
            acc_ref[...] = ALPHA * dr_ref[...]

        acc_ref[...] += _dot_nt(a_ref[...], w_ref[...])

        @pl.when(jnp.logical_and(i == 0, kk == 0))
        def _():
            dg_ref[...] = jnp.zeros_like(dg_ref)
            db_ref[...] = jnp.zeros_like(db_ref)

        @pl.when(kk == nk - 1)
        def _():
            dx = acc_ref[...]
            xhv = xh_ref[...]
            dz = _ln_bwd_rows(dx, xhv, rs_ref[...], g_ref[...])
            dz_ref[...] = dz
            dzb_ref[...] = dz.astype(BF16)
            dg_ref[...] += jnp.sum(dx * xhv, axis=0, keepdims=True)
            db_ref[...] += jnp.sum(dx, axis=0, keepdims=True)

    row = pl.BlockSpec((tm, d), lambda i, kk: (i, 0))
    vec = pl.BlockSpec((1, d), lambda i, kk: (0, 0))
    return pl.pallas_call(
        body, name=name, grid=(t // tm, nk),
        in_specs=[pl.BlockSpec((tm, tk), lambda i, kk: (i, kk)), pl.BlockSpec((d, tk), lambda i, kk: (0, kk)),
                  row, row, pl.BlockSpec((tm, 1), lambda i, kk: (i, 0)), vec],
        out_specs=[row, row, vec, vec],
        out_shape=[_sds((t, d), F32), _sds((t, d), BF16), _sds((1, d), F32), _sds((1, d), F32)],
        scratch_shapes=[pltpu.VMEM((tm, d), F32)],
        compiler_params=_cp(("arbitrary", "arbitrary")))(a, w, dres, xh, rs, g)


def _loss_ln_bwd(x2, target, xh, rs, g, tm=256):
    t, d = x2.shape

    def body(x_ref, t_ref, xh_ref, rs_ref, g_ref, loss_ref, dz_ref, dzb_ref, dg_ref, db_ref):
        @pl.when(pl.program_id(0) == 0)
        def _():
            loss_ref[...] = jnp.zeros_like(loss_ref)
            dg_ref[...] = jnp.zeros_like(dg_ref)
            db_ref[...] = jnp.zeros_like(db_ref)

        err = x_ref[...] - t_ref[...]
        per_row = jnp.mean(err * err, axis=-1, keepdims=True)
        loss_ref[...] += 0.5 * jnp.sum(per_row, axis=0, keepdims=True)
        dx = err * (1.0 / d)
        xhv = xh_ref[...]
        dz = _ln_bwd_rows(dx, xhv, rs_ref[...], g_ref[...])
        dz_ref[...] = dz
        dzb_ref[...] = dz.astype(BF16)
        dg_ref[...] += jnp.sum(dx * xhv, axis=0, keepdims=True)
        db_ref[...] += jnp.sum(dx, axis=0, keepdims=True)

    row = pl.BlockSpec((tm, d), lambda i: (i, 0))
    vec = pl.BlockSpec((1, d), lambda i: (0, 0))
    return pl.pallas_call(
        body, name="loss_ln_bwd", grid=(t // tm,),
        in_specs=[row, row, row, pl.BlockSpec((tm, 1), lambda i: (i, 0)), vec],
        out_specs=[pl.BlockSpec((1, LANE), lambda i: (0, 0)), row, row, vec, vec],
        out_shape=[_sds((1, LANE), F32), _sds((t, d), F32), _sds((t, d), BF16), _sds((1, d), F32),
                   _sds((1, d), F32)],
        compiler_params=_cp(("arbitrary",)))(x2, target, xh, rs, g)


HPA_FWD = 8
HPA = 2


STRIP = 16


def _attn_scores(q_ref, k_refs, bias_ref, i, dh, hh):
    cols = pl.ds(hh * dh, dh)
    q = q_ref[:, cols] * (dh ** -0.5)
    k = jnp.concatenate([r[:, cols] for r in k_refs], axis=0)
    s = _dot_nt(q, k)
    before_start = lax.broadcasted_iota(jnp.int32, (STRIP, KW), 1) < (2 - i) * QB
    strips = []
    for r in range(0, QB, STRIP):
        ss = jnp.where(before_start, NEG_INF, s[r:r + STRIP] + bias_ref[hh, r:r + STRIP])
        e = jnp.exp(ss - jnp.max(ss, axis=-1, keepdims=True))
        strips.append(e / jnp.sum(e, axis=-1, keepdims=True))
    return q, k, strips


def _attn_specs(dh, off, hp):
    w = hp * dh
    qcol, kcol, vcol = off["aq"] // w, off["ak"] // w, off["av"] // w
    q_spec = pl.BlockSpec((QB, w), lambda g, i: (i, qcol + g))
    k_specs = [pl.BlockSpec((QB, w), functools.partial(lambda g, i, j: (jnp.maximum(i - 2 + j, 0), kcol + g), j=j))
               for j in range(3)]
    v_specs = [pl.BlockSpec((QB, w), functools.partial(lambda g, i, j: (jnp.maximum(i - 2 + j, 0), vcol + g), j=j))
               for j in range(3)]
    bias_spec = pl.BlockSpec((hp, QB, KW), lambda g, i: (g, 0, 0))
    return q_spec, k_specs, v_specs, bias_spec


def _attn_fwd(p, bias, d, off):
    t = p.shape[0]
    dh = d // ATTN_HEADS

    def body(q_ref, k0, k1, k2, v0, v1, v2, bias_ref, o_ref):
        for hh in range(HPA_FWD):
            cols = pl.ds(hh * dh, dh)
            _, _, strips = _attn_scores(q_ref, (k0, k1, k2), bias_ref, pl.program_id(1), dh, hh)
            pr = jnp.concatenate([_mx(ps) for ps in strips], axis=0)
            v = jnp.concatenate([v0[:, cols], v1[:, cols], v2[:, cols]], axis=0)
            o_ref[:, cols] = _dot(pr, v).astype(o_ref.dtype)

    q_spec, k_specs, v_specs, bias_spec = _attn_specs(dh, off, HPA_FWD)
    return pl.pallas_call(
        body, name="attn_fwd", grid=(ATTN_HEADS // HPA_FWD, t // QB),
        in_specs=[q_spec] + k_specs + v_specs + [bias_spec],
        out_specs=pl.BlockSpec((None, QB, HPA_FWD * dh), lambda g, i: (0, i, g)),
        out_shape=_sds((N_BRANCH, t, d), BF16),
        compiler_params=_cp(("parallel", "parallel")))(p, p, p, p, p, p, p, bias)


def _attn_bwd(p, bias, do, dp, d, off):
    t = p.shape[0]
    dh = d // ATTN_HEADS
    tp = t + 2 * QB

    def body(q_ref, k0, k1, k2, v0, v1, v2, bias_ref, do_ref, dp_in, dq_ref, dk_ref, dv_ref, dbias_ref):
        i = pl.program_id(1)

        @pl.when(i == 0)
        def _():
            dk_ref[...] = jnp.zeros_like(dk_ref)
            dv_ref[...] = jnp.zeros_like(dv_ref)
            dbias_ref[...] = jnp.zeros_like(dbias_ref)

        rows = pl.ds(pl.multiple_of(i * QB, QB), KW)
        for hh in range(HPA):
            cols = pl.ds(hh * dh, dh)
            q, k, strips = _attn_scores(q_ref, (k0, k1, k2), bias_ref, i, dh, hh)
            v = jnp.concatenate([v0[:, cols], v1[:, cols], v2[:, cols]], axis=0)
            dov = do_ref[:, cols]
            dp = _dot_nt(dov, v)
            ds_strips = []
            for n, ps in enumerate(strips):
                r = n * STRIP
                dps = dp[r:r + STRIP]
                dss = ps * (dps - jnp.sum(ps * dps, axis=-1, keepdims=True))
                dbias_ref[hh, r:r + STRIP] += dss
                ds_strips.append(_mx(dss))
            ds = jnp.concatenate(ds_strips, axis=0)
            pr = jnp.concatenate([_mx(ps) for ps in strips], axis=0)
            dq_ref[:, cols] = (_dot(ds, k) * (dh ** -0.5)).astype(dq_ref.dtype)
            dk_ref[rows, cols] += _dot_tn(ds, q)
            dv_ref[rows, cols] += _dot_tn(pr, dov)

    q_spec, k_specs, v_specs, bias_spec = _attn_specs(dh, off, HPA)
    qcol = off["aq"] // (HPA * dh)
    acc_spec = pl.BlockSpec((tp, HPA * dh), lambda g, i: (0, g))
    return pl.pallas_call(
        body, name="attn_bwd", grid=(ATTN_HEADS // HPA, t // QB),
        in_specs=[q_spec] + k_specs + v_specs + [bias_spec,
                                                 pl.BlockSpec((None, QB, HPA * dh), lambda g, i: (0, i, g)), ANY],
        out_specs=[pl.BlockSpec((QB, HPA * dh), lambda g, i: (i, qcol + g)), acc_spec, acc_spec, bias_spec],
        out_shape=[_sds(dp.shape, BF16), _sds((tp, d), F32), _sds((tp, d), F32),
                   _sds((ATTN_HEADS, QB, KW), F32)],
        input_output_aliases={9: 0},
        compiler_params=_cp(("parallel", "arbitrary")))(p, p, p, p, p, p, p, bias, do, dp)


def _onehot_mm(name, a, b):
    def body(a_ref, b_ref, o_ref):
        o_ref[...] = _dot_hi(a_ref[...], b_ref[...])

    return pl.pallas_call(body, name=name, out_shape=_sds((a.shape[0], b.shape[1]), F32),
                          compiler_params=pltpu.CompilerParams(vmem_limit_bytes=VMEM_LIMIT))(a, b)


def _diag_index():
    ii, jj = np.arange(CHUNK)[:, None], np.arange(CHUNK)[None, :]
    return (ii - jj + CHUNK - 1).reshape(-1)


def _bias_expand(rel_bias):
    h = rel_bias.shape[0]
    nq, nk, shift = QB // CHUNK, KW // CHUNK, (2 * QB) // CHUNK
    nbin, ndc = 3 * LANE, 4
    rb = jnp.pad(rel_bias, ((0, 0), (0, nbin - rel_bias.shape[1])))
    win = np.clip(CHUNK * np.arange(ndc)[:, None] + np.arange(LANE)[None, :] - (CHUNK - 1), -REL_CLIP, REL_CLIP)
    sel = (jnp.arange(nbin)[:, None] == jnp.asarray((win + REL_CLIP).reshape(1, -1))).astype(F32)
    windows = _onehot_mm("bias_windows", rb, sel)
    diag_t = (jnp.arange(LANE)[:, None] == jnp.asarray(_diag_index().reshape(1, -1))).astype(F32)
    blocks = _onehot_mm("bias_blocks", windows.reshape(h * ndc, LANE), diag_t).reshape(h, ndc, CHUNK, CHUNK)
    off_band = jnp.full((h, CHUNK, CHUNK), NEG_INF, F32)
    rows = []
    for ic in range(nq):
        dcs = [ic - jc + shift for jc in range(nk)]
        rows.append(jnp.concatenate([blocks[:, min(dc, ndc - 1)] if 0 <= dc <= ATTN_LEFT else off_band
                                     for dc in dcs], axis=2))
    return jnp.concatenate(rows, axis=1)


def _bias_reduce(dbias):
    h = dbias.shape[0]
    nq, nk = QB // CHUNK, KW // CHUNK
    nbin = 3 * LANE
    blocks = dbias.reshape(h, nq, CHUNK, nk, CHUNK).transpose(0, 1, 3, 2, 4).reshape(h * nq * nk, CHUNK * CHUNK)
    diag = (jnp.asarray(_diag_index().reshape(-1, 1)) == jnp.arange(LANE)[None, :]).astype(F32)
    ic = np.arange(nq)[:, None, None]
    jc = np.arange(nk)[None, :, None]
    dl = np.arange(LANE)[None, None, :] - (CHUNK - 1)
    rel = np.clip(CHUNK * (ic - jc + (2 * QB) // CHUNK) + dl, -REL_CLIP, REL_CLIP) + REL_CLIP
    bins = (jnp.asarray(rel.reshape(-1, 1)) == jnp.arange(nbin)[None, :]).astype(F32)

    diags = _onehot_mm("bias_diag_sums", blocks, diag)
    out = _onehot_mm("bias_bin_sums", diags.reshape(h, nq * nk * LANE), bins)
    return out[:, :2 * REL_CLIP + 1]


def _chunk_masks():
    r = lax.broadcasted_iota(jnp.int32, (LB, LB), 0)
    c = lax.broadcasted_iota(jnp.int32, (LB, LB), 1)
    return (r // CHUNK) == (c // CHUNK), r >= c, r <= c


def _chunks(a):
    return [a[c * CHUNK:(c + 1) * CHUNK] for c in range(LB // CHUNK)]


def _per_chunk(a, f):
    return jnp.concatenate([jnp.broadcast_to(f(c), c.shape) for c in _chunks(a)], axis=0)


def _dot_sel(sel, x):
    def top(v):
        return lax.bitcast_convert_type(lax.bitcast_convert_type(v, jnp.int32) & jnp.int32(-65536), F32)

    hi = top(x)
    mid = top(x - hi)
    lo = (x - hi) - mid
    d = functools.partial(jnp.dot, sel.astype(jnp.bfloat16), preferred_element_type=F32)
    return d(hi.astype(jnp.bfloat16)) + d(mid.astype(jnp.bfloat16)) + d(lo.astype(jnp.bfloat16))


def _lin_block(gla, q, k, v, aux):
    dk = q.shape[-1]
    same, low, up = _chunk_masks()
    ones = same.astype(F32)
    if gla:
        glr, wlr, blr = aux
        q = q * (dk ** -0.5)
        pre = _dot(glr, wlr) + blr
        log_a = (jnp.minimum(pre, 0.0) - jnp.log(1.0 + jnp.exp(-jnp.abs(pre)))) / GATE_NORM
        b = _dot_sel(jnp.where(low, ones, 0.0), log_a)
        lastb = _per_chunk(b, lambda c: c[CHUNK - 1:])
    else:
        cs, sn, lg = aux
        pre = None
        half = dk // 2
        q = q * cs + pltpu.roll(q, half, 1) * sn
        k = (k * cs + pltpu.roll(k, half, 1) * sn) * (dk ** -0.5)
        pos = (lax.broadcasted_iota(jnp.int32, (LB, dk), 0) % CHUNK).astype(F32) + 1.0
        b = pos * lg
        lastb = jnp.broadcast_to(float(CHUNK) * lg, b.shape)
    eb, enb, el, dec = jnp.exp(b), jnp.exp(-b), jnp.exp(lastb - b), jnp.exp(lastb)
    qf, kf, qb, kb, kl = q * eb, k * enb, q * enb, k * eb, k * el
    s = jnp.where(same, jnp.where(low, _dot_nt(qf, kf), _dot_nt(qb, kb)), 0.0)
    return dict(pre=pre, eb=eb, enb=enb, el=el, dec=dec, qf=qf, kf=kf, qb=qb, kb=kb, kl=kl, s=s,
                same=same, low=low, up=up, ones=ones)


def _lin_norm_gate(gla, o, gate, gn):
    sg = _sigmoid(gate)
    silu = gate * sg
    if gla:
        r = lax.rsqrt(jnp.mean(o * o, axis=-1, keepdims=True) + LN_EPS)
        hn = o * r
        return silu * (hn * gn), (sg, silu, r, hn)
    mu = jnp.mean(o, axis=-1, keepdims=True)
    oc = o - mu
    r = lax.rsqrt(jnp.mean(oc * oc, axis=-1, keepdims=True) + LN_EPS)
    hn = oc * r
    return silu * hn, (sg, silu, r, hn)


HPS = 4


def _lin_specs(gla, dk, dv, off, rev, nb):
    pre = "g" if gla else "r"
    wk, wv = HPS * dk, HPS * dv
    qc, kc, vc, gc = (off[pre + "q"] // wk, off[pre + "k"] // wk, off[pre + "v"] // wv, off[pre + "g"] // wv)

    def blk(i):
        return nb - 1 - i if rev else i

    specs = [pl.BlockSpec((LB, wk), lambda g, i: (blk(i), qc + g)),
             pl.BlockSpec((LB, wk), lambda g, i: (blk(i), kc + g)),
             pl.BlockSpec((LB, wv), lambda g, i: (blk(i), vc + g)),
             pl.BlockSpec((LB, wv), lambda g, i: (blk(i), gc + g))]
    if gla:
        specs += [pl.BlockSpec((LB, LANE), lambda g, i: (blk(i), off["glr"] // LANE)),
                  pl.BlockSpec((LANE, wk), lambda g, i: (0, g)),
                  pl.BlockSpec((1, wk), lambda g, i: (0, g)),
                  pl.BlockSpec((1, dv), lambda g, i: (0, 0))]
    else:
        specs += [pl.BlockSpec((LB, dk), lambda g, i: (blk(i), 0)),
                  pl.BlockSpec((LB, dk), lambda g, i: (blk(i), 0)),
                  pl.BlockSpec((HPS, 1, dk), lambda g, i: (g, 0, 0))]
    return specs, blk


def _lin_aux(gla, refs, rows, hh, dk):
    if gla:
        glr_ref, wlr_ref, blr_ref, gn_ref = refs
        kcols = pl.ds(hh * dk, dk)
        return (glr_ref[rows, :], wlr_ref[:, kcols], blr_ref[:, kcols]), gn_ref[...]
    cs_ref, sn_ref, lg_ref = refs
    return (cs_ref[rows, :], sn_ref[rows, :], lg_ref[hh]), None


def _lin_fwd(gla, p, aux_arrays, d, off, branches, slot):
    t = p.shape[0]
    dk, dv = d // (2 * LIN_HEADS), d // LIN_HEADS
    nb, cb = t // LB, LB // CHUNK
    naux = len(aux_arrays)

    def body(*refs):
        q_ref, k_ref, v_ref, g_ref = refs[:4]
        aux_refs = refs[4:4 + naux]
        o_ref, bo_ref, st_out_ref, st_ref = refs[5 + naux:]

        @pl.when(pl.program_id(1) == 0)
        def _():
            st_ref[...] = jnp.zeros_like(st_ref)

        rows = slice(None)
        for hh in range(HPS):
            kcols, vcols = pl.ds(hh * dk, dk), pl.ds(hh * dv, dv)
            aux, gn = _lin_aux(gla, aux_refs, rows, hh, dk)
            v = v_ref[:, vcols]
            blk = _lin_block(gla, q_ref[:, kcols], k_ref[:, kcols], v, aux)
            st = st_ref[hh]
            inter = []
            for c, (qf, kl, dec, vc) in enumerate(zip(_chunks(blk["qf"]), _chunks(blk["kl"]), _chunks(blk["dec"]),
                                                      _chunks(v))):
                st_out_ref[hh, c] = st
                inter.append(_dot_nt(qf, st))
                st = st * dec[:1] + _dot_tn(vc, kl)
            st_ref[hh] = st
            o = _dot(blk["s"], v) + jnp.concatenate(inter, axis=0)
            o_ref[:, vcols] = o
            out, _ = _lin_norm_gate(gla, o, g_ref[:, vcols], gn)
            bo_ref[:, vcols] = out.astype(BF16)

    specs, _ = _lin_specs(gla, dk, dv, off, False, nb)
    orow = pl.BlockSpec((LB, HPS * dv), lambda g, i: (i, g))
    return pl.pallas_call(
        body, name="gla_fwd" if gla else "ret_fwd", grid=(LIN_HEADS // HPS, nb), in_specs=specs + [ANY],
        out_specs=[orow, pl.BlockSpec((None, LB, HPS * dv), lambda g, i: (slot, i, g)),
                   pl.BlockSpec((HPS, cb, dv, dk), lambda g, i: (g, i, 0, 0))],
        out_shape=[_sds((t, d), F32), _sds(branches.shape, BF16), _sds((LIN_HEADS, t // CHUNK, dv, dk), F32)],
        scratch_shapes=[pltpu.VMEM((HPS, dv, dk), F32)], input_output_aliases={4 + naux: 1},
        compiler_params=_cp(("parallel", "arbitrary")))(p, p, p, p, *aux_arrays, branches)


def _lin_bwd(gla, p, aux_arrays, o, states, dbo, slot, dp, d, off):
    assert HPS == LIN_HEADS
    t = p.shape[0]
    dk, dv = d // (2 * LIN_HEADS), d // LIN_HEADS
    nb, cb = t // LB, LB // CHUNK
    naux = len(aux_arrays)

    def body(*refs):
        q_ref, k_ref, v_ref, g_ref = refs[:4]
        aux_refs = refs[4:4 + naux]
        o_ref, st_in_ref, dbo_ref = refs[4 + naux:7 + naux]
        outs = refs[8 + naux:]
        dq_ref, dk_ref = outs[0].at[:, pl.ds(0, d // 2)], outs[0].at[:, pl.ds(d // 2, d // 2)]
        dv_ref, dg_ref = outs[0].at[:, pl.ds(d, d)], outs[0].at[:, pl.ds(2 * d, d)]
        dst_ref = outs[-1]
        first = pl.program_id(1) == 0

        @pl.when(first)
        def _():
            dst_ref[...] = jnp.zeros_like(dst_ref)

        if gla:
            dpre_ref, dblr_ref, dgn_ref = outs[1:4]

            @pl.when(first)
            def _():
                dblr_ref[...] = jnp.zeros_like(dblr_ref)
                dgn_ref[...] = jnp.zeros_like(dgn_ref)

        rows = slice(None)
        for hh in range(HPS):
            kcols, vcols = pl.ds(hh * dk, dk), pl.ds(hh * dv, dv)
            aux, gn = _lin_aux(gla, aux_refs, rows, hh, dk)
            v = v_ref[:, vcols]
            bk = _lin_block(gla, q_ref[:, kcols], k_ref[:, kcols], v, aux)
            eb, enb, el, dec = bk["eb"], bk["enb"], bk["el"], bk["dec"]
            qf, kf, qb, kb, kl, s = bk["qf"], bk["kf"], bk["qb"], bk["kb"], bk["kl"], bk["s"]
            gate = g_ref[:, vcols]
            dout = dbo_ref[:, vcols]
            _, (sg, silu, r, hn) = _lin_norm_gate(gla, o_ref[:, vcols], gate, gn)
            dsilu = sg * (1.0 + gate * (1.0 - sg))
            if gla:
                y = hn * gn
                dy = dout * silu
                dg_ref[:, vcols] = (dout * y * dsilu).astype(BF16)
                dgn_ref[hh] += jnp.sum(dy * hn, axis=0, keepdims=True)
                dhn = dy * gn
                do = r * (dhn - hn * jnp.mean(dhn * hn, axis=-1, keepdims=True))
            else:
                dhn = dout * silu
                dg_ref[:, vcols] = (dout * hn * dsilu).astype(BF16)
                do = r * (dhn - jnp.mean(dhn, axis=-1, keepdims=True)
                          - hn * jnp.mean(dhn * hn, axis=-1, keepdims=True))
            ds = jnp.where(bk["same"], _dot_nt(do, v), 0.0)
            dsf = jnp.where(bk["low"], ds, 0.0)
            dsb = ds - dsf
            dvv = _dot_tn(s, do)
            dqf = _dot(dsf, kf)
            dkf = _dot_tn(dsf, qf)
            dqb = _dot(dsb, kb)
            dkb = _dot_tn(dsb, qb)
            dst = dst_ref[hh]
            dv_st, dqf_st, dkl_c, ddec_c = [], [], [], []
            parts = zip(reversed(range(cb)), reversed(_chunks(do)), reversed(_chunks(v)), reversed(_chunks(qf)),
                        reversed(_chunks(kl)), reversed(_chunks(dec)))
            for c, do_c, v_c, qf_c, kl_c, dec_c in parts:
                st = st_in_ref[hh, c]
                dv_st.append(_dot_nt(kl_c, dst))
                dkl_c.append(_dot(v_c, dst))
                dqf_st.append(_dot(do_c, st))
                ddec_c.append(jnp.broadcast_to(jnp.sum(dst * st, axis=0, keepdims=True), (CHUNK, dk)))
                dst = dst * dec_c[:1] + _dot_tn(do_c, qf_c)
            dst_ref[hh] = dst

            def cat(pieces):
                return jnp.concatenate(pieces[::-1], axis=0)

            dvv = dvv + cat(dv_st)
            dqf = dqf + cat(dqf_st)
            dkl = cat(dkl_c)
            dq = dqf * eb + dqb * enb
            dkk = dkf * enb + dkb * eb + dkl * el
            dv_ref[:, vcols] = dvv.astype(BF16)
            if gla:
                db = dqf * qf - dkf * kf - dqb * qb + dkb * kb - dkl * kl
                dlast = _per_chunk(dkl * kl, lambda c: jnp.sum(c, axis=0, keepdims=True)) + cat(ddec_c) * dec
                dlog_a = _dot_sel(jnp.where(bk["up"], bk["ones"], 0.0), db) + dlast
                dpre = dlog_a * (1.0 / GATE_NORM) * (1.0 - _sigmoid(bk["pre"]))
                dpre_ref[:, kcols] = dpre
                dblr_ref[hh] += jnp.sum(dpre, axis=0, keepdims=True)
                dq_ref[:, kcols] = (dq * (dk ** -0.5)).astype(BF16)
                dk_ref[:, kcols] = dkk.astype(BF16)
            else:
                cs, sn, _ = aux
                half = dk // 2
                dkk = dkk * (dk ** -0.5)
                dq_ref[:, kcols] = (dq * cs + pltpu.roll(dq * sn, half, 1)).astype(BF16)
                dk_ref[:, kcols] = (dkk * cs + pltpu.roll(dkk * sn, half, 1)).astype(BF16)

    specs, blk = _lin_specs(gla, dk, dv, off, True, nb)
    vrow = pl.BlockSpec((LB, HPS * dv), lambda g, i: (blk(i), g))
    krow = pl.BlockSpec((LB, HPS * dk), lambda g, i: (blk(i), g))
    specs += [vrow, pl.BlockSpec((HPS, cb, dv, dk), lambda g, i: (g, blk(i), 0, 0)),
              pl.BlockSpec((None, LB, HPS * dv), lambda g, i: (slot, blk(i), g)), ANY]
    section = off[("g" if gla else "r") + "q"] // (3 * d)
    out_specs = [pl.BlockSpec((LB, 3 * d), lambda g, i: (blk(i), section))]
    out_shape = [_sds(dp.shape, BF16)]
    if gla:
        out_specs += [krow, pl.BlockSpec((HPS, 1, dk), lambda g, i: (g, 0, 0)),
                      pl.BlockSpec((HPS, 1, dv), lambda g, i: (g, 0, 0))]
        out_shape += [_sds((t, d // 2), F32), _sds((LIN_HEADS, 1, dk), F32), _sds((LIN_HEADS, 1, dv), F32)]
    out_specs.append(pl.BlockSpec((HPS, dv, dk), lambda g, i: (g, 0, 0)))
    out_shape.append(_sds((LIN_HEADS, dv, dk), F32))
    res = pl.pallas_call(
        body, name="gla_bwd" if gla else "ret_bwd", grid=(LIN_HEADS // HPS, nb), in_specs=specs,
        out_specs=out_specs, out_shape=out_shape, input_output_aliases={7 + naux: 0},
        compiler_params=_cp(("parallel", "arbitrary")))(p, p, p, p, *aux_arrays, o, states, dbo, dp)
    return res[:-1]


def _row_tile(rows, cols):
    cap = max(8, (2 << 20) // (4 * cols))
    t = rows
    while t > cap and t % 2 == 0:
        t //= 2
    return t


def _add_half(name, g, t, sel):
    nchip, hr, cols = t.shape
    tr = _row_tile(hr, cols)
    nb = hr // tr

    def body(sel_ref, g_ref, t_ref, o_ref):
        o_ref[...] = g_ref[...] + t_ref[...]

    half = pl.BlockSpec((None, tr, cols), lambda p, i, s: (p, i, 0))
    gs = pltpu.PrefetchScalarGridSpec(
        num_scalar_prefetch=1, grid=(nchip, nb),
        in_specs=[pl.BlockSpec((None, tr, cols), lambda p, i, s: (p, s[0] * nb + i, 0)), half], out_specs=half)
    return pl.pallas_call(body, name=name, grid_spec=gs, out_shape=_sds(t.shape, F32),
                          compiler_params=_cp(("parallel", "parallel")))(sel, g, t)


def _sum_shards(name, h, rcv, sel):
    _, rows, cols = h.shape
    tr = _row_tile(rows, cols)

    def body(sel_ref, h_ref, r0, r1, r2, o_ref):
        o_ref[...] = ((h_ref[...] + r0[...]) + r1[...]) + r2[...]

    rspecs = [pl.BlockSpec((None, tr, cols), functools.partial(lambda i, s, j: (j, i, 0), j=j)) for j in range(3)]
    gs = pltpu.PrefetchScalarGridSpec(
        num_scalar_prefetch=1, grid=(rows // tr,),
        in_specs=[pl.BlockSpec((None, tr, cols), lambda i, s: (s[0], i, 0))] + rspecs,
        out_specs=pl.BlockSpec((tr, cols), lambda i, s: (i, 0)))
    return pl.pallas_call(body, name=name, grid_spec=gs, out_shape=_sds((rows, cols), F32),
                          compiler_params=_cp(("parallel",)))(sel, h, rcv, rcv, rcv)


def _adamw_math(w, g, m, v):
    c1 = 1.0 - ADAM_B1 ** ADAM_STEP
    c2 = 1.0 - ADAM_B2 ** ADAM_STEP
    nm = ADAM_B1 * m + (1.0 - ADAM_B1) * g
    nv = ADAM_B2 * v + (1.0 - ADAM_B2) * jnp.square(g)
    return -ADAM_LR * ((nm / c1) / (jnp.sqrt(nv / c2) + ADAM_EPS) + ADAM_WD * w), nm, nv


def _adamw(name, w, g, m, v):
    rows, cols = w.shape
    tr = _row_tile(rows, cols)

    def body(w_ref, g_ref, m_ref, v_ref, d_ref, nm_ref, nv_ref):
        d_ref[...], nm_ref[...], nv_ref[...] = _adamw_math(w_ref[...], g_ref[...], m_ref[...], v_ref[...])

    spec = pl.BlockSpec((tr, cols), lambda i: (i, 0))
    return pl.pallas_call(body, name=name, grid=(rows // tr,), in_specs=[spec] * 4, out_specs=[spec] * 3,
                          out_shape=[_sds((rows, cols), F32)] * 3, compiler_params=_cp(("parallel",)))(w, g, m, v)


def _adamw_layer(name, w, g_own, g_sib, sel, m, v, layer, prev):
    depth, rows, cols = w.shape
    tr = _row_tile(rows // 2, cols)
    nbh = rows // 2 // tr
    nprev = 0 if prev is None else 4

    def body(sel_ref, w_ref, own_ref, sib_ref, m_ref, v_ref, *rest):
        go_ref, d_ref, nm_ref, nv_ref = rest[nprev:]
        gv = jnp.where(pl.program_id(0) // nbh == sel_ref[0], own_ref[...], sib_ref[...])
        go_ref[...] = gv
        d_ref[...], nm_ref[...], nv_ref[...] = _adamw_math(w_ref[...], gv, m_ref[...], v_ref[...])

    lay = pl.BlockSpec((None, tr, cols), lambda i, s: (layer, i, 0))
    hlf = pl.BlockSpec((tr, cols), lambda i, s: (i % nbh, 0))
    gs = pltpu.PrefetchScalarGridSpec(
        num_scalar_prefetch=1, grid=(2 * nbh,), in_specs=[lay, hlf, hlf, lay, lay] + [ANY] * nprev,
        out_specs=[lay] * 4)
    args = (sel, w, g_own, g_sib, m, v) + (() if prev is None else tuple(prev))
    return pl.pallas_call(
        body, name=name, grid_spec=gs, out_shape=[_sds((depth, rows, cols), F32)] * 4,
        input_output_aliases={6 + k: k for k in range(nprev)},
        compiler_params=_cp(("parallel",)))(*args)


def _adamw_colmajor(name, wt, mt, vt, halves, sel):
    c_dim, depth, r_dim = wt.shape
    hr = r_dim // 2

    def body(sel_ref, w_ref, m_ref, v_ref, *rest):
        g_refs, (go_ref, d_ref, nm_ref, nv_ref) = rest[:2 * depth], rest[2 * depth:]
        own_first = sel_ref[0] == 0
        for l in range(depth):
            own, sib = g_refs[2 * l][...], g_refs[2 * l + 1][...]
            g = jnp.concatenate([jnp.where(own_first, own, sib), jnp.where(own_first, sib, own)], axis=0).T
            go_ref[:, l, :] = g
            d_ref[:, l, :], nm_ref[:, l, :], nv_ref[:, l, :] = _adamw_math(w_ref[:, l, :], g, m_ref[:, l, :],
                                                                          v_ref[:, l, :])

    col = pl.BlockSpec((LANE, depth, r_dim), lambda j, s: (j, 0, 0))
    gs = pltpu.PrefetchScalarGridSpec(
        num_scalar_prefetch=1, grid=(c_dim // LANE,),
        in_specs=[col] * 3 + [pl.BlockSpec((hr, LANE), lambda j, s: (0, j))] * (2 * depth), out_specs=[col] * 4)
    flat = [h for pair in halves for h in pair]
    return pl.pallas_call(body, name=name, grid_spec=gs, out_shape=[_sds(wt.shape, F32)] * 4,
                          compiler_params=_cp(("parallel",)))(sel, wt, mt, vt, *flat)


def _adamw_tail(name, wt, mt, vt, gt_tail, prev):
    c_dim, depth, r_dim = wt.shape
    nt = gt_tail.shape[0]

    def body(w_ref, m_ref, v_ref, g_ref, *rest):
        go_ref, d_ref, nm_ref, nv_ref = rest[4:]
        g = g_ref[...]
        go_ref[...] = g
        d_ref[...], nm_ref[...], nv_ref[...] = _adamw_math(w_ref[...], g, m_ref[...], v_ref[...])

    tail = pl.BlockSpec((nt, depth, r_dim), lambda i: (c_dim // nt - 1, 0, 0))
    return pl.pallas_call(
        body, name=name, grid=(1,), in_specs=[tail] * 3 + [pl.BlockSpec((nt, depth, r_dim), lambda i: (0, 0, 0))]
        + [ANY] * 4, out_specs=[tail] * 4, out_shape=[_sds(wt.shape, F32)] * 4,
        input_output_aliases={4 + k: k for k in range(4)},
        compiler_params=_cp(("arbitrary",)))(wt, mt, vt, gt_tail, *prev)


def _place():
    x, y, c = (lax.axis_index(a) for a in MESH_AXES)
    chips = [(1 - x, y), (x, 1 - y), (1 - x, 1 - y)]
    return x, y, c, chips


def _chip_index(xy):
    return 2 * xy[0] + xy[1]


ANY = pl.BlockSpec(memory_space=pl.ANY)


HBM_SPEC = pl.BlockSpec(memory_space=pltpu.HBM)
SEM = pl.BlockSpec(memory_space=pltpu.SEMAPHORE)
EFFECT = pltpu.SideEffectType.DATAFLOW_SIDE_EFFECTING


def _half(ref, c):
    hr = ref.shape[-2] // 2
    return pl.ds(pl.multiple_of(c * hr, 16), hr)


def _gather_copies(srcs, lands, send, recv):
    x, y, c, chips = _place()
    me = _chip_index((x, y))
    return [pltpu.make_async_remote_copy(src_ref=s.at[_half(s, c)], dst_ref=g.at[me, _half(s, c)],
                                         send_sem=send.at[3 * a + j], recv_sem=recv.at[3 * a + j],
                                         device_id=(*ch, c), device_id_type=DEV)
            for a, (s, g) in enumerate(zip(srcs, lands)) for j, ch in enumerate(chips)]


def _scatter_copies(srcs, lands, send, recv):
    x, y, c, chips = _place()
    return [pltpu.make_async_remote_copy(src_ref=h.at[_chip_index(ch)], dst_ref=r.at[j],
                                         send_sem=send.at[3 * a + j], recv_sem=recv.at[3 * a + j],
                                         device_id=(*ch, c), device_id_type=DEV)
            for a, (h, r) in enumerate(zip(srcs, lands)) for j, ch in enumerate(chips)]


def _in_hbm(a):
    return pltpu.with_memory_space_constraint(a, pltpu.HBM)


def _split_start(name, srcs, land_shapes, copies_fn, after=None, per_src=3):
    ns, nl = len(srcs), len(land_shapes)
    ncp = per_src * ns
    lands = [lax.empty(s.shape, s.dtype) for s in land_shapes]
    behind = [] if after is None else [after]

    def body(*refs):
        src, land = refs[:ns], refs[ns:ns + nl]
        send, recv = refs[ns + nl + len(behind)], refs[ns + nl + len(behind) + 1]
        for cp in copies_fn(src, land, send, recv):
            cp.start()
        refs[-1][...] = jnp.zeros_like(refs[-1])

    bufs = list(srcs) + lands
    outs = pl.pallas_call(
        body, name=name, in_specs=[HBM_SPEC] * (ns + nl) + [ANY] * len(behind),
        out_specs=[SEM, SEM] + [HBM_SPEC] * (ns + nl) + [pl.BlockSpec(memory_space=pltpu.VMEM)],
        out_shape=[pltpu.SemaphoreType.DMA((ncp,)), pltpu.SemaphoreType.DMA((ncp,))]
        + [pltpu.HBM(b.shape, b.dtype) for b in bufs] + [_sds((8, LANE), F32)],
        input_output_aliases={i: 2 + i for i in range(ns + nl)},
        compiler_params=pltpu.CompilerParams(has_side_effects=EFFECT))(*[_in_hbm(b) for b in bufs], *behind)
    return outs[0], outs[1], list(outs[2:2 + ns]), list(outs[2 + ns:2 + ns + nl]), outs[-1]


def _split_wait(name, started, copies_fn, after):
    send, recv, srcs, lands, _ = started
    ns, nl = len(srcs), len(lands)

    def body(*refs):
        src, land = refs[:ns], refs[ns:ns + nl]
        for cp in copies_fn(src, land, refs[ns + nl], refs[ns + nl + 1]):
            cp.wait_send()
            cp.wait_recv()

    bufs = list(srcs) + list(lands)
    outs = pl.pallas_call(
        body, name=name, in_specs=[HBM_SPEC] * (ns + nl) + [SEM, SEM, ANY], out_specs=[HBM_SPEC] * (ns + nl),
        out_shape=[pltpu.HBM(b.shape, b.dtype) for b in bufs],
        input_output_aliases={i: i for i in range(ns + nl)},
        compiler_params=pltpu.CompilerParams(has_side_effects=EFFECT))(*bufs, send, recv, after)
    return list(outs[:ns]), list(outs[ns:])


def _gather_plain(name, srcs):
    n = len(srcs)

    def body(*refs):
        src, land = refs[:n], refs[n:2 * n]
        send, recv, fsend, frecv = refs[2 * n:]
        first = _gather_copies(src, land, send, recv)
        for cp in first:
            cp.start()
        _forward_body(land, first, fsend, frecv)

    return pl.pallas_call(
        body, name=name, in_specs=[ANY] * n, out_specs=[ANY] * n,
        out_shape=[_sds((4,) + s.shape, s.dtype) for s in srcs],
        scratch_shapes=[pltpu.SemaphoreType.DMA((3 * n,))] * 4)(*srcs)


def _forward_body(land, arrivals, fsend, frecv):
    x, y, c, chips = _place()
    n = len(land)
    passed = []
    for a in range(n):
        for j, ch in enumerate(chips):
            if arrivals is not None:
                arrivals[3 * a + j].wait_recv()
            slot = land[a].at[_chip_index(ch), _half(land[a], c)]
            fw = pltpu.make_async_remote_copy(src_ref=slot, dst_ref=slot, send_sem=fsend.at[3 * a + j],
                                              recv_sem=frecv.at[3 * a + j], device_id=(x, y, 1 - c),
                                              device_id_type=DEV)
            fw.start()
            passed.append(fw)
    for a in range(n):
        for j, ch in enumerate(chips):
            slot = land[a].at[_chip_index(ch), _half(land[a], 1 - c)]
            pltpu.make_async_remote_copy(src_ref=slot, dst_ref=slot, send_sem=fsend.at[3 * a + j],
                                         recv_sem=frecv.at[3 * a + j], device_id=(x, y, c),
                                         device_id_type=DEV).wait_recv()
    for cp in passed:
        cp.wait_send()
    if arrivals is not None:
        for cp in arrivals:
            cp.wait_send()


def _gather_forward(name, lands):
    n = len(lands)

    def body(*refs):
        _forward_body(refs[n:2 * n], None, refs[2 * n], refs[2 * n + 1])

    return pl.pallas_call(
        body, name=name, in_specs=[ANY] * n, out_specs=[ANY] * n,
        out_shape=[_sds(g.shape, g.dtype) for g in lands], input_output_aliases={a: a for a in range(n)},
        scratch_shapes=[pltpu.SemaphoreType.DMA((3 * n,))] * 2)(*lands)


def _sibling_copies(srcs, lands, send, recv):
    x, y, c, _ = _place()
    return [pltpu.make_async_remote_copy(src_ref=g.at[:, _half(g, 1 - c)], dst_ref=t, send_sem=send.at[a],
                                         recv_sem=recv.at[a], device_id=(x, y, 1 - c), device_id_type=DEV)
            for a, (g, t) in enumerate(zip(srcs, lands))]


def _sibling_share(name, sms):
    n = len(sms)

    def body(*refs):
        ins, outs = refs[:n], refs[n:2 * n]
        send, recv = refs[2 * n:]
        x, y, c, _ = _place()
        cps = [pltpu.make_async_remote_copy(src_ref=ins[a], dst_ref=outs[a], send_sem=send.at[a],
                                            recv_sem=recv.at[a], device_id=(x, y, 1 - c), device_id_type=DEV)
               for a in range(n)]
        for cp in cps:
            cp.start()
        for cp in cps:
            cp.wait()

    return pl.pallas_call(
        body, name=name, in_specs=[ANY] * n, out_specs=[ANY] * n, out_shape=[_sds(s.shape, F32) for s in sms],
        scratch_shapes=[pltpu.SemaphoreType.DMA((n,))] * 2)(*sms)


def _small_allreduce(v, after=None):
    rows = v.shape[0]
    ndev = 8
    behind = [] if after is None else [after]

    def body(v_ref, *rest):
        o_ref, gat_ref, send, recv = rest[len(behind):]
        x, y, c, _ = _place()
        me = 4 * x + 2 * y + c
        cps = []
        for k in range(1, ndev):
            to = (me + k) % ndev
            cp = pltpu.make_async_remote_copy(src_ref=v_ref, dst_ref=gat_ref.at[me], send_sem=send.at[k - 1],
                                              recv_sem=recv.at[me], device_id=(to // 4, (to // 2) % 2, to % 2),
                                              device_id_type=DEV)
            cp.start()
            cps.append(cp)
        gat_ref[me] = v_ref[...]
        for k in range(1, ndev):
            frm = (me + k) % ndev
            pltpu.make_async_remote_copy(src_ref=v_ref, dst_ref=gat_ref.at[frm], send_sem=send.at[k - 1],
                                         recv_sem=recv.at[frm], device_id=(x, y, c), device_id_type=DEV).wait_recv()
        for cp in cps:
            cp.wait_send()
        acc = gat_ref[0]
        for k in range(1, ndev):
            acc = acc + gat_ref[k]
        o_ref[...] = acc

    vm = pl.BlockSpec(memory_space=pltpu.VMEM)
    return pl.pallas_call(
        body, name="small_allreduce", in_specs=[vm] + [ANY] * len(behind), out_specs=vm,
        out_shape=_sds((rows, LANE), F32),
        scratch_shapes=[pltpu.VMEM((ndev, rows, LANE), F32), pltpu.SemaphoreType.DMA((ndev - 1,)),
                        pltpu.SemaphoreType.DMA((ndev,))])(v, *behind)


def _layout(d):
    half = d // 2
    names = [("aq", d), ("ak", d), ("av", d), ("rq", half), ("rk", half), ("rv", d), ("rg", d),
             ("gq", half), ("gk", half), ("gv", d), ("gg", d), ("gates", 3 * d), ("glr", 2 * LANE)]
    off, pos = {}, 0
    for nm, sz in names:
        off[nm] = pos
        pos += sz
    return off, pos


def _unpad_cols(g, d):
    a = 8 * d + d
    return jnp.concatenate([g[..., :a], g[..., a + 3 * d:a + 3 * d + GATE_RANK], g[..., a:a + 3 * d]], axis=-1)


def kernel(x, ln_in_g, ln_in_b, w_in, rel_bias, gla_w_lr, gla_b_lr, gla_norm_g, w_branch, w_out, ln1_g, ln1_b, w_up, w_down, ln2_g, ln2_b, loss_target, m_ln_in_g, m_ln_in_b, m_w_in, m_rel_bias, m_gla_w_lr, m_gla_b_lr, m_gla_norm_g, m_w_branch, m_w_out, m_ln1_g, m_ln1_b, m_w_up, m_w_down, m_ln2_g, m_ln2_b, v_ln_in_g, v_ln_in_b, v_w_in, v_rel_bias, v_gla_w_lr, v_gla_b_lr, v_gla_norm_g, v_w_branch, v_w_out, v_ln1_g, v_ln1_b, v_w_up, v_w_down, v_ln2_g, v_ln2_b):
    t, d = x.shape[1], x.shape[2]
    dff = 4 * d
    half = d // 2
    off, npad = _layout(d)
    xi, yi, ci = (lax.axis_index(a) for a in MESH_AXES)
    chip = 2 * xi + yi
    csel = jnp.reshape(ci, (1,)).astype(jnp.int32)
    psel = jnp.reshape(chip, (1,)).astype(jnp.int32)

    big_w = [w_in, w_branch.reshape(DEPTH, -1, d), w_out, w_up, w_down]
    big_m = [m_w_in, m_w_branch.reshape(DEPTH, -1, d), m_w_out, m_w_up, m_w_down]
    big_v = [v_w_in, v_w_branch.reshape(DEPTH, -1, d), v_w_out, v_w_up, v_w_down]
    W_IN, REST = [0], [1, 2, 3, 4]

    def shards_of(l, idx):
        return [big_w[i][l].astype(BF16) for i in idx]

    def lands_of(srcs):
        return [_sds((4,) + s.shape, s.dtype) for s in srcs]

    def full_w_in(g):
        per = g.shape[2]
        a = 8 * d + d

        def run(lo, hi):
            cuts = [(max(lo, c * per), min(hi, (c + 1) * per), c) for c in range(4)]
            return [g[c, :, x - c * per:y - c * per] for x, y, c in cuts if x < y]

        zeros = jnp.zeros((d, 2 * LANE - GATE_RANK), g.dtype)
        return jnp.concatenate(run(0, a) + run(a + GATE_RANK, 4 * per) + run(a, a + GATE_RANK) + [zeros], axis=1)

    def full_rest(gs):
        g_br, g_out, g_up, g_down = gs
        return (jnp.transpose(g_br.reshape(4, N_BRANCH, d // 4, d), (1, 0, 2, 3)).reshape(N_BRANCH, d, d),
                g_out.reshape(d, d), jnp.transpose(g_up, (1, 0, 2)).reshape(d, dff), g_down.reshape(dff, d))

    def with_own(srcs, lands):
        return [lax.dynamic_update_slice(g, s[None], (chip, 0, 0)) for s, g in zip(srcs, lands)]

    def gather_start(tag, l, idx, after):
        srcs = shards_of(l, idx)
        return srcs, _split_start(f"gather_{tag}{l}_start", srcs, lands_of(srcs), _gather_copies, after)

    def gather_finish(tag, l, pending, after):
        srcs, started = pending
        _, lands = _split_wait(f"gather_{tag}{l}_wait", started, _gather_copies, after)
        return with_own(srcs, _gather_forward(f"gather_{tag}{l}_pass", lands))

    def token(pending):
        return pending[1][4][0, 0]

    win, wbr, wout, wup, wdown = ([None] * DEPTH for _ in range(5))
    src_first = shards_of(0, W_IN)
    g_first = with_own(src_first, _gather_plain("gather_in0", src_first))
    win[0] = full_w_in(g_first[0])

    dkh = half // LIN_HEADS
    lr_rows = DEPTH * GATE_RANK
    lr_slab = jnp.zeros((lr_rows, 4, half // 4), F32)
    lr_slab = lax.dynamic_update_slice(lr_slab, (gla_w_lr.reshape(lr_rows, 1, half // 4) * jnp.where(ci == 0, 1.0, 0.0)),
                                       (0, chip, 0))
    wlr_full = _small_allreduce(lr_slab.reshape(-1, LANE)).reshape(DEPTH, GATE_RANK, half)
    wlr_pad = jnp.concatenate([wlr_full, jnp.zeros((DEPTH, 2 * LANE - GATE_RANK, half), F32)], axis=1)
    pend_rest = gather_start("rest", 0, REST, wlr_full[0, :1, :1] + g_first[0][0, :1, :1].astype(F32))

    inv = 10000.0 ** (-jnp.arange(0, dkh, 2, dtype=F32) / dkh)
    ang = jnp.arange(t, dtype=F32)[:, None] * inv[None, :]
    cos, sin = jnp.cos(ang), jnp.sin(ang)
    rope_c = jnp.concatenate([cos, cos], axis=1)
    rope_s = jnp.concatenate([-sin, sin], axis=1)
    log_gamma = jnp.log1p(-jnp.exp2(-5.0 - jnp.arange(LIN_HEADS, dtype=F32)))
    lg_tab = jnp.broadcast_to(log_gamma[:, None, None], (LIN_HEADS, 1, dkh))

    def vec(a):
        return a.reshape(1, -1)

    x0, x0b, xh_in, rs_in = _ln_in(x[0], vec(ln_in_g) + token(pend_rest), vec(ln_in_b))
    saved = []
    xl, xlb = x0, x0b
    for l in range(DEPTH):
        p = _mm("proj_in", xlb, win[l], 512, 1792)
        g_rest = gather_finish("rest", l, pend_rest, p)
        wbr[l], wout[l], wup[l], wdown[l] = full_rest(g_rest)
        tok = 0.0
        if l + 1 < DEPTH:
            pend_in = gather_start("in", l + 1, W_IN, g_rest[0])
            tok = token(pend_in)
        bias = _bias_expand(rel_bias[l] + tok)
        bo = _attn_fwd(p, bias, d, off)
        ret_aux = (rope_c, rope_s, lg_tab + tok)
        gla_aux = (p, wlr_pad[l], vec(gla_b_lr[l]) + tok, vec(gla_norm_g[l]))
        o_ret, bo, st_ret = _lin_fwd(False, p, ret_aux, d, off, bo, 1)
        o_gla, bo, st_gla = _lin_fwd(True, p, gla_aux, d, off, bo, 2)
        tok = 0.0
        if l + 1 < DEPTH:
            g_in = gather_finish("in", l + 1, pend_in, bo)
            win[l + 1] = full_w_in(g_in[0])
            pend_rest = gather_start("rest", l + 1, REST, g_in[0])
            tok = token(pend_rest)
        proj, merged = _merge_fwd(bo, wbr[l], p, off["gates"])
        x1, x1b, xh1, rs1 = _mm_res_ln("out_proj_ln", merged, wout[l], xl, vec(ln1_g[l]) + tok, vec(ln1_b[l]),
                                       256, False)
        u = _mm("mlp_up", x1b, wup[l], 1024, 1024)
        x2, x2b, xh2, rs2, act = _mm_res_ln("mlp_down_ln", u, wdown[l], x1, vec(ln2_g[l]), vec(ln2_b[l]), 256, True)
        saved.append(dict(xlb=xlb, p=p, bias=bias, ret_aux=ret_aux, gla_aux=gla_aux, o_ret=o_ret, o_gla=o_gla,
                          st_ret=st_ret, st_gla=st_gla, bo=bo, proj=proj, merged=merged, x1b=x1b, xh1=xh1,
                          rs1=rs1, u=u, xh2=xh2, rs2=rs2, act=act))
        xl, xlb = x2, x2b

    small = {}
    last = saved[-1]
    loss_p, dz2, dz2b, dg, db = _loss_ln_bwd(xl, loss_target[0], last["xh2"], last["rs2"], vec(ln2_g[DEPTH - 1]))
    small["loss"] = loss_p[:, :1]
    grad_x = None

    def sibling_start(tag, l, idx, shards):
        lands = [_sds((g.shape[0], g.shape[1] // 2, g.shape[2]), F32) for g in shards]
        return tag, l, idx, _split_start(f"grad_{tag}{l}_sibling_start", shards, lands, _sibling_copies, per_src=1)

    def scatter_start(sibling, after):
        tag, l, idx, started = sibling
        shards, theirs = _split_wait(f"grad_{tag}{l}_sibling_wait", started, _sibling_copies, after)
        hs = [_add_half("grad_sibling_add", g, th, csel) for g, th in zip(shards, theirs)]
        lands = [_sds((3,) + h.shape[1:], F32) for h in hs]
        return tag, l, idx, _split_start(f"grad_{tag}{l}_scatter_start", hs, lands, _scatter_copies)

    adam_out = [None] * len(big_w)
    w_in_halves = [None] * DEPTH

    def scatter_finish(pending, after):
        tag, l, idx, started = pending
        hs, rcv = _split_wait(f"grad_{tag}{l}_scatter_wait", started, _scatter_copies, after)
        sms = [_sum_shards("grad_chip_sum", h, r, psel) for h, r in zip(hs, rcv)]
        last = None
        for i, own, sib in zip(idx, sms, _sibling_share(f"grad_{tag}{l}_share", sms)):
            if i == W_IN[0]:
                w_in_halves[l] = (own, sib)
                last = sib
            else:
                adam_out[i] = _adamw_layer("adamw_large", big_w[i], own, sib, csel, big_m[i], big_v[i], l,
                                           adam_out[i])
                last = adam_out[i][0]
        return last

    in_flight = []

    def scatter(sibling, after):
        pending = scatter_start(sibling, after)
        in_flight.append(pending)
        if len(in_flight) > 3:
            scatter_finish(in_flight.pop(0), pending[3][4])
        return pending[3][4][0, 0]

    def token_of(sibling):
        return sibling[3][4][0, 0]

    carry_tok = 0.0
    for l in reversed(range(DEPTH)):
        s = saved[l]
        small[("ln2_g", l)], small[("ln2_b", l)] = dg, db
        du = _mm_nt_relu2_bwd(dz2b, wdown[l], s["u"])
        g_wdown = _mm_tn("grad_w_down", s["act"], dz2b, 512, 512)
        g_wup = _mm_tn("grad_w_up", s["x1b"], du, 512, 512, shard="cols")
        dz1, dz1b, dg1, db1 = _mm_nt_res_lnbwd("mlp_up_bwd_ln", du, wup[l], dz2, s["xh1"], s["rs1"],
                                               vec(ln1_g[l]) + carry_tok, 256, dff)
        small[("ln1_g", l)], small[("ln1_b", l)] = dg1, db1
        dproj, dp = _merge_bwd(dz1b, wout[l], s["proj"], s["p"], off["gates"], npad)
        g_wout = _mm_tn("grad_w_out", s["merged"], dz1b, 512, 512)
        dbo = _mm("branch_proj_bwd", dproj, wbr[l], 1024, 1024, nt=True)
        g_wbr = _mm_tn("grad_w_branch", s["bo"], dproj, d // 4, 1024, shard="rows")
        sib = sibling_start("rest", l, REST, [g_wbr, g_wout.reshape(4, d // 4, d), g_wup, g_wdown.reshape(4, d, d)])
        rc, rs_, lg = s["ret_aux"]
        gp, gw, gb, gn_ = s["gla_aux"]
        dp, dk_acc, dv_acc, dbias = _attn_bwd(s["p"], s["bias"] + token_of(sib), dbo, dp, d, off)
        tok = scatter(sib, dbias)
        small[("rel_bias", l)] = _bias_reduce(dbias)
        dp = lax.dynamic_update_slice(dp, dk_acc[2 * QB:].astype(BF16), (0, off["ak"]))
        dp = lax.dynamic_update_slice(dp, dv_acc[2 * QB:].astype(BF16), (0, off["av"]))
        (dp,) = _lin_bwd(False, s["p"], (rc, rs_, lg + tok), s["o_ret"], s["st_ret"], dbo, 1, dp, d, off)
        dp, dpre, dblr, dgn = _lin_bwd(True, s["p"], (gp, gw, gb + tok, gn_), s["o_gla"], s["st_gla"], dbo, 2, dp,
                                       d, off)
        small[("gla_b_lr", l)] = dblr.reshape(1, half)
        small[("gla_norm_g", l)] = jnp.sum(dgn, axis=0)
        dpre_b = dpre.astype(BF16)
        glr_b = s["p"][:, off["glr"]:off["glr"] + LANE].astype(BF16)
        dp = _gate_lr_bwd(dpre_b, wlr_pad[l], dp, off["glr"])
        small[("gla_w_lr", l)] = _mm_tn("grad_gla_w_lr", glr_b, dpre_b, LANE, half)[:GATE_RANK]
        if l > 0:
            prev = saved[l - 1]
            xh_p, rs_p, g_p = prev["xh2"], prev["rs2"], vec(ln2_g[l - 1])
        else:
            xh_p, rs_p, g_p = xh_in, rs_in, vec(ln_in_g)
        g_win = _mm_tn("grad_w_in", s["xlb"], dp, 1024, 896)
        sib = sibling_start("in", l, W_IN, [jnp.transpose(_unpad_cols(g_win, d).reshape(d, 4, -1), (1, 0, 2))])
        if l > 0:
            tok = token_of(sib)
        else:
            tok = scatter(sib, sib[3][4])
        dzp, dzpb, dg, db = _mm_nt_res_lnbwd("proj_in_bwd_ln", dp, win[l], dz1, xh_p, rs_p, g_p + tok, 1024, 1792)
        if l > 0:
            carry_tok = scatter(sib, dzp)
        dz2, dz2b = dzp, dzpb
        grad_x = dzp
    after = grad_x
    while in_flight:
        after = scatter_finish(in_flight.pop(0), after)
    wt, mt, vt = (jnp.transpose(a, (2, 0, 1)) for a in (big_w[0], big_m[0], big_v[0]))
    ntail = wt.shape[0] % LANE
    tails = [jnp.where(ci == 0, jnp.concatenate([own[:, -ntail:], sib[:, -ntail:]]),
                       jnp.concatenate([sib[:, -ntail:], own[:, -ntail:]])).T for own, sib in w_in_halves]
    adam_t = _adamw_tail("adamw_w_in_tail", wt, mt, vt, jnp.stack(tails, axis=1),
                         _adamw_colmajor("adamw_w_in", wt, mt, vt, w_in_halves, csel))
    adam_out[0] = [jnp.transpose(r, (1, 2, 0)) for r in adam_t]
    small["ln_in_g"], small["ln_in_b"] = dg, db
    rb_pad = 3 * LANE
    pieces = [small["loss"].reshape(-1), jnp.zeros((LANE - 1,), F32), small["ln_in_g"].reshape(-1),
              small["ln_in_b"].reshape(-1)]
    for l in range(DEPTH):
        rb = jnp.pad(small[("rel_bias", l)], ((0, 0), (0, rb_pad - (2 * REL_CLIP + 1))))
        pieces += [rb.reshape(-1), small[("gla_w_lr", l)].reshape(-1), small[("gla_b_lr", l)].reshape(-1),
                   small[("gla_norm_g", l)].reshape(-1), small[("ln1_g", l)].reshape(-1),
                   small[("ln1_b", l)].reshape(-1), small[("ln2_g", l)].reshape(-1), small[("ln2_b", l)].reshape(-1)]
    sizes = [pc.shape[0] for pc in pieces]
    packed = jnp.concatenate(pieces)
    padn = (-packed.shape[0]) % (8 * LANE)
    packed = jnp.concatenate([packed, jnp.zeros((padn,), F32)]).reshape(-1, LANE)
    red = _small_allreduce(packed, after).reshape(-1)

    parts, pos = [], 0
    for sz in sizes:
        parts.append(red[pos:pos + sz])
        pos += sz
    loss = parts[0][0]
    g_ln_in_g, g_ln_in_b = parts[2], parts[3]
    per = 8
    g_rel = jnp.stack([parts[4 + per * l].reshape(ATTN_HEADS, rb_pad)[:, :2 * REL_CLIP + 1] for l in range(DEPTH)])
    g_wlr_full = jnp.stack([parts[5 + per * l].reshape(GATE_RANK, half) for l in range(DEPTH)])
    g_wlr = lax.dynamic_slice_in_dim(g_wlr_full, chip * (half // 4), half // 4, axis=2)
    g_blr = jnp.stack([parts[6 + per * l] for l in range(DEPTH)])
    g_gn = jnp.stack([parts[7 + per * l] for l in range(DEPTH)])
    g_ln1g = jnp.stack([parts[8 + per * l] for l in range(DEPTH)])
    g_ln1b = jnp.stack([parts[9 + per * l] for l in range(DEPTH)])
    g_ln2g = jnp.stack([parts[10 + per * l] for l in range(DEPTH)])
    g_ln2b = jnp.stack([parts[11 + per * l] for l in range(DEPTH)])

    grads = [g_ln_in_g, g_ln_in_b, None, g_rel, g_wlr, g_blr, g_gn, None, None, g_ln1g, g_ln1b, None, None,
             g_ln2g, g_ln2b]
    ws = [ln_in_g, ln_in_b, w_in, rel_bias, gla_w_lr, gla_b_lr, gla_norm_g, w_branch, w_out, ln1_g, ln1_b,
          w_up, w_down, ln2_g, ln2_b]
    ms = [m_ln_in_g, m_ln_in_b, m_w_in, m_rel_bias, m_gla_w_lr, m_gla_b_lr, m_gla_norm_g, m_w_branch, m_w_out,
          m_ln1_g, m_ln1_b, m_w_up, m_w_down, m_ln2_g, m_ln2_b]
    vs = [v_ln_in_g, v_ln_in_b, v_w_in, v_rel_bias, v_gla_w_lr, v_gla_b_lr, v_gla_norm_g, v_w_branch, v_w_out,
          v_ln1_g, v_ln1_b, v_w_up, v_w_down, v_ln2_g, v_ln2_b]

    deltas, new_ms, new_vs = [None] * 15, [None] * 15, [None] * 15
    big_idx = [2, 7, 8, 11, 12]
    for i, res in zip(big_idx, adam_out):
        shp = ws[i].shape
        grads[i], deltas[i], new_ms[i], new_vs[i] = (r.reshape(shp) for r in res)
    small_idx = [i for i in range(15) if i not in big_idx]

    def pack(arrs):
        flat_ = jnp.concatenate([arrs[i].reshape(-1) for i in small_idx])
        pad_ = (-flat_.shape[0]) % (8 * LANE)
        return jnp.concatenate([flat_, jnp.ones((pad_,), F32)]).reshape(-1, LANE)

    dl, nm, nv = _adamw("adamw_small", pack(ws), pack(grads), pack(ms), pack(vs))
    pos = 0
    for i in small_idx:
        sz = int(np.prod(ws[i].shape))
        deltas[i] = dl.reshape(-1)[pos:pos + sz].reshape(ws[i].shape)
        new_ms[i] = nm.reshape(-1)[pos:pos + sz].reshape(ws[i].shape)
        new_vs[i] = nv.reshape(-1)[pos:pos + sz].reshape(ws[i].shape)
        pos += sz

    return (loss, grad_x[None], *grads, *deltas, *new_ms, *new_vs)
```

```python
import functools

import numpy as np
import jax
import jax.numpy as jnp
from jax import lax
from jax.experimental import pallas as pl
from jax.experimental.pallas import tpu as pltpu

F32 = jnp.float32
BF16 = jnp.bfloat16
MXU_DTYPE = BF16
HI = lax.Precision.HIGHEST

DEPTH = 2
CHUNK = 64
N_BRANCH = 3
ATTN_HEADS = 8
ATTN_LEFT = 8
REL_CLIP = 2 * CHUNK
LIN_HEADS = 4
GATE_RANK = 16
GATE_NORM = 16.0
LN_EPS = 1e-5
NEG_INF = -1e30
ALPHA = (2 * DEPTH) ** 0.25
ADAM_LR, ADAM_B1, ADAM_B2, ADAM_EPS, ADAM_WD, ADAM_STEP = 0.001, 0.9, 0.999, 1e-08, 0.01, 10

LANE = 128
VMEM_LIMIT = 56 << 20
QB = 256
KW = 3 * QB
LB = 256
MESH_AXES = ("x", "y", "c")
DEV = pl.DeviceIdType.MESH


def _cp(sem):
    return pltpu.CompilerParams(dimension_semantics=sem, vmem_limit_bytes=VMEM_LIMIT)


def _mx(v):
    return v.astype(MXU_DTYPE)


def _dot(a, b):
    return jnp.dot(_mx(a), _mx(b), preferred_element_type=F32)


def _dot_nt(a, b):
    return lax.dot_general(_mx(a), _mx(b), (((1,), (1,)), ((), ())), preferred_element_type=F32)


def _dot_tn(a, b):
    return lax.dot_general(_mx(a), _mx(b), (((0,), (0,)), ((), ())), preferred_element_type=F32)


def _dot_hi(a, b):
    return jnp.dot(a, b, precision=HI, preferred_element_type=F32)


def _sigmoid(v):
    return 1.0 / (1.0 + jnp.exp(-v))


def _sds(shape, dtype):
    return jax.ShapeDtypeStruct(shape, dtype)


def _mm(name, a, b, tm, tn, nt=False, out_dtype=F32):
    batched = a.ndim == 3
    m, k = a.shape[-2:]
    n = b.shape[-2] if nt else b.shape[-1]
    tm, tn = min(tm, m), min(tn, n)

    def body(a_ref, b_ref, o_ref):
        f = _dot_nt if nt else _dot
        o_ref[...] = f(a_ref[...], b_ref[...]).astype(o_ref.dtype)

    rows_inner = (n // tn) * m < (m // tm) * n

    def ij(u, v):
        return (v, u) if rows_inner else (u, v)

    if batched:
        nb = a.shape[0]
        grid = (nb,) + ij(m // tm, n // tn)
        a_spec = pl.BlockSpec((None, tm, k), lambda g, u, v: (g, ij(u, v)[0], 0))
        b_spec = (pl.BlockSpec((None, tn, k), lambda g, u, v: (g, ij(u, v)[1], 0)) if nt
                  else pl.BlockSpec((None, k, tn), lambda g, u, v: (g, 0, ij(u, v)[1])))
        o_spec = pl.BlockSpec((None, tm, tn), lambda g, u, v: (g,) + ij(u, v))
        out_shape = _sds((nb, m, n), out_dtype)
        sem = ("parallel", "parallel", "parallel")
    else:
        grid = ij(m // tm, n // tn)
        a_spec = pl.BlockSpec((tm, k), lambda u, v: (ij(u, v)[0], 0))
        b_spec = (pl.BlockSpec((tn, k), lambda u, v: (ij(u, v)[1], 0)) if nt
                  else pl.BlockSpec((k, tn), lambda u, v: (0, ij(u, v)[1])))
        o_spec = pl.BlockSpec((tm, tn), lambda u, v: ij(u, v))
        out_shape = _sds((m, n), out_dtype)
        sem = ("parallel", "parallel")
    return pl.pallas_call(body, name=name, grid=grid, in_specs=[a_spec, b_spec], out_specs=o_spec,
                          out_shape=out_shape, compiler_params=_cp(sem))(a, b)


def _mm_tn(name, a, b, tm, tn, shard=None):
    batched = a.ndim == 3
    k, m = a.shape[-2:]
    n = b.shape[-1]
    tm, tn = min(tm, m), min(tn, n)

    def body(a_ref, b_ref, o_ref):
        o_ref[...] = lax.dot_general(_mx(a_ref[...]), _mx(b_ref[...]), (((0,), (0,)), ((), ())),
                                     preferred_element_type=F32)

    if batched:
        nb = a.shape[0]
        grid = (nb, m // tm, n // tn)
        a_spec = pl.BlockSpec((None, k, tm), lambda g, i, j: (g, 0, i))
        b_spec = pl.BlockSpec((None, k, tn), lambda g, i, j: (g, 0, j))
        if shard == "rows":
            assert 4 * tm == m
            o_spec = pl.BlockSpec((None, tm, tn), lambda g, i, j: (i, g, j))
            out_shape = _sds((4, nb * tm, n), F32)
        else:
            o_spec = pl.BlockSpec((None, tm, tn), lambda g, i, j: (g, i, j))
            out_shape = _sds((nb, m, n), F32)
    else:
        grid = (m // tm, n // tn)
        a_spec = pl.BlockSpec((k, tm), lambda i, j: (0, i))
        b_spec = pl.BlockSpec((k, tn), lambda i, j: (0, j))
        if shard == "cols":
            per = n // 4 // tn
            o_spec = pl.BlockSpec((None, tm, tn), lambda i, j: (j // per, i, j % per))
            out_shape = _sds((4, m, n // 4), F32)
        else:
            o_spec = pl.BlockSpec((tm, tn), lambda i, j: (i, j))
            out_shape = _sds((m, n), F32)
    return pl.pallas_call(body, name=name, grid=grid, in_specs=[a_spec, b_spec], out_specs=o_spec,
                          out_shape=out_shape, compiler_params=_cp(("parallel",) * len(grid)))(a, b)


def _ln_rows(y, g, b):
    mu = jnp.mean(y, axis=-1, keepdims=True)
    yc = y - mu
    var = jnp.mean(yc * yc, axis=-1, keepdims=True)
    rs = lax.rsqrt(var + LN_EPS)
    xh = yc * rs
    return xh * g + b, xh, rs


def _ln_in(x, g, b, tm=256):
    t, d = x.shape

    def body(x_ref, g_ref, b_ref, o_ref, ob_ref, xh_ref, rs_ref):
        o, xh, rs = _ln_rows(x_ref[...], g_ref[...], b_ref[...])
        o_ref[...] = o
        ob_ref[...] = o.astype(BF16)
        xh_ref[...] = xh
        rs_ref[...] = rs

    row = pl.BlockSpec((tm, d), lambda i: (i, 0))
    vec = pl.BlockSpec((1, d), lambda i: (0, 0))
    return pl.pallas_call(
        body, name="ln_in", grid=(t // tm,), in_specs=[row, vec, vec],
        out_specs=[row, row, row, pl.BlockSpec((tm, 1), lambda i: (i, 0))],
        out_shape=[_sds((t, d), F32), _sds((t, d), BF16), _sds((t, d), F32), _sds((t, 1), F32)],
        compiler_params=_cp(("parallel",)))(x, g, b)


def _mm_res_ln(name, a, w, res, g, b, tm, relu2):
    t, k = a.shape
    d = w.shape[1]

    def body(a_ref, w_ref, r_ref, g_ref, b_ref, o_ref, ob_ref, xh_ref, rs_ref, *act_ref):
        av = a_ref[...]
        if relu2:
            av = jnp.square(jnp.maximum(av, 0.0))
            act_ref[0][...] = av.astype(BF16)
        y = ALPHA * r_ref[...] + _dot(av, w_ref[...])
        o, xh, rs = _ln_rows(y, g_ref[...], b_ref[...])
        o_ref[...] = o
        ob_ref[...] = o.astype(BF16)
        xh_ref[...] = xh
        rs_ref[...] = rs

    row = pl.BlockSpec((tm, d), lambda i: (i, 0))
    vec = pl.BlockSpec((1, d), lambda i: (0, 0))
    arow = pl.BlockSpec((tm, k), lambda i: (i, 0))
    out_specs = [row, row, row, pl.BlockSpec((tm, 1), lambda i: (i, 0))]
    out_shape = [_sds((t, d), F32), _sds((t, d), BF16), _sds((t, d), F32), _sds((t, 1), F32)]
    if relu2:
        out_specs.append(arow)
        out_shape.append(_sds((t, k), BF16))
    return pl.pallas_call(
        body, name=name, grid=(t // tm,),
        in_specs=[arow, pl.BlockSpec((k, d), lambda i: (0, 0)), row, vec, vec],
        out_specs=out_specs, out_shape=out_shape, compiler_params=_cp(("parallel",)))(a, w, res, g, b)


def _merge_fwd(bo, wb, p, gate_off, tm=512, tn=512):
    _, t, d = bo.shape
    gb = gate_off // tn

    def body(bo_ref, wb_ref, g0, g1, g2, proj_ref, m_ref):
        acc = None
        for n, g_ref in enumerate((g0, g1, g2)):
            pr = _dot(bo_ref[n], wb_ref[n])
            proj_ref[n] = pr
            term = _sigmoid(g_ref[...]) * pr
            acc = term if acc is None else acc + term
        m_ref[...] = acc.astype(BF16)

    gspecs = [pl.BlockSpec((tm, tn), functools.partial(lambda i, j, n: (i, gb + n * (d // tn) + j), n=n))
              for n in range(3)]
    return pl.pallas_call(
        body, name="merge_fwd", grid=(t // tm, d // tn),
        in_specs=[pl.BlockSpec((3, tm, d), lambda i, j: (0, i, 0)),
                  pl.BlockSpec((3, d, tn), lambda i, j: (0, 0, j))] + gspecs,
        out_specs=[pl.BlockSpec((3, tm, tn), lambda i, j: (0, i, j)), pl.BlockSpec((tm, tn), lambda i, j: (i, j))],
        out_shape=[_sds((3, t, d), F32), _sds((t, d), BF16)],
        compiler_params=_cp(("parallel", "parallel")))(bo, wb, p, p, p)


def _merge_bwd(dz, wout, proj, p, gate_off, npad, tm=256):
    t, d = dz.shape

    def body(dz_ref, w_ref, proj_ref, g0, g1, g2, dproj_ref, dp_ref):
        dm = _dot_nt(dz_ref[...], w_ref[...])
        for n, g_ref in enumerate((g0, g1, g2)):
            s = _sigmoid(g_ref[...])
            dproj_ref[n] = (dm * s).astype(BF16)
            dp_ref[:, n * d:(n + 1) * d] = (dm * proj_ref[n] * (s * (1.0 - s))).astype(BF16)

    gspecs = [pl.BlockSpec((tm, d), functools.partial(lambda i, n: (i, gate_off // d + n), n=n)) for n in range(3)]
    return pl.pallas_call(
        body, name="merge_bwd", grid=(t // tm,),
        in_specs=[pl.BlockSpec((tm, d), lambda i: (i, 0)), pl.BlockSpec((d, d), lambda i: (0, 0)),
                  pl.BlockSpec((3, tm, d), lambda i: (0, i, 0))] + gspecs,
        out_specs=[pl.BlockSpec((3, tm, d), lambda i: (0, i, 0)),
                   pl.BlockSpec((tm, 3 * d), lambda i: (i, gate_off // (3 * d)))],
        out_shape=[_sds((3, t, d), BF16), _sds((t, npad), BF16)],
        compiler_params=_cp(("parallel",)))(dz, wout, proj, p, p, p)


def _gate_lr_bwd(dpre, wlr, dp, col_off, tm=512):
    t, k = dpre.shape
    w = wlr.shape[0]

    def body(a_ref, w_ref, dp_in, o_ref):
        o_ref[...] = _dot_nt(a_ref[...], w_ref[...]).astype(BF16)

    return pl.pallas_call(
        body, name="gate_lr_bwd", grid=(t // tm,),
        in_specs=[pl.BlockSpec((tm, k), lambda i: (i, 0)), pl.BlockSpec((w, k), lambda i: (0, 0)), ANY],
        out_specs=pl.BlockSpec((tm, w), lambda i: (i, col_off // w)), out_shape=_sds(dp.shape, BF16),
        input_output_aliases={2: 0}, compiler_params=_cp(("parallel",)))(dpre, wlr, dp)


def _mm_nt_relu2_bwd(dz, wdown, u, tm=512, tn=1024):
    t, d = dz.shape
    f = wdown.shape[0]

    def body(dz_ref, w_ref, u_ref, du_ref):
        da = _dot_nt(dz_ref[...], w_ref[...])
        du_ref[...] = (da * (2.0 * jnp.maximum(u_ref[...], 0.0))).astype(BF16)

    return pl.pallas_call(
        body, name="mlp_down_bwd", grid=(t // tm, f // tn),
        in_specs=[pl.BlockSpec((tm, d), lambda i, j: (i, 0)), pl.BlockSpec((tn, d), lambda i, j: (j, 0)),
                  pl.BlockSpec((tm, tn), lambda i, j: (i, j))],
        out_specs=pl.BlockSpec((tm, tn), lambda i, j: (i, j)), out_shape=_sds((t, f), BF16),
        compiler_params=_cp(("parallel", "parallel")))(dz, wdown, u)


def _ln_bwd_rows(dx, xh, rs, g):
    dxh = dx * g
    m1 = jnp.mean(dxh, axis=-1, keepdims=True)
    m2 = jnp.mean(dxh * xh, axis=-1, keepdims=True)
    return rs * (dxh - m1 - xh * m2)


def _mm_nt_res_lnbwd(name, a, w, dres, xh, rs, g, tm, tk):
    t, k = a.shape
    d = w.shape[0]
    nk = k // tk

    def body(a_ref, w_ref, dr_ref, xh_ref, rs_ref, g_ref, dz_ref, dzb_ref, dg_ref, db_ref, acc_ref):
        i, kk = pl.program_id(0), pl.program_id(1)

        @pl.when(kk == 0)
        def _():
            acc_ref[...] = ALPHA * dr_ref[...]

        acc_ref[...] += _dot_nt(a_ref[...], w_ref[...])

        @pl.when(jnp.logical_and(i == 0, kk == 0))
        def _():
            dg_ref[...] = jnp.zeros_like(dg_ref)
            db_ref[...] = jnp.zeros_like(db_ref)

        @pl.when(kk == nk - 1)
        def _():
            dx = acc_ref[...]
            xhv = xh_ref[...]
            dz = _ln_bwd_rows(dx, xhv, rs_ref[...], g_ref[...])
            dz_ref[...] = dz
            dzb_ref[...] = dz.astype(BF16)
            dg_ref[...] += jnp.sum(dx * xhv, axis=0, keepdims=True)
            db_ref[...] += jnp.sum(dx, axis=0, keepdims=True)

    row = pl.BlockSpec((tm, d), lambda i, kk: (i, 0))
    vec = pl.BlockSpec((1, d), lambda i, kk: (0, 0))
    return pl.pallas_call(
        body, name=name, grid=(t // tm, nk),
        in_specs=[pl.BlockSpec((tm, tk), lambda i, kk: (i, kk)), pl.BlockSpec((d, tk), lambda i, kk: (0, kk)),
                  row, row, pl.BlockSpec((tm, 1), lambda i, kk: (i, 0)), vec],
        out_specs=[row, row, vec, vec],
        out_shape=[_sds((t, d), F32), _sds((t, d), BF16), _sds((1, d), F32), _sds((1, d), F32)],
        scratch_shapes=[pltpu.VMEM((tm, d), F32)],
        compiler_params=_cp(("arbitrary", "arbitrary")))(a, w, dres, xh, rs, g)


def _loss_ln_bwd(x2, target, xh, rs, g, tm=256):
    t, d = x2.shape

    def body(x_ref, t_ref, xh_ref, rs_ref, g_ref, loss_ref, dz_ref, dzb_ref, dg_ref, db_ref):
        @pl.when(pl.program_id(0) == 0)
        def _():
            loss_ref[...] = jnp.zeros_like(loss_ref)
            dg_ref[...] = jnp.zeros_like(dg_ref)
            db_ref[...] = jnp.zeros_like(db_ref)

        err = x_ref[...] - t_ref[...]
        per_row = jnp.mean(err * err, axis=-1, keepdims=True)
        loss_ref[...] += 0.5 * jnp.sum(per_row, axis=0, keepdims=True)
        dx = err * (1.0 / d)
        xhv = xh_ref[...]
        dz = _ln_bwd_rows(dx, xhv, rs_ref[...], g_ref[...])
        dz_ref[...] = dz
        dzb_ref[...] = dz.astype(BF16)
        dg_ref[...] += jnp.sum(dx * xhv, axis=0, keepdims=True)
        db_ref[...] += jnp.sum(dx, axis=0, keepdims=True)

    row = pl.BlockSpec((tm, d), lambda i: (i, 0))
    vec = pl.BlockSpec((1, d), lambda i: (0, 0))
    return pl.pallas_call(
        body, name="loss_ln_bwd", grid=(t // tm,),
        in_specs=[row, row, row, pl.BlockSpec((tm, 1), lambda i: (i, 0)), vec],
        out_specs=[pl.BlockSpec((1, LANE), lambda i: (0, 0)), row, row, vec, vec],
        out_shape=[_sds((1, LANE), F32), _sds((t, d), F32), _sds((t, d), BF16), _sds((1, d), F32),
                   _sds((1, d), F32)],
        compiler_params=_cp(("arbitrary",)))(x2, target, xh, rs, g)


HPA_FWD = 8
HPA = 4


STRIP = 16


def _attn_scores(q_ref, k_refs, bias_ref, i, dh, hh):
    cols = pl.ds(hh * dh, dh)
    q = q_ref[:, cols] * (dh ** -0.5)
    k = jnp.concatenate([r[:, cols] for r in k_refs], axis=0)
    s = _dot_nt(q, k)
    before_start = lax.broadcasted_iota(jnp.int32, (STRIP, KW), 1) < (2 - i) * QB
    strips = []
    for r in range(0, QB, STRIP):
        ss = jnp.where(before_start, NEG_INF, s[r:r + STRIP] + bias_ref[hh, r:r + STRIP])
        e = jnp.exp(ss - jnp.max(ss, axis=-1, keepdims=True))
        strips.append(e / jnp.sum(e, axis=-1, keepdims=True))
    return q, k, strips


def _attn_specs(dh, off, hp):
    w = hp * dh
    qcol, kcol, vcol = off["aq"] // w, off["ak"] // w, off["av"] // w
    q_spec = pl.BlockSpec((QB, w), lambda g, i: (i, qcol + g))
    k_specs = [pl.BlockSpec((QB, w), functools.partial(lambda g, i, j: (jnp.maximum(i - 2 + j, 0), kcol + g), j=j))
               for j in range(3)]
    v_specs = [pl.BlockSpec((QB, w), functools.partial(lambda g, i, j: (jnp.maximum(i - 2 + j, 0), vcol + g), j=j))
               for j in range(3)]
    bias_spec = pl.BlockSpec((hp, QB, KW), lambda g, i: (g, 0, 0))
    return q_spec, k_specs, v_specs, bias_spec


def _attn_fwd(p, bias, d, off):
    t = p.shape[0]
    dh = d // ATTN_HEADS

    def body(q_ref, k0, k1, k2, v0, v1, v2, bias_ref, o_ref):
        for hh in range(HPA_FWD):
            cols = pl.ds(hh * dh, dh)
            _, _, strips = _attn_scores(q_ref, (k0, k1, k2), bias_ref, pl.program_id(1), dh, hh)
            pr = jnp.concatenate([_mx(ps) for ps in strips], axis=0)
            v = jnp.concatenate([v0[:, cols], v1[:, cols], v2[:, cols]], axis=0)
            o_ref[:, cols] = _dot(pr, v).astype(o_ref.dtype)

    q_spec, k_specs, v_specs, bias_spec = _attn_specs(dh, off, HPA_FWD)
    return pl.pallas_call(
        body, name="attn_fwd", grid=(ATTN_HEADS // HPA_FWD, t // QB),
        in_specs=[q_spec] + k_specs + v_specs + [bias_spec],
        out_specs=pl.BlockSpec((None, QB, HPA_FWD * dh), lambda g, i: (0, i, g)),
        out_shape=_sds((N_BRANCH, t, d), BF16),
        compiler_params=_cp(("parallel", "parallel")))(p, p, p, p, p, p, p, bias)


def _attn_bwd(p, bias, do, dp, d, off):
    t = p.shape[0]
    dh = d // ATTN_HEADS
    tp = t + 2 * QB

    def body(q_ref, k0, k1, k2, v0, v1, v2, bias_ref, do_ref, dp_in, dq_ref, dk_out, dv_out, dbias_ref, dk_ref,
             dv_ref):
        i = pl.program_id(1)

        @pl.when(i == 0)
        def _():
            dk_ref[...] = jnp.zeros_like(dk_ref)
            dv_ref[...] = jnp.zeros_like(dv_ref)
            dbias_ref[...] = jnp.zeros_like(dbias_ref)

        rows = pl.ds(pl.multiple_of(i * QB, QB), KW)
        for hh in range(HPA):
            cols = pl.ds(hh * dh, dh)
            q, k, strips = _attn_scores(q_ref, (k0, k1, k2), bias_ref, i, dh, hh)
            v = jnp.concatenate([v0[:, cols], v1[:, cols], v2[:, cols]], axis=0)
            dov = do_ref[:, cols]
            dp = _dot_nt(dov, v)
            ds_strips = []
            for n, ps in enumerate(strips):
                r = n * STRIP
                dps = dp[r:r + STRIP]
                dss = ps * (dps - jnp.sum(ps * dps, axis=-1, keepdims=True))
                dbias_ref[hh, r:r + STRIP] += dss
                ds_strips.append(_mx(dss))
            ds = jnp.concatenate(ds_strips, axis=0)
            pr = jnp.concatenate([_mx(ps) for ps in strips], axis=0)
            dq_ref[:, cols] = (_dot(ds, k) * (dh ** -0.5)).astype(dq_ref.dtype)
            dk_ref[rows, cols] += _dot_tn(ds, q)
            dv_ref[rows, cols] += _dot_tn(pr, dov)

        @pl.when(i == t // QB - 1)
        def _():
            mine = pl.ds(pl.multiple_of(pl.program_id(0) * w, w), w)
            pltpu.sync_copy(dk_ref, dk_out.at[:, mine])
            pltpu.sync_copy(dv_ref, dv_out.at[:, mine])

    w = HPA * dh
    q_spec, k_specs, v_specs, bias_spec = _attn_specs(dh, off, HPA)
    qcol = off["aq"] // w
    return pl.pallas_call(
        body, name="attn_bwd", grid=(ATTN_HEADS // HPA, t // QB),
        in_specs=[q_spec] + k_specs + v_specs + [bias_spec,
                                                 pl.BlockSpec((None, QB, w), lambda g, i: (0, i, g)), ANY],
        out_specs=[pl.BlockSpec((QB, w), lambda g, i: (i, qcol + g)), ANY, ANY, bias_spec],
        out_shape=[_sds(dp.shape, BF16), _sds((tp, d), F32), _sds((tp, d), F32),
                   _sds((ATTN_HEADS, QB, KW), F32)],
        scratch_shapes=[pltpu.VMEM((tp, w), F32), pltpu.VMEM((tp, w), F32)], input_output_aliases={9: 0},
        compiler_params=_cp(("parallel", "arbitrary")))(p, p, p, p, p, p, p, bias, do, dp)


def _onehot_mm(name, a, b):
    def body(a_ref, b_ref, o_ref):
        o_ref[...] = _dot_hi(a_ref[...], b_ref[...])

    return pl.pallas_call(body, name=name, out_shape=_sds((a.shape[0], b.shape[1]), F32),
                          compiler_params=pltpu.CompilerParams(vmem_limit_bytes=VMEM_LIMIT))(a, b)


def _diag_index():
    ii, jj = np.arange(CHUNK)[:, None], np.arange(CHUNK)[None, :]
    return (ii - jj + CHUNK - 1).reshape(-1)


def _bias_expand(rel_bias):
    h = rel_bias.shape[0]
    nq, nk, shift = QB // CHUNK, KW // CHUNK, (2 * QB) // CHUNK
    nbin, ndc = 3 * LANE, 4
    rb = jnp.pad(rel_bias, ((0, 0), (0, nbin - rel_bias.shape[1])))
    win = np.clip(CHUNK * np.arange(ndc)[:, None] + np.arange(LANE)[None, :] - (CHUNK - 1), -REL_CLIP, REL_CLIP)
    sel = (jnp.arange(nbin)[:, None] == jnp.asarray((win + REL_CLIP).reshape(1, -1))).astype(F32)
    windows = _onehot_mm("bias_windows", rb, sel)
    diag_t = (jnp.arange(LANE)[:, None] == jnp.asarray(_diag_index().reshape(1, -1))).astype(F32)
    blocks = _onehot_mm("bias_blocks", windows.reshape(h * ndc, LANE), diag_t).reshape(h, ndc, CHUNK, CHUNK)
    off_band = jnp.full((h, CHUNK, CHUNK), NEG_INF, F32)
    rows = []
    for ic in range(nq):
        dcs = [ic - jc + shift for jc in range(nk)]
        rows.append(jnp.concatenate([blocks[:, min(dc, ndc - 1)] if 0 <= dc <= ATTN_LEFT else off_band
                                     for dc in dcs], axis=2))
    return jnp.concatenate(rows, axis=1)


def _bias_reduce(dbias):
    h = dbias.shape[0]
    nq, nk = QB // CHUNK, KW // CHUNK
    nbin = 3 * LANE
    blocks = dbias.reshape(h, nq, CHUNK, nk, CHUNK).transpose(0, 1, 3, 2, 4).reshape(h * nq * nk, CHUNK * CHUNK)
    diag = (jnp.asarray(_diag_index().reshape(-1, 1)) == jnp.arange(LANE)[None, :]).astype(F32)
    ic = np.arange(nq)[:, None, None]
    jc = np.arange(nk)[None, :, None]
    dl = np.arange(LANE)[None, None, :] - (CHUNK - 1)
    rel = np.clip(CHUNK * (ic - jc + (2 * QB) // CHUNK) + dl, -REL_CLIP, REL_CLIP) + REL_CLIP
    bins = (jnp.asarray(rel.reshape(-1, 1)) == jnp.arange(nbin)[None, :]).astype(F32)

    diags = _onehot_mm("bias_diag_sums", blocks, diag)
    out = _onehot_mm("bias_bin_sums", diags.reshape(h, nq * nk * LANE), bins)
    return out[:, :2 * REL_CLIP + 1]


def _chunk_masks():
    r = lax.broadcasted_iota(jnp.int32, (LB, LB), 0)
    c = lax.broadcasted_iota(jnp.int32, (LB, LB), 1)
    return (r // CHUNK) == (c // CHUNK), r >= c, r <= c


def _chunks(a):
    return [a[c * CHUNK:(c + 1) * CHUNK] for c in range(LB // CHUNK)]


def _per_chunk(a, f):
    return jnp.concatenate([jnp.broadcast_to(f(c), c.shape) for c in _chunks(a)], axis=0)


def _dot_sel(sel, x):
    def top(v):
        return lax.bitcast_convert_type(lax.bitcast_convert_type(v, jnp.int32) & jnp.int32(-65536), F32)

    hi = top(x)
    mid = top(x - hi)
    lo = (x - hi) - mid
    d = functools.partial(jnp.dot, sel.astype(jnp.bfloat16), preferred_element_type=F32)
    return d(hi.astype(jnp.bfloat16)) + d(mid.astype(jnp.bfloat16)) + d(lo.astype(jnp.bfloat16))


def _lin_block(gla, q, k, v, aux):
    dk = q.shape[-1]
    same, low, up = _chunk_masks()
    ones = same.astype(F32)
    if gla:
        glr, wlr, blr = aux
        q = q * (dk ** -0.5)
        pre = _dot(glr, wlr) + blr
        log_a = (jnp.minimum(pre, 0.0) - jnp.log(1.0 + jnp.exp(-jnp.abs(pre)))) / GATE_NORM
        b = _dot_sel(jnp.where(low, ones, 0.0), log_a)
        lastb = _per_chunk(b, lambda c: c[CHUNK - 1:])
    else:
        cs, sn, lg = aux
        pre = None
        half = dk // 2
        q = q * cs + pltpu.roll(q, half, 1) * sn
        k = (k * cs + pltpu.roll(k, half, 1) * sn) * (dk ** -0.5)
        pos = (lax.broadcasted_iota(jnp.int32, (LB, dk), 0) % CHUNK).astype(F32) + 1.0
        b = pos * lg
        lastb = jnp.broadcast_to(float(CHUNK) * lg, b.shape)
    eb, enb, el, dec = jnp.exp(b), jnp.exp(-b), jnp.exp(lastb - b), jnp.exp(lastb)
    qf, kf, qb, kb, kl = q * eb, k * enb, q * enb, k * eb, k * el
    s = jnp.where(same, jnp.where(low, _dot_nt(qf, kf), _dot_nt(qb, kb)), 0.0)
    return dict(pre=pre, eb=eb, enb=enb, el=el, dec=dec, qf=qf, kf=kf, qb=qb, kb=kb, kl=kl, s=s,
                same=same, low=low, up=up, ones=ones)


def _lin_norm_gate(gla, o, gate, gn):
    sg = _sigmoid(gate)
    silu = gate * sg
    if gla:
        r = lax.rsqrt(jnp.mean(o * o, axis=-1, keepdims=True) + LN_EPS)
        hn = o * r
        return silu * (hn * gn), (sg, silu, r, hn)
    mu = jnp.mean(o, axis=-1, keepdims=True)
    oc = o - mu
    r = lax.rsqrt(jnp.mean(oc * oc, axis=-1, keepdims=True) + LN_EPS)
    hn = oc * r
    return silu * hn, (sg, silu, r, hn)


HPS = 4


def _lin_specs(gla, dk, dv, off, rev, nb):
    pre = "g" if gla else "r"
    wk, wv = HPS * dk, HPS * dv
    qc, kc, vc, gc = (off[pre + "q"] // wk, off[pre + "k"] // wk, off[pre + "v"] // wv, off[pre + "g"] // wv)

    def blk(i):
        return nb - 1 - i if rev else i

    specs = [pl.BlockSpec((LB, wk), lambda g, i: (blk(i), qc + g)),
             pl.BlockSpec((LB, wk), lambda g, i: (blk(i), kc + g)),
             pl.BlockSpec((LB, wv), lambda g, i: (blk(i), vc + g)),
             pl.BlockSpec((LB, wv), lambda g, i: (blk(i), gc + g))]
    if gla:
        specs += [pl.BlockSpec((LB, LANE), lambda g, i: (blk(i), off["glr"] // LANE)),
                  pl.BlockSpec((LANE, wk), lambda g, i: (0, g)),
                  pl.BlockSpec((1, wk), lambda g, i: (0, g)),
                  pl.BlockSpec((1, dv), lambda g, i: (0, 0))]
    else:
        specs += [pl.BlockSpec((LB, dk), lambda g, i: (blk(i), 0)),
                  pl.BlockSpec((LB, dk), lambda g, i: (blk(i), 0)),
                  pl.BlockSpec((HPS, 1, dk), lambda g, i: (g, 0, 0))]
    return specs, blk


def _lin_aux(gla, refs, rows, hh, dk):
    if gla:
        glr_ref, wlr_ref, blr_ref, gn_ref = refs
        kcols = pl.ds(hh * dk, dk)
        return (glr_ref[rows, :], wlr_ref[:, kcols], blr_ref[:, kcols]), gn_ref[...]
    cs_ref, sn_ref, lg_ref = refs
    return (cs_ref[rows, :], sn_ref[rows, :], lg_ref[hh]), None


def _lin_fwd(gla, p, aux_arrays, d, off, branches, slot):
    t = p.shape[0]
    dk, dv = d // (2 * LIN_HEADS), d // LIN_HEADS
    nb, cb = t // LB, LB // CHUNK
    naux = len(aux_arrays)

    def body(*refs):
        q_ref, k_ref, v_ref, g_ref = refs[:4]
        aux_refs = refs[4:4 + naux]
        o_ref, bo_ref, st_out_ref, st_ref = refs[5 + naux:]

        @pl.when(pl.program_id(1) == 0)
        def _():
            st_ref[...] = jnp.zeros_like(st_ref)

        rows = slice(None)
        for hh in range(HPS):
            kcols, vcols = pl.ds(hh * dk, dk), pl.ds(hh * dv, dv)
            aux, gn = _lin_aux(gla, aux_refs, rows, hh, dk)
            v = v_ref[:, vcols]
            blk = _lin_block(gla, q_ref[:, kcols], k_ref[:, kcols], v, aux)
            st = st_ref[hh]
            inter = []
            for c, (qf, kl, dec, vc) in enumerate(zip(_chunks(blk["qf"]), _chunks(blk["kl"]), _chunks(blk["dec"]),
                                                      _chunks(v))):
                st_out_ref[hh, c] = st
                inter.append(_dot_nt(qf, st))
                st = st * dec[:1] + _dot_tn(vc, kl)
            st_ref[hh] = st
            o = _dot(blk["s"], v) + jnp.concatenate(inter, axis=0)
            o_ref[:, vcols] = o
            out, _ = _lin_norm_gate(gla, o, g_ref[:, vcols], gn)
            bo_ref[:, vcols] = out.astype(BF16)

    specs, _ = _lin_specs(gla, dk, dv, off, False, nb)
    orow = pl.BlockSpec((LB, HPS * dv), lambda g, i: (i, g))
    return pl.pallas_call(
        body, name="gla_fwd" if gla else "ret_fwd", grid=(LIN_HEADS // HPS, nb), in_specs=specs + [ANY],
        out_specs=[orow, pl.BlockSpec((None, LB, HPS * dv), lambda g, i: (slot, i, g)),
                   pl.BlockSpec((HPS, cb, dv, dk), lambda g, i: (g, i, 0, 0))],
        out_shape=[_sds((t, d), F32), _sds(branches.shape, BF16), _sds((LIN_HEADS, t // CHUNK, dv, dk), F32)],
        scratch_shapes=[pltpu.VMEM((HPS, dv, dk), F32)], input_output_aliases={4 + naux: 1},
        compiler_params=_cp(("parallel", "arbitrary")))(p, p, p, p, *aux_arrays, branches)


def _lin_bwd(gla, p, aux_arrays, o, states, dbo, slot, dp, d, off):
    assert HPS == LIN_HEADS
    t = p.shape[0]
    dk, dv = d // (2 * LIN_HEADS), d // LIN_HEADS
    nb, cb = t // LB, LB // CHUNK
    naux = len(aux_arrays)

    def body(*refs):
        q_ref, k_ref, v_ref, g_ref = refs[:4]
        aux_refs = refs[4:4 + naux]
        o_ref, st_in_ref, dbo_ref = refs[4 + naux:7 + naux]
        outs = refs[8 + naux:]
        dq_ref, dk_ref = outs[0].at[:, pl.ds(0, d // 2)], outs[0].at[:, pl.ds(d // 2, d // 2)]
        dv_ref, dg_ref = outs[0].at[:, pl.ds(d, d)], outs[0].at[:, pl.ds(2 * d, d)]
        dst_ref = outs[-1]
        first = pl.program_id(1) == 0

        @pl.when(first)
        def _():
            dst_ref[...] = jnp.zeros_like(dst_ref)

        if gla:
            dpre_ref, dblr_ref, dgn_ref = outs[1:4]

            @pl.when(first)
            def _():
                dblr_ref[...] = jnp.zeros_like(dblr_ref)
                dgn_ref[...] = jnp.zeros_like(dgn_ref)

        rows = slice(None)
        for hh in range(HPS):
            kcols, vcols = pl.ds(hh * dk, dk), pl.ds(hh * dv, dv)
            aux, gn = _lin_aux(gla, aux_refs, rows, hh, dk)
            v = v_ref[:, vcols]
            bk = _lin_block(gla, q_ref[:, kcols], k_ref[:, kcols], v, aux)
            eb, enb, el, dec = bk["eb"], bk["enb"], bk["el"], bk["dec"]
            qf, kf, qb, kb, kl, s = bk["qf"], bk["kf"], bk["qb"], bk["kb"], bk["kl"], bk["s"]
            gate = g_ref[:, vcols]
            dout = dbo_ref[:, vcols]
            _, (sg, silu, r, hn) = _lin_norm_gate(gla, o_ref[:, vcols], gate, gn)
            dsilu = sg * (1.0 + gate * (1.0 - sg))
            if gla:
                y = hn * gn
                dy = dout * silu
                dg_ref[:, vcols] = (dout * y * dsilu).astype(BF16)
                dgn_ref[hh] += jnp.sum(dy * hn, axis=0, keepdims=True)
                dhn = dy * gn
                do = r * (dhn - hn * jnp.mean(dhn * hn, axis=-1, keepdims=True))
            else:
                dhn = dout * silu
                dg_ref[:, vcols] = (dout * hn * dsilu).astype(BF16)
                do = r * (dhn - jnp.mean(dhn, axis=-1, keepdims=True)
                          - hn * jnp.mean(dhn * hn, axis=-1, keepdims=True))
            ds = jnp.where(bk["same"], _dot_nt(do, v), 0.0)
            dsf = jnp.where(bk["low"], ds, 0.0)
            dsb = ds - dsf
            dvv = _dot_tn(s, do)
            dqf = _dot(dsf, kf)
            dkf = _dot_tn(dsf, qf)
            dqb = _dot(dsb, kb)
            dkb = _dot_tn(dsb, qb)
            dst = dst_ref[hh]
            dv_st, dqf_st, dkl_c, ddec_c = [], [], [], []
            parts = zip(reversed(range(cb)), reversed(_chunks(do)), reversed(_chunks(v)), reversed(_chunks(qf)),
                        reversed(_chunks(kl)), reversed(_chunks(dec)))
            for c, do_c, v_c, qf_c, kl_c, dec_c in parts:
                st = st_in_ref[hh, c]
                dv_st.append(_dot_nt(kl_c, dst))
                dkl_c.append(_dot(v_c, dst))
                dqf_st.append(_dot(do_c, st))
                ddec_c.append(jnp.broadcast_to(jnp.sum(dst * st, axis=0, keepdims=True), (CHUNK, dk)))
                dst = dst * dec_c[:1] + _dot_tn(do_c, qf_c)
            dst_ref[hh] = dst

            def cat(pieces):
                return jnp.concatenate(pieces[::-1], axis=0)

            dvv = dvv + cat(dv_st)
            dqf = dqf + cat(dqf_st)
            dkl = cat(dkl_c)
            dq = dqf * eb + dqb * enb
            dkk = dkf * enb + dkb * eb + dkl * el
            dv_ref[:, vcols] = dvv.astype(BF16)
            if gla:
                db = dqf * qf - dkf * kf - dqb * qb + dkb * kb - dkl * kl
                dlast = _per_chunk(dkl * kl, lambda c: jnp.sum(c, axis=0, keepdims=True)) + cat(ddec_c) * dec
                dlog_a = _dot_sel(jnp.where(bk["up"], bk["ones"], 0.0), db) + dlast
                dpre = dlog_a * (1.0 / GATE_NORM) * (1.0 - _sigmoid(bk["pre"]))
                dpre_ref[:, kcols] = dpre
                dblr_ref[hh] += jnp.sum(dpre, axis=0, keepdims=True)
                dq_ref[:, kcols] = (dq * (dk ** -0.5)).astype(BF16)
                dk_ref[:, kcols] = dkk.astype(BF16)
            else:
                cs, sn, _ = aux
                half = dk // 2
                dkk = dkk * (dk ** -0.5)
                dq_ref[:, kcols] = (dq * cs + pltpu.roll(dq * sn, half, 1)).astype(BF16)
                dk_ref[:, kcols] = (dkk * cs + pltpu.roll(dkk * sn, half, 1)).astype(BF16)

    specs, blk = _lin_specs(gla, dk, dv, off, True, nb)
    vrow = pl.BlockSpec((LB, HPS * dv), lambda g, i: (blk(i), g))
    krow = pl.BlockSpec((LB, HPS * dk), lambda g, i: (blk(i), g))
    specs += [vrow, pl.BlockSpec((HPS, cb, dv, dk), lambda g, i: (g, blk(i), 0, 0)),
              pl.BlockSpec((None, LB, HPS * dv), lambda g, i: (slot, blk(i), g)), ANY]
    section = off[("g" if gla else "r") + "q"] // (3 * d)
    out_specs = [pl.BlockSpec((LB, 3 * d), lambda g, i: (blk(i), section))]
    out_shape = [_sds(dp.shape, BF16)]
    if gla:
        out_specs += [krow, pl.BlockSpec((HPS, 1, dk), lambda g, i: (g, 0, 0)),
                      pl.BlockSpec((HPS, 1, dv), lambda g, i: (g, 0, 0))]
        out_shape += [_sds((t, d // 2), F32), _sds((LIN_HEADS, 1, dk), F32), _sds((LIN_HEADS, 1, dv), F32)]
    out_specs.append(pl.BlockSpec((HPS, dv, dk), lambda g, i: (g, 0, 0)))
    out_shape.append(_sds((LIN_HEADS, dv, dk), F32))
    res = pl.pallas_call(
        body, name="gla_bwd" if gla else "ret_bwd", grid=(LIN_HEADS // HPS, nb), in_specs=specs,
        out_specs=out_specs, out_shape=out_shape, input_output_aliases={7 + naux: 0},
        compiler_params=_cp(("parallel", "arbitrary")))(p, p, p, p, *aux_arrays, o, states, dbo, dp)
    return res[:-1]


def _row_tile(rows, cols):
    cap = max(8, (2 << 20) // (4 * cols))
    t = rows
    while t > cap and t % 2 == 0:
        t //= 2
    return t


def _add_half(name, g, t, sel):
    nchip, hr, cols = t.shape
    tr = _row_tile(hr, cols)
    nb = hr // tr

    def body(sel_ref, g_ref, t_ref, o_ref):
        o_ref[...] = g_ref[...] + t_ref[...]

    half = pl.BlockSpec((None, tr, cols), lambda p, i, s: (p, i, 0))
    gs = pltpu.PrefetchScalarGridSpec(
        num_scalar_prefetch=1, grid=(nchip, nb),
        in_specs=[pl.BlockSpec((None, tr, cols), lambda p, i, s: (p, s[0] * nb + i, 0)), half], out_specs=half)
    return pl.pallas_call(body, name=name, grid_spec=gs, out_shape=_sds(t.shape, F32),
                          compiler_params=_cp(("parallel", "parallel")))(sel, g, t)


def _sum_shards(name, h, rcv, sel):
    _, rows, cols = h.shape
    tr = _row_tile(rows, cols)

    def body(sel_ref, h_ref, r0, r1, r2, o_ref):
        o_ref[...] = ((h_ref[...] + r0[...]) + r1[...]) + r2[...]

    rspecs = [pl.BlockSpec((None, tr, cols), functools.partial(lambda i, s, j: (j, i, 0), j=j)) for j in range(3)]
    gs = pltpu.PrefetchScalarGridSpec(
        num_scalar_prefetch=1, grid=(rows // tr,),
        in_specs=[pl.BlockSpec((None, tr, cols), lambda i, s: (s[0], i, 0))] + rspecs,
        out_specs=pl.BlockSpec((tr, cols), lambda i, s: (i, 0)))
    return pl.pallas_call(body, name=name, grid_spec=gs, out_shape=_sds((rows, cols), F32),
                          compiler_params=_cp(("parallel",)))(sel, h, rcv, rcv, rcv)


def _adamw_math(w, g, m, v):
    c1 = 1.0 - ADAM_B1 ** ADAM_STEP
    c2 = 1.0 - ADAM_B2 ** ADAM_STEP
    nm = ADAM_B1 * m + (1.0 - ADAM_B1) * g
    nv = ADAM_B2 * v + (1.0 - ADAM_B2) * jnp.square(g)
    return -ADAM_LR * ((nm / c1) / (jnp.sqrt(nv / c2) + ADAM_EPS) + ADAM_WD * w), nm, nv


def _adamw(name, w, g, m, v):
    rows, cols = w.shape
    tr = _row_tile(rows, cols)

    def body(w_ref, g_ref, m_ref, v_ref, d_ref, nm_ref, nv_ref):
        d_ref[...], nm_ref[...], nv_ref[...] = _adamw_math(w_ref[...], g_ref[...], m_ref[...], v_ref[...])

    spec = pl.BlockSpec((tr, cols), lambda i: (i, 0))
    return pl.pallas_call(body, name=name, grid=(rows // tr,), in_specs=[spec] * 4, out_specs=[spec] * 3,
                          out_shape=[_sds((rows, cols), F32)] * 3, compiler_params=_cp(("parallel",)))(w, g, m, v)


def _adamw_layer(name, w, g_own, g_sib, sel, m, v, layer, prev):
    depth, rows, cols = w.shape
    tr = _row_tile(rows // 2, cols)
    nbh = rows // 2 // tr
    nprev = 0 if prev is None else 4

    def body(sel_ref, w_ref, own_ref, sib_ref, m_ref, v_ref, *rest):
        go_ref, d_ref, nm_ref, nv_ref = rest[nprev:]
        gv = jnp.where(pl.program_id(0) // nbh == sel_ref[0], own_ref[...], sib_ref[...])
        go_ref[...] = gv
        d_ref[...], nm_ref[...], nv_ref[...] = _adamw_math(w_ref[...], gv, m_ref[...], v_ref[...])

    lay = pl.BlockSpec((None, tr, cols), lambda i, s: (layer, i, 0))
    hlf = pl.BlockSpec((tr, cols), lambda i, s: (i % nbh, 0))
    gs = pltpu.PrefetchScalarGridSpec(
        num_scalar_prefetch=1, grid=(2 * nbh,), in_specs=[lay, hlf, hlf, lay, lay] + [ANY] * nprev,
        out_specs=[lay] * 4)
    args = (sel, w, g_own, g_sib, m, v) + (() if prev is None else tuple(prev))
    return pl.pallas_call(
        body, name=name, grid_spec=gs, out_shape=[_sds((depth, rows, cols), F32)] * 4,
        input_output_aliases={6 + k: k for k in range(nprev)},
        compiler_params=_cp(("parallel",)))(*args)


def _adamw_colmajor(name, wt, mt, vt, halves, sel):
    c_dim, depth, r_dim = wt.shape
    hr = r_dim // 2

    def body(sel_ref, w_ref, m_ref, v_ref, *rest):
        g_refs, (go_ref, d_ref, nm_ref, nv_ref) = rest[:2 * depth], rest[2 * depth:]
        own_first = sel_ref[0] == 0
        for l in range(depth):
            own, sib = g_refs[2 * l][...], g_refs[2 * l + 1][...]
            g = jnp.concatenate([jnp.where(own_first, own, sib), jnp.where(own_first, sib, own)], axis=0).T
            go_ref[:, l, :] = g
            d_ref[:, l, :], nm_ref[:, l, :], nv_ref[:, l, :] = _adamw_math(w_ref[:, l, :], g, m_ref[:, l, :],
                                                                          v_ref[:, l, :])

    col = pl.BlockSpec((LANE, depth, r_dim), lambda j, s: (j, 0, 0))
    gs = pltpu.PrefetchScalarGridSpec(
        num_scalar_prefetch=1, grid=(c_dim // LANE,),
        in_specs=[col] * 3 + [pl.BlockSpec((hr, LANE), lambda j, s: (0, j))] * (2 * depth), out_specs=[col] * 4)
    flat = [h for pair in halves for h in pair]
    return pl.pallas_call(body, name=name, grid_spec=gs, out_shape=[_sds(wt.shape, F32)] * 4,
                          compiler_params=_cp(("parallel",)))(sel, wt, mt, vt, *flat)


def _adamw_tail(name, wt, mt, vt, gt_tail, prev):
    c_dim, depth, r_dim = wt.shape
    nt = gt_tail.shape[0]

    def body(w_ref, m_ref, v_ref, g_ref, *rest):
        go_ref, d_ref, nm_ref, nv_ref = rest[4:]
        g = g_ref[...]
        go_ref[...] = g
        d_ref[...], nm_ref[...], nv_ref[...] = _adamw_math(w_ref[...], g, m_ref[...], v_ref[...])

    tail = pl.BlockSpec((nt, depth, r_dim), lambda i: (c_dim // nt - 1, 0, 0))
    return pl.pallas_call(
        body, name=name, grid=(1,), in_specs=[tail] * 3 + [pl.BlockSpec((nt, depth, r_dim), lambda i: (0, 0, 0))]
        + [ANY] * 4, out_specs=[tail] * 4, out_shape=[_sds(wt.shape, F32)] * 4,
        input_output_aliases={4 + k: k for k in range(4)},
        compiler_params=_cp(("arbitrary",)))(wt, mt, vt, gt_tail, *prev)


def _place():
    x, y, c = (lax.axis_index(a) for a in MESH_AXES)
    chips = [(1 - x, y), (x, 1 - y), (1 - x, 1 - y)]
    return x, y, c, chips


def _chip_index(xy):
    return 2 * xy[0] + xy[1]


ANY = pl.BlockSpec(memory_space=pl.ANY)


HBM_SPEC = pl.BlockSpec(memory_space=pltpu.HBM)
SEM = pl.BlockSpec(memory_space=pltpu.SEMAPHORE)
EFFECT = pltpu.SideEffectType.DATAFLOW_SIDE_EFFECTING


def _half(ref, c):
    hr = ref.shape[-2] // 2
    return pl.ds(pl.multiple_of(c * hr, 16), hr)


def _gather_copies(srcs, lands, send, recv):
    x, y, c, chips = _place()
    me = _chip_index((x, y))
    return [pltpu.make_async_remote_copy(src_ref=s.at[_half(s, c)], dst_ref=g.at[me, _half(s, c)],
                                         send_sem=send.at[3 * a + j], recv_sem=recv.at[3 * a + j],
                                         device_id=(*ch, c), device_id_type=DEV)
            for a, (s, g) in enumerate(zip(srcs, lands)) for j, ch in enumerate(chips)]


def _scatter_copies(srcs, lands, send, recv):
    x, y, c, chips = _place()
    return [pltpu.make_async_remote_copy(src_ref=h.at[_chip_index(ch)], dst_ref=r.at[j],
                                         send_sem=send.at[3 * a + j], recv_sem=recv.at[3 * a + j],
                                         device_id=(*ch, c), device_id_type=DEV)
            for a, (h, r) in enumerate(zip(srcs, lands)) for j, ch in enumerate(chips)]


def _in_hbm(a):
    return pltpu.with_memory_space_constraint(a, pltpu.HBM)


def _split_start(name, srcs, land_shapes, copies_fn, after=None, per_src=3):
    ns, nl = len(srcs), len(land_shapes)
    ncp = per_src * ns
    lands = [lax.empty(s.shape, s.dtype) for s in land_shapes]
    behind = [] if after is None else [after]

    def body(*refs):
        src, land = refs[:ns], refs[ns:ns + nl]
        send, recv = refs[ns + nl + len(behind)], refs[ns + nl + len(behind) + 1]
        for cp in copies_fn(src, land, send, recv):
            cp.start()
        refs[-1][...] = jnp.zeros_like(refs[-1])

    bufs = list(srcs) + lands
    outs = pl.pallas_call(
        body, name=name, in_specs=[HBM_SPEC] * (ns + nl) + [ANY] * len(behind),
        out_specs=[SEM, SEM] + [HBM_SPEC] * (ns + nl) + [pl.BlockSpec(memory_space=pltpu.VMEM)],
        out_shape=[pltpu.SemaphoreType.DMA((ncp,)), pltpu.SemaphoreType.DMA((ncp,))]
        + [pltpu.HBM(b.shape, b.dtype) for b in bufs] + [_sds((8, LANE), F32)],
        input_output_aliases={i: 2 + i for i in range(ns + nl)},
        compiler_params=pltpu.CompilerParams(has_side_effects=EFFECT))(*[_in_hbm(b) for b in bufs], *behind)
    return outs[0], outs[1], list(outs[2:2 + ns]), list(outs[2 + ns:2 + ns + nl]), outs[-1]


def _split_wait(name, started, copies_fn, after):
    send, recv, srcs, lands, _ = started
    ns, nl = len(srcs), len(lands)

    def body(*refs):
        src, land = refs[:ns], refs[ns:ns + nl]
        for cp in copies_fn(src, land, refs[ns + nl], refs[ns + nl + 1]):
            cp.wait_send()
            cp.wait_recv()

    bufs = list(srcs) + list(lands)
    outs = pl.pallas_call(
        body, name=name, in_specs=[HBM_SPEC] * (ns + nl) + [SEM, SEM, ANY], out_specs=[HBM_SPEC] * (ns + nl),
        out_shape=[pltpu.HBM(b.shape, b.dtype) for b in bufs],
        input_output_aliases={i: i for i in range(ns + nl)},
        compiler_params=pltpu.CompilerParams(has_side_effects=EFFECT))(*bufs, send, recv, after)
    return list(outs[:ns]), list(outs[ns:])


def _gather_plain(name, srcs):
    n = len(srcs)

    def body(*refs):
        src, land = refs[:n], refs[n:2 * n]
        send, recv, fsend, frecv = refs[2 * n:]
        first = _gather_copies(src, land, send, recv)
        for cp in first:
            cp.start()
        _forward_body(land, first, fsend, frecv)

    return pl.pallas_call(
        body, name=name, in_specs=[ANY] * n, out_specs=[ANY] * n,
        out_shape=[_sds((4,) + s.shape, s.dtype) for s in srcs],
        scratch_shapes=[pltpu.SemaphoreType.DMA((3 * n,))] * 4)(*srcs)


def _forward_body(land, arrivals, fsend, frecv):
    x, y, c, chips = _place()
    n = len(land)
    passed = []
    for a in range(n):
        for j, ch in enumerate(chips):
            if arrivals is not None:
                arrivals[3 * a + j].wait_recv()
            slot = land[a].at[_chip_index(ch), _half(land[a], c)]
            fw = pltpu.make_async_remote_copy(src_ref=slot, dst_ref=slot, send_sem=fsend.at[3 * a + j],
                                              recv_sem=frecv.at[3 * a + j], device_id=(x, y, 1 - c),
                                              device_id_type=DEV)
            fw.start()
            passed.append(fw)
    for a in range(n):
        for j, ch in enumerate(chips):
            slot = land[a].at[_chip_index(ch), _half(land[a], 1 - c)]
            pltpu.make_async_remote_copy(src_ref=slot, dst_ref=slot, send_sem=fsend.at[3 * a + j],
                                         recv_sem=frecv.at[3 * a + j], device_id=(x, y, c),
                                         device_id_type=DEV).wait_recv()
    for cp in passed:
        cp.wait_send()
    if arrivals is not None:
        for cp in arrivals:
            cp.wait_send()


def _gather_forward(name, lands):
    n = len(lands)

    def body(*refs):
        _forward_body(refs[n:2 * n], None, refs[2 * n], refs[2 * n + 1])

    return pl.pallas_call(
        body, name=name, in_specs=[ANY] * n, out_specs=[ANY] * n,
        out_shape=[_sds(g.shape, g.dtype) for g in lands], input_output_aliases={a: a for a in range(n)},
        scratch_shapes=[pltpu.SemaphoreType.DMA((3 * n,))] * 2)(*lands)


def _sibling_copies(srcs, lands, send, recv):
    x, y, c, _ = _place()
    return [pltpu.make_async_remote_copy(src_ref=g.at[:, _half(g, 1 - c)], dst_ref=t, send_sem=send.at[a],
                                         recv_sem=recv.at[a], device_id=(x, y, 1 - c), device_id_type=DEV)
            for a, (g, t) in enumerate(zip(srcs, lands))]


def _sibling_share(name, sms):
    n = len(sms)

    def body(*refs):
        ins, outs = refs[:n], refs[n:2 * n]
        send, recv = refs[2 * n:]
        x, y, c, _ = _place()
        cps = [pltpu.make_async_remote_copy(src_ref=ins[a], dst_ref=outs[a], send_sem=send.at[a],
                                            recv_sem=recv.at[a], device_id=(x, y, 1 - c), device_id_type=DEV)
               for a in range(n)]
        for cp in cps:
            cp.start()
        for cp in cps:
            cp.wait()

    return pl.pallas_call(
        body, name=name, in_specs=[ANY] * n, out_specs=[ANY] * n, out_shape=[_sds(s.shape, F32) for s in sms],
        scratch_shapes=[pltpu.SemaphoreType.DMA((n,))] * 2)(*sms)


def _small_allreduce(v, after=None):
    rows = v.shape[0]
    ndev = 8
    behind = [] if after is None else [after]

    def body(v_ref, *rest):
        o_ref, gat_ref, send, recv = rest[len(behind):]
        x, y, c, _ = _place()
        me = 4 * x + 2 * y + c
        cps = []
        for k in range(1, ndev):
            to = (me + k) % ndev
            cp = pltpu.make_async_remote_copy(src_ref=v_ref, dst_ref=gat_ref.at[me], send_sem=send.at[k - 1],
                                              recv_sem=recv.at[me], device_id=(to // 4, (to // 2) % 2, to % 2),
                                              device_id_type=DEV)
            cp.start()
            cps.append(cp)
        gat_ref[me] = v_ref[...]
        for k in range(1, ndev):
            frm = (me + k) % ndev
            pltpu.make_async_remote_copy(src_ref=v_ref, dst_ref=gat_ref.at[frm], send_sem=send.at[k - 1],
                                         recv_sem=recv.at[frm], device_id=(x, y, c), device_id_type=DEV).wait_recv()
        for cp in cps:
            cp.wait_send()
        acc = gat_ref[0]
        for k in range(1, ndev):
            acc = acc + gat_ref[k]
        o_ref[...] = acc

    vm = pl.BlockSpec(memory_space=pltpu.VMEM)
    return pl.pallas_call(
        body, name="small_allreduce", in_specs=[vm] + [ANY] * len(behind), out_specs=vm,
        out_shape=_sds((rows, LANE), F32),
        scratch_shapes=[pltpu.VMEM((ndev, rows, LANE), F32), pltpu.SemaphoreType.DMA((ndev - 1,)),
                        pltpu.SemaphoreType.DMA((ndev,))])(v, *behind)


def _layout(d):
    half = d // 2
    names = [("aq", d), ("ak", d), ("av", d), ("rq", half), ("rk", half), ("rv", d), ("rg", d),
             ("gq", half), ("gk", half), ("gv", d), ("gg", d), ("gates", 3 * d), ("glr", 2 * LANE)]
    off, pos = {}, 0
    for nm, sz in names:
        off[nm] = pos
        pos += sz
    return off, pos


def _chip_shards_of_cols(g, d):
    a = 8 * d + d
    per = (a + GATE_RANK + 3 * d) // 4
    sections = ((0, a, 0), (a, a + GATE_RANK, 3 * d), (a + GATE_RANK, 4 * per, -GATE_RANK))

    def quarter(lo, hi):
        cuts = [(max(lo, x), min(hi, y), s) for x, y, s in sections]
        return jnp.concatenate([g[:, x + s:y + s] for x, y, s in cuts if x < y], axis=1)

    return jnp.stack([quarter(c * per, (c + 1) * per) for c in range(4)])


def kernel(x, ln_in_g, ln_in_b, w_in, rel_bias, gla_w_lr, gla_b_lr, gla_norm_g, w_branch, w_out, ln1_g, ln1_b, w_up, w_down, ln2_g, ln2_b, loss_target, m_ln_in_g, m_ln_in_b, m_w_in, m_rel_bias, m_gla_w_lr, m_gla_b_lr, m_gla_norm_g, m_w_branch, m_w_out, m_ln1_g, m_ln1_b, m_w_up, m_w_down, m_ln2_g, m_ln2_b, v_ln_in_g, v_ln_in_b, v_w_in, v_rel_bias, v_gla_w_lr, v_gla_b_lr, v_gla_norm_g, v_w_branch, v_w_out, v_ln1_g, v_ln1_b, v_w_up, v_w_down, v_ln2_g, v_ln2_b):
    t, d = x.shape[1], x.shape[2]
    dff = 4 * d
    half = d // 2
    off, npad = _layout(d)
    xi, yi, ci = (lax.axis_index(a) for a in MESH_AXES)
    chip = 2 * xi + yi
    csel = jnp.reshape(ci, (1,)).astype(jnp.int32)
    psel = jnp.reshape(chip, (1,)).astype(jnp.int32)

    big_w = [w_in, w_branch.reshape(DEPTH, -1, d), w_out, w_up, w_down]
    big_m = [m_w_in, m_w_branch.reshape(DEPTH, -1, d), m_w_out, m_w_up, m_w_down]
    big_v = [v_w_in, v_w_branch.reshape(DEPTH, -1, d), v_w_out, v_w_up, v_w_down]
    W_IN, REST = [0], [1, 2, 3, 4]

    def shards_of(l, idx):
        return [big_w[i][l].astype(BF16) for i in idx]

    def lands_of(srcs):
        return [_sds((4,) + s.shape, s.dtype) for s in srcs]

    def full_w_in(g):
        per = g.shape[2]
        a = 8 * d + d

        def run(lo, hi):
            cuts = [(max(lo, c * per), min(hi, (c + 1) * per), c) for c in range(4)]
            return [g[c, :, x - c * per:y - c * per] for x, y, c in cuts if x < y]

        zeros = jnp.zeros((d, 2 * LANE - GATE_RANK), g.dtype)
        return jnp.concatenate(run(0, a) + run(a + GATE_RANK, 4 * per) + run(a, a + GATE_RANK) + [zeros], axis=1)

    def full_rest(gs):
        g_br, g_out, g_up, g_down = gs
        return (jnp.transpose(g_br.reshape(4, N_BRANCH, d // 4, d), (1, 0, 2, 3)).reshape(N_BRANCH, d, d),
                g_out.reshape(d, d), jnp.transpose(g_up, (1, 0, 2)).reshape(d, dff), g_down.reshape(dff, d))

    def with_own(srcs, lands):
        return [lax.dynamic_update_slice(g, s[None], (chip, 0, 0)) for s, g in zip(srcs, lands)]

    def gather_start(tag, l, idx, after):
        srcs = shards_of(l, idx)
        return srcs, _split_start(f"gather_{tag}{l}_start", srcs, lands_of(srcs), _gather_copies, after)

    def gather_finish(tag, l, pending, after):
        srcs, started = pending
        _, lands = _split_wait(f"gather_{tag}{l}_wait", started, _gather_copies, after)
        return with_own(srcs, _gather_forward(f"gather_{tag}{l}_pass", lands))

    def token(pending):
        return pending[1][4][0, 0]

    win, wbr, wout, wup, wdown = ([None] * DEPTH for _ in range(5))
    src_first = shards_of(0, W_IN)
    g_first = with_own(src_first, _gather_plain("gather_in0", src_first))
    win[0] = full_w_in(g_first[0])

    dkh = half // LIN_HEADS
    lr_rows = DEPTH * GATE_RANK
    lr_slab = jnp.zeros((lr_rows, 4, half // 4), F32)
    lr_slab = lax.dynamic_update_slice(lr_slab, (gla_w_lr.reshape(lr_rows, 1, half // 4) * jnp.where(ci == 0, 1.0, 0.0)),
                                       (0, chip, 0))
    wlr_full = _small_allreduce(lr_slab.reshape(-1, LANE)).reshape(DEPTH, GATE_RANK, half)
    wlr_pad = jnp.concatenate([wlr_full, jnp.zeros((DEPTH, 2 * LANE - GATE_RANK, half), F32)], axis=1)
    pend_rest = gather_start("rest", 0, REST, wlr_full[0, :1, :1] + g_first[0][0, :1, :1].astype(F32))

    inv = 10000.0 ** (-jnp.arange(0, dkh, 2, dtype=F32) / dkh)
    ang = jnp.arange(t, dtype=F32)[:, None] * inv[None, :]
    cos, sin = jnp.cos(ang), jnp.sin(ang)
    rope_c = jnp.concatenate([cos, cos], axis=1)
    rope_s = jnp.concatenate([-sin, sin], axis=1)
    log_gamma = jnp.log1p(-jnp.exp2(-5.0 - jnp.arange(LIN_HEADS, dtype=F32)))
    lg_tab = jnp.broadcast_to(log_gamma[:, None, None], (LIN_HEADS, 1, dkh))

    def vec(a):
        return a.reshape(1, -1)

    x0, x0b, xh_in, rs_in = _ln_in(x[0], vec(ln_in_g) + token(pend_rest), vec(ln_in_b))
    saved = []
    xl, xlb = x0, x0b
    for l in range(DEPTH):
        p = _mm("proj_in", xlb, win[l], 512, 1792)
        g_rest = gather_finish("rest", l, pend_rest, p)
        wbr[l], wout[l], wup[l], wdown[l] = full_rest(g_rest)
        tok = 0.0
        if l + 1 < DEPTH:
            pend_in = gather_start("in", l + 1, W_IN, g_rest[0])
            tok = token(pend_in)
        bias = _bias_expand(rel_bias[l] + tok)
        bo = _attn_fwd(p, bias, d, off)
        ret_aux = (rope_c, rope_s, lg_tab + tok)
        gla_aux = (p, wlr_pad[l], vec(gla_b_lr[l]) + tok, vec(gla_norm_g[l]))
        o_ret, bo, st_ret = _lin_fwd(False, p, ret_aux, d, off, bo, 1)
        o_gla, bo, st_gla = _lin_fwd(True, p, gla_aux, d, off, bo, 2)
        tok = 0.0
        if l + 1 < DEPTH:
            g_in = gather_finish("in", l + 1, pend_in, bo)
            win[l + 1] = full_w_in(g_in[0])
            pend_rest = gather_start("rest", l + 1, REST, g_in[0])
            tok = token(pend_rest)
        proj, merged = _merge_fwd(bo, wbr[l], p, off["gates"])
        x1, x1b, xh1, rs1 = _mm_res_ln("out_proj_ln", merged, wout[l], xl, vec(ln1_g[l]) + tok, vec(ln1_b[l]),
                                       256, False)
        u = _mm("mlp_up", x1b, wup[l], 1024, 1024)
        x2, x2b, xh2, rs2, act = _mm_res_ln("mlp_down_ln", u, wdown[l], x1, vec(ln2_g[l]), vec(ln2_b[l]), 256, True)
        saved.append(dict(xlb=xlb, p=p, bias=bias, ret_aux=ret_aux, gla_aux=gla_aux, o_ret=o_ret, o_gla=o_gla,
                          st_ret=st_ret, st_gla=st_gla, bo=bo, proj=proj, merged=merged, x1b=x1b, xh1=xh1,
                          rs1=rs1, u=u, xh2=xh2, rs2=rs2, act=act))
        xl, xlb = x2, x2b

    small = {}
    last = saved[-1]
    loss_p, dz2, dz2b, dg, db = _loss_ln_bwd(xl, loss_target[0], last["xh2"], last["rs2"], vec(ln2_g[DEPTH - 1]))
    small["loss"] = loss_p[:, :1]
    grad_x = None

    def sibling_start(tag, l, idx, shards):
        lands = [_sds((g.shape[0], g.shape[1] // 2, g.shape[2]), F32) for g in shards]
        return tag, l, idx, _split_start(f"grad_{tag}{l}_sibling_start", shards, lands, _sibling_copies, per_src=1)

    def scatter_start(sibling, after):
        tag, l, idx, started = sibling
        shards, theirs = _split_wait(f"grad_{tag}{l}_sibling_wait", started, _sibling_copies, after)
        hs = [_add_half("grad_sibling_add", g, th, csel) for g, th in zip(shards, theirs)]
        lands = [_sds((3,) + h.shape[1:], F32) for h in hs]
        return tag, l, idx, _split_start(f"grad_{tag}{l}_scatter_start", hs, lands, _scatter_copies)

    adam_out = [None] * len(big_w)
    w_in_halves = [None] * DEPTH

    def scatter_finish(pending, after):
        tag, l, idx, started = pending
        hs, rcv = _split_wait(f"grad_{tag}{l}_scatter_wait", started, _scatter_copies, after)
        sms = [_sum_shards("grad_chip_sum", h, r, psel) for h, r in zip(hs, rcv)]
        last = None
        for i, own, sib in zip(idx, sms, _sibling_share(f"grad_{tag}{l}_share", sms)):
            if i == W_IN[0]:
                w_in_halves[l] = (own, sib)
                last = sib
            else:
                adam_out[i] = _adamw_layer("adamw_large", big_w[i], own, sib, csel, big_m[i], big_v[i], l,
                                           adam_out[i])
                last = adam_out[i][0]
        return last

    in_flight = []

    def scatter(sibling, after):
        pending = scatter_start(sibling, after)
        in_flight.append(pending)
        if len(in_flight) > 3:
            scatter_finish(in_flight.pop(0), pending[3][4])
        return pending[3][4][0, 0]

    def token_of(sibling):
        return sibling[3][4][0, 0]

    carry_tok = 0.0
    for l in reversed(range(DEPTH)):
        s = saved[l]
        small[("ln2_g", l)], small[("ln2_b", l)] = dg, db
        du = _mm_nt_relu2_bwd(dz2b, wdown[l], s["u"])
        g_wdown = _mm_tn("grad_w_down", s["act"], dz2b, 512, 512)
        g_wup = _mm_tn("grad_w_up", s["x1b"], du, 512, 512, shard="cols")
        dz1, dz1b, dg1, db1 = _mm_nt_res_lnbwd("mlp_up_bwd_ln", du, wup[l], dz2, s["xh1"], s["rs1"],
                                               vec(ln1_g[l]) + carry_tok, 256, dff)
        small[("ln1_g", l)], small[("ln1_b", l)] = dg1, db1
        dproj, dp = _merge_bwd(dz1b, wout[l], s["proj"], s["p"], off["gates"], npad)
        g_wout = _mm_tn("grad_w_out", s["merged"], dz1b, 512, 512)
        dbo = _mm("branch_proj_bwd", dproj, wbr[l], 1024, 1024, nt=True)
        g_wbr = _mm_tn("grad_w_branch", s["bo"], dproj, d // 4, 1024, shard="rows")
        sib = sibling_start("rest", l, REST, [g_wbr, g_wout.reshape(4, d // 4, d), g_wup, g_wdown.reshape(4, d, d)])
        rc, rs_, lg = s["ret_aux"]
        gp, gw, gb, gn_ = s["gla_aux"]
        dp, dk_acc, dv_acc, dbias = _attn_bwd(s["p"], s["bias"] + token_of(sib), dbo, dp, d, off)
        tok = scatter(sib, dbias)
        small[("rel_bias", l)] = _bias_reduce(dbias)
        dp = lax.dynamic_update_slice(dp, dk_acc[2 * QB:].astype(BF16), (0, off["ak"]))
        dp = lax.dynamic_update_slice(dp, dv_acc[2 * QB:].astype(BF16), (0, off["av"]))
        (dp,) = _lin_bwd(False, s["p"], (rc, rs_, lg + tok), s["o_ret"], s["st_ret"], dbo, 1, dp, d, off)
        dp, dpre, dblr, dgn = _lin_bwd(True, s["p"], (gp, gw, gb + tok, gn_), s["o_gla"], s["st_gla"], dbo, 2, dp,
                                       d, off)
        small[("gla_b_lr", l)] = dblr.reshape(1, half)
        small[("gla_norm_g", l)] = jnp.sum(dgn, axis=0)
        dpre_b = dpre.astype(BF16)
        glr_b = s["p"][:, off["glr"]:off["glr"] + LANE].astype(BF16)
        dp = _gate_lr_bwd(dpre_b, wlr_pad[l], dp, off["glr"])
        small[("gla_w_lr", l)] = _mm_tn("grad_gla_w_lr", glr_b, dpre_b, LANE, half)[:GATE_RANK]
        if l > 0:
            prev = saved[l - 1]
            xh_p, rs_p, g_p = prev["xh2"], prev["rs2"], vec(ln2_g[l - 1])
        else:
            xh_p, rs_p, g_p = xh_in, rs_in, vec(ln_in_g)
        g_win = _mm_tn("grad_w_in", s["xlb"], dp, 1024, 896)
        sib = sibling_start("in", l, W_IN, [_chip_shards_of_cols(g_win, d)])
        if l > 0:
            tok = token_of(sib)
        else:
            tok = scatter(sib, sib[3][4])
        dzp, dzpb, dg, db = _mm_nt_res_lnbwd("proj_in_bwd_ln", dp, win[l], dz1, xh_p, rs_p, g_p + tok, 1024, 1792)
        if l > 0:
            carry_tok = scatter(sib, dzp)
        dz2, dz2b = dzp, dzpb
        grad_x = dzp
    after = grad_x
    while in_flight:
        after = scatter_finish(in_flight.pop(0), after)
    wt, mt, vt = (jnp.transpose(a, (2, 0, 1)) for a in (big_w[0], big_m[0], big_v[0]))
    ntail = wt.shape[0] % LANE
    tails = [jnp.where(ci == 0, jnp.concatenate([own[:, -ntail:], sib[:, -ntail:]]),
                       jnp.concatenate([sib[:, -ntail:], own[:, -ntail:]])).T for own, sib in w_in_halves]
    adam_t = _adamw_tail("adamw_w_in_tail", wt, mt, vt, jnp.stack(tails, axis=1),
                         _adamw_colmajor("adamw_w_in", wt, mt, vt, w_in_halves, csel))
    adam_out[0] = [jnp.transpose(r, (1, 2, 0)) for r in adam_t]
    small["ln_in_g"], small["ln_in_b"] = dg, db
    rb_pad = 3 * LANE
    pieces = [small["loss"].reshape(-1), jnp.zeros((LANE - 1,), F32), small["ln_in_g"].reshape(-1),
              small["ln_in_b"].reshape(-1)]
    for l in range(DEPTH):
        rb = jnp.pad(small[("rel_bias", l)], ((0, 0), (0, rb_pad - (2 * REL_CLIP + 1))))
        pieces += [rb.reshape(-1), small[("gla_w_lr", l)].reshape(-1), small[("gla_b_lr", l)].reshape(-1),
                   small[("gla_norm_g", l)].reshape(-1), small[("ln1_g", l)].reshape(-1),
                   small[("ln1_b", l)].reshape(-1), small[("ln2_g", l)].reshape(-1), small[("ln2_b", l)].reshape(-1)]
    sizes = [pc.shape[0] for pc in pieces]
    packed = jnp.concatenate(pieces)
    padn = (-packed.shape[0]) % (8 * LANE)
    packed = jnp.concatenate([packed, jnp.zeros((padn,), F32)]).reshape(-1, LANE)
    red = _small_allreduce(packed, after).reshape(-1)

    parts, pos = [], 0
    for sz in sizes:
        parts.append(red[pos:pos + sz])
        pos += sz
    loss = parts[0][0]
    g_ln_in_g, g_ln_in_b = parts[2], parts[3]
    per = 8
    g_rel = jnp.stack([parts[4 + per * l].reshape(ATTN_HEADS, rb_pad)[:, :2 * REL_CLIP + 1] for l in range(DEPTH)])
    g_wlr_full = jnp.stack([parts[5 + per * l].reshape(GATE_RANK, half) for l in range(DEPTH)])
    g_wlr = lax.dynamic_slice_in_dim(g_wlr_full, chip * (half // 4), half // 4, axis=2)
    g_blr = jnp.stack([parts[6 + per * l] for l in range(DEPTH)])
    g_gn = jnp.stack([parts[7 + per * l] for l in range(DEPTH)])
    g_ln1g = jnp.stack([parts[8 + per * l] for l in range(DEPTH)])
    g_ln1b = jnp.stack([parts[9 + per * l] for l in range(DEPTH)])
    g_ln2g = jnp.stack([parts[10 + per * l] for l in range(DEPTH)])
    g_ln2b = jnp.stack([parts[11 + per * l] for l in range(DEPTH)])

    grads = [g_ln_in_g, g_ln_in_b, None, g_rel, g_wlr, g_blr, g_gn, None, None, g_ln1g, g_ln1b, None, None,
             g_ln2g, g_ln2b]
    ws = [ln_in_g, ln_in_b, w_in, rel_bias, gla_w_lr, gla_b_lr, gla_norm_g, w_branch, w_out, ln1_g, ln1_b,
          w_up, w_down, ln2_g, ln2_b]
    ms = [m_ln_in_g, m_ln_in_b, m_w_in, m_rel_bias, m_gla_w_lr, m_gla_b_lr, m_gla_norm_g, m_w_branch, m_w_out,
          m_ln1_g, m_ln1_b, m_w_up, m_w_down, m_ln2_g, m_ln2_b]
    vs = [v_ln_in_g, v_ln_in_b, v_w_in, v_rel_bias, v_gla_w_lr, v_gla_b_lr, v_gla_norm_g, v_w_branch, v_w_out,
          v_ln1_g, v_ln1_b, v_w_up, v_w_down, v_ln2_g, v_ln2_b]

    deltas, new_ms, new_vs = [None] * 15, [None] * 15, [None] * 15
    big_idx = [2, 7, 8, 11, 12]
    for i, res in zip(big_idx, adam_out):
        shp = ws[i].shape
        grads[i], deltas[i], new_ms[i], new_vs[i] = (r.reshape(shp) for r in res)
    small_idx = [i for i in range(15) if i not in big_idx]

    def pack(arrs):
        flat_ = jnp.concatenate([arrs[i].reshape(-1) for i in small_idx])
        pad_ = (-flat_.shape[0]) % (8 * LANE)
        return jnp.concatenate([flat_, jnp.ones((pad_,), F32)]).reshape(-1, LANE)

    dl, nm, nv = _adamw("adamw_small", pack(ws), pack(grads), pack(ms), pack(vs))
    pos = 0
    for i in small_idx:
        sz = int(np.prod(ws[i].shape))
        deltas[i] = dl.reshape(-1)[pos:pos + sz].reshape(ws[i].shape)
        new_ms[i] = nm.reshape(-1)[pos:pos + sz].reshape(ws[i].shape)
        new_vs[i] = nv.reshape(-1)[pos:pos + sz].reshape(ws[i].shape)
        pos += sz

    return (loss, grad_x[None], *grads, *deltas, *new_ms, *new_vs)
```

```python
import functools

import numpy as np
import jax
import jax.numpy as jnp
from jax import lax
from jax.experimental import pallas as pl
from jax.experimental.pallas import tpu as pltpu

F32 = jnp.float32
BF16 = jnp.bfloat16
MXU_DTYPE = BF16
HI = lax.Precision.HIGHEST

DEPTH = 2
CHUNK = 64
N_BRANCH = 3
ATTN_HEADS = 8
ATTN_LEFT = 8
REL_CLIP = 2 * CHUNK
LIN_HEADS = 4
GATE_RANK = 16
GATE_NORM = 16.0
LN_EPS = 1e-5
NEG_INF = -1e30
ALPHA = (2 * DEPTH) ** 0.25
ADAM_LR, ADAM_B1, ADAM_B2, ADAM_EPS, ADAM_WD, ADAM_STEP = 0.001, 0.9, 0.999, 1e-08, 0.01, 10

LANE = 128
VMEM_LIMIT = 56 << 20
QB = 256
KW = 3 * QB
LB = 256
MESH_AXES = ("x", "y", "c")
DEV = pl.DeviceIdType.MESH


def _cp(sem):
    return pltpu.CompilerParams(dimension_semantics=sem, vmem_limit_bytes=VMEM_LIMIT)


def _mx(v):
    return v.astype(MXU_DTYPE)


def _dot(a, b):
    return jnp.dot(_mx(a), _mx(b), preferred_element_type=F32)


def _dot_nt(a, b):
    return lax.dot_general(_mx(a), _mx(b), (((1,), (1,)), ((), ())), preferred_element_type=F32)


def _dot_tn(a, b):
    return lax.dot_general(_mx(a), _mx(b), (((0,), (0,)), ((), ())), preferred_element_type=F32)


def _dot_hi(a, b):
    return jnp.dot(a, b, precision=HI, preferred_element_type=F32)


def _sigmoid(v):
    return 1.0 / (1.0 + jnp.exp(-v))


def _sds(shape, dtype):
    return jax.ShapeDtypeStruct(shape, dtype)


def _mm(name, a, b, tm, tn, nt=False, out_dtype=F32):
    batched = a.ndim == 3
    m, k = a.shape[-2:]
    n = b.shape[-2] if nt else b.shape[-1]
    tm, tn = min(tm, m), min(tn, n)

    def body(a_ref, b_ref, o_ref):
        f = _dot_nt if nt else _dot
        o_ref[...] = f(a_ref[...], b_ref[...]).astype(o_ref.dtype)

    rows_inner = (n // tn) * m < (m // tm) * n

    def ij(u, v):
        return (v, u) if rows_inner else (u, v)

    if batched:
        nb = a.shape[0]
        grid = (nb,) + ij(m // tm, n // tn)
        a_spec = pl.BlockSpec((None, tm, k), lambda g, u, v: (g, ij(u, v)[0], 0))
        b_spec = (pl.BlockSpec((None, tn, k), lambda g, u, v: (g, ij(u, v)[1], 0)) if nt
                  else pl.BlockSpec((None, k, tn), lambda g, u, v: (g, 0, ij(u, v)[1])))
        o_spec = pl.BlockSpec((None, tm, tn), lambda g, u, v: (g,) + ij(u, v))
        out_shape = _sds((nb, m, n), out_dtype)
        sem = ("parallel", "parallel", "parallel")
    else:
        grid = ij(m // tm, n // tn)
        a_spec = pl.BlockSpec((tm, k), lambda u, v: (ij(u, v)[0], 0))
        b_spec = (pl.BlockSpec((tn, k), lambda u, v: (ij(u, v)[1], 0)) if nt
                  else pl.BlockSpec((k, tn), lambda u, v: (0, ij(u, v)[1])))
        o_spec = pl.BlockSpec((tm, tn), lambda u, v: ij(u, v))
        out_shape = _sds((m, n), out_dtype)
        sem = ("parallel", "parallel")
    return pl.pallas_call(body, name=name, grid=grid, in_specs=[a_spec, b_spec], out_specs=o_spec,
                          out_shape=out_shape, compiler_params=_cp(sem))(a, b)


def _mm_tn(name, a, b, tm, tn, shard=None):
    batched = a.ndim == 3
    k, m = a.shape[-2:]
    n = b.shape[-1]
    tm, tn = min(tm, m), min(tn, n)

    def body(a_ref, b_ref, o_ref):
        o_ref[...] = lax.dot_general(_mx(a_ref[...]), _mx(b_ref[...]), (((0,), (0,)), ((), ())),
                                     preferred_element_type=F32)

    if batched:
        nb = a.shape[0]
        grid = (nb, m // tm, n // tn)
        a_spec = pl.BlockSpec((None, k, tm), lambda g, i, j: (g, 0, i))
        b_spec = pl.BlockSpec((None, k, tn), lambda g, i, j: (g, 0, j))
        if shard == "rows":
            assert 4 * tm == m
            o_spec = pl.BlockSpec((None, tm, tn), lambda g, i, j: (i, g, j))
            out_shape = _sds((4, nb * tm, n), F32)
        else:
            o_spec = pl.BlockSpec((None, tm, tn), lambda g, i, j: (g, i, j))
            out_shape = _sds((nb, m, n), F32)
    else:
        grid = (m // tm, n // tn)
        a_spec = pl.BlockSpec((k, tm), lambda i, j: (0, i))
        b_spec = pl.BlockSpec((k, tn), lambda i, j: (0, j))
        if shard == "cols":
            per = n // 4 // tn
            o_spec = pl.BlockSpec((None, tm, tn), lambda i, j: (j // per, i, j % per))
            out_shape = _sds((4, m, n // 4), F32)
        else:
            o_spec = pl.BlockSpec((tm, tn), lambda i, j: (i, j))
            out_shape = _sds((m, n), F32)
    return pl.pallas_call(body, name=name, grid=grid, in_specs=[a_spec, b_spec], out_specs=o_spec,
                          out_shape=out_shape, compiler_params=_cp(("parallel",) * len(grid)))(a, b)


def _ln_rows(y, g, b):
    mu = jnp.mean(y, axis=-1, keepdims=True)
    yc = y - mu
    var = jnp.mean(yc * yc, axis=-1, keepdims=True)
    rs = lax.rsqrt(var + LN_EPS)
    xh = yc * rs
    return xh * g + b, xh, rs


def _ln_in(x, g, b, tm=256):
    t, d = x.shape

    def body(x_ref, g_ref, b_ref, o_ref, ob_ref, xh_ref, rs_ref):
        o, xh, rs = _ln_rows(x_ref[...], g_ref[...], b_ref[...])
        o_ref[...] = o
        ob_ref[...] = o.astype(BF16)
        xh_ref[...] = xh
        rs_ref[...] = rs

    row = pl.BlockSpec((tm, d), lambda i: (i, 0))
    vec = pl.BlockSpec((1, d), lambda i: (0, 0))
    return pl.pallas_call(
        body, name="ln_in", grid=(t // tm,), in_specs=[row, vec, vec],
        out_specs=[row, row, row, pl.BlockSpec((tm, 1), lambda i: (i, 0))],
        out_shape=[_sds((t, d), F32), _sds((t, d), BF16), _sds((t, d), F32), _sds((t, 1), F32)],
        compiler_params=_cp(("parallel",)))(x, g, b)


def _mm_res_ln(name, a, w, res, g, b, tm, relu2):
    t, k = a.shape
    d = w.shape[1]

    def body(a_ref, w_ref, r_ref, g_ref, b_ref, o_ref, ob_ref, xh_ref, rs_ref, *act_ref):
        av = a_ref[...]
        if relu2:
            av = jnp.square(jnp.maximum(av, 0.0))
            act_ref[0][...] = av.astype(BF16)
        y = ALPHA * r_ref[...] + _dot(av, w_ref[...])
        o, xh, rs = _ln_rows(y, g_ref[...], b_ref[...])
        o_ref[...] = o
        ob_ref[...] = o.astype(BF16)
        xh_ref[...] = xh
        rs_ref[...] = rs

    row = pl.BlockSpec((tm, d), lambda i: (i, 0))
    vec = pl.BlockSpec((1, d), lambda i: (0, 0))
    arow = pl.BlockSpec((tm, k), lambda i: (i, 0))
    out_specs = [row, row, row, pl.BlockSpec((tm, 1), lambda i: (i, 0))]
    out_shape = [_sds((t, d), F32), _sds((t, d), BF16), _sds((t, d), F32), _sds((t, 1), F32)]
    if relu2:
        out_specs.append(arow)
        out_shape.append(_sds((t, k), BF16))
    return pl.pallas_call(
        body, name=name, grid=(t // tm,),
        in_specs=[arow, pl.BlockSpec((k, d), lambda i: (0, 0)), row, vec, vec],
        out_specs=out_specs, out_shape=out_shape, compiler_params=_cp(("parallel",)))(a, w, res, g, b)


def _merge_fwd(bo, wb, p, gate_off, tm=512, tn=512):
    _, t, d = bo.shape
    gb = gate_off // tn

    def body(bo_ref, wb_ref, g0, g1, g2, proj_ref, m_ref):
        acc = None
        for n, g_ref in enumerate((g0, g1, g2)):
            pr = _dot(bo_ref[n], wb_ref[n])
            proj_ref[n] = pr
            term = _sigmoid(g_ref[...]) * pr
            acc = term if acc is None else acc + term
        m_ref[...] = acc.astype(BF16)

    gspecs = [pl.BlockSpec((tm, tn), functools.partial(lambda i, j, n: (i, gb + n * (d // tn) + j), n=n))
              for n in range(3)]
    return pl.pallas_call(
        body, name="merge_fwd", grid=(t // tm, d // tn),
        in_specs=[pl.BlockSpec((3, tm, d), lambda i, j: (0, i, 0)),
                  pl.BlockSpec((3, d, tn), lambda i, j: (0, 0, j))] + gspecs,
        out_specs=[pl.BlockSpec((3, tm, tn), lambda i, j: (0, i, j)), pl.BlockSpec((tm, tn), lambda i, j: (i, j))],
        out_shape=[_sds((3, t, d), F32), _sds((t, d), BF16)],
        compiler_params=_cp(("parallel", "parallel")))(bo, wb, p, p, p)


def _merge_bwd(dz, wout, proj, p, gate_off, npad, tm=256):
    t, d = dz.shape

    def body(dz_ref, w_ref, proj_ref, g0, g1, g2, dproj_ref, dp_ref):
        dm = _dot_nt(dz_ref[...], w_ref[...])
        for n, g_ref in enumerate((g0, g1, g2)):
            s = _sigmoid(g_ref[...])
            dproj_ref[n] = (dm * s).astype(BF16)
            dp_ref[:, n * d:(n + 1) * d] = (dm * proj_ref[n] * (s * (1.0 - s))).astype(BF16)

    gspecs = [pl.BlockSpec((tm, d), functools.partial(lambda i, n: (i, gate_off // d + n), n=n)) for n in range(3)]
    return pl.pallas_call(
        body, name="merge_bwd", grid=(t // tm,),
        in_specs=[pl.BlockSpec((tm, d), lambda i: (i, 0)), pl.BlockSpec((d, d), lambda i: (0, 0)),
                  pl.BlockSpec((3, tm, d), lambda i: (0, i, 0))] + gspecs,
        out_specs=[pl.BlockSpec((3, tm, d), lambda i: (0, i, 0)),
                   pl.BlockSpec((tm, 3 * d), lambda i: (i, gate_off // (3 * d)))],
        out_shape=[_sds((3, t, d), BF16), _sds((t, npad), BF16)],
        compiler_params=_cp(("parallel",)))(dz, wout, proj, p, p, p)


def _gate_lr_bwd(dpre, wlr, dp, col_off, tm=512):
    t, k = dpre.shape
    w = wlr.shape[0]

    def body(a_ref, w_ref, dp_in, o_ref):
        o_ref[...] = _dot_nt(a_ref[...], w_ref[...]).astype(BF16)

    return pl.pallas_call(
        body, name="gate_lr_bwd", grid=(t // tm,),
        in_specs=[pl.BlockSpec((tm, k), lambda i: (i, 0)), pl.BlockSpec((w, k), lambda i: (0, 0)), ANY],
        out_specs=pl.BlockSpec((tm, w), lambda i: (i, col_off // w)), out_shape=_sds(dp.shape, BF16),
        input_output_aliases={2: 0}, compiler_params=_cp(("parallel",)))(dpre, wlr, dp)


def _mm_nt_relu2_bwd(dz, wdown, u, tm=512, tn=1024):
    t, d = dz.shape
    f = wdown.shape[0]

    def body(dz_ref, w_ref, u_ref, du_ref):
        da = _dot_nt(dz_ref[...], w_ref[...])
        du_ref[...] = (da * (2.0 * jnp.maximum(u_ref[...], 0.0))).astype(BF16)

    return pl.pallas_call(
        body, name="mlp_down_bwd", grid=(f // tn, t // tm),
        in_specs=[pl.BlockSpec((tm, d), lambda j, i: (i, 0)), pl.BlockSpec((tn, d), lambda j, i: (j, 0)),
                  pl.BlockSpec((tm, tn), lambda j, i: (i, j))],
        out_specs=pl.BlockSpec((tm, tn), lambda j, i: (i, j)), out_shape=_sds((t, f), BF16),
        compiler_params=_cp(("parallel", "parallel")))(dz, wdown, u)


def _ln_bwd_rows(dx, xh, rs, g):
    dxh = dx * g
    m1 = jnp.mean(dxh, axis=-1, keepdims=True)
    m2 = jnp.mean(dxh * xh, axis=-1, keepdims=True)
    return rs * (dxh - m1 - xh * m2)


def _mm_nt_res_lnbwd(name, a, w, dres, xh, rs, g, tm, tk):
    t, k = a.shape
    d = w.shape[0]
    nk = k // tk

    def body(a_ref, w_ref, dr_ref, xh_ref, rs_ref, g_ref, dz_ref, dzb_ref, dg_ref, db_ref, acc_ref):
        i, kk = pl.program_id(0), pl.program_id(1)

        @pl.when(kk == 0)
        def _():
            acc_ref[...] = ALPHA * dr_ref[...]

        acc_ref[...] += _dot_nt(a_ref[...], w_ref[...])

        @pl.when(jnp.logical_and(i == 0, kk == 0))
        def _():
            dg_ref[...] = jnp.zeros_like(dg_ref)
            db_ref[...] = jnp.zeros_like(db_ref)

        @pl.when(kk == nk - 1)
        def _():
            dx = acc_ref[...]
            xhv = xh_ref[...]
            dz = _ln_bwd_rows(dx, xhv, rs_ref[...], g_ref[...])
            dz_ref[...] = dz
            dzb_ref[...] = dz.astype(BF16)
            dg_ref[...] += jnp.sum(dx * xhv, axis=0, keepdims=True)
            db_ref[...] += jnp.sum(dx, axis=0, keepdims=True)

    row = pl.BlockSpec((tm, d), lambda i, kk: (i, 0))
    vec = pl.BlockSpec((1, d), lambda i, kk: (0, 0))
    return pl.pallas_call(
        body, name=name, grid=(t // tm, nk),
        in_specs=[pl.BlockSpec((tm, tk), lambda i, kk: (i, kk)), pl.BlockSpec((d, tk), lambda i, kk: (0, kk)),
                  row, row, pl.BlockSpec((tm, 1), lambda i, kk: (i, 0)), vec],
        out_specs=[row, row, vec, vec],
        out_shape=[_sds((t, d), F32), _sds((t, d), BF16), _sds((1, d), F32), _sds((1, d), F32)],
        scratch_shapes=[pltpu.VMEM((tm, d), F32)],
        compiler_params=_cp(("arbitrary", "arbitrary")))(a, w, dres, xh, rs, g)


def _loss_ln_bwd(x2, target, xh, rs, g, tm=256):
    t, d = x2.shape

    def body(x_ref, t_ref, xh_ref, rs_ref, g_ref, loss_ref, dz_ref, dzb_ref, dg_ref, db_ref):
        @pl.when(pl.program_id(0) == 0)
        def _():
            loss_ref[...] = jnp.zeros_like(loss_ref)
            dg_ref[...] = jnp.zeros_like(dg_ref)
            db_ref[...] = jnp.zeros_like(db_ref)

        err = x_ref[...] - t_ref[...]
        per_row = jnp.mean(err * err, axis=-1, keepdims=True)
        loss_ref[...] += 0.5 * jnp.sum(per_row, axis=0, keepdims=True)
        dx = err * (1.0 / d)
        xhv = xh_ref[...]
        dz = _ln_bwd_rows(dx, xhv, rs_ref[...], g_ref[...])
        dz_ref[...] = dz
        dzb_ref[...] = dz.astype(BF16)
        dg_ref[...] += jnp.sum(dx * xhv, axis=0, keepdims=True)
        db_ref[...] += jnp.sum(dx, axis=0, keepdims=True)

    row = pl.BlockSpec((tm, d), lambda i: (i, 0))
    vec = pl.BlockSpec((1, d), lambda i: (0, 0))
    return pl.pallas_call(
        body, name="loss_ln_bwd", grid=(t // tm,),
        in_specs=[row, row, row, pl.BlockSpec((tm, 1), lambda i: (i, 0)), vec],
        out_specs=[pl.BlockSpec((1, LANE), lambda i: (0, 0)), row, row, vec, vec],
        out_shape=[_sds((1, LANE), F32), _sds((t, d), F32), _sds((t, d), BF16), _sds((1, d), F32),
                   _sds((1, d), F32)],
        compiler_params=_cp(("arbitrary",)))(x2, target, xh, rs, g)


HPA_FWD = 8
HPA = 4


STRIP = 16


def _attn_scores(q_ref, k_refs, bias_ref, i, dh, hh):
    cols = pl.ds(hh * dh, dh)
    q = q_ref[:, cols] * (dh ** -0.5)
    k = jnp.concatenate([r[:, cols] for r in k_refs], axis=0)
    s = _dot_nt(q, k)
    before_start = lax.broadcasted_iota(jnp.int32, (STRIP, KW), 1) < (2 - i) * QB
    strips = []
    for r in range(0, QB, STRIP):
        ss = jnp.where(before_start, NEG_INF, s[r:r + STRIP] + bias_ref[hh, r:r + STRIP])
        e = jnp.exp(ss - jnp.max(ss, axis=-1, keepdims=True))
        strips.append(e / jnp.sum(e, axis=-1, keepdims=True))
    return q, k, strips


def _attn_specs(dh, off, hp):
    w = hp * dh
    qcol, kcol, vcol = off["aq"] // w, off["ak"] // w, off["av"] // w
    q_spec = pl.BlockSpec((QB, w), lambda g, i: (i, qcol + g))
    k_specs = [pl.BlockSpec((QB, w), functools.partial(lambda g, i, j: (jnp.maximum(i - 2 + j, 0), kcol + g), j=j))
               for j in range(3)]
    v_specs = [pl.BlockSpec((QB, w), functools.partial(lambda g, i, j: (jnp.maximum(i - 2 + j, 0), vcol + g), j=j))
               for j in range(3)]
    bias_spec = pl.BlockSpec((hp, QB, KW), lambda g, i: (g, 0, 0))
    return q_spec, k_specs, v_specs, bias_spec


def _attn_fwd(p, bias, d, off):
    t = p.shape[0]
    dh = d // ATTN_HEADS

    def body(q_ref, k0, k1, k2, v0, v1, v2, bias_ref, o_ref):
        for hh in range(HPA_FWD):
            cols = pl.ds(hh * dh, dh)
            _, _, strips = _attn_scores(q_ref, (k0, k1, k2), bias_ref, pl.program_id(1), dh, hh)
            pr = jnp.concatenate([_mx(ps) for ps in strips], axis=0)
            v = jnp.concatenate([v0[:, cols], v1[:, cols], v2[:, cols]], axis=0)
            o_ref[:, cols] = _dot(pr, v).astype(o_ref.dtype)

    q_spec, k_specs, v_specs, bias_spec = _attn_specs(dh, off, HPA_FWD)
    return pl.pallas_call(
        body, name="attn_fwd", grid=(ATTN_HEADS // HPA_FWD, t // QB),
        in_specs=[q_spec] + k_specs + v_specs + [bias_spec],
        out_specs=pl.BlockSpec((None, QB, HPA_FWD * dh), lambda g, i: (0, i, g)),
        out_shape=_sds((N_BRANCH, t, d), BF16),
        compiler_params=_cp(("parallel", "parallel")))(p, p, p, p, p, p, p, bias)


def _attn_bwd(p, bias, do, dp, d, off):
    t = p.shape[0]
    dh = d // ATTN_HEADS
    tp = t + 2 * QB

    def body(q_ref, k0, k1, k2, v0, v1, v2, bias_ref, do_ref, dp_in, dq_ref, dk_out, dv_out, dbias_ref, dk_ref,
             dv_ref):
        i = pl.program_id(1)

        @pl.when(i == 0)
        def _():
            dk_ref[...] = jnp.zeros_like(dk_ref)
            dv_ref[...] = jnp.zeros_like(dv_ref)
            dbias_ref[...] = jnp.zeros_like(dbias_ref)

        rows = pl.ds(pl.multiple_of(i * QB, QB), KW)
        for hh in range(HPA):
            cols = pl.ds(hh * dh, dh)
            q, k, strips = _attn_scores(q_ref, (k0, k1, k2), bias_ref, i, dh, hh)
            v = jnp.concatenate([v0[:, cols], v1[:, cols], v2[:, cols]], axis=0)
            dov = do_ref[:, cols]
            dp = _dot_nt(dov, v)
            ds_strips = []
            for n, ps in enumerate(strips):
                r = n * STRIP
                dps = dp[r:r + STRIP]
                dss = ps * (dps - jnp.sum(ps * dps, axis=-1, keepdims=True))
                dbias_ref[hh, r:r + STRIP] += dss
                ds_strips.append(_mx(dss))
            ds = jnp.concatenate(ds_strips, axis=0)
            pr = jnp.concatenate([_mx(ps) for ps in strips], axis=0)
            dq_ref[:, cols] = (_dot(ds, k) * (dh ** -0.5)).astype(dq_ref.dtype)
            dk_ref[rows, cols] += _dot_tn(ds, q)
            dv_ref[rows, cols] += _dot_tn(pr, dov)

        @pl.when(i == t // QB - 1)
        def _():
            mine = pl.ds(pl.multiple_of(pl.program_id(0) * w, w), w)
            pltpu.sync_copy(dk_ref, dk_out.at[:, mine])
            pltpu.sync_copy(dv_ref, dv_out.at[:, mine])

    w = HPA * dh
    q_spec, k_specs, v_specs, bias_spec = _attn_specs(dh, off, HPA)
    qcol = off["aq"] // w
    return pl.pallas_call(
        body, name="attn_bwd", grid=(ATTN_HEADS // HPA, t // QB),
        in_specs=[q_spec] + k_specs + v_specs + [bias_spec,
                                                 pl.BlockSpec((None, QB, w), lambda g, i: (0, i, g)), ANY],
        out_specs=[pl.BlockSpec((QB, w), lambda g, i: (i, qcol + g)), ANY, ANY, bias_spec],
        out_shape=[_sds(dp.shape, BF16), _sds((tp, d), F32), _sds((tp, d), F32),
                   _sds((ATTN_HEADS, QB, KW), F32)],
        scratch_shapes=[pltpu.VMEM((tp, w), F32), pltpu.VMEM((tp, w), F32)], input_output_aliases={9: 0},
        compiler_params=_cp(("parallel", "arbitrary")))(p, p, p, p, p, p, p, bias, do, dp)


def _onehot_mm(name, a, b):
    def body(a_ref, b_ref, o_ref):
        o_ref[...] = _dot_hi(a_ref[...], b_ref[...])

    return pl.pallas_call(body, name=name, out_shape=_sds((a.shape[0], b.shape[1]), F32),
                          compiler_params=pltpu.CompilerParams(vmem_limit_bytes=VMEM_LIMIT))(a, b)


def _diag_index():
    ii, jj = np.arange(CHUNK)[:, None], np.arange(CHUNK)[None, :]
    return (ii - jj + CHUNK - 1).reshape(-1)


def _bias_expand(rel_bias):
    h = rel_bias.shape[0]
    nq, nk, shift = QB // CHUNK, KW // CHUNK, (2 * QB) // CHUNK
    nbin, ndc = 3 * LANE, 4
    rb = jnp.pad(rel_bias, ((0, 0), (0, nbin - rel_bias.shape[1])))
    win = np.clip(CHUNK * np.arange(ndc)[:, None] + np.arange(LANE)[None, :] - (CHUNK - 1), -REL_CLIP, REL_CLIP)
    sel = (jnp.arange(nbin)[:, None] == jnp.asarray((win + REL_CLIP).reshape(1, -1))).astype(F32)
    windows = _onehot_mm("bias_windows", rb, sel)
    diag_t = (jnp.arange(LANE)[:, None] == jnp.asarray(_diag_index().reshape(1, -1))).astype(F32)
    blocks = _onehot_mm("bias_blocks", windows.reshape(h * ndc, LANE), diag_t).reshape(h, ndc, CHUNK, CHUNK)
    off_band = jnp.full((h, CHUNK, CHUNK), NEG_INF, F32)
    dcs = [shift + nq - 1 - u for u in range(nk + nq - 1)]
    strip = jnp.concatenate([blocks[:, min(dc, ndc - 1)] if 0 <= dc <= ATTN_LEFT else off_band for dc in dcs], axis=2)
    return jnp.concatenate([strip[:, :, (nq - 1 - ic) * CHUNK:(nq - 1 - ic + nk) * CHUNK] for ic in range(nq)],
                           axis=1)


def _bias_reduce(dbias):
    h = dbias.shape[0]
    nq, nk = QB // CHUNK, KW // CHUNK
    nbin = 3 * LANE
    blocks = dbias.reshape(h, nq, CHUNK, nk, CHUNK).transpose(0, 1, 3, 2, 4).reshape(h * nq * nk, CHUNK * CHUNK)
    diag = (jnp.asarray(_diag_index().reshape(-1, 1)) == jnp.arange(LANE)[None, :]).astype(F32)
    ic = np.arange(nq)[:, None, None]
    jc = np.arange(nk)[None, :, None]
    dl = np.arange(LANE)[None, None, :] - (CHUNK - 1)
    rel = np.clip(CHUNK * (ic - jc + (2 * QB) // CHUNK) + dl, -REL_CLIP, REL_CLIP) + REL_CLIP
    bins = (jnp.asarray(rel.reshape(-1, 1)) == jnp.arange(nbin)[None, :]).astype(F32)

    diags = _onehot_mm("bias_diag_sums", blocks, diag)
    out = _onehot_mm("bias_bin_sums", diags.reshape(h, nq * nk * LANE), bins)
    return out[:, :2 * REL_CLIP + 1]


def _chunk_masks():
    r = lax.broadcasted_iota(jnp.int32, (LB, LB), 0)
    c = lax.broadcasted_iota(jnp.int32, (LB, LB), 1)
    return (r // CHUNK) == (c // CHUNK), r >= c, r <= c


def _chunks(a):
    return [a[c * CHUNK:(c + 1) * CHUNK] for c in range(LB // CHUNK)]


def _per_chunk(a, f):
    return jnp.concatenate([jnp.broadcast_to(f(c), c.shape) for c in _chunks(a)], axis=0)


def _dot_sel(sel, x):
    def top(v):
        return lax.bitcast_convert_type(lax.bitcast_convert_type(v, jnp.int32) & jnp.int32(-65536), F32)

    hi = top(x)
    mid = top(x - hi)
    lo = (x - hi) - mid
    d = functools.partial(jnp.dot, sel.astype(jnp.bfloat16), preferred_element_type=F32)
    return d(hi.astype(jnp.bfloat16)) + d(mid.astype(jnp.bfloat16)) + d(lo.astype(jnp.bfloat16))


def _lin_block(gla, q, k, v, aux):
    dk = q.shape[-1]
    same, low, up = _chunk_masks()
    ones = same.astype(F32)
    if gla:
        glr, wlr, blr = aux
        q = q * (dk ** -0.5)
        pre = _dot(glr, wlr) + blr
        log_a = (jnp.minimum(pre, 0.0) - jnp.log(1.0 + jnp.exp(-jnp.abs(pre)))) / GATE_NORM
        b = _dot_sel(jnp.where(low, ones, 0.0), log_a)
        lastb = _per_chunk(b, lambda c: c[CHUNK - 1:])
    else:
        cs, sn, lg = aux
        pre = None
        half = dk // 2
        q = q * cs + pltpu.roll(q, half, 1) * sn
        k = (k * cs + pltpu.roll(k, half, 1) * sn) * (dk ** -0.5)
        pos = (lax.broadcasted_iota(jnp.int32, (LB, dk), 0) % CHUNK).astype(F32) + 1.0
        b = pos * lg
        lastb = jnp.broadcast_to(float(CHUNK) * lg, b.shape)
    eb, enb, el, dec = jnp.exp(b), jnp.exp(-b), jnp.exp(lastb - b), jnp.exp(lastb)
    qf, kf, qb, kb, kl = q * eb, k * enb, q * enb, k * eb, k * el
    s = jnp.where(same, jnp.where(low, _dot_nt(qf, kf), _dot_nt(qb, kb)), 0.0)
    return dict(pre=pre, eb=eb, enb=enb, el=el, dec=dec, qf=qf, kf=kf, qb=qb, kb=kb, kl=kl, s=s,
                same=same, low=low, up=up, ones=ones)


def _lin_norm_gate(gla, o, gate, gn):
    sg = _sigmoid(gate)
    silu = gate * sg
    if gla:
        r = lax.rsqrt(jnp.mean(o * o, axis=-1, keepdims=True) + LN_EPS)
        hn = o * r
        return silu * (hn * gn), (sg, silu, r, hn)
    mu = jnp.mean(o, axis=-1, keepdims=True)
    oc = o - mu
    r = lax.rsqrt(jnp.mean(oc * oc, axis=-1, keepdims=True) + LN_EPS)
    hn = oc * r
    return silu * hn, (sg, silu, r, hn)


HPS = 4


def _lin_specs(gla, dk, dv, off, rev, nb):
    pre = "g" if gla else "r"
    wk, wv = HPS * dk, HPS * dv
    qc, kc, vc, gc = (off[pre + "q"] // wk, off[pre + "k"] // wk, off[pre + "v"] // wv, off[pre + "g"] // wv)

    def blk(i):
        return nb - 1 - i if rev else i

    specs = [pl.BlockSpec((LB, wk), lambda g, i: (blk(i), qc + g)),
             pl.BlockSpec((LB, wk), lambda g, i: (blk(i), kc + g)),
             pl.BlockSpec((LB, wv), lambda g, i: (blk(i), vc + g)),
             pl.BlockSpec((LB, wv), lambda g, i: (blk(i), gc + g))]
    if gla:
        specs += [pl.BlockSpec((LB, LANE), lambda g, i: (blk(i), off["glr"] // LANE)),
                  pl.BlockSpec((LANE, wk), lambda g, i: (0, g)),
                  pl.BlockSpec((1, wk), lambda g, i: (0, g)),
                  pl.BlockSpec((1, dv), lambda g, i: (0, 0))]
    else:
        specs += [pl.BlockSpec((LB, dk), lambda g, i: (blk(i), 0)),
                  pl.BlockSpec((LB, dk), lambda g, i: (blk(i), 0)),
                  pl.BlockSpec((HPS, 1, dk), lambda g, i: (g, 0, 0))]
    return specs, blk


def _lin_aux(gla, refs, rows, hh, dk):
    if gla:
        glr_ref, wlr_ref, blr_ref, gn_ref = refs
        kcols = pl.ds(hh * dk, dk)
        return (glr_ref[rows, :], wlr_ref[:, kcols], blr_ref[:, kcols]), gn_ref[...]
    cs_ref, sn_ref, lg_ref = refs
    return (cs_ref[rows, :], sn_ref[rows, :], lg_ref[hh]), None


def _lin_fwd(gla, p, aux_arrays, d, off, branches, slot):
    t = p.shape[0]
    dk, dv = d // (2 * LIN_HEADS), d // LIN_HEADS
    nb, cb = t // LB, LB // CHUNK
    naux = len(aux_arrays)

    def body(*refs):
        q_ref, k_ref, v_ref, g_ref = refs[:4]
        aux_refs = refs[4:4 + naux]
        o_ref, bo_ref, st_out_ref, st_ref = refs[5 + naux:]

        @pl.when(pl.program_id(1) == 0)
        def _():
            st_ref[...] = jnp.zeros_like(st_ref)

        rows = slice(None)
        for hh in range(HPS):
            kcols, vcols = pl.ds(hh * dk, dk), pl.ds(hh * dv, dv)
            aux, gn = _lin_aux(gla, aux_refs, rows, hh, dk)
            v = v_ref[:, vcols]
            blk = _lin_block(gla, q_ref[:, kcols], k_ref[:, kcols], v, aux)
            st = st_ref[hh]
            inter = []
            for c, (qf, kl, dec, vc) in enumerate(zip(_chunks(blk["qf"]), _chunks(blk["kl"]), _chunks(blk["dec"]),
                                                      _chunks(v))):
                st_out_ref[hh, c] = st
                inter.append(_dot_nt(qf, st))
                st = st * dec[:1] + _dot_tn(vc, kl)
            st_ref[hh] = st
            o = _dot(blk["s"], v) + jnp.concatenate(inter, axis=0)
            o_ref[:, vcols] = o
            out, _ = _lin_norm_gate(gla, o, g_ref[:, vcols], gn)
            bo_ref[:, vcols] = out.astype(BF16)

    specs, _ = _lin_specs(gla, dk, dv, off, False, nb)
    orow = pl.BlockSpec((LB, HPS * dv), lambda g, i: (i, g))
    return pl.pallas_call(
        body, name="gla_fwd" if gla else "ret_fwd", grid=(LIN_HEADS // HPS, nb), in_specs=specs + [ANY],
        out_specs=[orow, pl.BlockSpec((None, LB, HPS * dv), lambda g, i: (slot, i, g)),
                   pl.BlockSpec((HPS, cb, dv, dk), lambda g, i: (g, i, 0, 0))],
        out_shape=[_sds((t, d), F32), _sds(branches.shape, BF16), _sds((LIN_HEADS, t // CHUNK, dv, dk), F32)],
        scratch_shapes=[pltpu.VMEM((HPS, dv, dk), F32)], input_output_aliases={4 + naux: 1},
        compiler_params=_cp(("parallel", "arbitrary")))(p, p, p, p, *aux_arrays, branches)


def _lin_bwd(gla, p, aux_arrays, o, states, dbo, slot, dp, d, off):
    assert HPS == LIN_HEADS
    t = p.shape[0]
    dk, dv = d // (2 * LIN_HEADS), d // LIN_HEADS
    nb, cb = t // LB, LB // CHUNK
    naux = len(aux_arrays)

    def body(*refs):
        q_ref, k_ref, v_ref, g_ref = refs[:4]
        aux_refs = refs[4:4 + naux]
        o_ref, st_in_ref, dbo_ref = refs[4 + naux:7 + naux]
        outs = refs[8 + naux:]
        dq_ref, dk_ref = outs[0].at[:, pl.ds(0, d // 2)], outs[0].at[:, pl.ds(d // 2, d // 2)]
        dv_ref, dg_ref = outs[0].at[:, pl.ds(d, d)], outs[0].at[:, pl.ds(2 * d, d)]
        dst_ref = outs[-1]
        first = pl.program_id(1) == 0

        @pl.when(first)
        def _():
            dst_ref[...] = jnp.zeros_like(dst_ref)

        if gla:
            dpre_ref, dblr_ref, dgn_ref = outs[1:4]

            @pl.when(first)
            def _():
                dblr_ref[...] = jnp.zeros_like(dblr_ref)
                dgn_ref[...] = jnp.zeros_like(dgn_ref)

        rows = slice(None)
        for hh in range(HPS):
            kcols, vcols = pl.ds(hh * dk, dk), pl.ds(hh * dv, dv)
            aux, gn = _lin_aux(gla, aux_refs, rows, hh, dk)
            v = v_ref[:, vcols]
            bk = _lin_block(gla, q_ref[:, kcols], k_ref[:, kcols], v, aux)
            eb, enb, el, dec = bk["eb"], bk["enb"], bk["el"], bk["dec"]
            qf, kf, qb, kb, kl, s = bk["qf"], bk["kf"], bk["qb"], bk["kb"], bk["kl"], bk["s"]
            gate = g_ref[:, vcols]
            dout = dbo_ref[:, vcols]
            _, (sg, silu, r, hn) = _lin_norm_gate(gla, o_ref[:, vcols], gate, gn)
            dsilu = sg * (1.0 + gate * (1.0 - sg))
            if gla:
                y = hn * gn
                dy = dout * silu
                dg_ref[:, vcols] = (dout * y * dsilu).astype(BF16)
                dgn_ref[hh] += jnp.sum(dy * hn, axis=0, keepdims=True)
                dhn = dy * gn
                do = r * (dhn - hn * jnp.mean(dhn * hn, axis=-1, keepdims=True))
            else:
                dhn = dout * silu
                dg_ref[:, vcols] = (dout * hn * dsilu).astype(BF16)
                do = r * (dhn - jnp.mean(dhn, axis=-1, keepdims=True)
                          - hn * jnp.mean(dhn * hn, axis=-1, keepdims=True))
            ds = jnp.where(bk["same"], _dot_nt(do, v), 0.0)
            dsf = jnp.where(bk["low"], ds, 0.0)
            dsb = ds - dsf
            dvv = _dot_tn(s, do)
            dqf = _dot(dsf, kf)
            dkf = _dot_tn(dsf, qf)
            dqb = _dot(dsb, kb)
            dkb = _dot_tn(dsb, qb)
            dst = dst_ref[hh]
            dv_st, dqf_st, dkl_c, ddec_c = [], [], [], []
            parts = zip(reversed(range(cb)), reversed(_chunks(do)), reversed(_chunks(v)), reversed(_chunks(qf)),
                        reversed(_chunks(kl)), reversed(_chunks(dec)))
            for c, do_c, v_c, qf_c, kl_c, dec_c in parts:
                st = st_in_ref[hh, c]
                dv_st.append(_dot_nt(kl_c, dst))
                dkl_c.append(_dot(v_c, dst))
                dqf_st.append(_dot(do_c, st))
                ddec_c.append(jnp.broadcast_to(jnp.sum(dst * st, axis=0, keepdims=True), (CHUNK, dk)))
                dst = dst * dec_c[:1] + _dot_tn(do_c, qf_c)
            dst_ref[hh] = dst

            def cat(pieces):
                return jnp.concatenate(pieces[::-1], axis=0)

            dvv = dvv + cat(dv_st)
            dqf = dqf + cat(dqf_st)
            dkl = cat(dkl_c)
            dq = dqf * eb + dqb * enb
            dkk = dkf * enb + dkb * eb + dkl * el
            dv_ref[:, vcols] = dvv.astype(BF16)
            if gla:
                db = dqf * qf - dkf * kf - dqb * qb + dkb * kb - dkl * kl
                dlast = _per_chunk(dkl * kl, lambda c: jnp.sum(c, axis=0, keepdims=True)) + cat(ddec_c) * dec
                dlog_a = _dot_sel(jnp.where(bk["up"], bk["ones"], 0.0), db) + dlast
                dpre = dlog_a * (1.0 / GATE_NORM) * (1.0 - _sigmoid(bk["pre"]))
                dpre_ref[:, kcols] = dpre
                dblr_ref[hh] += jnp.sum(dpre, axis=0, keepdims=True)
                dq_ref[:, kcols] = (dq * (dk ** -0.5)).astype(BF16)
                dk_ref[:, kcols] = dkk.astype(BF16)
            else:
                cs, sn, _ = aux
                half = dk // 2
                dkk = dkk * (dk ** -0.5)
                dq_ref[:, kcols] = (dq * cs + pltpu.roll(dq * sn, half, 1)).astype(BF16)
                dk_ref[:, kcols] = (dkk * cs + pltpu.roll(dkk * sn, half, 1)).astype(BF16)

    specs, blk = _lin_specs(gla, dk, dv, off, True, nb)
    vrow = pl.BlockSpec((LB, HPS * dv), lambda g, i: (blk(i), g))
    krow = pl.BlockSpec((LB, HPS * dk), lambda g, i: (blk(i), g))
    specs += [vrow, pl.BlockSpec((HPS, cb, dv, dk), lambda g, i: (g, blk(i), 0, 0)),
              pl.BlockSpec((None, LB, HPS * dv), lambda g, i: (slot, blk(i), g)), ANY]
    section = off[("g" if gla else "r") + "q"] // (3 * d)
    out_specs = [pl.BlockSpec((LB, 3 * d), lambda g, i: (blk(i), section))]
    out_shape = [_sds(dp.shape, BF16)]
    if gla:
        out_specs += [krow, pl.BlockSpec((HPS, 1, dk), lambda g, i: (g, 0, 0)),
                      pl.BlockSpec((HPS, 1, dv), lambda g, i: (g, 0, 0))]
        out_shape += [_sds((t, d // 2), F32), _sds((LIN_HEADS, 1, dk), F32), _sds((LIN_HEADS, 1, dv), F32)]
    out_specs.append(pl.BlockSpec((HPS, dv, dk), lambda g, i: (g, 0, 0)))
    out_shape.append(_sds((LIN_HEADS, dv, dk), F32))
    res = pl.pallas_call(
        body, name="gla_bwd" if gla else "ret_bwd", grid=(LIN_HEADS // HPS, nb), in_specs=specs,
        out_specs=out_specs, out_shape=out_shape, input_output_aliases={7 + naux: 0},
        compiler_params=_cp(("parallel", "arbitrary")))(p, p, p, p, *aux_arrays, o, states, dbo, dp)
    return res[:-1]


def _row_tile(rows, cols):
    cap = max(8, (2 << 20) // (4 * cols))
    t = rows
    while t > cap and t % 2 == 0:
        t //= 2
    return t


def _add_half(name, g, t, sel):
    nchip, hr, cols = t.shape
    tr = _row_tile(hr, cols)
    nb = hr // tr

    def body(sel_ref, g_ref, t_ref, o_ref):
        o_ref[...] = g_ref[...] + t_ref[...]

    half = pl.BlockSpec((None, tr, cols), lambda p, i, s: (p, i, 0))
    gs = pltpu.PrefetchScalarGridSpec(
        num_scalar_prefetch=1, grid=(nchip, nb),
        in_specs=[pl.BlockSpec((None, tr, cols), lambda p, i, s: (p, s[0] * nb + i, 0)), half], out_specs=half)
    return pl.pallas_call(body, name=name, grid_spec=gs, out_shape=_sds(t.shape, F32),
                          compiler_params=_cp(("parallel", "parallel")))(sel, g, t)


def _sum_shards(name, h, rcv, sel):
    _, rows, cols = h.shape
    tr = _row_tile(rows, cols)

    def body(sel_ref, h_ref, r0, r1, r2, o_ref):
        o_ref[...] = ((h_ref[...] + r0[...]) + r1[...]) + r2[...]

    rspecs = [pl.BlockSpec((None, tr, cols), functools.partial(lambda i, s, j: (j, i, 0), j=j)) for j in range(3)]
    gs = pltpu.PrefetchScalarGridSpec(
        num_scalar_prefetch=1, grid=(rows // tr,),
        in_specs=[pl.BlockSpec((None, tr, cols), lambda i, s: (s[0], i, 0))] + rspecs,
        out_specs=pl.BlockSpec((tr, cols), lambda i, s: (i, 0)))
    return pl.pallas_call(body, name=name, grid_spec=gs, out_shape=_sds((rows, cols), F32),
                          compiler_params=_cp(("parallel",)))(sel, h, rcv, rcv, rcv)


def _adamw_math(w, g, m, v):
    c1 = 1.0 - ADAM_B1 ** ADAM_STEP
    c2 = 1.0 - ADAM_B2 ** ADAM_STEP
    nm = ADAM_B1 * m + (1.0 - ADAM_B1) * g
    nv = ADAM_B2 * v + (1.0 - ADAM_B2) * jnp.square(g)
    return -ADAM_LR * ((nm / c1) / (jnp.sqrt(nv / c2) + ADAM_EPS) + ADAM_WD * w), nm, nv


def _adamw(name, w, g, m, v):
    rows, cols = w.shape
    tr = _row_tile(rows, cols)

    def body(w_ref, g_ref, m_ref, v_ref, d_ref, nm_ref, nv_ref):
        d_ref[...], nm_ref[...], nv_ref[...] = _adamw_math(w_ref[...], g_ref[...], m_ref[...], v_ref[...])

    spec = pl.BlockSpec((tr, cols), lambda i: (i, 0))
    return pl.pallas_call(body, name=name, grid=(rows // tr,), in_specs=[spec] * 4, out_specs=[spec] * 3,
                          out_shape=[_sds((rows, cols), F32)] * 3, compiler_params=_cp(("parallel",)))(w, g, m, v)


def _adamw_layer(name, w, g_own, g_sib, sel, m, v, layer, prev):
    depth, rows, cols = w.shape
    tr = _row_tile(rows // 2, cols)
    nbh = rows // 2 // tr
    nprev = 0 if prev is None else 4

    def body(sel_ref, w_ref, own_ref, sib_ref, m_ref, v_ref, *rest):
        go_ref, d_ref, nm_ref, nv_ref = rest[nprev:]
        gv = jnp.where(pl.program_id(0) // nbh == sel_ref[0], own_ref[...], sib_ref[...])
        go_ref[...] = gv
        d_ref[...], nm_ref[...], nv_ref[...] = _adamw_math(w_ref[...], gv, m_ref[...], v_ref[...])

    lay = pl.BlockSpec((None, tr, cols), lambda i, s: (layer, i, 0))
    hlf = pl.BlockSpec((tr, cols), lambda i, s: (i % nbh, 0))
    gs = pltpu.PrefetchScalarGridSpec(
        num_scalar_prefetch=1, grid=(2 * nbh,), in_specs=[lay, hlf, hlf, lay, lay] + [ANY] * nprev,
        out_specs=[lay] * 4)
    args = (sel, w, g_own, g_sib, m, v) + (() if prev is None else tuple(prev))
    return pl.pallas_call(
        body, name=name, grid_spec=gs, out_shape=[_sds((depth, rows, cols), F32)] * 4,
        input_output_aliases={6 + k: k for k in range(nprev)},
        compiler_params=_cp(("parallel",)))(*args)


def _adamw_colmajor(name, wt, mt, vt, halves, sel):
    c_dim, depth, r_dim = wt.shape
    hr = r_dim // 2

    def body(sel_ref, w_ref, m_ref, v_ref, *rest):
        g_refs, (go_ref, d_ref, nm_ref, nv_ref) = rest[:2 * depth], rest[2 * depth:]
        own_first = sel_ref[0] == 0
        for l in range(depth):
            own, sib = g_refs[2 * l][...], g_refs[2 * l + 1][...]
            g = jnp.concatenate([jnp.where(own_first, own, sib), jnp.where(own_first, sib, own)], axis=0).T
            go_ref[:, l, :] = g
            d_ref[:, l, :], nm_ref[:, l, :], nv_ref[:, l, :] = _adamw_math(w_ref[:, l, :], g, m_ref[:, l, :],
                                                                          v_ref[:, l, :])

    col = pl.BlockSpec((LANE, depth, r_dim), lambda j, s: (j, 0, 0))
    gs = pltpu.PrefetchScalarGridSpec(
        num_scalar_prefetch=1, grid=(c_dim // LANE,),
        in_specs=[col] * 3 + [pl.BlockSpec((hr, LANE), lambda j, s: (0, j))] * (2 * depth), out_specs=[col] * 4)
    flat = [h for pair in halves for h in pair]
    return pl.pallas_call(body, name=name, grid_spec=gs, out_shape=[_sds(wt.shape, F32)] * 4,
                          compiler_params=_cp(("parallel",)))(sel, wt, mt, vt, *flat)


def _adamw_tail(name, wt, mt, vt, gt_tail, prev):
    c_dim, depth, r_dim = wt.shape
    nt = gt_tail.shape[0]

    def body(w_ref, m_ref, v_ref, g_ref, *rest):
        go_ref, d_ref, nm_ref, nv_ref = rest[4:]
        g = g_ref[...]
        go_ref[...] = g
        d_ref[...], nm_ref[...], nv_ref[...] = _adamw_math(w_ref[...], g, m_ref[...], v_ref[...])

    tail = pl.BlockSpec((nt, depth, r_dim), lambda i: (c_dim // nt - 1, 0, 0))
    return pl.pallas_call(
        body, name=name, grid=(1,), in_specs=[tail] * 3 + [pl.BlockSpec((nt, depth, r_dim), lambda i: (0, 0, 0))]
        + [ANY] * 4, out_specs=[tail] * 4, out_shape=[_sds(wt.shape, F32)] * 4,
        input_output_aliases={4 + k: k for k in range(4)},
        compiler_params=_cp(("arbitrary",)))(wt, mt, vt, gt_tail, *prev)


def _place():
    x, y, c = (lax.axis_index(a) for a in MESH_AXES)
    chips = [(1 - x, y), (x, 1 - y), (1 - x, 1 - y)]
    return x, y, c, chips


def _chip_index(xy):
    return 2 * xy[0] + xy[1]


ANY = pl.BlockSpec(memory_space=pl.ANY)


HBM_SPEC = pl.BlockSpec(memory_space=pltpu.HBM)
SEM = pl.BlockSpec(memory_space=pltpu.SEMAPHORE)
EFFECT = pltpu.SideEffectType.DATAFLOW_SIDE_EFFECTING


def _half(ref, c):
    hr = ref.shape[-2] // 2
    return pl.ds(pl.multiple_of(c * hr, 16), hr)


def _gather_copies(srcs, lands, send, recv):
    x, y, c, chips = _place()
    me = _chip_index((x, y))
    return [pltpu.make_async_remote_copy(src_ref=s.at[_half(s, c)], dst_ref=g.at[me, _half(s, c)],
                                         send_sem=send.at[3 * a + j], recv_sem=recv.at[3 * a + j],
                                         device_id=(*ch, c), device_id_type=DEV)
            for a, (s, g) in enumerate(zip(srcs, lands)) for j, ch in enumerate(chips)]


def _scatter_copies(srcs, lands, send, recv):
    x, y, c, chips = _place()
    return [pltpu.make_async_remote_copy(src_ref=h.at[_chip_index(ch)], dst_ref=r.at[j],
                                         send_sem=send.at[3 * a + j], recv_sem=recv.at[3 * a + j],
                                         device_id=(*ch, c), device_id_type=DEV)
            for a, (h, r) in enumerate(zip(srcs, lands)) for j, ch in enumerate(chips)]


def _in_hbm(a):
    return pltpu.with_memory_space_constraint(a, pltpu.HBM)


def _split_start(name, srcs, land_shapes, copies_fn, after=None, per_src=3):
    ns, nl = len(srcs), len(land_shapes)
    ncp = per_src * ns
    lands = [lax.empty(s.shape, s.dtype) for s in land_shapes]
    behind = [] if after is None else [after]

    def body(*refs):
        src, land = refs[:ns], refs[ns:ns + nl]
        send, recv = refs[ns + nl + len(behind)], refs[ns + nl + len(behind) + 1]
        for cp in copies_fn(src, land, send, recv):
            cp.start()
        refs[-1][...] = jnp.zeros_like(refs[-1])

    bufs = list(srcs) + lands
    outs = pl.pallas_call(
        body, name=name, in_specs=[HBM_SPEC] * (ns + nl) + [ANY] * len(behind),
        out_specs=[SEM, SEM] + [HBM_SPEC] * (ns + nl) + [pl.BlockSpec(memory_space=pltpu.VMEM)],
        out_shape=[pltpu.SemaphoreType.DMA((ncp,)), pltpu.SemaphoreType.DMA((ncp,))]
        + [pltpu.HBM(b.shape, b.dtype) for b in bufs] + [_sds((8, LANE), F32)],
        input_output_aliases={i: 2 + i for i in range(ns + nl)},
        compiler_params=pltpu.CompilerParams(has_side_effects=EFFECT))(*[_in_hbm(b) for b in bufs], *behind)
    return outs[0], outs[1], list(outs[2:2 + ns]), list(outs[2 + ns:2 + ns + nl]), outs[-1]


def _split_wait(name, started, copies_fn, after):
    send, recv, srcs, lands, _ = started
    ns, nl = len(srcs), len(lands)

    def body(*refs):
        src, land = refs[:ns], refs[ns:ns + nl]
        for cp in copies_fn(src, land, refs[ns + nl], refs[ns + nl + 1]):
            cp.wait_send()
            cp.wait_recv()

    bufs = list(srcs) + list(lands)
    outs = pl.pallas_call(
        body, name=name, in_specs=[HBM_SPEC] * (ns + nl) + [SEM, SEM, ANY], out_specs=[HBM_SPEC] * (ns + nl),
        out_shape=[pltpu.HBM(b.shape, b.dtype) for b in bufs],
        input_output_aliases={i: i for i in range(ns + nl)},
        compiler_params=pltpu.CompilerParams(has_side_effects=EFFECT))(*bufs, send, recv, after)
    return list(outs[:ns]), list(outs[ns:])


def _gather_plain(name, srcs):
    n = len(srcs)

    def body(*refs):
        src, land = refs[:n], refs[n:2 * n]
        send, recv, fsend, frecv = refs[2 * n:]
        first = _gather_copies(src, land, send, recv)
        for cp in first:
            cp.start()
        _forward_body(land, first, fsend, frecv)

    return pl.pallas_call(
        body, name=name, in_specs=[ANY] * n, out_specs=[ANY] * n,
        out_shape=[_sds((4,) + s.shape, s.dtype) for s in srcs],
        scratch_shapes=[pltpu.SemaphoreType.DMA((3 * n,))] * 4)(*srcs)


def _forward_body(land, arrivals, fsend, frecv):
    x, y, c, chips = _place()
    n = len(land)
    passed = []
    for a in range(n):
        for j, ch in enumerate(chips):
            if arrivals is not None:
                arrivals[3 * a + j].wait_recv()
            slot = land[a].at[_chip_index(ch), _half(land[a], c)]
            fw = pltpu.make_async_remote_copy(src_ref=slot, dst_ref=slot, send_sem=fsend.at[3 * a + j],
                                              recv_sem=frecv.at[3 * a + j], device_id=(x, y, 1 - c),
                                              device_id_type=DEV)
            fw.start()
            passed.append(fw)
    for a in range(n):
        for j, ch in enumerate(chips):
            slot = land[a].at[_chip_index(ch), _half(land[a], 1 - c)]
            pltpu.make_async_remote_copy(src_ref=slot, dst_ref=slot, send_sem=fsend.at[3 * a + j],
                                         recv_sem=frecv.at[3 * a + j], device_id=(x, y, c),
                                         device_id_type=DEV).wait_recv()
    for cp in passed:
        cp.wait_send()
    if arrivals is not None:
        for cp in arrivals:
            cp.wait_send()


def _gather_forward(name, lands):
    n = len(lands)

    def body(*refs):
        _forward_body(refs[n:2 * n], None, refs[2 * n], refs[2 * n + 1])

    return pl.pallas_call(
        body, name=name, in_specs=[ANY] * n, out_specs=[ANY] * n,
        out_shape=[_sds(g.shape, g.dtype) for g in lands], input_output_aliases={a: a for a in range(n)},
        scratch_shapes=[pltpu.SemaphoreType.DMA((3 * n,))] * 2)(*lands)


def _sibling_copies(srcs, lands, send, recv):
    x, y, c, _ = _place()
    return [pltpu.make_async_remote_copy(src_ref=g.at[:, _half(g, 1 - c)], dst_ref=t, send_sem=send.at[a],
                                         recv_sem=recv.at[a], device_id=(x, y, 1 - c), device_id_type=DEV)
            for a, (g, t) in enumerate(zip(srcs, lands))]


def _sibling_share(name, sms):
    n = len(sms)

    def body(*refs):
        ins, outs = refs[:n], refs[n:2 * n]
        send, recv = refs[2 * n:]
        x, y, c, _ = _place()
        cps = [pltpu.make_async_remote_copy(src_ref=ins[a], dst_ref=outs[a], send_sem=send.at[a],
                                            recv_sem=recv.at[a], device_id=(x, y, 1 - c), device_id_type=DEV)
               for a in range(n)]
        for cp in cps:
            cp.start()
        for cp in cps:
            cp.wait()

    return pl.pallas_call(
        body, name=name, in_specs=[ANY] * n, out_specs=[ANY] * n, out_shape=[_sds(s.shape, F32) for s in sms],
        scratch_shapes=[pltpu.SemaphoreType.DMA((n,))] * 2)(*sms)


def _small_allreduce(v, after=None):
    rows = v.shape[0]
    ndev = 8
    behind = [] if after is None else [after]

    def body(v_ref, *rest):
        o_ref, gat_ref, send, recv = rest[len(behind):]
        x, y, c, _ = _place()
        me = 4 * x + 2 * y + c
        cps = []
        for k in range(1, ndev):
            to = (me + k) % ndev
            cp = pltpu.make_async_remote_copy(src_ref=v_ref, dst_ref=gat_ref.at[me], send_sem=send.at[k - 1],
                                              recv_sem=recv.at[me], device_id=(to // 4, (to // 2) % 2, to % 2),
                                              device_id_type=DEV)
            cp.start()
            cps.append(cp)
        gat_ref[me] = v_ref[...]
        for k in range(1, ndev):
            frm = (me + k) % ndev
            pltpu.make_async_remote_copy(src_ref=v_ref, dst_ref=gat_ref.at[frm], send_sem=send.at[k - 1],
                                         recv_sem=recv.at[frm], device_id=(x, y, c), device_id_type=DEV).wait_recv()
        for cp in cps:
            cp.wait_send()
        acc = gat_ref[0]
        for k in range(1, ndev):
            acc = acc + gat_ref[k]
        o_ref[...] = acc

    vm = pl.BlockSpec(memory_space=pltpu.VMEM)
    return pl.pallas_call(
        body, name="small_allreduce", in_specs=[vm] + [ANY] * len(behind), out_specs=vm,
        out_shape=_sds((rows, LANE), F32),
        scratch_shapes=[pltpu.VMEM((ndev, rows, LANE), F32), pltpu.SemaphoreType.DMA((ndev - 1,)),
                        pltpu.SemaphoreType.DMA((ndev,))])(v, *behind)


def _layout(d):
    half = d // 2
    names = [("aq", d), ("ak", d), ("av", d), ("rq", half), ("rk", half), ("rv", d), ("rg", d),
             ("gq", half), ("gk", half), ("gv", d), ("gg", d), ("gates", 3 * d), ("glr", 2 * LANE)]
    off, pos = {}, 0
    for nm, sz in names:
        off[nm] = pos
        pos += sz
    return off, pos


def _chip_shards_of_cols(g, d):
    a = 8 * d + d
    per = (a + GATE_RANK + 3 * d) // 4
    sections = ((0, a, 0), (a, a + GATE_RANK, 3 * d), (a + GATE_RANK, 4 * per, -GATE_RANK))

    def quarter(lo, hi):
        cuts = [(max(lo, x), min(hi, y), s) for x, y, s in sections]
        return jnp.concatenate([g[:, x + s:y + s] for x, y, s in cuts if x < y], axis=1)

    return jnp.stack([quarter(c * per, (c + 1) * per) for c in range(4)])


def kernel(x, ln_in_g, ln_in_b, w_in, rel_bias, gla_w_lr, gla_b_lr, gla_norm_g, w_branch, w_out, ln1_g, ln1_b, w_up, w_down, ln2_g, ln2_b, loss_target, m_ln_in_g, m_ln_in_b, m_w_in, m_rel_bias, m_gla_w_lr, m_gla_b_lr, m_gla_norm_g, m_w_branch, m_w_out, m_ln1_g, m_ln1_b, m_w_up, m_w_down, m_ln2_g, m_ln2_b, v_ln_in_g, v_ln_in_b, v_w_in, v_rel_bias, v_gla_w_lr, v_gla_b_lr, v_gla_norm_g, v_w_branch, v_w_out, v_ln1_g, v_ln1_b, v_w_up, v_w_down, v_ln2_g, v_ln2_b):
    t, d = x.shape[1], x.shape[2]
    dff = 4 * d
    half = d // 2
    off, npad = _layout(d)
    xi, yi, ci = (lax.axis_index(a) for a in MESH_AXES)
    chip = 2 * xi + yi
    csel = jnp.reshape(ci, (1,)).astype(jnp.int32)
    psel = jnp.reshape(chip, (1,)).astype(jnp.int32)

    big_w = [w_in, w_branch.reshape(DEPTH, -1, d), w_out, w_up, w_down]
    big_m = [m_w_in, m_w_branch.reshape(DEPTH, -1, d), m_w_out, m_w_up, m_w_down]
    big_v = [v_w_in, v_w_branch.reshape(DEPTH, -1, d), v_w_out, v_w_up, v_w_down]
    W_IN, REST = [0], [1, 2, 3, 4]

    def shards_of(l, idx):
        return [big_w[i][l].astype(BF16) for i in idx]

    def lands_of(srcs):
        return [_sds((4,) + s.shape, s.dtype) for s in srcs]

    def full_w_in(g):
        per = g.shape[2]
        a = 8 * d + d

        def run(lo, hi):
            cuts = [(max(lo, c * per), min(hi, (c + 1) * per), c) for c in range(4)]
            return [g[c, :, x - c * per:y - c * per] for x, y, c in cuts if x < y]

        zeros = jnp.zeros((d, 2 * LANE - GATE_RANK), g.dtype)
        return jnp.concatenate(run(0, a) + run(a + GATE_RANK, 4 * per) + run(a, a + GATE_RANK) + [zeros], axis=1)

    def full_rest(gs):
        g_br, g_out, g_up, g_down = gs
        return (jnp.transpose(g_br.reshape(4, N_BRANCH, d // 4, d), (1, 0, 2, 3)).reshape(N_BRANCH, d, d),
                g_out.reshape(d, d), jnp.transpose(g_up, (1, 0, 2)).reshape(d, dff), g_down.reshape(dff, d))

    def with_own(srcs, lands):
        return [lax.dynamic_update_slice(g, s[None], (chip, 0, 0)) for s, g in zip(srcs, lands)]

    def gather_start(tag, l, idx, after):
        srcs = shards_of(l, idx)
        return srcs, _split_start(f"gather_{tag}{l}_start", srcs, lands_of(srcs), _gather_copies, after)

    def gather_finish(tag, l, pending, after):
        srcs, started = pending
        _, lands = _split_wait(f"gather_{tag}{l}_wait", started, _gather_copies, after)
        return with_own(srcs, _gather_forward(f"gather_{tag}{l}_pass", lands))

    def token(pending):
        return pending[1][4][0, 0]

    win, wbr, wout, wup, wdown = ([None] * DEPTH for _ in range(5))
    src_first = shards_of(0, W_IN)
    g_first = with_own(src_first, _gather_plain("gather_in0", src_first))
    win[0] = full_w_in(g_first[0])

    dkh = half // LIN_HEADS
    lr_rows = DEPTH * GATE_RANK
    lr_slab = jnp.zeros((lr_rows, 4, half // 4), F32)
    lr_slab = lax.dynamic_update_slice(lr_slab, (gla_w_lr.reshape(lr_rows, 1, half // 4) * jnp.where(ci == 0, 1.0, 0.0)),
                                       (0, chip, 0))
    wlr_full = _small_allreduce(lr_slab.reshape(-1, LANE)).reshape(DEPTH, GATE_RANK, half)
    wlr_pad = jnp.concatenate([wlr_full, jnp.zeros((DEPTH, 2 * LANE - GATE_RANK, half), F32)], axis=1)
    pend_rest = gather_start("rest", 0, REST, wlr_full[0, :1, :1] + g_first[0][0, :1, :1].astype(F32))

    inv = 10000.0 ** (-jnp.arange(0, dkh, 2, dtype=F32) / dkh)
    ang = jnp.arange(t, dtype=F32)[:, None] * inv[None, :]
    cos, sin = jnp.cos(ang), jnp.sin(ang)
    rope_c = jnp.concatenate([cos, cos], axis=1)
    rope_s = jnp.concatenate([-sin, sin], axis=1)
    log_gamma = jnp.log1p(-jnp.exp2(-5.0 - jnp.arange(LIN_HEADS, dtype=F32)))
    lg_tab = jnp.broadcast_to(log_gamma[:, None, None], (LIN_HEADS, 1, dkh))

    def vec(a):
        return a.reshape(1, -1)

    x0, x0b, xh_in, rs_in = _ln_in(x[0], vec(ln_in_g) + token(pend_rest), vec(ln_in_b))
    saved = []
    xl, xlb = x0, x0b
    for l in range(DEPTH):
        p = _mm("proj_in", xlb, win[l], 512, 1792)
        g_rest = gather_finish("rest", l, pend_rest, p)
        wbr[l], wout[l], wup[l], wdown[l] = full_rest(g_rest)
        tok = 0.0
        if l + 1 < DEPTH:
            pend_in = gather_start("in", l + 1, W_IN, g_rest[0])
            tok = token(pend_in)
        bias = _bias_expand(rel_bias[l] + tok)
        bo = _attn_fwd(p, bias, d, off)
        ret_aux = (rope_c, rope_s, lg_tab + tok)
        gla_aux = (p, wlr_pad[l], vec(gla_b_lr[l]) + tok, vec(gla_norm_g[l]))
        o_ret, bo, st_ret = _lin_fwd(False, p, ret_aux, d, off, bo, 1)
        o_gla, bo, st_gla = _lin_fwd(True, p, gla_aux, d, off, bo, 2)
        tok = 0.0
        if l + 1 < DEPTH:
            g_in = gather_finish("in", l + 1, pend_in, bo)
            win[l + 1] = full_w_in(g_in[0])
            pend_rest = gather_start("rest", l + 1, REST, g_in[0])
            tok = token(pend_rest)
        proj, merged = _merge_fwd(bo, wbr[l], p, off["gates"])
        x1, x1b, xh1, rs1 = _mm_res_ln("out_proj_ln", merged, wout[l], xl, vec(ln1_g[l]) + tok, vec(ln1_b[l]),
                                       256, False)
        u = _mm("mlp_up", x1b, wup[l], 1024, 1024)
        x2, x2b, xh2, rs2, act = _mm_res_ln("mlp_down_ln", u, wdown[l], x1, vec(ln2_g[l]), vec(ln2_b[l]), 256, True)
        saved.append(dict(xlb=xlb, p=p, bias=bias, ret_aux=ret_aux, gla_aux=gla_aux, o_ret=o_ret, o_gla=o_gla,
                          st_ret=st_ret, st_gla=st_gla, bo=bo, proj=proj, merged=merged, x1b=x1b, xh1=xh1,
                          rs1=rs1, u=u, xh2=xh2, rs2=rs2, act=act))
        xl, xlb = x2, x2b

    small = {}
    last = saved[-1]
    loss_p, dz2, dz2b, dg, db = _loss_ln_bwd(xl, loss_target[0], last["xh2"], last["rs2"], vec(ln2_g[DEPTH - 1]))
    small["loss"] = loss_p[:, :1]
    grad_x = None

    def sibling_start(tag, l, idx, shards):
        lands = [_sds((g.shape[0], g.shape[1] // 2, g.shape[2]), F32) for g in shards]
        return tag, l, idx, _split_start(f"grad_{tag}{l}_sibling_start", shards, lands, _sibling_copies, per_src=1)

    def scatter_start(sibling, after):
        tag, l, idx, started = sibling
        shards, theirs = _split_wait(f"grad_{tag}{l}_sibling_wait", started, _sibling_copies, after)
        hs = [_add_half("grad_sibling_add", g, th, csel) for g, th in zip(shards, theirs)]
        lands = [_sds((3,) + h.shape[1:], F32) for h in hs]
        return tag, l, idx, _split_start(f"grad_{tag}{l}_scatter_start", hs, lands, _scatter_copies)

    adam_out = [None] * len(big_w)
    w_in_halves = [None] * DEPTH

    def scatter_finish(pending, after):
        tag, l, idx, started = pending
        hs, rcv = _split_wait(f"grad_{tag}{l}_scatter_wait", started, _scatter_copies, after)
        sms = [_sum_shards("grad_chip_sum", h, r, psel) for h, r in zip(hs, rcv)]
        last = None
        for i, own, sib in zip(idx, sms, _sibling_share(f"grad_{tag}{l}_share", sms)):
            if i == W_IN[0]:
                w_in_halves[l] = (own, sib)
                last = sib
            else:
                adam_out[i] = _adamw_layer("adamw_large", big_w[i], own, sib, csel, big_m[i], big_v[i], l,
                                           adam_out[i])
                last = adam_out[i][0]
        return last

    in_flight = []

    def scatter(sibling, after):
        pending = scatter_start(sibling, after)
        in_flight.append(pending)
        if len(in_flight) > 3:
            scatter_finish(in_flight.pop(0), pending[3][4])
        return pending[3][4][0, 0]

    def token_of(sibling):
        return sibling[3][4][0, 0]

    carry_tok = 0.0
    for l in reversed(range(DEPTH)):
        s = saved[l]
        small[("ln2_g", l)], small[("ln2_b", l)] = dg, db
        du = _mm_nt_relu2_bwd(dz2b, wdown[l], s["u"])
        g_wdown = _mm_tn("grad_w_down", s["act"], dz2b, 512, 512)
        g_wup = _mm_tn("grad_w_up", s["x1b"], du, 512, 512, shard="cols")
        dz1, dz1b, dg1, db1 = _mm_nt_res_lnbwd("mlp_up_bwd_ln", du, wup[l], dz2, s["xh1"], s["rs1"],
                                               vec(ln1_g[l]) + carry_tok, 256, dff)
        small[("ln1_g", l)], small[("ln1_b", l)] = dg1, db1
        dproj, dp = _merge_bwd(dz1b, wout[l], s["proj"], s["p"], off["gates"], npad)
        g_wout = _mm_tn("grad_w_out", s["merged"], dz1b, 512, 512)
        dbo = _mm("branch_proj_bwd", dproj, wbr[l], 1024, 1024, nt=True)
        g_wbr = _mm_tn("grad_w_branch", s["bo"], dproj, d // 4, 1024, shard="rows")
        sib = sibling_start("rest", l, REST, [g_wbr, g_wout.reshape(4, d // 4, d), g_wup, g_wdown.reshape(4, d, d)])
        rc, rs_, lg = s["ret_aux"]
        gp, gw, gb, gn_ = s["gla_aux"]
        dp, dk_acc, dv_acc, dbias = _attn_bwd(s["p"], s["bias"] + token_of(sib), dbo, dp, d, off)
        tok = scatter(sib, dbias)
        small[("rel_bias", l)] = _bias_reduce(dbias)
        dp = lax.dynamic_update_slice(dp, dk_acc[2 * QB:].astype(BF16), (0, off["ak"]))
        dp = lax.dynamic_update_slice(dp, dv_acc[2 * QB:].astype(BF16), (0, off["av"]))
        (dp,) = _lin_bwd(False, s["p"], (rc, rs_, lg + tok), s["o_ret"], s["st_ret"], dbo, 1, dp, d, off)
        dp, dpre, dblr, dgn = _lin_bwd(True, s["p"], (gp, gw, gb + tok, gn_), s["o_gla"], s["st_gla"], dbo, 2, dp,
                                       d, off)
        small[("gla_b_lr", l)] = dblr.reshape(1, half)
        small[("gla_norm_g", l)] = jnp.sum(dgn, axis=0)
        dpre_b = dpre.astype(BF16)
        glr_b = s["p"][:, off["glr"]:off["glr"] + LANE].astype(BF16)
        dp = _gate_lr_bwd(dpre_b, wlr_pad[l], dp, off["glr"])
        small[("gla_w_lr", l)] = _mm_tn("grad_gla_w_lr", glr_b, dpre_b, LANE, half)[:GATE_RANK]
        if l > 0:
            prev = saved[l - 1]
            xh_p, rs_p, g_p = prev["xh2"], prev["rs2"], vec(ln2_g[l - 1])
        else:
            xh_p, rs_p, g_p = xh_in, rs_in, vec(ln_in_g)
        g_win = _mm_tn("grad_w_in", s["xlb"], dp, 1024, 896)
        sib = sibling_start("in", l, W_IN, [_chip_shards_of_cols(g_win, d)])
        if l > 0:
            tok = token_of(sib)
        else:
            tok = scatter(sib, sib[3][4])
        dzp, dzpb, dg, db = _mm_nt_res_lnbwd("proj_in_bwd_ln", dp, win[l], dz1, xh_p, rs_p, g_p + tok, 1024, 1792)
        if l > 0:
            carry_tok = scatter(sib, dzp)
        dz2, dz2b = dzp, dzpb
        grad_x = dzp
    after = grad_x
    while in_flight:
        after = scatter_finish(in_flight.pop(0), after)
    wt, mt, vt = (jnp.transpose(a, (2, 0, 1)) for a in (big_w[0], big_m[0], big_v[0]))
    ntail = wt.shape[0] % LANE
    tails = [jnp.where(ci == 0, jnp.concatenate([own[:, -ntail:], sib[:, -ntail:]]),
                       jnp.concatenate([sib[:, -ntail:], own[:, -ntail:]])).T for own, sib in w_in_halves]
    adam_t = _adamw_tail("adamw_w_in_tail", wt, mt, vt, jnp.stack(tails, axis=1),
                         _adamw_colmajor("adamw_w_in", wt, mt, vt, w_in_halves, csel))
    adam_out[0] = [jnp.transpose(r, (1, 2, 0)) for r in adam_t]
    small["ln_in_g"], small["ln_in_b"] = dg, db
    rb_pad = 3 * LANE
    pieces = [small["loss"].reshape(-1), jnp.zeros((LANE - 1,), F32), small["ln_in_g"].reshape(-1),
              small["ln_in_b"].reshape(-1)]
    for l in range(DEPTH):
        rb = jnp.pad(small[("rel_bias", l)], ((0, 0), (0, rb_pad - (2 * REL_CLIP + 1))))
        pieces += [rb.reshape(-1), small[("gla_w_lr", l)].reshape(-1), small[("gla_b_lr", l)].reshape(-1),
                   small[("gla_norm_g", l)].reshape(-1), small[("ln1_g", l)].reshape(-1),
                   small[("ln1_b", l)].reshape(-1), small[("ln2_g", l)].reshape(-1), small[("ln2_b", l)].reshape(-1)]
    sizes = [pc.shape[0] for pc in pieces]
    packed = jnp.concatenate(pieces)
    padn = (-packed.shape[0]) % (8 * LANE)
    packed = jnp.concatenate([packed, jnp.zeros((padn,), F32)]).reshape(-1, LANE)
    red = _small_allreduce(packed, after).reshape(-1)

    parts, pos = [], 0
    for sz in sizes:
        parts.append(red[pos:pos + sz])
        pos += sz
    loss = parts[0][0]
    g_ln_in_g, g_ln_in_b = parts[2], parts[3]
    per = 8
    g_rel = jnp.stack([parts[4 + per * l].reshape(ATTN_HEADS, rb_pad)[:, :2 * REL_CLIP + 1] for l in range(DEPTH)])
    g_wlr_full = jnp.stack([parts[5 + per * l].reshape(GATE_RANK, half) for l in range(DEPTH)])
    g_wlr = lax.dynamic_slice_in_dim(g_wlr_full, chip * (half // 4), half // 4, axis=2)
    g_blr = jnp.stack([parts[6 + per * l] for l in range(DEPTH)])
    g_gn = jnp.stack([parts[7 + per * l] for l in range(DEPTH)])
    g_ln1g = jnp.stack([parts[8 + per * l] for l in range(DEPTH)])
    g_ln1b = jnp.stack([parts[9 + per * l] for l in range(DEPTH)])
    g_ln2g = jnp.stack([parts[10 + per * l] for l in range(DEPTH)])
    g_ln2b = jnp.stack([parts[11 + per * l] for l in range(DEPTH)])

    grads = [g_ln_in_g, g_ln_in_b, None, g_rel, g_wlr, g_blr, g_gn, None, None, g_ln1g, g_ln1b, None, None,
             g_ln2g, g_ln2b]
    ws = [ln_in_g, ln_in_b, w_in, rel_bias, gla_w_lr, gla_b_lr, gla_norm_g, w_branch, w_out, ln1_g, ln1_b,
          w_up, w_down, ln2_g, ln2_b]
    ms = [m_ln_in_g, m_ln_in_b, m_w_in, m_rel_bias, m_gla_w_lr, m_gla_b_lr, m_gla_norm_g, m_w_branch, m_w_out,
          m_ln1_g, m_ln1_b, m_w_up, m_w_down, m_ln2_g, m_ln2_b]
    vs = [v_ln_in_g, v_ln_in_b, v_w_in, v_rel_bias, v_gla_w_lr, v_gla_b_lr, v_gla_norm_g, v_w_branch, v_w_out,
          v_ln1_g, v_ln1_b, v_w_up, v_w_down, v_ln2_g, v_ln2_b]

    deltas, new_ms, new_vs = [None] * 15, [None] * 15, [None] * 15
    big_idx = [2, 7, 8, 11, 12]
    for i, res in zip(big_idx, adam_out):
        shp = ws[i].shape
        grads[i], deltas[i], new_ms[i], new_vs[i] = (r.reshape(shp) for r in res)
    small_idx = [i for i in range(15) if i not in big_idx]

    def pack(arrs):
        flat_ = jnp.concatenate([arrs[i].reshape(-1) for i in small_idx])
        pad_ = (-flat_.shape[0]) % (8 * LANE)
        return jnp.concatenate([flat_, jnp.ones((pad_,), F32)]).reshape(-1, LANE)

    dl, nm, nv = _adamw("adamw_small", pack(ws), pack(grads), pack(ms), pack(vs))
    pos = 0
    for i in small_idx:
        sz = int(np.prod(ws[i].shape))
        deltas[i] = dl.reshape(-1)[pos:pos + sz].reshape(ws[i].shape)
        new_ms[i] = nm.reshape(-1)[pos:pos + sz].reshape(ws[i].shape)
        new_vs[i] = nv.reshape(-1)[pos:pos + sz].reshape(ws[i].shape)
        pos += sz

    return (loss, grad_x[None], *grads, *deltas, *new_ms, *new_vs)
```

```python
import functools

import numpy as np
import jax
import jax.numpy as jnp
from jax import lax
from jax.experimental import pallas as pl
from jax.experimental.pallas import tpu as pltpu

F32 = jnp.float32
BF16 = jnp.bfloat16
MXU_DTYPE = BF16
HI = lax.Precision.HIGHEST

DEPTH = 2
CHUNK = 64
N_BRANCH = 3
ATTN_HEADS = 8
ATTN_LEFT = 8
REL_CLIP = 2 * CHUNK
LIN_HEADS = 4
GATE_RANK = 16
GATE_NORM = 16.0
LN_EPS = 1e-5
NEG_INF = -1e30
ALPHA = (2 * DEPTH) ** 0.25
ADAM_LR, ADAM_B1, ADAM_B2, ADAM_EPS, ADAM_WD, ADAM_STEP = 0.001, 0.9, 0.999, 1e-08, 0.01, 10

LANE = 128
VMEM_LIMIT = 56 << 20
QB = 256
KW = 3 * QB
LB = 256
MESH_AXES = ("x", "y", "c")
DEV = pl.DeviceIdType.MESH


def _cp(sem):
    return pltpu.CompilerParams(dimension_semantics=sem, vmem_limit_bytes=VMEM_LIMIT)


def _mx(v):
    return v.astype(MXU_DTYPE)


def _dot(a, b):
    return jnp.dot(_mx(a), _mx(b), preferred_element_type=F32)


def _dot_nt(a, b):
    return lax.dot_general(_mx(a), _mx(b), (((1,), (1,)), ((), ())), preferred_element_type=F32)


def _dot_tn(a, b):
    return lax.dot_general(_mx(a), _mx(b), (((0,), (0,)), ((), ())), preferred_element_type=F32)


def _dot_hi(a, b):
    return jnp.dot(a, b, precision=HI, preferred_element_type=F32)


def _sigmoid(v):
    return 1.0 / (1.0 + jnp.exp(-v))


def _sds(shape, dtype):
    return jax.ShapeDtypeStruct(shape, dtype)


def _mm(name, a, b, tm, tn, nt=False, out_dtype=F32):
    batched = a.ndim == 3
    m, k = a.shape[-2:]
    n = b.shape[-2] if nt else b.shape[-1]
    tm, tn = min(tm, m), min(tn, n)

    def body(a_ref, b_ref, o_ref):
        f = _dot_nt if nt else _dot
        o_ref[...] = f(a_ref[...], b_ref[...]).astype(o_ref.dtype)

    rows_inner = (n // tn) * m < (m // tm) * n

    def ij(u, v):
        return (v, u) if rows_inner else (u, v)

    if batched:
        nb = a.shape[0]
        grid = (nb,) + ij(m // tm, n // tn)
        a_spec = pl.BlockSpec((None, tm, k), lambda g, u, v: (g, ij(u, v)[0], 0))
        b_spec = (pl.BlockSpec((None, tn, k), lambda g, u, v: (g, ij(u, v)[1], 0)) if nt
                  else pl.BlockSpec((None, k, tn), lambda g, u, v: (g, 0, ij(u, v)[1])))
        o_spec = pl.BlockSpec((None, tm, tn), lambda g, u, v: (g,) + ij(u, v))
        out_shape = _sds((nb, m, n), out_dtype)
        sem = ("parallel", "parallel", "parallel")
    else:
        grid = ij(m // tm, n // tn)
        a_spec = pl.BlockSpec((tm, k), lambda u, v: (ij(u, v)[0], 0))
        b_spec = (pl.BlockSpec((tn, k), lambda u, v: (ij(u, v)[1], 0)) if nt
                  else pl.BlockSpec((k, tn), lambda u, v: (0, ij(u, v)[1])))
        o_spec = pl.BlockSpec((tm, tn), lambda u, v: ij(u, v))
        out_shape = _sds((m, n), out_dtype)
        sem = ("parallel", "parallel")
    return pl.pallas_call(body, name=name, grid=grid, in_specs=[a_spec, b_spec], out_specs=o_spec,
                          out_shape=out_shape, compiler_params=_cp(sem))(a, b)


def _mm_tn(name, a, b, tm, tn, shard=None):
    batched = a.ndim == 3
    k, m = a.shape[-2:]
    n = b.shape[-1]
    tm, tn = min(tm, m), min(tn, n)

    def body(a_ref, b_ref, o_ref):
        o_ref[...] = lax.dot_general(_mx(a_ref[...]), _mx(b_ref[...]), (((0,), (0,)), ((), ())),
                                     preferred_element_type=F32)

    if batched:
        nb = a.shape[0]
        grid = (nb, m // tm, n // tn)
        a_spec = pl.BlockSpec((None, k, tm), lambda g, i, j: (g, 0, i))
        b_spec = pl.BlockSpec((None, k, tn), lambda g, i, j: (g, 0, j))
        if shard == "rows":
            assert 4 * tm == m
            o_spec = pl.BlockSpec((None, tm, tn), lambda g, i, j: (i, g, j))
            out_shape = _sds((4, nb * tm, n), F32)
        else:
            o_spec = pl.BlockSpec((None, tm, tn), lambda g, i, j: (g, i, j))
            out_shape = _sds((nb, m, n), F32)
    else:
        grid = (m // tm, n // tn)
        a_spec = pl.BlockSpec((k, tm), lambda i, j: (0, i))
        b_spec = pl.BlockSpec((k, tn), lambda i, j: (0, j))
        if shard == "cols":
            per = n // 4 // tn
            o_spec = pl.BlockSpec((None, tm, tn), lambda i, j: (j // per, i, j % per))
            out_shape = _sds((4, m, n // 4), F32)
        else:
            o_spec = pl.BlockSpec((tm, tn), lambda i, j: (i, j))
            out_shape = _sds((m, n), F32)
    return pl.pallas_call(body, name=name, grid=grid, in_specs=[a_spec, b_spec], out_specs=o_spec,
                          out_shape=out_shape, compiler_params=_cp(("parallel",) * len(grid)))(a, b)


def _ln_rows(y, g, b):
    mu = jnp.mean(y, axis=-1, keepdims=True)
    yc = y - mu
    var = jnp.mean(yc * yc, axis=-1, keepdims=True)
    rs = lax.rsqrt(var + LN_EPS)
    xh = yc * rs
    return xh * g + b, xh, rs


def _ln_in(x, g, b, tm=256):
    t, d = x.shape

    def body(x_ref, g_ref, b_ref, o_ref, ob_ref, xh_ref, rs_ref):
        o, xh, rs = _ln_rows(x_ref[...], g_ref[...], b_ref[...])
        o_ref[...] = o
        ob_ref[...] = o.astype(BF16)
        xh_ref[...] = xh
        rs_ref[...] = rs

    row = pl.BlockSpec((tm, d), lambda i: (i, 0))
    vec = pl.BlockSpec((1, d), lambda i: (0, 0))
    return pl.pallas_call(
        body, name="ln_in", grid=(t // tm,), in_specs=[row, vec, vec],
        out_specs=[row, row, row, pl.BlockSpec((tm, 1), lambda i: (i, 0))],
        out_shape=[_sds((t, d), F32), _sds((t, d), BF16), _sds((t, d), F32), _sds((t, 1), F32)],
        compiler_params=_cp(("parallel",)))(x, g, b)


def _mm_res_ln(name, a, w, res, g, b, tm, relu2):
    t, k = a.shape
    d = w.shape[1]

    def body(a_ref, w_ref, r_ref, g_ref, b_ref, o_ref, ob_ref, xh_ref, rs_ref, *act_ref):
        av = a_ref[...]
        if relu2:
            av = jnp.square(jnp.maximum(av, 0.0))
            act_ref[0][...] = av.astype(BF16)
        y = ALPHA * r_ref[...] + _dot(av, w_ref[...])
        o, xh, rs = _ln_rows(y, g_ref[...], b_ref[...])
        o_ref[...] = o
        ob_ref[...] = o.astype(BF16)
        xh_ref[...] = xh
        rs_ref[...] = rs

    row = pl.BlockSpec((tm, d), lambda i: (i, 0))
    vec = pl.BlockSpec((1, d), lambda i: (0, 0))
    arow = pl.BlockSpec((tm, k), lambda i: (i, 0))
    out_specs = [row, row, row, pl.BlockSpec((tm, 1), lambda i: (i, 0))]
    out_shape = [_sds((t, d), F32), _sds((t, d), BF16), _sds((t, d), F32), _sds((t, 1), F32)]
    if relu2:
        out_specs.append(arow)
        out_shape.append(_sds((t, k), BF16))
    return pl.pallas_call(
        body, name=name, grid=(t // tm,),
        in_specs=[arow, pl.BlockSpec((k, d), lambda i: (0, 0)), row, vec, vec],
        out_specs=out_specs, out_shape=out_shape, compiler_params=_cp(("parallel",)))(a, w, res, g, b)


def _merge_fwd(bo, wb, p, gate_off, tm=512, tn=512):
    _, t, d = bo.shape
    gb = gate_off // tn

    def body(bo_ref, wb_ref, g0, g1, g2, proj_ref, m_ref):
        acc = None
        for n, g_ref in enumerate((g0, g1, g2)):
            pr = _dot(bo_ref[n], wb_ref[n])
            proj_ref[n] = pr
            term = _sigmoid(g_ref[...]) * pr
            acc = term if acc is None else acc + term
        m_ref[...] = acc.astype(BF16)

    gspecs = [pl.BlockSpec((tm, tn), functools.partial(lambda i, j, n: (i, gb + n * (d // tn) + j), n=n))
              for n in range(3)]
    return pl.pallas_call(
        body, name="merge_fwd", grid=(t // tm, d // tn),
        in_specs=[pl.BlockSpec((3, tm, d), lambda i, j: (0, i, 0)),
                  pl.BlockSpec((3, d, tn), lambda i, j: (0, 0, j))] + gspecs,
        out_specs=[pl.BlockSpec((3, tm, tn), lambda i, j: (0, i, j)), pl.BlockSpec((tm, tn), lambda i, j: (i, j))],
        out_shape=[_sds((3, t, d), F32), _sds((t, d), BF16)],
        compiler_params=_cp(("parallel", "parallel")))(bo, wb, p, p, p)


def _merge_bwd(dz, wout, proj, p, gate_off, npad, tm=256):
    t, d = dz.shape

    def body(dz_ref, w_ref, proj_ref, g0, g1, g2, dproj_ref, dp_ref):
        dm = _dot_nt(dz_ref[...], w_ref[...])
        for n, g_ref in enumerate((g0, g1, g2)):
            s = _sigmoid(g_ref[...])
            dproj_ref[n] = (dm * s).astype(BF16)
            dp_ref[:, n * d:(n + 1) * d] = (dm * proj_ref[n] * (s * (1.0 - s))).astype(BF16)

    gspecs = [pl.BlockSpec((tm, d), functools.partial(lambda i, n: (i, gate_off // d + n), n=n)) for n in range(3)]
    return pl.pallas_call(
        body, name="merge_bwd", grid=(t // tm,),
        in_specs=[pl.BlockSpec((tm, d), lambda i: (i, 0)), pl.BlockSpec((d, d), lambda i: (0, 0)),
                  pl.BlockSpec((3, tm, d), lambda i: (0, i, 0))] + gspecs,
        out_specs=[pl.BlockSpec((3, tm, d), lambda i: (0, i, 0)),
                   pl.BlockSpec((tm, 3 * d), lambda i: (i, gate_off // (3 * d)))],
        out_shape=[_sds((3, t, d), BF16), _sds((t, npad), BF16)],
        compiler_params=_cp(("parallel",)))(dz, wout, proj, p, p, p)


def _gate_lr_bwd(dpre, wlr, dp, col_off, tm=512):
    t, k = dpre.shape
    w = wlr.shape[0]

    def body(a_ref, w_ref, dp_in, o_ref):
        o_ref[...] = _dot_nt(a_ref[...], w_ref[...]).astype(BF16)

    return pl.pallas_call(
        body, name="gate_lr_bwd", grid=(t // tm,),
        in_specs=[pl.BlockSpec((tm, k), lambda i: (i, 0)), pl.BlockSpec((w, k), lambda i: (0, 0)), ANY],
        out_specs=pl.BlockSpec((tm, w), lambda i: (i, col_off // w)), out_shape=_sds(dp.shape, BF16),
        input_output_aliases={2: 0}, compiler_params=_cp(("parallel",)))(dpre, wlr, dp)


def _mm_nt_relu2_bwd(dz, wdown, u, tm=512, tn=1024):
    t, d = dz.shape
    f = wdown.shape[0]

    def body(dz_ref, w_ref, u_ref, du_ref):
        da = _dot_nt(dz_ref[...], w_ref[...])
        du_ref[...] = (da * (2.0 * jnp.maximum(u_ref[...], 0.0))).astype(BF16)

    return pl.pallas_call(
        body, name="mlp_down_bwd", grid=(f // tn, t // tm),
        in_specs=[pl.BlockSpec((tm, d), lambda j, i: (i, 0)), pl.BlockSpec((tn, d), lambda j, i: (j, 0)),
                  pl.BlockSpec((tm, tn), lambda j, i: (i, j))],
        out_specs=pl.BlockSpec((tm, tn), lambda j, i: (i, j)), out_shape=_sds((t, f), BF16),
        compiler_params=_cp(("parallel", "parallel")))(dz, wdown, u)


def _ln_bwd_rows(dx, xh, rs, g):
    dxh = dx * g
    m1 = jnp.mean(dxh, axis=-1, keepdims=True)
    m2 = jnp.mean(dxh * xh, axis=-1, keepdims=True)
    return rs * (dxh - m1 - xh * m2)


def _mm_nt_res_lnbwd(name, a, w, dres, xh, rs, g, tm, tk):
    t, k = a.shape
    d = w.shape[0]
    nk = k // tk

    def body(a_ref, w_ref, dr_ref, xh_ref, rs_ref, g_ref, dz_ref, dzb_ref, dg_ref, db_ref, acc_ref):
        i, kk = pl.program_id(0), pl.program_id(1)

        @pl.when(kk == 0)
        def _():
            acc_ref[...] = ALPHA * dr_ref[...]

        acc_ref[...] += _dot_nt(a_ref[...], w_ref[...])

        @pl.when(jnp.logical_and(i == 0, kk == 0))
        def _():
            dg_ref[...] = jnp.zeros_like(dg_ref)
            db_ref[...] = jnp.zeros_like(db_ref)

        @pl.when(kk == nk - 1)
        def _():
            dx = acc_ref[...]
            xhv = xh_ref[...]
            dz = _ln_bwd_rows(dx, xhv, rs_ref[...], g_ref[...])
            dz_ref[...] = dz
            dzb_ref[...] = dz.astype(BF16)
            dg_ref[...] += jnp.sum(dx * xhv, axis=0, keepdims=True)
            db_ref[...] += jnp.sum(dx, axis=0, keepdims=True)

    row = pl.BlockSpec((tm, d), lambda i, kk: (i, 0))
    vec = pl.BlockSpec((1, d), lambda i, kk: (0, 0))
    return pl.pallas_call(
        body, name=name, grid=(t // tm, nk),
        in_specs=[pl.BlockSpec((tm, tk), lambda i, kk: (i, kk)), pl.BlockSpec((d, tk), lambda i, kk: (0, kk)),
                  row, row, pl.BlockSpec((tm, 1), lambda i, kk: (i, 0)), vec],
        out_specs=[row, row, vec, vec],
        out_shape=[_sds((t, d), F32), _sds((t, d), BF16), _sds((1, d), F32), _sds((1, d), F32)],
        scratch_shapes=[pltpu.VMEM((tm, d), F32)],
        compiler_params=_cp(("arbitrary", "arbitrary")))(a, w, dres, xh, rs, g)


def _loss_ln_bwd(x2, target, xh, rs, g, tm=256):
    t, d = x2.shape

    def body(x_ref, t_ref, xh_ref, rs_ref, g_ref, loss_ref, dz_ref, dzb_ref, dg_ref, db_ref):
        @pl.when(pl.program_id(0) == 0)
        def _():
            loss_ref[...] = jnp.zeros_like(loss_ref)
            dg_ref[...] = jnp.zeros_like(dg_ref)
            db_ref[...] = jnp.zeros_like(db_ref)

        err = x_ref[...] - t_ref[...]
        per_row = jnp.mean(err * err, axis=-1, keepdims=True)
        loss_ref[...] += 0.5 * jnp.sum(per_row, axis=0, keepdims=True)
        dx = err * (1.0 / d)
        xhv = xh_ref[...]
        dz = _ln_bwd_rows(dx, xhv, rs_ref[...], g_ref[...])
        dz_ref[...] = dz
        dzb_ref[...] = dz.astype(BF16)
        dg_ref[...] += jnp.sum(dx * xhv, axis=0, keepdims=True)
        db_ref[...] += jnp.sum(dx, axis=0, keepdims=True)

    row = pl.BlockSpec((tm, d), lambda i: (i, 0))
    vec = pl.BlockSpec((1, d), lambda i: (0, 0))
    return pl.pallas_call(
        body, name="loss_ln_bwd", grid=(t // tm,),
        in_specs=[row, row, row, pl.BlockSpec((tm, 1), lambda i: (i, 0)), vec],
        out_specs=[pl.BlockSpec((1, LANE), lambda i: (0, 0)), row, row, vec, vec],
        out_shape=[_sds((1, LANE), F32), _sds((t, d), F32), _sds((t, d), BF16), _sds((1, d), F32),
                   _sds((1, d), F32)],
        compiler_params=_cp(("arbitrary",)))(x2, target, xh, rs, g)


HPA_FWD = 8
HPA = 4


STRIP = 16


def _attn_scores(q_ref, k_refs, bias_ref, i, dh, hh):
    cols = pl.ds(hh * dh, dh)
    q = q_ref[:, cols] * (dh ** -0.5)
    k = jnp.concatenate([r[:, cols] for r in k_refs], axis=0)
    s = _dot_nt(q, k)
    before_start = lax.broadcasted_iota(jnp.int32, (STRIP, KW), 1) < (2 - i) * QB
    strips = []
    for r in range(0, QB, STRIP):
        ss = jnp.where(before_start, NEG_INF, s[r:r + STRIP] + bias_ref[hh, r:r + STRIP])
        e = jnp.exp(ss - jnp.max(ss, axis=-1, keepdims=True))
        strips.append(e / jnp.sum(e, axis=-1, keepdims=True))
    return q, k, strips


def _attn_specs(dh, off, hp):
    w = hp * dh
    qcol, kcol, vcol = off["aq"] // w, off["ak"] // w, off["av"] // w
    q_spec = pl.BlockSpec((QB, w), lambda g, i: (i, qcol + g))
    k_specs = [pl.BlockSpec((QB, w), functools.partial(lambda g, i, j: (jnp.maximum(i - 2 + j, 0), kcol + g), j=j))
               for j in range(3)]
    v_specs = [pl.BlockSpec((QB, w), functools.partial(lambda g, i, j: (jnp.maximum(i - 2 + j, 0), vcol + g), j=j))
               for j in range(3)]
    bias_spec = pl.BlockSpec((hp, QB, KW), lambda g, i: (g, 0, 0))
    return q_spec, k_specs, v_specs, bias_spec


def _attn_fwd(p, bias, d, off):
    t = p.shape[0]
    dh = d // ATTN_HEADS

    def body(q_ref, k0, k1, k2, v0, v1, v2, bias_ref, o_ref):
        for hh in range(HPA_FWD):
            cols = pl.ds(hh * dh, dh)
            _, _, strips = _attn_scores(q_ref, (k0, k1, k2), bias_ref, pl.program_id(1), dh, hh)
            pr = jnp.concatenate([_mx(ps) for ps in strips], axis=0)
            v = jnp.concatenate([v0[:, cols], v1[:, cols], v2[:, cols]], axis=0)
            o_ref[:, cols] = _dot(pr, v).astype(o_ref.dtype)

    q_spec, k_specs, v_specs, bias_spec = _attn_specs(dh, off, HPA_FWD)
    return pl.pallas_call(
        body, name="attn_fwd", grid=(ATTN_HEADS // HPA_FWD, t // QB),
        in_specs=[q_spec] + k_specs + v_specs + [bias_spec],
        out_specs=pl.BlockSpec((None, QB, HPA_FWD * dh), lambda g, i: (0, i, g)),
        out_shape=_sds((N_BRANCH, t, d), BF16),
        compiler_params=_cp(("parallel", "parallel")))(p, p, p, p, p, p, p, bias)


def _attn_bwd(p, bias, do, dp, d, off):
    t = p.shape[0]
    dh = d // ATTN_HEADS
    tp = t + 2 * QB

    def body(q_ref, k0, k1, k2, v0, v1, v2, bias_ref, do_ref, dp_in, dq_ref, dk_out, dv_out, dbias_ref, dk_ref,
             dv_ref):
        i = pl.program_id(1)

        @pl.when(i == 0)
        def _():
            dk_ref[...] = jnp.zeros_like(dk_ref)
            dv_ref[...] = jnp.zeros_like(dv_ref)
            dbias_ref[...] = jnp.zeros_like(dbias_ref)

        rows = pl.ds(pl.multiple_of(i * QB, QB), KW)
        for hh in range(HPA):
            cols = pl.ds(hh * dh, dh)
            q, k, strips = _attn_scores(q_ref, (k0, k1, k2), bias_ref, i, dh, hh)
            v = jnp.concatenate([v0[:, cols], v1[:, cols], v2[:, cols]], axis=0)
            dov = do_ref[:, cols]
            dp = _dot_nt(dov, v)
            ds_strips = []
            for n, ps in enumerate(strips):
                r = n * STRIP
                dps = dp[r:r + STRIP]
                dss = ps * (dps - jnp.sum(ps * dps, axis=-1, keepdims=True))
                dbias_ref[hh, r:r + STRIP] += dss
                ds_strips.append(_mx(dss))
            ds = jnp.concatenate(ds_strips, axis=0)
            pr = jnp.concatenate([_mx(ps) for ps in strips], axis=0)
            dq_ref[:, cols] = (_dot(ds, k) * (dh ** -0.5)).astype(dq_ref.dtype)
            dk_ref[rows, cols] += _dot_tn(ds, q)
            dv_ref[rows, cols] += _dot_tn(pr, dov)

        @pl.when(i == t // QB - 1)
        def _():
            mine = pl.ds(pl.multiple_of(pl.program_id(0) * w, w), w)
            pltpu.sync_copy(dk_ref, dk_out.at[:, mine])
            pltpu.sync_copy(dv_ref, dv_out.at[:, mine])

    w = HPA * dh
    q_spec, k_specs, v_specs, bias_spec = _attn_specs(dh, off, HPA)
    qcol = off["aq"] // w
    return pl.pallas_call(
        body, name="attn_bwd", grid=(ATTN_HEADS // HPA, t // QB),
        in_specs=[q_spec] + k_specs + v_specs + [bias_spec,
                                                 pl.BlockSpec((None, QB, w), lambda g, i: (0, i, g)), ANY],
        out_specs=[pl.BlockSpec((QB, w), lambda g, i: (i, qcol + g)), ANY, ANY, bias_spec],
        out_shape=[_sds(dp.shape, BF16), _sds((tp, d), F32), _sds((tp, d), F32),
                   _sds((ATTN_HEADS, QB, KW), F32)],
        scratch_shapes=[pltpu.VMEM((tp, w), F32), pltpu.VMEM((tp, w), F32)], input_output_aliases={9: 0},
        compiler_params=_cp(("parallel", "arbitrary")))(p, p, p, p, p, p, p, bias, do, dp)


def _onehot_mm(name, a, b):
    def body(a_ref, b_ref, o_ref):
        o_ref[...] = _dot_hi(a_ref[...], b_ref[...])

    return pl.pallas_call(body, name=name, out_shape=_sds((a.shape[0], b.shape[1]), F32),
                          compiler_params=pltpu.CompilerParams(vmem_limit_bytes=VMEM_LIMIT))(a, b)


def _diag_index():
    ii, jj = np.arange(CHUNK)[:, None], np.arange(CHUNK)[None, :]
    return (ii - jj + CHUNK - 1).reshape(-1)


def _bias_expand(rel_bias):
    h = rel_bias.shape[0]
    nq, nk, shift = QB // CHUNK, KW // CHUNK, (2 * QB) // CHUNK
    nbin, ndc = 3 * LANE, 4
    rb = jnp.pad(rel_bias, ((0, 0), (0, nbin - rel_bias.shape[1])))
    win = np.clip(CHUNK * np.arange(ndc)[:, None] + np.arange(LANE)[None, :] - (CHUNK - 1), -REL_CLIP, REL_CLIP)
    sel = (jnp.arange(nbin)[:, None] == jnp.asarray((win + REL_CLIP).reshape(1, -1))).astype(F32)
    windows = _onehot_mm("bias_windows", rb, sel)
    diag_t = (jnp.arange(LANE)[:, None] == jnp.asarray(_diag_index().reshape(1, -1))).astype(F32)
    blocks = _onehot_mm("bias_blocks", windows.reshape(h * ndc, LANE), diag_t).reshape(h, ndc, CHUNK, CHUNK)
    off_band = jnp.full((h, CHUNK, CHUNK), NEG_INF, F32)
    dcs = [shift + nq - 1 - u for u in range(nk + nq - 1)]
    strip = jnp.concatenate([blocks[:, min(dc, ndc - 1)] if 0 <= dc <= ATTN_LEFT else off_band for dc in dcs], axis=2)
    return jnp.concatenate([strip[:, :, (nq - 1 - ic) * CHUNK:(nq - 1 - ic + nk) * CHUNK] for ic in range(nq)],
                           axis=1)


def _bias_reduce(dbias):
    h = dbias.shape[0]
    nq, nk = QB // CHUNK, KW // CHUNK
    nbin = 3 * LANE
    blocks = dbias.reshape(h, nq, CHUNK, nk, CHUNK).transpose(0, 1, 3, 2, 4).reshape(h * nq * nk, CHUNK * CHUNK)
    diag = (jnp.asarray(_diag_index().reshape(-1, 1)) == jnp.arange(LANE)[None, :]).astype(F32)
    ic = np.arange(nq)[:, None, None]
    jc = np.arange(nk)[None, :, None]
    dl = np.arange(LANE)[None, None, :] - (CHUNK - 1)
    rel = np.clip(CHUNK * (ic - jc + (2 * QB) // CHUNK) + dl, -REL_CLIP, REL_CLIP) + REL_CLIP
    bins = (jnp.asarray(rel.reshape(-1, 1)) == jnp.arange(nbin)[None, :]).astype(F32)

    diags = _onehot_mm("bias_diag_sums", blocks, diag)
    out = _onehot_mm("bias_bin_sums", diags.reshape(h, nq * nk * LANE), bins)
    return out[:, :2 * REL_CLIP + 1]


def _chunk_masks():
    r = lax.broadcasted_iota(jnp.int32, (LB, LB), 0)
    c = lax.broadcasted_iota(jnp.int32, (LB, LB), 1)
    return (r // CHUNK) == (c // CHUNK), r >= c, r <= c


def _chunks(a):
    return [a[c * CHUNK:(c + 1) * CHUNK] for c in range(LB // CHUNK)]


def _per_chunk(a, f):
    return jnp.concatenate([jnp.broadcast_to(f(c), c.shape) for c in _chunks(a)], axis=0)


def _dot_sel(sel, x):
    def top(v):
        return lax.bitcast_convert_type(lax.bitcast_convert_type(v, jnp.int32) & jnp.int32(-65536), F32)

    hi = top(x)
    mid = top(x - hi)
    lo = (x - hi) - mid
    d = functools.partial(jnp.dot, sel.astype(jnp.bfloat16), preferred_element_type=F32)
    return d(hi.astype(jnp.bfloat16)) + d(mid.astype(jnp.bfloat16)) + d(lo.astype(jnp.bfloat16))


def _lin_block(gla, q, k, v, aux):
    dk = q.shape[-1]
    same, low, up = _chunk_masks()
    ones = same.astype(F32)
    if gla:
        glr, wlr, blr = aux
        q = q * (dk ** -0.5)
        pre = _dot(glr, wlr) + blr
        log_a = (jnp.minimum(pre, 0.0) - jnp.log(1.0 + jnp.exp(-jnp.abs(pre)))) / GATE_NORM
        b = _dot_sel(jnp.where(low, ones, 0.0), log_a)
        lastb = _per_chunk(b, lambda c: c[CHUNK - 1:])
    else:
        cs, sn, lg = aux
        pre = None
        half = dk // 2
        q = q * cs + pltpu.roll(q, half, 1) * sn
        k = (k * cs + pltpu.roll(k, half, 1) * sn) * (dk ** -0.5)
        pos = (lax.broadcasted_iota(jnp.int32, (LB, dk), 0) % CHUNK).astype(F32) + 1.0
        b = pos * lg
        lastb = jnp.broadcast_to(float(CHUNK) * lg, b.shape)
    eb, enb, el, dec = jnp.exp(b), jnp.exp(-b), jnp.exp(lastb - b), jnp.exp(lastb)
    qf, kf, qb, kb, kl = q * eb, k * enb, q * enb, k * eb, k * el
    s = jnp.where(same, jnp.where(low, _dot_nt(qf, kf), _dot_nt(qb, kb)), 0.0)
    return dict(pre=pre, eb=eb, enb=enb, el=el, dec=dec, qf=qf, kf=kf, qb=qb, kb=kb, kl=kl, s=s,
                same=same, low=low, up=up, ones=ones)


def _lin_norm_gate(gla, o, gate, gn):
    sg = _sigmoid(gate)
    silu = gate * sg
    if gla:
        r = lax.rsqrt(jnp.mean(o * o, axis=-1, keepdims=True) + LN_EPS)
        hn = o * r
        return silu * (hn * gn), (sg, silu, r, hn)
    mu = jnp.mean(o, axis=-1, keepdims=True)
    oc = o - mu
    r = lax.rsqrt(jnp.mean(oc * oc, axis=-1, keepdims=True) + LN_EPS)
    hn = oc * r
    return silu * hn, (sg, silu, r, hn)


HPS = 4


def _lin_specs(gla, dk, dv, off, rev, nb):
    pre = "g" if gla else "r"
    wk, wv = HPS * dk, HPS * dv
    qc, kc, vc, gc = (off[pre + "q"] // wk, off[pre + "k"] // wk, off[pre + "v"] // wv, off[pre + "g"] // wv)

    def blk(i):
        return nb - 1 - i if rev else i

    specs = [pl.BlockSpec((LB, wk), lambda g, i: (blk(i), qc + g)),
             pl.BlockSpec((LB, wk), lambda g, i: (blk(i), kc + g)),
             pl.BlockSpec((LB, wv), lambda g, i: (blk(i), vc + g)),
             pl.BlockSpec((LB, wv), lambda g, i: (blk(i), gc + g))]
    if gla:
        specs += [pl.BlockSpec((LB, LANE), lambda g, i: (blk(i), off["glr"] // LANE)),
                  pl.BlockSpec((LANE, wk), lambda g, i: (0, g)),
                  pl.BlockSpec((1, wk), lambda g, i: (0, g)),
                  pl.BlockSpec((1, dv), lambda g, i: (0, 0))]
    else:
        specs += [pl.BlockSpec((LB, dk), lambda g, i: (blk(i), 0)),
                  pl.BlockSpec((LB, dk), lambda g, i: (blk(i), 0)),
                  pl.BlockSpec((HPS, 1, dk), lambda g, i: (g, 0, 0))]
    return specs, blk


def _lin_aux(gla, refs, rows, hh, dk):
    if gla:
        glr_ref, wlr_ref, blr_ref, gn_ref = refs
        kcols = pl.ds(hh * dk, dk)
        return (glr_ref[rows, :], wlr_ref[:, kcols], blr_ref[:, kcols]), gn_ref[...]
    cs_ref, sn_ref, lg_ref = refs
    return (cs_ref[rows, :], sn_ref[rows, :], lg_ref[hh]), None


def _lin_fwd(gla, p, aux_arrays, d, off, branches, slot):
    t = p.shape[0]
    dk, dv = d // (2 * LIN_HEADS), d // LIN_HEADS
    nb, cb = t // LB, LB // CHUNK
    naux = len(aux_arrays)

    def body(*refs):
        q_ref, k_ref, v_ref, g_ref = refs[:4]
        aux_refs = refs[4:4 + naux]
        o_ref, bo_ref, st_out_ref, st_ref = refs[5 + naux:]

        @pl.when(pl.program_id(1) == 0)
        def _():
            st_ref[...] = jnp.zeros_like(st_ref)

        rows = slice(None)
        for hh in range(HPS):
            kcols, vcols = pl.ds(hh * dk, dk), pl.ds(hh * dv, dv)
            aux, gn = _lin_aux(gla, aux_refs, rows, hh, dk)
            v = v_ref[:, vcols]
            blk = _lin_block(gla, q_ref[:, kcols], k_ref[:, kcols], v, aux)
            st = st_ref[hh]
            inter = []
            for c, (qf, kl, dec, vc) in enumerate(zip(_chunks(blk["qf"]), _chunks(blk["kl"]), _chunks(blk["dec"]),
                                                      _chunks(v))):
                st_out_ref[hh, c] = st
                inter.append(_dot_nt(qf, st))
                st = st * dec[:1] + _dot_tn(vc, kl)
            st_ref[hh] = st
            o = _dot(blk["s"], v) + jnp.concatenate(inter, axis=0)
            o_ref[:, vcols] = o
            out, _ = _lin_norm_gate(gla, o, g_ref[:, vcols], gn)
            bo_ref[:, vcols] = out.astype(BF16)

    specs, _ = _lin_specs(gla, dk, dv, off, False, nb)
    orow = pl.BlockSpec((LB, HPS * dv), lambda g, i: (i, g))
    return pl.pallas_call(
        body, name="gla_fwd" if gla else "ret_fwd", grid=(LIN_HEADS // HPS, nb), in_specs=specs + [ANY],
        out_specs=[orow, pl.BlockSpec((None, LB, HPS * dv), lambda g, i: (slot, i, g)),
                   pl.BlockSpec((HPS, cb, dv, dk), lambda g, i: (g, i, 0, 0))],
        out_shape=[_sds((t, d), F32), _sds(branches.shape, BF16), _sds((LIN_HEADS, t // CHUNK, dv, dk), F32)],
        scratch_shapes=[pltpu.VMEM((HPS, dv, dk), F32)], input_output_aliases={4 + naux: 1},
        compiler_params=_cp(("parallel", "arbitrary")))(p, p, p, p, *aux_arrays, branches)


def _lin_bwd(gla, p, aux_arrays, o, states, dbo, slot, dp, d, off):
    assert HPS == LIN_HEADS
    t = p.shape[0]
    dk, dv = d // (2 * LIN_HEADS), d // LIN_HEADS
    nb, cb = t // LB, LB // CHUNK
    naux = len(aux_arrays)

    def body(*refs):
        q_ref, k_ref, v_ref, g_ref = refs[:4]
        aux_refs = refs[4:4 + naux]
        o_ref, st_in_ref, dbo_ref = refs[4 + naux:7 + naux]
        outs = refs[8 + naux:]
        dq_ref, dk_ref = outs[0].at[:, pl.ds(0, d // 2)], outs[0].at[:, pl.ds(d // 2, d // 2)]
        dv_ref, dg_ref = outs[0].at[:, pl.ds(d, d)], outs[0].at[:, pl.ds(2 * d, d)]
        dst_ref = outs[-1]
        first = pl.program_id(1) == 0

        @pl.when(first)
        def _():
            dst_ref[...] = jnp.zeros_like(dst_ref)

        if gla:
            dpre_ref, dblr_ref, dgn_ref = outs[1:4]

            @pl.when(first)
            def _():
                dblr_ref[...] = jnp.zeros_like(dblr_ref)
                dgn_ref[...] = jnp.zeros_like(dgn_ref)

        rows = slice(None)
        for hh in range(HPS):
            kcols, vcols = pl.ds(hh * dk, dk), pl.ds(hh * dv, dv)
            aux, gn = _lin_aux(gla, aux_refs, rows, hh, dk)
            v = v_ref[:, vcols]
            bk = _lin_block(gla, q_ref[:, kcols], k_ref[:, kcols], v, aux)
            eb, enb, el, dec = bk["eb"], bk["enb"], bk["el"], bk["dec"]
            qf, kf, qb, kb, kl, s = bk["qf"], bk["kf"], bk["qb"], bk["kb"], bk["kl"], bk["s"]
            gate = g_ref[:, vcols]
            dout = dbo_ref[:, vcols]
            _, (sg, silu, r, hn) = _lin_norm_gate(gla, o_ref[:, vcols], gate, gn)
            dsilu = sg * (1.0 + gate * (1.0 - sg))
            if gla:
                y = hn * gn
                dy = dout * silu
                dg_ref[:, vcols] = (dout * y * dsilu).astype(BF16)
                dgn_ref[hh] += jnp.sum(dy * hn, axis=0, keepdims=True)
                dhn = dy * gn
                do = r * (dhn - hn * jnp.mean(dhn * hn, axis=-1, keepdims=True))
            else:
                dhn = dout * silu
                dg_ref[:, vcols] = (dout * hn * dsilu).astype(BF16)
                do = r * (dhn - jnp.mean(dhn, axis=-1, keepdims=True)
                          - hn * jnp.mean(dhn * hn, axis=-1, keepdims=True))
            ds = jnp.where(bk["same"], _dot_nt(do, v), 0.0)
            dsf = jnp.where(bk["low"], ds, 0.0)
            dsb = ds - dsf
            dvv = _dot_tn(s, do)
            dqf = _dot(dsf, kf)
            dkf = _dot_tn(dsf, qf)
            dqb = _dot(dsb, kb)
            dkb = _dot_tn(dsb, qb)
            dst = dst_ref[hh]
            dv_st, dqf_st, dkl_c, ddec_c = [], [], [], []
            parts = zip(reversed(range(cb)), reversed(_chunks(do)), reversed(_chunks(v)), reversed(_chunks(qf)),
                        reversed(_chunks(kl)), reversed(_chunks(dec)))
            for c, do_c, v_c, qf_c, kl_c, dec_c in parts:
                st = st_in_ref[hh, c]
                dv_st.append(_dot_nt(kl_c, dst))
                dkl_c.append(_dot(v_c, dst))
                dqf_st.append(_dot(do_c, st))
                ddec_c.append(jnp.broadcast_to(jnp.sum(dst * st, axis=0, keepdims=True), (CHUNK, dk)))
                dst = dst * dec_c[:1] + _dot_tn(do_c, qf_c)
            dst_ref[hh] = dst

            def cat(pieces):
                return jnp.concatenate(pieces[::-1], axis=0)

            dvv = dvv + cat(dv_st)
            dqf = dqf + cat(dqf_st)
            dkl = cat(dkl_c)
            dq = dqf * eb + dqb * enb
            dkk = dkf * enb + dkb * eb + dkl * el
            dv_ref[:, vcols] = dvv.astype(BF16)
            if gla:
                db = dqf * qf - dkf * kf - dqb * qb + dkb * kb - dkl * kl
                dlast = _per_chunk(dkl * kl, lambda c: jnp.sum(c, axis=0, keepdims=True)) + cat(ddec_c) * dec
                dlog_a = _dot_sel(jnp.where(bk["up"], bk["ones"], 0.0), db) + dlast
                dpre = dlog_a * (1.0 / GATE_NORM) * (1.0 - _sigmoid(bk["pre"]))
                dpre_ref[:, kcols] = dpre
                dblr_ref[hh] += jnp.sum(dpre, axis=0, keepdims=True)
                dq_ref[:, kcols] = (dq * (dk ** -0.5)).astype(BF16)
                dk_ref[:, kcols] = dkk.astype(BF16)
            else:
                cs, sn, _ = aux
                half = dk // 2
                dkk = dkk * (dk ** -0.5)
                dq_ref[:, kcols] = (dq * cs + pltpu.roll(dq * sn, half, 1)).astype(BF16)
                dk_ref[:, kcols] = (dkk * cs + pltpu.roll(dkk * sn, half, 1)).astype(BF16)

    specs, blk = _lin_specs(gla, dk, dv, off, True, nb)
    vrow = pl.BlockSpec((LB, HPS * dv), lambda g, i: (blk(i), g))
    krow = pl.BlockSpec((LB, HPS * dk), lambda g, i: (blk(i), g))
    specs += [vrow, pl.BlockSpec((HPS, cb, dv, dk), lambda g, i: (g, blk(i), 0, 0)),
              pl.BlockSpec((None, LB, HPS * dv), lambda g, i: (slot, blk(i), g)), ANY]
    section = off[("g" if gla else "r") + "q"] // (3 * d)
    out_specs = [pl.BlockSpec((LB, 3 * d), lambda g, i: (blk(i), section))]
    out_shape = [_sds(dp.shape, BF16)]
    if gla:
        out_specs += [krow, pl.BlockSpec((HPS, 1, dk), lambda g, i: (g, 0, 0)),
                      pl.BlockSpec((HPS, 1, dv), lambda g, i: (g, 0, 0))]
        out_shape += [_sds((t, d // 2), F32), _sds((LIN_HEADS, 1, dk), F32), _sds((LIN_HEADS, 1, dv), F32)]
    out_specs.append(pl.BlockSpec((HPS, dv, dk), lambda g, i: (g, 0, 0)))
    out_shape.append(_sds((LIN_HEADS, dv, dk), F32))
    res = pl.pallas_call(
        body, name="gla_bwd" if gla else "ret_bwd", grid=(LIN_HEADS // HPS, nb), in_specs=specs,
        out_specs=out_specs, out_shape=out_shape, input_output_aliases={7 + naux: 0},
        compiler_params=_cp(("parallel", "arbitrary")))(p, p, p, p, *aux_arrays, o, states, dbo, dp)
    return res[:-1]


def _row_tile(rows, cols):
    cap = max(8, (2 << 20) // (4 * cols))
    t = rows
    while t > cap and t % 2 == 0:
        t //= 2
    return t


def _add_half(name, g, t, sel):
    nchip, hr, cols = t.shape
    tr = _row_tile(hr, cols)
    nb = hr // tr

    def body(sel_ref, g_ref, t_ref, o_ref):
        o_ref[...] = g_ref[...] + t_ref[...]

    half = pl.BlockSpec((None, tr, cols), lambda p, i, s: (p, i, 0))
    gs = pltpu.PrefetchScalarGridSpec(
        num_scalar_prefetch=1, grid=(nchip, nb),
        in_specs=[pl.BlockSpec((None, tr, cols), lambda p, i, s: (p, s[0] * nb + i, 0)), half], out_specs=half)
    return pl.pallas_call(body, name=name, grid_spec=gs, out_shape=_sds(t.shape, F32),
                          compiler_params=_cp(("parallel", "parallel")))(sel, g, t)


def _sum_shards(name, h, rcv, sel):
    _, rows, cols = h.shape
    tr = _row_tile(rows, cols)

    def body(sel_ref, h_ref, r0, r1, r2, o_ref):
        o_ref[...] = ((h_ref[...] + r0[...]) + r1[...]) + r2[...]

    rspecs = [pl.BlockSpec((None, tr, cols), functools.partial(lambda i, s, j: (j, i, 0), j=j)) for j in range(3)]
    gs = pltpu.PrefetchScalarGridSpec(
        num_scalar_prefetch=1, grid=(rows // tr,),
        in_specs=[pl.BlockSpec((None, tr, cols), lambda i, s: (s[0], i, 0))] + rspecs,
        out_specs=pl.BlockSpec((tr, cols), lambda i, s: (i, 0)))
    return pl.pallas_call(body, name=name, grid_spec=gs, out_shape=_sds((rows, cols), F32),
                          compiler_params=_cp(("parallel",)))(sel, h, rcv, rcv, rcv)


def _adamw_math(w, g, m, v):
    c1 = 1.0 - ADAM_B1 ** ADAM_STEP
    c2 = 1.0 - ADAM_B2 ** ADAM_STEP
    nm = ADAM_B1 * m + (1.0 - ADAM_B1) * g
    nv = ADAM_B2 * v + (1.0 - ADAM_B2) * jnp.square(g)
    return -ADAM_LR * ((nm / c1) / (jnp.sqrt(nv / c2) + ADAM_EPS) + ADAM_WD * w), nm, nv


def _adamw(name, w, g, m, v):
    rows, cols = w.shape
    tr = _row_tile(rows, cols)

    def body(w_ref, g_ref, m_ref, v_ref, d_ref, nm_ref, nv_ref):
        d_ref[...], nm_ref[...], nv_ref[...] = _adamw_math(w_ref[...], g_ref[...], m_ref[...], v_ref[...])

    spec = pl.BlockSpec((tr, cols), lambda i: (i, 0))
    return pl.pallas_call(body, name=name, grid=(rows // tr,), in_specs=[spec] * 4, out_specs=[spec] * 3,
                          out_shape=[_sds((rows, cols), F32)] * 3, compiler_params=_cp(("parallel",)))(w, g, m, v)


def _adamw_layer(name, w, g_own, g_sib, sel, m, v, layer, prev):
    depth, rows, cols = w.shape
    tr = _row_tile(rows // 2, cols)
    nbh = rows // 2 // tr
    nprev = 0 if prev is None else 4

    def body(sel_ref, w_ref, own_ref, sib_ref, m_ref, v_ref, *rest):
        go_ref, d_ref, nm_ref, nv_ref = rest[nprev:]
        gv = jnp.where(pl.program_id(0) // nbh == sel_ref[0], own_ref[...], sib_ref[...])
        go_ref[...] = gv
        d_ref[...], nm_ref[...], nv_ref[...] = _adamw_math(w_ref[...], gv, m_ref[...], v_ref[...])

    lay = pl.BlockSpec((None, tr, cols), lambda i, s: (layer, i, 0))
    hlf = pl.BlockSpec((tr, cols), lambda i, s: (i % nbh, 0))
    gs = pltpu.PrefetchScalarGridSpec(
        num_scalar_prefetch=1, grid=(2 * nbh,), in_specs=[lay, hlf, hlf, lay, lay] + [ANY] * nprev,
        out_specs=[lay] * 4)
    args = (sel, w, g_own, g_sib, m, v) + (() if prev is None else tuple(prev))
    return pl.pallas_call(
        body, name=name, grid_spec=gs, out_shape=[_sds((depth, rows, cols), F32)] * 4,
        input_output_aliases={6 + k: k for k in range(nprev)},
        compiler_params=_cp(("parallel",)))(*args)


def _adamw_colmajor(name, wt, mt, vt, halves, sel):
    c_dim, depth, r_dim = wt.shape
    hr = r_dim // 2

    def body(sel_ref, w_ref, m_ref, v_ref, *rest):
        g_refs, (go_ref, d_ref, nm_ref, nv_ref) = rest[:2 * depth], rest[2 * depth:]
        own_first = sel_ref[0] == 0
        for l in range(depth):
            own, sib = g_refs[2 * l][...], g_refs[2 * l + 1][...]
            g = jnp.concatenate([jnp.where(own_first, own, sib), jnp.where(own_first, sib, own)], axis=0).T
            go_ref[:, l, :] = g
            d_ref[:, l, :], nm_ref[:, l, :], nv_ref[:, l, :] = _adamw_math(w_ref[:, l, :], g, m_ref[:, l, :],
                                                                          v_ref[:, l, :])

    col = pl.BlockSpec((LANE, depth, r_dim), lambda j, s: (j, 0, 0))
    gs = pltpu.PrefetchScalarGridSpec(
        num_scalar_prefetch=1, grid=(c_dim // LANE,),
        in_specs=[col] * 3 + [pl.BlockSpec((hr, LANE), lambda j, s: (0, j))] * (2 * depth), out_specs=[col] * 4)
    flat = [h for pair in halves for h in pair]
    return pl.pallas_call(body, name=name, grid_spec=gs, out_shape=[_sds(wt.shape, F32)] * 4,
                          compiler_params=_cp(("parallel",)))(sel, wt, mt, vt, *flat)


def _adamw_tail(name, wt, mt, vt, gt_tail, prev):
    c_dim, depth, r_dim = wt.shape
    nt = gt_tail.shape[0]

    def body(w_ref, m_ref, v_ref, g_ref, *rest):
        go_ref, d_ref, nm_ref, nv_ref = rest[4:]
        g = g_ref[...]
        go_ref[...] = g
        d_ref[...], nm_ref[...], nv_ref[...] = _adamw_math(w_ref[...], g, m_ref[...], v_ref[...])

    tail = pl.BlockSpec((nt, depth, r_dim), lambda i: (c_dim // nt - 1, 0, 0))
    return pl.pallas_call(
        body, name=name, grid=(1,), in_specs=[tail] * 3 + [pl.BlockSpec((nt, depth, r_dim), lambda i: (0, 0, 0))]
        + [ANY] * 4, out_specs=[tail] * 4, out_shape=[_sds(wt.shape, F32)] * 4,
        input_output_aliases={4 + k: k for k in range(4)},
        compiler_params=_cp(("arbitrary",)))(wt, mt, vt, gt_tail, *prev)


def _place():
    x, y, c = (lax.axis_index(a) for a in MESH_AXES)
    chips = [(1 - x, y), (x, 1 - y), (1 - x, 1 - y)]
    return x, y, c, chips


def _chip_index(xy):
    return 2 * xy[0] + xy[1]


ANY = pl.BlockSpec(memory_space=pl.ANY)


HBM_SPEC = pl.BlockSpec(memory_space=pltpu.HBM)
SEM = pl.BlockSpec(memory_space=pltpu.SEMAPHORE)
EFFECT = pltpu.SideEffectType.DATAFLOW_SIDE_EFFECTING


def _half(ref, c):
    hr = ref.shape[-2] // 2
    return pl.ds(pl.multiple_of(c * hr, 16), hr)


def _gather_copies(srcs, lands, send, recv):
    x, y, c, chips = _place()
    me = _chip_index((x, y))
    return [pltpu.make_async_remote_copy(src_ref=s.at[_half(s, c)], dst_ref=g.at[me, _half(s, c)],
                                         send_sem=send.at[3 * a + j], recv_sem=recv.at[3 * a + j],
                                         device_id=(*ch, c), device_id_type=DEV)
            for a, (s, g) in enumerate(zip(srcs, lands)) for j, ch in enumerate(chips)]


def _scatter_copies(srcs, lands, send, recv):
    x, y, c, chips = _place()
    return [pltpu.make_async_remote_copy(src_ref=h.at[_chip_index(ch)], dst_ref=r.at[j],
                                         send_sem=send.at[3 * a + j], recv_sem=recv.at[3 * a + j],
                                         device_id=(*ch, c), device_id_type=DEV)
            for a, (h, r) in enumerate(zip(srcs, lands)) for j, ch in enumerate(chips)]


def _in_hbm(a):
    return pltpu.with_memory_space_constraint(a, pltpu.HBM)


def _split_start(name, srcs, land_shapes, copies_fn, after=None, per_src=3):
    ns, nl = len(srcs), len(land_shapes)
    ncp = per_src * ns
    lands = [lax.empty(s.shape, s.dtype) for s in land_shapes]
    behind = [] if after is None else [after]

    def body(*refs):
        src, land = refs[:ns], refs[ns:ns + nl]
        send, recv = refs[ns + nl + len(behind)], refs[ns + nl + len(behind) + 1]
        for cp in copies_fn(src, land, send, recv):
            cp.start()
        refs[-1][...] = jnp.zeros_like(refs[-1])

    bufs = list(srcs) + lands
    outs = pl.pallas_call(
        body, name=name, in_specs=[HBM_SPEC] * (ns + nl) + [ANY] * len(behind),
        out_specs=[SEM, SEM] + [HBM_SPEC] * (ns + nl) + [pl.BlockSpec(memory_space=pltpu.VMEM)],
        out_shape=[pltpu.SemaphoreType.DMA((ncp,)), pltpu.SemaphoreType.DMA((ncp,))]
        + [pltpu.HBM(b.shape, b.dtype) for b in bufs] + [_sds((8, LANE), F32)],
        input_output_aliases={i: 2 + i for i in range(ns + nl)},
        compiler_params=pltpu.CompilerParams(has_side_effects=EFFECT))(*[_in_hbm(b) for b in bufs], *behind)
    return outs[0], outs[1], list(outs[2:2 + ns]), list(outs[2 + ns:2 + ns + nl]), outs[-1]


def _split_wait(name, started, copies_fn, after):
    send, recv, srcs, lands, _ = started
    ns, nl = len(srcs), len(lands)

    def body(*refs):
        src, land = refs[:ns], refs[ns:ns + nl]
        for cp in copies_fn(src, land, refs[ns + nl], refs[ns + nl + 1]):
            cp.wait_send()
            cp.wait_recv()

    bufs = list(srcs) + list(lands)
    outs = pl.pallas_call(
        body, name=name, in_specs=[HBM_SPEC] * (ns + nl) + [SEM, SEM, ANY], out_specs=[HBM_SPEC] * (ns + nl),
        out_shape=[pltpu.HBM(b.shape, b.dtype) for b in bufs],
        input_output_aliases={i: i for i in range(ns + nl)},
        compiler_params=pltpu.CompilerParams(has_side_effects=EFFECT))(*bufs, send, recv, after)
    return list(outs[:ns]), list(outs[ns:])


def _gather_forward(name, lands):
    n = len(lands)

    def body(*refs):
        land, fsend, frecv = refs[n:2 * n], refs[2 * n], refs[2 * n + 1]
        x, y, c, chips = _place()
        passed = []
        for a in range(n):
            for j, ch in enumerate(chips):
                slot = land[a].at[_chip_index(ch), _half(land[a], c)]
                fw = pltpu.make_async_remote_copy(src_ref=slot, dst_ref=slot, send_sem=fsend.at[3 * a + j],
                                                  recv_sem=frecv.at[3 * a + j], device_id=(x, y, 1 - c),
                                                  device_id_type=DEV)
                fw.start()
                passed.append(fw)
        for a in range(n):
            for j, ch in enumerate(chips):
                slot = land[a].at[_chip_index(ch), _half(land[a], 1 - c)]
                pltpu.make_async_remote_copy(src_ref=slot, dst_ref=slot, send_sem=fsend.at[3 * a + j],
                                             recv_sem=frecv.at[3 * a + j], device_id=(x, y, c),
                                             device_id_type=DEV).wait_recv()
        for cp in passed:
            cp.wait_send()

    return pl.pallas_call(
        body, name=name, in_specs=[ANY] * n, out_specs=[ANY] * n,
        out_shape=[_sds(g.shape, g.dtype) for g in lands], input_output_aliases={a: a for a in range(n)},
        scratch_shapes=[pltpu.SemaphoreType.DMA((3 * n,))] * 2)(*lands)


def _sibling_copies(srcs, lands, send, recv):
    x, y, c, _ = _place()
    return [pltpu.make_async_remote_copy(src_ref=g.at[:, _half(g, 1 - c)], dst_ref=t, send_sem=send.at[a],
                                         recv_sem=recv.at[a], device_id=(x, y, 1 - c), device_id_type=DEV)
            for a, (g, t) in enumerate(zip(srcs, lands))]


def _sibling_share(name, sms):
    n = len(sms)

    def body(*refs):
        ins, outs = refs[:n], refs[n:2 * n]
        send, recv = refs[2 * n:]
        x, y, c, _ = _place()
        cps = [pltpu.make_async_remote_copy(src_ref=ins[a], dst_ref=outs[a], send_sem=send.at[a],
                                            recv_sem=recv.at[a], device_id=(x, y, 1 - c), device_id_type=DEV)
               for a in range(n)]
        for cp in cps:
            cp.start()
        for cp in cps:
            cp.wait()

    return pl.pallas_call(
        body, name=name, in_specs=[ANY] * n, out_specs=[ANY] * n, out_shape=[_sds(s.shape, F32) for s in sms],
        scratch_shapes=[pltpu.SemaphoreType.DMA((n,))] * 2)(*sms)


def _small_allreduce(v, after=None):
    rows = v.shape[0]
    ndev = 8
    behind = [] if after is None else [after]

    def body(v_ref, *rest):
        o_ref, gat_ref, send, recv = rest[len(behind):]
        x, y, c, _ = _place()
        me = 4 * x + 2 * y + c
        cps = []
        for k in range(1, ndev):
            to = (me + k) % ndev
            cp = pltpu.make_async_remote_copy(src_ref=v_ref, dst_ref=gat_ref.at[me], send_sem=send.at[k - 1],
                                              recv_sem=recv.at[me], device_id=(to // 4, (to // 2) % 2, to % 2),
                                              device_id_type=DEV)
            cp.start()
            cps.append(cp)
        gat_ref[me] = v_ref[...]
        for k in range(1, ndev):
            frm = (me + k) % ndev
            pltpu.make_async_remote_copy(src_ref=v_ref, dst_ref=gat_ref.at[frm], send_sem=send.at[k - 1],
                                         recv_sem=recv.at[frm], device_id=(x, y, c), device_id_type=DEV).wait_recv()
        for cp in cps:
            cp.wait_send()
        acc = gat_ref[0]
        for k in range(1, ndev):
            acc = acc + gat_ref[k]
        o_ref[...] = acc

    vm = pl.BlockSpec(memory_space=pltpu.VMEM)
    return pl.pallas_call(
        body, name="small_allreduce", in_specs=[vm] + [ANY] * len(behind), out_specs=vm,
        out_shape=_sds((rows, LANE), F32),
        scratch_shapes=[pltpu.VMEM((ndev, rows, LANE), F32), pltpu.SemaphoreType.DMA((ndev - 1,)),
                        pltpu.SemaphoreType.DMA((ndev,))])(v, *behind)


def _layout(d):
    half = d // 2
    names = [("aq", d), ("ak", d), ("av", d), ("rq", half), ("rk", half), ("rv", d), ("rg", d),
             ("gq", half), ("gk", half), ("gv", d), ("gg", d), ("gates", 3 * d), ("glr", 2 * LANE)]
    off, pos = {}, 0
    for nm, sz in names:
        off[nm] = pos
        pos += sz
    return off, pos


def _chip_shards_of_cols(g, d):
    a = 8 * d + d
    per = (a + GATE_RANK + 3 * d) // 4
    sections = ((0, a, 0), (a, a + GATE_RANK, 3 * d), (a + GATE_RANK, 4 * per, -GATE_RANK))

    def quarter(lo, hi):
        cuts = [(max(lo, x), min(hi, y), s) for x, y, s in sections]
        return jnp.concatenate([g[:, x + s:y + s] for x, y, s in cuts if x < y], axis=1)

    return jnp.stack([quarter(c * per, (c + 1) * per) for c in range(4)])


def kernel(x, ln_in_g, ln_in_b, w_in, rel_bias, gla_w_lr, gla_b_lr, gla_norm_g, w_branch, w_out, ln1_g, ln1_b, w_up, w_down, ln2_g, ln2_b, loss_target, m_ln_in_g, m_ln_in_b, m_w_in, m_rel_bias, m_gla_w_lr, m_gla_b_lr, m_gla_norm_g, m_w_branch, m_w_out, m_ln1_g, m_ln1_b, m_w_up, m_w_down, m_ln2_g, m_ln2_b, v_ln_in_g, v_ln_in_b, v_w_in, v_rel_bias, v_gla_w_lr, v_gla_b_lr, v_gla_norm_g, v_w_branch, v_w_out, v_ln1_g, v_ln1_b, v_w_up, v_w_down, v_ln2_g, v_ln2_b):
    t, d = x.shape[1], x.shape[2]
    dff = 4 * d
    half = d // 2
    off, npad = _layout(d)
    xi, yi, ci = (lax.axis_index(a) for a in MESH_AXES)
    chip = 2 * xi + yi
    csel = jnp.reshape(ci, (1,)).astype(jnp.int32)
    psel = jnp.reshape(chip, (1,)).astype(jnp.int32)

    big_w = [w_in, w_branch.reshape(DEPTH, -1, d), w_out, w_up, w_down]
    big_m = [m_w_in, m_w_branch.reshape(DEPTH, -1, d), m_w_out, m_w_up, m_w_down]
    big_v = [v_w_in, v_w_branch.reshape(DEPTH, -1, d), v_w_out, v_w_up, v_w_down]
    W_IN, REST = [0], [1, 2, 3, 4]

    def shards_of(l, idx):
        return [big_w[i][l].astype(BF16) for i in idx]

    def lands_of(srcs):
        return [_sds((4,) + s.shape, s.dtype) for s in srcs]

    def full_w_in(g):
        per = g.shape[2]
        a = 8 * d + d

        def run(lo, hi):
            cuts = [(max(lo, c * per), min(hi, (c + 1) * per), c) for c in range(4)]
            return [g[c, :, x - c * per:y - c * per] for x, y, c in cuts if x < y]

        zeros = jnp.zeros((d, 2 * LANE - GATE_RANK), g.dtype)
        return jnp.concatenate(run(0, a) + run(a + GATE_RANK, 4 * per) + run(a, a + GATE_RANK) + [zeros], axis=1)

    def full_rest(gs):
        g_br, g_out, g_up, g_down = gs
        return (jnp.transpose(g_br.reshape(4, N_BRANCH, d // 4, d), (1, 0, 2, 3)).reshape(N_BRANCH, d, d),
                g_out.reshape(d, d), jnp.transpose(g_up, (1, 0, 2)).reshape(d, dff), g_down.reshape(dff, d))

    def with_own(srcs, lands):
        return [lax.dynamic_update_slice(g, s[None], (chip, 0, 0)) for s, g in zip(srcs, lands)]

    def gather_start(tag, l, idx, after):
        srcs = shards_of(l, idx)
        return srcs, _split_start(f"gather_{tag}{l}_start", srcs, lands_of(srcs), _gather_copies, after)

    def gather_finish(tag, l, pending, after):
        srcs, started = pending
        _, lands = _split_wait(f"gather_{tag}{l}_wait", started, _gather_copies, after)
        return with_own(srcs, _gather_forward(f"gather_{tag}{l}_pass", lands))

    def token(pending):
        return pending[1][4][0, 0]

    dkh = half // LIN_HEADS
    lr_rows = DEPTH * GATE_RANK
    lr_slab = jnp.zeros((lr_rows, 4, half // 4), F32)
    lr_slab = lax.dynamic_update_slice(lr_slab, (gla_w_lr.reshape(lr_rows, 1, half // 4) * jnp.where(ci == 0, 1.0, 0.0)),
                                       (0, chip, 0))
    wlr_full = _small_allreduce(lr_slab.reshape(-1, LANE)).reshape(DEPTH, GATE_RANK, half)
    wlr_pad = jnp.concatenate([wlr_full, jnp.zeros((DEPTH, 2 * LANE - GATE_RANK, half), F32)], axis=1)

    win, wbr, wout, wup, wdown = ([None] * DEPTH for _ in range(5))
    pend_first = gather_start("in", 0, W_IN, wlr_full)

    inv = 10000.0 ** (-jnp.arange(0, dkh, 2, dtype=F32) / dkh)
    ang = jnp.arange(t, dtype=F32)[:, None] * inv[None, :]
    cos, sin = jnp.cos(ang), jnp.sin(ang)
    rope_c = jnp.concatenate([cos, cos], axis=1)
    rope_s = jnp.concatenate([-sin, sin], axis=1)
    log_gamma = jnp.log1p(-jnp.exp2(-5.0 - jnp.arange(LIN_HEADS, dtype=F32)))
    lg_tab = jnp.broadcast_to(log_gamma[:, None, None], (LIN_HEADS, 1, dkh))

    def vec(a):
        return a.reshape(1, -1)

    x0, x0b, xh_in, rs_in = _ln_in(x[0], vec(ln_in_g) + token(pend_first), vec(ln_in_b))
    g_first = gather_finish("in", 0, pend_first, x0b)
    win[0] = full_w_in(g_first[0])
    pend_rest = gather_start("rest", 0, REST, g_first[0])
    saved = []
    xl, xlb = x0, x0b
    for l in range(DEPTH):
        p = _mm("proj_in", xlb, win[l], 512, 1792)
        g_rest = gather_finish("rest", l, pend_rest, p)
        wbr[l], wout[l], wup[l], wdown[l] = full_rest(g_rest)
        tok = 0.0
        if l + 1 < DEPTH:
            pend_in = gather_start("in", l + 1, W_IN, g_rest[0])
            tok = token(pend_in)
        bias = _bias_expand(rel_bias[l] + tok)
        bo = _attn_fwd(p, bias, d, off)
        ret_aux = (rope_c, rope_s, lg_tab + tok)
        gla_aux = (p, wlr_pad[l], vec(gla_b_lr[l]) + tok, vec(gla_norm_g[l]))
        o_ret, bo, st_ret = _lin_fwd(False, p, ret_aux, d, off, bo, 1)
        o_gla, bo, st_gla = _lin_fwd(True, p, gla_aux, d, off, bo, 2)
        tok = 0.0
        if l + 1 < DEPTH:
            g_in = gather_finish("in", l + 1, pend_in, bo)
            win[l + 1] = full_w_in(g_in[0])
            pend_rest = gather_start("rest", l + 1, REST, g_in[0])
            tok = token(pend_rest)
        proj, merged = _merge_fwd(bo, wbr[l], p, off["gates"])
        x1, x1b, xh1, rs1 = _mm_res_ln("out_proj_ln", merged, wout[l], xl, vec(ln1_g[l]) + tok, vec(ln1_b[l]),
                                       256, False)
        u = _mm("mlp_up", x1b, wup[l], 1024, 1024)
        x2, x2b, xh2, rs2, act = _mm_res_ln("mlp_down_ln", u, wdown[l], x1, vec(ln2_g[l]), vec(ln2_b[l]), 256, True)
        saved.append(dict(xlb=xlb, p=p, bias=bias, ret_aux=ret_aux, gla_aux=gla_aux, o_ret=o_ret, o_gla=o_gla,
                          st_ret=st_ret, st_gla=st_gla, bo=bo, proj=proj, merged=merged, x1b=x1b, xh1=xh1,
                          rs1=rs1, u=u, xh2=xh2, rs2=rs2, act=act))
        xl, xlb = x2, x2b

    small = {}
    last = saved[-1]
    loss_p, dz2, dz2b, dg, db = _loss_ln_bwd(xl, loss_target[0], last["xh2"], last["rs2"], vec(ln2_g[DEPTH - 1]))
    small["loss"] = loss_p[:, :1]
    grad_x = None

    def sibling_start(tag, l, idx, shards):
        lands = [_sds((g.shape[0], g.shape[1] // 2, g.shape[2]), F32) for g in shards]
        return tag, l, idx, _split_start(f"grad_{tag}{l}_sibling_start", shards, lands, _sibling_copies, per_src=1)

    def scatter_start(sibling, after):
        tag, l, idx, started = sibling
        shards, theirs = _split_wait(f"grad_{tag}{l}_sibling_wait", started, _sibling_copies, after)
        hs = [_add_half("grad_sibling_add", g, th, csel) for g, th in zip(shards, theirs)]
        lands = [_sds((3,) + h.shape[1:], F32) for h in hs]
        return tag, l, idx, _split_start(f"grad_{tag}{l}_scatter_start", hs, lands, _scatter_copies)

    adam_out = [None] * len(big_w)
    w_in_halves = [None] * DEPTH

    def scatter_finish(pending, after):
        tag, l, idx, started = pending
        hs, rcv = _split_wait(f"grad_{tag}{l}_scatter_wait", started, _scatter_copies, after)
        sms = [_sum_shards("grad_chip_sum", h, r, psel) for h, r in zip(hs, rcv)]
        last = None
        for i, own, sib in zip(idx, sms, _sibling_share(f"grad_{tag}{l}_share", sms)):
            if i == W_IN[0]:
                w_in_halves[l] = (own, sib)
                last = sib
            else:
                adam_out[i] = _adamw_layer("adamw_large", big_w[i], own, sib, csel, big_m[i], big_v[i], l,
                                           adam_out[i])
                last = adam_out[i][0]
        return last

    in_flight = []

    def scatter(sibling, after):
        pending = scatter_start(sibling, after)
        in_flight.append(pending)
        if len(in_flight) > 3:
            scatter_finish(in_flight.pop(0), pending[3][4])
        return pending[3][4][0, 0]

    def token_of(sibling):
        return sibling[3][4][0, 0]

    carry_tok = 0.0
    for l in reversed(range(DEPTH)):
        s = saved[l]
        small[("ln2_g", l)], small[("ln2_b", l)] = dg, db
        du = _mm_nt_relu2_bwd(dz2b, wdown[l], s["u"])
        g_wdown = _mm_tn("grad_w_down", s["act"], dz2b, 512, 512)
        g_wup = _mm_tn("grad_w_up", s["x1b"], du, 512, 512, shard="cols")
        dz1, dz1b, dg1, db1 = _mm_nt_res_lnbwd("mlp_up_bwd_ln", du, wup[l], dz2, s["xh1"], s["rs1"],
                                               vec(ln1_g[l]) + carry_tok, 256, dff)
        small[("ln1_g", l)], small[("ln1_b", l)] = dg1, db1
        dproj, dp = _merge_bwd(dz1b, wout[l], s["proj"], s["p"], off["gates"], npad)
        g_wout = _mm_tn("grad_w_out", s["merged"], dz1b, 512, 512)
        dbo = _mm("branch_proj_bwd", dproj, wbr[l], 1024, 1024, nt=True)
        g_wbr = _mm_tn("grad_w_branch", s["bo"], dproj, d // 4, 1024, shard="rows")
        sib = sibling_start("rest", l, REST, [g_wbr, g_wout.reshape(4, d // 4, d), g_wup, g_wdown.reshape(4, d, d)])
        rc, rs_, lg = s["ret_aux"]
        gp, gw, gb, gn_ = s["gla_aux"]
        dp, dk_acc, dv_acc, dbias = _attn_bwd(s["p"], s["bias"] + token_of(sib), dbo, dp, d, off)
        tok = scatter(sib, dbias)
        small[("rel_bias", l)] = _bias_reduce(dbias)
        dp = lax.dynamic_update_slice(dp, dk_acc[2 * QB:].astype(BF16), (0, off["ak"]))
        dp = lax.dynamic_update_slice(dp, dv_acc[2 * QB:].astype(BF16), (0, off["av"]))
        (dp,) = _lin_bwd(False, s["p"], (rc, rs_, lg + tok), s["o_ret"], s["st_ret"], dbo, 1, dp, d, off)
        dp, dpre, dblr, dgn = _lin_bwd(True, s["p"], (gp, gw, gb + tok, gn_), s["o_gla"], s["st_gla"], dbo, 2, dp,
                                       d, off)
        small[("gla_b_lr", l)] = dblr.reshape(1, half)
        small[("gla_norm_g", l)] = jnp.sum(dgn, axis=0)
        dpre_b = dpre.astype(BF16)
        glr_b = s["p"][:, off["glr"]:off["glr"] + LANE].astype(BF16)
        dp = _gate_lr_bwd(dpre_b, wlr_pad[l], dp, off["glr"])
        small[("gla_w_lr", l)] = _mm_tn("grad_gla_w_lr", glr_b, dpre_b, LANE, half)[:GATE_RANK]
        if l > 0:
            prev = saved[l - 1]
            xh_p, rs_p, g_p = prev["xh2"], prev["rs2"], vec(ln2_g[l - 1])
        else:
            xh_p, rs_p, g_p = xh_in, rs_in, vec(ln_in_g)
        g_win = _mm_tn("grad_w_in", s["xlb"], dp, 1024, 896)
        sib = sibling_start("in", l, W_IN, [_chip_shards_of_cols(g_win, d)])
        if l > 0:
            tok = token_of(sib)
        else:
            tok = scatter(sib, sib[3][4])
        dzp, dzpb, dg, db = _mm_nt_res_lnbwd("proj_in_bwd_ln", dp, win[l], dz1, xh_p, rs_p, g_p + tok, 1024, 1792)
        if l > 0:
            carry_tok = scatter(sib, dzp)
        dz2, dz2b = dzp, dzpb
        grad_x = dzp
    after = grad_x
    while in_flight:
        after = scatter_finish(in_flight.pop(0), after)
    wt, mt, vt = (jnp.transpose(a, (2, 0, 1)) for a in (big_w[0], big_m[0], big_v[0]))
    ntail = wt.shape[0] % LANE
    tails = [jnp.where(ci == 0, jnp.concatenate([own[:, -ntail:], sib[:, -ntail:]]),
                       jnp.concatenate([sib[:, -ntail:], own[:, -ntail:]])).T for own, sib in w_in_halves]
    adam_t = _adamw_tail("adamw_w_in_tail", wt, mt, vt, jnp.stack(tails, axis=1),
                         _adamw_colmajor("adamw_w_in", wt, mt, vt, w_in_halves, csel))
    adam_out[0] = [jnp.transpose(r, (1, 2, 0)) for r in adam_t]
    small["ln_in_g"], small["ln_in_b"] = dg, db
    rb_pad = 3 * LANE
    pieces = [small["loss"].reshape(-1), jnp.zeros((LANE - 1,), F32), small["ln_in_g"].reshape(-1),
              small["ln_in_b"].reshape(-1)]
    for l in range(DEPTH):
        rb = jnp.pad(small[("rel_bias", l)], ((0, 0), (0, rb_pad - (2 * REL_CLIP + 1))))
        pieces += [rb.reshape(-1), small[("gla_w_lr", l)].reshape(-1), small[("gla_b_lr", l)].reshape(-1),
                   small[("gla_norm_g", l)].reshape(-1), small[("ln1_g", l)].reshape(-1),
                   small[("ln1_b", l)].reshape(-1), small[("ln2_g", l)].reshape(-1), small[("ln2_b", l)].reshape(-1)]
    sizes = [pc.shape[0] for pc in pieces]
    packed = jnp.concatenate(pieces)
    padn = (-packed.shape[0]) % (8 * LANE)
    packed = jnp.concatenate([packed, jnp.zeros((padn,), F32)]).reshape(-1, LANE)
    red = _small_allreduce(packed, after).reshape(-1)

    parts, pos = [], 0
    for sz in sizes:
        parts.append(red[pos:pos + sz])
        pos += sz
    loss = parts[0][0]
    g_ln_in_g, g_ln_in_b = parts[2], parts[3]
    per = 8
    g_rel = jnp.stack([parts[4 + per * l].reshape(ATTN_HEADS, rb_pad)[:, :2 * REL_CLIP + 1] for l in range(DEPTH)])
    g_wlr_full = jnp.stack([parts[5 + per * l].reshape(GATE_RANK, half) for l in range(DEPTH)])
    g_wlr = lax.dynamic_slice_in_dim(g_wlr_full, chip * (half // 4), half // 4, axis=2)
    g_blr = jnp.stack([parts[6 + per * l] for l in range(DEPTH)])
    g_gn = jnp.stack([parts[7 + per * l] for l in range(DEPTH)])
    g_ln1g = jnp.stack([parts[8 + per * l] for l in range(DEPTH)])
    g_ln1b = jnp.stack([parts[9 + per * l] for l in range(DEPTH)])
    g_ln2g = jnp.stack([parts[10 + per * l] for l in range(DEPTH)])
    g_ln2b = jnp.stack([parts[11 + per * l] for l in range(DEPTH)])

    grads = [g_ln_in_g, g_ln_in_b, None, g_rel, g_wlr, g_blr, g_gn, None, None, g_ln1g, g_ln1b, None, None,
             g_ln2g, g_ln2b]
    ws = [ln_in_g, ln_in_b, w_in, rel_bias, gla_w_lr, gla_b_lr, gla_norm_g, w_branch, w_out, ln1_g, ln1_b,
          w_up, w_down, ln2_g, ln2_b]
    ms = [m_ln_in_g, m_ln_in_b, m_w_in, m_rel_bias, m_gla_w_lr, m_gla_b_lr, m_gla_norm_g, m_w_branch, m_w_out,
          m_ln1_g, m_ln1_b, m_w_up, m_w_down, m_ln2_g, m_ln2_b]
    vs = [v_ln_in_g, v_ln_in_b, v_w_in, v_rel_bias, v_gla_w_lr, v_gla_b_lr, v_gla_norm_g, v_w_branch, v_w_out,
          v_ln1_g, v_ln1_b, v_w_up, v_w_down, v_ln2_g, v_ln2_b]

    deltas, new_ms, new_vs = [None] * 15, [None] * 15, [None] * 15
    big_idx = [2, 7, 8, 11, 12]
    for i, res in zip(big_idx, adam_out):
        shp = ws[i].shape
        grads[i], deltas[i], new_ms[i], new_vs[i] = (r.reshape(shp) for r in res)
    small_idx = [i for i in range(15) if i not in big_idx]

    def pack(arrs):
        flat_ = jnp.concatenate([arrs[i].reshape(-1) for i in small_idx])
        pad_ = (-flat_.shape[0]) % (8 * LANE)
        return jnp.concatenate([flat_, jnp.ones((pad_,), F32)]).reshape(-1, LANE)

    dl, nm, nv = _adamw("adamw_small", pack(ws), pack(grads), pack(ms), pack(vs))
    pos = 0
    for i in small_idx:
        sz = int(np.prod(ws[i].shape))
        deltas[i] = dl.reshape(-1)[pos:pos + sz].reshape(ws[i].shape)
        new_ms[i] = nm.reshape(-1)[pos:pos + sz].reshape(ws[i].shape)
        new_vs[i] = nv.reshape(-1)[pos:pos + sz].reshape(ws[i].shape)
        pos += sz

    return (loss, grad_x[None], *grads, *deltas, *new_ms, *new_vs)
```

```python
import functools

import numpy as np
import jax
import jax.numpy as jnp
from jax import lax
from jax.experimental import pallas as pl
from jax.experimental.pallas import tpu as pltpu

F32 = jnp.float32
BF16 = jnp.bfloat16
MXU_DTYPE = BF16
HI = lax.Precision.HIGHEST

DEPTH = 2
CHUNK = 64
N_BRANCH = 3
ATTN_HEADS = 8
ATTN_LEFT = 8
REL_CLIP = 2 * CHUNK
LIN_HEADS = 4
GATE_RANK = 16
GATE_NORM = 16.0
LN_EPS = 1e-5
NEG_INF = -1e30
ALPHA = (2 * DEPTH) ** 0.25
ADAM_LR, ADAM_B1, ADAM_B2, ADAM_EPS, ADAM_WD, ADAM_STEP = 0.001, 0.9, 0.999, 1e-08, 0.01, 10

LANE = 128
VMEM_LIMIT = 56 << 20
QB = 256
KW = 3 * QB
LB = 256
MESH_AXES = ("x", "y", "c")
DEV = pl.DeviceIdType.MESH


def _cp(sem):
    return pltpu.CompilerParams(dimension_semantics=sem, vmem_limit_bytes=VMEM_LIMIT)


def _mx(v):
    return v.astype(MXU_DTYPE)


def _dot(a, b):
    return jnp.dot(_mx(a), _mx(b), preferred_element_type=F32)


def _dot_nt(a, b):
    return lax.dot_general(_mx(a), _mx(b), (((1,), (1,)), ((), ())), preferred_element_type=F32)


def _dot_tn(a, b):
    return lax.dot_general(_mx(a), _mx(b), (((0,), (0,)), ((), ())), preferred_element_type=F32)


def _dot_hi(a, b):
    return jnp.dot(a, b, precision=HI, preferred_element_type=F32)


def _sigmoid(v):
    return 1.0 / (1.0 + jnp.exp(-v))


def _sds(shape, dtype):
    return jax.ShapeDtypeStruct(shape, dtype)


def _mm(name, a, b, tm, tn, nt=False, out_dtype=F32):
    batched = a.ndim == 3
    m, k = a.shape[-2:]
    n = b.shape[-2] if nt else b.shape[-1]
    tm, tn = min(tm, m), min(tn, n)

    def body(a_ref, b_ref, o_ref):
        f = _dot_nt if nt else _dot
        o_ref[...] = f(a_ref[...], b_ref[...]).astype(o_ref.dtype)

    rows_inner = (n // tn) * m < (m // tm) * n

    def ij(u, v):
        return (v, u) if rows_inner else (u, v)

    if batched:
        nb = a.shape[0]
        grid = (nb,) + ij(m // tm, n // tn)
        a_spec = pl.BlockSpec((None, tm, k), lambda g, u, v: (g, ij(u, v)[0], 0))
        b_spec = (pl.BlockSpec((None, tn, k), lambda g, u, v: (g, ij(u, v)[1], 0)) if nt
                  else pl.BlockSpec((None, k, tn), lambda g, u, v: (g, 0, ij(u, v)[1])))
        o_spec = pl.BlockSpec((None, tm, tn), lambda g, u, v: (g,) + ij(u, v))
        out_shape = _sds((nb, m, n), out_dtype)
        sem = ("parallel", "parallel", "parallel")
    else:
        grid = ij(m // tm, n // tn)
        a_spec = pl.BlockSpec((tm, k), lambda u, v: (ij(u, v)[0], 0))
        b_spec = (pl.BlockSpec((tn, k), lambda u, v: (ij(u, v)[1], 0)) if nt
                  else pl.BlockSpec((k, tn), lambda u, v: (0, ij(u, v)[1])))
        o_spec = pl.BlockSpec((tm, tn), lambda u, v: ij(u, v))
        out_shape = _sds((m, n), out_dtype)
        sem = ("parallel", "parallel")
    return pl.pallas_call(body, name=name, grid=grid, in_specs=[a_spec, b_spec], out_specs=o_spec,
                          out_shape=out_shape, compiler_params=_cp(sem))(a, b)


def _mm_tn(name, a, b, tm, tn, shard=None):
    batched = a.ndim == 3
    k, m = a.shape[-2:]
    n = b.shape[-1]
    tm, tn = min(tm, m), min(tn, n)

    def body(a_ref, b_ref, o_ref):
        o_ref[...] = lax.dot_general(_mx(a_ref[...]), _mx(b_ref[...]), (((0,), (0,)), ((), ())),
                                     preferred_element_type=F32)

    if batched:
        nb = a.shape[0]
        grid = (nb, m // tm, n // tn)
        a_spec = pl.BlockSpec((None, k, tm), lambda g, i, j: (g, 0, i))
        b_spec = pl.BlockSpec((None, k, tn), lambda g, i, j: (g, 0, j))
        if shard == "rows":
            assert 4 * tm == m
            o_spec = pl.BlockSpec((None, tm, tn), lambda g, i, j: (i, g, j))
            out_shape = _sds((4, nb * tm, n), F32)
        else:
            o_spec = pl.BlockSpec((None, tm, tn), lambda g, i, j: (g, i, j))
            out_shape = _sds((nb, m, n), F32)
    else:
        grid = (m // tm, n // tn)
        a_spec = pl.BlockSpec((k, tm), lambda i, j: (0, i))
        b_spec = pl.BlockSpec((k, tn), lambda i, j: (0, j))
        if shard == "cols":
            per = n // 4 // tn
            o_spec = pl.BlockSpec((None, tm, tn), lambda i, j: (j // per, i, j % per))
            out_shape = _sds((4, m, n // 4), F32)
        else:
            o_spec = pl.BlockSpec((tm, tn), lambda i, j: (i, j))
            out_shape = _sds((m, n), F32)
    return pl.pallas_call(body, name=name, grid=grid, in_specs=[a_spec, b_spec], out_specs=o_spec,
                          out_shape=out_shape, compiler_params=_cp(("parallel",) * len(grid)))(a, b)


def _ln_rows(y, g, b):
    mu = jnp.mean(y, axis=-1, keepdims=True)
    yc = y - mu
    var = jnp.mean(yc * yc, axis=-1, keepdims=True)
    rs = lax.rsqrt(var + LN_EPS)
    xh = yc * rs
    return xh * g + b, xh, rs


def _ln_in(x, g, b, tm=256):
    t, d = x.shape

    def body(x_ref, g_ref, b_ref, o_ref, ob_ref, xh_ref, rs_ref):
        o, xh, rs = _ln_rows(x_ref[...], g_ref[...], b_ref[...])
        o_ref[...] = o
        ob_ref[...] = o.astype(BF16)
        xh_ref[...] = xh
        rs_ref[...] = rs

    row = pl.BlockSpec((tm, d), lambda i: (i, 0))
    vec = pl.BlockSpec((1, d), lambda i: (0, 0))
    return pl.pallas_call(
        body, name="ln_in", grid=(t // tm,), in_specs=[row, vec, vec],
        out_specs=[row, row, row, pl.BlockSpec((tm, 1), lambda i: (i, 0))],
        out_shape=[_sds((t, d), F32), _sds((t, d), BF16), _sds((t, d), F32), _sds((t, 1), F32)],
        compiler_params=_cp(("parallel",)))(x, g, b)


def _mm_res_ln(name, a, w, res, g, b, tm, relu2):
    t, k = a.shape
    d = w.shape[1]

    def body(a_ref, w_ref, r_ref, g_ref, b_ref, o_ref, ob_ref, xh_ref, rs_ref, *act_ref):
        av = a_ref[...]
        if relu2:
            av = jnp.square(jnp.maximum(av, 0.0))
            act_ref[0][...] = av.astype(BF16)
        y = ALPHA * r_ref[...] + _dot(av, w_ref[...])
        o, xh, rs = _ln_rows(y, g_ref[...], b_ref[...])
        o_ref[...] = o
        ob_ref[...] = o.astype(BF16)
        xh_ref[...] = xh
        rs_ref[...] = rs

    row = pl.BlockSpec((tm, d), lambda i: (i, 0))
    vec = pl.BlockSpec((1, d), lambda i: (0, 0))
    arow = pl.BlockSpec((tm, k), lambda i: (i, 0))
    out_specs = [row, row, row, pl.BlockSpec((tm, 1), lambda i: (i, 0))]
    out_shape = [_sds((t, d), F32), _sds((t, d), BF16), _sds((t, d), F32), _sds((t, 1), F32)]
    if relu2:
        out_specs.append(arow)
        out_shape.append(_sds((t, k), BF16))
    return pl.pallas_call(
        body, name=name, grid=(t // tm,),
        in_specs=[arow, pl.BlockSpec((k, d), lambda i: (0, 0)), row, vec, vec],
        out_specs=out_specs, out_shape=out_shape, compiler_params=_cp(("parallel",)))(a, w, res, g, b)


def _merge_fwd(bo, wb, p, gate_off, tm=512, tn=512):
    _, t, d = bo.shape
    gb = gate_off // tn

    def body(bo_ref, wb_ref, g0, g1, g2, proj_ref, m_ref):
        acc = None
        for n, g_ref in enumerate((g0, g1, g2)):
            pr = _dot(bo_ref[n], wb_ref[n])
            proj_ref[n] = pr
            term = _sigmoid(g_ref[...]) * pr
            acc = term if acc is None else acc + term
        m_ref[...] = acc.astype(BF16)

    gspecs = [pl.BlockSpec((tm, tn), functools.partial(lambda i, j, n: (i, gb + n * (d // tn) + j), n=n))
              for n in range(3)]
    return pl.pallas_call(
        body, name="merge_fwd", grid=(t // tm, d // tn),
        in_specs=[pl.BlockSpec((3, tm, d), lambda i, j: (0, i, 0)),
                  pl.BlockSpec((3, d, tn), lambda i, j: (0, 0, j))] + gspecs,
        out_specs=[pl.BlockSpec((3, tm, tn), lambda i, j: (0, i, j)), pl.BlockSpec((tm, tn), lambda i, j: (i, j))],
        out_shape=[_sds((3, t, d), F32), _sds((t, d), BF16)],
        compiler_params=_cp(("parallel", "parallel")))(bo, wb, p, p, p)


def _merge_bwd(dz, wout, proj, p, gate_off, npad, tm=256):
    t, d = dz.shape

    def body(dz_ref, w_ref, proj_ref, g0, g1, g2, dproj_ref, dp_ref):
        dm = _dot_nt(dz_ref[...], w_ref[...])
        for n, g_ref in enumerate((g0, g1, g2)):
            s = _sigmoid(g_ref[...])
            dproj_ref[n] = (dm * s).astype(BF16)
            dp_ref[:, n * d:(n + 1) * d] = (dm * proj_ref[n] * (s * (1.0 - s))).astype(BF16)

    gspecs = [pl.BlockSpec((tm, d), functools.partial(lambda i, n: (i, gate_off // d + n), n=n)) for n in range(3)]
    return pl.pallas_call(
        body, name="merge_bwd", grid=(t // tm,),
        in_specs=[pl.BlockSpec((tm, d), lambda i: (i, 0)), pl.BlockSpec((d, d), lambda i: (0, 0)),
                  pl.BlockSpec((3, tm, d), lambda i: (0, i, 0))] + gspecs,
        out_specs=[pl.BlockSpec((3, tm, d), lambda i: (0, i, 0)),
                   pl.BlockSpec((tm, 3 * d), lambda i: (i, gate_off // (3 * d)))],
        out_shape=[_sds((3, t, d), BF16), _sds((t, npad), BF16)],
        compiler_params=_cp(("parallel",)))(dz, wout, proj, p, p, p)


def _gate_lr_bwd(dpre, wlr, dp, col_off, tm=512):
    t, k = dpre.shape
    w = wlr.shape[0]

    def body(a_ref, w_ref, dp_in, o_ref):
        o_ref[...] = _dot_nt(a_ref[...], w_ref[...]).astype(BF16)

    return pl.pallas_call(
        body, name="gate_lr_bwd", grid=(t // tm,),
        in_specs=[pl.BlockSpec((tm, k), lambda i: (i, 0)), pl.BlockSpec((w, k), lambda i: (0, 0)), ANY],
        out_specs=pl.BlockSpec((tm, w), lambda i: (i, col_off // w)), out_shape=_sds(dp.shape, BF16),
        input_output_aliases={2: 0}, compiler_params=_cp(("parallel",)))(dpre, wlr, dp)


def _mm_nt_relu2_bwd(dz, wdown, u, tm=512, tn=1024):
    t, d = dz.shape
    f = wdown.shape[0]

    def body(dz_ref, w_ref, u_ref, du_ref):
        da = _dot_nt(dz_ref[...], w_ref[...])
        du_ref[...] = (da * (2.0 * jnp.maximum(u_ref[...], 0.0))).astype(BF16)

    return pl.pallas_call(
        body, name="mlp_down_bwd", grid=(f // tn, t // tm),
        in_specs=[pl.BlockSpec((tm, d), lambda j, i: (i, 0)), pl.BlockSpec((tn, d), lambda j, i: (j, 0)),
                  pl.BlockSpec((tm, tn), lambda j, i: (i, j))],
        out_specs=pl.BlockSpec((tm, tn), lambda j, i: (i, j)), out_shape=_sds((t, f), BF16),
        compiler_params=_cp(("parallel", "parallel")))(dz, wdown, u)


def _ln_bwd_rows(dx, xh, rs, g):
    dxh = dx * g
    m1 = jnp.mean(dxh, axis=-1, keepdims=True)
    m2 = jnp.mean(dxh * xh, axis=-1, keepdims=True)
    return rs * (dxh - m1 - xh * m2)


def _mm_nt_res_lnbwd(name, a, w, dres, xh, rs, g, tm, tk):
    t, k = a.shape
    d = w.shape[0]
    nk = k // tk

    def body(a_ref, w_ref, dr_ref, xh_ref, rs_ref, g_ref, dz_ref, dzb_ref, dg_ref, db_ref, acc_ref):
        i, kk = pl.program_id(0), pl.program_id(1)

        @pl.when(kk == 0)
        def _():
            acc_ref[...] = ALPHA * dr_ref[...]

        acc_ref[...] += _dot_nt(a_ref[...], w_ref[...])

        @pl.when(jnp.logical_and(i == 0, kk == 0))
        def _():
            dg_ref[...] = jnp.zeros_like(dg_ref)
            db_ref[...] = jnp.zeros_like(db_ref)

        @pl.when(kk == nk - 1)
        def _():
            dx = acc_ref[...]
            xhv = xh_ref[...]
            dz = _ln_bwd_rows(dx, xhv, rs_ref[...], g_ref[...])
            dz_ref[...] = dz
            dzb_ref[...] = dz.astype(BF16)
            dg_ref[...] += jnp.sum(dx * xhv, axis=0, keepdims=True)
            db_ref[...] += jnp.sum(dx, axis=0, keepdims=True)

    row = pl.BlockSpec((tm, d), lambda i, kk: (i, 0))
    vec = pl.BlockSpec((1, d), lambda i, kk: (0, 0))
    return pl.pallas_call(
        body, name=name, grid=(t // tm, nk),
        in_specs=[pl.BlockSpec((tm, tk), lambda i, kk: (i, kk)), pl.BlockSpec((d, tk), lambda i, kk: (0, kk)),
                  row, row, pl.BlockSpec((tm, 1), lambda i, kk: (i, 0)), vec],
        out_specs=[row, row, vec, vec],
        out_shape=[_sds((t, d), F32), _sds((t, d), BF16), _sds((1, d), F32), _sds((1, d), F32)],
        scratch_shapes=[pltpu.VMEM((tm, d), F32)],
        compiler_params=_cp(("arbitrary", "arbitrary")))(a, w, dres, xh, rs, g)


def _loss_ln_bwd(x2, target, xh, rs, g, tm=256):
    t, d = x2.shape

    def body(x_ref, t_ref, xh_ref, rs_ref, g_ref, loss_ref, dz_ref, dzb_ref, dg_ref, db_ref):
        @pl.when(pl.program_id(0) == 0)
        def _():
            loss_ref[...] = jnp.zeros_like(loss_ref)
            dg_ref[...] = jnp.zeros_like(dg_ref)
            db_ref[...] = jnp.zeros_like(db_ref)

        err = x_ref[...] - t_ref[...]
        per_row = jnp.mean(err * err, axis=-1, keepdims=True)
        loss_ref[...] += 0.5 * jnp.sum(per_row, axis=0, keepdims=True)
        dx = err * (1.0 / d)
        xhv = xh_ref[...]
        dz = _ln_bwd_rows(dx, xhv, rs_ref[...], g_ref[...])
        dz_ref[...] = dz
        dzb_ref[...] = dz.astype(BF16)
        dg_ref[...] += jnp.sum(dx * xhv, axis=0, keepdims=True)
        db_ref[...] += jnp.sum(dx, axis=0, keepdims=True)

    row = pl.BlockSpec((tm, d), lambda i: (i, 0))
    vec = pl.BlockSpec((1, d), lambda i: (0, 0))
    return pl.pallas_call(
        body, name="loss_ln_bwd", grid=(t // tm,),
        in_specs=[row, row, row, pl.BlockSpec((tm, 1), lambda i: (i, 0)), vec],
        out_specs=[pl.BlockSpec((1, LANE), lambda i: (0, 0)), row, row, vec, vec],
        out_shape=[_sds((1, LANE), F32), _sds((t, d), F32), _sds((t, d), BF16), _sds((1, d), F32),
                   _sds((1, d), F32)],
        compiler_params=_cp(("arbitrary",)))(x2, target, xh, rs, g)


HPA_FWD = 8
HPA = 4


STRIP = 16


def _attn_scores(q_ref, k_refs, bias_ref, i, dh, hh):
    cols = pl.ds(hh * dh, dh)
    q = q_ref[:, cols] * (dh ** -0.5)
    k = jnp.concatenate([r[:, cols] for r in k_refs], axis=0)
    s = _dot_nt(q, k)
    before_start = lax.broadcasted_iota(jnp.int32, (STRIP, KW), 1) < (2 - i) * QB
    strips = []
    for r in range(0, QB, STRIP):
        ss = jnp.where(before_start, NEG_INF, s[r:r + STRIP] + bias_ref[hh, r:r + STRIP])
        e = jnp.exp(ss - jnp.max(ss, axis=-1, keepdims=True))
        strips.append(e / jnp.sum(e, axis=-1, keepdims=True))
    return q, k, strips


def _attn_specs(dh, off, hp):
    w = hp * dh
    qcol, kcol, vcol = off["aq"] // w, off["ak"] // w, off["av"] // w
    q_spec = pl.BlockSpec((QB, w), lambda g, i: (i, qcol + g))
    k_specs = [pl.BlockSpec((QB, w), functools.partial(lambda g, i, j: (jnp.maximum(i - 2 + j, 0), kcol + g), j=j))
               for j in range(3)]
    v_specs = [pl.BlockSpec((QB, w), functools.partial(lambda g, i, j: (jnp.maximum(i - 2 + j, 0), vcol + g), j=j))
               for j in range(3)]
    bias_spec = pl.BlockSpec((hp, QB, KW), lambda g, i: (g, 0, 0))
    return q_spec, k_specs, v_specs, bias_spec


def _attn_fwd(p, bias, d, off):
    t = p.shape[0]
    dh = d // ATTN_HEADS

    def body(q_ref, k0, k1, k2, v0, v1, v2, bias_ref, o_ref):
        for hh in range(HPA_FWD):
            cols = pl.ds(hh * dh, dh)
            _, _, strips = _attn_scores(q_ref, (k0, k1, k2), bias_ref, pl.program_id(1), dh, hh)
            pr = jnp.concatenate([_mx(ps) for ps in strips], axis=0)
            v = jnp.concatenate([v0[:, cols], v1[:, cols], v2[:, cols]], axis=0)
            o_ref[:, cols] = _dot(pr, v).astype(o_ref.dtype)

    q_spec, k_specs, v_specs, bias_spec = _attn_specs(dh, off, HPA_FWD)
    return pl.pallas_call(
        body, name="attn_fwd", grid=(ATTN_HEADS // HPA_FWD, t // QB),
        in_specs=[q_spec] + k_specs + v_specs + [bias_spec],
        out_specs=pl.BlockSpec((None, QB, HPA_FWD * dh), lambda g, i: (0, i, g)),
        out_shape=_sds((N_BRANCH, t, d), BF16),
        compiler_params=_cp(("parallel", "parallel")))(p, p, p, p, p, p, p, bias)


def _attn_bwd(p, bias, do, dp, d, off):
    t = p.shape[0]
    dh = d // ATTN_HEADS
    tp = t + 2 * QB

    def body(q_ref, k0, k1, k2, v0, v1, v2, bias_ref, do_ref, dp_in, dq_ref, dk_out, dv_out, dbias_ref, dk_ref,
             dv_ref):
        i = pl.program_id(1)

        @pl.when(i == 0)
        def _():
            dk_ref[...] = jnp.zeros_like(dk_ref)
            dv_ref[...] = jnp.zeros_like(dv_ref)
            dbias_ref[...] = jnp.zeros_like(dbias_ref)

        rows = pl.ds(pl.multiple_of(i * QB, QB), KW)
        for hh in range(HPA):
            cols = pl.ds(hh * dh, dh)
            q, k, strips = _attn_scores(q_ref, (k0, k1, k2), bias_ref, i, dh, hh)
            v = jnp.concatenate([v0[:, cols], v1[:, cols], v2[:, cols]], axis=0)
            dov = do_ref[:, cols]
            dp = _dot_nt(dov, v)
            ds_strips = []
            for n, ps in enumerate(strips):
                r = n * STRIP
                dps = dp[r:r + STRIP]
                dss = ps * (dps - jnp.sum(ps * dps, axis=-1, keepdims=True))
                dbias_ref[hh, r:r + STRIP] += dss
                ds_strips.append(_mx(dss))
            ds = jnp.concatenate(ds_strips, axis=0)
            pr = jnp.concatenate([_mx(ps) for ps in strips], axis=0)
            dq_ref[:, cols] = (_dot(ds, k) * (dh ** -0.5)).astype(dq_ref.dtype)
            dk_ref[rows, cols] += _dot_tn(ds, q)
            dv_ref[rows, cols] += _dot_tn(pr, dov)

        @pl.when(i == t // QB - 1)
        def _():
            mine = pl.ds(pl.multiple_of(pl.program_id(0) * w, w), w)
            pltpu.sync_copy(dk_ref, dk_out.at[:, mine])
            pltpu.sync_copy(dv_ref, dv_out.at[:, mine])

    w = HPA * dh
    q_spec, k_specs, v_specs, bias_spec = _attn_specs(dh, off, HPA)
    qcol = off["aq"] // w
    return pl.pallas_call(
        body, name="attn_bwd", grid=(ATTN_HEADS // HPA, t // QB),
        in_specs=[q_spec] + k_specs + v_specs + [bias_spec,
                                                 pl.BlockSpec((None, QB, w), lambda g, i: (0, i, g)), ANY],
        out_specs=[pl.BlockSpec((QB, w), lambda g, i: (i, qcol + g)), ANY, ANY, bias_spec],
        out_shape=[_sds(dp.shape, BF16), _sds((tp, d), F32), _sds((tp, d), F32),
                   _sds((ATTN_HEADS, QB, KW), F32)],
        scratch_shapes=[pltpu.VMEM((tp, w), F32), pltpu.VMEM((tp, w), F32)], input_output_aliases={9: 0},
        compiler_params=_cp(("parallel", "arbitrary")))(p, p, p, p, p, p, p, bias, do, dp)


def _onehot_mm(name, a, b):
    def body(a_ref, b_ref, o_ref):
        o_ref[...] = _dot_hi(a_ref[...], b_ref[...])

    return pl.pallas_call(body, name=name, out_shape=_sds((a.shape[0], b.shape[1]), F32),
                          compiler_params=pltpu.CompilerParams(vmem_limit_bytes=VMEM_LIMIT))(a, b)


def _diag_index():
    ii, jj = np.arange(CHUNK)[:, None], np.arange(CHUNK)[None, :]
    return (ii - jj + CHUNK - 1).reshape(-1)


def _bias_expand(rel_bias):
    h = rel_bias.shape[0]
    nq, nk, shift = QB // CHUNK, KW // CHUNK, (2 * QB) // CHUNK
    nbin, ndc = 3 * LANE, 4
    rb = jnp.pad(rel_bias, ((0, 0), (0, nbin - rel_bias.shape[1])))
    win = np.clip(CHUNK * np.arange(ndc)[:, None] + np.arange(LANE)[None, :] - (CHUNK - 1), -REL_CLIP, REL_CLIP)
    sel = (jnp.arange(nbin)[:, None] == jnp.asarray((win + REL_CLIP).reshape(1, -1))).astype(F32)
    windows = _onehot_mm("bias_windows", rb, sel)
    diag_t = (jnp.arange(LANE)[:, None] == jnp.asarray(_diag_index().reshape(1, -1))).astype(F32)
    blocks = _onehot_mm("bias_blocks", windows.reshape(h * ndc, LANE), diag_t).reshape(h, ndc, CHUNK, CHUNK)
    off_band = jnp.full((h, CHUNK, CHUNK), NEG_INF, F32)
    dcs = [shift + nq - 1 - u for u in range(nk + nq - 1)]
    strip = jnp.concatenate([blocks[:, min(dc, ndc - 1)] if 0 <= dc <= ATTN_LEFT else off_band for dc in dcs], axis=2)
    return jnp.concatenate([strip[:, :, (nq - 1 - ic) * CHUNK:(nq - 1 - ic + nk) * CHUNK] for ic in range(nq)],
                           axis=1)


def _bias_reduce(dbias):
    h = dbias.shape[0]
    nq, nk = QB // CHUNK, KW // CHUNK
    nbin = 3 * LANE
    blocks = dbias.reshape(h, nq, CHUNK, nk, CHUNK).transpose(0, 1, 3, 2, 4).reshape(h * nq * nk, CHUNK * CHUNK)
    diag = (jnp.asarray(_diag_index().reshape(-1, 1)) == jnp.arange(LANE)[None, :]).astype(F32)
    ic = np.arange(nq)[:, None, None]
    jc = np.arange(nk)[None, :, None]
    dl = np.arange(LANE)[None, None, :] - (CHUNK - 1)
    rel = np.clip(CHUNK * (ic - jc + (2 * QB) // CHUNK) + dl, -REL_CLIP, REL_CLIP) + REL_CLIP
    bins = (jnp.asarray(rel.reshape(-1, 1)) == jnp.arange(nbin)[None, :]).astype(F32)

    diags = _onehot_mm("bias_diag_sums", blocks, diag)
    out = _onehot_mm("bias_bin_sums", diags.reshape(h, nq * nk * LANE), bins)
    return out[:, :2 * REL_CLIP + 1]


def _chunk_masks():
    r = lax.broadcasted_iota(jnp.int32, (LB, LB), 0)
    c = lax.broadcasted_iota(jnp.int32, (LB, LB), 1)
    return (r // CHUNK) == (c // CHUNK), r >= c, r <= c


def _chunks(a):
    return [a[c * CHUNK:(c + 1) * CHUNK] for c in range(LB // CHUNK)]


def _per_chunk(a, f):
    return jnp.concatenate([jnp.broadcast_to(f(c), c.shape) for c in _chunks(a)], axis=0)


def _dot_sel(sel, x):
    def top(v):
        return lax.bitcast_convert_type(lax.bitcast_convert_type(v, jnp.int32) & jnp.int32(-65536), F32)

    hi = top(x)
    mid = top(x - hi)
    lo = (x - hi) - mid
    d = functools.partial(jnp.dot, sel.astype(jnp.bfloat16), preferred_element_type=F32)
    return d(hi.astype(jnp.bfloat16)) + d(mid.astype(jnp.bfloat16)) + d(lo.astype(jnp.bfloat16))


def _lin_block(gla, q, k, v, aux):
    dk = q.shape[-1]
    same, low, up = _chunk_masks()
    ones = same.astype(F32)
    if gla:
        glr, wlr, blr = aux
        q = q * (dk ** -0.5)
        pre = _dot(glr, wlr) + blr
        log_a = (jnp.minimum(pre, 0.0) - jnp.log(1.0 + jnp.exp(-jnp.abs(pre)))) / GATE_NORM
        b = _dot_sel(jnp.where(low, ones, 0.0), log_a)
        lastb = _per_chunk(b, lambda c: c[CHUNK - 1:])
    else:
        cs, sn, lg = aux
        pre = None
        half = dk // 2
        q = q * cs + pltpu.roll(q, half, 1) * sn
        k = (k * cs + pltpu.roll(k, half, 1) * sn) * (dk ** -0.5)
        pos = (lax.broadcasted_iota(jnp.int32, (LB, dk), 0) % CHUNK).astype(F32) + 1.0
        b = pos * lg
        lastb = jnp.broadcast_to(float(CHUNK) * lg, b.shape)
    eb, enb, el, dec = jnp.exp(b), jnp.exp(-b), jnp.exp(lastb - b), jnp.exp(lastb)
    qf, kf, qb, kb, kl = q * eb, k * enb, q * enb, k * eb, k * el
    s = jnp.where(same, jnp.where(low, _dot_nt(qf, kf), _dot_nt(qb, kb)), 0.0)
    return dict(pre=pre, eb=eb, enb=enb, el=el, dec=dec, qf=qf, kf=kf, qb=qb, kb=kb, kl=kl, s=s,
                same=same, low=low, up=up, ones=ones)


def _lin_norm_gate(gla, o, gate, gn):
    sg = _sigmoid(gate)
    silu = gate * sg
    if gla:
        r = lax.rsqrt(jnp.mean(o * o, axis=-1, keepdims=True) + LN_EPS)
        hn = o * r
        return silu * (hn * gn), (sg, silu, r, hn)
    mu = jnp.mean(o, axis=-1, keepdims=True)
    oc = o - mu
    r = lax.rsqrt(jnp.mean(oc * oc, axis=-1, keepdims=True) + LN_EPS)
    hn = oc * r
    return silu * hn, (sg, silu, r, hn)


HPS = 4


def _lin_specs(gla, dk, dv, off, rev, nb):
    pre = "g" if gla else "r"
    wk, wv = HPS * dk, HPS * dv
    qc, kc, vc, gc = (off[pre + "q"] // wk, off[pre + "k"] // wk, off[pre + "v"] // wv, off[pre + "g"] // wv)

    def blk(i):
        return nb - 1 - i if rev else i

    specs = [pl.BlockSpec((LB, wk), lambda g, i: (blk(i), qc + g)),
             pl.BlockSpec((LB, wk), lambda g, i: (blk(i), kc + g)),
             pl.BlockSpec((LB, wv), lambda g, i: (blk(i), vc + g)),
             pl.BlockSpec((LB, wv), lambda g, i: (blk(i), gc + g))]
    if gla:
        specs += [pl.BlockSpec((LB, LANE), lambda g, i: (blk(i), off["glr"] // LANE)),
                  pl.BlockSpec((LANE, wk), lambda g, i: (0, g)),
                  pl.BlockSpec((1, wk), lambda g, i: (0, g)),
                  pl.BlockSpec((1, dv), lambda g, i: (0, 0))]
    else:
        specs += [pl.BlockSpec((LB, dk), lambda g, i: (blk(i), 0)),
                  pl.BlockSpec((LB, dk), lambda g, i: (blk(i), 0)),
                  pl.BlockSpec((HPS, 1, dk), lambda g, i: (g, 0, 0))]
    return specs, blk


def _lin_aux(gla, refs, rows, hh, dk):
    if gla:
        glr_ref, wlr_ref, blr_ref, gn_ref = refs
        kcols = pl.ds(hh * dk, dk)
        return (glr_ref[rows, :], wlr_ref[:, kcols], blr_ref[:, kcols]), gn_ref[...]
    cs_ref, sn_ref, lg_ref = refs
    return (cs_ref[rows, :], sn_ref[rows, :], lg_ref[hh]), None


def _lin_fwd(gla, p, aux_arrays, d, off, branches, slot):
    t = p.shape[0]
    dk, dv = d // (2 * LIN_HEADS), d // LIN_HEADS
    nb, cb = t // LB, LB // CHUNK
    naux = len(aux_arrays)

    def body(*refs):
        q_ref, k_ref, v_ref, g_ref = refs[:4]
        aux_refs = refs[4:4 + naux]
        o_ref, bo_ref, st_out_ref, st_ref = refs[5 + naux:]

        @pl.when(pl.program_id(1) == 0)
        def _():
            st_ref[...] = jnp.zeros_like(st_ref)

        rows = slice(None)
        for hh in range(HPS):
            kcols, vcols = pl.ds(hh * dk, dk), pl.ds(hh * dv, dv)
            aux, gn = _lin_aux(gla, aux_refs, rows, hh, dk)
            v = v_ref[:, vcols]
            blk = _lin_block(gla, q_ref[:, kcols], k_ref[:, kcols], v, aux)
            st = st_ref[hh]
            inter = []
            for c, (qf, kl, dec, vc) in enumerate(zip(_chunks(blk["qf"]), _chunks(blk["kl"]), _chunks(blk["dec"]),
                                                      _chunks(v))):
                st_out_ref[hh, c] = st
                inter.append(_dot_nt(qf, st))
                st = st * dec[:1] + _dot_tn(vc, kl)
            st_ref[hh] = st
            o = _dot(blk["s"], v) + jnp.concatenate(inter, axis=0)
            o_ref[:, vcols] = o
            out, _ = _lin_norm_gate(gla, o, g_ref[:, vcols], gn)
            bo_ref[:, vcols] = out.astype(BF16)

    specs, _ = _lin_specs(gla, dk, dv, off, False, nb)
    orow = pl.BlockSpec((LB, HPS * dv), lambda g, i: (i, g))
    return pl.pallas_call(
        body, name="gla_fwd" if gla else "ret_fwd", grid=(LIN_HEADS // HPS, nb), in_specs=specs + [ANY],
        out_specs=[orow, pl.BlockSpec((None, LB, HPS * dv), lambda g, i: (slot, i, g)),
                   pl.BlockSpec((HPS, cb, dv, dk), lambda g, i: (g, i, 0, 0))],
        out_shape=[_sds((t, d), F32), _sds(branches.shape, BF16), _sds((LIN_HEADS, t // CHUNK, dv, dk), F32)],
        scratch_shapes=[pltpu.VMEM((HPS, dv, dk), F32)], input_output_aliases={4 + naux: 1},
        compiler_params=_cp(("parallel", "arbitrary")))(p, p, p, p, *aux_arrays, branches)


def _lin_bwd(gla, p, aux_arrays, o, states, dbo, slot, dp, d, off):
    assert HPS == LIN_HEADS
    t = p.shape[0]
    dk, dv = d // (2 * LIN_HEADS), d // LIN_HEADS
    nb, cb = t // LB, LB // CHUNK
    naux = len(aux_arrays)

    def body(*refs):
        q_ref, k_ref, v_ref, g_ref = refs[:4]
        aux_refs = refs[4:4 + naux]
        o_ref, st_in_ref, dbo_ref = refs[4 + naux:7 + naux]
        outs = refs[8 + naux:]
        dq_ref, dk_ref = outs[0].at[:, pl.ds(0, d // 2)], outs[0].at[:, pl.ds(d // 2, d // 2)]
        dv_ref, dg_ref = outs[0].at[:, pl.ds(d, d)], outs[0].at[:, pl.ds(2 * d, d)]
        dst_ref = outs[-1]
        first = pl.program_id(1) == 0

        @pl.when(first)
        def _():
            dst_ref[...] = jnp.zeros_like(dst_ref)

        if gla:
            dpre_ref, dblr_ref, dgn_ref = outs[1:4]

            @pl.when(first)
            def _():
                dblr_ref[...] = jnp.zeros_like(dblr_ref)
                dgn_ref[...] = jnp.zeros_like(dgn_ref)

        rows = slice(None)
        for hh in range(HPS):
            kcols, vcols = pl.ds(hh * dk, dk), pl.ds(hh * dv, dv)
            aux, gn = _lin_aux(gla, aux_refs, rows, hh, dk)
            v = v_ref[:, vcols]
            bk = _lin_block(gla, q_ref[:, kcols], k_ref[:, kcols], v, aux)
            eb, enb, el, dec = bk["eb"], bk["enb"], bk["el"], bk["dec"]
            qf, kf, qb, kb, kl, s = bk["qf"], bk["kf"], bk["qb"], bk["kb"], bk["kl"], bk["s"]
            gate = g_ref[:, vcols]
            dout = dbo_ref[:, vcols]
            _, (sg, silu, r, hn) = _lin_norm_gate(gla, o_ref[:, vcols], gate, gn)
            dsilu = sg * (1.0 + gate * (1.0 - sg))
            if gla:
                y = hn * gn
                dy = dout * silu
                dg_ref[:, vcols] = (dout * y * dsilu).astype(BF16)
                dgn_ref[hh] += jnp.sum(dy * hn, axis=0, keepdims=True)
                dhn = dy * gn
                do = r * (dhn - hn * jnp.mean(dhn * hn, axis=-1, keepdims=True))
            else:
                dhn = dout * silu
                dg_ref[:, vcols] = (dout * hn * dsilu).astype(BF16)
                do = r * (dhn - jnp.mean(dhn, axis=-1, keepdims=True)
                          - hn * jnp.mean(dhn * hn, axis=-1, keepdims=True))
            ds = jnp.where(bk["same"], _dot_nt(do, v), 0.0)
            dsf = jnp.where(bk["low"], ds, 0.0)
            dsb = ds - dsf
            dvv = _dot_tn(s, do)
            dqf = _dot(dsf, kf)
            dkf = _dot_tn(dsf, qf)
            dqb = _dot(dsb, kb)
            dkb = _dot_tn(dsb, qb)
            dst = dst_ref[hh]
            dv_st, dqf_st, dkl_c, ddec_c = [], [], [], []
            parts = zip(reversed(range(cb)), reversed(_chunks(do)), reversed(_chunks(v)), reversed(_chunks(qf)),
                        reversed(_chunks(kl)), reversed(_chunks(dec)))
            for c, do_c, v_c, qf_c, kl_c, dec_c in parts:
                st = st_in_ref[hh, c]
                dv_st.append(_dot_nt(kl_c, dst))
                dkl_c.append(_dot(v_c, dst))
                dqf_st.append(_dot(do_c, st))
                ddec_c.append(jnp.broadcast_to(jnp.sum(dst * st, axis=0, keepdims=True), (CHUNK, dk)))
                dst = dst * dec_c[:1] + _dot_tn(do_c, qf_c)
            dst_ref[hh] = dst

            def cat(pieces):
                return jnp.concatenate(pieces[::-1], axis=0)

            dvv = dvv + cat(dv_st)
            dqf = dqf + cat(dqf_st)
            dkl = cat(dkl_c)
            dq = dqf * eb + dqb * enb
            dkk = dkf * enb + dkb * eb + dkl * el
            dv_ref[:, vcols] = dvv.astype(BF16)
            if gla:
                db = dqf * qf - dkf * kf - dqb * qb + dkb * kb - dkl * kl
                dlast = _per_chunk(dkl * kl, lambda c: jnp.sum(c, axis=0, keepdims=True)) + cat(ddec_c) * dec
                dlog_a = _dot_sel(jnp.where(bk["up"], bk["ones"], 0.0), db) + dlast
                dpre = dlog_a * (1.0 / GATE_NORM) * (1.0 - _sigmoid(bk["pre"]))
                dpre_ref[:, kcols] = dpre
                dblr_ref[hh] += jnp.sum(dpre, axis=0, keepdims=True)
                dq_ref[:, kcols] = (dq * (dk ** -0.5)).astype(BF16)
                dk_ref[:, kcols] = dkk.astype(BF16)
            else:
                cs, sn, _ = aux
                half = dk // 2
                dkk = dkk * (dk ** -0.5)
                dq_ref[:, kcols] = (dq * cs + pltpu.roll(dq * sn, half, 1)).astype(BF16)
                dk_ref[:, kcols] = (dkk * cs + pltpu.roll(dkk * sn, half, 1)).astype(BF16)

    specs, blk = _lin_specs(gla, dk, dv, off, True, nb)
    vrow = pl.BlockSpec((LB, HPS * dv), lambda g, i: (blk(i), g))
    krow = pl.BlockSpec((LB, HPS * dk), lambda g, i: (blk(i), g))
    specs += [vrow, pl.BlockSpec((HPS, cb, dv, dk), lambda g, i: (g, blk(i), 0, 0)),
              pl.BlockSpec((None, LB, HPS * dv), lambda g, i: (slot, blk(i), g)), ANY]
    section = off[("g" if gla else "r") + "q"] // (3 * d)
    out_specs = [pl.BlockSpec((LB, 3 * d), lambda g, i: (blk(i), section))]
    out_shape = [_sds(dp.shape, BF16)]
    if gla:
        out_specs += [krow, pl.BlockSpec((HPS, 1, dk), lambda g, i: (g, 0, 0)),
                      pl.BlockSpec((HPS, 1, dv), lambda g, i: (g, 0, 0))]
        out_shape += [_sds((t, d // 2), F32), _sds((LIN_HEADS, 1, dk), F32), _sds((LIN_HEADS, 1, dv), F32)]
    out_specs.append(pl.BlockSpec((HPS, dv, dk), lambda g, i: (g, 0, 0)))
    out_shape.append(_sds((LIN_HEADS, dv, dk), F32))
    res = pl.pallas_call(
        body, name="gla_bwd" if gla else "ret_bwd", grid=(LIN_HEADS // HPS, nb), in_specs=specs,
        out_specs=out_specs, out_shape=out_shape, input_output_aliases={7 + naux: 0},
        compiler_params=_cp(("parallel", "arbitrary")))(p, p, p, p, *aux_arrays, o, states, dbo, dp)
    return res[:-1]


def _row_tile(rows, cols):
    cap = max(8, (2 << 20) // (4 * cols))
    t = rows
    while t > cap and t % 2 == 0:
        t //= 2
    return t


def _add_half(name, g, t, sel):
    nchip, hr, cols = t.shape
    tr = _row_tile(hr, cols)
    nb = hr // tr

    def body(sel_ref, g_ref, t_ref, o_ref):
        o_ref[...] = g_ref[...] + t_ref[...]

    half = pl.BlockSpec((None, tr, cols), lambda p, i, s: (p, i, 0))
    gs = pltpu.PrefetchScalarGridSpec(
        num_scalar_prefetch=1, grid=(nchip, nb),
        in_specs=[pl.BlockSpec((None, tr, cols), lambda p, i, s: (p, s[0] * nb + i, 0)), half], out_specs=half)
    return pl.pallas_call(body, name=name, grid_spec=gs, out_shape=_sds(t.shape, F32),
                          compiler_params=_cp(("parallel", "parallel")))(sel, g, t)


def _sum_shards(name, h, rcv, sel):
    _, rows, cols = h.shape
    tr = _row_tile(rows, cols)

    def body(sel_ref, h_ref, r0, r1, r2, o_ref):
        o_ref[...] = ((h_ref[...] + r0[...]) + r1[...]) + r2[...]

    rspecs = [pl.BlockSpec((None, tr, cols), functools.partial(lambda i, s, j: (j, i, 0), j=j)) for j in range(3)]
    gs = pltpu.PrefetchScalarGridSpec(
        num_scalar_prefetch=1, grid=(rows // tr,),
        in_specs=[pl.BlockSpec((None, tr, cols), lambda i, s: (s[0], i, 0))] + rspecs,
        out_specs=pl.BlockSpec((tr, cols), lambda i, s: (i, 0)))
    return pl.pallas_call(body, name=name, grid_spec=gs, out_shape=_sds((rows, cols), F32),
                          compiler_params=_cp(("parallel",)))(sel, h, rcv, rcv, rcv)


def _adamw_math(w, g, m, v):
    c1 = 1.0 - ADAM_B1 ** ADAM_STEP
    c2 = 1.0 - ADAM_B2 ** ADAM_STEP
    nm = ADAM_B1 * m + (1.0 - ADAM_B1) * g
    nv = ADAM_B2 * v + (1.0 - ADAM_B2) * jnp.square(g)
    return -ADAM_LR * ((nm / c1) / (jnp.sqrt(nv / c2) + ADAM_EPS) + ADAM_WD * w), nm, nv


def _adamw(name, w, g, m, v):
    rows, cols = w.shape
    tr = _row_tile(rows, cols)

    def body(w_ref, g_ref, m_ref, v_ref, d_ref, nm_ref, nv_ref):
        d_ref[...], nm_ref[...], nv_ref[...] = _adamw_math(w_ref[...], g_ref[...], m_ref[...], v_ref[...])

    spec = pl.BlockSpec((tr, cols), lambda i: (i, 0))
    return pl.pallas_call(body, name=name, grid=(rows // tr,), in_specs=[spec] * 4, out_specs=[spec] * 3,
                          out_shape=[_sds((rows, cols), F32)] * 3, compiler_params=_cp(("parallel",)))(w, g, m, v)


def _adamw_layer(name, w, g_own, g_sib, sel, m, v, layer, prev):
    depth, rows, cols = w.shape
    tr = _row_tile(rows // 2, cols)
    nbh = rows // 2 // tr
    nprev = 0 if prev is None else 4

    def body(sel_ref, w_ref, own_ref, sib_ref, m_ref, v_ref, *rest):
        go_ref, d_ref, nm_ref, nv_ref = rest[nprev:]
        gv = jnp.where(pl.program_id(0) // nbh == sel_ref[0], own_ref[...], sib_ref[...])
        go_ref[...] = gv
        d_ref[...], nm_ref[...], nv_ref[...] = _adamw_math(w_ref[...], gv, m_ref[...], v_ref[...])

    lay = pl.BlockSpec((None, tr, cols), lambda i, s: (layer, i, 0))
    hlf = pl.BlockSpec((tr, cols), lambda i, s: (i % nbh, 0))
    gs = pltpu.PrefetchScalarGridSpec(
        num_scalar_prefetch=1, grid=(2 * nbh,), in_specs=[lay, hlf, hlf, lay, lay] + [ANY] * nprev,
        out_specs=[lay] * 4)
    args = (sel, w, g_own, g_sib, m, v) + (() if prev is None else tuple(prev))
    return pl.pallas_call(
        body, name=name, grid_spec=gs, out_shape=[_sds((depth, rows, cols), F32)] * 4,
        input_output_aliases={6 + k: k for k in range(nprev)},
        compiler_params=_cp(("parallel",)))(*args)


def _adamw_colmajor(name, wt, mt, vt, halves, sel):
    c_dim, depth, r_dim = wt.shape
    hr = r_dim // 2

    def body(sel_ref, w_ref, m_ref, v_ref, *rest):
        g_refs, (go_ref, d_ref, nm_ref, nv_ref) = rest[:2 * depth], rest[2 * depth:]
        own_first = sel_ref[0] == 0
        for l in range(depth):
            own, sib = g_refs[2 * l][...], g_refs[2 * l + 1][...]
            g = jnp.concatenate([jnp.where(own_first, own, sib), jnp.where(own_first, sib, own)], axis=0).T
            go_ref[:, l, :] = g
            d_ref[:, l, :], nm_ref[:, l, :], nv_ref[:, l, :] = _adamw_math(w_ref[:, l, :], g, m_ref[:, l, :],
                                                                          v_ref[:, l, :])

    col = pl.BlockSpec((LANE, depth, r_dim), lambda j, s: (j, 0, 0))
    gs = pltpu.PrefetchScalarGridSpec(
        num_scalar_prefetch=1, grid=(c_dim // LANE,),
        in_specs=[col] * 3 + [pl.BlockSpec((hr, LANE), lambda j, s: (0, j))] * (2 * depth), out_specs=[col] * 4)
    flat = [h for pair in halves for h in pair]
    return pl.pallas_call(body, name=name, grid_spec=gs, out_shape=[_sds(wt.shape, F32)] * 4,
                          compiler_params=_cp(("parallel",)))(sel, wt, mt, vt, *flat)


def _adamw_tail(name, wt, mt, vt, gt_tail, prev):
    c_dim, depth, r_dim = wt.shape
    nt = gt_tail.shape[0]

    def body(w_ref, m_ref, v_ref, g_ref, *rest):
        go_ref, d_ref, nm_ref, nv_ref = rest[4:]
        g = g_ref[...]
        go_ref[...] = g
        d_ref[...], nm_ref[...], nv_ref[...] = _adamw_math(w_ref[...], g, m_ref[...], v_ref[...])

    tail = pl.BlockSpec((nt, depth, r_dim), lambda i: (c_dim // nt - 1, 0, 0))
    return pl.pallas_call(
        body, name=name, grid=(1,), in_specs=[tail] * 3 + [pl.BlockSpec((nt, depth, r_dim), lambda i: (0, 0, 0))]
        + [ANY] * 4, out_specs=[tail] * 4, out_shape=[_sds(wt.shape, F32)] * 4,
        input_output_aliases={4 + k: k for k in range(4)},
        compiler_params=_cp(("arbitrary",)))(wt, mt, vt, gt_tail, *prev)


def _place():
    x, y, c = (lax.axis_index(a) for a in MESH_AXES)
    chips = [(1 - x, y), (x, 1 - y), (1 - x, 1 - y)]
    return x, y, c, chips


def _chip_index(xy):
    return 2 * xy[0] + xy[1]


ANY = pl.BlockSpec(memory_space=pl.ANY)


HBM_SPEC = pl.BlockSpec(memory_space=pltpu.HBM)
SEM = pl.BlockSpec(memory_space=pltpu.SEMAPHORE)
EFFECT = pltpu.SideEffectType.DATAFLOW_SIDE_EFFECTING


def _half(ref, c):
    hr = ref.shape[-2] // 2
    return pl.ds(pl.multiple_of(c * hr, 16), hr)


def _gather_copies(srcs, lands, send, recv):
    x, y, c, chips = _place()
    me = _chip_index((x, y))
    return [pltpu.make_async_remote_copy(src_ref=s.at[_half(s, c)], dst_ref=g.at[me, _half(s, c)],
                                         send_sem=send.at[3 * a + j], recv_sem=recv.at[3 * a + j],
                                         device_id=(*ch, c), device_id_type=DEV)
            for a, (s, g) in enumerate(zip(srcs, lands)) for j, ch in enumerate(chips)]


def _scatter_copies(srcs, lands, send, recv):
    x, y, c, chips = _place()
    return [pltpu.make_async_remote_copy(src_ref=h.at[_chip_index(ch)], dst_ref=r.at[j],
                                         send_sem=send.at[3 * a + j], recv_sem=recv.at[3 * a + j],
                                         device_id=(*ch, c), device_id_type=DEV)
            for a, (h, r) in enumerate(zip(srcs, lands)) for j, ch in enumerate(chips)]


def _in_hbm(a):
    return pltpu.with_memory_space_constraint(a, pltpu.HBM)


def _split_start(name, srcs, land_shapes, copies_fn, after=None, per_src=3):
    ns, nl = len(srcs), len(land_shapes)
    ncp = per_src * ns
    lands = [lax.empty(s.shape, s.dtype) for s in land_shapes]
    behind = [] if after is None else [after]

    def body(*refs):
        src, land = refs[:ns], refs[ns:ns + nl]
        send, recv = refs[ns + nl + len(behind)], refs[ns + nl + len(behind) + 1]
        for cp in copies_fn(src, land, send, recv):
            cp.start()
        refs[-1][...] = jnp.zeros_like(refs[-1])

    bufs = list(srcs) + lands
    outs = pl.pallas_call(
        body, name=name, in_specs=[HBM_SPEC] * (ns + nl) + [ANY] * len(behind),
        out_specs=[SEM, SEM] + [HBM_SPEC] * (ns + nl) + [pl.BlockSpec(memory_space=pltpu.VMEM)],
        out_shape=[pltpu.SemaphoreType.DMA((ncp,)), pltpu.SemaphoreType.DMA((ncp,))]
        + [pltpu.HBM(b.shape, b.dtype) for b in bufs] + [_sds((8, LANE), F32)],
        input_output_aliases={i: 2 + i for i in range(ns + nl)},
        compiler_params=pltpu.CompilerParams(has_side_effects=EFFECT))(*[_in_hbm(b) for b in bufs], *behind)
    return outs[0], outs[1], list(outs[2:2 + ns]), list(outs[2 + ns:2 + ns + nl]), outs[-1]


def _split_wait(name, started, copies_fn, after):
    send, recv, srcs, lands, _ = started
    ns, nl = len(srcs), len(lands)

    def body(*refs):
        src, land = refs[:ns], refs[ns:ns + nl]
        for cp in copies_fn(src, land, refs[ns + nl], refs[ns + nl + 1]):
            cp.wait_send()
            cp.wait_recv()

    bufs = list(srcs) + list(lands)
    outs = pl.pallas_call(
        body, name=name, in_specs=[HBM_SPEC] * (ns + nl) + [SEM, SEM, ANY], out_specs=[HBM_SPEC] * (ns + nl),
        out_shape=[pltpu.HBM(b.shape, b.dtype) for b in bufs],
        input_output_aliases={i: i for i in range(ns + nl)},
        compiler_params=pltpu.CompilerParams(has_side_effects=EFFECT))(*bufs, send, recv, after)
    return list(outs[:ns]), list(outs[ns:])


def _gather_plain(name, srcs):
    n = len(srcs)

    def body(*refs):
        src, land = refs[:n], refs[n:2 * n]
        send, recv, fsend, frecv = refs[2 * n:]
        first = _gather_copies(src, land, send, recv)
        for cp in first:
            cp.start()
        _forward_body(land, first, fsend, frecv)

    return pl.pallas_call(
        body, name=name, in_specs=[ANY] * n, out_specs=[ANY] * n,
        out_shape=[_sds((4,) + s.shape, s.dtype) for s in srcs],
        scratch_shapes=[pltpu.SemaphoreType.DMA((3 * n,))] * 4)(*srcs)


def _forward_body(land, arrivals, fsend, frecv):
    x, y, c, chips = _place()
    n = len(land)
    passed = []
    for a in range(n):
        for j, ch in enumerate(chips):
            if arrivals is not None:
                arrivals[3 * a + j].wait_recv()
            slot = land[a].at[_chip_index(ch), _half(land[a], c)]
            fw = pltpu.make_async_remote_copy(src_ref=slot, dst_ref=slot, send_sem=fsend.at[3 * a + j],
                                              recv_sem=frecv.at[3 * a + j], device_id=(x, y, 1 - c),
                                              device_id_type=DEV)
            fw.start()
            passed.append(fw)
    for a in range(n):
        for j, ch in enumerate(chips):
            slot = land[a].at[_chip_index(ch), _half(land[a], 1 - c)]
            pltpu.make_async_remote_copy(src_ref=slot, dst_ref=slot, send_sem=fsend.at[3 * a + j],
                                         recv_sem=frecv.at[3 * a + j], device_id=(x, y, c),
                                         device_id_type=DEV).wait_recv()
    for cp in passed:
        cp.wait_send()
    if arrivals is not None:
        for cp in arrivals:
            cp.wait_send()


def _gather_forward(name, lands):
    n = len(lands)

    def body(*refs):
        _forward_body(refs[n:2 * n], None, refs[2 * n], refs[2 * n + 1])

    return pl.pallas_call(
        body, name=name, in_specs=[ANY] * n, out_specs=[ANY] * n,
        out_shape=[_sds(g.shape, g.dtype) for g in lands], input_output_aliases={a: a for a in range(n)},
        scratch_shapes=[pltpu.SemaphoreType.DMA((3 * n,))] * 2)(*lands)


def _sibling_copies(srcs, lands, send, recv):
    x, y, c, _ = _place()
    return [pltpu.make_async_remote_copy(src_ref=g.at[:, _half(g, 1 - c)], dst_ref=t, send_sem=send.at[a],
                                         recv_sem=recv.at[a], device_id=(x, y, 1 - c), device_id_type=DEV)
            for a, (g, t) in enumerate(zip(srcs, lands))]


def _sibling_share(name, sms):
    n = len(sms)

    def body(*refs):
        ins, outs = refs[:n], refs[n:2 * n]
        send, recv = refs[2 * n:]
        x, y, c, _ = _place()
        cps = [pltpu.make_async_remote_copy(src_ref=ins[a], dst_ref=outs[a], send_sem=send.at[a],
                                            recv_sem=recv.at[a], device_id=(x, y, 1 - c), device_id_type=DEV)
               for a in range(n)]
        for cp in cps:
            cp.start()
        for cp in cps:
            cp.wait()

    return pl.pallas_call(
        body, name=name, in_specs=[ANY] * n, out_specs=[ANY] * n, out_shape=[_sds(s.shape, F32) for s in sms],
        scratch_shapes=[pltpu.SemaphoreType.DMA((n,))] * 2)(*sms)


def _small_allreduce(v, after=None):
    rows = v.shape[0]
    ndev = 8
    behind = [] if after is None else [after]

    def body(v_ref, *rest):
        o_ref, gat_ref, send, recv = rest[len(behind):]
        x, y, c, _ = _place()
        me = 4 * x + 2 * y + c
        cps = []
        for k in range(1, ndev):
            to = (me + k) % ndev
            cp = pltpu.make_async_remote_copy(src_ref=v_ref, dst_ref=gat_ref.at[me], send_sem=send.at[k - 1],
                                              recv_sem=recv.at[me], device_id=(to // 4, (to // 2) % 2, to % 2),
                                              device_id_type=DEV)
            cp.start()
            cps.append(cp)
        gat_ref[me] = v_ref[...]
        for k in range(1, ndev):
            frm = (me + k) % ndev
            pltpu.make_async_remote_copy(src_ref=v_ref, dst_ref=gat_ref.at[frm], send_sem=send.at[k - 1],
                                         recv_sem=recv.at[frm], device_id=(x, y, c), device_id_type=DEV).wait_recv()
        for cp in cps:
            cp.wait_send()
        acc = gat_ref[0]
        for k in range(1, ndev):
            acc = acc + gat_ref[k]
        o_ref[...] = acc

    vm = pl.BlockSpec(memory_space=pltpu.VMEM)
    return pl.pallas_call(
        body, name="small_allreduce", in_specs=[vm] + [ANY] * len(behind), out_specs=vm,
        out_shape=_sds((rows, LANE), F32),
        scratch_shapes=[pltpu.VMEM((ndev, rows, LANE), F32), pltpu.SemaphoreType.DMA((ndev - 1,)),
                        pltpu.SemaphoreType.DMA((ndev,))])(v, *behind)


def _layout(d):
    half = d // 2
    names = [("aq", d), ("ak", d), ("av", d), ("rq", half), ("rk", half), ("rv", d), ("rg", d),
             ("gq", half), ("gk", half), ("gv", d), ("gg", d), ("gates", 3 * d), ("glr", 2 * LANE)]
    off, pos = {}, 0
    for nm, sz in names:
        off[nm] = pos
        pos += sz
    return off, pos


def _chip_shards_of_cols(g, d):
    a = 8 * d + d
    per = (a + GATE_RANK + 3 * d) // 4
    sections = ((0, a, 0), (a, a + GATE_RANK, 3 * d), (a + GATE_RANK, 4 * per, -GATE_RANK))

    def quarter(lo, hi):
        cuts = [(max(lo, x), min(hi, y), s) for x, y, s in sections]
        return jnp.concatenate([g[:, x + s:y + s] for x, y, s in cuts if x < y], axis=1)

    return jnp.stack([quarter(c * per, (c + 1) * per) for c in range(4)])


def kernel(x, ln_in_g, ln_in_b, w_in, rel_bias, gla_w_lr, gla_b_lr, gla_norm_g, w_branch, w_out, ln1_g, ln1_b, w_up, w_down, ln2_g, ln2_b, loss_target, m_ln_in_g, m_ln_in_b, m_w_in, m_rel_bias, m_gla_w_lr, m_gla_b_lr, m_gla_norm_g, m_w_branch, m_w_out, m_ln1_g, m_ln1_b, m_w_up, m_w_down, m_ln2_g, m_ln2_b, v_ln_in_g, v_ln_in_b, v_w_in, v_rel_bias, v_gla_w_lr, v_gla_b_lr, v_gla_norm_g, v_w_branch, v_w_out, v_ln1_g, v_ln1_b, v_w_up, v_w_down, v_ln2_g, v_ln2_b):
    t, d = x.shape[1], x.shape[2]
    dff = 4 * d
    half = d // 2
    off, npad = _layout(d)
    xi, yi, ci = (lax.axis_index(a) for a in MESH_AXES)
    chip = 2 * xi + yi
    csel = jnp.reshape(ci, (1,)).astype(jnp.int32)
    psel = jnp.reshape(chip, (1,)).astype(jnp.int32)

    big_w = [w_in, w_branch.reshape(DEPTH, -1, d), w_out, w_up, w_down]
    big_m = [m_w_in, m_w_branch.reshape(DEPTH, -1, d), m_w_out, m_w_up, m_w_down]
    big_v = [v_w_in, v_w_branch.reshape(DEPTH, -1, d), v_w_out, v_w_up, v_w_down]
    W_IN, REST = [0], [1, 2, 3, 4]

    def shards_of(l, idx):
        return [big_w[i][l].astype(BF16) for i in idx]

    def lands_of(srcs):
        return [_sds((4,) + s.shape, s.dtype) for s in srcs]

    def full_w_in(g):
        per = g.shape[2]
        a = 8 * d + d

        def run(lo, hi):
            cuts = [(max(lo, c * per), min(hi, (c + 1) * per), c) for c in range(4)]
            return [g[c, :, x - c * per:y - c * per] for x, y, c in cuts if x < y]

        zeros = jnp.zeros((d, 2 * LANE - GATE_RANK), g.dtype)
        return jnp.concatenate(run(0, a) + run(a + GATE_RANK, 4 * per) + run(a, a + GATE_RANK) + [zeros], axis=1)

    def full_rest(gs):
        g_br, g_out, g_up, g_down = gs
        return (jnp.transpose(g_br.reshape(4, N_BRANCH, d // 4, d), (1, 0, 2, 3)).reshape(N_BRANCH, d, d),
                g_out.reshape(d, d), jnp.transpose(g_up, (1, 0, 2)).reshape(d, dff), g_down.reshape(dff, d))

    def with_own(srcs, lands):
        return [lax.dynamic_update_slice(g, s[None], (chip, 0, 0)) for s, g in zip(srcs, lands)]

    def gather_start(tag, l, idx, after):
        srcs = shards_of(l, idx)
        return srcs, _split_start(f"gather_{tag}{l}_start", srcs, lands_of(srcs), _gather_copies, after)

    def gather_finish(tag, l, pending, after):
        srcs, started = pending
        _, lands = _split_wait(f"gather_{tag}{l}_wait", started, _gather_copies, after)
        return with_own(srcs, _gather_forward(f"gather_{tag}{l}_pass", lands))

    def token(pending):
        return pending[1][4][0, 0]

    win, wbr, wout, wup, wdown = ([None] * DEPTH for _ in range(5))
    src_first = shards_of(0, W_IN)
    g_first = with_own(src_first, _gather_plain("gather_in0", src_first))
    win[0] = full_w_in(g_first[0])

    dkh = half // LIN_HEADS
    lr_rows = DEPTH * GATE_RANK
    lr_slab = jnp.zeros((lr_rows, 4, half // 4), F32)
    lr_slab = lax.dynamic_update_slice(lr_slab, (gla_w_lr.reshape(lr_rows, 1, half // 4) * jnp.where(ci == 0, 1.0, 0.0)),
                                       (0, chip, 0))
    wlr_full = _small_allreduce(lr_slab.reshape(-1, LANE)).reshape(DEPTH, GATE_RANK, half)
    wlr_pad = jnp.concatenate([wlr_full, jnp.zeros((DEPTH, 2 * LANE - GATE_RANK, half), F32)], axis=1)
    pend_rest = gather_start("rest", 0, REST, wlr_full[0, :1, :1] + g_first[0][0, :1, :1].astype(F32))

    inv = 10000.0 ** (-jnp.arange(0, dkh, 2, dtype=F32) / dkh)
    ang = jnp.arange(t, dtype=F32)[:, None] * inv[None, :]
    cos, sin = jnp.cos(ang), jnp.sin(ang)
    rope_c = jnp.concatenate([cos, cos], axis=1)
    rope_s = jnp.concatenate([-sin, sin], axis=1)
    log_gamma = jnp.log1p(-jnp.exp2(-5.0 - jnp.arange(LIN_HEADS, dtype=F32)))
    lg_tab = jnp.broadcast_to(log_gamma[:, None, None], (LIN_HEADS, 1, dkh))

    def vec(a):
        return a.reshape(1, -1)

    x0, x0b, xh_in, rs_in = _ln_in(x[0], vec(ln_in_g) + token(pend_rest), vec(ln_in_b))
    saved = []
    xl, xlb = x0, x0b
    for l in range(DEPTH):
        p = _mm("proj_in", xlb, win[l], 512, 1792)
        g_rest = gather_finish("rest", l, pend_rest, p)
        wbr[l], wout[l], wup[l], wdown[l] = full_rest(g_rest)
        tok = 0.0
        if l + 1 < DEPTH:
            pend_in = gather_start("in", l + 1, W_IN, g_rest[0])
            tok = token(pend_in)
        bias = _bias_expand(rel_bias[l] + tok)
        bo = _attn_fwd(p, bias, d, off)
        ret_aux = (rope_c, rope_s, lg_tab + tok)
        gla_aux = (p, wlr_pad[l], vec(gla_b_lr[l]) + tok, vec(gla_norm_g[l]))
        o_ret, bo, st_ret = _lin_fwd(False, p, ret_aux, d, off, bo, 1)
        o_gla, bo, st_gla = _lin_fwd(True, p, gla_aux, d, off, bo, 2)
        tok = 0.0
        if l + 1 < DEPTH:
            g_in = gather_finish("in", l + 1, pend_in, bo)
            win[l + 1] = full_w_in(g_in[0])
            pend_rest = gather_start("rest", l + 1, REST, g_in[0])
            tok = token(pend_rest)
        proj, merged = _merge_fwd(bo, wbr[l], p, off["gates"])
        x1, x1b, xh1, rs1 = _mm_res_ln("out_proj_ln", merged, wout[l], xl, vec(ln1_g[l]) + tok, vec(ln1_b[l]),
                                       512, False)
        u = _mm("mlp_up", x1b, wup[l], 1024, 1024)
        x2, x2b, xh2, rs2, act = _mm_res_ln("mlp_down_ln", u, wdown[l], x1, vec(ln2_g[l]), vec(ln2_b[l]), 256, True)
        saved.append(dict(xlb=xlb, p=p, bias=bias, ret_aux=ret_aux, gla_aux=gla_aux, o_ret=o_ret, o_gla=o_gla,
                          st_ret=st_ret, st_gla=st_gla, bo=bo, proj=proj, merged=merged, x1b=x1b, xh1=xh1,
                          rs1=rs1, u=u, xh2=xh2, rs2=rs2, act=act))
        xl, xlb = x2, x2b

    small = {}
    last = saved[-1]
    loss_p, dz2, dz2b, dg, db = _loss_ln_bwd(xl, loss_target[0], last["xh2"], last["rs2"], vec(ln2_g[DEPTH - 1]))
    small["loss"] = loss_p[:, :1]
    grad_x = None

    def sibling_start(tag, l, idx, shards):
        lands = [_sds((g.shape[0], g.shape[1] // 2, g.shape[2]), F32) for g in shards]
        return tag, l, idx, _split_start(f"grad_{tag}{l}_sibling_start", shards, lands, _sibling_copies, per_src=1)

    def scatter_start(sibling, after):
        tag, l, idx, started = sibling
        shards, theirs = _split_wait(f"grad_{tag}{l}_sibling_wait", started, _sibling_copies, after)
        hs = [_add_half("grad_sibling_add", g, th, csel) for g, th in zip(shards, theirs)]
        lands = [_sds((3,) + h.shape[1:], F32) for h in hs]
        return tag, l, idx, _split_start(f"grad_{tag}{l}_scatter_start", hs, lands, _scatter_copies)

    adam_out = [None] * len(big_w)
    w_in_halves = [None] * DEPTH

    def scatter_finish(pending, after):
        tag, l, idx, started = pending
        hs, rcv = _split_wait(f"grad_{tag}{l}_scatter_wait", started, _scatter_copies, after)
        sms = [_sum_shards("grad_chip_sum", h, r, psel) for h, r in zip(hs, rcv)]
        last = None
        for i, own, sib in zip(idx, sms, _sibling_share(f"grad_{tag}{l}_share", sms)):
            if i == W_IN[0]:
                w_in_halves[l] = (own, sib)
                last = sib
            else:
                adam_out[i] = _adamw_layer("adamw_large", big_w[i], own, sib, csel, big_m[i], big_v[i], l,
                                           adam_out[i])
                last = adam_out[i][0]
        return last

    in_flight = []

    def scatter(sibling, after):
        pending = scatter_start(sibling, after)
        in_flight.append(pending)
        if len(in_flight) > 3:
            scatter_finish(in_flight.pop(0), pending[3][4])
        return pending[3][4][0, 0]

    def token_of(sibling):
        return sibling[3][4][0, 0]

    carry_tok = 0.0
    for l in reversed(range(DEPTH)):
        s = saved[l]
        small[("ln2_g", l)], small[("ln2_b", l)] = dg, db
        du = _mm_nt_relu2_bwd(dz2b, wdown[l], s["u"])
        g_wdown = _mm_tn("grad_w_down", s["act"], dz2b, 512, 512)
        g_wup = _mm_tn("grad_w_up", s["x1b"], du, 512, 512, shard="cols")
        dz1, dz1b, dg1, db1 = _mm_nt_res_lnbwd("mlp_up_bwd_ln", du, wup[l], dz2, s["xh1"], s["rs1"],
                                               vec(ln1_g[l]) + carry_tok, 512, dff)
        small[("ln1_g", l)], small[("ln1_b", l)] = dg1, db1
        dproj, dp = _merge_bwd(dz1b, wout[l], s["proj"], s["p"], off["gates"], npad)
        g_wout = _mm_tn("grad_w_out", s["merged"], dz1b, 512, 512)
        dbo = _mm("branch_proj_bwd", dproj, wbr[l], 1024, 1024, nt=True)
        g_wbr = _mm_tn("grad_w_branch", s["bo"], dproj, d // 4, 1024, shard="rows")
        sib = sibling_start("rest", l, REST, [g_wbr, g_wout.reshape(4, d // 4, d), g_wup, g_wdown.reshape(4, d, d)])
        rc, rs_, lg = s["ret_aux"]
        gp, gw, gb, gn_ = s["gla_aux"]
        dp, dk_acc, dv_acc, dbias = _attn_bwd(s["p"], s["bias"] + token_of(sib), dbo, dp, d, off)
        tok = scatter(sib, dbias)
        small[("rel_bias", l)] = _bias_reduce(dbias)
        dp = lax.dynamic_update_slice(dp, dk_acc[2 * QB:].astype(BF16), (0, off["ak"]))
        dp = lax.dynamic_update_slice(dp, dv_acc[2 * QB:].astype(BF16), (0, off["av"]))
        (dp,) = _lin_bwd(False, s["p"], (rc, rs_, lg + tok), s["o_ret"], s["st_ret"], dbo, 1, dp, d, off)
        dp, dpre, dblr, dgn = _lin_bwd(True, s["p"], (gp, gw, gb + tok, gn_), s["o_gla"], s["st_gla"], dbo, 2, dp,
                                       d, off)
        small[("gla_b_lr", l)] = dblr.reshape(1, half)
        small[("gla_norm_g", l)] = jnp.sum(dgn, axis=0)
        dpre_b = dpre.astype(BF16)
        glr_b = s["p"][:, off["glr"]:off["glr"] + LANE].astype(BF16)
        dp = _gate_lr_bwd(dpre_b, wlr_pad[l], dp, off["glr"])
        small[("gla_w_lr", l)] = _mm_tn("grad_gla_w_lr", glr_b, dpre_b, LANE, half)[:GATE_RANK]
        if l > 0:
            prev = saved[l - 1]
            xh_p, rs_p, g_p = prev["xh2"], prev["rs2"], vec(ln2_g[l - 1])
        else:
            xh_p, rs_p, g_p = xh_in, rs_in, vec(ln_in_g)
        g_win = _mm_tn("grad_w_in", s["xlb"], dp, 1024, 896)
        sib = sibling_start("in", l, W_IN, [_chip_shards_of_cols(g_win, d)])
        if l > 0:
            tok = token_of(sib)
        else:
            tok = scatter(sib, sib[3][4])
        dzp, dzpb, dg, db = _mm_nt_res_lnbwd("proj_in_bwd_ln", dp, win[l], dz1, xh_p, rs_p, g_p + tok, 1024, 1792)
        if l > 0:
            carry_tok = scatter(sib, dzp)
        dz2, dz2b = dzp, dzpb
        grad_x = dzp
    after = grad_x
    while in_flight:
        after = scatter_finish(in_flight.pop(0), after)
    wt, mt, vt = (jnp.transpose(a, (2, 0, 1)) for a in (big_w[0], big_m[0], big_v[0]))
    ntail = wt.shape[0] % LANE
    tails = [jnp.where(ci == 0, jnp.concatenate([own[:, -ntail:], sib[:, -ntail:]]),
                       jnp.concatenate([sib[:, -ntail:], own[:, -ntail:]])).T for own, sib in w_in_halves]
    adam_t = _adamw_tail("adamw_w_in_tail", wt, mt, vt, jnp.stack(tails, axis=1),
                         _adamw_colmajor("adamw_w_in", wt, mt, vt, w_in_halves, csel))
    adam_out[0] = [jnp.transpose(r, (1, 2, 0)) for r in adam_t]
    small["ln_in_g"], small["ln_in_b"] = dg, db
    rb_pad = 3 * LANE
    pieces = [small["loss"].reshape(-1), jnp.zeros((LANE - 1,), F32), small["ln_in_g"].reshape(-1),
              small["ln_in_b"].reshape(-1)]
    for l in range(DEPTH):
        rb = jnp.pad(small[("rel_bias", l)], ((0, 0), (0, rb_pad - (2 * REL_CLIP + 1))))
        pieces += [rb.reshape(-1), small[("gla_w_lr", l)].reshape(-1), small[("gla_b_lr", l)].reshape(-1),
                   small[("gla_norm_g", l)].reshape(-1), small[("ln1_g", l)].reshape(-1),
                   small[("ln1_b", l)].reshape(-1), small[("ln2_g", l)].reshape(-1), small[("ln2_b", l)].reshape(-1)]
    sizes = [pc.shape[0] for pc in pieces]
    packed = jnp.concatenate(pieces)
    padn = (-packed.shape[0]) % (8 * LANE)
    packed = jnp.concatenate([packed, jnp.zeros((padn,), F32)]).reshape(-1, LANE)
    red = _small_allreduce(packed, after).reshape(-1)

    parts, pos = [], 0
    for sz in sizes:
        parts.append(red[pos:pos + sz])
        pos += sz
    loss = parts[0][0]
    g_ln_in_g, g_ln_in_b = parts[2], parts[3]
    per = 8
    g_rel = jnp.stack([parts[4 + per * l].reshape(ATTN_HEADS, rb_pad)[:, :2 * REL_CLIP + 1] for l in range(DEPTH)])
    g_wlr_full = jnp.stack([parts[5 + per * l].reshape(GATE_RANK, half) for l in range(DEPTH)])
    g_wlr = lax.dynamic_slice_in_dim(g_wlr_full, chip * (half // 4), half // 4, axis=2)
    g_blr = jnp.stack([parts[6 + per * l] for l in range(DEPTH)])
    g_gn = jnp.stack([parts[7 + per * l] for l in range(DEPTH)])
    g_ln1g = jnp.stack([parts[8 + per * l] for l in range(DEPTH)])
    g_ln1b = jnp.stack([parts[9 + per * l] for l in range(DEPTH)])
    g_ln2g = jnp.stack([parts[10 + per * l] for l in range(DEPTH)])
    g_ln2b = jnp.stack([parts[11 + per * l] for l in range(DEPTH)])

    grads = [g_ln_in_g, g_ln_in_b, None, g_rel, g_wlr, g_blr, g_gn, None, None, g_ln1g, g_ln1b, None, None,
             g_ln2g, g_ln2b]
    ws = [ln_in_g, ln_in_b, w_in, rel_bias, gla_w_lr, gla_b_lr, gla_norm_g, w_branch, w_out, ln1_g, ln1_b,
          w_up, w_down, ln2_g, ln2_b]
    ms = [m_ln_in_g, m_ln_in_b, m_w_in, m_rel_bias, m_gla_w_lr, m_gla_b_lr, m_gla_norm_g, m_w_branch, m_w_out,
          m_ln1_g, m_ln1_b, m_w_up, m_w_down, m_ln2_g, m_ln2_b]
    vs = [v_ln_in_g, v_ln_in_b, v_w_in, v_rel_bias, v_gla_w_lr, v_gla_b_lr, v_gla_norm_g, v_w_branch, v_w_out,
          v_ln1_g, v_ln1_b, v_w_up, v_w_down, v_ln2_g, v_ln2_b]

    deltas, new_ms, new_vs = [None] * 15, [None] * 15, [None] * 15
    big_idx = [2, 7, 8, 11, 12]
    for i, res in zip(big_idx, adam_out):
        shp = ws[i].shape
        grads[i], deltas[i], new_ms[i], new_vs[i] = (r.reshape(shp) for r in res)
    small_idx = [i for i in range(15) if i not in big_idx]

    def pack(arrs):
        flat_ = jnp.concatenate([arrs[i].reshape(-1) for i in small_idx])
        pad_ = (-flat_.shape[0]) % (8 * LANE)
        return jnp.concatenate([flat_, jnp.ones((pad_,), F32)]).reshape(-1, LANE)

    dl, nm, nv = _adamw("adamw_small", pack(ws), pack(grads), pack(ms), pack(vs))
    pos = 0
    for i in small_idx:
        sz = int(np.prod(ws[i].shape))
        deltas[i] = dl.reshape(-1)[pos:pos + sz].reshape(ws[i].shape)
        new_ms[i] = nm.reshape(-1)[pos:pos + sz].reshape(ws[i].shape)
        new_vs[i] = nv.reshape(-1)[pos:pos + sz].reshape(ws[i].shape)
        pos += sz

    return (loss, grad_x[None], *grads, *deltas, *new_ms, *new_vs)
```

```python
import functools

import numpy as np
import jax
import jax.numpy as jnp
from jax import lax
from jax.experimental import pallas as pl
from jax.experimental.pallas import tpu as pltpu

F32 = jnp.float32
BF16 = jnp.bfloat16
MXU_DTYPE = BF16
HI = lax.Precision.HIGHEST

DEPTH = 2
CHUNK = 64
N_BRANCH = 3
ATTN_HEADS = 8
ATTN_LEFT = 8
REL_CLIP = 2 * CHUNK
LIN_HEADS = 4
GATE_RANK = 16
GATE_NORM = 16.0
LN_EPS = 1e-5
NEG_INF = -1e30
ALPHA = (2 * DEPTH) ** 0.25
ADAM_LR, ADAM_B1, ADAM_B2, ADAM_EPS, ADAM_WD, ADAM_STEP = 0.001, 0.9, 0.999, 1e-08, 0.01, 10

LANE = 128
VMEM_LIMIT = 56 << 20
QB = 256
KW = 3 * QB
LB = 256
MESH_AXES = ("x", "y", "c")
DEV = pl.DeviceIdType.MESH


def _cp(sem):
    return pltpu.CompilerParams(dimension_semantics=sem, vmem_limit_bytes=VMEM_LIMIT)


def _mx(v):
    return v.astype(MXU_DTYPE)


def _dot(a, b):
    return jnp.dot(_mx(a), _mx(b), preferred_element_type=F32)


def _dot_nt(a, b):
    return lax.dot_general(_mx(a), _mx(b), (((1,), (1,)), ((), ())), preferred_element_type=F32)


def _dot_tn(a, b):
    return lax.dot_general(_mx(a), _mx(b), (((0,), (0,)), ((), ())), preferred_element_type=F32)


def _dot_hi(a, b):
    return jnp.dot(a, b, precision=HI, preferred_element_type=F32)


def _sigmoid(v):
    return 1.0 / (1.0 + jnp.exp(-v))


def _sds(shape, dtype):
    return jax.ShapeDtypeStruct(shape, dtype)


def _mm(name, a, b, tm, tn, nt=False, out_dtype=F32):
    batched = a.ndim == 3
    m, k = a.shape[-2:]
    n = b.shape[-2] if nt else b.shape[-1]
    tm, tn = min(tm, m), min(tn, n)

    def body(a_ref, b_ref, o_ref):
        f = _dot_nt if nt else _dot
        o_ref[...] = f(a_ref[...], b_ref[...]).astype(o_ref.dtype)

    rows_inner = (n // tn) * m < (m // tm) * n

    def ij(u, v):
        return (v, u) if rows_inner else (u, v)

    if batched:
        nb = a.shape[0]
        grid = (nb,) + ij(m // tm, n // tn)
        a_spec = pl.BlockSpec((None, tm, k), lambda g, u, v: (g, ij(u, v)[0], 0))
        b_spec = (pl.BlockSpec((None, tn, k), lambda g, u, v: (g, ij(u, v)[1], 0)) if nt
                  else pl.BlockSpec((None, k, tn), lambda g, u, v: (g, 0, ij(u, v)[1])))
        o_spec = pl.BlockSpec((None, tm, tn), lambda g, u, v: (g,) + ij(u, v))
        out_shape = _sds((nb, m, n), out_dtype)
        sem = ("parallel", "parallel", "parallel")
    else:
        grid = ij(m // tm, n // tn)
        a_spec = pl.BlockSpec((tm, k), lambda u, v: (ij(u, v)[0], 0))
        b_spec = (pl.BlockSpec((tn, k), lambda u, v: (ij(u, v)[1], 0)) if nt
                  else pl.BlockSpec((k, tn), lambda u, v: (0, ij(u, v)[1])))
        o_spec = pl.BlockSpec((tm, tn), lambda u, v: ij(u, v))
        out_shape = _sds((m, n), out_dtype)
        sem = ("parallel", "parallel")
    return pl.pallas_call(body, name=name, grid=grid, in_specs=[a_spec, b_spec], out_specs=o_spec,
                          out_shape=out_shape, compiler_params=_cp(sem))(a, b)


def _mm_tn(name, a, b, tm, tn, shard=None):
    batched = a.ndim == 3
    k, m = a.shape[-2:]
    n = b.shape[-1]
    tm, tn = min(tm, m), min(tn, n)

    def body(a_ref, b_ref, o_ref):
        o_ref[...] = lax.dot_general(_mx(a_ref[...]), _mx(b_ref[...]), (((0,), (0,)), ((), ())),
                                     preferred_element_type=F32)

    if batched:
        nb = a.shape[0]
        grid = (nb, m // tm, n // tn)
        a_spec = pl.BlockSpec((None, k, tm), lambda g, i, j: (g, 0, i))
        b_spec = pl.BlockSpec((None, k, tn), lambda g, i, j: (g, 0, j))
        if shard == "rows":
            assert 4 * tm == m
            o_spec = pl.BlockSpec((None, tm, tn), lambda g, i, j: (i, g, j))
            out_shape = _sds((4, nb * tm, n), F32)
        else:
            o_spec = pl.BlockSpec((None, tm, tn), lambda g, i, j: (g, i, j))
            out_shape = _sds((nb, m, n), F32)
    else:
        grid = (m // tm, n // tn)
        a_spec = pl.BlockSpec((k, tm), lambda i, j: (0, i))
        b_spec = pl.BlockSpec((k, tn), lambda i, j: (0, j))
        if shard == "cols":
            per = n // 4 // tn
            o_spec = pl.BlockSpec((None, tm, tn), lambda i, j: (j // per, i, j % per))
            out_shape = _sds((4, m, n // 4), F32)
        else:
            o_spec = pl.BlockSpec((tm, tn), lambda i, j: (i, j))
            out_shape = _sds((m, n), F32)
    return pl.pallas_call(body, name=name, grid=grid, in_specs=[a_spec, b_spec], out_specs=o_spec,
                          out_shape=out_shape, compiler_params=_cp(("parallel",) * len(grid)))(a, b)


def _ln_rows(y, g, b):
    mu = jnp.mean(y, axis=-1, keepdims=True)
    yc = y - mu
    var = jnp.mean(yc * yc, axis=-1, keepdims=True)
    rs = lax.rsqrt(var + LN_EPS)
    xh = yc * rs
    return xh * g + b, xh, rs


def _ln_in(x, g, b, tm=256):
    t, d = x.shape

    def body(x_ref, g_ref, b_ref, o_ref, ob_ref, xh_ref, rs_ref):
        o, xh, rs = _ln_rows(x_ref[...], g_ref[...], b_ref[...])
        o_ref[...] = o
        ob_ref[...] = o.astype(BF16)
        xh_ref[...] = xh
        rs_ref[...] = rs

    row = pl.BlockSpec((tm, d), lambda i: (i, 0))
    vec = pl.BlockSpec((1, d), lambda i: (0, 0))
    return pl.pallas_call(
        body, name="ln_in", grid=(t // tm,), in_specs=[row, vec, vec],
        out_specs=[row, row, row, pl.BlockSpec((tm, 1), lambda i: (i, 0))],
        out_shape=[_sds((t, d), F32), _sds((t, d), BF16), _sds((t, d), F32), _sds((t, 1), F32)],
        compiler_params=_cp(("parallel",)))(x, g, b)


def _mm_res_ln(name, a, w, res, g, b, tm, relu2):
    t, k = a.shape
    d = w.shape[1]

    def body(a_ref, w_ref, r_ref, g_ref, b_ref, o_ref, ob_ref, xh_ref, rs_ref, *act_ref):
        av = a_ref[...]
        if relu2:
            av = jnp.square(jnp.maximum(av, 0.0))
            act_ref[0][...] = av.astype(BF16)
        y = ALPHA * r_ref[...] + _dot(av, w_ref[...])
        o, xh, rs = _ln_rows(y, g_ref[...], b_ref[...])
        o_ref[...] = o
        ob_ref[...] = o.astype(BF16)
        xh_ref[...] = xh
        rs_ref[...] = rs

    row = pl.BlockSpec((tm, d), lambda i: (i, 0))
    vec = pl.BlockSpec((1, d), lambda i: (0, 0))
    arow = pl.BlockSpec((tm, k), lambda i: (i, 0))
    out_specs = [row, row, row, pl.BlockSpec((tm, 1), lambda i: (i, 0))]
    out_shape = [_sds((t, d), F32), _sds((t, d), BF16), _sds((t, d), F32), _sds((t, 1), F32)]
    if relu2:
        out_specs.append(arow)
        out_shape.append(_sds((t, k), BF16))
    return pl.pallas_call(
        body, name=name, grid=(t // tm,),
        in_specs=[arow, pl.BlockSpec((k, d), lambda i: (0, 0)), row, vec, vec],
        out_specs=out_specs, out_shape=out_shape, compiler_params=_cp(("parallel",)))(a, w, res, g, b)


def _merge_fwd(bo, wb, p, gate_off, tm=512, tn=512):
    _, t, d = bo.shape
    gb = gate_off // tn

    def body(bo_ref, wb_ref, g0, g1, g2, proj_ref, m_ref):
        acc = None
        for n, g_ref in enumerate((g0, g1, g2)):
            pr = _dot(bo_ref[n], wb_ref[n])
            proj_ref[n] = pr
            term = _sigmoid(g_ref[...]) * pr
            acc = term if acc is None else acc + term
        m_ref[...] = acc.astype(BF16)

    gspecs = [pl.BlockSpec((tm, tn), functools.partial(lambda i, j, n: (i, gb + n * (d // tn) + j), n=n))
              for n in range(3)]
    return pl.pallas_call(
        body, name="merge_fwd", grid=(t // tm, d // tn),
        in_specs=[pl.BlockSpec((3, tm, d), lambda i, j: (0, i, 0)),
                  pl.BlockSpec((3, d, tn), lambda i, j: (0, 0, j))] + gspecs,
        out_specs=[pl.BlockSpec((3, tm, tn), lambda i, j: (0, i, j)), pl.BlockSpec((tm, tn), lambda i, j: (i, j))],
        out_shape=[_sds((3, t, d), F32), _sds((t, d), BF16)],
        compiler_params=_cp(("parallel", "parallel")))(bo, wb, p, p, p)


def _merge_bwd(dz, wout, proj, p, gate_off, npad, tm=256):
    t, d = dz.shape

    def body(dz_ref, w_ref, proj_ref, g0, g1, g2, dproj_ref, dp_ref):
        dm = _dot_nt(dz_ref[...], w_ref[...])
        for n, g_ref in enumerate((g0, g1, g2)):
            s = _sigmoid(g_ref[...])
            dproj_ref[n] = (dm * s).astype(BF16)
            dp_ref[:, n * d:(n + 1) * d] = (dm * proj_ref[n] * (s * (1.0 - s))).astype(BF16)

    gspecs = [pl.BlockSpec((tm, d), functools.partial(lambda i, n: (i, gate_off // d + n), n=n)) for n in range(3)]
    return pl.pallas_call(
        body, name="merge_bwd", grid=(t // tm,),
        in_specs=[pl.BlockSpec((tm, d), lambda i: (i, 0)), pl.BlockSpec((d, d), lambda i: (0, 0)),
                  pl.BlockSpec((3, tm, d), lambda i: (0, i, 0))] + gspecs,
        out_specs=[pl.BlockSpec((3, tm, d), lambda i: (0, i, 0)),
                   pl.BlockSpec((tm, 3 * d), lambda i: (i, gate_off // (3 * d)))],
        out_shape=[_sds((3, t, d), BF16), _sds((t, npad), BF16)],
        compiler_params=_cp(("parallel",)))(dz, wout, proj, p, p, p)


def _gate_lr_bwd(dpre, wlr, dp, col_off, tm=512):
    t, k = dpre.shape
    w = wlr.shape[0]

    def body(a_ref, w_ref, dp_in, o_ref):
        o_ref[...] = _dot_nt(a_ref[...], w_ref[...]).astype(BF16)

    return pl.pallas_call(
        body, name="gate_lr_bwd", grid=(t // tm,),
        in_specs=[pl.BlockSpec((tm, k), lambda i: (i, 0)), pl.BlockSpec((w, k), lambda i: (0, 0)), ANY],
        out_specs=pl.BlockSpec((tm, w), lambda i: (i, col_off // w)), out_shape=_sds(dp.shape, BF16),
        input_output_aliases={2: 0}, compiler_params=_cp(("parallel",)))(dpre, wlr, dp)


def _mm_nt_relu2_bwd(dz, wdown, u, tm=512, tn=1024):
    t, d = dz.shape
    f = wdown.shape[0]

    def body(dz_ref, w_ref, u_ref, du_ref):
        da = _dot_nt(dz_ref[...], w_ref[...])
        du_ref[...] = (da * (2.0 * jnp.maximum(u_ref[...], 0.0))).astype(BF16)

    return pl.pallas_call(
        body, name="mlp_down_bwd", grid=(f // tn, t // tm),
        in_specs=[pl.BlockSpec((tm, d), lambda j, i: (i, 0)), pl.BlockSpec((tn, d), lambda j, i: (j, 0)),
                  pl.BlockSpec((tm, tn), lambda j, i: (i, j))],
        out_specs=pl.BlockSpec((tm, tn), lambda j, i: (i, j)), out_shape=_sds((t, f), BF16),
        compiler_params=_cp(("parallel", "parallel")))(dz, wdown, u)


def _ln_bwd_rows(dx, xh, rs, g):
    dxh = dx * g
    m1 = jnp.mean(dxh, axis=-1, keepdims=True)
    m2 = jnp.mean(dxh * xh, axis=-1, keepdims=True)
    return rs * (dxh - m1 - xh * m2)


def _mm_nt_res_lnbwd(name, a, w, dres, xh, rs, g, tm, tk):
    t, k = a.shape
    d = w.shape[0]
    nk = k // tk

    def body(a_ref, w_ref, dr_ref, xh_ref, rs_ref, g_ref, dz_ref, dzb_ref, dg_ref, db_ref, acc_ref):
        i, kk = pl.program_id(0), pl.program_id(1)

        @pl.when(kk == 0)
        def _():
            acc_ref[...] = ALPHA * dr_ref[...]

        acc_ref[...] += _dot_nt(a_ref[...], w_ref[...])

        @pl.when(jnp.logical_and(i == 0, kk == 0))
        def _():
            dg_ref[...] = jnp.zeros_like(dg_ref)
            db_ref[...] = jnp.zeros_like(db_ref)

        @pl.when(kk == nk - 1)
        def _():
            dx = acc_ref[...]
            xhv = xh_ref[...]
            dz = _ln_bwd_rows(dx, xhv, rs_ref[...], g_ref[...])
            dz_ref[...] = dz
            dzb_ref[...] = dz.astype(BF16)
            dg_ref[...] += jnp.sum(dx * xhv, axis=0, keepdims=True)
            db_ref[...] += jnp.sum(dx, axis=0, keepdims=True)

    row = pl.BlockSpec((tm, d), lambda i, kk: (i, 0))
    vec = pl.BlockSpec((1, d), lambda i, kk: (0, 0))
    return pl.pallas_call(
        body, name=name, grid=(t // tm, nk),
        in_specs=[pl.BlockSpec((tm, tk), lambda i, kk: (i, kk)), pl.BlockSpec((d, tk), lambda i, kk: (0, kk)),
                  row, row, pl.BlockSpec((tm, 1), lambda i, kk: (i, 0)), vec],
        out_specs=[row, row, vec, vec],
        out_shape=[_sds((t, d), F32), _sds((t, d), BF16), _sds((1, d), F32), _sds((1, d), F32)],
        scratch_shapes=[pltpu.VMEM((tm, d), F32)],
        compiler_params=_cp(("arbitrary", "arbitrary")))(a, w, dres, xh, rs, g)


def _loss_ln_bwd(x2, target, xh, rs, g, tm=256):
    t, d = x2.shape

    def body(x_ref, t_ref, xh_ref, rs_ref, g_ref, loss_ref, dz_ref, dzb_ref, dg_ref, db_ref):
        @pl.when(pl.program_id(0) == 0)
        def _():
            loss_ref[...] = jnp.zeros_like(loss_ref)
            dg_ref[...] = jnp.zeros_like(dg_ref)
            db_ref[...] = jnp.zeros_like(db_ref)

        err = x_ref[...] - t_ref[...]
        per_row = jnp.mean(err * err, axis=-1, keepdims=True)
        loss_ref[...] += 0.5 * jnp.sum(per_row, axis=0, keepdims=True)
        dx = err * (1.0 / d)
        xhv = xh_ref[...]
        dz = _ln_bwd_rows(dx, xhv, rs_ref[...], g_ref[...])
        dz_ref[...] = dz
        dzb_ref[...] = dz.astype(BF16)
        dg_ref[...] += jnp.sum(dx * xhv, axis=0, keepdims=True)
        db_ref[...] += jnp.sum(dx, axis=0, keepdims=True)

    row = pl.BlockSpec((tm, d), lambda i: (i, 0))
    vec = pl.BlockSpec((1, d), lambda i: (0, 0))
    return pl.pallas_call(
        body, name="loss_ln_bwd", grid=(t // tm,),
        in_specs=[row, row, row, pl.BlockSpec((tm, 1), lambda i: (i, 0)), vec],
        out_specs=[pl.BlockSpec((1, LANE), lambda i: (0, 0)), row, row, vec, vec],
        out_shape=[_sds((1, LANE), F32), _sds((t, d), F32), _sds((t, d), BF16), _sds((1, d), F32),
                   _sds((1, d), F32)],
        compiler_params=_cp(("arbitrary",)))(x2, target, xh, rs, g)


HPA_FWD = 8
HPA = 4


STRIP = 16


def _attn_scores(q_ref, k_refs, bias_ref, i, dh, hh):
    cols = pl.ds(hh * dh, dh)
    q = q_ref[:, cols] * (dh ** -0.5)
    k = jnp.concatenate([r[:, cols] for r in k_refs], axis=0)
    s = _dot_nt(q, k)
    before_start = lax.broadcasted_iota(jnp.int32, (STRIP, KW), 1) < (2 - i) * QB
    strips = []
    for r in range(0, QB, STRIP):
        ss = jnp.where(before_start, NEG_INF, s[r:r + STRIP] + bias_ref[hh, r:r + STRIP])
        e = jnp.exp(ss - jnp.max(ss, axis=-1, keepdims=True))
        strips.append(e / jnp.sum(e, axis=-1, keepdims=True))
    return q, k, strips


def _attn_specs(dh, off, hp):
    w = hp * dh
    qcol, kcol, vcol = off["aq"] // w, off["ak"] // w, off["av"] // w
    q_spec = pl.BlockSpec((QB, w), lambda g, i: (i, qcol + g))
    k_specs = [pl.BlockSpec((QB, w), functools.partial(lambda g, i, j: (jnp.maximum(i - 2 + j, 0), kcol + g), j=j))
               for j in range(3)]
    v_specs = [pl.BlockSpec((QB, w), functools.partial(lambda g, i, j: (jnp.maximum(i - 2 + j, 0), vcol + g), j=j))
               for j in range(3)]
    bias_spec = pl.BlockSpec((hp, QB, KW), lambda g, i: (g, 0, 0))
    return q_spec, k_specs, v_specs, bias_spec


def _attn_fwd(p, bias, d, off):
    t = p.shape[0]
    dh = d // ATTN_HEADS

    def body(q_ref, k0, k1, k2, v0, v1, v2, bias_ref, o_ref):
        for hh in range(HPA_FWD):
            cols = pl.ds(hh * dh, dh)
            _, _, strips = _attn_scores(q_ref, (k0, k1, k2), bias_ref, pl.program_id(1), dh, hh)
            pr = jnp.concatenate([_mx(ps) for ps in strips], axis=0)
            v = jnp.concatenate([v0[:, cols], v1[:, cols], v2[:, cols]], axis=0)
            o_ref[:, cols] = _dot(pr, v).astype(o_ref.dtype)

    q_spec, k_specs, v_specs, bias_spec = _attn_specs(dh, off, HPA_FWD)
    return pl.pallas_call(
        body, name="attn_fwd", grid=(ATTN_HEADS // HPA_FWD, t // QB),
        in_specs=[q_spec] + k_specs + v_specs + [bias_spec],
        out_specs=pl.BlockSpec((None, QB, HPA_FWD * dh), lambda g, i: (0, i, g)),
        out_shape=_sds((N_BRANCH, t, d), BF16),
        compiler_params=_cp(("parallel", "parallel")))(p, p, p, p, p, p, p, bias)


def _attn_bwd(p, bias, do, dp, d, off):
    t = p.shape[0]
    dh = d // ATTN_HEADS
    tp = t + 2 * QB

    def body(q_ref, k0, k1, k2, v0, v1, v2, bias_ref, do_ref, dp_in, dq_ref, dk_out, dv_out, dbias_ref, dk_ref,
             dv_ref):
        i = pl.program_id(1)

        @pl.when(i == 0)
        def _():
            dk_ref[...] = jnp.zeros_like(dk_ref)
            dv_ref[...] = jnp.zeros_like(dv_ref)
            dbias_ref[...] = jnp.zeros_like(dbias_ref)

        rows = pl.ds(pl.multiple_of(i * QB, QB), KW)
        for hh in range(HPA):
            cols = pl.ds(hh * dh, dh)
            q, k, strips = _attn_scores(q_ref, (k0, k1, k2), bias_ref, i, dh, hh)
            v = jnp.concatenate([v0[:, cols], v1[:, cols], v2[:, cols]], axis=0)
            dov = do_ref[:, cols]
            dp = _dot_nt(dov, v)
            ds_strips = []
            for n, ps in enumerate(strips):
                r = n * STRIP
                dps = dp[r:r + STRIP]
                dss = ps * (dps - jnp.sum(ps * dps, axis=-1, keepdims=True))
                dbias_ref[hh, r:r + STRIP] += dss
                ds_strips.append(_mx(dss))
            ds = jnp.concatenate(ds_strips, axis=0)
            pr = jnp.concatenate([_mx(ps) for ps in strips], axis=0)
            dq_ref[:, cols] = (_dot(ds, k) * (dh ** -0.5)).astype(dq_ref.dtype)
            dk_ref[rows, cols] += _dot_tn(ds, q)
            dv_ref[rows, cols] += _dot_tn(pr, dov)

        @pl.when(i == t // QB - 1)
        def _():
            mine = pl.ds(pl.multiple_of(pl.program_id(0) * w, w), w)
            pltpu.sync_copy(dk_ref, dk_out.at[:, mine])
            pltpu.sync_copy(dv_ref, dv_out.at[:, mine])

    w = HPA * dh
    q_spec, k_specs, v_specs, bias_spec = _attn_specs(dh, off, HPA)
    qcol = off["aq"] // w
    return pl.pallas_call(
        body, name="attn_bwd", grid=(ATTN_HEADS // HPA, t // QB),
        in_specs=[q_spec] + k_specs + v_specs + [bias_spec,
                                                 pl.BlockSpec((None, QB, w), lambda g, i: (0, i, g)), ANY],
        out_specs=[pl.BlockSpec((QB, w), lambda g, i: (i, qcol + g)), ANY, ANY, bias_spec],
        out_shape=[_sds(dp.shape, BF16), _sds((tp, d), F32), _sds((tp, d), F32),
                   _sds((ATTN_HEADS, QB, KW), F32)],
        scratch_shapes=[pltpu.VMEM((tp, w), F32), pltpu.VMEM((tp, w), F32)], input_output_aliases={9: 0},
        compiler_params=_cp(("parallel", "arbitrary")))(p, p, p, p, p, p, p, bias, do, dp)


def _onehot_mm(name, a, b):
    def body(a_ref, b_ref, o_ref):
        o_ref[...] = _dot_hi(a_ref[...], b_ref[...])

    return pl.pallas_call(body, name=name, out_shape=_sds((a.shape[0], b.shape[1]), F32),
                          compiler_params=pltpu.CompilerParams(vmem_limit_bytes=VMEM_LIMIT))(a, b)


def _diag_index():
    ii, jj = np.arange(CHUNK)[:, None], np.arange(CHUNK)[None, :]
    return (ii - jj + CHUNK - 1).reshape(-1)


def _bias_expand(rel_bias):
    h = rel_bias.shape[0]
    nq, nk, shift = QB // CHUNK, KW // CHUNK, (2 * QB) // CHUNK
    nbin, ndc = 3 * LANE, 4
    rb = jnp.pad(rel_bias, ((0, 0), (0, nbin - rel_bias.shape[1])))
    win = np.clip(CHUNK * np.arange(ndc)[:, None] + np.arange(LANE)[None, :] - (CHUNK - 1), -REL_CLIP, REL_CLIP)
    sel = (jnp.arange(nbin)[:, None] == jnp.asarray((win + REL_CLIP).reshape(1, -1))).astype(F32)
    windows = _onehot_mm("bias_windows", rb, sel)
    diag_t = (jnp.arange(LANE)[:, None] == jnp.asarray(_diag_index().reshape(1, -1))).astype(F32)
    blocks = _onehot_mm("bias_blocks", windows.reshape(h * ndc, LANE), diag_t).reshape(h, ndc, CHUNK, CHUNK)
    off_band = jnp.full((h, CHUNK, CHUNK), NEG_INF, F32)
    dcs = [shift + nq - 1 - u for u in range(nk + nq - 1)]
    strip = jnp.concatenate([blocks[:, min(dc, ndc - 1)] if 0 <= dc <= ATTN_LEFT else off_band for dc in dcs], axis=2)
    return jnp.concatenate([strip[:, :, (nq - 1 - ic) * CHUNK:(nq - 1 - ic + nk) * CHUNK] for ic in range(nq)],
                           axis=1)


def _bias_reduce(dbias):
    h = dbias.shape[0]
    nq, nk = QB // CHUNK, KW // CHUNK
    nbin = 3 * LANE
    blocks = dbias.reshape(h, nq, CHUNK, nk, CHUNK).transpose(0, 1, 3, 2, 4).reshape(h * nq * nk, CHUNK * CHUNK)
    diag = (jnp.asarray(_diag_index().reshape(-1, 1)) == jnp.arange(LANE)[None, :]).astype(F32)
    ic = np.arange(nq)[:, None, None]
    jc = np.arange(nk)[None, :, None]
    dl = np.arange(LANE)[None, None, :] - (CHUNK - 1)
    rel = np.clip(CHUNK * (ic - jc + (2 * QB) // CHUNK) + dl, -REL_CLIP, REL_CLIP) + REL_CLIP
    bins = (jnp.asarray(rel.reshape(-1, 1)) == jnp.arange(nbin)[None, :]).astype(F32)

    diags = _onehot_mm("bias_diag_sums", blocks, diag)
    out = _onehot_mm("bias_bin_sums", diags.reshape(h, nq * nk * LANE), bins)
    return out[:, :2 * REL_CLIP + 1]


def _chunk_masks():
    r = lax.broadcasted_iota(jnp.int32, (LB, LB), 0)
    c = lax.broadcasted_iota(jnp.int32, (LB, LB), 1)
    return (r // CHUNK) == (c // CHUNK), r >= c, r <= c


def _chunks(a):
    return [a[c * CHUNK:(c + 1) * CHUNK] for c in range(LB // CHUNK)]


def _per_chunk(a, f):
    return jnp.concatenate([jnp.broadcast_to(f(c), c.shape) for c in _chunks(a)], axis=0)


def _dot_sel(sel, x):
    def top(v):
        return lax.bitcast_convert_type(lax.bitcast_convert_type(v, jnp.int32) & jnp.int32(-65536), F32)

    hi = top(x)
    mid = top(x - hi)
    lo = (x - hi) - mid
    d = functools.partial(jnp.dot, sel.astype(jnp.bfloat16), preferred_element_type=F32)
    return d(hi.astype(jnp.bfloat16)) + d(mid.astype(jnp.bfloat16)) + d(lo.astype(jnp.bfloat16))


def _lin_block(gla, q, k, v, aux):
    dk = q.shape[-1]
    same, low, up = _chunk_masks()
    ones = same.astype(F32)
    if gla:
        glr, wlr, blr = aux
        q = q * (dk ** -0.5)
        pre = _dot(glr, wlr) + blr
        log_a = (jnp.minimum(pre, 0.0) - jnp.log(1.0 + jnp.exp(-jnp.abs(pre)))) / GATE_NORM
        b = _dot_sel(jnp.where(low, ones, 0.0), log_a)
        lastb = _per_chunk(b, lambda c: c[CHUNK - 1:])
    else:
        cs, sn, lg = aux
        pre = None
        half = dk // 2
        q = q * cs + pltpu.roll(q, half, 1) * sn
        k = (k * cs + pltpu.roll(k, half, 1) * sn) * (dk ** -0.5)
        pos = (lax.broadcasted_iota(jnp.int32, (LB, dk), 0) % CHUNK).astype(F32) + 1.0
        b = pos * lg
        lastb = jnp.broadcast_to(float(CHUNK) * lg, b.shape)
    eb, enb, el, dec = jnp.exp(b), jnp.exp(-b), jnp.exp(lastb - b), jnp.exp(lastb)
    qf, kf, qb, kb, kl = q * eb, k * enb, q * enb, k * eb, k * el
    s = jnp.where(same, jnp.where(low, _dot_nt(qf, kf), _dot_nt(qb, kb)), 0.0)
    return dict(pre=pre, eb=eb, enb=enb, el=el, dec=dec, qf=qf, kf=kf, qb=qb, kb=kb, kl=kl, s=s,
                same=same, low=low, up=up, ones=ones)


def _lin_norm_gate(gla, o, gate, gn):
    sg = _sigmoid(gate)
    silu = gate * sg
    if gla:
        r = lax.rsqrt(jnp.mean(o * o, axis=-1, keepdims=True) + LN_EPS)
        hn = o * r
        return silu * (hn * gn), (sg, silu, r, hn)
    mu = jnp.mean(o, axis=-1, keepdims=True)
    oc = o - mu
    r = lax.rsqrt(jnp.mean(oc * oc, axis=-1, keepdims=True) + LN_EPS)
    hn = oc * r
    return silu * hn, (sg, silu, r, hn)


HPS = 4


def _lin_specs(gla, dk, dv, off, rev, nb):
    pre = "g" if gla else "r"
    wk, wv = HPS * dk, HPS * dv
    qc, kc, vc, gc = (off[pre + "q"] // wk, off[pre + "k"] // wk, off[pre + "v"] // wv, off[pre + "g"] // wv)

    def blk(i):
        return nb - 1 - i if rev else i

    specs = [pl.BlockSpec((LB, wk), lambda g, i: (blk(i), qc + g)),
             pl.BlockSpec((LB, wk), lambda g, i: (blk(i), kc + g)),
             pl.BlockSpec((LB, wv), lambda g, i: (blk(i), vc + g)),
             pl.BlockSpec((LB, wv), lambda g, i: (blk(i), gc + g))]
    if gla:
        specs += [pl.BlockSpec((LB, LANE), lambda g, i: (blk(i), off["glr"] // LANE)),
                  pl.BlockSpec((LANE, wk), lambda g, i: (0, g)),
                  pl.BlockSpec((1, wk), lambda g, i: (0, g)),
                  pl.BlockSpec((1, dv), lambda g, i: (0, 0))]
    else:
        specs += [pl.BlockSpec((LB, dk), lambda g, i: (blk(i), 0)),
                  pl.BlockSpec((LB, dk), lambda g, i: (blk(i), 0)),
                  pl.BlockSpec((HPS, 1, dk), lambda g, i: (g, 0, 0))]
    return specs, blk


def _lin_aux(gla, refs, rows, hh, dk):
    if gla:
        glr_ref, wlr_ref, blr_ref, gn_ref = refs
        kcols = pl.ds(hh * dk, dk)
        return (glr_ref[rows, :], wlr_ref[:, kcols], blr_ref[:, kcols]), gn_ref[...]
    cs_ref, sn_ref, lg_ref = refs
    return (cs_ref[rows, :], sn_ref[rows, :], lg_ref[hh]), None


def _lin_fwd(gla, p, aux_arrays, d, off, branches, slot):
    t = p.shape[0]
    dk, dv = d // (2 * LIN_HEADS), d // LIN_HEADS
    nb, cb = t // LB, LB // CHUNK
    naux = len(aux_arrays)

    def body(*refs):
        q_ref, k_ref, v_ref, g_ref = refs[:4]
        aux_refs = refs[4:4 + naux]
        o_ref, bo_ref, st_out_ref, st_ref = refs[5 + naux:]

        @pl.when(pl.program_id(1) == 0)
        def _():
            st_ref[...] = jnp.zeros_like(st_ref)

        rows = slice(None)
        for hh in range(HPS):
            kcols, vcols = pl.ds(hh * dk, dk), pl.ds(hh * dv, dv)
            aux, gn = _lin_aux(gla, aux_refs, rows, hh, dk)
            v = v_ref[:, vcols]
            blk = _lin_block(gla, q_ref[:, kcols], k_ref[:, kcols], v, aux)
            st = st_ref[hh]
            inter = []
            for c, (qf, kl, dec, vc) in enumerate(zip(_chunks(blk["qf"]), _chunks(blk["kl"]), _chunks(blk["dec"]),
                                                      _chunks(v))):
                st_out_ref[hh, c] = st
                inter.append(_dot_nt(qf, st))
                st = st * dec[:1] + _dot_tn(vc, kl)
            st_ref[hh] = st
            o = _dot(blk["s"], v) + jnp.concatenate(inter, axis=0)
            o_ref[:, vcols] = o
            out, _ = _lin_norm_gate(gla, o, g_ref[:, vcols], gn)
            bo_ref[:, vcols] = out.astype(BF16)

    specs, _ = _lin_specs(gla, dk, dv, off, False, nb)
    orow = pl.BlockSpec((LB, HPS * dv), lambda g, i: (i, g))
    return pl.pallas_call(
        body, name="gla_fwd" if gla else "ret_fwd", grid=(LIN_HEADS // HPS, nb), in_specs=specs + [ANY],
        out_specs=[orow, pl.BlockSpec((None, LB, HPS * dv), lambda g, i: (slot, i, g)),
                   pl.BlockSpec((HPS, cb, dv, dk), lambda g, i: (g, i, 0, 0))],
        out_shape=[_sds((t, d), F32), _sds(branches.shape, BF16), _sds((LIN_HEADS, t // CHUNK, dv, dk), F32)],
        scratch_shapes=[pltpu.VMEM((HPS, dv, dk), F32)], input_output_aliases={4 + naux: 1},
        compiler_params=_cp(("parallel", "arbitrary")))(p, p, p, p, *aux_arrays, branches)


def _lin_bwd(gla, p, aux_arrays, o, states, dbo, slot, dp, d, off):
    assert HPS == LIN_HEADS
    t = p.shape[0]
    dk, dv = d // (2 * LIN_HEADS), d // LIN_HEADS
    nb, cb = t // LB, LB // CHUNK
    naux = len(aux_arrays)

    def body(*refs):
        q_ref, k_ref, v_ref, g_ref = refs[:4]
        aux_refs = refs[4:4 + naux]
        o_ref, st_in_ref, dbo_ref = refs[4 + naux:7 + naux]
        outs = refs[8 + naux:]
        dq_ref, dk_ref = outs[0].at[:, pl.ds(0, d // 2)], outs[0].at[:, pl.ds(d // 2, d // 2)]
        dv_ref, dg_ref = outs[0].at[:, pl.ds(d, d)], outs[0].at[:, pl.ds(2 * d, d)]
        dst_ref = outs[-1]
        first = pl.program_id(1) == 0

        @pl.when(first)
        def _():
            dst_ref[...] = jnp.zeros_like(dst_ref)

        if gla:
            dpre_ref, dblr_ref, dgn_ref = outs[1:4]

            @pl.when(first)
            def _():
                dblr_ref[...] = jnp.zeros_like(dblr_ref)
                dgn_ref[...] = jnp.zeros_like(dgn_ref)

        rows = slice(None)
        for hh in range(HPS):
            kcols, vcols = pl.ds(hh * dk, dk), pl.ds(hh * dv, dv)
            aux, gn = _lin_aux(gla, aux_refs, rows, hh, dk)
            v = v_ref[:, vcols]
            bk = _lin_block(gla, q_ref[:, kcols], k_ref[:, kcols], v, aux)
            eb, enb, el, dec = bk["eb"], bk["enb"], bk["el"], bk["dec"]
            qf, kf, qb, kb, kl, s = bk["qf"], bk["kf"], bk["qb"], bk["kb"], bk["kl"], bk["s"]
            gate = g_ref[:, vcols]
            dout = dbo_ref[:, vcols]
            _, (sg, silu, r, hn) = _lin_norm_gate(gla, o_ref[:, vcols], gate, gn)
            dsilu = sg * (1.0 + gate * (1.0 - sg))
            if gla:
                y = hn * gn
                dy = dout * silu
                dg_ref[:, vcols] = (dout * y * dsilu).astype(BF16)
                dgn_ref[hh] += jnp.sum(dy * hn, axis=0, keepdims=True)
                dhn = dy * gn
                do = r * (dhn - hn * jnp.mean(dhn * hn, axis=-1, keepdims=True))
            else:
                dhn = dout * silu
                dg_ref[:, vcols] = (dout * hn * dsilu).astype(BF16)
                do = r * (dhn - jnp.mean(dhn, axis=-1, keepdims=True)
                          - hn * jnp.mean(dhn * hn, axis=-1, keepdims=True))
            ds = jnp.where(bk["same"], _dot_nt(do, v), 0.0)
            dsf = jnp.where(bk["low"], ds, 0.0)
            dsb = ds - dsf
            dvv = _dot_tn(s, do)
            dqf = _dot(dsf, kf)
            dkf = _dot_tn(dsf, qf)
            dqb = _dot(dsb, kb)
            dkb = _dot_tn(dsb, qb)
            dst = dst_ref[hh]
            dv_st, dqf_st, dkl_c, ddec_c = [], [], [], []
            parts = zip(reversed(range(cb)), reversed(_chunks(do)), reversed(_chunks(v)), reversed(_chunks(qf)),
                        reversed(_chunks(kl)), reversed(_chunks(dec)))
            for c, do_c, v_c, qf_c, kl_c, dec_c in parts:
                st = st_in_ref[hh, c]
                dv_st.append(_dot_nt(kl_c, dst))
                dkl_c.append(_dot(v_c, dst))
                dqf_st.append(_dot(do_c, st))
                ddec_c.append(jnp.broadcast_to(jnp.sum(dst * st, axis=0, keepdims=True), (CHUNK, dk)))
                dst = dst * dec_c[:1] + _dot_tn(do_c, qf_c)
            dst_ref[hh] = dst

            def cat(pieces):
                return jnp.concatenate(pieces[::-1], axis=0)

            dvv = dvv + cat(dv_st)
            dqf = dqf + cat(dqf_st)
            dkl = cat(dkl_c)
            dq = dqf * eb + dqb * enb
            dkk = dkf * enb + dkb * eb + dkl * el
            dv_ref[:, vcols] = dvv.astype(BF16)
            if gla:
                db = dqf * qf - dkf * kf - dqb * qb + dkb * kb - dkl * kl
                dlast = _per_chunk(dkl * kl, lambda c: jnp.sum(c, axis=0, keepdims=True)) + cat(ddec_c) * dec
                dlog_a = _dot_sel(jnp.where(bk["up"], bk["ones"], 0.0), db) + dlast
                dpre = dlog_a * (1.0 / GATE_NORM) * (1.0 - _sigmoid(bk["pre"]))
                dpre_ref[:, kcols] = dpre
                dblr_ref[hh] += jnp.sum(dpre, axis=0, keepdims=True)
                dq_ref[:, kcols] = (dq * (dk ** -0.5)).astype(BF16)
                dk_ref[:, kcols] = dkk.astype(BF16)
            else:
                cs, sn, _ = aux
                half = dk // 2
                dkk = dkk * (dk ** -0.5)
                dq_ref[:, kcols] = (dq * cs + pltpu.roll(dq * sn, half, 1)).astype(BF16)
                dk_ref[:, kcols] = (dkk * cs + pltpu.roll(dkk * sn, half, 1)).astype(BF16)

    specs, blk = _lin_specs(gla, dk, dv, off, True, nb)
    vrow = pl.BlockSpec((LB, HPS * dv), lambda g, i: (blk(i), g))
    krow = pl.BlockSpec((LB, HPS * dk), lambda g, i: (blk(i), g))
    specs += [vrow, pl.BlockSpec((HPS, cb, dv, dk), lambda g, i: (g, blk(i), 0, 0)),
              pl.BlockSpec((None, LB, HPS * dv), lambda g, i: (slot, blk(i), g)), ANY]
    section = off[("g" if gla else "r") + "q"] // (3 * d)
    out_specs = [pl.BlockSpec((LB, 3 * d), lambda g, i: (blk(i), section))]
    out_shape = [_sds(dp.shape, BF16)]
    if gla:
        out_specs += [krow, pl.BlockSpec((HPS, 1, dk), lambda g, i: (g, 0, 0)),
                      pl.BlockSpec((HPS, 1, dv), lambda g, i: (g, 0, 0))]
        out_shape += [_sds((t, d // 2), F32), _sds((LIN_HEADS, 1, dk), F32), _sds((LIN_HEADS, 1, dv), F32)]
    out_specs.append(pl.BlockSpec((HPS, dv, dk), lambda g, i: (g, 0, 0)))
    out_shape.append(_sds((LIN_HEADS, dv, dk), F32))
    res = pl.pallas_call(
        body, name="gla_bwd" if gla else "ret_bwd", grid=(LIN_HEADS // HPS, nb), in_specs=specs,
        out_specs=out_specs, out_shape=out_shape, input_output_aliases={7 + naux: 0},
        compiler_params=_cp(("parallel", "arbitrary")))(p, p, p, p, *aux_arrays, o, states, dbo, dp)
    return res[:-1]


def _row_tile(rows, cols):
    cap = max(8, (2 << 20) // (4 * cols))
    t = rows
    while t > cap and t % 2 == 0:
        t //= 2
    return t


def _add_half(name, g, t, sel):
    nchip, hr, cols = t.shape
    tr = _row_tile(hr, cols)
    nb = hr // tr

    def body(sel_ref, g_ref, t_ref, o_ref):
        o_ref[...] = g_ref[...] + t_ref[...]

    half = pl.BlockSpec((None, tr, cols), lambda p, i, s: (p, i, 0))
    gs = pltpu.PrefetchScalarGridSpec(
        num_scalar_prefetch=1, grid=(nchip, nb),
        in_specs=[pl.BlockSpec((None, tr, cols), lambda p, i, s: (p, s[0] * nb + i, 0)), half], out_specs=half)
    return pl.pallas_call(body, name=name, grid_spec=gs, out_shape=_sds(t.shape, F32),
                          compiler_params=_cp(("parallel", "parallel")))(sel, g, t)


def _sum_shards(name, h, rcv, sel):
    _, rows, cols = h.shape
    tr = _row_tile(rows, cols)

    def body(sel_ref, h_ref, r0, r1, r2, o_ref):
        o_ref[...] = ((h_ref[...] + r0[...]) + r1[...]) + r2[...]

    rspecs = [pl.BlockSpec((None, tr, cols), functools.partial(lambda i, s, j: (j, i, 0), j=j)) for j in range(3)]
    gs = pltpu.PrefetchScalarGridSpec(
        num_scalar_prefetch=1, grid=(rows // tr,),
        in_specs=[pl.BlockSpec((None, tr, cols), lambda i, s: (s[0], i, 0))] + rspecs,
        out_specs=pl.BlockSpec((tr, cols), lambda i, s: (i, 0)))
    return pl.pallas_call(body, name=name, grid_spec=gs, out_shape=_sds((rows, cols), F32),
                          compiler_params=_cp(("parallel",)))(sel, h, rcv, rcv, rcv)


def _adamw_math(w, g, m, v):
    c1 = 1.0 - ADAM_B1 ** ADAM_STEP
    c2 = 1.0 - ADAM_B2 ** ADAM_STEP
    nm = ADAM_B1 * m + (1.0 - ADAM_B1) * g
    nv = ADAM_B2 * v + (1.0 - ADAM_B2) * jnp.square(g)
    return -ADAM_LR * ((nm / c1) / (jnp.sqrt(nv / c2) + ADAM_EPS) + ADAM_WD * w), nm, nv


def _adamw(name, w, g, m, v):
    rows, cols = w.shape
    tr = _row_tile(rows, cols)

    def body(w_ref, g_ref, m_ref, v_ref, d_ref, nm_ref, nv_ref):
        d_ref[...], nm_ref[...], nv_ref[...] = _adamw_math(w_ref[...], g_ref[...], m_ref[...], v_ref[...])

    spec = pl.BlockSpec((tr, cols), lambda i: (i, 0))
    return pl.pallas_call(body, name=name, grid=(rows // tr,), in_specs=[spec] * 4, out_specs=[spec] * 3,
                          out_shape=[_sds((rows, cols), F32)] * 3, compiler_params=_cp(("parallel",)))(w, g, m, v)


def _adamw_layer(name, w, g_own, g_sib, sel, m, v, layer, prev):
    depth, rows, cols = w.shape
    tr = _row_tile(rows // 2, cols)
    nbh = rows // 2 // tr
    nprev = 0 if prev is None else 4

    def body(sel_ref, w_ref, own_ref, sib_ref, m_ref, v_ref, *rest):
        go_ref, d_ref, nm_ref, nv_ref = rest[nprev:]
        gv = jnp.where(pl.program_id(0) // nbh == sel_ref[0], own_ref[...], sib_ref[...])
        go_ref[...] = gv
        d_ref[...], nm_ref[...], nv_ref[...] = _adamw_math(w_ref[...], gv, m_ref[...], v_ref[...])

    lay = pl.BlockSpec((None, tr, cols), lambda i, s: (layer, i, 0))
    hlf = pl.BlockSpec((tr, cols), lambda i, s: (i % nbh, 0))
    gs = pltpu.PrefetchScalarGridSpec(
        num_scalar_prefetch=1, grid=(2 * nbh,), in_specs=[lay, hlf, hlf, lay, lay] + [ANY] * nprev,
        out_specs=[lay] * 4)
    args = (sel, w, g_own, g_sib, m, v) + (() if prev is None else tuple(prev))
    return pl.pallas_call(
        body, name=name, grid_spec=gs, out_shape=[_sds((depth, rows, cols), F32)] * 4,
        input_output_aliases={6 + k: k for k in range(nprev)},
        compiler_params=_cp(("parallel",)))(*args)


def _adamw_colmajor(name, wt, mt, vt, halves, sel):
    c_dim, depth, r_dim = wt.shape
    hr = r_dim // 2

    def body(sel_ref, w_ref, m_ref, v_ref, *rest):
        g_refs, (go_ref, d_ref, nm_ref, nv_ref) = rest[:2 * depth], rest[2 * depth:]
        own_first = sel_ref[0] == 0
        for l in range(depth):
            own, sib = g_refs[2 * l][...], g_refs[2 * l + 1][...]
            g = jnp.concatenate([jnp.where(own_first, own, sib), jnp.where(own_first, sib, own)], axis=0).T
            go_ref[:, l, :] = g
            d_ref[:, l, :], nm_ref[:, l, :], nv_ref[:, l, :] = _adamw_math(w_ref[:, l, :], g, m_ref[:, l, :],
                                                                          v_ref[:, l, :])

    col = pl.BlockSpec((LANE, depth, r_dim), lambda j, s: (j, 0, 0))
    gs = pltpu.PrefetchScalarGridSpec(
        num_scalar_prefetch=1, grid=(c_dim // LANE,),
        in_specs=[col] * 3 + [pl.BlockSpec((hr, LANE), lambda j, s: (0, j))] * (2 * depth), out_specs=[col] * 4)
    flat = [h for pair in halves for h in pair]
    return pl.pallas_call(body, name=name, grid_spec=gs, out_shape=[_sds(wt.shape, F32)] * 4,
                          compiler_params=_cp(("parallel",)))(sel, wt, mt, vt, *flat)


def _adamw_tail(name, wt, mt, vt, gt_tail, prev):
    c_dim, depth, r_dim = wt.shape
    nt = gt_tail.shape[0]

    def body(w_ref, m_ref, v_ref, g_ref, *rest):
        go_ref, d_ref, nm_ref, nv_ref = rest[4:]
        g = g_ref[...]
        go_ref[...] = g
        d_ref[...], nm_ref[...], nv_ref[...] = _adamw_math(w_ref[...], g, m_ref[...], v_ref[...])

    tail = pl.BlockSpec((nt, depth, r_dim), lambda i: (c_dim // nt - 1, 0, 0))
    return pl.pallas_call(
        body, name=name, grid=(1,), in_specs=[tail] * 3 + [pl.BlockSpec((nt, depth, r_dim), lambda i: (0, 0, 0))]
        + [ANY] * 4, out_specs=[tail] * 4, out_shape=[_sds(wt.shape, F32)] * 4,
        input_output_aliases={4 + k: k for k in range(4)},
        compiler_params=_cp(("arbitrary",)))(wt, mt, vt, gt_tail, *prev)


def _place():
    x, y, c = (lax.axis_index(a) for a in MESH_AXES)
    chips = [(1 - x, y), (x, 1 - y), (1 - x, 1 - y)]
    return x, y, c, chips


def _chip_index(xy):
    return 2 * xy[0] + xy[1]


ANY = pl.BlockSpec(memory_space=pl.ANY)


HBM_SPEC = pl.BlockSpec(memory_space=pltpu.HBM)
SEM = pl.BlockSpec(memory_space=pltpu.SEMAPHORE)
EFFECT = pltpu.SideEffectType.DATAFLOW_SIDE_EFFECTING


def _half(ref, c):
    hr = ref.shape[-2] // 2
    return pl.ds(pl.multiple_of(c * hr, 16), hr)


def _gather_copies(srcs, lands, send, recv):
    x, y, c, chips = _place()
    me = _chip_index((x, y))
    return [pltpu.make_async_remote_copy(src_ref=s.at[_half(s, c)], dst_ref=g.at[me, _half(s, c)],
                                         send_sem=send.at[3 * a + j], recv_sem=recv.at[3 * a + j],
                                         device_id=(*ch, c), device_id_type=DEV)
            for a, (s, g) in enumerate(zip(srcs, lands)) for j, ch in enumerate(chips)]


def _scatter_copies(srcs, lands, send, recv):
    x, y, c, chips = _place()
    return [pltpu.make_async_remote_copy(src_ref=h.at[_chip_index(ch)], dst_ref=r.at[j],
                                         send_sem=send.at[3 * a + j], recv_sem=recv.at[3 * a + j],
                                         device_id=(*ch, c), device_id_type=DEV)
            for a, (h, r) in enumerate(zip(srcs, lands)) for j, ch in enumerate(chips)]


def _in_hbm(a):
    return pltpu.with_memory_space_constraint(a, pltpu.HBM)


def _split_start(name, srcs, land_shapes, copies_fn, after=None, per_src=3):
    ns, nl = len(srcs), len(land_shapes)
    ncp = per_src * ns
    lands = [lax.empty(s.shape, s.dtype) for s in land_shapes]
    behind = [] if after is None else [after]

    def body(*refs):
        src, land = refs[:ns], refs[ns:ns + nl]
        send, recv = refs[ns + nl + len(behind)], refs[ns + nl + len(behind) + 1]
        for cp in copies_fn(src, land, send, recv):
            cp.start()
        refs[-1][...] = jnp.zeros_like(refs[-1])

    bufs = list(srcs) + lands
    outs = pl.pallas_call(
        body, name=name, in_specs=[HBM_SPEC] * (ns + nl) + [ANY] * len(behind),
        out_specs=[SEM, SEM] + [HBM_SPEC] * (ns + nl) + [pl.BlockSpec(memory_space=pltpu.VMEM)],
        out_shape=[pltpu.SemaphoreType.DMA((ncp,)), pltpu.SemaphoreType.DMA((ncp,))]
        + [pltpu.HBM(b.shape, b.dtype) for b in bufs] + [_sds((8, LANE), F32)],
        input_output_aliases={i: 2 + i for i in range(ns + nl)},
        compiler_params=pltpu.CompilerParams(has_side_effects=EFFECT))(*[_in_hbm(b) for b in bufs], *behind)
    return outs[0], outs[1], list(outs[2:2 + ns]), list(outs[2 + ns:2 + ns + nl]), outs[-1]


def _split_wait(name, started, copies_fn, after):
    send, recv, srcs, lands, _ = started
    ns, nl = len(srcs), len(lands)

    def body(*refs):
        src, land = refs[:ns], refs[ns:ns + nl]
        for cp in copies_fn(src, land, refs[ns + nl], refs[ns + nl + 1]):
            cp.wait_send()
            cp.wait_recv()

    bufs = list(srcs) + list(lands)
    outs = pl.pallas_call(
        body, name=name, in_specs=[HBM_SPEC] * (ns + nl) + [SEM, SEM, ANY], out_specs=[HBM_SPEC] * (ns + nl),
        out_shape=[pltpu.HBM(b.shape, b.dtype) for b in bufs],
        input_output_aliases={i: i for i in range(ns + nl)},
        compiler_params=pltpu.CompilerParams(has_side_effects=EFFECT))(*bufs, send, recv, after)
    return list(outs[:ns]), list(outs[ns:])


def _gather_plain(name, srcs):
    n = len(srcs)

    def body(*refs):
        src, land = refs[:n], refs[n:2 * n]
        send, recv, fsend, frecv = refs[2 * n:]
        first = _gather_copies(src, land, send, recv)
        for cp in first:
            cp.start()
        _forward_body(land, first, fsend, frecv)

    return pl.pallas_call(
        body, name=name, in_specs=[ANY] * n, out_specs=[ANY] * n,
        out_shape=[_sds((4,) + s.shape, s.dtype) for s in srcs],
        scratch_shapes=[pltpu.SemaphoreType.DMA((3 * n,))] * 4)(*srcs)


def _forward_body(land, arrivals, fsend, frecv):
    x, y, c, chips = _place()
    n = len(land)
    passed = []
    for a in range(n):
        for j, ch in enumerate(chips):
            if arrivals is not None:
                arrivals[3 * a + j].wait_recv()
            slot = land[a].at[_chip_index(ch), _half(land[a], c)]
            fw = pltpu.make_async_remote_copy(src_ref=slot, dst_ref=slot, send_sem=fsend.at[3 * a + j],
                                              recv_sem=frecv.at[3 * a + j], device_id=(x, y, 1 - c),
                                              device_id_type=DEV)
            fw.start()
            passed.append(fw)
    for a in range(n):
        for j, ch in enumerate(chips):
            slot = land[a].at[_chip_index(ch), _half(land[a], 1 - c)]
            pltpu.make_async_remote_copy(src_ref=slot, dst_ref=slot, send_sem=fsend.at[3 * a + j],
                                         recv_sem=frecv.at[3 * a + j], device_id=(x, y, c),
                                         device_id_type=DEV).wait_recv()
    for cp in passed:
        cp.wait_send()
    if arrivals is not None:
        for cp in arrivals:
            cp.wait_send()


def _gather_forward(name, lands):
    n = len(lands)

    def body(*refs):
        _forward_body(refs[n:2 * n], None, refs[2 * n], refs[2 * n + 1])

    return pl.pallas_call(
        body, name=name, in_specs=[ANY] * n, out_specs=[ANY] * n,
        out_shape=[_sds(g.shape, g.dtype) for g in lands], input_output_aliases={a: a for a in range(n)},
        scratch_shapes=[pltpu.SemaphoreType.DMA((3 * n,))] * 2)(*lands)


def _sibling_copies(srcs, lands, send, recv):
    x, y, c, _ = _place()
    return [pltpu.make_async_remote_copy(src_ref=g.at[:, _half(g, 1 - c)], dst_ref=t, send_sem=send.at[a],
                                         recv_sem=recv.at[a], device_id=(x, y, 1 - c), device_id_type=DEV)
            for a, (g, t) in enumerate(zip(srcs, lands))]


def _sibling_share(name, sms):
    n = len(sms)

    def body(*refs):
        ins, outs = refs[:n], refs[n:2 * n]
        send, recv = refs[2 * n:]
        x, y, c, _ = _place()
        cps = [pltpu.make_async_remote_copy(src_ref=ins[a], dst_ref=outs[a], send_sem=send.at[a],
                                            recv_sem=recv.at[a], device_id=(x, y, 1 - c), device_id_type=DEV)
               for a in range(n)]
        for cp in cps:
            cp.start()
        for cp in cps:
            cp.wait()

    return pl.pallas_call(
        body, name=name, in_specs=[ANY] * n, out_specs=[ANY] * n, out_shape=[_sds(s.shape, F32) for s in sms],
        scratch_shapes=[pltpu.SemaphoreType.DMA((n,))] * 2)(*sms)


def _small_allreduce(v, after=None):
    rows = v.shape[0]
    ndev = 8
    behind = [] if after is None else [after]

    def body(v_ref, *rest):
        o_ref, gat_ref, send, recv = rest[len(behind):]
        x, y, c, _ = _place()
        me = 4 * x + 2 * y + c
        cps = []
        for k in range(1, ndev):
            to = (me + k) % ndev
            cp = pltpu.make_async_remote_copy(src_ref=v_ref, dst_ref=gat_ref.at[me], send_sem=send.at[k - 1],
                                              recv_sem=recv.at[me], device_id=(to // 4, (to // 2) % 2, to % 2),
                                              device_id_type=DEV)
            cp.start()
            cps.append(cp)
        gat_ref[me] = v_ref[...]
        for k in range(1, ndev):
            frm = (me + k) % ndev
            pltpu.make_async_remote_copy(src_ref=v_ref, dst_ref=gat_ref.at[frm], send_sem=send.at[k - 1],
                                         recv_sem=recv.at[frm], device_id=(x, y, c), device_id_type=DEV).wait_recv()
        for cp in cps:
            cp.wait_send()
        acc = gat_ref[0]
        for k in range(1, ndev):
            acc = acc + gat_ref[k]
        o_ref[...] = acc

    vm = pl.BlockSpec(memory_space=pltpu.VMEM)
    return pl.pallas_call(
        body, name="small_allreduce", in_specs=[vm] + [ANY] * len(behind), out_specs=vm,
        out_shape=_sds((rows, LANE), F32),
        scratch_shapes=[pltpu.VMEM((ndev, rows, LANE), F32), pltpu.SemaphoreType.DMA((ndev - 1,)),
                        pltpu.SemaphoreType.DMA((ndev,))])(v, *behind)


def _layout(d):
    half = d // 2
    names = [("aq", d), ("ak", d), ("av", d), ("rq", half), ("rk", half), ("rv", d), ("rg", d),
             ("gq", half), ("gk", half), ("gv", d), ("gg", d), ("gates", 3 * d), ("glr", 2 * LANE)]
    off, pos = {}, 0
    for nm, sz in names:
        off[nm] = pos
        pos += sz
    return off, pos


def _chip_shards_of_cols(g, d):
    a = 8 * d + d
    per = (a + GATE_RANK + 3 * d) // 4
    sections = ((0, a, 0), (a, a + GATE_RANK, 3 * d), (a + GATE_RANK, 4 * per, -GATE_RANK))

    def quarter(lo, hi):
        cuts = [(max(lo, x), min(hi, y), s) for x, y, s in sections]
        return jnp.concatenate([g[:, x + s:y + s] for x, y, s in cuts if x < y], axis=1)

    return jnp.stack([quarter(c * per, (c + 1) * per) for c in range(4)])


def kernel(x, ln_in_g, ln_in_b, w_in, rel_bias, gla_w_lr, gla_b_lr, gla_norm_g, w_branch, w_out, ln1_g, ln1_b, w_up, w_down, ln2_g, ln2_b, loss_target, m_ln_in_g, m_ln_in_b, m_w_in, m_rel_bias, m_gla_w_lr, m_gla_b_lr, m_gla_norm_g, m_w_branch, m_w_out, m_ln1_g, m_ln1_b, m_w_up, m_w_down, m_ln2_g, m_ln2_b, v_ln_in_g, v_ln_in_b, v_w_in, v_rel_bias, v_gla_w_lr, v_gla_b_lr, v_gla_norm_g, v_w_branch, v_w_out, v_ln1_g, v_ln1_b, v_w_up, v_w_down, v_ln2_g, v_ln2_b):
    t, d = x.shape[1], x.shape[2]
    dff = 4 * d
    half = d // 2
    off, npad = _layout(d)
    xi, yi, ci = (lax.axis_index(a) for a in MESH_AXES)
    chip = 2 * xi + yi
    csel = jnp.reshape(ci, (1,)).astype(jnp.int32)
    psel = jnp.reshape(chip, (1,)).astype(jnp.int32)

    big_w = [w_in, w_branch.reshape(DEPTH, -1, d), w_out, w_up, w_down]
    big_m = [m_w_in, m_w_branch.reshape(DEPTH, -1, d), m_w_out, m_w_up, m_w_down]
    big_v = [v_w_in, v_w_branch.reshape(DEPTH, -1, d), v_w_out, v_w_up, v_w_down]
    W_IN, REST = [0], [1, 2, 3, 4]

    def shards_of(l, idx):
        return [big_w[i][l].astype(BF16) for i in idx]

    def lands_of(srcs):
        return [_sds((4,) + s.shape, s.dtype) for s in srcs]

    def full_w_in(g):
        per = g.shape[2]
        a = 8 * d + d

        def run(lo, hi):
            cuts = [(max(lo, c * per), min(hi, (c + 1) * per), c) for c in range(4)]
            return [g[c, :, x - c * per:y - c * per] for x, y, c in cuts if x < y]

        zeros = jnp.zeros((d, 2 * LANE - GATE_RANK), g.dtype)
        return jnp.concatenate(run(0, a) + run(a + GATE_RANK, 4 * per) + run(a, a + GATE_RANK) + [zeros], axis=1)

    def full_rest(gs):
        g_br, g_out, g_up, g_down = gs
        return (jnp.transpose(g_br.reshape(4, N_BRANCH, d // 4, d), (1, 0, 2, 3)).reshape(N_BRANCH, d, d),
                g_out.reshape(d, d), jnp.transpose(g_up, (1, 0, 2)).reshape(d, dff), g_down.reshape(dff, d))

    def with_own(srcs, lands):
        return [lax.dynamic_update_slice(g, s[None], (chip, 0, 0)) for s, g in zip(srcs, lands)]

    def gather_start(tag, l, idx, after):
        srcs = shards_of(l, idx)
        return srcs, _split_start(f"gather_{tag}{l}_start", srcs, lands_of(srcs), _gather_copies, after)

    def gather_finish(tag, l, pending, after):
        srcs, started = pending
        _, lands = _split_wait(f"gather_{tag}{l}_wait", started, _gather_copies, after)
        return with_own(srcs, _gather_forward(f"gather_{tag}{l}_pass", lands))

    def token(pending):
        return pending[1][4][0, 0]

    win, wbr, wout, wup, wdown = ([None] * DEPTH for _ in range(5))
    src_first = shards_of(0, W_IN)
    g_first = with_own(src_first, _gather_plain("gather_in0", src_first))
    win[0] = full_w_in(g_first[0])

    dkh = half // LIN_HEADS
    lr_rows = DEPTH * GATE_RANK
    lr_slab = jnp.zeros((lr_rows, 4, half // 4), F32)
    lr_slab = lax.dynamic_update_slice(lr_slab, (gla_w_lr.reshape(lr_rows, 1, half // 4) * jnp.where(ci == 0, 1.0, 0.0)),
                                       (0, chip, 0))
    wlr_full = _small_allreduce(lr_slab.reshape(-1, LANE)).reshape(DEPTH, GATE_RANK, half)
    wlr_pad = jnp.concatenate([wlr_full, jnp.zeros((DEPTH, 2 * LANE - GATE_RANK, half), F32)], axis=1)
    pend_rest = gather_start("rest", 0, REST, wlr_full[0, :1, :1] + g_first[0][0, :1, :1].astype(F32))

    inv = 10000.0 ** (-jnp.arange(0, dkh, 2, dtype=F32) / dkh)
    ang = jnp.arange(t, dtype=F32)[:, None] * inv[None, :]
    cos, sin = jnp.cos(ang), jnp.sin(ang)
    rope_c = jnp.concatenate([cos, cos], axis=1)
    rope_s = jnp.concatenate([-sin, sin], axis=1)
    log_gamma = jnp.log1p(-jnp.exp2(-5.0 - jnp.arange(LIN_HEADS, dtype=F32)))
    lg_tab = jnp.broadcast_to(log_gamma[:, None, None], (LIN_HEADS, 1, dkh))

    def vec(a):
        return a.reshape(1, -1)

    x0, x0b, xh_in, rs_in = _ln_in(x[0], vec(ln_in_g) + token(pend_rest), vec(ln_in_b))
    saved = []
    xl, xlb = x0, x0b
    for l in range(DEPTH):
        p = _mm("proj_in", xlb, win[l], 1024, 1792)
        g_rest = gather_finish("rest", l, pend_rest, p)
        wbr[l], wout[l], wup[l], wdown[l] = full_rest(g_rest)
        tok = 0.0
        if l + 1 < DEPTH:
            pend_in = gather_start("in", l + 1, W_IN, g_rest[0])
            tok = token(pend_in)
        bias = _bias_expand(rel_bias[l] + tok)
        bo = _attn_fwd(p, bias, d, off)
        ret_aux = (rope_c, rope_s, lg_tab + tok)
        gla_aux = (p, wlr_pad[l], vec(gla_b_lr[l]) + tok, vec(gla_norm_g[l]))
        o_ret, bo, st_ret = _lin_fwd(False, p, ret_aux, d, off, bo, 1)
        o_gla, bo, st_gla = _lin_fwd(True, p, gla_aux, d, off, bo, 2)
        tok = 0.0
        if l + 1 < DEPTH:
            g_in = gather_finish("in", l + 1, pend_in, bo)
            win[l + 1] = full_w_in(g_in[0])
            pend_rest = gather_start("rest", l + 1, REST, g_in[0])
            tok = token(pend_rest)
        proj, merged = _merge_fwd(bo, wbr[l], p, off["gates"])
        x1, x1b, xh1, rs1 = _mm_res_ln("out_proj_ln", merged, wout[l], xl, vec(ln1_g[l]) + tok, vec(ln1_b[l]),
                                       512, False)
        u = _mm("mlp_up", x1b, wup[l], 1024, 1024)
        x2, x2b, xh2, rs2, act = _mm_res_ln("mlp_down_ln", u, wdown[l], x1, vec(ln2_g[l]), vec(ln2_b[l]), 256, True)
        saved.append(dict(xlb=xlb, p=p, bias=bias, ret_aux=ret_aux, gla_aux=gla_aux, o_ret=o_ret, o_gla=o_gla,
                          st_ret=st_ret, st_gla=st_gla, bo=bo, proj=proj, merged=merged, x1b=x1b, xh1=xh1,
                          rs1=rs1, u=u, xh2=xh2, rs2=rs2, act=act))
        xl, xlb = x2, x2b

    small = {}
    last = saved[-1]
    loss_p, dz2, dz2b, dg, db = _loss_ln_bwd(xl, loss_target[0], last["xh2"], last["rs2"], vec(ln2_g[DEPTH - 1]))
    small["loss"] = loss_p[:, :1]
    grad_x = None

    def sibling_start(tag, l, idx, shards):
        lands = [_sds((g.shape[0], g.shape[1] // 2, g.shape[2]), F32) for g in shards]
        return tag, l, idx, _split_start(f"grad_{tag}{l}_sibling_start", shards, lands, _sibling_copies, per_src=1)

    def scatter_start(sibling, after):
        tag, l, idx, started = sibling
        shards, theirs = _split_wait(f"grad_{tag}{l}_sibling_wait", started, _sibling_copies, after)
        hs = [_add_half("grad_sibling_add", g, th, csel) for g, th in zip(shards, theirs)]
        lands = [_sds((3,) + h.shape[1:], F32) for h in hs]
        return tag, l, idx, _split_start(f"grad_{tag}{l}_scatter_start", hs, lands, _scatter_copies)

    adam_out = [None] * len(big_w)
    w_in_halves = [None] * DEPTH

    def scatter_finish(pending, after):
        tag, l, idx, started = pending
        hs, rcv = _split_wait(f"grad_{tag}{l}_scatter_wait", started, _scatter_copies, after)
        sms = [_sum_shards("grad_chip_sum", h, r, psel) for h, r in zip(hs, rcv)]
        last = None
        for i, own, sib in zip(idx, sms, _sibling_share(f"grad_{tag}{l}_share", sms)):
            if i == W_IN[0]:
                w_in_halves[l] = (own, sib)
                last = sib
            else:
                adam_out[i] = _adamw_layer("adamw_large", big_w[i], own, sib, csel, big_m[i], big_v[i], l,
                                           adam_out[i])
                last = adam_out[i][0]
        return last

    in_flight = []

    def scatter(sibling, after):
        pending = scatter_start(sibling, after)
        in_flight.append(pending)
        if len(in_flight) > 3:
            scatter_finish(in_flight.pop(0), pending[3][4])
        return pending[3][4][0, 0]

    def token_of(sibling):
        return sibling[3][4][0, 0]

    carry_tok = 0.0
    for l in reversed(range(DEPTH)):
        s = saved[l]
        small[("ln2_g", l)], small[("ln2_b", l)] = dg, db
        du = _mm_nt_relu2_bwd(dz2b, wdown[l], s["u"])
        g_wdown = _mm_tn("grad_w_down", s["act"], dz2b, 512, 512)
        g_wup = _mm_tn("grad_w_up", s["x1b"], du, 512, 512, shard="cols")
        dz1, dz1b, dg1, db1 = _mm_nt_res_lnbwd("mlp_up_bwd_ln", du, wup[l], dz2, s["xh1"], s["rs1"],
                                               vec(ln1_g[l]) + carry_tok, 512, dff)
        small[("ln1_g", l)], small[("ln1_b", l)] = dg1, db1
        dproj, dp = _merge_bwd(dz1b, wout[l], s["proj"], s["p"], off["gates"], npad)
        g_wout = _mm_tn("grad_w_out", s["merged"], dz1b, 512, 512)
        dbo = _mm("branch_proj_bwd", dproj, wbr[l], 1024, 1024, nt=True)
        g_wbr = _mm_tn("grad_w_branch", s["bo"], dproj, d // 4, 1024, shard="rows")
        sib = sibling_start("rest", l, REST, [g_wbr, g_wout.reshape(4, d // 4, d), g_wup, g_wdown.reshape(4, d, d)])
        rc, rs_, lg = s["ret_aux"]
        gp, gw, gb, gn_ = s["gla_aux"]
        dp, dk_acc, dv_acc, dbias = _attn_bwd(s["p"], s["bias"] + token_of(sib), dbo, dp, d, off)
        tok = scatter(sib, dbias)
        small[("rel_bias", l)] = _bias_reduce(dbias)
        dp = lax.dynamic_update_slice(dp, dk_acc[2 * QB:].astype(BF16), (0, off["ak"]))
        dp = lax.dynamic_update_slice(dp, dv_acc[2 * QB:].astype(BF16), (0, off["av"]))
        (dp,) = _lin_bwd(False, s["p"], (rc, rs_, lg + tok), s["o_ret"], s["st_ret"], dbo, 1, dp, d, off)
        dp, dpre, dblr, dgn = _lin_bwd(True, s["p"], (gp, gw, gb + tok, gn_), s["o_gla"], s["st_gla"], dbo, 2, dp,
                                       d, off)
        small[("gla_b_lr", l)] = dblr.reshape(1, half)
        small[("gla_norm_g", l)] = jnp.sum(dgn, axis=0)
        dpre_b = dpre.astype(BF16)
        glr_b = s["p"][:, off["glr"]:off["glr"] + LANE].astype(BF16)
        dp = _gate_lr_bwd(dpre_b, wlr_pad[l], dp, off["glr"])
        small[("gla_w_lr", l)] = _mm_tn("grad_gla_w_lr", glr_b, dpre_b, LANE, half)[:GATE_RANK]
        if l > 0:
            prev = saved[l - 1]
            xh_p, rs_p, g_p = prev["xh2"], prev["rs2"], vec(ln2_g[l - 1])
        else:
            xh_p, rs_p, g_p = xh_in, rs_in, vec(ln_in_g)
        g_win = _mm_tn("grad_w_in", s["xlb"], dp, 1024, 896)
        sib = sibling_start("in", l, W_IN, [_chip_shards_of_cols(g_win, d)])
        if l > 0:
            tok = token_of(sib)
        else:
            tok = scatter(sib, sib[3][4])
        dzp, dzpb, dg, db = _mm_nt_res_lnbwd("proj_in_bwd_ln", dp, win[l], dz1, xh_p, rs_p, g_p + tok, 1024, 1792)
        if l > 0:
            carry_tok = scatter(sib, dzp)
        dz2, dz2b = dzp, dzpb
        grad_x = dzp
    after = grad_x
    while in_flight:
        after = scatter_finish(in_flight.pop(0), after)
    wt, mt, vt = (jnp.transpose(a, (2, 0, 1)) for a in (big_w[0], big_m[0], big_v[0]))
    ntail = wt.shape[0] % LANE
    tails = [jnp.where(ci == 0, jnp.concatenate([own[:, -ntail:], sib[:, -ntail:]]),
                       jnp.concatenate([sib[:, -ntail:], own[:, -ntail:]])).T for own, sib in w_in_halves]
    adam_t = _adamw_tail("adamw_w_in_tail", wt, mt, vt, jnp.stack(tails, axis=1),
                         _adamw_colmajor("adamw_w_in", wt, mt, vt, w_in_halves, csel))
    adam_out[0] = [jnp.transpose(r, (1, 2, 0)) for r in adam_t]
    small["ln_in_g"], small["ln_in_b"] = dg, db
    rb_pad = 3 * LANE
    pieces = [small["loss"].reshape(-1), jnp.zeros((LANE - 1,), F32), small["ln_in_g"].reshape(-1),
              small["ln_in_b"].reshape(-1)]
    for l in range(DEPTH):
        rb = jnp.pad(small[("rel_bias", l)], ((0, 0), (0, rb_pad - (2 * REL_CLIP + 1))))
        pieces += [rb.reshape(-1), small[("gla_w_lr", l)].reshape(-1), small[("gla_b_lr", l)].reshape(-1),
                   small[("gla_norm_g", l)].reshape(-1), small[("ln1_g", l)].reshape(-1),
                   small[("ln1_b", l)].reshape(-1), small[("ln2_g", l)].reshape(-1), small[("ln2_b", l)].reshape(-1)]
    sizes = [pc.shape[0] for pc in pieces]
    packed = jnp.concatenate(pieces)
    padn = (-packed.shape[0]) % (8 * LANE)
    packed = jnp.concatenate([packed, jnp.zeros((padn,), F32)]).reshape(-1, LANE)
    red = _small_allreduce(packed, after).reshape(-1)

    parts, pos = [], 0
    for sz in sizes:
        parts.append(red[pos:pos + sz])
        pos += sz
    loss = parts[0][0]
    g_ln_in_g, g_ln_in_b = parts[2], parts[3]
    per = 8
    g_rel = jnp.stack([parts[4 + per * l].reshape(ATTN_HEADS, rb_pad)[:, :2 * REL_CLIP + 1] for l in range(DEPTH)])
    g_wlr_full = jnp.stack([parts[5 + per * l].reshape(GATE_RANK, half) for l in range(DEPTH)])
    g_wlr = lax.dynamic_slice_in_dim(g_wlr_full, chip * (half // 4), half // 4, axis=2)
    g_blr = jnp.stack([parts[6 + per * l] for l in range(DEPTH)])
    g_gn = jnp.stack([parts[7 + per * l] for l in range(DEPTH)])
    g_ln1g = jnp.stack([parts[8 + per * l] for l in range(DEPTH)])
    g_ln1b = jnp.stack([parts[9 + per * l] for l in range(DEPTH)])
    g_ln2g = jnp.stack([parts[10 + per * l] for l in range(DEPTH)])
    g_ln2b = jnp.stack([parts[11 + per * l] for l in range(DEPTH)])

    grads = [g_ln_in_g, g_ln_in_b, None, g_rel, g_wlr, g_blr, g_gn, None, None, g_ln1g, g_ln1b, None, None,
             g_ln2g, g_ln2b]
    ws = [ln_in_g, ln_in_b, w_in, rel_bias, gla_w_lr, gla_b_lr, gla_norm_g, w_branch, w_out, ln1_g, ln1_b,
          w_up, w_down, ln2_g, ln2_b]
    ms = [m_ln_in_g, m_ln_in_b, m_w_in, m_rel_bias, m_gla_w_lr, m_gla_b_lr, m_gla_norm_g, m_w_branch, m_w_out,
          m_ln1_g, m_ln1_b, m_w_up, m_w_down, m_ln2_g, m_ln2_b]
    vs = [v_ln_in_g, v_ln_in_b, v_w_in, v_rel_bias, v_gla_w_lr, v_gla_b_lr, v_gla_norm_g, v_w_branch, v_w_out,
          v_ln1_g, v_ln1_b, v_w_up, v_w_down, v_ln2_g, v_ln2_b]

    deltas, new_ms, new_vs = [None] * 15, [None] * 15, [None] * 15
    big_idx = [2, 7, 8, 11, 12]
    for i, res in zip(big_idx, adam_out):
        shp = ws[i].shape
        grads[i], deltas[i], new_ms[i], new_vs[i] = (r.reshape(shp) for r in res)
    small_idx = [i for i in range(15) if i not in big_idx]

    def pack(arrs):
        flat_ = jnp.concatenate([arrs[i].reshape(-1) for i in small_idx])
        pad_ = (-flat_.shape[0]) % (8 * LANE)
        return jnp.concatenate([flat_, jnp.ones((pad_,), F32)]).reshape(-1, LANE)

    dl, nm, nv = _adamw("adamw_small", pack(ws), pack(grads), pack(ms), pack(vs))
    pos = 0
    for i in small_idx:
        sz = int(np.prod(ws[i].shape))
        deltas[i] = dl.reshape(-1)[pos:pos + sz].reshape(ws[i].shape)
        new_ms[i] = nm.reshape(-1)[pos:pos + sz].reshape(ws[i].shape)
        new_vs[i] = nv.reshape(-1)[pos:pos + sz].reshape(ws[i].shape)
        pos += sz

    return (loss, grad_x[None], *grads, *deltas, *new_ms, *new_vs)
```
